```python
import math
import jax, jax.numpy as jnp
from jax import lax
import numpy as np

D_MODEL = 1024
BATCH = 8
SEQ = 2048
DEPTH = 2
DEC_BATCH = 128
DEC_SEQ = 1
PAST_LEN = 16384
PAGE_SIZE = 128


N_META = 16
EPS = 1e-6
GLA_HEADS = 4
GLA_DK = 64
GLA_DV = 128
GLA_RANK = 16
GLA_TAU = 16.0
GLA_CHUNK = 64
GLA_K = GLA_HEADS * GLA_DK
GLA_V = GLA_HEADS * GLA_DV
S5_GROUPS = 32
S5_H = 16
S5_P = 64
S5_W = S5_GROUPS * S5_H
IN0 = 2 * GLA_K + 2 * GLA_V + GLA_RANK + S5_W
SPLIT0 = [GLA_K, 2 * GLA_K, 2 * GLA_K + GLA_V, 2 * GLA_K + 2 * GLA_V, 2 * GLA_K + 2 * GLA_V + GLA_RANK]
RNN_W = 1536
RNN_BLOCKS = 16
RNN_BW = RNN_W // RNN_BLOCKS
RNN_C = 8.0
RNN_CONV = 4
D_FF = 2816
FFN_CONV = 3

kernel_name = 'hybrid_gla_s5_rglru_convffn_step'


def rmsnorm(x, g):
    xf = x.astype(jnp.float32)
    y = xf * lax.rsqrt(jnp.mean(xf * xf, axis=-1, keepdims=True) + EPS)
    return (y * g.astype(jnp.float32)).astype(x.dtype)


def causal_dwconv(x, buf, w, b):
    W = w.shape[0]
    T = x.shape[1]
    xx = jnp.concatenate([buf.astype(x.dtype), x], axis=1)
    y = b + xx[:, 0:T] * w[0]
    for j in range(1, W):
        y = y + xx[:, j:j + T] * w[j]
    return y, xx[:, -(W - 1):]


def _lin_combine(e1, e2):
    a1, b1 = e1
    a2, b2 = e2
    return a1 * a2, a2 * b1 + b2


def _cplx_combine(e1, e2):
    a1r, a1i, b1r, b1i = e1
    a2r, a2i, b2r, b2i = e2
    return (a2r * a1r - a2i * a1i, a2r * a1i + a2i * a1r,
            a2r * b1r - a2i * b1i + b2r, a2r * b1i + a2i * b1r + b2i)


def gla_chunked(q, k, v, log_a, s0):
    B_, T = q.shape[:2]
    c = min(GLA_CHUNK, T)
    pad = (-T) % c
    n = (T + pad) // c

    def prep(z):
        z = jnp.pad(z.astype(jnp.float32), ((0, 0), (pad, 0), (0, 0), (0, 0)))
        return z.reshape(B_, n, c, GLA_HEADS, z.shape[-1]).transpose(1, 0, 3, 2, 4)

    qc, kc, vc, ac = prep(q), prep(k), prep(v), prep(log_a)
    mask = jnp.tril(jnp.ones((c, c), bool))[:, :, None]

    def step(S, xs):
        qi, ki, vi, ai = xs
        b = jnp.cumsum(ai, axis=2)
        o = jnp.einsum('bhcd,bhde->bhce', qi * jnp.exp(b), S)
        rel = b[:, :, :, None, :] - b[:, :, None, :, :]
        decay = jnp.exp(jnp.where(mask, rel, -jnp.inf))
        att = jnp.einsum('bhid,bhjd,bhijd->bhij', qi, ki, decay)
        o = o + jnp.einsum('bhij,bhje->bhie', att, vi)
        b_last = b[:, :, -1:, :]
        S = jnp.exp(b_last[:, :, 0, :, None]) * S + jnp.einsum('bhcd,bhce->bhde', ki * jnp.exp(b_last - b), vi)
        return S, o

    S, o = lax.scan(step, s0.astype(jnp.float32), (qc, kc, vc, ac))
    o = o.transpose(1, 0, 3, 2, 4).reshape(B_, T + pad, GLA_HEADS, GLA_DV)[:, pad:]
    return o, S


def s5_mixer(u, x0_re, x0_im, lam_re, lam_im, log_dt, b_re, b_im, c_re, c_im, d, w_glu, b_glu):
    f32 = jnp.float32
    B_, T = u.shape[:2]
    uf = u.astype(f32).reshape(B_, T, S5_GROUPS, S5_H)
    lr, li = lam_re.astype(f32), lam_im.astype(f32)
    dt = jnp.exp(log_dt.astype(f32))[:, None]
    mag = jnp.exp(lr * dt)
    ab_re, ab_im = mag * jnp.cos(li * dt), mag * jnp.sin(li * dt)
    den = lr * lr + li * li
    nr, ni = ab_re - 1.0, ab_im
    f_re = (nr * lr + ni * li) / den
    f_im = (ni * lr - nr * li) / den
    br, bi = b_re.astype(f32), b_im.astype(f32)
    bb_re = f_re[..., None] * br - f_im[..., None] * bi
    bb_im = f_re[..., None] * bi + f_im[..., None] * br
    bu_re = jnp.einsum('gph,btgh->btgp', bb_re, uf)
    bu_im = jnp.einsum('gph,btgh->btgp', bb_im, uf)
    full = (B_, T, S5_GROUPS, S5_P)
    a_re = jnp.broadcast_to(ab_re, full)
    a_im = jnp.broadcast_to(ab_im, full)
    Ar, Ai, xr, xi = lax.associative_scan(_cplx_combine, (a_re, a_im, bu_re, bu_im), axis=1)
    x0r = x0_re.astype(f32)[:, None]
    x0i = x0_im.astype(f32)[:, None]
    xr, xi = xr + Ar * x0r - Ai * x0i, xi + Ar * x0i + Ai * x0r
    y = (jnp.einsum('ghp,btgp->btgh', c_re.astype(f32), xr)
         - jnp.einsum('ghp,btgp->btgh', c_im.astype(f32), xi)
         + d.astype(f32) * uf).reshape(B_, T, S5_W)
    y = jax.nn.gelu(y)
    y = y * jax.nn.sigmoid(y @ w_glu.astype(f32) + b_glu.astype(f32))
    return y, xr[:, -1], xi[:, -1]


def even_mixer(x, s_gla, s_re, s_im, norm_mix, w_in, w_alpha, b_alpha, gla_norm,
               lam_re, lam_im, log_dt, b_re, b_im, c_re, c_im, d, w_glu, b_glu, w_out):
    B_, T, _ = x.shape
    z = rmsnorm(x, norm_mix) @ w_in
    q, k, v, g, lr, u = jnp.split(z, SPLIT0, axis=-1)
    log_a = jax.nn.log_sigmoid((lr @ w_alpha + b_alpha).astype(jnp.float32)) / GLA_TAU
    hd = lambda t, dh: t.reshape(B_, T, GLA_HEADS, dh)
    o, s_gla = gla_chunked(hd(q, GLA_DK) * (GLA_DK ** -0.5), hd(k, GLA_DK), hd(v, GLA_DV),
                           hd(log_a, GLA_DK), s_gla)
    o = rmsnorm(o, gla_norm).reshape(B_, T, GLA_V) * jax.nn.silu(g.astype(jnp.float32))
    y5, s_re, s_im = s5_mixer(u, s_re, s_im, lam_re, lam_im, log_dt, b_re, b_im, c_re, c_im, d, w_glu, b_glu)
    mix = jnp.concatenate([o, y5], axis=-1).astype(x.dtype) @ w_out
    return x + mix, s_gla, s_re, s_im


def odd_mixer(x, h0, conv_buf, norm_mix, w_in, conv_w, conv_b, w_a, b_a, w_x, b_x, lam, w_out):
    f32 = jnp.float32
    B_, T, _ = x.shape
    z = rmsnorm(x, norm_mix) @ w_in
    gate, xr = jnp.split(z, 2, axis=-1)
    xc, new_buf = causal_dwconv(xr, conv_buf, conv_w, conv_b)
    xc = xc.astype(f32)
    xb = xc.reshape(B_, T, RNN_BLOCKS, RNN_BW)
    r = jax.nn.sigmoid(jnp.einsum('btnc,ncd->btnd', xb, w_a.astype(f32)) + b_a.astype(f32)).reshape(B_, T, RNN_W)
    i = jax.nn.sigmoid(jnp.einsum('btnc,ncd->btnd', xb, w_x.astype(f32)) + b_x.astype(f32)).reshape(B_, T, RNN_W)
    log_a = -RNN_C * r * jax.nn.softplus(-lam.astype(f32))
    a = jnp.exp(log_a)
    bx = jnp.sqrt(-jnp.expm1(2.0 * log_a)) * (i * xc)
    a_cum, h = lax.associative_scan(_lin_combine, (a, bx), axis=1)
    h = h + a_cum * h0.astype(f32)[:, None]
    y = (h * jax.nn.gelu(gate.astype(f32))).astype(x.dtype) @ w_out
    return x + y, h[:, -1], new_buf


def conv_ffn(x, buf, norm, w_up, conv_w, conv_b, w_down):
    hup = rmsnorm(x, norm) @ w_up
    gate, val = jnp.split(hup, 2, axis=-1)
    gate, new_buf = causal_dwconv(gate, buf, conv_w, conv_b)
    return x + (jax.nn.gelu(gate) * val) @ w_down, new_buf


def trunk(x, s_gla, s_re, s_im, h_rnn, buf_rnn, buf_ffn, mix0, mix1, ffn, norm_final):
    norm_ffn, w_up, f_conv_w, f_conv_b, w_down = ffn
    new_ffn = []
    for layer in range(DEPTH):
        if layer % 2 == 0:
            x, s_gla, s_re, s_im = even_mixer(x, s_gla, s_re, s_im, *mix0)
        else:
            x, h_rnn, buf_rnn = odd_mixer(x, h_rnn, buf_rnn, *mix1)
        x, nb = conv_ffn(x, buf_ffn[layer], norm_ffn[layer], w_up[layer], f_conv_w[layer],
                         f_conv_b[layer], w_down[layer])
        new_ffn.append(nb)
    return rmsnorm(x, norm_final), s_gla, s_re, s_im, h_rnn, buf_rnn, jnp.stack(new_ffn)


def setup_inputs(seed: int = 0) -> dict:
    key = jax.random.key(seed)
    ks = iter(jax.random.split(key, 64))
    f32 = jnp.float32
    nrm = lambda shape, scale: jax.random.normal(next(ks), shape, f32) * scale
    gain = lambda shape: 1.0 + nrm(shape, 0.02)
    lam_re = -0.5 + nrm((S5_GROUPS, S5_P), 0.01)
    lam_im = math.pi * jnp.arange(S5_P, dtype=f32)[None, :] + nrm((S5_GROUPS, S5_P), 0.01)
    log_dt = jax.random.uniform(next(ks), (S5_GROUPS,), f32, math.log(1e-3), math.log(1e-1))
    d_a = jax.random.uniform(next(ks), (RNN_W,), f32, 0.9, 0.999) ** (1.0 / RNN_C)
    rnn_lam = jnp.log(d_a) - jnp.log1p(-d_a)
    return {
        'x_prompt': nrm((BATCH, SEQ, D_MODEL), 1.0),
        'x_sample': nrm((DEC_BATCH, DEC_SEQ, D_MODEL), 1.0),
        'state_gla': nrm((DEC_BATCH, GLA_HEADS, GLA_DK, GLA_DV), 0.5),
        'state_s5_re': nrm((DEC_BATCH, S5_GROUPS, S5_P), 0.3),
        'state_s5_im': nrm((DEC_BATCH, S5_GROUPS, S5_P), 0.3),
        'state_rglru': nrm((DEC_BATCH, RNN_W), 0.5),
        'cache_rglru_conv': nrm((DEC_BATCH, RNN_CONV - 1, RNN_W), 1.0),
        'cache_ffn_conv': nrm((DEPTH, DEC_BATCH, FFN_CONV - 1, D_FF), 1.0),
        'meta_tokens': nrm((N_META, D_MODEL), 1.0),
        'norm_mix_0': gain((D_MODEL,)),
        'w_in_0': nrm((D_MODEL, IN0), D_MODEL ** -0.5),
        'w_alpha_0': nrm((GLA_RANK, GLA_K), GLA_RANK ** -0.5),
        'b_alpha_0': nrm((GLA_K,), 0.1),
        'gla_norm_0': gain((GLA_HEADS, GLA_DV)),
        's5_lam_re': lam_re,
        's5_lam_im': lam_im,
        's5_log_dt': log_dt,
        's5_b_re': nrm((S5_GROUPS, S5_P, S5_H), (0.5 / S5_H) ** 0.5),
        's5_b_im': nrm((S5_GROUPS, S5_P, S5_H), (0.5 / S5_H) ** 0.5),
        's5_c_re': nrm((S5_GROUPS, S5_H, S5_P), S5_P ** -0.5),
        's5_c_im': nrm((S5_GROUPS, S5_H, S5_P), S5_P ** -0.5),
        's5_d': nrm((S5_GROUPS, S5_H), 1.0),
        's5_w_glu': nrm((S5_W, S5_W), S5_W ** -0.5),
        's5_b_glu': nrm((S5_W,), 0.02),
        'w_out_0': nrm((GLA_V + S5_W, D_MODEL), (GLA_V + S5_W) ** -0.5),
        'norm_mix_1': gain((D_MODEL,)),
        'w_in_1': nrm((D_MODEL, 2 * RNN_W), D_MODEL ** -0.5),
        'rnn_conv_w': nrm((RNN_CONV, RNN_W), RNN_CONV ** -0.5),
        'rnn_conv_b': nrm((RNN_W,), 0.02),
        'rnn_w_a': nrm((RNN_BLOCKS, RNN_BW, RNN_BW), RNN_BW ** -0.5),
        'rnn_b_a': nrm((RNN_BLOCKS, RNN_BW), 0.02),
        'rnn_w_x': nrm((RNN_BLOCKS, RNN_BW, RNN_BW), RNN_BW ** -0.5),
        'rnn_b_x': nrm((RNN_BLOCKS, RNN_BW), 0.02),
        'rnn_lam': rnn_lam,
        'w_out_1': nrm((RNN_W, D_MODEL), RNN_W ** -0.5),
        'norm_ffn': gain((DEPTH, D_MODEL)),
        'ffn_w_up': nrm((DEPTH, D_MODEL, 2 * D_FF), D_MODEL ** -0.5),
        'ffn_conv_w': nrm((DEPTH, FFN_CONV, D_FF), FFN_CONV ** -0.5),
        'ffn_conv_b': nrm((DEPTH, D_FF), 0.02),
        'ffn_w_down': nrm((DEPTH, D_FF, D_MODEL), D_FF ** -0.5),
        'norm_final': gain((D_MODEL,)),
    }


def reference(x_prompt, x_sample, state_gla, state_s5_re, state_s5_im, state_rglru, cache_rglru_conv,
              cache_ffn_conv, meta_tokens, norm_mix_0, w_in_0, w_alpha_0, b_alpha_0, gla_norm_0,
              s5_lam_re, s5_lam_im, s5_log_dt, s5_b_re, s5_b_im, s5_c_re, s5_c_im, s5_d, s5_w_glu,
              s5_b_glu, w_out_0, norm_mix_1, w_in_1, rnn_conv_w, rnn_conv_b, rnn_w_a, rnn_b_a, rnn_w_x,
              rnn_b_x, rnn_lam, w_out_1, norm_ffn, ffn_w_up, ffn_conv_w, ffn_conv_b, ffn_w_down, norm_final):
    f32 = jnp.float32
    mix0 = (norm_mix_0, w_in_0, w_alpha_0, b_alpha_0, gla_norm_0, s5_lam_re, s5_lam_im, s5_log_dt,
            s5_b_re, s5_b_im, s5_c_re, s5_c_im, s5_d, s5_w_glu, s5_b_glu, w_out_0)
    mix1 = (norm_mix_1, w_in_1, rnn_conv_w, rnn_conv_b, rnn_w_a, rnn_b_a, rnn_w_x, rnn_b_x, rnn_lam, w_out_1)
    ffn = (norm_ffn, ffn_w_up, ffn_conv_w, ffn_conv_b, ffn_w_down)

    bp = x_prompt.shape[0]
    meta = jnp.broadcast_to(meta_tokens.astype(x_prompt.dtype)[None], (bp, N_META, D_MODEL))
    xp = jnp.concatenate([meta, x_prompt], axis=1)
    yp, gla_p, re_p, im_p, h_p, rc_p, fc_p = trunk(
        xp,
        jnp.zeros((bp, GLA_HEADS, GLA_DK, GLA_DV), f32),
        jnp.zeros((bp, S5_GROUPS, S5_P), f32),
        jnp.zeros((bp, S5_GROUPS, S5_P), f32),
        jnp.zeros((bp, RNN_W), f32),
        jnp.zeros((bp, RNN_CONV - 1, RNN_W), x_prompt.dtype),
        jnp.zeros((DEPTH, bp, FFN_CONV - 1, D_FF), x_prompt.dtype),
        mix0, mix1, ffn, norm_final)
    yp = yp[:, N_META:]

    ys, gla_s, re_s, im_s, h_s, rc_s, fc_s = trunk(
        x_sample, state_gla, state_s5_re, state_s5_im, state_rglru, cache_rglru_conv, cache_ffn_conv,
        mix0, mix1, ffn, norm_final)

    return (yp, ys, gla_p, gla_s, re_p, re_s, im_p, im_s, h_p, h_s, rc_p, rc_s, fc_p, fc_s)
```

```python
import functools
import math

import jax
import jax.numpy as jnp
from jax import lax
from jax.experimental import pallas as pl
from jax.experimental.pallas import tpu as pltpu

F32 = jnp.float32
BF16 = jnp.bfloat16

D_MODEL = 1024
N_META = 16
EPS = 1e-6
GLA_HEADS = 4
GLA_DK = 64
GLA_DV = 128
GLA_RANK = 16
GLA_TAU = 16.0
GLA_K = GLA_HEADS * GLA_DK
GLA_V = GLA_HEADS * GLA_DV
S5_GROUPS = 32
S5_H = 16
S5_P = 64
S5_W = S5_GROUPS * S5_H
S5_N = S5_GROUPS * S5_P
RNN_W = 1536
RNN_BLOCKS = 16
RNN_BW = RNN_W // RNN_BLOCKS
RNN_C = 8.0
RNN_CONV = 4
D_FF = 2816
FFN_CONV = 3

LANES = 128
SUBLANES = 8
FF_CHUNK = 256
SCAN_CHUNK = 512
S5_PACK = 8
GATE_WIN = 768
GATE_K = 512
GATE_N = 256
VMEM_LIMIT = 56 * 1024 * 1024


def _rms(x, g):
    return x * lax.rsqrt(jnp.mean(x * x, axis=-1, keepdims=True) + EPS) * g


def _mm(a, w):
    return jnp.dot(a.astype(BF16), w, preferred_element_type=F32)


def _const_spec(shape):
    nd = len(shape)
    return pl.BlockSpec(shape, lambda i, _n=nd: (0,) * _n, pipeline_mode=pl.Buffered(1))


def _row_spec(rows, cols):
    return pl.BlockSpec((rows, cols), lambda i: (i, 0))


def _params(sem="arbitrary"):
    return pltpu.CompilerParams(dimension_semantics=(sem,), vmem_limit_bytes=VMEM_LIMIT)


def _s5_prep_kernel(lr_ref, li_ref, ldt_ref, brt_ref, bit_ref, are_ref, aim_ref, bbre_ref, bbim_ref):
    lr = lr_ref[...]
    li = li_ref[...]
    dt = jnp.exp(ldt_ref[...])
    mag = jnp.exp(lr * dt)
    ab_re = mag * jnp.cos(li * dt)
    ab_im = mag * jnp.sin(li * dt)
    den = lr * lr + li * li
    nr = ab_re - 1.0
    ni = ab_im
    f_re = (nr * lr + ni * li) / den
    f_im = (ni * lr - nr * li) / den
    are_ref[...] = ab_re
    aim_ref[...] = ab_im
    brt = brt_ref[...]
    bit = bit_ref[...]
    bbre_ref[...] = f_re[:, None, :] * brt - f_im[:, None, :] * bit
    bbim_ref[...] = f_re[:, None, :] * bit + f_im[:, None, :] * brt


def _s5_prep(lam_re, lam_im, log_dt, b_re, b_im):
    g, p, h = b_re.shape
    brt = jnp.transpose(b_re, (0, 2, 1))
    bit = jnp.transpose(b_im, (0, 2, 1))
    return pl.pallas_call(
        _s5_prep_kernel,
        out_shape=(jax.ShapeDtypeStruct((g, p), F32), jax.ShapeDtypeStruct((g, p), F32),
                   jax.ShapeDtypeStruct((g, h, p), F32), jax.ShapeDtypeStruct((g, h, p), F32)),
        name="s5_prep",
    )(lam_re, lam_im, log_dt.reshape(g, 1), brt, bit)


def _l0_in_kernel(x_ref, g_ref, wq_ref, wk_ref, wv_ref, wg_ref, wu_ref, wlr_ref, wal_ref, bal_ref,
                  q_ref, k_ref, v_ref, gs_ref, la_ref, u_ref):
    xn = _rms(x_ref[...], g_ref[...]).astype(BF16)
    q_ref[...] = _mm(xn, wq_ref[...]) * (GLA_DK ** -0.5)
    k_ref[...] = _mm(xn, wk_ref[...])
    v_ref[...] = _mm(xn, wv_ref[...])
    g = _mm(xn, wg_ref[...])
    gs_ref[...] = g * jax.nn.sigmoid(g)
    u_ref[...] = _mm(xn, wu_ref[...])
    lr = _mm(xn, wlr_ref[...])
    pre = _mm(lr, wal_ref[...]) + bal_ref[...]
    la_ref[...] = jax.nn.log_sigmoid(pre) * (1.0 / GLA_TAU)


def _l0_in(x, w, rows):
    n = x.shape[0]
    outs = [GLA_K, GLA_K, GLA_V, GLA_V, GLA_K, S5_W]
    return pl.pallas_call(
        _l0_in_kernel,
        grid=(n // rows,),
        in_specs=[_row_spec(rows, D_MODEL), _const_spec((1, D_MODEL)),
                  _const_spec((D_MODEL, GLA_K)), _const_spec((D_MODEL, GLA_K)),
                  _const_spec((D_MODEL, GLA_V)), _const_spec((D_MODEL, GLA_V)),
                  _const_spec((D_MODEL, S5_W)), _const_spec((D_MODEL, LANES)),
                  _const_spec((LANES, GLA_K)), _const_spec((1, GLA_K))],
        out_specs=[_row_spec(rows, c) for c in outs],
        out_shape=[jax.ShapeDtypeStruct((n, c), F32) for c in outs],
        compiler_params=_params("parallel"),
        name="l0_in",
    )(x, w["norm_mix_0"], w["w_q"], w["w_k"], w["w_v"], w["w_g"], w["w_u"], w["w_lr"], w["w_alpha"],
      w["b_alpha"])


def _gla_chunk_kernel(q_ref, k_ref, v0_ref, v1_ref, v2_ref, v3_ref, la_ref, gs_ref, gn_ref, s0_ref,
                      o_ref, s_ref, b_sc, gam_sc, qt_sc, kt_sc, kh_sc, oo_sc, *, nb, c):
    v_refs = (v0_ref, v1_ref, v2_ref, v3_ref)

    @pl.when(pl.program_id(0) == 0)
    def _():
        s_ref[...] = s0_ref[...]

    def cum_body(t, run):
        rows = pl.ds(pl.multiple_of(t * nb, nb), nb)
        run = run + la_ref[rows, :]
        b_sc[rows, :] = run
        return run

    bl = lax.fori_loop(0, c, cum_body, jnp.zeros((nb, GLA_K), F32))
    gam = jnp.exp(bl)
    for b in range(nb):
        gam_sc[b] = jnp.broadcast_to(gam[b:b + 1, :], gam_sc.shape[1:])

    def scale_body(t, carry):
        rows = pl.ds(pl.multiple_of(t * nb, nb), nb)
        b = b_sc[rows, :]
        k = k_ref[rows, :]
        qt = q_ref[rows, :] * jnp.exp(b)
        kt = k * jnp.exp(-b)
        kh = k * jnp.exp(bl - b)
        for j in range(GLA_K // LANES):
            ls = slice(j * LANES, (j + 1) * LANES)
            qt_sc[j, rows, :] = qt[:, ls]
            kt_sc[j, rows, :] = kt[:, ls]
            kh_sc[j, rows, :] = kh[:, ls]
        return carry

    lax.fori_loop(0, c, scale_body, 0)

    causal = lax.broadcasted_iota(jnp.int32, (c, c), 0) >= lax.broadcasted_iota(jnp.int32, (c, c), 1)
    eye = lax.broadcasted_iota(jnp.int32, (GLA_DK, GLA_DK), 0) == lax.broadcasted_iota(
        jnp.int32, (GLA_DK, GLA_DK), 1)
    heads_per_buf = LANES // GLA_DK

    def seq_body(bi, carry):
        rows = pl.ds(bi, c, stride=nb)
        for h in range(GLA_HEADS):
            j = h // heads_per_buf
            ks = slice((h % heads_per_buf) * GLA_DK, (h % heads_per_buf + 1) * GLA_DK)
            qb = qt_sc[j, rows, :][:, ks].astype(BF16)
            kb = kt_sc[j, rows, :][:, ks].astype(BF16)
            khb = kh_sc[j, rows, :][:, ks].astype(BF16)
            vb = v_refs[h][rows, :].astype(BF16)
            att = lax.dot_general(qb, kb, (((1,), (1,)), ((), ())), preferred_element_type=F32)
            att = jnp.where(causal, att, 0.0)
            s = s_ref[bi * GLA_HEADS + h]
            o = (jnp.dot(att.astype(BF16), vb, preferred_element_type=F32)
                 + jnp.dot(qb, s.astype(BF16), preferred_element_type=F32))
            oo_sc[h, rows, :] = o
            gam_row = gam_sc[bi][0:1, h * GLA_DK:(h + 1) * GLA_DK]
            gam_col = jnp.sum(jnp.where(eye, gam_row, 0.0), axis=1, keepdims=True)
            s_ref[bi * GLA_HEADS + h] = gam_col * s + lax.dot_general(
                khb, vb, (((0,), (0,)), ((), ())), preferred_element_type=F32)
        return carry

    lax.fori_loop(0, nb, seq_body, 0)

    for h in range(GLA_HEADS):
        vs = slice(h * GLA_DV, (h + 1) * GLA_DV)
        o_ref[:, vs] = _rms(oo_sc[h], gn_ref[:, vs]) * gs_ref[:, vs]


def _gla_chunk(q, k, v, la, gs, gn, s0, nb, c):
    n = q.shape[0]
    rows = nb * c
    sshape = (nb * GLA_HEADS, GLA_DK, GLA_DV)
    kern = functools.partial(_gla_chunk_kernel, nb=nb, c=c)
    v_specs = [pl.BlockSpec((rows, GLA_DV), lambda i, _h=h: (i, _h)) for h in range(GLA_HEADS)]
    pair = pltpu.VMEM((GLA_K // LANES, rows, LANES), F32)
    return pl.pallas_call(
        kern,
        grid=(n // rows,),
        in_specs=[_row_spec(rows, GLA_K), _row_spec(rows, GLA_K)] + v_specs
                 + [_row_spec(rows, GLA_K), _row_spec(rows, GLA_V), _const_spec((1, GLA_V)),
                    _const_spec(sshape)],
        out_specs=[_row_spec(rows, GLA_V), pl.BlockSpec(sshape, lambda i: (0, 0, 0))],
        out_shape=[jax.ShapeDtypeStruct((n, GLA_V), F32), jax.ShapeDtypeStruct(sshape, F32)],
        scratch_shapes=[pltpu.VMEM((rows, GLA_K), F32), pltpu.VMEM((nb, SUBLANES, GLA_K), F32), pair, pair, pair,
                        pltpu.VMEM((GLA_HEADS, rows, GLA_DV), F32)],
        compiler_params=_params(),
        name="gla_chunk",
    )(q, k, v, v, v, v, la, gs, gn, s0.reshape(sshape))


def _gla_step_kernel(q_ref, k_ref, la_ref, v_ref, gs_ref, gn_ref, s0_ref, o_ref, s_ref):
    q = q_ref[...]
    k = k_ref[...]
    a = jnp.exp(la_ref[...])
    v = v_ref[...]
    o = jnp.zeros_like(v)
    for d in range(GLA_DK):
        s_new = a[:, d:d + 1] * s0_ref[:, d, :] + k[:, d:d + 1] * v
        s_ref[:, d, :] = s_new
        o = o + q[:, d:d + 1] * s_new
    o_ref[...] = _rms(o, gn_ref[...]) * gs_ref[...]


def _gla_step(q, k, v, la, gs, gn, s0):
    nb = q.shape[0]
    heads = lambda z: jnp.transpose(z.reshape(nb, GLA_HEADS, GLA_DK), (1, 0, 2))
    hspec = pl.BlockSpec((None, nb, GLA_DK), lambda h: (h, 0, 0))
    vspec = pl.BlockSpec((nb, GLA_DV), lambda h: (0, h))
    sspec = pl.BlockSpec((nb, None, GLA_DK, GLA_DV), lambda h: (0, h, 0, 0))
    return pl.pallas_call(
        _gla_step_kernel,
        grid=(GLA_HEADS,),
        in_specs=[hspec, hspec, hspec, vspec, vspec, pl.BlockSpec((1, GLA_DV), lambda h: (0, h)), sspec],
        out_specs=[vspec, sspec],
        out_shape=[jax.ShapeDtypeStruct((nb, GLA_V), F32),
                   jax.ShapeDtypeStruct((nb, GLA_HEADS, GLA_DK, GLA_DV), F32)],
        compiler_params=_params("parallel"),
        name="gla_step",
    )(heads(q), heads(k), heads(la), v, gs, gn, s0)


def _l0_out_kernel(u_ref, og_ref, x_ref, bbre_ref, bbim_ref, are_ref, aim_ref, cre_ref, cim_ref, d_ref,
                   wglu_ref, bglu_ref, woa_ref, wob_ref, xr0_ref, xi0_ref,
                   xo_ref, xre_ref, xim_ref, sr_sc, si_sc, *, nb, tt):
    @pl.when(pl.program_id(0) == 0)
    def _():
        xre_ref[...] = xr0_ref[...]
        xim_ref[...] = xi0_ref[...]

    u = u_ref[...]
    ub = u.astype(BF16)
    kin = S5_PACK * S5_H
    kst = S5_PACK * S5_P
    npack = S5_GROUPS // S5_PACK
    for j in range(npack):
        uj = ub[:, j * kin:(j + 1) * kin]
        sr_sc[:, j * kst:(j + 1) * kst] = jnp.dot(uj, bbre_ref[j], preferred_element_type=F32)
        si_sc[:, j * kst:(j + 1) * kst] = jnp.dot(uj, bbim_ref[j], preferred_element_type=F32)

    for ch in range(S5_N // SCAN_CHUNK):
        cs = pl.ds(ch * SCAN_CHUNK, SCAN_CHUNK)
        ar = are_ref[:, cs]
        ai = aim_ref[:, cs]

        def body(t, carry, cs=cs, ar=ar, ai=ai):
            xr, xi = carry
            rows = pl.ds(pl.multiple_of(t * nb, nb), nb)
            nxr = ar * xr - ai * xi + sr_sc[rows, cs]
            nxi = ar * xi + ai * xr + si_sc[rows, cs]
            sr_sc[rows, cs] = nxr
            si_sc[rows, cs] = nxi
            return nxr, nxi

        xr, xi = lax.fori_loop(0, tt, body, (xre_ref[:, cs], xim_ref[:, cs]))
        xre_ref[:, cs] = xr
        xim_ref[:, cs] = xi

    ys = []
    for j in range(npack):
        xrj = sr_sc[:, j * kst:(j + 1) * kst].astype(BF16)
        xij = si_sc[:, j * kst:(j + 1) * kst].astype(BF16)
        ys.append(jnp.dot(xrj, cre_ref[j], preferred_element_type=F32)
                  - jnp.dot(xij, cim_ref[j], preferred_element_type=F32))
    y = jnp.concatenate(ys, axis=-1) + d_ref[...] * u
    y = jax.nn.gelu(y)
    y = y * jax.nn.sigmoid(_mm(y, wglu_ref[...]) + bglu_ref[...])
    xo_ref[...] = x_ref[...] + _mm(og_ref[...], woa_ref[...]) + _mm(y, wob_ref[...])


def _l0_out(u, og, x, w, xr0, xi0, nb, tt):
    n = x.shape[0]
    rows = nb * tt
    npack = S5_GROUPS // S5_PACK
    kin, kst = S5_PACK * S5_H, S5_PACK * S5_P
    kern = functools.partial(_l0_out_kernel, nb=nb, tt=tt)
    st_spec = pl.BlockSpec((nb, S5_N), lambda i: (0, 0))
    return pl.pallas_call(
        kern,
        grid=(n // rows,),
        in_specs=[_row_spec(rows, S5_W), _row_spec(rows, GLA_V), _row_spec(rows, D_MODEL),
                  _const_spec((npack, kin, kst)), _const_spec((npack, kin, kst)),
                  _const_spec((1, S5_N)), _const_spec((1, S5_N)),
                  _const_spec((npack, kst, kin)), _const_spec((npack, kst, kin)),
                  _const_spec((1, S5_W)), _const_spec((S5_W, S5_W)), _const_spec((1, S5_W)),
                  _const_spec((GLA_V, D_MODEL)), _const_spec((S5_W, D_MODEL)),
                  _const_spec((nb, S5_N)), _const_spec((nb, S5_N))],
        out_specs=[_row_spec(rows, D_MODEL), st_spec, st_spec],
        out_shape=[jax.ShapeDtypeStruct((n, D_MODEL), F32), jax.ShapeDtypeStruct((nb, S5_N), F32),
                   jax.ShapeDtypeStruct((nb, S5_N), F32)],
        scratch_shapes=[pltpu.VMEM((rows, S5_N), F32)] * 2,
        compiler_params=_params(),
        name="l0_out",
    )(u, og, x, w["s5_bbre"], w["s5_bbim"], w["s5_are"], w["s5_aim"], w["s5_cre"], w["s5_cim"],
      w["s5_d"], w["s5_w_glu"], w["s5_b_glu"], w["w_out_0a"], w["w_out_0b"], xr0, xi0)


def _ffn_kernel(x_ref, g_ref, wg_ref, wv_ref, cw_ref, cb_ref, wd_ref, c0_ref, gf_ref, xo_ref, c_ref,
                *, nb, tt, final):
    @pl.when(pl.program_id(0) == 0)
    def _():
        c_ref[...] = c0_ref[...]

    rows = nb * tt
    x = x_ref[...]
    xn = _rms(x, g_ref[...]).astype(BF16)
    acc = jnp.zeros((rows, D_MODEL), F32)
    for ci in range(D_FF // FF_CHUNK):
        cs = slice(ci * FF_CHUNK, (ci + 1) * FF_CHUNK)
        gate = jnp.dot(xn, wg_ref[:, cs], preferred_element_type=F32)
        val = jnp.dot(xn, wv_ref[:, cs], preferred_element_type=F32)
        ext = jnp.concatenate([c_ref[:, cs], gate], axis=0)
        y = cb_ref[:, cs] + ext[0:rows] * cw_ref[0:1, cs]
        y = y + ext[nb:nb + rows] * cw_ref[1:2, cs]
        y = y + ext[2 * nb:2 * nb + rows] * cw_ref[2:3, cs]
        c_ref[:, cs] = ext[tt * nb:(tt + 2) * nb]
        hmid = jax.nn.gelu(y) * val
        acc = acc + jnp.dot(hmid.astype(BF16), wd_ref[cs, :], preferred_element_type=F32)
    out = x + acc
    if final:
        out = _rms(out, gf_ref[...])
    xo_ref[...] = out


def _ffn(x, w, layer, c0, nb, tt, final):
    n = x.shape[0]
    rows = nb * tt
    kern = functools.partial(_ffn_kernel, nb=nb, tt=tt, final=final)
    cshape = ((FFN_CONV - 1) * nb, D_FF)
    return pl.pallas_call(
        kern,
        grid=(n // rows,),
        in_specs=[_row_spec(rows, D_MODEL), _const_spec((1, D_MODEL)),
                  _const_spec((D_MODEL, D_FF)), _const_spec((D_MODEL, D_FF)),
                  _const_spec((FFN_CONV, D_FF)), _const_spec((1, D_FF)),
                  _const_spec((D_FF, D_MODEL)), _const_spec(cshape), _const_spec((1, D_MODEL))],
        out_specs=[_row_spec(rows, D_MODEL), pl.BlockSpec(cshape, lambda i: (0, 0))],
        out_shape=[jax.ShapeDtypeStruct((n, D_MODEL), F32), jax.ShapeDtypeStruct(cshape, F32)],
        compiler_params=_params(),
        name="ffn%d" % layer,
    )(x, w["norm_ffn"][layer], w["ffn_w_gate"][layer], w["ffn_w_val"][layer], w["ffn_conv_w"][layer],
      w["ffn_conv_b"][layer], w["ffn_w_down"][layer], c0, w["norm_final"])


def _l1_kernel(x_ref, g_ref, wgt_ref, wxr_ref, cw_ref, cb_ref, wa_ref, ba_ref, wx_ref, bx_ref, lam_ref,
               wo_ref, h0_ref, c0_ref, xo_ref, h_ref, c_ref, a_sc, b_sc, *, nb, tt):
    @pl.when(pl.program_id(0) == 0)
    def _():
        h_ref[...] = h0_ref[...]
        c_ref[...] = c0_ref[...]

    rows = nb * tt
    x = x_ref[...]
    xn = _rms(x, g_ref[...]).astype(BF16)
    xr = jnp.dot(xn, wxr_ref[...], preferred_element_type=F32)
    ext = jnp.concatenate([c_ref[...], xr], axis=0)
    xc = cb_ref[...] + ext[0:rows] * cw_ref[0:1, :]
    for j in range(1, RNN_CONV):
        xc = xc + ext[j * nb:j * nb + rows] * cw_ref[j:j + 1, :]
    c_ref[...] = ext[tt * nb:(tt + RNN_CONV - 1) * nb]

    xcb = xc.astype(BF16)
    rs, gs = [], []
    for wi in range(RNN_W // GATE_WIN):
        for ni in range(GATE_WIN // GATE_N):
            k0 = wi * GATE_WIN + ni * LANES
            lhs = xcb[:, k0:k0 + GATE_K]
            rs.append(jnp.dot(lhs, wa_ref[wi, ni], preferred_element_type=F32))
            gs.append(jnp.dot(lhs, wx_ref[wi, ni], preferred_element_type=F32))
    r = jax.nn.sigmoid(jnp.concatenate(rs, axis=-1) + ba_ref[...])
    ig = jax.nn.sigmoid(jnp.concatenate(gs, axis=-1) + bx_ref[...])
    log_a = (-RNN_C) * r * jax.nn.softplus(-lam_ref[...])
    a = jnp.exp(log_a)
    a_sc[...] = a
    b_sc[...] = jnp.sqrt(jnp.tanh(-log_a) * (a * a + 1.0)) * (ig * xc)

    for ch in range(RNN_W // SCAN_CHUNK):
        cs = pl.ds(ch * SCAN_CHUNK, SCAN_CHUNK)

        def body(t, h, cs=cs):
            rws = pl.ds(pl.multiple_of(t * nb, nb), nb)
            h = a_sc[rws, cs] * h + b_sc[rws, cs]
            b_sc[rws, cs] = h
            return h

        h_ref[:, cs] = lax.fori_loop(0, tt, body, h_ref[:, cs])

    gate = jnp.dot(xn, wgt_ref[...], preferred_element_type=F32)
    xo_ref[...] = x + _mm(b_sc[...] * jax.nn.gelu(gate), wo_ref[...])


def _l1(x, w, h0, c0, nb, tt):
    n = x.shape[0]
    rows = nb * tt
    kern = functools.partial(_l1_kernel, nb=nb, tt=tt)
    cshape = ((RNN_CONV - 1) * nb, RNN_W)
    gshape = (RNN_W // GATE_WIN, GATE_WIN // GATE_N, GATE_K, GATE_N)
    return pl.pallas_call(
        kern,
        grid=(n // rows,),
        in_specs=[_row_spec(rows, D_MODEL), _const_spec((1, D_MODEL)),
                  _const_spec((D_MODEL, RNN_W)), _const_spec((D_MODEL, RNN_W)),
                  _const_spec((RNN_CONV, RNN_W)), _const_spec((1, RNN_W)),
                  _const_spec(gshape), _const_spec((1, RNN_W)),
                  _const_spec(gshape), _const_spec((1, RNN_W)),
                  _const_spec((1, RNN_W)), _const_spec((RNN_W, D_MODEL)),
                  _const_spec((nb, RNN_W)), _const_spec(cshape)],
        out_specs=[_row_spec(rows, D_MODEL), pl.BlockSpec((nb, RNN_W), lambda i: (0, 0)),
                   pl.BlockSpec(cshape, lambda i: (0, 0))],
        out_shape=[jax.ShapeDtypeStruct((n, D_MODEL), F32), jax.ShapeDtypeStruct((nb, RNN_W), F32),
                   jax.ShapeDtypeStruct(cshape, F32)],
        scratch_shapes=[pltpu.VMEM((rows, RNN_W), F32)] * 2,
        compiler_params=_params(),
        name="l1_mixer",
    )(x, w["norm_mix_1"], w["w_gate_1"], w["w_xr_1"], w["rnn_conv_w"], w["rnn_conv_b"],
      w["rnn_wa"], w["rnn_b_a"], w["rnn_wx"], w["rnn_b_x"], w["rnn_lam"], w["w_out_1"], h0, c0)


def _pack_gate(wblk):
    dense = jax.scipy.linalg.block_diag(*[wblk[i] for i in range(RNN_BLOCKS)])
    tiles = []
    for wi in range(RNN_W // GATE_WIN):
        row = []
        for ni in range(GATE_WIN // GATE_N):
            k0 = wi * GATE_WIN + ni * LANES
            n0 = wi * GATE_WIN + ni * GATE_N
            row.append(dense[k0:k0 + GATE_K, n0:n0 + GATE_N])
        tiles.append(jnp.stack(row))
    return jnp.stack(tiles).astype(BF16)


def _prep_weights(p):
    w = {}
    row = lambda v: v.reshape(1, -1).astype(F32)
    w_in = p["w_in_0"]
    c = 0
    for name, width in (("w_q", GLA_K), ("w_k", GLA_K), ("w_v", GLA_V), ("w_g", GLA_V)):
        w[name] = w_in[:, c:c + width].astype(BF16)
        c += width
    w["w_lr"] = jnp.pad(w_in[:, c:c + GLA_RANK], ((0, 0), (0, LANES - GLA_RANK))).astype(BF16)
    c += GLA_RANK
    w["w_u"] = w_in[:, c:c + S5_W].astype(BF16)
    w["w_alpha"] = jnp.pad(p["w_alpha_0"], ((0, LANES - GLA_RANK), (0, 0))).astype(BF16)
    w["b_alpha"] = row(p["b_alpha_0"])
    w["norm_mix_0"] = row(p["norm_mix_0"])
    w["gla_norm"] = row(p["gla_norm_0"])

    are, aim, bbre, bbim = _s5_prep(p["s5_lam_re"], p["s5_lam_im"], p["s5_log_dt"], p["s5_b_re"],
                                    p["s5_b_im"])
    npack = S5_GROUPS // S5_PACK
    eye = jnp.eye(S5_PACK, dtype=F32)
    pack_b = lambda m: jnp.einsum("jghp,gk->jghkp", m.reshape(npack, S5_PACK, S5_H, S5_P), eye).reshape(
        npack, S5_PACK * S5_H, S5_PACK * S5_P).astype(BF16)
    pack_c = lambda m: jnp.einsum("jghp,gk->jgpkh", m.reshape(npack, S5_PACK, S5_H, S5_P), eye).reshape(
        npack, S5_PACK * S5_P, S5_PACK * S5_H).astype(BF16)
    w["s5_are"] = are.reshape(1, S5_N)
    w["s5_aim"] = aim.reshape(1, S5_N)
    w["s5_bbre"] = pack_b(bbre)
    w["s5_bbim"] = pack_b(bbim)
    w["s5_cre"] = pack_c(p["s5_c_re"])
    w["s5_cim"] = pack_c(p["s5_c_im"])
    w["s5_d"] = row(p["s5_d"])
    w["s5_w_glu"] = p["s5_w_glu"].astype(BF16)
    w["s5_b_glu"] = row(p["s5_b_glu"])
    w["w_out_0a"] = p["w_out_0"][:GLA_V].astype(BF16)
    w["w_out_0b"] = p["w_out_0"][GLA_V:].astype(BF16)

    w["norm_mix_1"] = row(p["norm_mix_1"])
    w["w_gate_1"] = p["w_in_1"][:, :RNN_W].astype(BF16)
    w["w_xr_1"] = p["w_in_1"][:, RNN_W:].astype(BF16)
    w["rnn_conv_w"] = p["rnn_conv_w"].astype(F32)
    w["rnn_conv_b"] = row(p["rnn_conv_b"])
    w["rnn_wa"] = _pack_gate(p["rnn_w_a"])
    w["rnn_wx"] = _pack_gate(p["rnn_w_x"])
    w["rnn_b_a"] = row(p["rnn_b_a"])
    w["rnn_b_x"] = row(p["rnn_b_x"])
    w["rnn_lam"] = row(p["rnn_lam"])
    w["w_out_1"] = p["w_out_1"].astype(BF16)

    w["norm_ffn"] = [row(p["norm_ffn"][l]) for l in range(2)]
    w["ffn_w_gate"] = [p["ffn_w_up"][l, :, :D_FF].astype(BF16) for l in range(2)]
    w["ffn_w_val"] = [p["ffn_w_up"][l, :, D_FF:].astype(BF16) for l in range(2)]
    w["ffn_conv_w"] = [p["ffn_conv_w"][l].astype(F32) for l in range(2)]
    w["ffn_conv_b"] = [row(p["ffn_conv_b"][l]) for l in range(2)]
    w["ffn_w_down"] = [p["ffn_w_down"][l].astype(BF16) for l in range(2)]
    w["norm_final"] = row(p["norm_final"])
    return w


def _trunk(x, w, st, nb, nt, tt_gla, tt):
    q, k, v, gs, la, u = _l0_in(x, w, nb * tt)
    if nt == 1:
        og, s_gla = _gla_step(q, k, v, la, gs, w["gla_norm"], st["gla"])
    else:
        og, s_gla = _gla_chunk(q, k, v, la, gs, w["gla_norm"], st["gla"], nb, tt_gla)
        s_gla = s_gla.reshape(nb, GLA_HEADS, GLA_DK, GLA_DV)
    x, s_re, s_im = _l0_out(u, og, x, w, st["s5_re"], st["s5_im"], nb, tt)
    x, fc0 = _ffn(x, w, 0, st["fc"][0], nb, tt, False)
    x, h, rc = _l1(x, w, st["h"], st["rc"], nb, tt)
    x, fc1 = _ffn(x, w, 1, st["fc"][1], nb, tt, True)
    return x, dict(gla=s_gla, s5_re=s_re, s5_im=s_im, h=h, rc=rc, fc=[fc0, fc1])


def _time_major(cache):
    b, j, c = cache.shape
    return jnp.transpose(cache, (1, 0, 2)).reshape(j * b, c)


def _batch_major(cache, nb):
    jb, c = cache.shape
    return jnp.transpose(cache.reshape(jb // nb, nb, c), (1, 0, 2))


def kernel(x_prompt, x_sample, state_gla, state_s5_re, state_s5_im, state_rglru, cache_rglru_conv,
           cache_ffn_conv, meta_tokens, norm_mix_0, w_in_0, w_alpha_0, b_alpha_0, gla_norm_0,
           s5_lam_re, s5_lam_im, s5_log_dt, s5_b_re, s5_b_im, s5_c_re, s5_c_im, s5_d, s5_w_glu,
           s5_b_glu, w_out_0, norm_mix_1, w_in_1, rnn_conv_w, rnn_conv_b, rnn_w_a, rnn_b_a, rnn_w_x,
           rnn_b_x, rnn_lam, w_out_1, norm_ffn, ffn_w_up, ffn_conv_w, ffn_conv_b, ffn_w_down, norm_final):
    w = _prep_weights(dict(
        norm_mix_0=norm_mix_0, w_in_0=w_in_0, w_alpha_0=w_alpha_0, b_alpha_0=b_alpha_0,
        gla_norm_0=gla_norm_0, s5_lam_re=s5_lam_re, s5_lam_im=s5_lam_im, s5_log_dt=s5_log_dt,
        s5_b_re=s5_b_re, s5_b_im=s5_b_im, s5_c_re=s5_c_re, s5_c_im=s5_c_im, s5_d=s5_d,
        s5_w_glu=s5_w_glu, s5_b_glu=s5_b_glu, w_out_0=w_out_0, norm_mix_1=norm_mix_1, w_in_1=w_in_1,
        rnn_conv_w=rnn_conv_w, rnn_conv_b=rnn_conv_b, rnn_w_a=rnn_w_a, rnn_b_a=rnn_b_a,
        rnn_w_x=rnn_w_x, rnn_b_x=rnn_b_x, rnn_lam=rnn_lam, w_out_1=w_out_1, norm_ffn=norm_ffn,
        ffn_w_up=ffn_w_up, ffn_conv_w=ffn_conv_w, ffn_conv_b=ffn_conv_b, ffn_w_down=ffn_w_down,
        norm_final=norm_final))

    bp, seq, _ = x_prompt.shape
    bs = x_sample.shape[0]

    zeros = dict(gla=jnp.zeros((bp, GLA_HEADS, GLA_DK, GLA_DV), F32),
                 s5_re=jnp.zeros((bp, S5_N), F32), s5_im=jnp.zeros((bp, S5_N), F32),
                 h=jnp.zeros((bp, RNN_W), F32), rc=jnp.zeros(((RNN_CONV - 1) * bp, RNN_W), F32),
                 fc=[jnp.zeros(((FFN_CONV - 1) * bp, D_FF), F32)] * 2)
    x_meta = jnp.repeat(meta_tokens.astype(F32), bp, axis=0)
    _, st = _trunk(x_meta, w, zeros, bp, N_META, N_META, N_META)

    xp = jnp.transpose(x_prompt, (1, 0, 2)).reshape(seq * bp, D_MODEL)
    yp, st = _trunk(xp, w, st, bp, seq, 64, 32)
    yp = jnp.transpose(yp.reshape(seq, bp, D_MODEL), (1, 0, 2))

    st_s = dict(gla=state_gla, s5_re=state_s5_re.reshape(bs, S5_N), s5_im=state_s5_im.reshape(bs, S5_N),
                h=state_rglru, rc=_time_major(cache_rglru_conv),
                fc=[_time_major(cache_ffn_conv[l]) for l in range(2)])
    ys, ss = _trunk(x_sample.reshape(bs, D_MODEL), w, st_s, bs, 1, 1, 1)
    ys = ys.reshape(bs, 1, D_MODEL)

    grp = lambda z, nb: z.reshape(nb, S5_GROUPS, S5_P)
    return (yp, ys, st["gla"], ss["gla"], grp(st["s5_re"], bp), grp(ss["s5_re"], bs),
            grp(st["s5_im"], bp), grp(ss["s5_im"], bs), st["h"], ss["h"],
            _batch_major(st["rc"], bp), _batch_major(ss["rc"], bs),
            jnp.stack([_batch_major(st["fc"][l], bp) for l in range(2)]),
            jnp.stack([_batch_major(ss["fc"][l], bs) for l in range(2)]))
```

```python
import functools
import math

import jax
import jax.numpy as jnp
from jax import lax
from jax.experimental import pallas as pl
from jax.experimental.pallas import tpu as pltpu

F32 = jnp.float32
BF16 = jnp.bfloat16

D_MODEL = 1024
N_META = 16
EPS = 1e-6
GLA_HEADS = 4
GLA_DK = 64
GLA_DV = 128
GLA_RANK = 16
GLA_TAU = 16.0
GLA_K = GLA_HEADS * GLA_DK
GLA_V = GLA_HEADS * GLA_DV
S5_GROUPS = 32
S5_H = 16
S5_P = 64
S5_W = S5_GROUPS * S5_H
S5_N = S5_GROUPS * S5_P
RNN_W = 1536
RNN_BLOCKS = 16
RNN_BW = RNN_W // RNN_BLOCKS
RNN_C = 8.0
RNN_CONV = 4
D_FF = 2816
FFN_CONV = 3

LANES = 128
SUBLANES = 8
FF_CHUNK = 256
SCAN_CHUNK = 512
S5_PACK = 8
GATE_WIN = 768
GATE_K = 512
GATE_N = 256
VMEM_LIMIT = 56 * 1024 * 1024


def _rms(x, g):
    return x * lax.rsqrt(jnp.mean(x * x, axis=-1, keepdims=True) + EPS) * g


def _mm(a, w):
    return jnp.dot(a.astype(BF16), w, preferred_element_type=F32)


def _const_spec(shape):
    nd = len(shape)
    return pl.BlockSpec(shape, lambda i, _n=nd: (0,) * _n, pipeline_mode=pl.Buffered(1))


def _row_spec(rows, cols):
    return pl.BlockSpec((rows, cols), lambda i: (i, 0))


def _seq_spec(nb, tt, cols):
    return pl.BlockSpec((nb, tt, cols), lambda i: (0, i, 0))


def _tm_scratch(rows, cols):
    return pltpu.VMEM((cols // LANES, rows, LANES), F32)


def _load_time_major(x_ref, tm_sc):
    if tm_sc is None:
        return x_ref[...]
    nb, tt, cols = x_ref.shape
    for b in range(nb):
        for j in range(cols // LANES):
            tm_sc[j, pl.ds(b, tt, stride=nb), :] = x_ref[b, :, j * LANES:(j + 1) * LANES]
    return jnp.concatenate([tm_sc[j] for j in range(cols // LANES)], axis=-1)


def _store_time_major(o_ref, val, tm_sc):
    if tm_sc is None:
        o_ref[...] = val
        return
    nb, tt, cols = o_ref.shape
    for j in range(cols // LANES):
        tm_sc[j] = val[:, j * LANES:(j + 1) * LANES]
    for b in range(nb):
        for j in range(cols // LANES):
            o_ref[b, :, j * LANES:(j + 1) * LANES] = tm_sc[j, pl.ds(b, tt, stride=nb), :]


def _params(sem="arbitrary"):
    return pltpu.CompilerParams(dimension_semantics=(sem,), vmem_limit_bytes=VMEM_LIMIT)


def _s5_prep_kernel(lr_ref, li_ref, ldt_ref, brt_ref, bit_ref, are_ref, aim_ref, bbre_ref, bbim_ref):
    lr = lr_ref[...]
    li = li_ref[...]
    dt = jnp.exp(ldt_ref[...])
    mag = jnp.exp(lr * dt)
    ab_re = mag * jnp.cos(li * dt)
    ab_im = mag * jnp.sin(li * dt)
    den = lr * lr + li * li
    nr = ab_re - 1.0
    ni = ab_im
    f_re = (nr * lr + ni * li) / den
    f_im = (ni * lr - nr * li) / den
    are_ref[...] = ab_re
    aim_ref[...] = ab_im
    brt = brt_ref[...]
    bit = bit_ref[...]
    bbre_ref[...] = f_re[:, None, :] * brt - f_im[:, None, :] * bit
    bbim_ref[...] = f_re[:, None, :] * bit + f_im[:, None, :] * brt


def _s5_prep(lam_re, lam_im, log_dt, b_re, b_im):
    g, p, h = b_re.shape
    brt = jnp.transpose(b_re, (0, 2, 1))
    bit = jnp.transpose(b_im, (0, 2, 1))
    return pl.pallas_call(
        _s5_prep_kernel,
        out_shape=(jax.ShapeDtypeStruct((g, p), F32), jax.ShapeDtypeStruct((g, p), F32),
                   jax.ShapeDtypeStruct((g, h, p), F32), jax.ShapeDtypeStruct((g, h, p), F32)),
        name="s5_prep",
    )(lam_re, lam_im, log_dt.reshape(g, 1), brt, bit)


def _l0_in_kernel(x_ref, g_ref, wq_ref, wk_ref, wv_ref, wg_ref, wu_ref, wlr_ref, wal_ref, bal_ref,
                  q_ref, k_ref, v_ref, gs_ref, la_ref, u_ref, *tm_sc):
    xn = _rms(_load_time_major(x_ref, tm_sc[0] if tm_sc else None), g_ref[...]).astype(BF16)
    q_ref[...] = _mm(xn, wq_ref[...]) * (GLA_DK ** -0.5)
    k_ref[...] = _mm(xn, wk_ref[...])
    v_ref[...] = _mm(xn, wv_ref[...])
    g = _mm(xn, wg_ref[...])
    gs_ref[...] = g * jax.nn.sigmoid(g)
    u_ref[...] = _mm(xn, wu_ref[...])
    lr = _mm(xn, wlr_ref[...])
    pre = _mm(lr, wal_ref[...]) + bal_ref[...]
    la_ref[...] = jax.nn.log_sigmoid(pre) * (1.0 / GLA_TAU)


def _l0_in(x, w, nb, tt):
    rows = nb * tt
    batch_major = x.ndim == 3
    n = x.shape[0] * x.shape[1] if batch_major else x.shape[0]
    outs = [GLA_K, GLA_K, GLA_V, GLA_V, GLA_K, S5_W]
    return pl.pallas_call(
        _l0_in_kernel,
        grid=(n // rows,),
        in_specs=[_seq_spec(nb, tt, D_MODEL) if batch_major else _row_spec(rows, D_MODEL),
                  _const_spec((1, D_MODEL)),
                  _const_spec((D_MODEL, GLA_K)), _const_spec((D_MODEL, GLA_K)),
                  _const_spec((D_MODEL, GLA_V)), _const_spec((D_MODEL, GLA_V)),
                  _const_spec((D_MODEL, S5_W)), _const_spec((D_MODEL, LANES)),
                  _const_spec((LANES, GLA_K)), _const_spec((1, GLA_K))],
        out_specs=[_row_spec(rows, c) for c in outs],
        out_shape=[jax.ShapeDtypeStruct((n, c), F32) for c in outs],
        scratch_shapes=[_tm_scratch(rows, D_MODEL)] if batch_major else [],
        compiler_params=_params("parallel"),
        name="l0_in",
    )(x, w["norm_mix_0"], w["w_q"], w["w_k"], w["w_v"], w["w_g"], w["w_u"], w["w_lr"], w["w_alpha"],
      w["b_alpha"])


def _gla_chunk_kernel(q_ref, k_ref, v0_ref, v1_ref, v2_ref, v3_ref, la_ref, gs_ref, gn_ref, s0_ref,
                      o_ref, s_ref, b_sc, gam_sc, qt_sc, kt_sc, kh_sc, oo_sc, *, nb, c):
    v_refs = (v0_ref, v1_ref, v2_ref, v3_ref)

    @pl.when(pl.program_id(0) == 0)
    def _():
        s_ref[...] = s0_ref[...]

    def cum_body(t, run):
        rows = pl.ds(pl.multiple_of(t * nb, nb), nb)
        run = run + la_ref[rows, :]
        b_sc[rows, :] = run
        return run

    bl = lax.fori_loop(0, c, cum_body, jnp.zeros((nb, GLA_K), F32))
    gam = jnp.exp(bl)
    for b in range(nb):
        gam_sc[b] = jnp.broadcast_to(gam[b:b + 1, :], gam_sc.shape[1:])

    def scale_body(t, carry):
        rows = pl.ds(pl.multiple_of(t * nb, nb), nb)
        b = b_sc[rows, :]
        k = k_ref[rows, :]
        qt = q_ref[rows, :] * jnp.exp(b)
        kt = k * jnp.exp(-b)
        kh = k * jnp.exp(bl - b)
        for j in range(GLA_K // LANES):
            ls = slice(j * LANES, (j + 1) * LANES)
            qt_sc[j, rows, :] = qt[:, ls]
            kt_sc[j, rows, :] = kt[:, ls]
            kh_sc[j, rows, :] = kh[:, ls]
        return carry

    lax.fori_loop(0, c, scale_body, 0)

    causal = lax.broadcasted_iota(jnp.int32, (c, c), 0) >= lax.broadcasted_iota(jnp.int32, (c, c), 1)
    eye = lax.broadcasted_iota(jnp.int32, (GLA_DK, GLA_DK), 0) == lax.broadcasted_iota(
        jnp.int32, (GLA_DK, GLA_DK), 1)
    heads_per_buf = LANES // GLA_DK

    def seq_body(bi, carry):
        rows = pl.ds(bi, c, stride=nb)
        for h in range(GLA_HEADS):
            j = h // heads_per_buf
            ks = slice((h % heads_per_buf) * GLA_DK, (h % heads_per_buf + 1) * GLA_DK)
            qb = qt_sc[j, rows, :][:, ks].astype(BF16)
            kb = kt_sc[j, rows, :][:, ks].astype(BF16)
            khb = kh_sc[j, rows, :][:, ks].astype(BF16)
            vb = v_refs[h][rows, :].astype(BF16)
            att = lax.dot_general(qb, kb, (((1,), (1,)), ((), ())), preferred_element_type=F32)
            att = jnp.where(causal, att, 0.0)
            s = s_ref[bi * GLA_HEADS + h]
            o = (jnp.dot(att.astype(BF16), vb, preferred_element_type=F32)
                 + jnp.dot(qb, s.astype(BF16), preferred_element_type=F32))
            oo_sc[h, rows, :] = o
            gam_row = gam_sc[bi][0:1, h * GLA_DK:(h + 1) * GLA_DK]
            gam_col = jnp.sum(jnp.where(eye, gam_row, 0.0), axis=1, keepdims=True)
            s_ref[bi * GLA_HEADS + h] = gam_col * s + lax.dot_general(
                khb, vb, (((0,), (0,)), ((), ())), preferred_element_type=F32)
        return carry

    lax.fori_loop(0, nb, seq_body, 0)

    for h in range(GLA_HEADS):
        vs = slice(h * GLA_DV, (h + 1) * GLA_DV)
        o_ref[:, vs] = _rms(oo_sc[h], gn_ref[:, vs]) * gs_ref[:, vs]


def _gla_chunk(q, k, v, la, gs, gn, s0, nb, c):
    n = q.shape[0]
    rows = nb * c
    sshape = (nb * GLA_HEADS, GLA_DK, GLA_DV)
    kern = functools.partial(_gla_chunk_kernel, nb=nb, c=c)
    v_specs = [pl.BlockSpec((rows, GLA_DV), lambda i, _h=h: (i, _h)) for h in range(GLA_HEADS)]
    pair = pltpu.VMEM((GLA_K // LANES, rows, LANES), F32)
    return pl.pallas_call(
        kern,
        grid=(n // rows,),
        in_specs=[_row_spec(rows, GLA_K), _row_spec(rows, GLA_K)] + v_specs
                 + [_row_spec(rows, GLA_K), _row_spec(rows, GLA_V), _const_spec((1, GLA_V)),
                    _const_spec(sshape)],
        out_specs=[_row_spec(rows, GLA_V), pl.BlockSpec(sshape, lambda i: (0, 0, 0))],
        out_shape=[jax.ShapeDtypeStruct((n, GLA_V), F32), jax.ShapeDtypeStruct(sshape, F32)],
        scratch_shapes=[pltpu.VMEM((rows, GLA_K), F32), pltpu.VMEM((nb, SUBLANES, GLA_K), F32), pair, pair, pair,
                        pltpu.VMEM((GLA_HEADS, rows, GLA_DV), F32)],
        compiler_params=_params(),
        name="gla_chunk",
    )(q, k, v, v, v, v, la, gs, gn, s0.reshape(sshape))


def _gla_step_kernel(q_ref, k_ref, la_ref, v_ref, gs_ref, gn_ref, s0_ref, o_ref, s_ref):
    q = q_ref[...]
    k = k_ref[...]
    a = jnp.exp(la_ref[...])
    v = v_ref[...]
    o = jnp.zeros_like(v)
    for d in range(GLA_DK):
        s_new = a[:, d:d + 1] * s0_ref[:, d, :] + k[:, d:d + 1] * v
        s_ref[:, d, :] = s_new
        o = o + q[:, d:d + 1] * s_new
    o_ref[...] = _rms(o, gn_ref[...]) * gs_ref[...]


def _gla_step(q, k, v, la, gs, gn, s0):
    nb = q.shape[0]
    heads = lambda z: jnp.transpose(z.reshape(nb, GLA_HEADS, GLA_DK), (1, 0, 2))
    hspec = pl.BlockSpec((None, nb, GLA_DK), lambda h: (h, 0, 0))
    vspec = pl.BlockSpec((nb, GLA_DV), lambda h: (0, h))
    sspec = pl.BlockSpec((nb, None, GLA_DK, GLA_DV), lambda h: (0, h, 0, 0))
    return pl.pallas_call(
        _gla_step_kernel,
        grid=(GLA_HEADS,),
        in_specs=[hspec, hspec, hspec, vspec, vspec, pl.BlockSpec((1, GLA_DV), lambda h: (0, h)), sspec],
        out_specs=[vspec, sspec],
        out_shape=[jax.ShapeDtypeStruct((nb, GLA_V), F32),
                   jax.ShapeDtypeStruct((nb, GLA_HEADS, GLA_DK, GLA_DV), F32)],
        compiler_params=_params("parallel"),
        name="gla_step",
    )(heads(q), heads(k), heads(la), v, gs, gn, s0)


def _l0_out_kernel(u_ref, og_ref, x_ref, bbre_ref, bbim_ref, are_ref, aim_ref, cre_ref, cim_ref, d_ref,
                   wglu_ref, bglu_ref, woa_ref, wob_ref, xr0_ref, xi0_ref,
                   xo_ref, xre_ref, xim_ref, sr_sc, si_sc, *tm_sc, nb, tt):
    @pl.when(pl.program_id(0) == 0)
    def _():
        xre_ref[...] = xr0_ref[...]
        xim_ref[...] = xi0_ref[...]

    u = u_ref[...]
    ub = u.astype(BF16)
    kin = S5_PACK * S5_H
    kst = S5_PACK * S5_P
    npack = S5_GROUPS // S5_PACK
    for j in range(npack):
        uj = ub[:, j * kin:(j + 1) * kin]
        sr_sc[:, j * kst:(j + 1) * kst] = jnp.dot(uj, bbre_ref[j], preferred_element_type=F32)
        si_sc[:, j * kst:(j + 1) * kst] = jnp.dot(uj, bbim_ref[j], preferred_element_type=F32)

    for ch in range(S5_N // SCAN_CHUNK):
        cs = pl.ds(ch * SCAN_CHUNK, SCAN_CHUNK)
        ar = are_ref[:, cs]
        ai = aim_ref[:, cs]

        def body(t, carry, cs=cs, ar=ar, ai=ai):
            xr, xi = carry
            rows = pl.ds(pl.multiple_of(t * nb, nb), nb)
            nxr = ar * xr - ai * xi + sr_sc[rows, cs]
            nxi = ar * xi + ai * xr + si_sc[rows, cs]
            sr_sc[rows, cs] = nxr
            si_sc[rows, cs] = nxi
            return nxr, nxi

        xr, xi = lax.fori_loop(0, tt, body, (xre_ref[:, cs], xim_ref[:, cs]))
        xre_ref[:, cs] = xr
        xim_ref[:, cs] = xi

    ys = []
    for j in range(npack):
        xrj = sr_sc[:, j * kst:(j + 1) * kst].astype(BF16)
        xij = si_sc[:, j * kst:(j + 1) * kst].astype(BF16)
        ys.append(jnp.dot(xrj, cre_ref[j], preferred_element_type=F32)
                  - jnp.dot(xij, cim_ref[j], preferred_element_type=F32))
    y = jnp.concatenate(ys, axis=-1) + d_ref[...] * u
    y = jax.nn.gelu(y)
    y = y * jax.nn.sigmoid(_mm(y, wglu_ref[...]) + bglu_ref[...])
    x = _load_time_major(x_ref, tm_sc[0] if tm_sc else None)
    xo_ref[...] = x + _mm(og_ref[...], woa_ref[...]) + _mm(y, wob_ref[...])


def _l0_out(u, og, x, w, xr0, xi0, nb, tt):
    n = u.shape[0]
    rows = nb * tt
    batch_major = x.ndim == 3
    npack = S5_GROUPS // S5_PACK
    kin, kst = S5_PACK * S5_H, S5_PACK * S5_P
    kern = functools.partial(_l0_out_kernel, nb=nb, tt=tt)
    st_spec = pl.BlockSpec((nb, S5_N), lambda i: (0, 0))
    return pl.pallas_call(
        kern,
        grid=(n // rows,),
        in_specs=[_row_spec(rows, S5_W), _row_spec(rows, GLA_V),
                  _seq_spec(nb, tt, D_MODEL) if batch_major else _row_spec(rows, D_MODEL),
                  _const_spec((npack, kin, kst)), _const_spec((npack, kin, kst)),
                  _const_spec((1, S5_N)), _const_spec((1, S5_N)),
                  _const_spec((npack, kst, kin)), _const_spec((npack, kst, kin)),
                  _const_spec((1, S5_W)), _const_spec((S5_W, S5_W)), _const_spec((1, S5_W)),
                  _const_spec((GLA_V, D_MODEL)), _const_spec((S5_W, D_MODEL)),
                  _const_spec((nb, S5_N)), _const_spec((nb, S5_N))],
        out_specs=[_row_spec(rows, D_MODEL), st_spec, st_spec],
        out_shape=[jax.ShapeDtypeStruct((n, D_MODEL), F32), jax.ShapeDtypeStruct((nb, S5_N), F32),
                   jax.ShapeDtypeStruct((nb, S5_N), F32)],
        scratch_shapes=[pltpu.VMEM((rows, S5_N), F32)] * 2
                       + ([_tm_scratch(rows, D_MODEL)] if batch_major else []),
        compiler_params=_params(),
        name="l0_out",
    )(u, og, x, w["s5_bbre"], w["s5_bbim"], w["s5_are"], w["s5_aim"], w["s5_cre"], w["s5_cim"],
      w["s5_d"], w["s5_w_glu"], w["s5_b_glu"], w["w_out_0a"], w["w_out_0b"], xr0, xi0)


def _ffn_kernel(x_ref, g_ref, wg_ref, wv_ref, cw_ref, cb_ref, wd_ref, c0_ref, gf_ref, xo_ref, c_ref,
                *tm_sc, nb, tt, final):
    @pl.when(pl.program_id(0) == 0)
    def _():
        c_ref[...] = c0_ref[...]

    rows = nb * tt
    x = x_ref[...]
    xn = _rms(x, g_ref[...]).astype(BF16)
    acc = jnp.zeros((rows, D_MODEL), F32)
    for ci in range(D_FF // FF_CHUNK):
        cs = slice(ci * FF_CHUNK, (ci + 1) * FF_CHUNK)
        gate = jnp.dot(xn, wg_ref[:, cs], preferred_element_type=F32)
        val = jnp.dot(xn, wv_ref[:, cs], preferred_element_type=F32)
        ext = jnp.concatenate([c_ref[:, cs], gate], axis=0)
        y = cb_ref[:, cs] + ext[0:rows] * cw_ref[0:1, cs]
        y = y + ext[nb:nb + rows] * cw_ref[1:2, cs]
        y = y + ext[2 * nb:2 * nb + rows] * cw_ref[2:3, cs]
        c_ref[:, cs] = ext[tt * nb:(tt + 2) * nb]
        hmid = jax.nn.gelu(y) * val
        acc = acc + jnp.dot(hmid.astype(BF16), wd_ref[cs, :], preferred_element_type=F32)
    out = x + acc
    if final:
        out = _rms(out, gf_ref[...])
    _store_time_major(xo_ref, out, tm_sc[0] if tm_sc else None)


def _ffn(x, w, layer, c0, nb, tt, final, batch_major_out=False):
    n = x.shape[0]
    rows = nb * tt
    kern = functools.partial(_ffn_kernel, nb=nb, tt=tt, final=final)
    if batch_major_out:
        o_spec, o_shape = _seq_spec(nb, tt, D_MODEL), (nb, n // nb, D_MODEL)
    else:
        o_spec, o_shape = _row_spec(rows, D_MODEL), (n, D_MODEL)
    cshape = ((FFN_CONV - 1) * nb, D_FF)
    return pl.pallas_call(
        kern,
        grid=(n // rows,),
        in_specs=[_row_spec(rows, D_MODEL), _const_spec((1, D_MODEL)),
                  _const_spec((D_MODEL, D_FF)), _const_spec((D_MODEL, D_FF)),
                  _const_spec((FFN_CONV, D_FF)), _const_spec((1, D_FF)),
                  _const_spec((D_FF, D_MODEL)), _const_spec(cshape), _const_spec((1, D_MODEL))],
        out_specs=[o_spec, pl.BlockSpec(cshape, lambda i: (0, 0))],
        out_shape=[jax.ShapeDtypeStruct(o_shape, F32), jax.ShapeDtypeStruct(cshape, F32)],
        scratch_shapes=[_tm_scratch(rows, D_MODEL)] if batch_major_out else [],
        compiler_params=_params(),
        name="ffn%d" % layer,
    )(x, w["norm_ffn"][layer], w["ffn_w_gate"][layer], w["ffn_w_val"][layer], w["ffn_conv_w"][layer],
      w["ffn_conv_b"][layer], w["ffn_w_down"][layer], c0, w["norm_final"])


def _l1_kernel(x_ref, g_ref, wgt_ref, wxr_ref, cw_ref, cb_ref, wa_ref, ba_ref, wx_ref, bx_ref, lam_ref,
               wo_ref, h0_ref, c0_ref, xo_ref, h_ref, c_ref, a_sc, b_sc, *, nb, tt):
    @pl.when(pl.program_id(0) == 0)
    def _():
        h_ref[...] = h0_ref[...]
        c_ref[...] = c0_ref[...]

    rows = nb * tt
    x = x_ref[...]
    xn = _rms(x, g_ref[...]).astype(BF16)
    xr = jnp.dot(xn, wxr_ref[...], preferred_element_type=F32)
    ext = jnp.concatenate([c_ref[...], xr], axis=0)
    xc = cb_ref[...] + ext[0:rows] * cw_ref[0:1, :]
    for j in range(1, RNN_CONV):
        xc = xc + ext[j * nb:j * nb + rows] * cw_ref[j:j + 1, :]
    c_ref[...] = ext[tt * nb:(tt + RNN_CONV - 1) * nb]

    xcb = xc.astype(BF16)
    rs, gs = [], []
    for wi in range(RNN_W // GATE_WIN):
        for ni in range(GATE_WIN // GATE_N):
            k0 = wi * GATE_WIN + ni * LANES
            lhs = xcb[:, k0:k0 + GATE_K]
            rs.append(jnp.dot(lhs, wa_ref[wi, ni], preferred_element_type=F32))
            gs.append(jnp.dot(lhs, wx_ref[wi, ni], preferred_element_type=F32))
    r = jax.nn.sigmoid(jnp.concatenate(rs, axis=-1) + ba_ref[...])
    ig = jax.nn.sigmoid(jnp.concatenate(gs, axis=-1) + bx_ref[...])
    log_a = (-RNN_C) * r * jax.nn.softplus(-lam_ref[...])
    a = jnp.exp(log_a)
    a_sc[...] = a
    b_sc[...] = jnp.sqrt(jnp.tanh(-log_a) * (a * a + 1.0)) * (ig * xc)

    for ch in range(RNN_W // SCAN_CHUNK):
        cs = pl.ds(ch * SCAN_CHUNK, SCAN_CHUNK)

        def body(t, h, cs=cs):
            rws = pl.ds(pl.multiple_of(t * nb, nb), nb)
            h = a_sc[rws, cs] * h + b_sc[rws, cs]
            b_sc[rws, cs] = h
            return h

        h_ref[:, cs] = lax.fori_loop(0, tt, body, h_ref[:, cs])

    gate = jnp.dot(xn, wgt_ref[...], preferred_element_type=F32)
    xo_ref[...] = x + _mm(b_sc[...] * jax.nn.gelu(gate), wo_ref[...])


def _l1(x, w, h0, c0, nb, tt):
    n = x.shape[0]
    rows = nb * tt
    kern = functools.partial(_l1_kernel, nb=nb, tt=tt)
    cshape = ((RNN_CONV - 1) * nb, RNN_W)
    gshape = (RNN_W // GATE_WIN, GATE_WIN // GATE_N, GATE_K, GATE_N)
    return pl.pallas_call(
        kern,
        grid=(n // rows,),
        in_specs=[_row_spec(rows, D_MODEL), _const_spec((1, D_MODEL)),
                  _const_spec((D_MODEL, RNN_W)), _const_spec((D_MODEL, RNN_W)),
                  _const_spec((RNN_CONV, RNN_W)), _const_spec((1, RNN_W)),
                  _const_spec(gshape), _const_spec((1, RNN_W)),
                  _const_spec(gshape), _const_spec((1, RNN_W)),
                  _const_spec((1, RNN_W)), _const_spec((RNN_W, D_MODEL)),
                  _const_spec((nb, RNN_W)), _const_spec(cshape)],
        out_specs=[_row_spec(rows, D_MODEL), pl.BlockSpec((nb, RNN_W), lambda i: (0, 0)),
                   pl.BlockSpec(cshape, lambda i: (0, 0))],
        out_shape=[jax.ShapeDtypeStruct((n, D_MODEL), F32), jax.ShapeDtypeStruct((nb, RNN_W), F32),
                   jax.ShapeDtypeStruct(cshape, F32)],
        scratch_shapes=[pltpu.VMEM((rows, RNN_W), F32)] * 2,
        compiler_params=_params(),
        name="l1_mixer",
    )(x, w["norm_mix_1"], w["w_gate_1"], w["w_xr_1"], w["rnn_conv_w"], w["rnn_conv_b"],
      w["rnn_wa"], w["rnn_b_a"], w["rnn_wx"], w["rnn_b_x"], w["rnn_lam"], w["w_out_1"], h0, c0)


def _pack_gate(wblk):
    dense = jax.scipy.linalg.block_diag(*[wblk[i] for i in range(RNN_BLOCKS)])
    tiles = []
    for wi in range(RNN_W // GATE_WIN):
        row = []
        for ni in range(GATE_WIN // GATE_N):
            k0 = wi * GATE_WIN + ni * LANES
            n0 = wi * GATE_WIN + ni * GATE_N
            row.append(dense[k0:k0 + GATE_K, n0:n0 + GATE_N])
        tiles.append(jnp.stack(row))
    return jnp.stack(tiles).astype(BF16)


def _prep_weights(p):
    w = {}
    row = lambda v: v.reshape(1, -1).astype(F32)
    w_in = p["w_in_0"]
    c = 0
    for name, width in (("w_q", GLA_K), ("w_k", GLA_K), ("w_v", GLA_V), ("w_g", GLA_V)):
        w[name] = w_in[:, c:c + width].astype(BF16)
        c += width
    w["w_lr"] = jnp.pad(w_in[:, c:c + GLA_RANK], ((0, 0), (0, LANES - GLA_RANK))).astype(BF16)
    c += GLA_RANK
    w["w_u"] = w_in[:, c:c + S5_W].astype(BF16)
    w["w_alpha"] = jnp.pad(p["w_alpha_0"], ((0, LANES - GLA_RANK), (0, 0))).astype(BF16)
    w["b_alpha"] = row(p["b_alpha_0"])
    w["norm_mix_0"] = row(p["norm_mix_0"])
    w["gla_norm"] = row(p["gla_norm_0"])

    are, aim, bbre, bbim = _s5_prep(p["s5_lam_re"], p["s5_lam_im"], p["s5_log_dt"], p["s5_b_re"],
                                    p["s5_b_im"])
    npack = S5_GROUPS // S5_PACK
    eye = jnp.eye(S5_PACK, dtype=F32)
    pack_b = lambda m: jnp.einsum("jghp,gk->jghkp", m.reshape(npack, S5_PACK, S5_H, S5_P), eye).reshape(
        npack, S5_PACK * S5_H, S5_PACK * S5_P).astype(BF16)
    pack_c = lambda m: jnp.einsum("jghp,gk->jgpkh", m.reshape(npack, S5_PACK, S5_H, S5_P), eye).reshape(
        npack, S5_PACK * S5_P, S5_PACK * S5_H).astype(BF16)
    w["s5_are"] = are.reshape(1, S5_N)
    w["s5_aim"] = aim.reshape(1, S5_N)
    w["s5_bbre"] = pack_b(bbre)
    w["s5_bbim"] = pack_b(bbim)
    w["s5_cre"] = pack_c(p["s5_c_re"])
    w["s5_cim"] = pack_c(p["s5_c_im"])
    w["s5_d"] = row(p["s5_d"])
    w["s5_w_glu"] = p["s5_w_glu"].astype(BF16)
    w["s5_b_glu"] = row(p["s5_b_glu"])
    w["w_out_0a"] = p["w_out_0"][:GLA_V].astype(BF16)
    w["w_out_0b"] = p["w_out_0"][GLA_V:].astype(BF16)

    w["norm_mix_1"] = row(p["norm_mix_1"])
    w["w_gate_1"] = p["w_in_1"][:, :RNN_W].astype(BF16)
    w["w_xr_1"] = p["w_in_1"][:, RNN_W:].astype(BF16)
    w["rnn_conv_w"] = p["rnn_conv_w"].astype(F32)
    w["rnn_conv_b"] = row(p["rnn_conv_b"])
    w["rnn_wa"] = _pack_gate(p["rnn_w_a"])
    w["rnn_wx"] = _pack_gate(p["rnn_w_x"])
    w["rnn_b_a"] = row(p["rnn_b_a"])
    w["rnn_b_x"] = row(p["rnn_b_x"])
    w["rnn_lam"] = row(p["rnn_lam"])
    w["w_out_1"] = p["w_out_1"].astype(BF16)

    w["norm_ffn"] = [row(p["norm_ffn"][l]) for l in range(2)]
    w["ffn_w_gate"] = [p["ffn_w_up"][l, :, :D_FF].astype(BF16) for l in range(2)]
    w["ffn_w_val"] = [p["ffn_w_up"][l, :, D_FF:].astype(BF16) for l in range(2)]
    w["ffn_conv_w"] = [p["ffn_conv_w"][l].astype(F32) for l in range(2)]
    w["ffn_conv_b"] = [row(p["ffn_conv_b"][l]) for l in range(2)]
    w["ffn_w_down"] = [p["ffn_w_down"][l].astype(BF16) for l in range(2)]
    w["norm_final"] = row(p["norm_final"])
    return w


def _tile_steps(nt):
    cap = lambda m: min(nt, m)
    return dict(l0_in=cap(64), gla=cap(64), l0_out=cap(32), ffn=cap(64), l1=cap(32))


def _trunk(x, w, st, nb, nt):
    tt = _tile_steps(nt)
    batch_major = x.ndim == 3
    q, k, v, gs, la, u = _l0_in(x, w, nb, tt["l0_in"])
    if nt == 1:
        og, s_gla = _gla_step(q, k, v, la, gs, w["gla_norm"], st["gla"])
    else:
        og, s_gla = _gla_chunk(q, k, v, la, gs, w["gla_norm"], st["gla"], nb, tt["gla"])
        s_gla = s_gla.reshape(nb, GLA_HEADS, GLA_DK, GLA_DV)
    x, s_re, s_im = _l0_out(u, og, x, w, st["s5_re"], st["s5_im"], nb, tt["l0_out"])
    x, fc0 = _ffn(x, w, 0, st["fc"][0], nb, tt["ffn"], False)
    x, h, rc = _l1(x, w, st["h"], st["rc"], nb, tt["l1"])
    x, fc1 = _ffn(x, w, 1, st["fc"][1], nb, tt["ffn"], True, batch_major_out=batch_major)
    return x, dict(gla=s_gla, s5_re=s_re, s5_im=s_im, h=h, rc=rc, fc=[fc0, fc1])


def _time_major(cache):
    b, j, c = cache.shape
    return jnp.transpose(cache, (1, 0, 2)).reshape(j * b, c)


def _batch_major(cache, nb):
    jb, c = cache.shape
    return jnp.transpose(cache.reshape(jb // nb, nb, c), (1, 0, 2))


def kernel(x_prompt, x_sample, state_gla, state_s5_re, state_s5_im, state_rglru, cache_rglru_conv,
           cache_ffn_conv, meta_tokens, norm_mix_0, w_in_0, w_alpha_0, b_alpha_0, gla_norm_0,
           s5_lam_re, s5_lam_im, s5_log_dt, s5_b_re, s5_b_im, s5_c_re, s5_c_im, s5_d, s5_w_glu,
           s5_b_glu, w_out_0, norm_mix_1, w_in_1, rnn_conv_w, rnn_conv_b, rnn_w_a, rnn_b_a, rnn_w_x,
           rnn_b_x, rnn_lam, w_out_1, norm_ffn, ffn_w_up, ffn_conv_w, ffn_conv_b, ffn_w_down, norm_final):
    w = _prep_weights(dict(
        norm_mix_0=norm_mix_0, w_in_0=w_in_0, w_alpha_0=w_alpha_0, b_alpha_0=b_alpha_0,
        gla_norm_0=gla_norm_0, s5_lam_re=s5_lam_re, s5_lam_im=s5_lam_im, s5_log_dt=s5_log_dt,
        s5_b_re=s5_b_re, s5_b_im=s5_b_im, s5_c_re=s5_c_re, s5_c_im=s5_c_im, s5_d=s5_d,
        s5_w_glu=s5_w_glu, s5_b_glu=s5_b_glu, w_out_0=w_out_0, norm_mix_1=norm_mix_1, w_in_1=w_in_1,
        rnn_conv_w=rnn_conv_w, rnn_conv_b=rnn_conv_b, rnn_w_a=rnn_w_a, rnn_b_a=rnn_b_a,
        rnn_w_x=rnn_w_x, rnn_b_x=rnn_b_x, rnn_lam=rnn_lam, w_out_1=w_out_1, norm_ffn=norm_ffn,
        ffn_w_up=ffn_w_up, ffn_conv_w=ffn_conv_w, ffn_conv_b=ffn_conv_b, ffn_w_down=ffn_w_down,
        norm_final=norm_final))

    bp, seq, _ = x_prompt.shape
    bs = x_sample.shape[0]

    zeros = dict(gla=jnp.zeros((bp, GLA_HEADS, GLA_DK, GLA_DV), F32),
                 s5_re=jnp.zeros((bp, S5_N), F32), s5_im=jnp.zeros((bp, S5_N), F32),
                 h=jnp.zeros((bp, RNN_W), F32), rc=jnp.zeros(((RNN_CONV - 1) * bp, RNN_W), F32),
                 fc=[jnp.zeros(((FFN_CONV - 1) * bp, D_FF), F32)] * 2)
    x_meta = jnp.repeat(meta_tokens.astype(F32), bp, axis=0)
    _, st = _trunk(x_meta, w, zeros, bp, N_META)

    yp, st = _trunk(x_prompt, w, st, bp, seq)

    st_s = dict(gla=state_gla, s5_re=state_s5_re.reshape(bs, S5_N), s5_im=state_s5_im.reshape(bs, S5_N),
                h=state_rglru, rc=_time_major(cache_rglru_conv),
                fc=[_time_major(cache_ffn_conv[l]) for l in range(2)])
    ys, ss = _trunk(x_sample.reshape(bs, D_MODEL), w, st_s, bs, 1)
    ys = ys.reshape(bs, 1, D_MODEL)

    grp = lambda z, nb: z.reshape(nb, S5_GROUPS, S5_P)
    return (yp, ys, st["gla"], ss["gla"], grp(st["s5_re"], bp), grp(ss["s5_re"], bs),
            grp(st["s5_im"], bp), grp(ss["s5_im"], bs), st["h"], ss["h"],
            _batch_major(st["rc"], bp), _batch_major(ss["rc"], bs),
            jnp.stack([_batch_major(st["fc"][l], bp) for l in range(2)]),
            jnp.stack([_batch_major(ss["fc"][l], bs) for l in range(2)]))
```

```python
import functools
import math

import jax
import jax.numpy as jnp
from jax import lax
from jax.experimental import pallas as pl
from jax.experimental.pallas import tpu as pltpu

F32 = jnp.float32
BF16 = jnp.bfloat16

D_MODEL = 1024
N_META = 16
EPS = 1e-6
F32_TINY = 1.1754944e-38
GLA_HEADS = 4
GLA_DK = 64
GLA_DV = 128
GLA_RANK = 16
GLA_TAU = 16.0
GLA_K = GLA_HEADS * GLA_DK
GLA_V = GLA_HEADS * GLA_DV
S5_GROUPS = 32
S5_H = 16
S5_P = 64
S5_W = S5_GROUPS * S5_H
S5_N = S5_GROUPS * S5_P
RNN_W = 1536
RNN_BLOCKS = 16
RNN_BW = RNN_W // RNN_BLOCKS
RNN_C = 8.0
RNN_CONV = 4
D_FF = 2816
FFN_CONV = 3

LANES = 128
SUBLANES = 8
FF_CHUNK = 256
L1_SPLIT = 2
S5_PACK = 8
GATE_WIN = 768
GATE_K = 512
GATE_N = 256
VMEM_LIMIT = 56 * 1024 * 1024


def _rms(x, g):
    return x * lax.rsqrt(jnp.mean(x * x, axis=-1, keepdims=True) + EPS) * g


def _sigmoid(x):
    return 0.5 * jnp.tanh(0.5 * x) + 0.5


def _sqrt_nonneg(t):
    return t * lax.rsqrt(jnp.maximum(t, F32_TINY))


def _mm(a, w):
    return jnp.dot(a.astype(BF16), w, preferred_element_type=F32)


def _const_spec(shape):
    nd = len(shape)
    return pl.BlockSpec(shape, lambda i, _n=nd: (0,) * _n, pipeline_mode=pl.Buffered(1))


def _row_spec(rows, cols):
    return pl.BlockSpec((rows, cols), lambda i: (i, 0))


def _seq_spec(nb, tt, cols):
    return pl.BlockSpec((nb, tt, cols), lambda i: (0, i, 0))


def _tm_scratch(rows, cols):
    return pltpu.VMEM((cols // LANES, rows, LANES), F32)


def _load_time_major(x_ref, tm_sc):
    if tm_sc is None:
        return x_ref[...]
    nb, tt, cols = x_ref.shape
    for b in range(nb):
        for j in range(cols // LANES):
            tm_sc[j, pl.ds(b, tt, stride=nb), :] = x_ref[b, :, j * LANES:(j + 1) * LANES]
    return jnp.concatenate([tm_sc[j] for j in range(cols // LANES)], axis=-1)


def _store_time_major(o_ref, val, tm_sc):
    if tm_sc is None:
        o_ref[...] = val
        return
    nb, tt, cols = o_ref.shape
    for j in range(cols // LANES):
        tm_sc[j] = val[:, j * LANES:(j + 1) * LANES]
    for b in range(nb):
        for j in range(cols // LANES):
            o_ref[b, :, j * LANES:(j + 1) * LANES] = tm_sc[j, pl.ds(b, tt, stride=nb), :]


def _params(sem="arbitrary"):
    return pltpu.CompilerParams(dimension_semantics=(sem,), vmem_limit_bytes=VMEM_LIMIT)


def _s5_prep_kernel(lr_ref, li_ref, ldt_ref, brt_ref, bit_ref, are_ref, aim_ref, bbre_ref, bbim_ref):
    lr = lr_ref[...]
    li = li_ref[...]
    dt = jnp.exp(ldt_ref[...])
    mag = jnp.exp(lr * dt)
    ab_re = mag * jnp.cos(li * dt)
    ab_im = mag * jnp.sin(li * dt)
    den = lr * lr + li * li
    nr = ab_re - 1.0
    ni = ab_im
    f_re = (nr * lr + ni * li) / den
    f_im = (ni * lr - nr * li) / den
    are_ref[...] = ab_re
    aim_ref[...] = ab_im
    brt = brt_ref[...]
    bit = bit_ref[...]
    bbre_ref[...] = f_re[:, None, :] * brt - f_im[:, None, :] * bit
    bbim_ref[...] = f_re[:, None, :] * bit + f_im[:, None, :] * brt


def _s5_prep(lam_re, lam_im, log_dt, b_re, b_im):
    g, p, h = b_re.shape
    brt = jnp.transpose(b_re, (0, 2, 1))
    bit = jnp.transpose(b_im, (0, 2, 1))
    return pl.pallas_call(
        _s5_prep_kernel,
        out_shape=(jax.ShapeDtypeStruct((g, p), F32), jax.ShapeDtypeStruct((g, p), F32),
                   jax.ShapeDtypeStruct((g, h, p), F32), jax.ShapeDtypeStruct((g, h, p), F32)),
        name="s5_prep",
    )(lam_re, lam_im, log_dt.reshape(g, 1), brt, bit)


def _l0_in_kernel(x_ref, g_ref, wq_ref, wk_ref, wv_ref, wg_ref, wu_ref, wlr_ref, wal_ref, bal_ref,
                  q_ref, k_ref, v_ref, gs_ref, la_ref, u_ref, *tm_sc):
    xn = _rms(_load_time_major(x_ref, tm_sc[0] if tm_sc else None), g_ref[...]).astype(BF16)
    q_ref[...] = _mm(xn, wq_ref[...]) * (GLA_DK ** -0.5)
    k_ref[...] = _mm(xn, wk_ref[...])
    v_ref[...] = _mm(xn, wv_ref[...])
    g = _mm(xn, wg_ref[...])
    gs_ref[...] = g * _sigmoid(g)
    u_ref[...] = _mm(xn, wu_ref[...])
    lr = _mm(xn, wlr_ref[...])
    pre = _mm(lr, wal_ref[...]) + bal_ref[...]
    la_ref[...] = jax.nn.log_sigmoid(pre) * (1.0 / GLA_TAU)


def _l0_in(x, w, nb, tt):
    rows = nb * tt
    batch_major = x.ndim == 3
    n = x.shape[0] * x.shape[1] if batch_major else x.shape[0]
    outs = [GLA_K, GLA_K, GLA_V, GLA_V, GLA_K, S5_W]
    return pl.pallas_call(
        _l0_in_kernel,
        grid=(n // rows,),
        in_specs=[_seq_spec(nb, tt, D_MODEL) if batch_major else _row_spec(rows, D_MODEL),
                  _const_spec((1, D_MODEL)),
                  _const_spec((D_MODEL, GLA_K)), _const_spec((D_MODEL, GLA_K)),
                  _const_spec((D_MODEL, GLA_V)), _const_spec((D_MODEL, GLA_V)),
                  _const_spec((D_MODEL, S5_W)), _const_spec((D_MODEL, LANES)),
                  _const_spec((LANES, GLA_K)), _const_spec((1, GLA_K))],
        out_specs=[_row_spec(rows, c) for c in outs],
        out_shape=[jax.ShapeDtypeStruct((n, c), F32) for c in outs],
        scratch_shapes=[_tm_scratch(rows, D_MODEL)] if batch_major else [],
        compiler_params=_params("parallel"),
        name="l0_in",
    )(x, w["norm_mix_0"], w["w_q"], w["w_k"], w["w_v"], w["w_g"], w["w_u"], w["w_lr"], w["w_alpha"],
      w["b_alpha"])


def _gla_chunk_kernel(q_ref, k_ref, v0_ref, v1_ref, v2_ref, v3_ref, la_ref, gs_ref, gn_ref, s0_ref,
                      o_ref, s_ref, b_sc, gam_sc, qt_sc, kt_sc, kh_sc, oo_sc, *, nb, c):
    v_refs = (v0_ref, v1_ref, v2_ref, v3_ref)

    @pl.when(pl.program_id(0) == 0)
    def _():
        s_ref[...] = s0_ref[...]

    def cum_body(t, run):
        rows = pl.ds(pl.multiple_of(t * nb, nb), nb)
        run = run + la_ref[rows, :]
        b_sc[rows, :] = run
        return run

    bl = lax.fori_loop(0, c, cum_body, jnp.zeros((nb, GLA_K), F32))
    gam = jnp.exp(bl)
    for b in range(nb):
        gam_sc[b] = jnp.broadcast_to(gam[b:b + 1, :], gam_sc.shape[1:])

    def scale_body(t, carry):
        rows = pl.ds(pl.multiple_of(t * nb, nb), nb)
        b = b_sc[rows, :]
        k = k_ref[rows, :]
        qt = q_ref[rows, :] * jnp.exp(b)
        kt = k * jnp.exp(-b)
        kh = k * jnp.exp(bl - b)
        for j in range(GLA_K // LANES):
            ls = slice(j * LANES, (j + 1) * LANES)
            qt_sc[j, rows, :] = qt[:, ls]
            kt_sc[j, rows, :] = kt[:, ls]
            kh_sc[j, rows, :] = kh[:, ls]
        return carry

    lax.fori_loop(0, c, scale_body, 0)

    causal = lax.broadcasted_iota(jnp.int32, (c, c), 0) >= lax.broadcasted_iota(jnp.int32, (c, c), 1)
    eye = lax.broadcasted_iota(jnp.int32, (GLA_DK, GLA_DK), 0) == lax.broadcasted_iota(
        jnp.int32, (GLA_DK, GLA_DK), 1)
    heads_per_buf = LANES // GLA_DK

    def seq_body(bi, carry):
        rows = pl.ds(bi, c, stride=nb)
        for h in range(GLA_HEADS):
            j = h // heads_per_buf
            ks = slice((h % heads_per_buf) * GLA_DK, (h % heads_per_buf + 1) * GLA_DK)
            qb = qt_sc[j, rows, :][:, ks].astype(BF16)
            kb = kt_sc[j, rows, :][:, ks].astype(BF16)
            khb = kh_sc[j, rows, :][:, ks].astype(BF16)
            vb = v_refs[h][rows, :].astype(BF16)
            att = lax.dot_general(qb, kb, (((1,), (1,)), ((), ())), preferred_element_type=F32)
            att = jnp.where(causal, att, 0.0)
            s = s_ref[bi * GLA_HEADS + h]
            o = (jnp.dot(att.astype(BF16), vb, preferred_element_type=F32)
                 + jnp.dot(qb, s.astype(BF16), preferred_element_type=F32))
            oo_sc[h, rows, :] = o
            gam_row = gam_sc[bi][0:1, h * GLA_DK:(h + 1) * GLA_DK]
            gam_col = jnp.sum(jnp.where(eye, gam_row, 0.0), axis=1, keepdims=True)
            s_ref[bi * GLA_HEADS + h] = gam_col * s + lax.dot_general(
                khb, vb, (((0,), (0,)), ((), ())), preferred_element_type=F32)
        return carry

    lax.fori_loop(0, nb, seq_body, 0)

    for h in range(GLA_HEADS):
        vs = slice(h * GLA_DV, (h + 1) * GLA_DV)
        o_ref[:, vs] = _rms(oo_sc[h], gn_ref[:, vs]) * gs_ref[:, vs]


def _gla_chunk(q, k, v, la, gs, gn, s0, nb, c):
    n = q.shape[0]
    rows = nb * c
    sshape = (nb * GLA_HEADS, GLA_DK, GLA_DV)
    kern = functools.partial(_gla_chunk_kernel, nb=nb, c=c)
    v_specs = [pl.BlockSpec((rows, GLA_DV), lambda i, _h=h: (i, _h)) for h in range(GLA_HEADS)]
    pair = pltpu.VMEM((GLA_K // LANES, rows, LANES), F32)
    return pl.pallas_call(
        kern,
        grid=(n // rows,),
        in_specs=[_row_spec(rows, GLA_K), _row_spec(rows, GLA_K)] + v_specs
                 + [_row_spec(rows, GLA_K), _row_spec(rows, GLA_V), _const_spec((1, GLA_V)),
                    _const_spec(sshape)],
        out_specs=[_row_spec(rows, GLA_V), pl.BlockSpec(sshape, lambda i: (0, 0, 0))],
        out_shape=[jax.ShapeDtypeStruct((n, GLA_V), F32), jax.ShapeDtypeStruct(sshape, F32)],
        scratch_shapes=[pltpu.VMEM((rows, GLA_K), F32), pltpu.VMEM((nb, SUBLANES, GLA_K), F32), pair, pair, pair,
                        pltpu.VMEM((GLA_HEADS, rows, GLA_DV), F32)],
        compiler_params=_params(),
        name="gla_chunk",
    )(q, k, v, v, v, v, la, gs, gn, s0.reshape(sshape))


def _gla_step_kernel(q_ref, k_ref, la_ref, v_ref, gs_ref, gn_ref, s0_ref, o_ref, s_ref):
    q = q_ref[...]
    k = k_ref[...]
    a = jnp.exp(la_ref[...])
    v = v_ref[...]
    o = jnp.zeros_like(v)
    for d in range(GLA_DK):
        s_new = a[:, d:d + 1] * s0_ref[:, d, :] + k[:, d:d + 1] * v
        s_ref[:, d, :] = s_new
        o = o + q[:, d:d + 1] * s_new
    o_ref[...] = _rms(o, gn_ref[...]) * gs_ref[...]


def _gla_step(q, k, v, la, gs, gn, s0):
    nb = q.shape[0]
    heads = lambda z: jnp.transpose(z.reshape(nb, GLA_HEADS, GLA_DK), (1, 0, 2))
    hspec = pl.BlockSpec((None, nb, GLA_DK), lambda h: (h, 0, 0))
    vspec = pl.BlockSpec((nb, GLA_DV), lambda h: (0, h))
    sspec = pl.BlockSpec((nb, None, GLA_DK, GLA_DV), lambda h: (0, h, 0, 0))
    return pl.pallas_call(
        _gla_step_kernel,
        grid=(GLA_HEADS,),
        in_specs=[hspec, hspec, hspec, vspec, vspec, pl.BlockSpec((1, GLA_DV), lambda h: (0, h)), sspec],
        out_specs=[vspec, sspec],
        out_shape=[jax.ShapeDtypeStruct((nb, GLA_V), F32),
                   jax.ShapeDtypeStruct((nb, GLA_HEADS, GLA_DK, GLA_DV), F32)],
        compiler_params=_params("parallel"),
        name="gla_step",
    )(heads(q), heads(k), heads(la), v, gs, gn, s0)


def _l0_out_kernel(u_ref, og_ref, x_ref, bbre_ref, bbim_ref, are_ref, aim_ref, cre_ref, cim_ref, d_ref,
                   wglu_ref, bglu_ref, woa_ref, wob_ref, xr0_ref, xi0_ref,
                   xo_ref, xre_ref, xim_ref, sr_sc, si_sc, *tm_sc, nb, tt):
    @pl.when(pl.program_id(0) == 0)
    def _():
        xre_ref[...] = xr0_ref[...]
        xim_ref[...] = xi0_ref[...]

    u = u_ref[...]
    ub = u.astype(BF16)
    kin = S5_PACK * S5_H
    kst = S5_PACK * S5_P
    npack = S5_GROUPS // S5_PACK
    ys = []
    for j in range(npack):
        cs = slice(j * kst, (j + 1) * kst)
        uj = ub[:, j * kin:(j + 1) * kin]
        bur = jnp.dot(uj, bbre_ref[j], preferred_element_type=F32)
        bui = jnp.dot(uj, bbim_ref[j], preferred_element_type=F32)
        ar = jnp.broadcast_to(are_ref[:, cs], (nb, kst))
        ai = jnp.broadcast_to(aim_ref[:, cs], (nb, kst))
        xr = xre_ref[:, cs]
        xi = xim_ref[:, cs]
        for t in range(tt):
            rows = slice(t * nb, (t + 1) * nb)
            xr, xi = ar * xr - ai * xi + bur[rows], ar * xi + ai * xr + bui[rows]
            sr_sc[rows, cs] = xr
            si_sc[rows, cs] = xi
        xre_ref[:, cs] = xr
        xim_ref[:, cs] = xi
        ys.append(jnp.dot(sr_sc[:, cs].astype(BF16), cre_ref[j], preferred_element_type=F32)
                  - jnp.dot(si_sc[:, cs].astype(BF16), cim_ref[j], preferred_element_type=F32))
    y = jnp.concatenate(ys, axis=-1) + d_ref[...] * u
    y = jax.nn.gelu(y)
    y = y * _sigmoid(_mm(y, wglu_ref[...]) + bglu_ref[...])
    x = _load_time_major(x_ref, tm_sc[0] if tm_sc else None)
    xo_ref[...] = x + _mm(og_ref[...], woa_ref[...]) + _mm(y, wob_ref[...])


def _l0_out(u, og, x, w, xr0, xi0, nb, tt):
    n = u.shape[0]
    rows = nb * tt
    batch_major = x.ndim == 3
    npack = S5_GROUPS // S5_PACK
    kin, kst = S5_PACK * S5_H, S5_PACK * S5_P
    kern = functools.partial(_l0_out_kernel, nb=nb, tt=tt)
    st_spec = pl.BlockSpec((nb, S5_N), lambda i: (0, 0))
    return pl.pallas_call(
        kern,
        grid=(n // rows,),
        in_specs=[_row_spec(rows, S5_W), _row_spec(rows, GLA_V),
                  _seq_spec(nb, tt, D_MODEL) if batch_major else _row_spec(rows, D_MODEL),
                  _const_spec((npack, kin, kst)), _const_spec((npack, kin, kst)),
                  _const_spec((1, S5_N)), _const_spec((1, S5_N)),
                  _const_spec((npack, kst, kin)), _const_spec((npack, kst, kin)),
                  _const_spec((1, S5_W)), _const_spec((S5_W, S5_W)), _const_spec((1, S5_W)),
                  _const_spec((GLA_V, D_MODEL)), _const_spec((S5_W, D_MODEL)),
                  _const_spec((nb, S5_N)), _const_spec((nb, S5_N))],
        out_specs=[_row_spec(rows, D_MODEL), st_spec, st_spec],
        out_shape=[jax.ShapeDtypeStruct((n, D_MODEL), F32), jax.ShapeDtypeStruct((nb, S5_N), F32),
                   jax.ShapeDtypeStruct((nb, S5_N), F32)],
        scratch_shapes=[pltpu.VMEM((rows, S5_N), F32)] * 2
                       + ([_tm_scratch(rows, D_MODEL)] if batch_major else []),
        compiler_params=_params(),
        name="l0_out",
    )(u, og, x, w["s5_bbre"], w["s5_bbim"], w["s5_are"], w["s5_aim"], w["s5_cre"], w["s5_cim"],
      w["s5_d"], w["s5_w_glu"], w["s5_b_glu"], w["w_out_0a"], w["w_out_0b"], xr0, xi0)


def _ffn_kernel(x_ref, g_ref, wg_ref, wv_ref, cw_ref, cb_ref, wd_ref, c0_ref, gf_ref, xo_ref, c_ref,
                hm_sc, *tm_sc, nb, tt, final):
    @pl.when(pl.program_id(0) == 0)
    def _():
        c_ref[...] = c0_ref[...]

    rows = nb * tt
    x = x_ref[...]
    xn = _rms(x, g_ref[...]).astype(BF16)
    for ci in range(D_FF // FF_CHUNK):
        cs = slice(ci * FF_CHUNK, (ci + 1) * FF_CHUNK)
        gate = jnp.dot(xn, wg_ref[:, cs], preferred_element_type=F32)
        val = jnp.dot(xn, wv_ref[:, cs], preferred_element_type=F32)
        ext = jnp.concatenate([c_ref[:, cs], gate], axis=0)
        y = cb_ref[:, cs] + ext[0:rows] * cw_ref[0:1, cs]
        y = y + ext[nb:nb + rows] * cw_ref[1:2, cs]
        y = y + ext[2 * nb:2 * nb + rows] * cw_ref[2:3, cs]
        c_ref[:, cs] = ext[tt * nb:(tt + 2) * nb]
        hm_sc[:, cs] = (jax.nn.gelu(y) * val).astype(BF16)
    out = x + jnp.dot(hm_sc[...], wd_ref[...], preferred_element_type=F32)
    if final:
        out = _rms(out, gf_ref[...])
    _store_time_major(xo_ref, out, tm_sc[0] if tm_sc else None)


def _ffn(x, w, layer, c0, nb, tt, final, batch_major_out=False):
    n = x.shape[0]
    rows = nb * tt
    kern = functools.partial(_ffn_kernel, nb=nb, tt=tt, final=final)
    if batch_major_out:
        o_spec, o_shape = _seq_spec(nb, tt, D_MODEL), (nb, n // nb, D_MODEL)
    else:
        o_spec, o_shape = _row_spec(rows, D_MODEL), (n, D_MODEL)
    cshape = ((FFN_CONV - 1) * nb, D_FF)
    return pl.pallas_call(
        kern,
        grid=(n // rows,),
        in_specs=[_row_spec(rows, D_MODEL), _const_spec((1, D_MODEL)),
                  _const_spec((D_MODEL, D_FF)), _const_spec((D_MODEL, D_FF)),
                  _const_spec((FFN_CONV, D_FF)), _const_spec((1, D_FF)),
                  _const_spec((D_FF, D_MODEL)), _const_spec(cshape), _const_spec((1, D_MODEL))],
        out_specs=[o_spec, pl.BlockSpec(cshape, lambda i: (0, 0))],
        out_shape=[jax.ShapeDtypeStruct(o_shape, F32), jax.ShapeDtypeStruct(cshape, F32)],
        scratch_shapes=[pltpu.VMEM((rows, D_FF), BF16)]
                       + ([_tm_scratch(rows, D_MODEL)] if batch_major_out else []),
        compiler_params=_params(),
        name="ffn%d" % layer,
    )(x, w["norm_ffn"][layer], w["ffn_w_gate"][layer], w["ffn_w_val"][layer], w["ffn_conv_w"][layer],
      w["ffn_conv_b"][layer], w["ffn_w_down"][layer], c0, w["norm_final"])


def _l1_kernel(x_ref, g_ref, wgt_ref, wxr_ref, cw_ref, cb_ref, wa_ref, ba_ref, wx_ref, bx_ref, lam_ref,
               wo_ref, h0_ref, c0_ref, xo_ref, h_ref, c_ref, a_sc, b_sc, *, nb, tt):
    @pl.when(pl.program_id(0) == 0)
    def _():
        h_ref[...] = h0_ref[...]
        c_ref[...] = c0_ref[...]

    sp = jax.nn.softplus(-lam_ref[...])
    nsplit = L1_SPLIT if tt % L1_SPLIT == 0 else 1
    th = tt // nsplit
    rows = nb * th
    carry = c_ref[...]
    h = h_ref[...]
    for part in range(nsplit):
        prow = slice(part * rows, (part + 1) * rows)
        x = x_ref[prow, :]
        xn = _rms(x, g_ref[...]).astype(BF16)
        xr = jnp.dot(xn, wxr_ref[...], preferred_element_type=F32)
        ext = jnp.concatenate([carry, xr], axis=0)
        xc = cb_ref[...] + ext[0:rows] * cw_ref[0:1, :]
        for j in range(1, RNN_CONV):
            xc = xc + ext[j * nb:j * nb + rows] * cw_ref[j:j + 1, :]
        carry = ext[th * nb:(th + RNN_CONV - 1) * nb]

        xcb = xc.astype(BF16)
        rs, gs = [], []
        for wi in range(RNN_W // GATE_WIN):
            for ni in range(GATE_WIN // GATE_N):
                k0 = wi * GATE_WIN + ni * LANES
                lhs = xcb[:, k0:k0 + GATE_K]
                rs.append(jnp.dot(lhs, wa_ref[wi, ni], preferred_element_type=F32))
                gs.append(jnp.dot(lhs, wx_ref[wi, ni], preferred_element_type=F32))
        r = _sigmoid(jnp.concatenate(rs, axis=-1) + ba_ref[...])
        ig = _sigmoid(jnp.concatenate(gs, axis=-1) + bx_ref[...])
        log_a = (-RNN_C) * r * sp
        a = jnp.exp(log_a)
        a_sc[prow, :] = a
        b_sc[prow, :] = _sqrt_nonneg(jnp.tanh(-log_a) * (a * a + 1.0)) * (ig * xc)
        gg = jax.nn.gelu(jnp.dot(xn, wgt_ref[...], preferred_element_type=F32))

        for t in range(part * th, (part + 1) * th):
            rws = slice(t * nb, (t + 1) * nb)
            h = a_sc[rws, :] * h + b_sc[rws, :]
            b_sc[rws, :] = h
        xo_ref[prow, :] = x + _mm(b_sc[prow, :] * gg, wo_ref[...])
    c_ref[...] = carry
    h_ref[...] = h


def _l1(x, w, h0, c0, nb, tt):
    n = x.shape[0]
    rows = nb * tt
    kern = functools.partial(_l1_kernel, nb=nb, tt=tt)
    cshape = ((RNN_CONV - 1) * nb, RNN_W)
    gshape = (RNN_W // GATE_WIN, GATE_WIN // GATE_N, GATE_K, GATE_N)
    return pl.pallas_call(
        kern,
        grid=(n // rows,),
        in_specs=[_row_spec(rows, D_MODEL), _const_spec((1, D_MODEL)),
                  _const_spec((D_MODEL, RNN_W)), _const_spec((D_MODEL, RNN_W)),
                  _const_spec((RNN_CONV, RNN_W)), _const_spec((1, RNN_W)),
                  _const_spec(gshape), _const_spec((1, RNN_W)),
                  _const_spec(gshape), _const_spec((1, RNN_W)),
                  _const_spec((1, RNN_W)), _const_spec((RNN_W, D_MODEL)),
                  _const_spec((nb, RNN_W)), _const_spec(cshape)],
        out_specs=[_row_spec(rows, D_MODEL), pl.BlockSpec((nb, RNN_W), lambda i: (0, 0)),
                   pl.BlockSpec(cshape, lambda i: (0, 0))],
        out_shape=[jax.ShapeDtypeStruct((n, D_MODEL), F32), jax.ShapeDtypeStruct((nb, RNN_W), F32),
                   jax.ShapeDtypeStruct(cshape, F32)],
        scratch_shapes=[pltpu.VMEM((rows, RNN_W), F32)] * 2,
        compiler_params=_params(),
        name="l1_mixer",
    )(x, w["norm_mix_1"], w["w_gate_1"], w["w_xr_1"], w["rnn_conv_w"], w["rnn_conv_b"],
      w["rnn_wa"], w["rnn_b_a"], w["rnn_wx"], w["rnn_b_x"], w["rnn_lam"], w["w_out_1"], h0, c0)


def _pack_gate(wblk):
    dense = jax.scipy.linalg.block_diag(*[wblk[i] for i in range(RNN_BLOCKS)])
    tiles = []
    for wi in range(RNN_W // GATE_WIN):
        row = []
        for ni in range(GATE_WIN // GATE_N):
            k0 = wi * GATE_WIN + ni * LANES
            n0 = wi * GATE_WIN + ni * GATE_N
            row.append(dense[k0:k0 + GATE_K, n0:n0 + GATE_N])
        tiles.append(jnp.stack(row))
    return jnp.stack(tiles).astype(BF16)


def _prep_weights(p):
    w = {}
    row = lambda v: v.reshape(1, -1).astype(F32)
    w_in = p["w_in_0"]
    c = 0
    for name, width in (("w_q", GLA_K), ("w_k", GLA_K), ("w_v", GLA_V), ("w_g", GLA_V)):
        w[name] = w_in[:, c:c + width].astype(BF16)
        c += width
    w["w_lr"] = jnp.pad(w_in[:, c:c + GLA_RANK], ((0, 0), (0, LANES - GLA_RANK))).astype(BF16)
    c += GLA_RANK
    w["w_u"] = w_in[:, c:c + S5_W].astype(BF16)
    w["w_alpha"] = jnp.pad(p["w_alpha_0"], ((0, LANES - GLA_RANK), (0, 0))).astype(BF16)
    w["b_alpha"] = row(p["b_alpha_0"])
    w["norm_mix_0"] = row(p["norm_mix_0"])
    w["gla_norm"] = row(p["gla_norm_0"])

    are, aim, bbre, bbim = _s5_prep(p["s5_lam_re"], p["s5_lam_im"], p["s5_log_dt"], p["s5_b_re"],
                                    p["s5_b_im"])
    npack = S5_GROUPS // S5_PACK
    eye = jnp.eye(S5_PACK, dtype=F32)
    pack_b = lambda m: jnp.einsum("jghp,gk->jghkp", m.reshape(npack, S5_PACK, S5_H, S5_P), eye).reshape(
        npack, S5_PACK * S5_H, S5_PACK * S5_P).astype(BF16)
    pack_c = lambda m: jnp.einsum("jghp,gk->jgpkh", m.reshape(npack, S5_PACK, S5_H, S5_P), eye).reshape(
        npack, S5_PACK * S5_P, S5_PACK * S5_H).astype(BF16)
    w["s5_are"] = are.reshape(1, S5_N)
    w["s5_aim"] = aim.reshape(1, S5_N)
    w["s5_bbre"] = pack_b(bbre)
    w["s5_bbim"] = pack_b(bbim)
    w["s5_cre"] = pack_c(p["s5_c_re"])
    w["s5_cim"] = pack_c(p["s5_c_im"])
    w["s5_d"] = row(p["s5_d"])
    w["s5_w_glu"] = p["s5_w_glu"].astype(BF16)
    w["s5_b_glu"] = row(p["s5_b_glu"])
    w["w_out_0a"] = p["w_out_0"][:GLA_V].astype(BF16)
    w["w_out_0b"] = p["w_out_0"][GLA_V:].astype(BF16)

    w["norm_mix_1"] = row(p["norm_mix_1"])
    w["w_gate_1"] = p["w_in_1"][:, :RNN_W].astype(BF16)
    w["w_xr_1"] = p["w_in_1"][:, RNN_W:].astype(BF16)
    w["rnn_conv_w"] = p["rnn_conv_w"].astype(F32)
    w["rnn_conv_b"] = row(p["rnn_conv_b"])
    w["rnn_wa"] = _pack_gate(p["rnn_w_a"])
    w["rnn_wx"] = _pack_gate(p["rnn_w_x"])
    w["rnn_b_a"] = row(p["rnn_b_a"])
    w["rnn_b_x"] = row(p["rnn_b_x"])
    w["rnn_lam"] = row(p["rnn_lam"])
    w["w_out_1"] = p["w_out_1"].astype(BF16)

    w["norm_ffn"] = [row(p["norm_ffn"][l]) for l in range(2)]
    w["ffn_w_gate"] = [p["ffn_w_up"][l, :, :D_FF].astype(BF16) for l in range(2)]
    w["ffn_w_val"] = [p["ffn_w_up"][l, :, D_FF:].astype(BF16) for l in range(2)]
    w["ffn_conv_w"] = [p["ffn_conv_w"][l].astype(F32) for l in range(2)]
    w["ffn_conv_b"] = [row(p["ffn_conv_b"][l]) for l in range(2)]
    w["ffn_w_down"] = [p["ffn_w_down"][l].astype(BF16) for l in range(2)]
    w["norm_final"] = row(p["norm_final"])
    return w


def _tile_steps(nt):
    cap = lambda m: min(nt, m)
    return dict(l0_in=cap(64), gla=cap(64), l0_out=cap(64), ffn=cap(64), l1=cap(64))


def _trunk(x, w, st, nb, nt):
    tt = _tile_steps(nt)
    batch_major = x.ndim == 3
    q, k, v, gs, la, u = _l0_in(x, w, nb, tt["l0_in"])
    if nt == 1:
        og, s_gla = _gla_step(q, k, v, la, gs, w["gla_norm"], st["gla"])
    else:
        og, s_gla = _gla_chunk(q, k, v, la, gs, w["gla_norm"], st["gla"], nb, tt["gla"])
        s_gla = s_gla.reshape(nb, GLA_HEADS, GLA_DK, GLA_DV)
    x, s_re, s_im = _l0_out(u, og, x, w, st["s5_re"], st["s5_im"], nb, tt["l0_out"])
    x, fc0 = _ffn(x, w, 0, st["fc"][0], nb, tt["ffn"], False)
    x, h, rc = _l1(x, w, st["h"], st["rc"], nb, tt["l1"])
    x, fc1 = _ffn(x, w, 1, st["fc"][1], nb, tt["ffn"], True, batch_major_out=batch_major)
    return x, dict(gla=s_gla, s5_re=s_re, s5_im=s_im, h=h, rc=rc, fc=[fc0, fc1])


def _time_major(cache):
    b, j, c = cache.shape
    return jnp.transpose(cache, (1, 0, 2)).reshape(j * b, c)


def _batch_major(cache, nb):
    jb, c = cache.shape
    return jnp.transpose(cache.reshape(jb // nb, nb, c), (1, 0, 2))


def kernel(x_prompt, x_sample, state_gla, state_s5_re, state_s5_im, state_rglru, cache_rglru_conv,
           cache_ffn_conv, meta_tokens, norm_mix_0, w_in_0, w_alpha_0, b_alpha_0, gla_norm_0,
           s5_lam_re, s5_lam_im, s5_log_dt, s5_b_re, s5_b_im, s5_c_re, s5_c_im, s5_d, s5_w_glu,
           s5_b_glu, w_out_0, norm_mix_1, w_in_1, rnn_conv_w, rnn_conv_b, rnn_w_a, rnn_b_a, rnn_w_x,
           rnn_b_x, rnn_lam, w_out_1, norm_ffn, ffn_w_up, ffn_conv_w, ffn_conv_b, ffn_w_down, norm_final):
    w = _prep_weights(dict(
        norm_mix_0=norm_mix_0, w_in_0=w_in_0, w_alpha_0=w_alpha_0, b_alpha_0=b_alpha_0,
        gla_norm_0=gla_norm_0, s5_lam_re=s5_lam_re, s5_lam_im=s5_lam_im, s5_log_dt=s5_log_dt,
        s5_b_re=s5_b_re, s5_b_im=s5_b_im, s5_c_re=s5_c_re, s5_c_im=s5_c_im, s5_d=s5_d,
        s5_w_glu=s5_w_glu, s5_b_glu=s5_b_glu, w_out_0=w_out_0, norm_mix_1=norm_mix_1, w_in_1=w_in_1,
        rnn_conv_w=rnn_conv_w, rnn_conv_b=rnn_conv_b, rnn_w_a=rnn_w_a, rnn_b_a=rnn_b_a,
        rnn_w_x=rnn_w_x, rnn_b_x=rnn_b_x, rnn_lam=rnn_lam, w_out_1=w_out_1, norm_ffn=norm_ffn,
        ffn_w_up=ffn_w_up, ffn_conv_w=ffn_conv_w, ffn_conv_b=ffn_conv_b, ffn_w_down=ffn_w_down,
        norm_final=norm_final))

    bp, seq, _ = x_prompt.shape
    bs = x_sample.shape[0]

    zeros = dict(gla=jnp.zeros((bp, GLA_HEADS, GLA_DK, GLA_DV), F32),
                 s5_re=jnp.zeros((bp, S5_N), F32), s5_im=jnp.zeros((bp, S5_N), F32),
                 h=jnp.zeros((bp, RNN_W), F32), rc=jnp.zeros(((RNN_CONV - 1) * bp, RNN_W), F32),
                 fc=[jnp.zeros(((FFN_CONV - 1) * bp, D_FF), F32)] * 2)
    x_meta = jnp.repeat(meta_tokens.astype(F32), bp, axis=0)
    _, st = _trunk(x_meta, w, zeros, bp, N_META)

    yp, st = _trunk(x_prompt, w, st, bp, seq)

    st_s = dict(gla=state_gla, s5_re=state_s5_re.reshape(bs, S5_N), s5_im=state_s5_im.reshape(bs, S5_N),
                h=state_rglru, rc=_time_major(cache_rglru_conv),
                fc=[_time_major(cache_ffn_conv[l]) for l in range(2)])
    ys, ss = _trunk(x_sample.reshape(bs, D_MODEL), w, st_s, bs, 1)
    ys = ys.reshape(bs, 1, D_MODEL)

    grp = lambda z, nb: z.reshape(nb, S5_GROUPS, S5_P)
    return (yp, ys, st["gla"], ss["gla"], grp(st["s5_re"], bp), grp(ss["s5_re"], bs),
            grp(st["s5_im"], bp), grp(ss["s5_im"], bs), st["h"], ss["h"],
            _batch_major(st["rc"], bp), _batch_major(ss["rc"], bs),
            jnp.stack([_batch_major(st["fc"][l], bp) for l in range(2)]),
            jnp.stack([_batch_major(ss["fc"][l], bs) for l in range(2)]))
```

```python
import functools
import math

import jax
import jax.numpy as jnp
from jax import lax
from jax.experimental import pallas as pl
from jax.experimental.pallas import tpu as pltpu

F32 = jnp.float32
BF16 = jnp.bfloat16

D_MODEL = 1024
N_META = 16
EPS = 1e-6
F32_TINY = 1.1754944e-38
GLA_HEADS = 4
GLA_DK = 64
GLA_DV = 128
GLA_RANK = 16
GLA_TAU = 16.0
GLA_K = GLA_HEADS * GLA_DK
GLA_V = GLA_HEADS * GLA_DV
S5_GROUPS = 32
S5_H = 16
S5_P = 64
S5_W = S5_GROUPS * S5_H
S5_N = S5_GROUPS * S5_P
RNN_W = 1536
RNN_BLOCKS = 16
RNN_BW = RNN_W // RNN_BLOCKS
RNN_C = 8.0
RNN_CONV = 4
D_FF = 2816
FFN_CONV = 3

LANES = 128
FF_CHUNK = 256
L1_SPLIT = 2
S5_PACK = 8
GATE_WIN = 768
GATE_K = 512
GATE_N = 256
VMEM_LIMIT = 56 * 1024 * 1024


def _rms(x, g):
    return x * lax.rsqrt(jnp.mean(x * x, axis=-1, keepdims=True) + EPS) * g


def _sigmoid(x):
    return 0.5 * jnp.tanh(0.5 * x) + 0.5


def _sqrt_nonneg(t):
    return t * lax.rsqrt(jnp.maximum(t, F32_TINY))


def _mm(a, w):
    return jnp.dot(a.astype(BF16), w, preferred_element_type=F32)


def _const_spec(shape):
    nd = len(shape)
    return pl.BlockSpec(shape, lambda i, _n=nd: (0,) * _n, pipeline_mode=pl.Buffered(1))


def _row_spec(rows, cols):
    return pl.BlockSpec((rows, cols), lambda i: (i, 0))


def _seq_spec(nb, tt, cols):
    return pl.BlockSpec((nb, tt, cols), lambda i: (0, i, 0))


def _tm_scratch(rows, cols):
    return pltpu.VMEM((cols // LANES, rows, LANES), F32)


def _load_time_major(x_ref, tm_sc):
    if tm_sc is None:
        return x_ref[...]
    nb, tt, cols = x_ref.shape
    for b in range(nb):
        for j in range(cols // LANES):
            tm_sc[j, pl.ds(b, tt, stride=nb), :] = x_ref[b, :, j * LANES:(j + 1) * LANES]
    return jnp.concatenate([tm_sc[j] for j in range(cols // LANES)], axis=-1)


def _store_time_major(o_ref, val, tm_sc):
    if tm_sc is None:
        o_ref[...] = val
        return
    nb, tt, cols = o_ref.shape
    for j in range(cols // LANES):
        tm_sc[j] = val[:, j * LANES:(j + 1) * LANES]
    for b in range(nb):
        for j in range(cols // LANES):
            o_ref[b, :, j * LANES:(j + 1) * LANES] = tm_sc[j, pl.ds(b, tt, stride=nb), :]


def _params(sem="arbitrary"):
    return pltpu.CompilerParams(dimension_semantics=(sem,), vmem_limit_bytes=VMEM_LIMIT)


def _s5_prep_kernel(lr_ref, li_ref, ldt_ref, brt_ref, bit_ref, are_ref, aim_ref, bbre_ref, bbim_ref):
    lr = lr_ref[...]
    li = li_ref[...]
    dt = jnp.exp(ldt_ref[...])
    mag = jnp.exp(lr * dt)
    ab_re = mag * jnp.cos(li * dt)
    ab_im = mag * jnp.sin(li * dt)
    den = lr * lr + li * li
    nr = ab_re - 1.0
    ni = ab_im
    f_re = (nr * lr + ni * li) / den
    f_im = (ni * lr - nr * li) / den
    are_ref[...] = ab_re
    aim_ref[...] = ab_im
    brt = brt_ref[...]
    bit = bit_ref[...]
    bbre_ref[...] = f_re[:, None, :] * brt - f_im[:, None, :] * bit
    bbim_ref[...] = f_re[:, None, :] * bit + f_im[:, None, :] * brt


def _s5_prep(lam_re, lam_im, log_dt, b_re, b_im):
    g, p, h = b_re.shape
    brt = jnp.transpose(b_re, (0, 2, 1))
    bit = jnp.transpose(b_im, (0, 2, 1))
    return pl.pallas_call(
        _s5_prep_kernel,
        out_shape=(jax.ShapeDtypeStruct((g, p), F32), jax.ShapeDtypeStruct((g, p), F32),
                   jax.ShapeDtypeStruct((g, h, p), F32), jax.ShapeDtypeStruct((g, h, p), F32)),
        name="s5_prep",
    )(lam_re, lam_im, log_dt.reshape(g, 1), brt, bit)


def _l0_in_kernel(x_ref, g_ref, wq_ref, wk_ref, wv_ref, wg_ref, wu_ref, wlr_ref, wal_ref, bal_ref,
                  q_ref, k_ref, v_ref, gs_ref, la_ref, u_ref, *tm_sc):
    xn = _rms(_load_time_major(x_ref, tm_sc[0] if tm_sc else None), g_ref[...]).astype(BF16)
    q_ref[...] = _mm(xn, wq_ref[...]) * (GLA_DK ** -0.5)
    k_ref[...] = _mm(xn, wk_ref[...])
    v_ref[...] = _mm(xn, wv_ref[...])
    g = _mm(xn, wg_ref[...])
    gs_ref[...] = g * _sigmoid(g)
    u_ref[...] = _mm(xn, wu_ref[...])
    lr = _mm(xn, wlr_ref[...])
    pre = _mm(lr, wal_ref[...]) + bal_ref[...]
    la_ref[...] = jax.nn.log_sigmoid(pre) * (1.0 / GLA_TAU)


def _l0_in(x, w, nb, tt):
    rows = nb * tt
    batch_major = x.ndim == 3
    n = x.shape[0] * x.shape[1] if batch_major else x.shape[0]
    outs = [GLA_K, GLA_K, GLA_V, GLA_V, GLA_K, S5_W]
    return pl.pallas_call(
        _l0_in_kernel,
        grid=(n // rows,),
        in_specs=[_seq_spec(nb, tt, D_MODEL) if batch_major else _row_spec(rows, D_MODEL),
                  _const_spec((1, D_MODEL)),
                  _const_spec((D_MODEL, GLA_K)), _const_spec((D_MODEL, GLA_K)),
                  _const_spec((D_MODEL, GLA_V)), _const_spec((D_MODEL, GLA_V)),
                  _const_spec((D_MODEL, S5_W)), _const_spec((D_MODEL, LANES)),
                  _const_spec((LANES, GLA_K)), _const_spec((1, GLA_K))],
        out_specs=[_row_spec(rows, c) for c in outs],
        out_shape=[jax.ShapeDtypeStruct((n, c), F32) for c in outs],
        scratch_shapes=[_tm_scratch(rows, D_MODEL)] if batch_major else [],
        compiler_params=_params("parallel"),
        name="l0_in",
    )(x, w["norm_mix_0"], w["w_q"], w["w_k"], w["w_v"], w["w_g"], w["w_u"], w["w_lr"], w["w_alpha"],
      w["b_alpha"])


def _gla_chunk_kernel(q_ref, k_ref, v_ref, la_ref, gs_ref, gn_ref, st0_ref, o_ref, st_ref, b_sc, *, nb, c):
    rows = nb * c
    seq_mask = nb - 1

    @pl.when(pl.program_id(0) == 0)
    def _():
        st_ref[...] = st0_ref[...]

    def cum_body(t, run):
        rws = pl.ds(pl.multiple_of(t * nb, nb), nb)
        run = run + la_ref[rws, :]
        b_sc[rws, :] = run
        return run

    bl = lax.fori_loop(0, c, cum_body, jnp.zeros((nb, GLA_K), F32))
    b = b_sc[...]
    k = k_ref[...]
    qt = q_ref[...] * jnp.exp(b)
    kt = k * jnp.exp(-b)
    kh = k * jnp.exp(jnp.concatenate([bl] * c, axis=0) - b)
    gam = jnp.exp(bl)

    ri = lax.broadcasted_iota(jnp.int32, (rows, rows), 0)
    ci = lax.broadcasted_iota(jnp.int32, (rows, rows), 1)
    pair_ok = (ri >= ci) & (((ri - ci) & seq_mask) == 0)
    xw = nb * GLA_DK
    own_blk = (lax.broadcasted_iota(jnp.int32, (rows, xw), 1) // GLA_DK
               == (lax.broadcasted_iota(jnp.int32, (rows, xw), 0) & seq_mask))
    own_blk_seq = (lax.broadcasted_iota(jnp.int32, (nb, xw), 1) // GLA_DK
                   == lax.broadcasted_iota(jnp.int32, (nb, xw), 0))
    reps = xw // LANES

    def head_dup(z, h):
        blk = z[:, (h // 2) * LANES:(h // 2 + 1) * LANES]
        rolled = pltpu.roll(blk, GLA_DK, axis=1)
        low = lax.broadcasted_iota(jnp.int32, blk.shape, 1) < GLA_DK
        return jnp.where(low, blk, rolled) if h % 2 == 0 else jnp.where(low, rolled, blk)

    def expand(zd, own):
        return jnp.where(own, jnp.concatenate([zd] * reps, axis=1), 0.0)

    for h in range(GLA_HEADS):
        vs = slice(h * GLA_DV, (h + 1) * GLA_DV)
        qd, ktd, khd = head_dup(qt, h), head_dup(kt, h), head_dup(kh, h)
        vb = v_ref[:, vs].astype(BF16)
        st = st_ref[h]
        att = lax.dot_general(qd[:, :GLA_DK].astype(BF16), ktd[:, :GLA_DK].astype(BF16),
                              (((1,), (1,)), ((), ())), preferred_element_type=F32)
        att = jnp.where(pair_ok, att, 0.0).astype(BF16)
        o = jnp.dot(att, vb, preferred_element_type=F32) + lax.dot_general(
            expand(qd, own_blk).astype(BF16), st.astype(BF16), (((1,), (1,)), ((), ())),
            preferred_element_type=F32)
        upd = lax.dot_general(vb, expand(khd, own_blk).astype(BF16), (((0,), (0,)), ((), ())),
                              preferred_element_type=F32)
        gam_row = jnp.sum(expand(head_dup(gam, h), own_blk_seq), axis=0, keepdims=True)
        st_ref[h] = st * gam_row + upd
        o_ref[:, vs] = _rms(o, gn_ref[:, vs]) * gs_ref[:, vs]


def _gla_chunk(q, k, v, la, gs, gn, st0, nb, c):
    assert nb & (nb - 1) == 0 and (nb * GLA_DK) % LANES == 0
    n = q.shape[0]
    rows = nb * c
    sshape = (GLA_HEADS, GLA_DV, nb * GLA_DK)
    kern = functools.partial(_gla_chunk_kernel, nb=nb, c=c)
    return pl.pallas_call(
        kern,
        grid=(n // rows,),
        in_specs=[_row_spec(rows, GLA_K), _row_spec(rows, GLA_K), _row_spec(rows, GLA_V),
                  _row_spec(rows, GLA_K), _row_spec(rows, GLA_V), _const_spec((1, GLA_V)),
                  _const_spec(sshape)],
        out_specs=[_row_spec(rows, GLA_V), pl.BlockSpec(sshape, lambda i: (0, 0, 0))],
        out_shape=[jax.ShapeDtypeStruct((n, GLA_V), F32), jax.ShapeDtypeStruct(sshape, F32)],
        scratch_shapes=[pltpu.VMEM((rows, GLA_K), F32)],
        compiler_params=_params(),
        name="gla_chunk",
    )(q, k, v, la, gs, gn, st0)


def _gla_state_from_stacked(st):
    nb = st.shape[2] // GLA_DK
    return jnp.transpose(st.reshape(GLA_HEADS, GLA_DV, nb, GLA_DK), (2, 0, 3, 1))


def _gla_step_kernel(q_ref, k_ref, la_ref, v_ref, gs_ref, gn_ref, s0_ref, o_ref, s_ref):
    q = q_ref[...]
    k = k_ref[...]
    a = jnp.exp(la_ref[...])
    v = v_ref[...]
    o = jnp.zeros_like(v)
    for d in range(GLA_DK):
        s_new = a[:, d:d + 1] * s0_ref[:, d, :] + k[:, d:d + 1] * v
        s_ref[:, d, :] = s_new
        o = o + q[:, d:d + 1] * s_new
    o_ref[...] = _rms(o, gn_ref[...]) * gs_ref[...]


def _gla_step(q, k, v, la, gs, gn, s0):
    nb = q.shape[0]
    heads = lambda z: jnp.transpose(z.reshape(nb, GLA_HEADS, GLA_DK), (1, 0, 2))
    hspec = pl.BlockSpec((None, nb, GLA_DK), lambda h: (h, 0, 0))
    vspec = pl.BlockSpec((nb, GLA_DV), lambda h: (0, h))
    sspec = pl.BlockSpec((nb, None, GLA_DK, GLA_DV), lambda h: (0, h, 0, 0))
    return pl.pallas_call(
        _gla_step_kernel,
        grid=(GLA_HEADS,),
        in_specs=[hspec, hspec, hspec, vspec, vspec, pl.BlockSpec((1, GLA_DV), lambda h: (0, h)), sspec],
        out_specs=[vspec, sspec],
        out_shape=[jax.ShapeDtypeStruct((nb, GLA_V), F32),
                   jax.ShapeDtypeStruct((nb, GLA_HEADS, GLA_DK, GLA_DV), F32)],
        compiler_params=_params("parallel"),
        name="gla_step",
    )(heads(q), heads(k), heads(la), v, gs, gn, s0)


def _l0_out_kernel(u_ref, og_ref, x_ref, bbre_ref, bbim_ref, are_ref, aim_ref, cre_ref, cim_ref, d_ref,
                   wglu_ref, bglu_ref, woa_ref, wob_ref, xr0_ref, xi0_ref,
                   xo_ref, xre_ref, xim_ref, sr_sc, si_sc, *tm_sc, nb, tt):
    @pl.when(pl.program_id(0) == 0)
    def _():
        xre_ref[...] = xr0_ref[...]
        xim_ref[...] = xi0_ref[...]

    u = u_ref[...]
    ub = u.astype(BF16)
    kin = S5_PACK * S5_H
    kst = S5_PACK * S5_P
    npack = S5_GROUPS // S5_PACK
    ys = []
    for j in range(npack):
        cs = slice(j * kst, (j + 1) * kst)
        uj = ub[:, j * kin:(j + 1) * kin]
        bur = jnp.dot(uj, bbre_ref[j], preferred_element_type=F32)
        bui = jnp.dot(uj, bbim_ref[j], preferred_element_type=F32)
        ar = jnp.broadcast_to(are_ref[:, cs], (nb, kst))
        ai = jnp.broadcast_to(aim_ref[:, cs], (nb, kst))
        xr = xre_ref[:, cs]
        xi = xim_ref[:, cs]
        for t in range(tt):
            rows = slice(t * nb, (t + 1) * nb)
            xr, xi = ar * xr - ai * xi + bur[rows], ar * xi + ai * xr + bui[rows]
            sr_sc[rows, cs] = xr
            si_sc[rows, cs] = xi
        xre_ref[:, cs] = xr
        xim_ref[:, cs] = xi
        ys.append(jnp.dot(sr_sc[:, cs].astype(BF16), cre_ref[j], preferred_element_type=F32)
                  - jnp.dot(si_sc[:, cs].astype(BF16), cim_ref[j], preferred_element_type=F32))
    y = jnp.concatenate(ys, axis=-1) + d_ref[...] * u
    y = jax.nn.gelu(y)
    y = y * _sigmoid(_mm(y, wglu_ref[...]) + bglu_ref[...])
    x = _load_time_major(x_ref, tm_sc[0] if tm_sc else None)
    xo_ref[...] = x + _mm(og_ref[...], woa_ref[...]) + _mm(y, wob_ref[...])


def _l0_out(u, og, x, w, xr0, xi0, nb, tt):
    n = u.shape[0]
    rows = nb * tt
    batch_major = x.ndim == 3
    npack = S5_GROUPS // S5_PACK
    kin, kst = S5_PACK * S5_H, S5_PACK * S5_P
    kern = functools.partial(_l0_out_kernel, nb=nb, tt=tt)
    st_spec = pl.BlockSpec((nb, S5_N), lambda i: (0, 0))
    return pl.pallas_call(
        kern,
        grid=(n // rows,),
        in_specs=[_row_spec(rows, S5_W), _row_spec(rows, GLA_V),
                  _seq_spec(nb, tt, D_MODEL) if batch_major else _row_spec(rows, D_MODEL),
                  _const_spec((npack, kin, kst)), _const_spec((npack, kin, kst)),
                  _const_spec((1, S5_N)), _const_spec((1, S5_N)),
                  _const_spec((npack, kst, kin)), _const_spec((npack, kst, kin)),
                  _const_spec((1, S5_W)), _const_spec((S5_W, S5_W)), _const_spec((1, S5_W)),
                  _const_spec((GLA_V, D_MODEL)), _const_spec((S5_W, D_MODEL)),
                  _const_spec((nb, S5_N)), _const_spec((nb, S5_N))],
        out_specs=[_row_spec(rows, D_MODEL), st_spec, st_spec],
        out_shape=[jax.ShapeDtypeStruct((n, D_MODEL), F32), jax.ShapeDtypeStruct((nb, S5_N), F32),
                   jax.ShapeDtypeStruct((nb, S5_N), F32)],
        scratch_shapes=[pltpu.VMEM((rows, S5_N), F32)] * 2
                       + ([_tm_scratch(rows, D_MODEL)] if batch_major else []),
        compiler_params=_params(),
        name="l0_out",
    )(u, og, x, w["s5_bbre"], w["s5_bbim"], w["s5_are"], w["s5_aim"], w["s5_cre"], w["s5_cim"],
      w["s5_d"], w["s5_w_glu"], w["s5_b_glu"], w["w_out_0a"], w["w_out_0b"], xr0, xi0)


def _ffn_kernel(x_ref, g_ref, wg_ref, wv_ref, cw_ref, cb_ref, wd_ref, c0_ref, gf_ref, xo_ref, c_ref,
                hm_sc, *tm_sc, nb, tt, final):
    @pl.when(pl.program_id(0) == 0)
    def _():
        c_ref[...] = c0_ref[...]

    rows = nb * tt
    x = x_ref[...]
    xn = _rms(x, g_ref[...]).astype(BF16)
    for ci in range(D_FF // FF_CHUNK):
        cs = slice(ci * FF_CHUNK, (ci + 1) * FF_CHUNK)
        gate = jnp.dot(xn, wg_ref[:, cs], preferred_element_type=F32)
        val = jnp.dot(xn, wv_ref[:, cs], preferred_element_type=F32)
        ext = jnp.concatenate([c_ref[:, cs], gate], axis=0)
        y = cb_ref[:, cs] + ext[0:rows] * cw_ref[0:1, cs]
        y = y + ext[nb:nb + rows] * cw_ref[1:2, cs]
        y = y + ext[2 * nb:2 * nb + rows] * cw_ref[2:3, cs]
        c_ref[:, cs] = ext[tt * nb:(tt + 2) * nb]
        hm_sc[:, cs] = (jax.nn.gelu(y) * val).astype(BF16)
    out = x + jnp.dot(hm_sc[...], wd_ref[...], preferred_element_type=F32)
    if final:
        out = _rms(out, gf_ref[...])
    _store_time_major(xo_ref, out, tm_sc[0] if tm_sc else None)


def _ffn(x, w, layer, c0, nb, tt, final, batch_major_out=False):
    n = x.shape[0]
    rows = nb * tt
    kern = functools.partial(_ffn_kernel, nb=nb, tt=tt, final=final)
    if batch_major_out:
        o_spec, o_shape = _seq_spec(nb, tt, D_MODEL), (nb, n // nb, D_MODEL)
    else:
        o_spec, o_shape = _row_spec(rows, D_MODEL), (n, D_MODEL)
    cshape = ((FFN_CONV - 1) * nb, D_FF)
    return pl.pallas_call(
        kern,
        grid=(n // rows,),
        in_specs=[_row_spec(rows, D_MODEL), _const_spec((1, D_MODEL)),
                  _const_spec((D_MODEL, D_FF)), _const_spec((D_MODEL, D_FF)),
                  _const_spec((FFN_CONV, D_FF)), _const_spec((1, D_FF)),
                  _const_spec((D_FF, D_MODEL)), _const_spec(cshape), _const_spec((1, D_MODEL))],
        out_specs=[o_spec, pl.BlockSpec(cshape, lambda i: (0, 0))],
        out_shape=[jax.ShapeDtypeStruct(o_shape, F32), jax.ShapeDtypeStruct(cshape, F32)],
        scratch_shapes=[pltpu.VMEM((rows, D_FF), BF16)]
                       + ([_tm_scratch(rows, D_MODEL)] if batch_major_out else []),
        compiler_params=_params(),
        name="ffn%d" % layer,
    )(x, w["norm_ffn"][layer], w["ffn_w_gate"][layer], w["ffn_w_val"][layer], w["ffn_conv_w"][layer],
      w["ffn_conv_b"][layer], w["ffn_w_down"][layer], c0, w["norm_final"])


def _l1_kernel(x_ref, g_ref, wgt_ref, wxr_ref, cw_ref, cb_ref, wa_ref, ba_ref, wx_ref, bx_ref, lam_ref,
               wo_ref, h0_ref, c0_ref, xo_ref, h_ref, c_ref, a_sc, b_sc, *, nb, tt):
    @pl.when(pl.program_id(0) == 0)
    def _():
        h_ref[...] = h0_ref[...]
        c_ref[...] = c0_ref[...]

    sp = jax.nn.softplus(-lam_ref[...])
    nsplit = L1_SPLIT if tt % L1_SPLIT == 0 else 1
    th = tt // nsplit
    rows = nb * th
    carry = c_ref[...]
    h = h_ref[...]
    for part in range(nsplit):
        prow = slice(part * rows, (part + 1) * rows)
        x = x_ref[prow, :]
        xn = _rms(x, g_ref[...]).astype(BF16)
        xr = jnp.dot(xn, wxr_ref[...], preferred_element_type=F32)
        ext = jnp.concatenate([carry, xr], axis=0)
        xc = cb_ref[...] + ext[0:rows] * cw_ref[0:1, :]
        for j in range(1, RNN_CONV):
            xc = xc + ext[j * nb:j * nb + rows] * cw_ref[j:j + 1, :]
        carry = ext[th * nb:(th + RNN_CONV - 1) * nb]

        xcb = xc.astype(BF16)
        rs, gs = [], []
        for wi in range(RNN_W // GATE_WIN):
            for ni in range(GATE_WIN // GATE_N):
                k0 = wi * GATE_WIN + ni * LANES
                lhs = xcb[:, k0:k0 + GATE_K]
                rs.append(jnp.dot(lhs, wa_ref[wi, ni], preferred_element_type=F32))
                gs.append(jnp.dot(lhs, wx_ref[wi, ni], preferred_element_type=F32))
        r = _sigmoid(jnp.concatenate(rs, axis=-1) + ba_ref[...])
        ig = _sigmoid(jnp.concatenate(gs, axis=-1) + bx_ref[...])
        log_a = (-RNN_C) * r * sp
        a = jnp.exp(log_a)
        a_sc[prow, :] = a
        b_sc[prow, :] = _sqrt_nonneg(jnp.tanh(-log_a) * (a * a + 1.0)) * (ig * xc)
        gg = jax.nn.gelu(jnp.dot(xn, wgt_ref[...], preferred_element_type=F32))

        for t in range(part * th, (part + 1) * th):
            rws = slice(t * nb, (t + 1) * nb)
            h = a_sc[rws, :] * h + b_sc[rws, :]
            b_sc[rws, :] = h
        xo_ref[prow, :] = x + _mm(b_sc[prow, :] * gg, wo_ref[...])
    c_ref[...] = carry
    h_ref[...] = h


def _l1(x, w, h0, c0, nb, tt):
    n = x.shape[0]
    rows = nb * tt
    kern = functools.partial(_l1_kernel, nb=nb, tt=tt)
    cshape = ((RNN_CONV - 1) * nb, RNN_W)
    gshape = (RNN_W // GATE_WIN, GATE_WIN // GATE_N, GATE_K, GATE_N)
    return pl.pallas_call(
        kern,
        grid=(n // rows,),
        in_specs=[_row_spec(rows, D_MODEL), _const_spec((1, D_MODEL)),
                  _const_spec((D_MODEL, RNN_W)), _const_spec((D_MODEL, RNN_W)),
                  _const_spec((RNN_CONV, RNN_W)), _const_spec((1, RNN_W)),
                  _const_spec(gshape), _const_spec((1, RNN_W)),
                  _const_spec(gshape), _const_spec((1, RNN_W)),
                  _const_spec((1, RNN_W)), _const_spec((RNN_W, D_MODEL)),
                  _const_spec((nb, RNN_W)), _const_spec(cshape)],
        out_specs=[_row_spec(rows, D_MODEL), pl.BlockSpec((nb, RNN_W), lambda i: (0, 0)),
                   pl.BlockSpec(cshape, lambda i: (0, 0))],
        out_shape=[jax.ShapeDtypeStruct((n, D_MODEL), F32), jax.ShapeDtypeStruct((nb, RNN_W), F32),
                   jax.ShapeDtypeStruct(cshape, F32)],
        scratch_shapes=[pltpu.VMEM((rows, RNN_W), F32)] * 2,
        compiler_params=_params(),
        name="l1_mixer",
    )(x, w["norm_mix_1"], w["w_gate_1"], w["w_xr_1"], w["rnn_conv_w"], w["rnn_conv_b"],
      w["rnn_wa"], w["rnn_b_a"], w["rnn_wx"], w["rnn_b_x"], w["rnn_lam"], w["w_out_1"], h0, c0)


def _pack_gate(wblk):
    dense = jax.scipy.linalg.block_diag(*[wblk[i] for i in range(RNN_BLOCKS)])
    tiles = []
    for wi in range(RNN_W // GATE_WIN):
        row = []
        for ni in range(GATE_WIN // GATE_N):
            k0 = wi * GATE_WIN + ni * LANES
            n0 = wi * GATE_WIN + ni * GATE_N
            row.append(dense[k0:k0 + GATE_K, n0:n0 + GATE_N])
        tiles.append(jnp.stack(row))
    return jnp.stack(tiles).astype(BF16)


def _prep_weights(p):
    w = {}
    row = lambda v: v.reshape(1, -1).astype(F32)
    w_in = p["w_in_0"]
    c = 0
    for name, width in (("w_q", GLA_K), ("w_k", GLA_K), ("w_v", GLA_V), ("w_g", GLA_V)):
        w[name] = w_in[:, c:c + width].astype(BF16)
        c += width
    w["w_lr"] = jnp.pad(w_in[:, c:c + GLA_RANK], ((0, 0), (0, LANES - GLA_RANK))).astype(BF16)
    c += GLA_RANK
    w["w_u"] = w_in[:, c:c + S5_W].astype(BF16)
    w["w_alpha"] = jnp.pad(p["w_alpha_0"], ((0, LANES - GLA_RANK), (0, 0))).astype(BF16)
    w["b_alpha"] = row(p["b_alpha_0"])
    w["norm_mix_0"] = row(p["norm_mix_0"])
    w["gla_norm"] = row(p["gla_norm_0"])

    are, aim, bbre, bbim = _s5_prep(p["s5_lam_re"], p["s5_lam_im"], p["s5_log_dt"], p["s5_b_re"],
                                    p["s5_b_im"])
    npack = S5_GROUPS // S5_PACK
    eye = jnp.eye(S5_PACK, dtype=F32)
    pack_b = lambda m: jnp.einsum("jghp,gk->jghkp", m.reshape(npack, S5_PACK, S5_H, S5_P), eye).reshape(
        npack, S5_PACK * S5_H, S5_PACK * S5_P).astype(BF16)
    pack_c = lambda m: jnp.einsum("jghp,gk->jgpkh", m.reshape(npack, S5_PACK, S5_H, S5_P), eye).reshape(
        npack, S5_PACK * S5_P, S5_PACK * S5_H).astype(BF16)
    w["s5_are"] = are.reshape(1, S5_N)
    w["s5_aim"] = aim.reshape(1, S5_N)
    w["s5_bbre"] = pack_b(bbre)
    w["s5_bbim"] = pack_b(bbim)
    w["s5_cre"] = pack_c(p["s5_c_re"])
    w["s5_cim"] = pack_c(p["s5_c_im"])
    w["s5_d"] = row(p["s5_d"])
    w["s5_w_glu"] = p["s5_w_glu"].astype(BF16)
    w["s5_b_glu"] = row(p["s5_b_glu"])
    w["w_out_0a"] = p["w_out_0"][:GLA_V].astype(BF16)
    w["w_out_0b"] = p["w_out_0"][GLA_V:].astype(BF16)

    w["norm_mix_1"] = row(p["norm_mix_1"])
    w["w_gate_1"] = p["w_in_1"][:, :RNN_W].astype(BF16)
    w["w_xr_1"] = p["w_in_1"][:, RNN_W:].astype(BF16)
    w["rnn_conv_w"] = p["rnn_conv_w"].astype(F32)
    w["rnn_conv_b"] = row(p["rnn_conv_b"])
    w["rnn_wa"] = _pack_gate(p["rnn_w_a"])
    w["rnn_wx"] = _pack_gate(p["rnn_w_x"])
    w["rnn_b_a"] = row(p["rnn_b_a"])
    w["rnn_b_x"] = row(p["rnn_b_x"])
    w["rnn_lam"] = row(p["rnn_lam"])
    w["w_out_1"] = p["w_out_1"].astype(BF16)

    w["norm_ffn"] = [row(p["norm_ffn"][l]) for l in range(2)]
    w["ffn_w_gate"] = [p["ffn_w_up"][l, :, :D_FF].astype(BF16) for l in range(2)]
    w["ffn_w_val"] = [p["ffn_w_up"][l, :, D_FF:].astype(BF16) for l in range(2)]
    w["ffn_conv_w"] = [p["ffn_conv_w"][l].astype(F32) for l in range(2)]
    w["ffn_conv_b"] = [row(p["ffn_conv_b"][l]) for l in range(2)]
    w["ffn_w_down"] = [p["ffn_w_down"][l].astype(BF16) for l in range(2)]
    w["norm_final"] = row(p["norm_final"])
    return w


def _tile_steps(nt):
    cap = lambda m: min(nt, m)
    return dict(l0_in=cap(64), gla=cap(32), l0_out=cap(64), ffn=cap(64), l1=cap(64))


def _trunk(x, w, st, nb, nt):
    tt = _tile_steps(nt)
    batch_major = x.ndim == 3
    q, k, v, gs, la, u = _l0_in(x, w, nb, tt["l0_in"])
    if nt == 1:
        og, s_gla = _gla_step(q, k, v, la, gs, w["gla_norm"], st["gla"])
    else:
        og, s_gla = _gla_chunk(q, k, v, la, gs, w["gla_norm"], st["gla"], nb, tt["gla"])
    x, s_re, s_im = _l0_out(u, og, x, w, st["s5_re"], st["s5_im"], nb, tt["l0_out"])
    x, fc0 = _ffn(x, w, 0, st["fc"][0], nb, tt["ffn"], False)
    x, h, rc = _l1(x, w, st["h"], st["rc"], nb, tt["l1"])
    x, fc1 = _ffn(x, w, 1, st["fc"][1], nb, tt["ffn"], True, batch_major_out=batch_major)
    return x, dict(gla=s_gla, s5_re=s_re, s5_im=s_im, h=h, rc=rc, fc=[fc0, fc1])


def _time_major(cache):
    b, j, c = cache.shape
    return jnp.transpose(cache, (1, 0, 2)).reshape(j * b, c)


def _batch_major(cache, nb):
    jb, c = cache.shape
    return jnp.transpose(cache.reshape(jb // nb, nb, c), (1, 0, 2))


def kernel(x_prompt, x_sample, state_gla, state_s5_re, state_s5_im, state_rglru, cache_rglru_conv,
           cache_ffn_conv, meta_tokens, norm_mix_0, w_in_0, w_alpha_0, b_alpha_0, gla_norm_0,
           s5_lam_re, s5_lam_im, s5_log_dt, s5_b_re, s5_b_im, s5_c_re, s5_c_im, s5_d, s5_w_glu,
           s5_b_glu, w_out_0, norm_mix_1, w_in_1, rnn_conv_w, rnn_conv_b, rnn_w_a, rnn_b_a, rnn_w_x,
           rnn_b_x, rnn_lam, w_out_1, norm_ffn, ffn_w_up, ffn_conv_w, ffn_conv_b, ffn_w_down, norm_final):
    w = _prep_weights(dict(
        norm_mix_0=norm_mix_0, w_in_0=w_in_0, w_alpha_0=w_alpha_0, b_alpha_0=b_alpha_0,
        gla_norm_0=gla_norm_0, s5_lam_re=s5_lam_re, s5_lam_im=s5_lam_im, s5_log_dt=s5_log_dt,
        s5_b_re=s5_b_re, s5_b_im=s5_b_im, s5_c_re=s5_c_re, s5_c_im=s5_c_im, s5_d=s5_d,
        s5_w_glu=s5_w_glu, s5_b_glu=s5_b_glu, w_out_0=w_out_0, norm_mix_1=norm_mix_1, w_in_1=w_in_1,
        rnn_conv_w=rnn_conv_w, rnn_conv_b=rnn_conv_b, rnn_w_a=rnn_w_a, rnn_b_a=rnn_b_a,
        rnn_w_x=rnn_w_x, rnn_b_x=rnn_b_x, rnn_lam=rnn_lam, w_out_1=w_out_1, norm_ffn=norm_ffn,
        ffn_w_up=ffn_w_up, ffn_conv_w=ffn_conv_w, ffn_conv_b=ffn_conv_b, ffn_w_down=ffn_w_down,
        norm_final=norm_final))

    bp, seq, _ = x_prompt.shape
    bs = x_sample.shape[0]

    zeros = dict(gla=jnp.zeros((GLA_HEADS, GLA_DV, bp * GLA_DK), F32),
                 s5_re=jnp.zeros((bp, S5_N), F32), s5_im=jnp.zeros((bp, S5_N), F32),
                 h=jnp.zeros((bp, RNN_W), F32), rc=jnp.zeros(((RNN_CONV - 1) * bp, RNN_W), F32),
                 fc=[jnp.zeros(((FFN_CONV - 1) * bp, D_FF), F32)] * 2)
    x_meta = jnp.repeat(meta_tokens.astype(F32), bp, axis=0)
    _, st = _trunk(x_meta, w, zeros, bp, N_META)

    yp, st = _trunk(x_prompt, w, st, bp, seq)

    st_s = dict(gla=state_gla, s5_re=state_s5_re.reshape(bs, S5_N), s5_im=state_s5_im.reshape(bs, S5_N),
                h=state_rglru, rc=_time_major(cache_rglru_conv),
                fc=[_time_major(cache_ffn_conv[l]) for l in range(2)])
    ys, ss = _trunk(x_sample.reshape(bs, D_MODEL), w, st_s, bs, 1)
    ys = ys.reshape(bs, 1, D_MODEL)

    grp = lambda z, nb: z.reshape(nb, S5_GROUPS, S5_P)
    return (yp, ys, _gla_state_from_stacked(st["gla"]), ss["gla"], grp(st["s5_re"], bp), grp(ss["s5_re"], bs),
            grp(st["s5_im"], bp), grp(ss["s5_im"], bs), st["h"], ss["h"],
            _batch_major(st["rc"], bp), _batch_major(ss["rc"], bs),
            jnp.stack([_batch_major(st["fc"][l], bp) for l in range(2)]),
            jnp.stack([_batch_major(ss["fc"][l], bs) for l in range(2)]))
```

```python
import functools

import jax
import jax.numpy as jnp
from jax import lax
from jax.experimental import pallas as pl
from jax.experimental.pallas import tpu as pltpu

F32 = jnp.float32
BF16 = jnp.bfloat16

D_MODEL = 1024
N_META = 16
EPS = 1e-6
F32_TINY = 1.1754944e-38
GLA_HEADS = 4
GLA_DK = 64
GLA_DV = 128
GLA_RANK = 16
GLA_TAU = 16.0
GLA_K = GLA_HEADS * GLA_DK
GLA_V = GLA_HEADS * GLA_DV
S5_GROUPS = 32
S5_H = 16
S5_P = 64
S5_W = S5_GROUPS * S5_H
S5_N = S5_GROUPS * S5_P
RNN_W = 1536
RNN_BLOCKS = 16
RNN_BW = RNN_W // RNN_BLOCKS
RNN_C = 8.0
RNN_CONV = 4
D_FF = 2816
FFN_CONV = 3

LANES = 128
FF_CHUNK = 256
L1_SPLIT = 2
S5_PACK = 8
GATE_WIN = 768
GATE_K = 512
GATE_N = 256
VMEM_LIMIT = 56 * 1024 * 1024


def _rms(x, g):
    return x * lax.rsqrt(jnp.mean(x * x, axis=-1, keepdims=True) + EPS) * g


def _sigmoid(x):
    return 0.5 * jnp.tanh(0.5 * x) + 0.5


def _sqrt_nonneg(t):
    return t * lax.rsqrt(jnp.maximum(t, F32_TINY))


def _mm(a, w):
    return jnp.dot(a.astype(BF16), w, preferred_element_type=F32)


def _const_spec(shape, index=None):
    idx = tuple(index) if index is not None else (0,) * len(shape)
    return pl.BlockSpec(shape, lambda i: idx, pipeline_mode=pl.Buffered(1))


def _row_spec(rows, cols):
    return pl.BlockSpec((rows, cols), lambda i: (i, 0))


def _seq_spec(nb, tt, cols):
    return pl.BlockSpec((nb, tt, cols), lambda i: (0, i, 0))


def _tm_scratch(rows, cols):
    return pltpu.VMEM((cols // LANES, rows, LANES), F32)


def _load_time_major(x_ref, tm_sc):
    if tm_sc is None:
        return x_ref[...]
    nb, tt, cols = x_ref.shape
    for b in range(nb):
        for j in range(cols // LANES):
            tm_sc[j, pl.ds(b, tt, stride=nb), :] = x_ref[b, :, j * LANES:(j + 1) * LANES]
    return jnp.concatenate([tm_sc[j] for j in range(cols // LANES)], axis=-1)


def _store_time_major(o_ref, val, tm_sc):
    if tm_sc is None:
        o_ref[...] = val
        return
    nb, tt, cols = o_ref.shape
    for j in range(cols // LANES):
        tm_sc[j] = val[:, j * LANES:(j + 1) * LANES]
    for b in range(nb):
        for j in range(cols // LANES):
            o_ref[b, :, j * LANES:(j + 1) * LANES] = tm_sc[j, pl.ds(b, tt, stride=nb), :]


def _cache_shape(nb, tt, taps, width):
    return (nb, taps * width) if tt == 1 else (taps * nb, width)


def _params(sem="arbitrary"):
    return pltpu.CompilerParams(dimension_semantics=(sem,), vmem_limit_bytes=VMEM_LIMIT)


def _s5_prep_kernel(lr_ref, li_ref, ldt_ref, brt_ref, bit_ref, are_ref, aim_ref, bbre_ref, bbim_ref):
    lr = lr_ref[...]
    li = li_ref[...]
    dt = jnp.exp(ldt_ref[...])
    mag = jnp.exp(lr * dt)
    ab_re = mag * jnp.cos(li * dt)
    ab_im = mag * jnp.sin(li * dt)
    den = lr * lr + li * li
    nr = ab_re - 1.0
    ni = ab_im
    f_re = (nr * lr + ni * li) / den
    f_im = (ni * lr - nr * li) / den
    are_ref[...] = ab_re
    aim_ref[...] = ab_im
    brt = brt_ref[...]
    bit = bit_ref[...]
    bbre_ref[...] = f_re[:, None, :] * brt - f_im[:, None, :] * bit
    bbim_ref[...] = f_re[:, None, :] * bit + f_im[:, None, :] * brt


def _s5_prep(lam_re, lam_im, log_dt, b_re, b_im):
    g, p, h = b_re.shape
    brt = jnp.transpose(b_re, (0, 2, 1))
    bit = jnp.transpose(b_im, (0, 2, 1))
    return pl.pallas_call(
        _s5_prep_kernel,
        out_shape=(jax.ShapeDtypeStruct((g, p), F32), jax.ShapeDtypeStruct((g, p), F32),
                   jax.ShapeDtypeStruct((g, h, p), F32), jax.ShapeDtypeStruct((g, h, p), F32)),
        name="s5_prep",
    )(lam_re, lam_im, log_dt.reshape(g, 1), brt, bit)


def _l0_in_kernel(x_ref, g_ref, wq_ref, wk_ref, wv_ref, wg_ref, wu_ref, wlr_ref, wal_ref, bal_ref,
                  q_ref, k_ref, v_ref, gs_ref, la_ref, u_ref, *tm_sc):
    xn = _rms(_load_time_major(x_ref, tm_sc[0] if tm_sc else None), g_ref[...]).astype(BF16)
    q_ref[...] = _mm(xn, wq_ref[...]) * (GLA_DK ** -0.5)
    k_ref[...] = _mm(xn, wk_ref[...])
    v_ref[...] = _mm(xn, wv_ref[...])
    g = _mm(xn, wg_ref[...])
    gs_ref[...] = g * _sigmoid(g)
    u_ref[...] = _mm(xn, wu_ref[...])
    lr = _mm(xn, wlr_ref[...])
    pre = _mm(lr, wal_ref[...]) + bal_ref[...]
    la_ref[...] = jax.nn.log_sigmoid(pre) * (1.0 / GLA_TAU)


def _l0_in(x, w, nb, tt):
    rows = nb * tt
    batch_major = x.ndim == 3
    n = x.shape[0] * x.shape[1] if batch_major else x.shape[0]
    outs = [GLA_K, GLA_K, GLA_V, GLA_V, GLA_K, S5_W]
    return pl.pallas_call(
        _l0_in_kernel,
        grid=(n // rows,),
        in_specs=[_seq_spec(nb, tt, D_MODEL) if batch_major else _row_spec(rows, D_MODEL),
                  _const_spec((1, D_MODEL)),
                  _const_spec((D_MODEL, GLA_K), (0, 0)), _const_spec((D_MODEL, GLA_K), (0, 1)),
                  _const_spec((D_MODEL, GLA_V), (0, 1)), _const_spec((D_MODEL, GLA_V), (0, 2)),
                  _const_spec((D_MODEL, S5_W)), _const_spec((D_MODEL, LANES)),
                  _const_spec((LANES, GLA_K)), _const_spec((1, GLA_K))],
        out_specs=[_row_spec(rows, c) for c in outs],
        out_shape=[jax.ShapeDtypeStruct((n, c), F32) for c in outs],
        scratch_shapes=[_tm_scratch(rows, D_MODEL)] if batch_major else [],
        compiler_params=_params("parallel"),
        name="l0_in",
    )(x, w["norm_mix_0"], w["w_in_0"], w["w_in_0"], w["w_in_0"], w["w_in_0"], w["w_u"], w["w_lr"],
      w["w_alpha"], w["b_alpha"])


def _gla_chunk_kernel(q_ref, k_ref, v_ref, la_ref, gs_ref, gn_ref, st0_ref, o_ref, st_ref, b_sc, *, nb, c):
    rows = nb * c
    seq_mask = nb - 1

    @pl.when(pl.program_id(0) == 0)
    def _():
        st_ref[...] = st0_ref[...]

    def cum_body(t, run):
        rws = pl.ds(pl.multiple_of(t * nb, nb), nb)
        run = run + la_ref[rws, :]
        b_sc[rws, :] = run
        return run

    bl = lax.fori_loop(0, c, cum_body, jnp.zeros((nb, GLA_K), F32))
    b = b_sc[...]
    k = k_ref[...]
    qt = q_ref[...] * jnp.exp(b)
    kt = k * jnp.exp(-b)
    kh = k * jnp.exp(jnp.concatenate([bl] * c, axis=0) - b)
    gam = jnp.exp(bl)

    ri = lax.broadcasted_iota(jnp.int32, (rows, rows), 0)
    ci = lax.broadcasted_iota(jnp.int32, (rows, rows), 1)
    pair_ok = (ri >= ci) & (((ri - ci) & seq_mask) == 0)
    xw = nb * GLA_DK
    own_blk = (lax.broadcasted_iota(jnp.int32, (rows, xw), 1) // GLA_DK
               == (lax.broadcasted_iota(jnp.int32, (rows, xw), 0) & seq_mask))
    own_blk_seq = (lax.broadcasted_iota(jnp.int32, (nb, xw), 1) // GLA_DK
                   == lax.broadcasted_iota(jnp.int32, (nb, xw), 0))
    reps = xw // LANES

    def head_dup(z, h):
        blk = z[:, (h // 2) * LANES:(h // 2 + 1) * LANES]
        rolled = pltpu.roll(blk, GLA_DK, axis=1)
        low = lax.broadcasted_iota(jnp.int32, blk.shape, 1) < GLA_DK
        return jnp.where(low, blk, rolled) if h % 2 == 0 else jnp.where(low, rolled, blk)

    def expand(zd, own):
        return jnp.where(own, jnp.concatenate([zd] * reps, axis=1), 0.0)

    for h in range(GLA_HEADS):
        vs = slice(h * GLA_DV, (h + 1) * GLA_DV)
        qd, ktd, khd = head_dup(qt, h), head_dup(kt, h), head_dup(kh, h)
        vb = v_ref[:, vs].astype(BF16)
        st = st_ref[h]
        att = lax.dot_general(qd[:, :GLA_DK].astype(BF16), ktd[:, :GLA_DK].astype(BF16),
                              (((1,), (1,)), ((), ())), preferred_element_type=F32)
        att = jnp.where(pair_ok, att, 0.0).astype(BF16)
        o = jnp.dot(att, vb, preferred_element_type=F32) + lax.dot_general(
            expand(qd, own_blk).astype(BF16), st.astype(BF16), (((1,), (1,)), ((), ())),
            preferred_element_type=F32)
        upd = lax.dot_general(vb, expand(khd, own_blk).astype(BF16), (((0,), (0,)), ((), ())),
                              preferred_element_type=F32)
        gam_row = jnp.sum(expand(head_dup(gam, h), own_blk_seq), axis=0, keepdims=True)
        st_ref[h] = st * gam_row + upd
        o_ref[:, vs] = _rms(o, gn_ref[:, vs]) * gs_ref[:, vs]


def _gla_chunk(q, k, v, la, gs, gn, st0, nb, c):
    assert nb & (nb - 1) == 0 and (nb * GLA_DK) % LANES == 0
    n = q.shape[0]
    rows = nb * c
    sshape = (GLA_HEADS, GLA_DV, nb * GLA_DK)
    kern = functools.partial(_gla_chunk_kernel, nb=nb, c=c)
    return pl.pallas_call(
        kern,
        grid=(n // rows,),
        in_specs=[_row_spec(rows, GLA_K), _row_spec(rows, GLA_K), _row_spec(rows, GLA_V),
                  _row_spec(rows, GLA_K), _row_spec(rows, GLA_V), _const_spec((1, GLA_V)),
                  _const_spec(sshape)],
        out_specs=[_row_spec(rows, GLA_V), pl.BlockSpec(sshape, lambda i: (0, 0, 0))],
        out_shape=[jax.ShapeDtypeStruct((n, GLA_V), F32), jax.ShapeDtypeStruct(sshape, F32)],
        scratch_shapes=[pltpu.VMEM((rows, GLA_K), F32)],
        compiler_params=_params(),
        name="gla_chunk",
    )(q, k, v, la, gs, gn, st0)


def _gla_state_from_stacked(st):
    nb = st.shape[2] // GLA_DK
    return jnp.transpose(st.reshape(GLA_HEADS, GLA_DV, nb, GLA_DK), (2, 0, 3, 1))


def _gla_step_kernel(q_ref, k_ref, la_ref, v_ref, gs_ref, gn_ref, s0_ref, o_ref, s_ref):
    qT = q_ref[...].T
    kT = k_ref[...].T
    aT = jnp.exp(la_ref[...]).T
    o_rows = []
    for b in range(q_ref.shape[0]):
        s_new = aT[:, b:b + 1] * s0_ref[b] + kT[:, b:b + 1] * v_ref[b:b + 1, :]
        s_ref[b] = s_new
        o_rows.append(jnp.sum(qT[:, b:b + 1] * s_new, axis=0, keepdims=True))
    o = jnp.concatenate(o_rows, axis=0)
    o_ref[...] = _rms(o, gn_ref[...]) * gs_ref[...]


def _gla_step(q, k, v, la, gs, gn, s0):
    nb = q.shape[0]
    heads = lambda z: jnp.transpose(z.reshape(nb, GLA_HEADS, GLA_DK), (1, 0, 2))
    hspec = pl.BlockSpec((None, nb, GLA_DK), lambda h: (h, 0, 0))
    vspec = pl.BlockSpec((nb, GLA_DV), lambda h: (0, h))
    sspec = pl.BlockSpec((nb, None, GLA_DK, GLA_DV), lambda h: (0, h, 0, 0))
    return pl.pallas_call(
        _gla_step_kernel,
        grid=(GLA_HEADS,),
        in_specs=[hspec, hspec, hspec, vspec, vspec, pl.BlockSpec((1, GLA_DV), lambda h: (0, h)), sspec],
        out_specs=[vspec, sspec],
        out_shape=[jax.ShapeDtypeStruct((nb, GLA_V), F32),
                   jax.ShapeDtypeStruct((nb, GLA_HEADS, GLA_DK, GLA_DV), F32)],
        compiler_params=_params("parallel"),
        name="gla_step",
    )(heads(q), heads(k), heads(la), v, gs, gn, s0)


def _l0_out_kernel(u_ref, og_ref, x_ref, bbre_ref, bbim_ref, are_ref, aim_ref, cre_ref, cim_ref, d_ref,
                   wglu_ref, bglu_ref, woa_ref, wob_ref, xr0_ref, xi0_ref,
                   xo_ref, xre_ref, xim_ref, sr_sc, si_sc, *tm_sc, nb, tt):
    @pl.when(pl.program_id(0) == 0)
    def _():
        xre_ref[...] = xr0_ref[...]
        xim_ref[...] = xi0_ref[...]

    u = u_ref[...]
    ub = u.astype(BF16)
    kin = S5_PACK * S5_H
    kst = S5_PACK * S5_P
    npack = S5_GROUPS // S5_PACK
    ys = []
    for j in range(npack):
        cs = slice(j * kst, (j + 1) * kst)
        uj = ub[:, j * kin:(j + 1) * kin]
        bur = jnp.dot(uj, bbre_ref[j], preferred_element_type=F32)
        bui = jnp.dot(uj, bbim_ref[j], preferred_element_type=F32)
        ar = jnp.broadcast_to(are_ref[:, cs], (nb, kst))
        ai = jnp.broadcast_to(aim_ref[:, cs], (nb, kst))
        xr = xre_ref[:, cs]
        xi = xim_ref[:, cs]
        for t in range(tt):
            rows = slice(t * nb, (t + 1) * nb)
            xr, xi = ar * xr - ai * xi + bur[rows], ar * xi + ai * xr + bui[rows]
            sr_sc[rows, cs] = xr
            si_sc[rows, cs] = xi
        xre_ref[:, cs] = xr
        xim_ref[:, cs] = xi
        ys.append(jnp.dot(sr_sc[:, cs].astype(BF16), cre_ref[j], preferred_element_type=F32)
                  - jnp.dot(si_sc[:, cs].astype(BF16), cim_ref[j], preferred_element_type=F32))
    y = jnp.concatenate(ys, axis=-1) + d_ref[...] * u
    y = jax.nn.gelu(y)
    y = y * _sigmoid(_mm(y, wglu_ref[...]) + bglu_ref[...])
    x = _load_time_major(x_ref, tm_sc[0] if tm_sc else None)
    xo_ref[...] = x + _mm(og_ref[...], woa_ref[...]) + _mm(y, wob_ref[...])


def _l0_out(u, og, x, w, xr0, xi0, nb, tt):
    n = u.shape[0]
    rows = nb * tt
    batch_major = x.ndim == 3
    npack = S5_GROUPS // S5_PACK
    kin, kst = S5_PACK * S5_H, S5_PACK * S5_P
    kern = functools.partial(_l0_out_kernel, nb=nb, tt=tt)
    st_spec = pl.BlockSpec((nb, S5_N), lambda i: (0, 0))
    return pl.pallas_call(
        kern,
        grid=(n // rows,),
        in_specs=[_row_spec(rows, S5_W), _row_spec(rows, GLA_V),
                  _seq_spec(nb, tt, D_MODEL) if batch_major else _row_spec(rows, D_MODEL),
                  _const_spec((npack, kin, kst)), _const_spec((npack, kin, kst)),
                  _const_spec((1, S5_N)), _const_spec((1, S5_N)),
                  _const_spec((npack, kst, kin)), _const_spec((npack, kst, kin)),
                  _const_spec((1, S5_W)), _const_spec((S5_W, S5_W)), _const_spec((1, S5_W)),
                  _const_spec((GLA_V, D_MODEL), (0, 0)), _const_spec((S5_W, D_MODEL), (1, 0)),
                  _const_spec((nb, S5_N)), _const_spec((nb, S5_N))],
        out_specs=[_row_spec(rows, D_MODEL), st_spec, st_spec],
        out_shape=[jax.ShapeDtypeStruct((n, D_MODEL), F32), jax.ShapeDtypeStruct((nb, S5_N), F32),
                   jax.ShapeDtypeStruct((nb, S5_N), F32)],
        scratch_shapes=[pltpu.VMEM((rows, S5_N), F32)] * 2
                       + ([_tm_scratch(rows, D_MODEL)] if batch_major else []),
        compiler_params=_params(),
        name="l0_out",
    )(u, og, x, w["s5_bbre"], w["s5_bbim"], w["s5_are"], w["s5_aim"], w["s5_cre"], w["s5_cim"],
      w["s5_d"], w["s5_w_glu"], w["s5_b_glu"], w["w_out_0"], w["w_out_0"], xr0, xi0)


def _ffn_kernel(x_ref, g_ref, wg_ref, wv_ref, cw_ref, cb_ref, wd_ref, c0_ref, gf_ref, xo_ref, c_ref,
                hm_sc, *tm_sc, nb, tt, final):
    if tt > 1:
        @pl.when(pl.program_id(0) == 0)
        def _():
            c_ref[...] = c0_ref[...]

    rows = nb * tt
    x = x_ref[...]
    xn = _rms(x, g_ref[...]).astype(BF16)
    for ci in range(D_FF // FF_CHUNK):
        cs = slice(ci * FF_CHUNK, (ci + 1) * FF_CHUNK)
        gate = jnp.dot(xn, wg_ref[:, cs], preferred_element_type=F32)
        val = jnp.dot(xn, wv_ref[:, cs], preferred_element_type=F32)
        if tt == 1:
            taps = [c0_ref[:, j * D_FF + ci * FF_CHUNK:j * D_FF + (ci + 1) * FF_CHUNK]
                    for j in range(FFN_CONV - 1)] + [gate]
            for j in range(FFN_CONV - 1):
                c_ref[:, j * D_FF + ci * FF_CHUNK:j * D_FF + (ci + 1) * FF_CHUNK] = taps[j + 1]
        else:
            ext = jnp.concatenate([c_ref[:, cs], gate], axis=0)
            taps = [ext[j * nb:j * nb + rows] for j in range(FFN_CONV)]
            c_ref[:, cs] = ext[tt * nb:(tt + FFN_CONV - 1) * nb]
        y = cb_ref[:, cs] + taps[0] * cw_ref[0:1, cs]
        for j in range(1, FFN_CONV):
            y = y + taps[j] * cw_ref[j:j + 1, cs]
        hm_sc[:, cs] = (jax.nn.gelu(y) * val).astype(BF16)
    out = x + jnp.dot(hm_sc[...], wd_ref[...], preferred_element_type=F32)
    if final:
        out = _rms(out, gf_ref[...])
    _store_time_major(xo_ref, out, tm_sc[0] if tm_sc else None)


def _ffn(x, w, layer, c0, nb, tt, final, batch_major_out=False):
    n = x.shape[0]
    rows = nb * tt
    kern = functools.partial(_ffn_kernel, nb=nb, tt=tt, final=final)
    if batch_major_out:
        o_spec, o_shape = _seq_spec(nb, tt, D_MODEL), (nb, n // nb, D_MODEL)
    else:
        o_spec, o_shape = _row_spec(rows, D_MODEL), (n, D_MODEL)
    cshape = _cache_shape(nb, tt, FFN_CONV - 1, D_FF)
    return pl.pallas_call(
        kern,
        grid=(n // rows,),
        in_specs=[_row_spec(rows, D_MODEL), _const_spec((None, 1, D_MODEL), (layer, 0, 0)),
                  _const_spec((None, D_MODEL, D_FF), (layer, 0, 0)),
                  _const_spec((None, D_MODEL, D_FF), (layer, 0, 1)),
                  _const_spec((None, FFN_CONV, D_FF), (layer, 0, 0)),
                  _const_spec((None, 1, D_FF), (layer, 0, 0)),
                  _const_spec((None, D_FF, D_MODEL), (layer, 0, 0)), _const_spec(cshape),
                  _const_spec((1, D_MODEL))],
        out_specs=[o_spec, pl.BlockSpec(cshape, lambda i: (0, 0))],
        out_shape=[jax.ShapeDtypeStruct(o_shape, F32), jax.ShapeDtypeStruct(cshape, F32)],
        scratch_shapes=[pltpu.VMEM((rows, D_FF), BF16)]
                       + ([_tm_scratch(rows, D_MODEL)] if batch_major_out else []),
        compiler_params=_params(),
        name="ffn%d" % layer,
    )(x, w["norm_ffn"], w["ffn_w_up"], w["ffn_w_up"], w["ffn_conv_w"], w["ffn_conv_b"], w["ffn_w_down"], c0,
      w["norm_final"])


def _l1_kernel(x_ref, g_ref, wgt_ref, wxr_ref, cw_ref, cb_ref, wa_ref, ba_ref, wx_ref, bx_ref, lam_ref,
               wo_ref, h0_ref, c0_ref, xo_ref, h_ref, c_ref, a_sc, b_sc, *, nb, tt):
    @pl.when(pl.program_id(0) == 0)
    def _():
        h_ref[...] = h0_ref[...]
        if tt > 1:
            c_ref[...] = c0_ref[...]

    sp = jax.nn.softplus(-lam_ref[...])
    nsplit = L1_SPLIT if tt % L1_SPLIT == 0 else 1
    th = tt // nsplit
    rows = nb * th
    carry = c_ref[...] if tt > 1 else None
    h = h_ref[...]
    for part in range(nsplit):
        prow = slice(part * rows, (part + 1) * rows)
        x = x_ref[prow, :]
        xn = _rms(x, g_ref[...]).astype(BF16)
        xr = jnp.dot(xn, wxr_ref[...], preferred_element_type=F32)
        if tt == 1:
            taps = [c0_ref[:, j * RNN_W:(j + 1) * RNN_W] for j in range(RNN_CONV - 1)] + [xr]
            for j in range(RNN_CONV - 1):
                c_ref[:, j * RNN_W:(j + 1) * RNN_W] = taps[j + 1]
        else:
            ext = jnp.concatenate([carry, xr], axis=0)
            taps = [ext[j * nb:j * nb + rows] for j in range(RNN_CONV)]
            carry = ext[th * nb:(th + RNN_CONV - 1) * nb]
        xc = cb_ref[...] + taps[0] * cw_ref[0:1, :]
        for j in range(1, RNN_CONV):
            xc = xc + taps[j] * cw_ref[j:j + 1, :]

        xcb = xc.astype(BF16)
        rs, gs = [], []
        for wi in range(RNN_W // GATE_WIN):
            for ni in range(GATE_WIN // GATE_N):
                k0 = wi * GATE_WIN + ni * LANES
                lhs = xcb[:, k0:k0 + GATE_K]
                rs.append(jnp.dot(lhs, wa_ref[wi, ni], preferred_element_type=F32))
                gs.append(jnp.dot(lhs, wx_ref[wi, ni], preferred_element_type=F32))
        r = _sigmoid(jnp.concatenate(rs, axis=-1) + ba_ref[...])
        ig = _sigmoid(jnp.concatenate(gs, axis=-1) + bx_ref[...])
        log_a = (-RNN_C) * r * sp
        a = jnp.exp(log_a)
        a_sc[prow, :] = a
        b_sc[prow, :] = _sqrt_nonneg(jnp.tanh(-log_a) * (a * a + 1.0)) * (ig * xc)
        gg = jax.nn.gelu(jnp.dot(xn, wgt_ref[...], preferred_element_type=F32))

        for t in range(part * th, (part + 1) * th):
            rws = slice(t * nb, (t + 1) * nb)
            h = a_sc[rws, :] * h + b_sc[rws, :]
            b_sc[rws, :] = h
        xo_ref[prow, :] = x + _mm(b_sc[prow, :] * gg, wo_ref[...])
    if tt > 1:
        c_ref[...] = carry
    h_ref[...] = h


def _l1(x, w, h0, c0, nb, tt):
    n = x.shape[0]
    rows = nb * tt
    kern = functools.partial(_l1_kernel, nb=nb, tt=tt)
    cshape = _cache_shape(nb, tt, RNN_CONV - 1, RNN_W)
    gshape = (RNN_W // GATE_WIN, GATE_WIN // GATE_N, GATE_K, GATE_N)
    return pl.pallas_call(
        kern,
        grid=(n // rows,),
        in_specs=[_row_spec(rows, D_MODEL), _const_spec((1, D_MODEL)),
                  _const_spec((D_MODEL, RNN_W), (0, 0)), _const_spec((D_MODEL, RNN_W), (0, 1)),
                  _const_spec((RNN_CONV, RNN_W)), _const_spec((1, RNN_W)),
                  _const_spec(gshape), _const_spec((1, RNN_W)),
                  _const_spec(gshape), _const_spec((1, RNN_W)),
                  _const_spec((1, RNN_W)), _const_spec((RNN_W, D_MODEL)),
                  _const_spec((nb, RNN_W)), _const_spec(cshape)],
        out_specs=[_row_spec(rows, D_MODEL), pl.BlockSpec((nb, RNN_W), lambda i: (0, 0)),
                   pl.BlockSpec(cshape, lambda i: (0, 0))],
        out_shape=[jax.ShapeDtypeStruct((n, D_MODEL), F32), jax.ShapeDtypeStruct((nb, RNN_W), F32),
                   jax.ShapeDtypeStruct(cshape, F32)],
        scratch_shapes=[pltpu.VMEM((rows, RNN_W), F32)] * 2,
        compiler_params=_params(),
        name="l1_mixer",
    )(x, w["norm_mix_1"], w["w_in_1"], w["w_in_1"], w["rnn_conv_w"], w["rnn_conv_b"],
      w["rnn_wa"], w["rnn_b_a"], w["rnn_wx"], w["rnn_b_x"], w["rnn_lam"], w["w_out_1"], h0, c0)


def _pack_gate(wblk):
    eye = jnp.eye(RNN_BLOCKS, dtype=wblk.dtype)
    dense = (wblk[:, :, None, :] * eye[:, None, :, None]).reshape(RNN_W, RNN_W)
    tiles = [[dense[wi * GATE_WIN + ni * LANES:wi * GATE_WIN + ni * LANES + GATE_K,
                    wi * GATE_WIN + ni * GATE_N:wi * GATE_WIN + (ni + 1) * GATE_N]
              for ni in range(GATE_WIN // GATE_N)] for wi in range(RNN_W // GATE_WIN)]
    return jnp.stack([jnp.stack(r) for r in tiles]).astype(BF16)


def _prep_weights(p):
    w = {}
    row = lambda v: v.reshape(1, -1).astype(F32)
    w_in = p["w_in_0"]
    c = 2 * GLA_K + 2 * GLA_V
    w["w_in_0"] = w_in.astype(BF16)
    w["w_lr"] = jnp.pad(w_in[:, c:c + GLA_RANK], ((0, 0), (0, LANES - GLA_RANK))).astype(BF16)
    c += GLA_RANK
    w["w_u"] = w_in[:, c:c + S5_W].astype(BF16)
    w["w_alpha"] = jnp.pad(p["w_alpha_0"], ((0, LANES - GLA_RANK), (0, 0))).astype(BF16)
    w["b_alpha"] = row(p["b_alpha_0"])
    w["norm_mix_0"] = row(p["norm_mix_0"])
    w["gla_norm"] = row(p["gla_norm_0"])

    are, aim, bbre, bbim = _s5_prep(p["s5_lam_re"], p["s5_lam_im"], p["s5_log_dt"], p["s5_b_re"],
                                    p["s5_b_im"])
    npack = S5_GROUPS // S5_PACK
    eye = jnp.eye(S5_PACK, dtype=F32)[None, :, None, :, None]
    grouped = lambda m: m.reshape(npack, S5_PACK, S5_H, S5_P)
    pack_b = lambda m: (grouped(m)[:, :, :, None, :] * eye).reshape(
        npack, S5_PACK * S5_H, S5_PACK * S5_P).astype(BF16)
    pack_c = lambda m: (jnp.swapaxes(grouped(m), 2, 3)[:, :, :, None, :] * eye).reshape(
        npack, S5_PACK * S5_P, S5_PACK * S5_H).astype(BF16)
    w["s5_are"] = are.reshape(1, S5_N)
    w["s5_aim"] = aim.reshape(1, S5_N)
    w["s5_bbre"] = pack_b(bbre)
    w["s5_bbim"] = pack_b(bbim)
    w["s5_cre"] = pack_c(p["s5_c_re"])
    w["s5_cim"] = pack_c(p["s5_c_im"])
    w["s5_d"] = row(p["s5_d"])
    w["s5_w_glu"] = p["s5_w_glu"].astype(BF16)
    w["s5_b_glu"] = row(p["s5_b_glu"])
    w["w_out_0"] = p["w_out_0"].astype(BF16)

    w["norm_mix_1"] = row(p["norm_mix_1"])
    w["w_in_1"] = p["w_in_1"].astype(BF16)
    w["rnn_conv_w"] = p["rnn_conv_w"].astype(F32)
    w["rnn_conv_b"] = row(p["rnn_conv_b"])
    w["rnn_wa"] = _pack_gate(p["rnn_w_a"])
    w["rnn_wx"] = _pack_gate(p["rnn_w_x"])
    w["rnn_b_a"] = row(p["rnn_b_a"])
    w["rnn_b_x"] = row(p["rnn_b_x"])
    w["rnn_lam"] = row(p["rnn_lam"])
    w["w_out_1"] = p["w_out_1"].astype(BF16)

    depth = p["norm_ffn"].shape[0]
    w["norm_ffn"] = p["norm_ffn"].reshape(depth, 1, D_MODEL)
    w["ffn_w_up"] = p["ffn_w_up"].astype(BF16)
    w["ffn_conv_w"] = p["ffn_conv_w"]
    w["ffn_conv_b"] = p["ffn_conv_b"].reshape(depth, 1, D_FF)
    w["ffn_w_down"] = p["ffn_w_down"].astype(BF16)
    w["norm_final"] = row(p["norm_final"])
    return w


def _tile_steps(nt):
    cap = lambda m: min(nt, m)
    return dict(l0_in=cap(64), gla=cap(32), l0_out=cap(64), ffn=cap(64), l1=cap(64))


def _trunk(x, w, st, nb, nt):
    tt = _tile_steps(nt)
    batch_major = x.ndim == 3
    q, k, v, gs, la, u = _l0_in(x, w, nb, tt["l0_in"])
    if nt == 1:
        og, s_gla = _gla_step(q, k, v, la, gs, w["gla_norm"], st["gla"])
    else:
        og, s_gla = _gla_chunk(q, k, v, la, gs, w["gla_norm"], st["gla"], nb, tt["gla"])
    x, s_re, s_im = _l0_out(u, og, x, w, st["s5_re"], st["s5_im"], nb, tt["l0_out"])
    x, fc0 = _ffn(x, w, 0, st["fc"][0], nb, tt["ffn"], False)
    x, h, rc = _l1(x, w, st["h"], st["rc"], nb, tt["l1"])
    x, fc1 = _ffn(x, w, 1, st["fc"][1], nb, tt["ffn"], True, batch_major_out=batch_major)
    return x, dict(gla=s_gla, s5_re=s_re, s5_im=s_im, h=h, rc=rc, fc=[fc0, fc1])


def _batch_major(cache, nb):
    jb, c = cache.shape
    return jnp.transpose(cache.reshape(jb // nb, nb, c), (1, 0, 2))


def kernel(x_prompt, x_sample, state_gla, state_s5_re, state_s5_im, state_rglru, cache_rglru_conv,
           cache_ffn_conv, meta_tokens, norm_mix_0, w_in_0, w_alpha_0, b_alpha_0, gla_norm_0,
           s5_lam_re, s5_lam_im, s5_log_dt, s5_b_re, s5_b_im, s5_c_re, s5_c_im, s5_d, s5_w_glu,
           s5_b_glu, w_out_0, norm_mix_1, w_in_1, rnn_conv_w, rnn_conv_b, rnn_w_a, rnn_b_a, rnn_w_x,
           rnn_b_x, rnn_lam, w_out_1, norm_ffn, ffn_w_up, ffn_conv_w, ffn_conv_b, ffn_w_down, norm_final):
    w = _prep_weights(dict(
        norm_mix_0=norm_mix_0, w_in_0=w_in_0, w_alpha_0=w_alpha_0, b_alpha_0=b_alpha_0,
        gla_norm_0=gla_norm_0, s5_lam_re=s5_lam_re, s5_lam_im=s5_lam_im, s5_log_dt=s5_log_dt,
        s5_b_re=s5_b_re, s5_b_im=s5_b_im, s5_c_re=s5_c_re, s5_c_im=s5_c_im, s5_d=s5_d,
        s5_w_glu=s5_w_glu, s5_b_glu=s5_b_glu, w_out_0=w_out_0, norm_mix_1=norm_mix_1, w_in_1=w_in_1,
        rnn_conv_w=rnn_conv_w, rnn_conv_b=rnn_conv_b, rnn_w_a=rnn_w_a, rnn_b_a=rnn_b_a,
        rnn_w_x=rnn_w_x, rnn_b_x=rnn_b_x, rnn_lam=rnn_lam, w_out_1=w_out_1, norm_ffn=norm_ffn,
        ffn_w_up=ffn_w_up, ffn_conv_w=ffn_conv_w, ffn_conv_b=ffn_conv_b, ffn_w_down=ffn_w_down,
        norm_final=norm_final))

    bp, seq, _ = x_prompt.shape
    bs = x_sample.shape[0]

    zeros = dict(gla=jnp.zeros((GLA_HEADS, GLA_DV, bp * GLA_DK), F32),
                 s5_re=jnp.zeros((bp, S5_N), F32), s5_im=jnp.zeros((bp, S5_N), F32),
                 h=jnp.zeros((bp, RNN_W), F32), rc=jnp.zeros(((RNN_CONV - 1) * bp, RNN_W), F32),
                 fc=[jnp.zeros(((FFN_CONV - 1) * bp, D_FF), F32)] * 2)
    x_meta = jnp.repeat(meta_tokens.astype(F32), bp, axis=0)
    _, st = _trunk(x_meta, w, zeros, bp, N_META)

    yp, st = _trunk(x_prompt, w, st, bp, seq)

    st_s = dict(gla=state_gla, s5_re=state_s5_re.reshape(bs, S5_N), s5_im=state_s5_im.reshape(bs, S5_N),
                h=state_rglru, rc=cache_rglru_conv.reshape(bs, (RNN_CONV - 1) * RNN_W),
                fc=[cache_ffn_conv[l].reshape(bs, (FFN_CONV - 1) * D_FF) for l in range(2)])
    ys, ss = _trunk(x_sample.reshape(bs, D_MODEL), w, st_s, bs, 1)
    ys = ys.reshape(bs, 1, D_MODEL)

    grp = lambda z, nb: z.reshape(nb, S5_GROUPS, S5_P)
    return (yp, ys, _gla_state_from_stacked(st["gla"]), ss["gla"], grp(st["s5_re"], bp), grp(ss["s5_re"], bs),
            grp(st["s5_im"], bp), grp(ss["s5_im"], bs), st["h"], ss["h"],
            _batch_major(st["rc"], bp), ss["rc"].reshape(bs, RNN_CONV - 1, RNN_W),
            jnp.stack([_batch_major(st["fc"][l], bp) for l in range(2)]),
            jnp.stack([ss["fc"][l].reshape(bs, FFN_CONV - 1, D_FF) for l in range(2)]))
```

```python
import functools

import jax
import jax.numpy as jnp
from jax import lax
from jax.experimental import pallas as pl
from jax.experimental.pallas import tpu as pltpu

F32 = jnp.float32
BF16 = jnp.bfloat16

D_MODEL = 1024
N_META = 16
EPS = 1e-6
F32_TINY = 1.1754944e-38
GLA_HEADS = 4
GLA_DK = 64
GLA_DV = 128
GLA_RANK = 16
GLA_TAU = 16.0
GLA_K = GLA_HEADS * GLA_DK
GLA_V = GLA_HEADS * GLA_DV
S5_GROUPS = 32
S5_H = 16
S5_P = 64
S5_W = S5_GROUPS * S5_H
S5_N = S5_GROUPS * S5_P
RNN_W = 1536
RNN_BLOCKS = 16
RNN_BW = RNN_W // RNN_BLOCKS
RNN_C = 8.0
RNN_CONV = 4
D_FF = 2816
FFN_CONV = 3

LANES = 128
FF_CHUNK = 256
L1_SPLIT = 2
S5_PACK = 8
GATE_WIN = 768
GATE_K = 512
GATE_N = 256
VMEM_LIMIT = 56 * 1024 * 1024


def _rms(x, g):
    return x * lax.rsqrt(jnp.mean(x * x, axis=-1, keepdims=True) + EPS) * g


def _sigmoid(x):
    return 0.5 * jnp.tanh(0.5 * x) + 0.5


def _sqrt_nonneg(t):
    return t * lax.rsqrt(jnp.maximum(t, F32_TINY))


def _mm(a, w):
    return jnp.dot(a.astype(BF16), w, preferred_element_type=F32)


def _const_spec(shape, index=None):
    idx = tuple(index) if index is not None else (0,) * len(shape)
    return pl.BlockSpec(shape, lambda i: idx, pipeline_mode=pl.Buffered(1))


def _row_spec(rows, cols):
    return pl.BlockSpec((rows, cols), lambda i: (i, 0))


def _seq_spec(nb, tt, cols):
    return pl.BlockSpec((nb, tt, cols), lambda i: (0, i, 0))


def _tm_scratch(rows, cols):
    return pltpu.VMEM((cols // LANES, rows, LANES), F32)


def _load_time_major(x_ref, tm_sc):
    if tm_sc is None:
        return x_ref[...]
    nb, tt, cols = x_ref.shape
    for b in range(nb):
        for j in range(cols // LANES):
            tm_sc[j, pl.ds(b, tt, stride=nb), :] = x_ref[b, :, j * LANES:(j + 1) * LANES]
    return jnp.concatenate([tm_sc[j] for j in range(cols // LANES)], axis=-1)


def _store_time_major(o_ref, val, tm_sc):
    if tm_sc is None:
        o_ref[...] = val
        return
    nb, tt, cols = o_ref.shape
    for j in range(cols // LANES):
        tm_sc[j] = val[:, j * LANES:(j + 1) * LANES]
    for b in range(nb):
        for j in range(cols // LANES):
            o_ref[b, :, j * LANES:(j + 1) * LANES] = tm_sc[j, pl.ds(b, tt, stride=nb), :]


def _cache_shape(nb, tt, taps, width):
    return (nb, taps * width) if tt == 1 else (taps * nb, width)


def _params(sem="arbitrary"):
    return pltpu.CompilerParams(dimension_semantics=(sem,), vmem_limit_bytes=VMEM_LIMIT)


def _s5_prep_kernel(lr_ref, li_ref, ldt_ref, brt_ref, bit_ref, are_ref, aim_ref, bbre_ref, bbim_ref):
    lr = lr_ref[...]
    li = li_ref[...]
    dt = jnp.exp(ldt_ref[...])
    mag = jnp.exp(lr * dt)
    ab_re = mag * jnp.cos(li * dt)
    ab_im = mag * jnp.sin(li * dt)
    den = lr * lr + li * li
    nr = ab_re - 1.0
    ni = ab_im
    f_re = (nr * lr + ni * li) / den
    f_im = (ni * lr - nr * li) / den
    are_ref[...] = ab_re
    aim_ref[...] = ab_im
    brt = brt_ref[...]
    bit = bit_ref[...]
    bbre_ref[...] = f_re[:, None, :] * brt - f_im[:, None, :] * bit
    bbim_ref[...] = f_re[:, None, :] * bit + f_im[:, None, :] * brt


def _s5_prep(lam_re, lam_im, log_dt, b_re, b_im):
    g, p, h = b_re.shape
    brt = jnp.transpose(b_re, (0, 2, 1))
    bit = jnp.transpose(b_im, (0, 2, 1))
    return pl.pallas_call(
        _s5_prep_kernel,
        out_shape=(jax.ShapeDtypeStruct((g, p), F32), jax.ShapeDtypeStruct((g, p), F32),
                   jax.ShapeDtypeStruct((g, h, p), F32), jax.ShapeDtypeStruct((g, h, p), F32)),
        name="s5_prep",
    )(lam_re, lam_im, log_dt.reshape(g, 1), brt, bit)


def _is_first():
    return pl.program_id(0) == 0


def _is_last():
    return pl.program_id(0) == pl.num_programs(0) - 1


def _side_spec(rows, cols, block=0):
    return pl.BlockSpec((rows, cols), lambda i: (block, 0))


def _rows_view(ref, rows):
    return ref.at[pl.ds(0, rows)]


def _l0_in_body(x, wts, outs):
    g_ref, wq_ref, wk_ref, wv_ref, wg_ref, wu_ref, wlr_ref, wal_ref, bal_ref = wts
    q_ref, k_ref, v_ref, gs_ref, la_ref, u_ref = outs
    xn = _rms(x, g_ref[...]).astype(BF16)
    q_ref[...] = _mm(xn, wq_ref[...]) * (GLA_DK ** -0.5)
    k_ref[...] = _mm(xn, wk_ref[...])
    v_ref[...] = _mm(xn, wv_ref[...])
    g = _mm(xn, wg_ref[...])
    gs_ref[...] = g * _sigmoid(g)
    u_ref[...] = _mm(xn, wu_ref[...])
    lr = _mm(xn, wlr_ref[...])
    pre = _mm(lr, wal_ref[...]) + bal_ref[...]
    la_ref[...] = jax.nn.log_sigmoid(pre) * (1.0 / GLA_TAU)


def _l0_in_kernel(*refs):
    x_ref, xs_ref = refs[0:2]
    wts = refs[2:11]
    outs, side_outs = refs[11:17], refs[17:23]
    tm_sc = refs[23]

    @pl.when(_is_first())
    def _():
        _l0_in_body(xs_ref[...], wts, side_outs)

    _l0_in_body(_load_time_major(x_ref, tm_sc), wts, outs)


L0_IN_COLS = (GLA_K, GLA_K, GLA_V, GLA_V, GLA_K, S5_W)


def _l0_in(x, x_side, w, tt):
    nb, nt, _ = x.shape
    rows, n, ns = nb * tt, nb * nt, x_side.shape[0]
    res = pl.pallas_call(
        _l0_in_kernel,
        grid=(n // rows,),
        in_specs=[_seq_spec(nb, tt, D_MODEL), _const_spec((ns, D_MODEL)), _const_spec((1, D_MODEL)),
                  _const_spec((D_MODEL, GLA_K), (0, 0)), _const_spec((D_MODEL, GLA_K), (0, 1)),
                  _const_spec((D_MODEL, GLA_V), (0, 1)), _const_spec((D_MODEL, GLA_V), (0, 2)),
                  _const_spec((D_MODEL, S5_W)), _const_spec((D_MODEL, LANES)),
                  _const_spec((LANES, GLA_K)), _const_spec((1, GLA_K))],
        out_specs=[_row_spec(rows, c) for c in L0_IN_COLS] + [_side_spec(ns, c) for c in L0_IN_COLS],
        out_shape=[jax.ShapeDtypeStruct((n, c), F32) for c in L0_IN_COLS]
                  + [jax.ShapeDtypeStruct((ns, c), F32) for c in L0_IN_COLS],
        scratch_shapes=[_tm_scratch(rows, D_MODEL)],
        compiler_params=_params(),
        name="l0_in",
    )(x, x_side, w["norm_mix_0"], w["w_in_0"], w["w_in_0"], w["w_in_0"], w["w_in_0"], w["w_u"], w["w_lr"],
      w["w_alpha"], w["b_alpha"])
    return res[:6], res[6:]


def _gla_chunk_body(ins, gn_ref, o_ref, st_ref, b_sc, nb, c):
    q_ref, k_ref, v_ref, la_ref, gs_ref = ins
    rows = nb * c
    seq_mask = nb - 1

    def cum_body(t, run):
        rws = pl.ds(pl.multiple_of(t * nb, nb), nb)
        run = run + la_ref[rws, :]
        b_sc[rws, :] = run
        return run

    bl = lax.fori_loop(0, c, cum_body, jnp.zeros((nb, GLA_K), F32))
    b = b_sc[...]
    k = k_ref[...]
    qt = q_ref[...] * jnp.exp(b)
    kt = k * jnp.exp(-b)
    kh = k * jnp.exp(jnp.concatenate([bl] * c, axis=0) - b)
    gam = jnp.exp(bl)

    ri = lax.broadcasted_iota(jnp.int32, (rows, rows), 0)
    ci = lax.broadcasted_iota(jnp.int32, (rows, rows), 1)
    pair_ok = (ri >= ci) & (((ri - ci) & seq_mask) == 0)
    xw = nb * GLA_DK
    own_blk = (lax.broadcasted_iota(jnp.int32, (rows, xw), 1) // GLA_DK
               == (lax.broadcasted_iota(jnp.int32, (rows, xw), 0) & seq_mask))
    own_blk_seq = (lax.broadcasted_iota(jnp.int32, (nb, xw), 1) // GLA_DK
                   == lax.broadcasted_iota(jnp.int32, (nb, xw), 0))
    reps = xw // LANES

    def head_dup(z, h):
        blk = z[:, (h // 2) * LANES:(h // 2 + 1) * LANES]
        rolled = pltpu.roll(blk, GLA_DK, axis=1)
        low = lax.broadcasted_iota(jnp.int32, blk.shape, 1) < GLA_DK
        return jnp.where(low, blk, rolled) if h % 2 == 0 else jnp.where(low, rolled, blk)

    def expand(zd, own):
        return jnp.where(own, jnp.concatenate([zd] * reps, axis=1), 0.0)

    for h in range(GLA_HEADS):
        vs = slice(h * GLA_DV, (h + 1) * GLA_DV)
        qd, ktd, khd = head_dup(qt, h), head_dup(kt, h), head_dup(kh, h)
        vb = v_ref[:, vs].astype(BF16)
        st = st_ref[h]
        att = lax.dot_general(qd[:, :GLA_DK].astype(BF16), ktd[:, :GLA_DK].astype(BF16),
                              (((1,), (1,)), ((), ())), preferred_element_type=F32)
        att = jnp.where(pair_ok, att, 0.0).astype(BF16)
        o = jnp.dot(att, vb, preferred_element_type=F32) + lax.dot_general(
            expand(qd, own_blk).astype(BF16), st.astype(BF16), (((1,), (1,)), ((), ())),
            preferred_element_type=F32)
        upd = lax.dot_general(vb, expand(khd, own_blk).astype(BF16), (((0,), (0,)), ((), ())),
                              preferred_element_type=F32)
        gam_row = jnp.sum(expand(head_dup(gam, h), own_blk_seq), axis=0, keepdims=True)
        st_ref[h] = st * gam_row + upd
        o_ref[:, vs] = _rms(o, gn_ref[:, vs]) * gs_ref[:, vs]


def _gla_chunk_kernel(*refs, nb, c, c_meta):
    ins, meta_ins = refs[0:5], refs[5:10]
    gn_ref, o_ref, om_ref, st_ref, b_sc = refs[10:15]

    @pl.when(_is_first())
    def _():
        st_ref[...] = jnp.zeros(st_ref.shape, F32)
        _gla_chunk_body(meta_ins, gn_ref, om_ref, st_ref, _rows_view(b_sc, nb * c_meta), nb, c_meta)

    _gla_chunk_body(ins, gn_ref, o_ref, st_ref, b_sc, nb, c)


def _gla_chunk(main, side, gn, nb, c, c_meta):
    assert nb & (nb - 1) == 0 and (nb * GLA_DK) % LANES == 0
    q, k, v, la, gs = main
    n = q.shape[0]
    rows, mrows = nb * c, nb * c_meta
    sshape = (GLA_HEADS, GLA_DV, nb * GLA_DK)
    cols = (GLA_K, GLA_K, GLA_V, GLA_K, GLA_V)
    kern = functools.partial(_gla_chunk_kernel, nb=nb, c=c, c_meta=c_meta)
    qs, ks, vs, las, gss = side
    return pl.pallas_call(
        kern,
        grid=(n // rows,),
        in_specs=[_row_spec(rows, w_) for w_ in cols] + [_side_spec(mrows, w_) for w_ in cols]
                 + [_const_spec((1, GLA_V))],
        out_specs=[_row_spec(rows, GLA_V), _side_spec(mrows, GLA_V),
                   pl.BlockSpec(sshape, lambda i: (0, 0, 0))],
        out_shape=[jax.ShapeDtypeStruct((n, GLA_V), F32), jax.ShapeDtypeStruct((mrows, GLA_V), F32),
                   jax.ShapeDtypeStruct(sshape, F32)],
        scratch_shapes=[pltpu.VMEM((rows, GLA_K), F32)],
        compiler_params=_params(),
        name="gla_chunk",
    )(q, k, v, la, gs, qs, ks, vs, las, gss, gn)


def _gla_state_from_stacked(st):
    nb = st.shape[2] // GLA_DK
    return jnp.transpose(st.reshape(GLA_HEADS, GLA_DV, nb, GLA_DK), (2, 0, 3, 1))


def _gla_step_kernel(q_ref, k_ref, la_ref, v_ref, gs_ref, gn_ref, s0_ref, o_ref, s_ref):
    qT = q_ref[...].T
    kT = k_ref[...].T
    aT = jnp.exp(la_ref[...]).T
    o_rows = []
    for b in range(q_ref.shape[0]):
        s_new = aT[:, b:b + 1] * s0_ref[b] + kT[:, b:b + 1] * v_ref[b:b + 1, :]
        s_ref[b] = s_new
        o_rows.append(jnp.sum(qT[:, b:b + 1] * s_new, axis=0, keepdims=True))
    o = jnp.concatenate(o_rows, axis=0)
    o_ref[...] = _rms(o, gn_ref[...]) * gs_ref[...]


def _gla_step(side, gn, s0, row0):
    nb = s0.shape[0]
    blk = row0 // nb
    q, k, v, la, gs = side
    heads = lambda z: jnp.transpose(z[row0:row0 + nb].reshape(nb, GLA_HEADS, GLA_DK), (1, 0, 2))
    hspec = pl.BlockSpec((None, nb, GLA_DK), lambda h: (h, 0, 0))
    vspec = pl.BlockSpec((nb, GLA_DV), lambda h: (blk, h))
    ospec = pl.BlockSpec((nb, GLA_DV), lambda h: (0, h))
    sspec = pl.BlockSpec((nb, None, GLA_DK, GLA_DV), lambda h: (0, h, 0, 0))
    return pl.pallas_call(
        _gla_step_kernel,
        grid=(GLA_HEADS,),
        in_specs=[hspec, hspec, hspec, vspec, vspec, pl.BlockSpec((1, GLA_DV), lambda h: (0, h)), sspec],
        out_specs=[ospec, sspec],
        out_shape=[jax.ShapeDtypeStruct((nb, GLA_V), F32),
                   jax.ShapeDtypeStruct((nb, GLA_HEADS, GLA_DK, GLA_DV), F32)],
        compiler_params=_params("parallel"),
        name="gla_step",
    )(heads(q), heads(k), heads(la), v, gs, gn, s0)


def _l0_out_body(u_ref, og_ref, x, wts, xo_ref, xre_ref, xim_ref, sr_sc, si_sc, nb, tt):
    (bbre_ref, bbim_ref, are_ref, aim_ref, cre_ref, cim_ref, d_ref, wglu_ref, bglu_ref, woa_ref,
     wob_ref) = wts
    u = u_ref[...]
    ub = u.astype(BF16)
    kin = S5_PACK * S5_H
    kst = S5_PACK * S5_P
    npack = S5_GROUPS // S5_PACK
    ys = []
    for j in range(npack):
        cs = slice(j * kst, (j + 1) * kst)
        uj = ub[:, j * kin:(j + 1) * kin]
        bur = jnp.dot(uj, bbre_ref[j], preferred_element_type=F32)
        bui = jnp.dot(uj, bbim_ref[j], preferred_element_type=F32)
        ar = jnp.broadcast_to(are_ref[:, cs], (nb, kst))
        ai = jnp.broadcast_to(aim_ref[:, cs], (nb, kst))
        xr = xre_ref[:, cs]
        xi = xim_ref[:, cs]
        for t in range(tt):
            rows = slice(t * nb, (t + 1) * nb)
            xr, xi = ar * xr - ai * xi + bur[rows], ar * xi + ai * xr + bui[rows]
            sr_sc[rows, cs] = xr
            si_sc[rows, cs] = xi
        xre_ref[:, cs] = xr
        xim_ref[:, cs] = xi
        ys.append(jnp.dot(sr_sc[:, cs].astype(BF16), cre_ref[j], preferred_element_type=F32)
                  - jnp.dot(si_sc[:, cs].astype(BF16), cim_ref[j], preferred_element_type=F32))
    y = jnp.concatenate(ys, axis=-1) + d_ref[...] * u
    y = jax.nn.gelu(y)
    y = y * _sigmoid(_mm(y, wglu_ref[...]) + bglu_ref[...])
    xo_ref[...] = x + _mm(og_ref[...], woa_ref[...]) + _mm(y, wob_ref[...])


def _l0_out_kernel(*refs, nb, tt, tt_meta, nb_s):
    u_ref, og_ref, x_ref = refs[0:3]
    um_ref, ogm_ref, xm_ref = refs[3:6]
    us_ref, ogs_ref, xs_ref, xr0s_ref, xi0s_ref = refs[6:11]
    wts = refs[11:22]
    xo_ref, xre_ref, xim_ref, xom_ref, xos_ref, xres_ref, xims_ref = refs[22:29]
    sr_sc, si_sc, tm_sc = refs[29:32]

    @pl.when(_is_first())
    def _():
        xre_ref[...] = jnp.zeros(xre_ref.shape, F32)
        xim_ref[...] = jnp.zeros(xim_ref.shape, F32)
        mrows = nb * tt_meta
        _l0_out_body(um_ref, ogm_ref, xm_ref[...], wts, xom_ref, xre_ref, xim_ref,
                     _rows_view(sr_sc, mrows), _rows_view(si_sc, mrows), nb, tt_meta)

    _l0_out_body(u_ref, og_ref, _load_time_major(x_ref, tm_sc), wts, xo_ref, xre_ref, xim_ref,
                 sr_sc, si_sc, nb, tt)

    @pl.when(_is_last())
    def _():
        xres_ref[...] = xr0s_ref[...]
        xims_ref[...] = xi0s_ref[...]
        _l0_out_body(us_ref, ogs_ref, xs_ref[...], wts, xos_ref, xres_ref, xims_ref,
                     _rows_view(sr_sc, nb_s), _rows_view(si_sc, nb_s), nb_s, 1)


def _l0_out(u, og, x, u_side, og_meta, og_samp, x_side, xr0_s, xi0_s, w, tt, tt_meta):
    nb = x.shape[0]
    n = u.shape[0]
    rows, mrows, nb_s = nb * tt, nb * tt_meta, xr0_s.shape[0]
    assert mrows == nb_s and max(mrows, nb_s) <= rows
    npack = S5_GROUPS // S5_PACK
    kin, kst = S5_PACK * S5_H, S5_PACK * S5_P
    kern = functools.partial(_l0_out_kernel, nb=nb, tt=tt, tt_meta=tt_meta, nb_s=nb_s)
    st_spec = pl.BlockSpec((nb, S5_N), lambda i: (0, 0))
    sts_spec = pl.BlockSpec((nb_s, S5_N), lambda i: (0, 0))
    f32 = lambda *shape: jax.ShapeDtypeStruct(shape, F32)
    return pl.pallas_call(
        kern,
        grid=(n // rows,),
        in_specs=[_row_spec(rows, S5_W), _row_spec(rows, GLA_V), _seq_spec(nb, tt, D_MODEL),
                  _side_spec(mrows, S5_W, 0), _side_spec(mrows, GLA_V), _side_spec(mrows, D_MODEL, 0),
                  _side_spec(nb_s, S5_W, 1), _side_spec(nb_s, GLA_V), _side_spec(nb_s, D_MODEL, 1),
                  _const_spec((nb_s, S5_N)), _const_spec((nb_s, S5_N)),
                  _const_spec((npack, kin, kst)), _const_spec((npack, kin, kst)),
                  _const_spec((1, S5_N)), _const_spec((1, S5_N)),
                  _const_spec((npack, kst, kin)), _const_spec((npack, kst, kin)),
                  _const_spec((1, S5_W)), _const_spec((S5_W, S5_W)), _const_spec((1, S5_W)),
                  _const_spec((GLA_V, D_MODEL), (0, 0)), _const_spec((S5_W, D_MODEL), (1, 0))],
        out_specs=[_row_spec(rows, D_MODEL), st_spec, st_spec, _side_spec(mrows, D_MODEL),
                   _side_spec(nb_s, D_MODEL), sts_spec, sts_spec],
        out_shape=[f32(n, D_MODEL), f32(nb, S5_N), f32(nb, S5_N), f32(mrows, D_MODEL), f32(nb_s, D_MODEL),
                   f32(nb_s, S5_N), f32(nb_s, S5_N)],
        scratch_shapes=[pltpu.VMEM((rows, S5_N), F32)] * 2 + [_tm_scratch(rows, D_MODEL)],
        compiler_params=_params(),
        name="l0_out",
    )(u, og, x, u_side, og_meta, x_side, u_side, og_samp, x_side, xr0_s, xi0_s,
      w["s5_bbre"], w["s5_bbim"], w["s5_are"], w["s5_aim"], w["s5_cre"], w["s5_cim"],
      w["s5_d"], w["s5_w_glu"], w["s5_b_glu"], w["w_out_0"], w["w_out_0"])


def _ffn_body(x, wts, c0_ref, c_ref, hm_sc, nb, tt, final):
    g_ref, wg_ref, wv_ref, cw_ref, cb_ref, wd_ref, gf_ref = wts
    rows = nb * tt
    xn = _rms(x, g_ref[...]).astype(BF16)
    for ci in range(D_FF // FF_CHUNK):
        cs = slice(ci * FF_CHUNK, (ci + 1) * FF_CHUNK)
        gate = jnp.dot(xn, wg_ref[:, cs], preferred_element_type=F32)
        val = jnp.dot(xn, wv_ref[:, cs], preferred_element_type=F32)
        if tt == 1:
            taps = [c0_ref[:, j * D_FF + ci * FF_CHUNK:j * D_FF + (ci + 1) * FF_CHUNK]
                    for j in range(FFN_CONV - 1)] + [gate]
            for j in range(FFN_CONV - 1):
                c_ref[:, j * D_FF + ci * FF_CHUNK:j * D_FF + (ci + 1) * FF_CHUNK] = taps[j + 1]
        else:
            ext = jnp.concatenate([c_ref[:, cs], gate], axis=0)
            taps = [ext[j * nb:j * nb + rows] for j in range(FFN_CONV)]
            c_ref[:, cs] = ext[tt * nb:(tt + FFN_CONV - 1) * nb]
        y = cb_ref[:, cs] + taps[0] * cw_ref[0:1, cs]
        for j in range(1, FFN_CONV):
            y = y + taps[j] * cw_ref[j:j + 1, cs]
        hm_sc[:, cs] = (jax.nn.gelu(y) * val).astype(BF16)
    out = x + jnp.dot(hm_sc[...], wd_ref[...], preferred_element_type=F32)
    return _rms(out, gf_ref[...]) if final else out


def _ffn_kernel(*refs, nb, tt, tt_meta, nb_s, final, batch_major_out):
    x_ref, xm_ref, xs_ref, c0s_ref = refs[0:4]
    wts = refs[4:11]
    xo_ref, c_ref, xom_ref, xos_ref, cs_ref = refs[11:16]
    hm_sc = refs[16]
    tm_sc = refs[17] if batch_major_out else None

    @pl.when(_is_first())
    def _():
        c_ref[...] = jnp.zeros(c_ref.shape, F32)
        xom_ref[...] = _ffn_body(xm_ref[...], wts, None, c_ref, _rows_view(hm_sc, nb * tt_meta), nb, tt_meta,
                                 final)

    _store_time_major(xo_ref, _ffn_body(x_ref[...], wts, None, c_ref, hm_sc, nb, tt, final), tm_sc)

    @pl.when(_is_last())
    def _():
        xos_ref[...] = _ffn_body(xs_ref[...], wts, c0s_ref, cs_ref, _rows_view(hm_sc, nb_s), nb_s, 1, final)


def _ffn(x, x_meta, x_samp, c0_s, w, layer, nb, tt, tt_meta, final, batch_major_out=False):
    n = x.shape[0]
    rows, mrows, nb_s = nb * tt, nb * tt_meta, x_samp.shape[0]
    kern = functools.partial(_ffn_kernel, nb=nb, tt=tt, tt_meta=tt_meta, nb_s=nb_s, final=final,
                             batch_major_out=batch_major_out)
    if batch_major_out:
        o_spec, o_shape = _seq_spec(nb, tt, D_MODEL), (nb, n // nb, D_MODEL)
    else:
        o_spec, o_shape = _row_spec(rows, D_MODEL), (n, D_MODEL)
    cshape = _cache_shape(nb, tt, FFN_CONV - 1, D_FF)
    cs_shape = _cache_shape(nb_s, 1, FFN_CONV - 1, D_FF)
    f32 = lambda *shape: jax.ShapeDtypeStruct(shape, F32)
    return pl.pallas_call(
        kern,
        grid=(n // rows,),
        in_specs=[_row_spec(rows, D_MODEL), _side_spec(mrows, D_MODEL), _side_spec(nb_s, D_MODEL),
                  _const_spec(cs_shape),
                  _const_spec((None, 1, D_MODEL), (layer, 0, 0)),
                  _const_spec((None, D_MODEL, D_FF), (layer, 0, 0)),
                  _const_spec((None, D_MODEL, D_FF), (layer, 0, 1)),
                  _const_spec((None, FFN_CONV, D_FF), (layer, 0, 0)),
                  _const_spec((None, 1, D_FF), (layer, 0, 0)),
                  _const_spec((None, D_FF, D_MODEL), (layer, 0, 0)), _const_spec((1, D_MODEL))],
        out_specs=[o_spec, pl.BlockSpec(cshape, lambda i: (0, 0)), _side_spec(mrows, D_MODEL),
                   _side_spec(nb_s, D_MODEL), pl.BlockSpec(cs_shape, lambda i: (0, 0))],
        out_shape=[f32(*o_shape), f32(*cshape), f32(mrows, D_MODEL), f32(nb_s, D_MODEL), f32(*cs_shape)],
        scratch_shapes=[pltpu.VMEM((rows, D_FF), BF16)]
                       + ([_tm_scratch(rows, D_MODEL)] if batch_major_out else []),
        compiler_params=_params(),
        name="ffn%d" % layer,
    )(x, x_meta, x_samp, c0_s, w["norm_ffn"], w["ffn_w_up"], w["ffn_w_up"], w["ffn_conv_w"], w["ffn_conv_b"],
      w["ffn_w_down"], w["norm_final"])


def _l1_body(x_ref, wts, xo_ref, h_ref, c0_ref, c_ref, a_sc, b_sc, nb, tt):
    g_ref, wgt_ref, wxr_ref, cw_ref, cb_ref, wa_ref, ba_ref, wx_ref, bx_ref, lam_ref, wo_ref = wts
    sp = jax.nn.softplus(-lam_ref[...])
    nsplit = L1_SPLIT if tt % L1_SPLIT == 0 else 1
    th = tt // nsplit
    rows = nb * th
    carry = c_ref[...] if tt > 1 else None
    h = h_ref[...]
    for part in range(nsplit):
        prow = slice(part * rows, (part + 1) * rows)
        x = x_ref[prow, :]
        xn = _rms(x, g_ref[...]).astype(BF16)
        xr = jnp.dot(xn, wxr_ref[...], preferred_element_type=F32)
        if tt == 1:
            taps = [c0_ref[:, j * RNN_W:(j + 1) * RNN_W] for j in range(RNN_CONV - 1)] + [xr]
            for j in range(RNN_CONV - 1):
                c_ref[:, j * RNN_W:(j + 1) * RNN_W] = taps[j + 1]
        else:
            ext = jnp.concatenate([carry, xr], axis=0)
            taps = [ext[j * nb:j * nb + rows] for j in range(RNN_CONV)]
            carry = ext[th * nb:(th + RNN_CONV - 1) * nb]
        xc = cb_ref[...] + taps[0] * cw_ref[0:1, :]
        for j in range(1, RNN_CONV):
            xc = xc + taps[j] * cw_ref[j:j + 1, :]

        xcb = xc.astype(BF16)
        rs, gs = [], []
        for wi in range(RNN_W // GATE_WIN):
            for ni in range(GATE_WIN // GATE_N):
                k0 = wi * GATE_WIN + ni * LANES
                lhs = xcb[:, k0:k0 + GATE_K]
                rs.append(jnp.dot(lhs, wa_ref[wi, ni], preferred_element_type=F32))
                gs.append(jnp.dot(lhs, wx_ref[wi, ni], preferred_element_type=F32))
        r = _sigmoid(jnp.concatenate(rs, axis=-1) + ba_ref[...])
        ig = _sigmoid(jnp.concatenate(gs, axis=-1) + bx_ref[...])
        log_a = (-RNN_C) * r * sp
        a = jnp.exp(log_a)
        a_sc[prow, :] = a
        b_sc[prow, :] = _sqrt_nonneg(jnp.tanh(-log_a) * (a * a + 1.0)) * (ig * xc)
        gg = jax.nn.gelu(jnp.dot(xn, wgt_ref[...], preferred_element_type=F32))

        for t in range(part * th, (part + 1) * th):
            rws = slice(t * nb, (t + 1) * nb)
            h = a_sc[rws, :] * h + b_sc[rws, :]
            b_sc[rws, :] = h
        xo_ref[prow, :] = x + _mm(b_sc[prow, :] * gg, wo_ref[...])
    if tt > 1:
        c_ref[...] = carry
    h_ref[...] = h


def _l1_kernel(*refs, nb, tt, tt_meta, nb_s):
    x_ref, xm_ref, xs_ref, h0s_ref, c0s_ref = refs[0:5]
    wts = refs[5:16]
    xo_ref, h_ref, c_ref, xom_ref, xos_ref, hs_ref, cs_ref = refs[16:23]
    a_sc, b_sc = refs[23:25]

    @pl.when(_is_first())
    def _():
        h_ref[...] = jnp.zeros(h_ref.shape, F32)
        c_ref[...] = jnp.zeros(c_ref.shape, F32)
        mrows = nb * tt_meta
        _l1_body(xm_ref, wts, xom_ref, h_ref, None, c_ref, _rows_view(a_sc, mrows), _rows_view(b_sc, mrows),
                 nb, tt_meta)

    _l1_body(x_ref, wts, xo_ref, h_ref, None, c_ref, a_sc, b_sc, nb, tt)

    @pl.when(_is_last())
    def _():
        hs_ref[...] = h0s_ref[...]
        _l1_body(xs_ref, wts, xos_ref, hs_ref, c0s_ref, cs_ref, _rows_view(a_sc, nb_s), _rows_view(b_sc, nb_s),
                 nb_s, 1)


def _l1(x, x_meta, x_samp, h0_s, c0_s, w, nb, tt, tt_meta):
    n = x.shape[0]
    rows, mrows, nb_s = nb * tt, nb * tt_meta, x_samp.shape[0]
    kern = functools.partial(_l1_kernel, nb=nb, tt=tt, tt_meta=tt_meta, nb_s=nb_s)
    cshape = _cache_shape(nb, tt, RNN_CONV - 1, RNN_W)
    cs_shape = _cache_shape(nb_s, 1, RNN_CONV - 1, RNN_W)
    gshape = (RNN_W // GATE_WIN, GATE_WIN // GATE_N, GATE_K, GATE_N)
    f32 = lambda *shape: jax.ShapeDtypeStruct(shape, F32)
    return pl.pallas_call(
        kern,
        grid=(n // rows,),
        in_specs=[_row_spec(rows, D_MODEL), _side_spec(mrows, D_MODEL), _side_spec(nb_s, D_MODEL),
                  _const_spec((nb_s, RNN_W)), _const_spec(cs_shape),
                  _const_spec((1, D_MODEL)),
                  _const_spec((D_MODEL, RNN_W), (0, 0)), _const_spec((D_MODEL, RNN_W), (0, 1)),
                  _const_spec((RNN_CONV, RNN_W)), _const_spec((1, RNN_W)),
                  _const_spec(gshape), _const_spec((1, RNN_W)),
                  _const_spec(gshape), _const_spec((1, RNN_W)),
                  _const_spec((1, RNN_W)), _const_spec((RNN_W, D_MODEL))],
        out_specs=[_row_spec(rows, D_MODEL), pl.BlockSpec((nb, RNN_W), lambda i: (0, 0)),
                   pl.BlockSpec(cshape, lambda i: (0, 0)), _side_spec(mrows, D_MODEL),
                   _side_spec(nb_s, D_MODEL), pl.BlockSpec((nb_s, RNN_W), lambda i: (0, 0)),
                   pl.BlockSpec(cs_shape, lambda i: (0, 0))],
        out_shape=[f32(n, D_MODEL), f32(nb, RNN_W), f32(*cshape), f32(mrows, D_MODEL), f32(nb_s, D_MODEL),
                   f32(nb_s, RNN_W), f32(*cs_shape)],
        scratch_shapes=[pltpu.VMEM((rows, RNN_W), F32)] * 2,
        compiler_params=_params(),
        name="l1_mixer",
    )(x, x_meta, x_samp, h0_s, c0_s, w["norm_mix_1"], w["w_in_1"], w["w_in_1"], w["rnn_conv_w"],
      w["rnn_conv_b"], w["rnn_wa"], w["rnn_b_a"], w["rnn_wx"], w["rnn_b_x"], w["rnn_lam"], w["w_out_1"])


def _pack_gate(wblk):
    eye = jnp.eye(RNN_BLOCKS, dtype=wblk.dtype)
    dense = (wblk[:, :, None, :] * eye[:, None, :, None]).reshape(RNN_W, RNN_W)
    tiles = [[dense[wi * GATE_WIN + ni * LANES:wi * GATE_WIN + ni * LANES + GATE_K,
                    wi * GATE_WIN + ni * GATE_N:wi * GATE_WIN + (ni + 1) * GATE_N]
              for ni in range(GATE_WIN // GATE_N)] for wi in range(RNN_W // GATE_WIN)]
    return jnp.stack([jnp.stack(r) for r in tiles]).astype(BF16)


def _prep_weights(p):
    w = {}
    row = lambda v: v.reshape(1, -1).astype(F32)
    w_in = p["w_in_0"]
    c = 2 * GLA_K + 2 * GLA_V
    w["w_in_0"] = w_in.astype(BF16)
    w["w_lr"] = jnp.pad(w_in[:, c:c + GLA_RANK], ((0, 0), (0, LANES - GLA_RANK))).astype(BF16)
    c += GLA_RANK
    w["w_u"] = w_in[:, c:c + S5_W].astype(BF16)
    w["w_alpha"] = jnp.pad(p["w_alpha_0"], ((0, LANES - GLA_RANK), (0, 0))).astype(BF16)
    w["b_alpha"] = row(p["b_alpha_0"])
    w["norm_mix_0"] = row(p["norm_mix_0"])
    w["gla_norm"] = row(p["gla_norm_0"])

    are, aim, bbre, bbim = _s5_prep(p["s5_lam_re"], p["s5_lam_im"], p["s5_log_dt"], p["s5_b_re"],
                                    p["s5_b_im"])
    npack = S5_GROUPS // S5_PACK
    eye = jnp.eye(S5_PACK, dtype=F32)[None, :, None, :, None]
    grouped = lambda m: m.reshape(npack, S5_PACK, S5_H, S5_P)
    pack_b = lambda m: (grouped(m)[:, :, :, None, :] * eye).reshape(
        npack, S5_PACK * S5_H, S5_PACK * S5_P).astype(BF16)
    pack_c = lambda m: (jnp.swapaxes(grouped(m), 2, 3)[:, :, :, None, :] * eye).reshape(
        npack, S5_PACK * S5_P, S5_PACK * S5_H).astype(BF16)
    w["s5_are"] = are.reshape(1, S5_N)
    w["s5_aim"] = aim.reshape(1, S5_N)
    w["s5_bbre"] = pack_b(bbre)
    w["s5_bbim"] = pack_b(bbim)
    w["s5_cre"] = pack_c(p["s5_c_re"])
    w["s5_cim"] = pack_c(p["s5_c_im"])
    w["s5_d"] = row(p["s5_d"])
    w["s5_w_glu"] = p["s5_w_glu"].astype(BF16)
    w["s5_b_glu"] = row(p["s5_b_glu"])
    w["w_out_0"] = p["w_out_0"].astype(BF16)

    w["norm_mix_1"] = row(p["norm_mix_1"])
    w["w_in_1"] = p["w_in_1"].astype(BF16)
    w["rnn_conv_w"] = p["rnn_conv_w"].astype(F32)
    w["rnn_conv_b"] = row(p["rnn_conv_b"])
    w["rnn_wa"] = _pack_gate(p["rnn_w_a"])
    w["rnn_wx"] = _pack_gate(p["rnn_w_x"])
    w["rnn_b_a"] = row(p["rnn_b_a"])
    w["rnn_b_x"] = row(p["rnn_b_x"])
    w["rnn_lam"] = row(p["rnn_lam"])
    w["w_out_1"] = p["w_out_1"].astype(BF16)

    depth = p["norm_ffn"].shape[0]
    w["norm_ffn"] = p["norm_ffn"].reshape(depth, 1, D_MODEL)
    w["ffn_w_up"] = p["ffn_w_up"].astype(BF16)
    w["ffn_conv_w"] = p["ffn_conv_w"]
    w["ffn_conv_b"] = p["ffn_conv_b"].reshape(depth, 1, D_FF)
    w["ffn_w_down"] = p["ffn_w_down"].astype(BF16)
    w["norm_final"] = row(p["norm_final"])
    return w


def _tile_steps():
    return dict(l0_in=64, gla=32, l0_out=64, ffn=64, l1=64)


def _batch_major(cache, nb):
    jb, c = cache.shape
    return jnp.transpose(cache.reshape(jb // nb, nb, c), (1, 0, 2))


def kernel(x_prompt, x_sample, state_gla, state_s5_re, state_s5_im, state_rglru, cache_rglru_conv,
           cache_ffn_conv, meta_tokens, norm_mix_0, w_in_0, w_alpha_0, b_alpha_0, gla_norm_0,
           s5_lam_re, s5_lam_im, s5_log_dt, s5_b_re, s5_b_im, s5_c_re, s5_c_im, s5_d, s5_w_glu,
           s5_b_glu, w_out_0, norm_mix_1, w_in_1, rnn_conv_w, rnn_conv_b, rnn_w_a, rnn_b_a, rnn_w_x,
           rnn_b_x, rnn_lam, w_out_1, norm_ffn, ffn_w_up, ffn_conv_w, ffn_conv_b, ffn_w_down, norm_final):
    w = _prep_weights(dict(
        norm_mix_0=norm_mix_0, w_in_0=w_in_0, w_alpha_0=w_alpha_0, b_alpha_0=b_alpha_0,
        gla_norm_0=gla_norm_0, s5_lam_re=s5_lam_re, s5_lam_im=s5_lam_im, s5_log_dt=s5_log_dt,
        s5_b_re=s5_b_re, s5_b_im=s5_b_im, s5_c_re=s5_c_re, s5_c_im=s5_c_im, s5_d=s5_d,
        s5_w_glu=s5_w_glu, s5_b_glu=s5_b_glu, w_out_0=w_out_0, norm_mix_1=norm_mix_1, w_in_1=w_in_1,
        rnn_conv_w=rnn_conv_w, rnn_conv_b=rnn_conv_b, rnn_w_a=rnn_w_a, rnn_b_a=rnn_b_a,
        rnn_w_x=rnn_w_x, rnn_b_x=rnn_b_x, rnn_lam=rnn_lam, w_out_1=w_out_1, norm_ffn=norm_ffn,
        ffn_w_up=ffn_w_up, ffn_conv_w=ffn_conv_w, ffn_conv_b=ffn_conv_b, ffn_w_down=ffn_w_down,
        norm_final=norm_final))

    bp = x_prompt.shape[0]
    bs = x_sample.shape[0]
    tt = _tile_steps()
    mrows = bp * N_META
    assert mrows == bs

    x_side = jnp.concatenate([jnp.repeat(meta_tokens.astype(F32), bp, axis=0),
                              x_sample.reshape(bs, D_MODEL)], axis=0)
    s5_re_s = state_s5_re.reshape(bs, S5_N)
    s5_im_s = state_s5_im.reshape(bs, S5_N)
    rc_s = cache_rglru_conv.reshape(bs, (RNN_CONV - 1) * RNN_W)
    fc_s = [cache_ffn_conv[l].reshape(bs, (FFN_CONV - 1) * D_FF) for l in range(2)]

    main, side = _l0_in(x_prompt, x_side, w, tt["l0_in"])
    gla_in = lambda z: (z[0], z[1], z[2], z[4], z[3])
    og, og_m, gla_p = _gla_chunk(gla_in(main), gla_in(side), w["gla_norm"], bp, tt["gla"], N_META)
    og_s, gla_s = _gla_step(gla_in(side), w["gla_norm"], state_gla, mrows)
    x, re_p, im_p, x_m, x_s, re_s, im_s = _l0_out(main[5], og, x_prompt, side[5], og_m, og_s, x_side,
                                                  s5_re_s, s5_im_s, w, tt["l0_out"], N_META)
    x, fc0_p, x_m, x_s, fc0_s = _ffn(x, x_m, x_s, fc_s[0], w, 0, bp, tt["ffn"], N_META, False)
    x, h_p, rc_p, x_m, x_s, h_s, rc_s = _l1(x, x_m, x_s, state_rglru, rc_s, w, bp, tt["l1"], N_META)
    yp, fc1_p, _, ys, fc1_s = _ffn(x, x_m, x_s, fc_s[1], w, 1, bp, tt["ffn"], N_META, True,
                                   batch_major_out=True)

    grp = lambda z, nb: z.reshape(nb, S5_GROUPS, S5_P)
    return (yp, ys.reshape(bs, 1, D_MODEL), _gla_state_from_stacked(gla_p), gla_s,
            grp(re_p, bp), grp(re_s, bs), grp(im_p, bp), grp(im_s, bs), h_p, h_s,
            _batch_major(rc_p, bp), rc_s.reshape(bs, RNN_CONV - 1, RNN_W),
            jnp.stack([_batch_major(fc0_p, bp), _batch_major(fc1_p, bp)]),
            jnp.stack([fc0_s.reshape(bs, FFN_CONV - 1, D_FF), fc1_s.reshape(bs, FFN_CONV - 1, D_FF)]))
```

```python
import functools

import jax
import jax.numpy as jnp
from jax import lax
from jax.experimental import pallas as pl
from jax.experimental.pallas import tpu as pltpu

F32 = jnp.float32
BF16 = jnp.bfloat16

D_MODEL = 1024
N_META = 16
EPS = 1e-6
F32_TINY = 1.1754944e-38
GLA_HEADS = 4
GLA_DK = 64
GLA_DV = 128
GLA_RANK = 16
GLA_TAU = 16.0
GLA_K = GLA_HEADS * GLA_DK
GLA_V = GLA_HEADS * GLA_DV
S5_GROUPS = 32
S5_H = 16
S5_P = 64
S5_W = S5_GROUPS * S5_H
S5_N = S5_GROUPS * S5_P
RNN_W = 1536
RNN_BLOCKS = 16
RNN_BW = RNN_W // RNN_BLOCKS
RNN_C = 8.0
RNN_CONV = 4
D_FF = 2816
FFN_CONV = 3

LANES = 128
FF_CHUNK = 256
L1_SPLIT = 2
S5_PACK = 8
GATE_WIN = 768
GATE_K = 512
GATE_N = 256
VMEM_LIMIT = 56 * 1024 * 1024


def _rms(x, g):
    return x * lax.rsqrt(jnp.mean(x * x, axis=-1, keepdims=True) + EPS) * g


def _sigmoid(x):
    return 0.5 * jnp.tanh(0.5 * x) + 0.5


def _sqrt_nonneg(t):
    return t * lax.rsqrt(jnp.maximum(t, F32_TINY))


def _mm(a, w):
    return jnp.dot(a.astype(BF16), w, preferred_element_type=F32)


def _const_spec(shape, index=None):
    idx = tuple(index) if index is not None else (0,) * len(shape)
    return pl.BlockSpec(shape, lambda i: idx, pipeline_mode=pl.Buffered(1))


def _row_spec(rows, cols):
    return pl.BlockSpec((rows, cols), lambda i: (i, 0))


def _seq_spec(nb, tt, cols):
    return pl.BlockSpec((nb, tt, cols), lambda i: (0, i, 0))


def _tm_scratch(rows, cols):
    return pltpu.VMEM((cols // LANES, rows, LANES), F32)


def _load_time_major(x_ref, tm_sc):
    if tm_sc is None:
        return x_ref[...]
    nb, tt, cols = x_ref.shape
    for b in range(nb):
        for j in range(cols // LANES):
            tm_sc[j, pl.ds(b, tt, stride=nb), :] = x_ref[b, :, j * LANES:(j + 1) * LANES]
    return jnp.concatenate([tm_sc[j] for j in range(cols // LANES)], axis=-1)


def _store_time_major(o_ref, val, tm_sc):
    if tm_sc is None:
        o_ref[...] = val
        return
    nb, tt, cols = o_ref.shape
    for j in range(cols // LANES):
        tm_sc[j] = val[:, j * LANES:(j + 1) * LANES]
    for b in range(nb):
        for j in range(cols // LANES):
            o_ref[b, :, j * LANES:(j + 1) * LANES] = tm_sc[j, pl.ds(b, tt, stride=nb), :]


def _cache_shape(nb, tt, taps, width):
    return (nb, taps * width) if tt == 1 else (taps * nb, width)


def _params(sem="arbitrary"):
    return pltpu.CompilerParams(dimension_semantics=(sem,), vmem_limit_bytes=VMEM_LIMIT)


def _s5_prep_kernel(lr_ref, li_ref, ldt_ref, brt_ref, bit_ref, are_ref, aim_ref, bbre_ref, bbim_ref):
    lr = lr_ref[...]
    li = li_ref[...]
    dt = jnp.exp(ldt_ref[...])
    mag = jnp.exp(lr * dt)
    ab_re = mag * jnp.cos(li * dt)
    ab_im = mag * jnp.sin(li * dt)
    den = lr * lr + li * li
    nr = ab_re - 1.0
    ni = ab_im
    f_re = (nr * lr + ni * li) / den
    f_im = (ni * lr - nr * li) / den
    are_ref[...] = ab_re
    aim_ref[...] = ab_im
    brt = brt_ref[...]
    bit = bit_ref[...]
    bbre_ref[...] = f_re[:, None, :] * brt - f_im[:, None, :] * bit
    bbim_ref[...] = f_re[:, None, :] * bit + f_im[:, None, :] * brt


def _s5_prep(lam_re, lam_im, log_dt, b_re, b_im):
    g, p, h = b_re.shape
    brt = jnp.transpose(b_re, (0, 2, 1))
    bit = jnp.transpose(b_im, (0, 2, 1))
    return pl.pallas_call(
        _s5_prep_kernel,
        out_shape=(jax.ShapeDtypeStruct((g, p), F32), jax.ShapeDtypeStruct((g, p), F32),
                   jax.ShapeDtypeStruct((g, h, p), F32), jax.ShapeDtypeStruct((g, h, p), F32)),
        name="s5_prep",
    )(lam_re, lam_im, log_dt.reshape(g, 1), brt, bit)


def _is_first():
    return pl.program_id(0) == 0


def _side_spec(rows, cols, block=0):
    return pl.BlockSpec((rows, cols), lambda i: (block, 0))


def _l0_in_body(x, wts, outs):
    g_ref, wq_ref, wk_ref, wv_ref, wg_ref, wu_ref, wlr_ref, wal_ref, bal_ref = wts
    q_ref, k_ref, v_ref, gs_ref, la_ref, u_ref = outs
    xn = _rms(x, g_ref[...]).astype(BF16)
    q_ref[...] = _mm(xn, wq_ref[...]) * (GLA_DK ** -0.5)
    k_ref[...] = _mm(xn, wk_ref[...])
    v_ref[...] = _mm(xn, wv_ref[...])
    g = _mm(xn, wg_ref[...])
    gs_ref[...] = g * _sigmoid(g)
    u_ref[...] = _mm(xn, wu_ref[...])
    lr = _mm(xn, wlr_ref[...])
    pre = _mm(lr, wal_ref[...]) + bal_ref[...]
    la_ref[...] = jax.nn.log_sigmoid(pre) * (1.0 / GLA_TAU)


def _l0_in_kernel(*refs):
    x_ref, wts, outs, tm_sc = refs[0], refs[1:10], refs[10:16], refs[16:]
    _l0_in_body(_load_time_major(x_ref, tm_sc[0] if tm_sc else None), wts, outs)


L0_IN_COLS = (GLA_K, GLA_K, GLA_V, GLA_V, GLA_K, S5_W)


def _l0_in(x, w, nb, tt):
    rows = nb * tt
    batch_major = x.ndim == 3
    n = x.shape[0] * x.shape[1] if batch_major else x.shape[0]
    return pl.pallas_call(
        _l0_in_kernel,
        grid=(n // rows,),
        in_specs=[_seq_spec(nb, tt, D_MODEL) if batch_major else _row_spec(rows, D_MODEL),
                  _const_spec((1, D_MODEL)),
                  _const_spec((D_MODEL, GLA_K), (0, 0)), _const_spec((D_MODEL, GLA_K), (0, 1)),
                  _const_spec((D_MODEL, GLA_V), (0, 1)), _const_spec((D_MODEL, GLA_V), (0, 2)),
                  _const_spec((D_MODEL, S5_W)), _const_spec((D_MODEL, LANES)),
                  _const_spec((LANES, GLA_K)), _const_spec((1, GLA_K))],
        out_specs=[_row_spec(rows, c) for c in L0_IN_COLS],
        out_shape=[jax.ShapeDtypeStruct((n, c), F32) for c in L0_IN_COLS],
        scratch_shapes=[_tm_scratch(rows, D_MODEL)] if batch_major else [],
        compiler_params=_params("parallel"),
        name="l0_in",
    )(x, w["norm_mix_0"], w["w_in_0"], w["w_in_0"], w["w_in_0"], w["w_in_0"], w["w_u"], w["w_lr"],
      w["w_alpha"], w["b_alpha"])


def _gla_chunk_body(ins, gn_ref, o_ref, st_ref, b_sc, nb, c):
    q_ref, k_ref, v_ref, la_ref, gs_ref = ins
    rows = nb * c
    seq_mask = nb - 1

    def cum_body(t, run):
        rws = pl.ds(pl.multiple_of(t * nb, nb), nb)
        run = run + la_ref[rws, :]
        b_sc[rws, :] = run
        return run

    bl = lax.fori_loop(0, c, cum_body, jnp.zeros((nb, GLA_K), F32))
    b = b_sc[...]
    k = k_ref[...]
    qt = q_ref[...] * jnp.exp(b)
    kt = k * jnp.exp(-b)
    kh = k * jnp.exp(jnp.concatenate([bl] * c, axis=0) - b)
    gam = jnp.exp(bl)

    ri = lax.broadcasted_iota(jnp.int32, (rows, rows), 0)
    ci = lax.broadcasted_iota(jnp.int32, (rows, rows), 1)
    pair_ok = (ri >= ci) & (((ri - ci) & seq_mask) == 0)
    xw = nb * GLA_DK
    own_blk = (lax.broadcasted_iota(jnp.int32, (rows, xw), 1) // GLA_DK
               == (lax.broadcasted_iota(jnp.int32, (rows, xw), 0) & seq_mask))
    own_blk_seq = (lax.broadcasted_iota(jnp.int32, (nb, xw), 1) // GLA_DK
                   == lax.broadcasted_iota(jnp.int32, (nb, xw), 0))
    reps = xw // LANES

    def head_dup(z, h):
        blk = z[:, (h // 2) * LANES:(h // 2 + 1) * LANES]
        rolled = pltpu.roll(blk, GLA_DK, axis=1)
        low = lax.broadcasted_iota(jnp.int32, blk.shape, 1) < GLA_DK
        return jnp.where(low, blk, rolled) if h % 2 == 0 else jnp.where(low, rolled, blk)

    def expand(zd, own):
        return jnp.where(own, jnp.concatenate([zd] * reps, axis=1), 0.0)

    for h in range(GLA_HEADS):
        vs = slice(h * GLA_DV, (h + 1) * GLA_DV)
        qd, ktd, khd = head_dup(qt, h), head_dup(kt, h), head_dup(kh, h)
        vb = v_ref[:, vs].astype(BF16)
        st = st_ref[h]
        att = lax.dot_general(qd[:, :GLA_DK].astype(BF16), ktd[:, :GLA_DK].astype(BF16),
                              (((1,), (1,)), ((), ())), preferred_element_type=F32)
        att = jnp.where(pair_ok, att, 0.0).astype(BF16)
        o = jnp.dot(att, vb, preferred_element_type=F32) + lax.dot_general(
            expand(qd, own_blk).astype(BF16), st.astype(BF16), (((1,), (1,)), ((), ())),
            preferred_element_type=F32)
        upd = lax.dot_general(vb, expand(khd, own_blk).astype(BF16), (((0,), (0,)), ((), ())),
                              preferred_element_type=F32)
        gam_row = jnp.sum(expand(head_dup(gam, h), own_blk_seq), axis=0, keepdims=True)
        st_ref[h] = st * gam_row + upd
        o_ref[:, vs] = _rms(o, gn_ref[:, vs]) * gs_ref[:, vs]


def _gla_chunk_kernel(*refs, nb, c):
    ins, gn_ref, st0_ref, o_ref, st_ref, b_sc = refs[0:5], refs[5], refs[6], refs[7], refs[8], refs[9]

    @pl.when(_is_first())
    def _():
        st_ref[...] = st0_ref[...]

    _gla_chunk_body(ins, gn_ref, o_ref, st_ref, b_sc, nb, c)


def _gla_chunk(ins, gn, st0, nb, c, row_blocks=None):
    assert nb & (nb - 1) == 0 and (nb * GLA_DK) % LANES == 0
    rows = nb * c
    n = ins[0].shape[0] // row_blocks[0] if row_blocks else ins[0].shape[0]
    first = (row_blocks[1] * n) // rows if row_blocks else 0
    sshape = (GLA_HEADS, GLA_DV, nb * GLA_DK)
    cols = (GLA_K, GLA_K, GLA_V, GLA_K, GLA_V)
    kern = functools.partial(_gla_chunk_kernel, nb=nb, c=c)
    return pl.pallas_call(
        kern,
        grid=(n // rows,),
        in_specs=[pl.BlockSpec((rows, w_), lambda i: (first + i, 0)) for w_ in cols]
                 + [_const_spec((1, GLA_V)), _const_spec(sshape)],
        out_specs=[_row_spec(rows, GLA_V), pl.BlockSpec(sshape, lambda i: (0, 0, 0))],
        out_shape=[jax.ShapeDtypeStruct((n, GLA_V), F32), jax.ShapeDtypeStruct(sshape, F32)],
        scratch_shapes=[pltpu.VMEM((rows, GLA_K), F32)],
        compiler_params=_params(),
        name="gla_chunk",
    )(*ins, gn, st0)


def _gla_state_from_stacked(st):
    nb = st.shape[2] // GLA_DK
    return jnp.transpose(st.reshape(GLA_HEADS, GLA_DV, nb, GLA_DK), (2, 0, 3, 1))


def _gla_step_kernel(q_ref, k_ref, la_ref, v_ref, gs_ref, gn_ref, s0_ref, o_ref, s_ref):
    qT = q_ref[...].T
    kT = k_ref[...].T
    aT = jnp.exp(la_ref[...]).T
    o_rows = []
    for b in range(q_ref.shape[0]):
        s_new = aT[:, b:b + 1] * s0_ref[b] + kT[:, b:b + 1] * v_ref[b:b + 1, :]
        s_ref[b] = s_new
        o_rows.append(jnp.sum(qT[:, b:b + 1] * s_new, axis=0, keepdims=True))
    o = jnp.concatenate(o_rows, axis=0)
    o_ref[...] = _rms(o, gn_ref[...]) * gs_ref[...]


def _gla_step(side, gn, s0, row0):
    nb = s0.shape[0]
    blk = row0 // nb
    q, k, v, la, gs = side
    heads = lambda z: jnp.transpose(z[row0:row0 + nb].reshape(nb, GLA_HEADS, GLA_DK), (1, 0, 2))
    hspec = pl.BlockSpec((None, nb, GLA_DK), lambda h: (h, 0, 0))
    vspec = pl.BlockSpec((nb, GLA_DV), lambda h: (blk, h))
    ospec = pl.BlockSpec((nb, GLA_DV), lambda h: (0, h))
    sspec = pl.BlockSpec((nb, None, GLA_DK, GLA_DV), lambda h: (0, h, 0, 0))
    return pl.pallas_call(
        _gla_step_kernel,
        grid=(GLA_HEADS,),
        in_specs=[hspec, hspec, hspec, vspec, vspec, pl.BlockSpec((1, GLA_DV), lambda h: (0, h)), sspec],
        out_specs=[ospec, sspec],
        out_shape=[jax.ShapeDtypeStruct((nb, GLA_V), F32),
                   jax.ShapeDtypeStruct((nb, GLA_HEADS, GLA_DK, GLA_DV), F32)],
        compiler_params=_params("parallel"),
        name="gla_step",
    )(heads(q), heads(k), heads(la), v, gs, gn, s0)


def _l0_out_body(u_ref, og_ref, x, wts, xo_ref, xre_ref, xim_ref, sr_sc, si_sc, nb, tt):
    (bbre_ref, bbim_ref, are_ref, aim_ref, cre_ref, cim_ref, d_ref, wglu_ref, bglu_ref, woa_ref,
     wob_ref) = wts
    u = u_ref[...]
    ub = u.astype(BF16)
    kin = S5_PACK * S5_H
    kst = S5_PACK * S5_P
    npack = S5_GROUPS // S5_PACK
    ys = []
    for j in range(npack):
        cs = slice(j * kst, (j + 1) * kst)
        uj = ub[:, j * kin:(j + 1) * kin]
        bur = jnp.dot(uj, bbre_ref[j], preferred_element_type=F32)
        bui = jnp.dot(uj, bbim_ref[j], preferred_element_type=F32)
        ar = jnp.broadcast_to(are_ref[:, cs], (nb, kst))
        ai = jnp.broadcast_to(aim_ref[:, cs], (nb, kst))
        xr = xre_ref[:, cs]
        xi = xim_ref[:, cs]
        for t in range(tt):
            rows = slice(t * nb, (t + 1) * nb)
            xr, xi = ar * xr - ai * xi + bur[rows], ar * xi + ai * xr + bui[rows]
            sr_sc[rows, cs] = xr
            si_sc[rows, cs] = xi
        xre_ref[:, cs] = xr
        xim_ref[:, cs] = xi
        ys.append(jnp.dot(sr_sc[:, cs].astype(BF16), cre_ref[j], preferred_element_type=F32)
                  - jnp.dot(si_sc[:, cs].astype(BF16), cim_ref[j], preferred_element_type=F32))
    y = jnp.concatenate(ys, axis=-1) + d_ref[...] * u
    y = jax.nn.gelu(y)
    y = y * _sigmoid(_mm(y, wglu_ref[...]) + bglu_ref[...])
    xo_ref[...] = x + _mm(og_ref[...], woa_ref[...]) + _mm(y, wob_ref[...])


L0_OUT_WEIGHTS = 11


def _l0_out_weight_specs():
    npack = S5_GROUPS // S5_PACK
    kin, kst = S5_PACK * S5_H, S5_PACK * S5_P
    return [_const_spec((npack, kin, kst)), _const_spec((npack, kin, kst)),
            _const_spec((1, S5_N)), _const_spec((1, S5_N)),
            _const_spec((npack, kst, kin)), _const_spec((npack, kst, kin)),
            _const_spec((1, S5_W)), _const_spec((S5_W, S5_W)), _const_spec((1, S5_W)),
            _const_spec((GLA_V, D_MODEL), (0, 0)), _const_spec((S5_W, D_MODEL), (1, 0))]


def _l0_out_weights(w):
    return (w["s5_bbre"], w["s5_bbim"], w["s5_are"], w["s5_aim"], w["s5_cre"], w["s5_cim"],
            w["s5_d"], w["s5_w_glu"], w["s5_b_glu"], w["w_out_0"], w["w_out_0"])


def _l0_out_kernel(*refs, nb, tt):
    u_ref, og_ref, x_ref, xr0_ref, xi0_ref = refs[0:5]
    wts = refs[5:5 + L0_OUT_WEIGHTS]
    xo_ref, xre_ref, xim_ref, sr_sc, si_sc, tm_sc = refs[5 + L0_OUT_WEIGHTS:]

    @pl.when(_is_first())
    def _():
        xre_ref[...] = xr0_ref[...]
        xim_ref[...] = xi0_ref[...]

    _l0_out_body(u_ref, og_ref, _load_time_major(x_ref, tm_sc), wts, xo_ref, xre_ref, xim_ref,
                 sr_sc, si_sc, nb, tt)


def _l0_out(u, og, x, xr0, xi0, w, tt):
    nb = x.shape[0]
    n = u.shape[0]
    rows = nb * tt
    kern = functools.partial(_l0_out_kernel, nb=nb, tt=tt)
    st_spec = pl.BlockSpec((nb, S5_N), lambda i: (0, 0))
    f32 = lambda *shape: jax.ShapeDtypeStruct(shape, F32)
    return pl.pallas_call(
        kern,
        grid=(n // rows,),
        in_specs=[_row_spec(rows, S5_W), _row_spec(rows, GLA_V), _seq_spec(nb, tt, D_MODEL),
                  _const_spec((nb, S5_N)), _const_spec((nb, S5_N))] + _l0_out_weight_specs(),
        out_specs=[_row_spec(rows, D_MODEL), st_spec, st_spec],
        out_shape=[f32(n, D_MODEL), f32(nb, S5_N), f32(nb, S5_N)],
        scratch_shapes=[pltpu.VMEM((rows, S5_N), F32)] * 2 + [_tm_scratch(rows, D_MODEL)],
        compiler_params=_params(),
        name="l0_out",
    )(u, og, x, xr0, xi0, *_l0_out_weights(w))


def _l0_out_side_kernel(*refs, nb, tt_meta, nb_s):
    um_ref, ogm_ref, xm_ref, us_ref, ogs_ref, xs_ref, xr0s_ref, xi0s_ref = refs[0:8]
    wts = refs[8:8 + L0_OUT_WEIGHTS]
    xom_ref, xre_ref, xim_ref, xos_ref, xres_ref, xims_ref, sr_sc, si_sc = refs[8 + L0_OUT_WEIGHTS:]
    xre_ref[...] = jnp.zeros(xre_ref.shape, F32)
    xim_ref[...] = jnp.zeros(xim_ref.shape, F32)
    _l0_out_body(um_ref, ogm_ref, xm_ref[...], wts, xom_ref, xre_ref, xim_ref, sr_sc, si_sc, nb, tt_meta)
    xres_ref[...] = xr0s_ref[...]
    xims_ref[...] = xi0s_ref[...]
    _l0_out_body(us_ref, ogs_ref, xs_ref[...], wts, xos_ref, xres_ref, xims_ref, sr_sc, si_sc, nb_s, 1)


def _l0_out_side(u_side, og_meta, og_samp, x_side, xr0_s, xi0_s, w, nb, tt_meta):
    mrows, nb_s = nb * tt_meta, xr0_s.shape[0]
    assert mrows == nb_s
    kern = functools.partial(_l0_out_side_kernel, nb=nb, tt_meta=tt_meta, nb_s=nb_s)
    f32 = lambda *shape: jax.ShapeDtypeStruct(shape, F32)
    full = lambda r, c: pl.BlockSpec((r, c), lambda i: (0, 0))
    return pl.pallas_call(
        kern,
        grid=(1,),
        in_specs=[_side_spec(mrows, S5_W, 0), full(mrows, GLA_V), _side_spec(mrows, D_MODEL, 0),
                  _side_spec(nb_s, S5_W, 1), full(nb_s, GLA_V), _side_spec(nb_s, D_MODEL, 1),
                  full(nb_s, S5_N), full(nb_s, S5_N)] + _l0_out_weight_specs(),
        out_specs=[full(mrows, D_MODEL), full(nb, S5_N), full(nb, S5_N), full(nb_s, D_MODEL),
                   full(nb_s, S5_N), full(nb_s, S5_N)],
        out_shape=[f32(mrows, D_MODEL), f32(nb, S5_N), f32(nb, S5_N), f32(nb_s, D_MODEL),
                   f32(nb_s, S5_N), f32(nb_s, S5_N)],
        scratch_shapes=[pltpu.VMEM((mrows, S5_N), F32)] * 2,
        compiler_params=_params(),
        name="l0_out_side",
    )(u_side, og_meta, x_side, u_side, og_samp, x_side, xr0_s, xi0_s, *_l0_out_weights(w))


def _ffn_body(x, wts, c0_ref, c_ref, hm_sc, nb, tt, final):
    g_ref, wg_ref, wv_ref, cw_ref, cb_ref, wd_ref, gf_ref = wts
    rows = nb * tt
    xn = _rms(x, g_ref[...]).astype(BF16)
    for ci in range(D_FF // FF_CHUNK):
        cs = slice(ci * FF_CHUNK, (ci + 1) * FF_CHUNK)
        gate = jnp.dot(xn, wg_ref[:, cs], preferred_element_type=F32)
        val = jnp.dot(xn, wv_ref[:, cs], preferred_element_type=F32)
        if tt == 1:
            taps = [c0_ref[:, j * D_FF + ci * FF_CHUNK:j * D_FF + (ci + 1) * FF_CHUNK]
                    for j in range(FFN_CONV - 1)] + [gate]
            for j in range(FFN_CONV - 1):
                c_ref[:, j * D_FF + ci * FF_CHUNK:j * D_FF + (ci + 1) * FF_CHUNK] = taps[j + 1]
        else:
            ext = jnp.concatenate([c_ref[:, cs], gate], axis=0)
            taps = [ext[j * nb:j * nb + rows] for j in range(FFN_CONV)]
            c_ref[:, cs] = ext[tt * nb:(tt + FFN_CONV - 1) * nb]
        y = cb_ref[:, cs] + taps[0] * cw_ref[0:1, cs]
        for j in range(1, FFN_CONV):
            y = y + taps[j] * cw_ref[j:j + 1, cs]
        hm_sc[:, cs] = (jax.nn.gelu(y) * val).astype(BF16)
    out = x + jnp.dot(hm_sc[...], wd_ref[...], preferred_element_type=F32)
    return _rms(out, gf_ref[...]) if final else out


FFN_WEIGHTS = 7


def _ffn_weight_specs(layer):
    return [_const_spec((None, 1, D_MODEL), (layer, 0, 0)),
            _const_spec((None, D_MODEL, D_FF), (layer, 0, 0)),
            _const_spec((None, D_MODEL, D_FF), (layer, 0, 1)),
            _const_spec((None, FFN_CONV, D_FF), (layer, 0, 0)),
            _const_spec((None, 1, D_FF), (layer, 0, 0)),
            _const_spec((None, D_FF, D_MODEL), (layer, 0, 0)), _const_spec((1, D_MODEL))]


def _ffn_weights(w):
    return (w["norm_ffn"], w["ffn_w_up"], w["ffn_w_up"], w["ffn_conv_w"], w["ffn_conv_b"], w["ffn_w_down"],
            w["norm_final"])


def _ffn_kernel(*refs, nb, tt, final, batch_major_out):
    x_ref, c0_ref = refs[0:2]
    wts = refs[2:2 + FFN_WEIGHTS]
    xo_ref, c_ref, hm_sc = refs[2 + FFN_WEIGHTS:5 + FFN_WEIGHTS]
    tm_sc = refs[5 + FFN_WEIGHTS] if batch_major_out else None

    @pl.when(_is_first())
    def _():
        c_ref[...] = c0_ref[...]

    _store_time_major(xo_ref, _ffn_body(x_ref[...], wts, None, c_ref, hm_sc, nb, tt, final), tm_sc)


def _ffn(x, c0, w, layer, nb, tt, final, batch_major_out=False):
    n = x.shape[0]
    rows = nb * tt
    kern = functools.partial(_ffn_kernel, nb=nb, tt=tt, final=final, batch_major_out=batch_major_out)
    if batch_major_out:
        o_spec, o_shape = _seq_spec(nb, tt, D_MODEL), (nb, n // nb, D_MODEL)
    else:
        o_spec, o_shape = _row_spec(rows, D_MODEL), (n, D_MODEL)
    cshape = _cache_shape(nb, tt, FFN_CONV - 1, D_FF)
    f32 = lambda *shape: jax.ShapeDtypeStruct(shape, F32)
    return pl.pallas_call(
        kern,
        grid=(n // rows,),
        in_specs=[_row_spec(rows, D_MODEL), _const_spec(cshape)] + _ffn_weight_specs(layer),
        out_specs=[o_spec, pl.BlockSpec(cshape, lambda i: (0, 0))],
        out_shape=[f32(*o_shape), f32(*cshape)],
        scratch_shapes=[pltpu.VMEM((rows, D_FF), BF16)]
                       + ([_tm_scratch(rows, D_MODEL)] if batch_major_out else []),
        compiler_params=_params(),
        name="ffn%d" % layer,
    )(x, c0, *_ffn_weights(w))


def _ffn_side_kernel(*refs, nb, tt_meta, nb_s, final):
    xm_ref, xs_ref, c0s_ref = refs[0:3]
    wts = refs[3:3 + FFN_WEIGHTS]
    xom_ref, c_ref, xos_ref, cs_ref, hm_sc = refs[3 + FFN_WEIGHTS:]
    c_ref[...] = jnp.zeros(c_ref.shape, F32)
    xom_ref[...] = _ffn_body(xm_ref[...], wts, None, c_ref, hm_sc, nb, tt_meta, final)
    xos_ref[...] = _ffn_body(xs_ref[...], wts, c0s_ref, cs_ref, hm_sc, nb_s, 1, final)


def _ffn_side(x_meta, x_samp, c0_s, w, layer, nb, tt_meta, final):
    mrows, nb_s = nb * tt_meta, x_samp.shape[0]
    assert mrows == nb_s
    kern = functools.partial(_ffn_side_kernel, nb=nb, tt_meta=tt_meta, nb_s=nb_s, final=final)
    cshape = _cache_shape(nb, tt_meta, FFN_CONV - 1, D_FF)
    cs_shape = _cache_shape(nb_s, 1, FFN_CONV - 1, D_FF)
    f32 = lambda *shape: jax.ShapeDtypeStruct(shape, F32)
    full = lambda shape: pl.BlockSpec(shape, lambda i: (0,) * len(shape))
    return pl.pallas_call(
        kern,
        grid=(1,),
        in_specs=[full((mrows, D_MODEL)), full((nb_s, D_MODEL)), full(cs_shape)] + _ffn_weight_specs(layer),
        out_specs=[full((mrows, D_MODEL)), full(cshape), full((nb_s, D_MODEL)), full(cs_shape)],
        out_shape=[f32(mrows, D_MODEL), f32(*cshape), f32(nb_s, D_MODEL), f32(*cs_shape)],
        scratch_shapes=[pltpu.VMEM((mrows, D_FF), BF16)],
        compiler_params=_params(),
        name="ffn%d_side" % layer,
    )(x_meta, x_samp, c0_s, *_ffn_weights(w))


def _l1_body(x_ref, wts, xo_ref, h_ref, c0_ref, c_ref, a_sc, b_sc, nb, tt):
    g_ref, wgt_ref, wxr_ref, cw_ref, cb_ref, wa_ref, ba_ref, wx_ref, bx_ref, lam_ref, wo_ref = wts
    sp = jax.nn.softplus(-lam_ref[...])
    nsplit = L1_SPLIT if tt % L1_SPLIT == 0 else 1
    th = tt // nsplit
    rows = nb * th
    carry = c_ref[...] if tt > 1 else None
    h = h_ref[...]
    for part in range(nsplit):
        prow = slice(part * rows, (part + 1) * rows)
        x = x_ref[prow, :]
        xn = _rms(x, g_ref[...]).astype(BF16)
        xr = jnp.dot(xn, wxr_ref[...], preferred_element_type=F32)
        if tt == 1:
            taps = [c0_ref[:, j * RNN_W:(j + 1) * RNN_W] for j in range(RNN_CONV - 1)] + [xr]
            for j in range(RNN_CONV - 1):
                c_ref[:, j * RNN_W:(j + 1) * RNN_W] = taps[j + 1]
        else:
            ext = jnp.concatenate([carry, xr], axis=0)
            taps = [ext[j * nb:j * nb + rows] for j in range(RNN_CONV)]
            carry = ext[th * nb:(th + RNN_CONV - 1) * nb]
        xc = cb_ref[...] + taps[0] * cw_ref[0:1, :]
        for j in range(1, RNN_CONV):
            xc = xc + taps[j] * cw_ref[j:j + 1, :]

        xcb = xc.astype(BF16)
        rs, gs = [], []
        for wi in range(RNN_W // GATE_WIN):
            for ni in range(GATE_WIN // GATE_N):
                k0 = wi * GATE_WIN + ni * LANES
                lhs = xcb[:, k0:k0 + GATE_K]
                rs.append(jnp.dot(lhs, wa_ref[wi, ni], preferred_element_type=F32))
                gs.append(jnp.dot(lhs, wx_ref[wi, ni], preferred_element_type=F32))
        r = _sigmoid(jnp.concatenate(rs, axis=-1) + ba_ref[...])
        ig = _sigmoid(jnp.concatenate(gs, axis=-1) + bx_ref[...])
        log_a = (-RNN_C) * r * sp
        a = jnp.exp(log_a)
        a_sc[prow, :] = a
        b_sc[prow, :] = _sqrt_nonneg(jnp.tanh(-log_a) * (a * a + 1.0)) * (ig * xc)
        gg = jax.nn.gelu(jnp.dot(xn, wgt_ref[...], preferred_element_type=F32))

        for t in range(part * th, (part + 1) * th):
            rws = slice(t * nb, (t + 1) * nb)
            h = a_sc[rws, :] * h + b_sc[rws, :]
            b_sc[rws, :] = h
        xo_ref[prow, :] = x + _mm(b_sc[prow, :] * gg, wo_ref[...])
    if tt > 1:
        c_ref[...] = carry
    h_ref[...] = h


L1_WEIGHTS = 11


def _l1_weight_specs():
    gshape = (RNN_W // GATE_WIN, GATE_WIN // GATE_N, GATE_K, GATE_N)
    return [_const_spec((1, D_MODEL)),
            _const_spec((D_MODEL, RNN_W), (0, 0)), _const_spec((D_MODEL, RNN_W), (0, 1)),
            _const_spec((RNN_CONV, RNN_W)), _const_spec((1, RNN_W)),
            _const_spec(gshape), _const_spec((1, RNN_W)),
            _const_spec(gshape), _const_spec((1, RNN_W)),
            _const_spec((1, RNN_W)), _const_spec((RNN_W, D_MODEL))]


def _l1_weights(w):
    return (w["norm_mix_1"], w["w_in_1"], w["w_in_1"], w["rnn_conv_w"], w["rnn_conv_b"], w["rnn_wa"],
            w["rnn_b_a"], w["rnn_wx"], w["rnn_b_x"], w["rnn_lam"], w["w_out_1"])


def _l1_kernel(*refs, nb, tt):
    x_ref, h0_ref, c0_ref = refs[0:3]
    wts = refs[3:3 + L1_WEIGHTS]
    xo_ref, h_ref, c_ref, a_sc, b_sc = refs[3 + L1_WEIGHTS:]

    @pl.when(_is_first())
    def _():
        h_ref[...] = h0_ref[...]
        c_ref[...] = c0_ref[...]

    _l1_body(x_ref, wts, xo_ref, h_ref, None, c_ref, a_sc, b_sc, nb, tt)


def _l1(x, h0, c0, w, nb, tt):
    n = x.shape[0]
    rows = nb * tt
    kern = functools.partial(_l1_kernel, nb=nb, tt=tt)
    cshape = _cache_shape(nb, tt, RNN_CONV - 1, RNN_W)
    f32 = lambda *shape: jax.ShapeDtypeStruct(shape, F32)
    return pl.pallas_call(
        kern,
        grid=(n // rows,),
        in_specs=[_row_spec(rows, D_MODEL), _const_spec((nb, RNN_W)), _const_spec(cshape)] + _l1_weight_specs(),
        out_specs=[_row_spec(rows, D_MODEL), pl.BlockSpec((nb, RNN_W), lambda i: (0, 0)),
                   pl.BlockSpec(cshape, lambda i: (0, 0))],
        out_shape=[f32(n, D_MODEL), f32(nb, RNN_W), f32(*cshape)],
        scratch_shapes=[pltpu.VMEM((rows, RNN_W), F32)] * 2,
        compiler_params=_params(),
        name="l1_mixer",
    )(x, h0, c0, *_l1_weights(w))


def _l1_side_kernel(*refs, nb, tt_meta, nb_s):
    xm_ref, xs_ref, h0s_ref, c0s_ref = refs[0:4]
    wts = refs[4:4 + L1_WEIGHTS]
    xom_ref, h_ref, c_ref, xos_ref, hs_ref, cs_ref, a_sc, b_sc = refs[4 + L1_WEIGHTS:]
    h_ref[...] = jnp.zeros(h_ref.shape, F32)
    c_ref[...] = jnp.zeros(c_ref.shape, F32)
    _l1_body(xm_ref, wts, xom_ref, h_ref, None, c_ref, a_sc, b_sc, nb, tt_meta)
    hs_ref[...] = h0s_ref[...]
    _l1_body(xs_ref, wts, xos_ref, hs_ref, c0s_ref, cs_ref, a_sc, b_sc, nb_s, 1)


def _l1_side(x_meta, x_samp, h0_s, c0_s, w, nb, tt_meta):
    mrows, nb_s = nb * tt_meta, x_samp.shape[0]
    assert mrows == nb_s
    kern = functools.partial(_l1_side_kernel, nb=nb, tt_meta=tt_meta, nb_s=nb_s)
    cshape = _cache_shape(nb, tt_meta, RNN_CONV - 1, RNN_W)
    cs_shape = _cache_shape(nb_s, 1, RNN_CONV - 1, RNN_W)
    f32 = lambda *shape: jax.ShapeDtypeStruct(shape, F32)
    full = lambda shape: pl.BlockSpec(shape, lambda i: (0,) * len(shape))
    return pl.pallas_call(
        kern,
        grid=(1,),
        in_specs=[full((mrows, D_MODEL)), full((nb_s, D_MODEL)), full((nb_s, RNN_W)), full(cs_shape)]
                 + _l1_weight_specs(),
        out_specs=[full((mrows, D_MODEL)), full((nb, RNN_W)), full(cshape), full((nb_s, D_MODEL)),
                   full((nb_s, RNN_W)), full(cs_shape)],
        out_shape=[f32(mrows, D_MODEL), f32(nb, RNN_W), f32(*cshape), f32(nb_s, D_MODEL), f32(nb_s, RNN_W),
                   f32(*cs_shape)],
        scratch_shapes=[pltpu.VMEM((mrows, RNN_W), F32)] * 2,
        compiler_params=_params(),
        name="l1_side",
    )(x_meta, x_samp, h0_s, c0_s, *_l1_weights(w))


def _pack_gate(wblk):
    eye = jnp.eye(RNN_BLOCKS, dtype=wblk.dtype)
    dense = (wblk[:, :, None, :] * eye[:, None, :, None]).reshape(RNN_W, RNN_W)
    tiles = [[dense[wi * GATE_WIN + ni * LANES:wi * GATE_WIN + ni * LANES + GATE_K,
                    wi * GATE_WIN + ni * GATE_N:wi * GATE_WIN + (ni + 1) * GATE_N]
              for ni in range(GATE_WIN // GATE_N)] for wi in range(RNN_W // GATE_WIN)]
    return jnp.stack([jnp.stack(r) for r in tiles]).astype(BF16)


def _prep_weights(p):
    w = {}
    row = lambda v: v.reshape(1, -1).astype(F32)
    w_in = p["w_in_0"]
    c = 2 * GLA_K + 2 * GLA_V
    w["w_in_0"] = w_in.astype(BF16)
    w["w_lr"] = jnp.pad(w_in[:, c:c + GLA_RANK], ((0, 0), (0, LANES - GLA_RANK))).astype(BF16)
    c += GLA_RANK
    w["w_u"] = w_in[:, c:c + S5_W].astype(BF16)
    w["w_alpha"] = jnp.pad(p["w_alpha_0"], ((0, LANES - GLA_RANK), (0, 0))).astype(BF16)
    w["b_alpha"] = row(p["b_alpha_0"])
    w["norm_mix_0"] = row(p["norm_mix_0"])
    w["gla_norm"] = row(p["gla_norm_0"])

    are, aim, bbre, bbim = _s5_prep(p["s5_lam_re"], p["s5_lam_im"], p["s5_log_dt"], p["s5_b_re"],
                                    p["s5_b_im"])
    npack = S5_GROUPS // S5_PACK
    eye = jnp.eye(S5_PACK, dtype=F32)[None, :, None, :, None]
    grouped = lambda m: m.reshape(npack, S5_PACK, S5_H, S5_P)
    pack_b = lambda m: (grouped(m)[:, :, :, None, :] * eye).reshape(
        npack, S5_PACK * S5_H, S5_PACK * S5_P).astype(BF16)
    pack_c = lambda m: (jnp.swapaxes(grouped(m), 2, 3)[:, :, :, None, :] * eye).reshape(
        npack, S5_PACK * S5_P, S5_PACK * S5_H).astype(BF16)
    w["s5_are"] = are.reshape(1, S5_N)
    w["s5_aim"] = aim.reshape(1, S5_N)
    w["s5_bbre"] = pack_b(bbre)
    w["s5_bbim"] = pack_b(bbim)
    w["s5_cre"] = pack_c(p["s5_c_re"])
    w["s5_cim"] = pack_c(p["s5_c_im"])
    w["s5_d"] = row(p["s5_d"])
    w["s5_w_glu"] = p["s5_w_glu"].astype(BF16)
    w["s5_b_glu"] = row(p["s5_b_glu"])
    w["w_out_0"] = p["w_out_0"].astype(BF16)

    w["norm_mix_1"] = row(p["norm_mix_1"])
    w["w_in_1"] = p["w_in_1"].astype(BF16)
    w["rnn_conv_w"] = p["rnn_conv_w"].astype(F32)
    w["rnn_conv_b"] = row(p["rnn_conv_b"])
    w["rnn_wa"] = _pack_gate(p["rnn_w_a"])
    w["rnn_wx"] = _pack_gate(p["rnn_w_x"])
    w["rnn_b_a"] = row(p["rnn_b_a"])
    w["rnn_b_x"] = row(p["rnn_b_x"])
    w["rnn_lam"] = row(p["rnn_lam"])
    w["w_out_1"] = p["w_out_1"].astype(BF16)

    depth = p["norm_ffn"].shape[0]
    w["norm_ffn"] = p["norm_ffn"].reshape(depth, 1, D_MODEL)
    w["ffn_w_up"] = p["ffn_w_up"].astype(BF16)
    w["ffn_conv_w"] = p["ffn_conv_w"]
    w["ffn_conv_b"] = p["ffn_conv_b"].reshape(depth, 1, D_FF)
    w["ffn_w_down"] = p["ffn_w_down"].astype(BF16)
    w["norm_final"] = row(p["norm_final"])
    return w


def _tile_steps():
    return dict(l0_in=64, gla=32, l0_out=64, ffn=64, l1=64)


def _batch_major(cache, nb):
    jb, c = cache.shape
    return jnp.transpose(cache.reshape(jb // nb, nb, c), (1, 0, 2))


def kernel(x_prompt, x_sample, state_gla, state_s5_re, state_s5_im, state_rglru, cache_rglru_conv,
           cache_ffn_conv, meta_tokens, norm_mix_0, w_in_0, w_alpha_0, b_alpha_0, gla_norm_0,
           s5_lam_re, s5_lam_im, s5_log_dt, s5_b_re, s5_b_im, s5_c_re, s5_c_im, s5_d, s5_w_glu,
           s5_b_glu, w_out_0, norm_mix_1, w_in_1, rnn_conv_w, rnn_conv_b, rnn_w_a, rnn_b_a, rnn_w_x,
           rnn_b_x, rnn_lam, w_out_1, norm_ffn, ffn_w_up, ffn_conv_w, ffn_conv_b, ffn_w_down, norm_final):
    w = _prep_weights(dict(
        norm_mix_0=norm_mix_0, w_in_0=w_in_0, w_alpha_0=w_alpha_0, b_alpha_0=b_alpha_0,
        gla_norm_0=gla_norm_0, s5_lam_re=s5_lam_re, s5_lam_im=s5_lam_im, s5_log_dt=s5_log_dt,
        s5_b_re=s5_b_re, s5_b_im=s5_b_im, s5_c_re=s5_c_re, s5_c_im=s5_c_im, s5_d=s5_d,
        s5_w_glu=s5_w_glu, s5_b_glu=s5_b_glu, w_out_0=w_out_0, norm_mix_1=norm_mix_1, w_in_1=w_in_1,
        rnn_conv_w=rnn_conv_w, rnn_conv_b=rnn_conv_b, rnn_w_a=rnn_w_a, rnn_b_a=rnn_b_a,
        rnn_w_x=rnn_w_x, rnn_b_x=rnn_b_x, rnn_lam=rnn_lam, w_out_1=w_out_1, norm_ffn=norm_ffn,
        ffn_w_up=ffn_w_up, ffn_conv_w=ffn_conv_w, ffn_conv_b=ffn_conv_b, ffn_w_down=ffn_w_down,
        norm_final=norm_final))

    bp = x_prompt.shape[0]
    bs = x_sample.shape[0]
    tt = _tile_steps()
    mrows = bp * N_META
    assert mrows == bs

    x_side = jnp.concatenate([jnp.repeat(meta_tokens.astype(F32), bp, axis=0),
                              x_sample.reshape(bs, D_MODEL)], axis=0)
    s5_re_s = state_s5_re.reshape(bs, S5_N)
    s5_im_s = state_s5_im.reshape(bs, S5_N)
    rc_s = cache_rglru_conv.reshape(bs, (RNN_CONV - 1) * RNN_W)
    fc_s = [cache_ffn_conv[l].reshape(bs, (FFN_CONV - 1) * D_FF) for l in range(2)]

    gla_in = lambda z: (z[0], z[1], z[2], z[4], z[3])
    gn = w["gla_norm"]

    side = _l0_in(x_side, w, mrows + bs, 1)
    og_m, gla_m = _gla_chunk(gla_in(side), gn, jnp.zeros((GLA_HEADS, GLA_DV, bp * GLA_DK), F32), bp, N_META,
                             row_blocks=(2, 0))
    og_s, gla_s = _gla_step(gla_in(side), gn, state_gla, mrows)
    x_m, re_m, im_m, x_s, re_s, im_s = _l0_out_side(side[5], og_m, og_s, x_side, s5_re_s, s5_im_s, w, bp, N_META)
    x_m, fc0_m, x_s, fc0_s = _ffn_side(x_m, x_s, fc_s[0], w, 0, bp, N_META, False)
    x_m, h_m, rc_m, x_s, h_s, rc_s = _l1_side(x_m, x_s, state_rglru, rc_s, w, bp, N_META)
    _, fc1_m, ys, fc1_s = _ffn_side(x_m, x_s, fc_s[1], w, 1, bp, N_META, True)

    main = _l0_in(x_prompt, w, bp, tt["l0_in"])
    og, gla_p = _gla_chunk(gla_in(main), gn, gla_m, bp, tt["gla"])
    x, re_p, im_p = _l0_out(main[5], og, x_prompt, re_m, im_m, w, tt["l0_out"])
    x, fc0_p = _ffn(x, fc0_m, w, 0, bp, tt["ffn"], False)
    x, h_p, rc_p = _l1(x, h_m, rc_m, w, bp, tt["l1"])
    yp, fc1_p = _ffn(x, fc1_m, w, 1, bp, tt["ffn"], True, batch_major_out=True)

    grp = lambda z, nb: z.reshape(nb, S5_GROUPS, S5_P)
    return (yp, ys.reshape(bs, 1, D_MODEL), _gla_state_from_stacked(gla_p), gla_s,
            grp(re_p, bp), grp(re_s, bs), grp(im_p, bp), grp(im_s, bs), h_p, h_s,
            _batch_major(rc_p, bp), rc_s.reshape(bs, RNN_CONV - 1, RNN_W),
            jnp.stack([_batch_major(fc0_p, bp), _batch_major(fc1_p, bp)]),
            jnp.stack([fc0_s.reshape(bs, FFN_CONV - 1, D_FF), fc1_s.reshape(bs, FFN_CONV - 1, D_FF)]))
```

```python
import functools

import jax
import jax.numpy as jnp
from jax import lax
from jax.experimental import pallas as pl
from jax.experimental.pallas import tpu as pltpu

F32 = jnp.float32
BF16 = jnp.bfloat16

D_MODEL = 1024
N_META = 16
EPS = 1e-6
F32_TINY = 1.1754944e-38
GLA_HEADS = 4
GLA_DK = 64
GLA_DV = 128
GLA_RANK = 16
GLA_TAU = 16.0
GLA_K = GLA_HEADS * GLA_DK
GLA_V = GLA_HEADS * GLA_DV
S5_GROUPS = 32
S5_H = 16
S5_P = 64
S5_W = S5_GROUPS * S5_H
S5_N = S5_GROUPS * S5_P
RNN_W = 1536
RNN_BLOCKS = 16
RNN_BW = RNN_W // RNN_BLOCKS
RNN_C = 8.0
RNN_CONV = 4
D_FF = 2816
FFN_CONV = 3

LANES = 128
FF_CHUNK = 256
L1_SPLIT = 2
GLA_CHUNKS_PER_STEP = 2
S5_PACK = 8
GATE_WIN = 768
GATE_K = 512
GATE_N = 256
VMEM_LIMIT = 56 * 1024 * 1024


def _rms(x, g):
    return x * lax.rsqrt(jnp.mean(x * x, axis=-1, keepdims=True) + EPS) * g


def _sigmoid(x):
    return 0.5 * jnp.tanh(0.5 * x) + 0.5


def _sqrt_nonneg(t):
    return t * lax.rsqrt(jnp.maximum(t, F32_TINY))


def _mm(a, w):
    return jnp.dot(a.astype(BF16), w, preferred_element_type=F32)


def _const_spec(shape, index=None):
    idx = tuple(index) if index is not None else (0,) * len(shape)
    return pl.BlockSpec(shape, lambda i: idx, pipeline_mode=pl.Buffered(1))


def _row_spec(rows, cols):
    return pl.BlockSpec((rows, cols), lambda i: (i, 0))


def _seq_spec(nb, tt, cols):
    return pl.BlockSpec((nb, tt, cols), lambda i: (0, i, 0))


def _tm_scratch(rows, cols):
    return pltpu.VMEM((cols // LANES, rows, LANES), F32)


def _load_time_major(x_ref, tm_sc):
    if tm_sc is None:
        return x_ref[...]
    nb, tt, cols = x_ref.shape
    for b in range(nb):
        for j in range(cols // LANES):
            tm_sc[j, pl.ds(b, tt, stride=nb), :] = x_ref[b, :, j * LANES:(j + 1) * LANES]
    return jnp.concatenate([tm_sc[j] for j in range(cols // LANES)], axis=-1)


def _store_time_major(o_ref, val, tm_sc):
    if tm_sc is None:
        o_ref[...] = val
        return
    nb, tt, cols = o_ref.shape
    for j in range(cols // LANES):
        tm_sc[j] = val[:, j * LANES:(j + 1) * LANES]
    for b in range(nb):
        for j in range(cols // LANES):
            o_ref[b, :, j * LANES:(j + 1) * LANES] = tm_sc[j, pl.ds(b, tt, stride=nb), :]


def _cache_shape(nb, tt, taps, width):
    return (nb, taps, width) if tt == 1 else (taps * nb, width)


def _params(sem="arbitrary"):
    return pltpu.CompilerParams(dimension_semantics=(sem,), vmem_limit_bytes=VMEM_LIMIT)


def _s5_prep_kernel(lr_ref, li_ref, ldt_ref, brt_ref, bit_ref, are_ref, aim_ref, bbre_ref, bbim_ref):
    lr = lr_ref[...]
    li = li_ref[...]
    dt = jnp.exp(ldt_ref[...])
    mag = jnp.exp(lr * dt)
    ab_re = mag * jnp.cos(li * dt)
    ab_im = mag * jnp.sin(li * dt)
    den = lr * lr + li * li
    nr = ab_re - 1.0
    ni = ab_im
    f_re = (nr * lr + ni * li) / den
    f_im = (ni * lr - nr * li) / den
    are_ref[...] = ab_re
    aim_ref[...] = ab_im
    brt = brt_ref[...]
    bit = bit_ref[...]
    bbre_ref[...] = f_re[:, None, :] * brt - f_im[:, None, :] * bit
    bbim_ref[...] = f_re[:, None, :] * bit + f_im[:, None, :] * brt


def _s5_prep(lam_re, lam_im, log_dt, b_re, b_im):
    g, p, h = b_re.shape
    brt = jnp.transpose(b_re, (0, 2, 1))
    bit = jnp.transpose(b_im, (0, 2, 1))
    return pl.pallas_call(
        _s5_prep_kernel,
        out_shape=(jax.ShapeDtypeStruct((g, p), F32), jax.ShapeDtypeStruct((g, p), F32),
                   jax.ShapeDtypeStruct((g, h, p), F32), jax.ShapeDtypeStruct((g, h, p), F32)),
        name="s5_prep",
    )(lam_re, lam_im, log_dt.reshape(g, 1), brt, bit)


def _is_first():
    return pl.program_id(0) == 0


def _side_spec(rows, cols, block=0):
    return pl.BlockSpec((rows, cols), lambda i: (block, 0))


def _l0_in_body(x, wts, outs):
    g_ref, wq_ref, wk_ref, wv_ref, wg_ref, wu_ref, wlr_ref, wal_ref, bal_ref = wts
    q_ref, k_ref, v_ref, gs_ref, la_ref, u_ref = outs
    xn = _rms(x, g_ref[...]).astype(BF16)
    q_ref[...] = _mm(xn, wq_ref[...]) * (GLA_DK ** -0.5)
    k_ref[...] = _mm(xn, wk_ref[...])
    v_ref[...] = _mm(xn, wv_ref[...])
    g = _mm(xn, wg_ref[...])
    gs_ref[...] = g * _sigmoid(g)
    u_ref[...] = _mm(xn, wu_ref[...])
    lr = _mm(xn, wlr_ref[...])
    pre = _mm(lr, wal_ref[...]) + bal_ref[...]
    la_ref[...] = jax.nn.log_sigmoid(pre) * (1.0 / GLA_TAU)


def _l0_in_kernel(*refs):
    x_ref, wts, outs, tm_sc = refs[0], refs[1:10], refs[10:16], refs[16:]
    _l0_in_body(_load_time_major(x_ref, tm_sc[0] if tm_sc else None), wts, outs)


L0_IN_COLS = (GLA_K, GLA_K, GLA_V, GLA_V, GLA_K, S5_W)


def _l0_in(x, w, nb, tt):
    rows = nb * tt
    batch_major = x.ndim == 3
    n = x.shape[0] * x.shape[1] if batch_major else x.shape[0]
    return pl.pallas_call(
        _l0_in_kernel,
        grid=(n // rows,),
        in_specs=[_seq_spec(nb, tt, D_MODEL) if batch_major else _row_spec(rows, D_MODEL),
                  _const_spec((1, D_MODEL)),
                  _const_spec((D_MODEL, GLA_K), (0, 0)), _const_spec((D_MODEL, GLA_K), (0, 1)),
                  _const_spec((D_MODEL, GLA_V), (0, 1)), _const_spec((D_MODEL, GLA_V), (0, 2)),
                  _const_spec((D_MODEL, S5_W)), _const_spec((D_MODEL, LANES)),
                  _const_spec((LANES, GLA_K)), _const_spec((1, GLA_K))],
        out_specs=[_row_spec(rows, c) for c in L0_IN_COLS],
        out_shape=[jax.ShapeDtypeStruct((n, c), F32) for c in L0_IN_COLS],
        scratch_shapes=[_tm_scratch(rows, D_MODEL)] if batch_major else [],
        compiler_params=_params("parallel"),
        name="l0_in",
    )(x, w["norm_mix_0"], w["w_in_0"], w["w_in_0"], w["w_in_0"], w["w_in_0"], w["w_u"], w["w_lr"],
      w["w_alpha"], w["b_alpha"])


def _gla_chunk_body(ins, gn_ref, o_ref, st_ref, b_sc, nb, c):
    q_ref, k_ref, v_ref, la_ref, gs_ref = ins
    rows = nb * c
    seq_mask = nb - 1

    def cum_body(t, run):
        rws = pl.ds(pl.multiple_of(t * nb, nb), nb)
        run = run + la_ref[rws, :]
        b_sc[rws, :] = run
        return run

    bl = lax.fori_loop(0, c, cum_body, jnp.zeros((nb, GLA_K), F32))
    b = b_sc[...]
    k = k_ref[...]
    qt = q_ref[...] * jnp.exp(b)
    kt = k * jnp.exp(-b)
    kh = k * jnp.exp(jnp.concatenate([bl] * c, axis=0) - b)
    gam = jnp.exp(bl)

    ri = lax.broadcasted_iota(jnp.int32, (rows, rows), 0)
    ci = lax.broadcasted_iota(jnp.int32, (rows, rows), 1)
    pair_ok = (ri >= ci) & (((ri - ci) & seq_mask) == 0)
    xw = nb * GLA_DK
    own_blk = (lax.broadcasted_iota(jnp.int32, (rows, xw), 1) // GLA_DK
               == (lax.broadcasted_iota(jnp.int32, (rows, xw), 0) & seq_mask))
    own_blk_seq = (lax.broadcasted_iota(jnp.int32, (nb, xw), 1) // GLA_DK
                   == lax.broadcasted_iota(jnp.int32, (nb, xw), 0))
    reps = xw // LANES

    def head_dup(z, h):
        blk = z[:, (h // 2) * LANES:(h // 2 + 1) * LANES]
        rolled = pltpu.roll(blk, GLA_DK, axis=1)
        low = lax.broadcasted_iota(jnp.int32, blk.shape, 1) < GLA_DK
        return jnp.where(low, blk, rolled) if h % 2 == 0 else jnp.where(low, rolled, blk)

    def expand(zd, own):
        return jnp.where(own, jnp.concatenate([zd] * reps, axis=1), 0.0)

    for h in range(GLA_HEADS):
        vs = slice(h * GLA_DV, (h + 1) * GLA_DV)
        qd, ktd, khd = head_dup(qt, h), head_dup(kt, h), head_dup(kh, h)
        vb = v_ref[:, vs].astype(BF16)
        st = st_ref[h]
        att = lax.dot_general(qd[:, :GLA_DK].astype(BF16), ktd[:, :GLA_DK].astype(BF16),
                              (((1,), (1,)), ((), ())), preferred_element_type=F32)
        att = jnp.where(pair_ok, att, 0.0).astype(BF16)
        o = jnp.dot(att, vb, preferred_element_type=F32) + lax.dot_general(
            expand(qd, own_blk).astype(BF16), st.astype(BF16), (((1,), (1,)), ((), ())),
            preferred_element_type=F32)
        upd = lax.dot_general(vb, expand(khd, own_blk).astype(BF16), (((0,), (0,)), ((), ())),
                              preferred_element_type=F32)
        gam_row = jnp.sum(expand(head_dup(gam, h), own_blk_seq), axis=0, keepdims=True)
        st_ref[h] = st * gam_row + upd
        o_ref[:, vs] = _rms(o, gn_ref[:, vs]) * gs_ref[:, vs]


def _gla_chunk_kernel(*refs, nb, c, chunks):
    ins, gn_ref, st0_ref, o_ref, st_ref, b_sc = refs[0:5], refs[5], refs[6], refs[7], refs[8], refs[9]

    @pl.when(_is_first())
    def _():
        st_ref[...] = st0_ref[...]

    for j in range(chunks):
        rws = pl.ds(j * nb * c, nb * c)
        _gla_chunk_body([r.at[rws] for r in ins], gn_ref, o_ref.at[rws], st_ref, b_sc, nb, c)


def _gla_chunk(ins, gn, st0, nb, c, chunks=1, row_blocks=None):
    assert nb & (nb - 1) == 0 and (nb * GLA_DK) % LANES == 0
    rows = nb * c * chunks
    n = ins[0].shape[0] // row_blocks[0] if row_blocks else ins[0].shape[0]
    first = (row_blocks[1] * n) // rows if row_blocks else 0
    sshape = (GLA_HEADS, GLA_DV, nb * GLA_DK)
    cols = (GLA_K, GLA_K, GLA_V, GLA_K, GLA_V)
    kern = functools.partial(_gla_chunk_kernel, nb=nb, c=c, chunks=chunks)
    return pl.pallas_call(
        kern,
        grid=(n // rows,),
        in_specs=[pl.BlockSpec((rows, w_), lambda i: (first + i, 0)) for w_ in cols]
                 + [_const_spec((1, GLA_V)), _const_spec(sshape)],
        out_specs=[_row_spec(rows, GLA_V), pl.BlockSpec(sshape, lambda i: (0, 0, 0))],
        out_shape=[jax.ShapeDtypeStruct((n, GLA_V), F32), jax.ShapeDtypeStruct(sshape, F32)],
        scratch_shapes=[pltpu.VMEM((nb * c, GLA_K), F32)],
        compiler_params=_params(),
        name="gla_chunk",
    )(*ins, gn, st0)


def _gla_state_from_stacked(st):
    nb = st.shape[2] // GLA_DK
    return jnp.transpose(st.reshape(GLA_HEADS, GLA_DV, nb, GLA_DK), (2, 0, 3, 1))


def _gla_step_kernel(q_ref, k_ref, la_ref, v_ref, gs_ref, gn_ref, s0_ref, o_ref, s_ref):
    qT = q_ref[...].T
    kT = k_ref[...].T
    aT = jnp.exp(la_ref[...]).T
    o_rows = []
    for b in range(q_ref.shape[0]):
        s_new = aT[:, b:b + 1] * s0_ref[b] + kT[:, b:b + 1] * v_ref[b:b + 1, :]
        s_ref[b] = s_new
        o_rows.append(jnp.sum(qT[:, b:b + 1] * s_new, axis=0, keepdims=True))
    o = jnp.concatenate(o_rows, axis=0)
    o_ref[...] = _rms(o, gn_ref[...]) * gs_ref[...]


def _gla_step(side, gn, s0, row0):
    nb = s0.shape[0]
    blk = row0 // nb
    q, k, v, la, gs = side
    heads = lambda z: jnp.transpose(z[row0:row0 + nb].reshape(nb, GLA_HEADS, GLA_DK), (1, 0, 2))
    hspec = pl.BlockSpec((None, nb, GLA_DK), lambda h: (h, 0, 0))
    vspec = pl.BlockSpec((nb, GLA_DV), lambda h: (blk, h))
    ospec = pl.BlockSpec((nb, GLA_DV), lambda h: (0, h))
    sspec = pl.BlockSpec((nb, None, GLA_DK, GLA_DV), lambda h: (0, h, 0, 0))
    return pl.pallas_call(
        _gla_step_kernel,
        grid=(GLA_HEADS,),
        in_specs=[hspec, hspec, hspec, vspec, vspec, pl.BlockSpec((1, GLA_DV), lambda h: (0, h)), sspec],
        out_specs=[ospec, sspec],
        out_shape=[jax.ShapeDtypeStruct((nb, GLA_V), F32),
                   jax.ShapeDtypeStruct((nb, GLA_HEADS, GLA_DK, GLA_DV), F32)],
        compiler_params=_params("parallel"),
        name="gla_step",
    )(heads(q), heads(k), heads(la), v, gs, gn, s0)


def _l0_out_body(u_ref, og_ref, x, wts, xo_ref, xre_ref, xim_ref, sr_sc, si_sc, nb, tt):
    (bbre_ref, bbim_ref, are_ref, aim_ref, cre_ref, cim_ref, d_ref, wglu_ref, bglu_ref, woa_ref,
     wob_ref) = wts
    u = u_ref[...]
    ub = u.astype(BF16)
    kin = S5_PACK * S5_H
    kst = S5_PACK * S5_P
    npack = S5_GROUPS // S5_PACK
    ys = []
    for j in range(npack):
        cs = slice(j * kst, (j + 1) * kst)
        uj = ub[:, j * kin:(j + 1) * kin]
        bur = jnp.dot(uj, bbre_ref[j], preferred_element_type=F32)
        bui = jnp.dot(uj, bbim_ref[j], preferred_element_type=F32)
        ar = jnp.broadcast_to(are_ref[:, cs], (nb, kst))
        ai = jnp.broadcast_to(aim_ref[:, cs], (nb, kst))
        xr = xre_ref[:, cs]
        xi = xim_ref[:, cs]
        for t in range(tt):
            rows = slice(t * nb, (t + 1) * nb)
            xr, xi = ar * xr - ai * xi + bur[rows], ar * xi + ai * xr + bui[rows]
            sr_sc[rows, cs] = xr
            si_sc[rows, cs] = xi
        xre_ref[:, cs] = xr
        xim_ref[:, cs] = xi
        ys.append(jnp.dot(sr_sc[:, cs].astype(BF16), cre_ref[j], preferred_element_type=F32)
                  - jnp.dot(si_sc[:, cs].astype(BF16), cim_ref[j], preferred_element_type=F32))
    y = jnp.concatenate(ys, axis=-1) + d_ref[...] * u
    y = jax.nn.gelu(y)
    y = y * _sigmoid(_mm(y, wglu_ref[...]) + bglu_ref[...])
    xo_ref[...] = x + _mm(og_ref[...], woa_ref[...]) + _mm(y, wob_ref[...])


L0_OUT_WEIGHTS = 11


def _l0_out_weight_specs():
    npack = S5_GROUPS // S5_PACK
    kin, kst = S5_PACK * S5_H, S5_PACK * S5_P
    return [_const_spec((npack, kin, kst)), _const_spec((npack, kin, kst)),
            _const_spec((1, S5_N)), _const_spec((1, S5_N)),
            _const_spec((npack, kst, kin)), _const_spec((npack, kst, kin)),
            _const_spec((1, S5_W)), _const_spec((S5_W, S5_W)), _const_spec((1, S5_W)),
            _const_spec((GLA_V, D_MODEL), (0, 0)), _const_spec((S5_W, D_MODEL), (1, 0))]


def _l0_out_weights(w):
    return (w["s5_bbre"], w["s5_bbim"], w["s5_are"], w["s5_aim"], w["s5_cre"], w["s5_cim"],
            w["s5_d"], w["s5_w_glu"], w["s5_b_glu"], w["w_out_0"], w["w_out_0"])


def _l0_out_kernel(*refs, nb, tt):
    u_ref, og_ref, x_ref, xr0_ref, xi0_ref = refs[0:5]
    wts = refs[5:5 + L0_OUT_WEIGHTS]
    xo_ref, xre_ref, xim_ref, sr_sc, si_sc, tm_sc = refs[5 + L0_OUT_WEIGHTS:]

    @pl.when(_is_first())
    def _():
        xre_ref[...] = xr0_ref[...]
        xim_ref[...] = xi0_ref[...]

    _l0_out_body(u_ref, og_ref, _load_time_major(x_ref, tm_sc), wts, xo_ref, xre_ref, xim_ref,
                 sr_sc, si_sc, nb, tt)


def _l0_out(u, og, x, xr0, xi0, w, tt):
    nb = x.shape[0]
    n = u.shape[0]
    rows = nb * tt
    kern = functools.partial(_l0_out_kernel, nb=nb, tt=tt)
    st_spec = pl.BlockSpec((nb, S5_N), lambda i: (0, 0))
    f32 = lambda *shape: jax.ShapeDtypeStruct(shape, F32)
    return pl.pallas_call(
        kern,
        grid=(n // rows,),
        in_specs=[_row_spec(rows, S5_W), _row_spec(rows, GLA_V), _seq_spec(nb, tt, D_MODEL),
                  _const_spec((nb, S5_N)), _const_spec((nb, S5_N))] + _l0_out_weight_specs(),
        out_specs=[_row_spec(rows, D_MODEL), st_spec, st_spec],
        out_shape=[f32(n, D_MODEL), f32(nb, S5_N), f32(nb, S5_N)],
        scratch_shapes=[pltpu.VMEM((rows, S5_N), F32)] * 2 + [_tm_scratch(rows, D_MODEL)],
        compiler_params=_params(),
        name="l0_out",
    )(u, og, x, xr0, xi0, *_l0_out_weights(w))


def _l0_out_side_kernel(*refs, nb, tt_meta, nb_s):
    um_ref, ogm_ref, xm_ref, us_ref, ogs_ref, xs_ref, xr0s_ref, xi0s_ref = refs[0:8]
    wts = refs[8:8 + L0_OUT_WEIGHTS]
    xom_ref, xre_ref, xim_ref, xos_ref, xres_ref, xims_ref, sr_sc, si_sc = refs[8 + L0_OUT_WEIGHTS:]
    xre_ref[...] = jnp.zeros(xre_ref.shape, F32)
    xim_ref[...] = jnp.zeros(xim_ref.shape, F32)
    _l0_out_body(um_ref, ogm_ref, xm_ref[...], wts, xom_ref, xre_ref, xim_ref, sr_sc, si_sc, nb, tt_meta)
    xres_ref[...] = xr0s_ref[...]
    xims_ref[...] = xi0s_ref[...]
    _l0_out_body(us_ref, ogs_ref, xs_ref[...], wts, xos_ref, xres_ref, xims_ref, sr_sc, si_sc, nb_s, 1)


def _l0_out_side(u_side, og_meta, og_samp, x_side, xr0_s, xi0_s, w, nb, tt_meta):
    mrows, nb_s = nb * tt_meta, xr0_s.shape[0]
    assert mrows == nb_s
    kern = functools.partial(_l0_out_side_kernel, nb=nb, tt_meta=tt_meta, nb_s=nb_s)
    f32 = lambda *shape: jax.ShapeDtypeStruct(shape, F32)
    full = lambda r, c: pl.BlockSpec((r, c), lambda i: (0, 0))
    return pl.pallas_call(
        kern,
        grid=(1,),
        in_specs=[_side_spec(mrows, S5_W, 0), full(mrows, GLA_V), _side_spec(mrows, D_MODEL, 0),
                  _side_spec(nb_s, S5_W, 1), full(nb_s, GLA_V), _side_spec(nb_s, D_MODEL, 1),
                  full(nb_s, S5_N), full(nb_s, S5_N)] + _l0_out_weight_specs(),
        out_specs=[full(mrows, D_MODEL), full(nb, S5_N), full(nb, S5_N), full(nb_s, D_MODEL),
                   full(nb_s, S5_N), full(nb_s, S5_N)],
        out_shape=[f32(mrows, D_MODEL), f32(nb, S5_N), f32(nb, S5_N), f32(nb_s, D_MODEL),
                   f32(nb_s, S5_N), f32(nb_s, S5_N)],
        scratch_shapes=[pltpu.VMEM((mrows, S5_N), F32)] * 2,
        compiler_params=_params(),
        name="l0_out_side",
    )(u_side, og_meta, x_side, u_side, og_samp, x_side, xr0_s, xi0_s, *_l0_out_weights(w))


def _ffn_body(x, wts, c0_ref, c_ref, hm_sc, nb, tt, final):
    g_ref, wg_ref, wv_ref, cw_ref, cb_ref, wd_ref, gf_ref = wts
    rows = nb * tt
    xn = _rms(x, g_ref[...]).astype(BF16)
    for ci in range(D_FF // FF_CHUNK):
        cs = slice(ci * FF_CHUNK, (ci + 1) * FF_CHUNK)
        gate = jnp.dot(xn, wg_ref[:, cs], preferred_element_type=F32)
        val = jnp.dot(xn, wv_ref[:, cs], preferred_element_type=F32)
        if tt == 1:
            taps = [c0_ref[:, j, cs] for j in range(FFN_CONV - 1)] + [gate]
            for j in range(FFN_CONV - 1):
                c_ref[:, j, cs] = taps[j + 1]
        else:
            ext = jnp.concatenate([c_ref[:, cs], gate], axis=0)
            taps = [ext[j * nb:j * nb + rows] for j in range(FFN_CONV)]
            c_ref[:, cs] = ext[tt * nb:(tt + FFN_CONV - 1) * nb]
        y = cb_ref[:, cs] + taps[0] * cw_ref[0:1, cs]
        for j in range(1, FFN_CONV):
            y = y + taps[j] * cw_ref[j:j + 1, cs]
        hm_sc[:, cs] = (jax.nn.gelu(y) * val).astype(BF16)
    out = x + jnp.dot(hm_sc[...], wd_ref[...], preferred_element_type=F32)
    return _rms(out, gf_ref[...]) if final else out


FFN_WEIGHTS = 7


def _ffn_weight_specs(layer):
    return [_const_spec((None, 1, D_MODEL), (layer, 0, 0)),
            _const_spec((None, D_MODEL, D_FF), (layer, 0, 0)),
            _const_spec((None, D_MODEL, D_FF), (layer, 0, 1)),
            _const_spec((None, FFN_CONV, D_FF), (layer, 0, 0)),
            _const_spec((None, 1, D_FF), (layer, 0, 0)),
            _const_spec((None, D_FF, D_MODEL), (layer, 0, 0)), _const_spec((1, D_MODEL))]


def _ffn_weights(w):
    return (w["norm_ffn"], w["ffn_w_up"], w["ffn_w_up"], w["ffn_conv_w"], w["ffn_conv_b"], w["ffn_w_down"],
            w["norm_final"])


def _ffn_kernel(*refs, nb, tt, final, batch_major_out):
    x_ref, c0_ref = refs[0:2]
    wts = refs[2:2 + FFN_WEIGHTS]
    xo_ref, c_ref, hm_sc = refs[2 + FFN_WEIGHTS:5 + FFN_WEIGHTS]
    tm_sc = refs[5 + FFN_WEIGHTS] if batch_major_out else None

    @pl.when(_is_first())
    def _():
        c_ref[...] = c0_ref[...]

    _store_time_major(xo_ref, _ffn_body(x_ref[...], wts, None, c_ref, hm_sc, nb, tt, final), tm_sc)


def _ffn(x, c0, w, layer, nb, tt, final, batch_major_out=False):
    n = x.shape[0]
    rows = nb * tt
    kern = functools.partial(_ffn_kernel, nb=nb, tt=tt, final=final, batch_major_out=batch_major_out)
    if batch_major_out:
        o_spec, o_shape = _seq_spec(nb, tt, D_MODEL), (nb, n // nb, D_MODEL)
    else:
        o_spec, o_shape = _row_spec(rows, D_MODEL), (n, D_MODEL)
    cshape = _cache_shape(nb, tt, FFN_CONV - 1, D_FF)
    f32 = lambda *shape: jax.ShapeDtypeStruct(shape, F32)
    return pl.pallas_call(
        kern,
        grid=(n // rows,),
        in_specs=[_row_spec(rows, D_MODEL), _const_spec(cshape)] + _ffn_weight_specs(layer),
        out_specs=[o_spec, pl.BlockSpec(cshape, lambda i: (0, 0))],
        out_shape=[f32(*o_shape), f32(*cshape)],
        scratch_shapes=[pltpu.VMEM((rows, D_FF), BF16)]
                       + ([_tm_scratch(rows, D_MODEL)] if batch_major_out else []),
        compiler_params=_params(),
        name="ffn%d" % layer,
    )(x, c0, *_ffn_weights(w))


def _ffn_side_kernel(*refs, nb, tt_meta, nb_s, final):
    xm_ref, xs_ref, c0s_ref = refs[0:3]
    wts = refs[3:3 + FFN_WEIGHTS]
    xom_ref, c_ref, xos_ref, cs_ref, hm_sc = refs[3 + FFN_WEIGHTS:]
    c_ref[...] = jnp.zeros(c_ref.shape, F32)
    xom_ref[...] = _ffn_body(xm_ref[...], wts, None, c_ref, hm_sc, nb, tt_meta, final)
    xos_ref[...] = _ffn_body(xs_ref[...], wts, c0s_ref, cs_ref, hm_sc, nb_s, 1, final)


def _ffn_side(x_meta, x_samp, c0_s, w, layer, nb, tt_meta, final):
    mrows, nb_s = nb * tt_meta, x_samp.shape[0]
    assert mrows == nb_s
    kern = functools.partial(_ffn_side_kernel, nb=nb, tt_meta=tt_meta, nb_s=nb_s, final=final)
    cshape = _cache_shape(nb, tt_meta, FFN_CONV - 1, D_FF)
    cs_shape = _cache_shape(nb_s, 1, FFN_CONV - 1, D_FF)
    f32 = lambda *shape: jax.ShapeDtypeStruct(shape, F32)
    full = lambda shape: pl.BlockSpec(shape, lambda i: (0,) * len(shape))
    return pl.pallas_call(
        kern,
        grid=(1,),
        in_specs=[full((mrows, D_MODEL)), full((nb_s, D_MODEL)),
                  pl.BlockSpec((None,) + cs_shape, lambda i: (layer, 0, 0, 0))] + _ffn_weight_specs(layer),
        out_specs=[full((mrows, D_MODEL)), full(cshape), full((nb_s, D_MODEL)), full(cs_shape)],
        out_shape=[f32(mrows, D_MODEL), f32(*cshape), f32(nb_s, D_MODEL), f32(*cs_shape)],
        scratch_shapes=[pltpu.VMEM((mrows, D_FF), BF16)],
        compiler_params=_params(),
        name="ffn%d_side" % layer,
    )(x_meta, x_samp, c0_s, *_ffn_weights(w))


def _l1_body(x_ref, wts, xo_ref, h_ref, c0_ref, c_ref, a_sc, b_sc, nb, tt):
    g_ref, wgt_ref, wxr_ref, cw_ref, cb_ref, wa_ref, ba_ref, wx_ref, bx_ref, lam_ref, wo_ref = wts
    sp = jax.nn.softplus(-lam_ref[...])
    nsplit = L1_SPLIT if tt % L1_SPLIT == 0 else 1
    th = tt // nsplit
    rows = nb * th
    carry = c_ref[...] if tt > 1 else None
    h = h_ref[...]
    for part in range(nsplit):
        prow = slice(part * rows, (part + 1) * rows)
        x = x_ref[prow, :]
        xn = _rms(x, g_ref[...]).astype(BF16)
        xr = jnp.dot(xn, wxr_ref[...], preferred_element_type=F32)
        if tt == 1:
            taps = [c0_ref[:, j, :] for j in range(RNN_CONV - 1)] + [xr]
            for j in range(RNN_CONV - 1):
                c_ref[:, j, :] = taps[j + 1]
        else:
            ext = jnp.concatenate([carry, xr], axis=0)
            taps = [ext[j * nb:j * nb + rows] for j in range(RNN_CONV)]
            carry = ext[th * nb:(th + RNN_CONV - 1) * nb]
        xc = cb_ref[...] + taps[0] * cw_ref[0:1, :]
        for j in range(1, RNN_CONV):
            xc = xc + taps[j] * cw_ref[j:j + 1, :]

        xcb = xc.astype(BF16)
        rs, gs = [], []
        for wi in range(RNN_W // GATE_WIN):
            for ni in range(GATE_WIN // GATE_N):
                k0 = wi * GATE_WIN + ni * LANES
                lhs = xcb[:, k0:k0 + GATE_K]
                rs.append(jnp.dot(lhs, wa_ref[wi, ni], preferred_element_type=F32))
                gs.append(jnp.dot(lhs, wx_ref[wi, ni], preferred_element_type=F32))
        r = _sigmoid(jnp.concatenate(rs, axis=-1) + ba_ref[...])
        ig = _sigmoid(jnp.concatenate(gs, axis=-1) + bx_ref[...])
        log_a = (-RNN_C) * r * sp
        a = jnp.exp(log_a)
        a_sc[prow, :] = a
        b_sc[prow, :] = _sqrt_nonneg(jnp.tanh(-log_a) * (a * a + 1.0)) * (ig * xc)
        gg = jax.nn.gelu(jnp.dot(xn, wgt_ref[...], preferred_element_type=F32))

        for t in range(part * th, (part + 1) * th):
            rws = slice(t * nb, (t + 1) * nb)
            h = a_sc[rws, :] * h + b_sc[rws, :]
            b_sc[rws, :] = h
        xo_ref[prow, :] = x + _mm(b_sc[prow, :] * gg, wo_ref[...])
    if tt > 1:
        c_ref[...] = carry
    h_ref[...] = h


L1_WEIGHTS = 11


def _l1_weight_specs():
    gshape = (RNN_W // GATE_WIN, GATE_WIN // GATE_N, GATE_K, GATE_N)
    return [_const_spec((1, D_MODEL)),
            _const_spec((D_MODEL, RNN_W), (0, 0)), _const_spec((D_MODEL, RNN_W), (0, 1)),
            _const_spec((RNN_CONV, RNN_W)), _const_spec((1, RNN_W)),
            _const_spec(gshape), _const_spec((1, RNN_W)),
            _const_spec(gshape), _const_spec((1, RNN_W)),
            _const_spec((1, RNN_W)), _const_spec((RNN_W, D_MODEL))]


def _l1_weights(w):
    return (w["norm_mix_1"], w["w_in_1"], w["w_in_1"], w["rnn_conv_w"], w["rnn_conv_b"], w["rnn_wa"],
            w["rnn_b_a"], w["rnn_wx"], w["rnn_b_x"], w["rnn_lam"], w["w_out_1"])


def _l1_kernel(*refs, nb, tt):
    x_ref, h0_ref, c0_ref = refs[0:3]
    wts = refs[3:3 + L1_WEIGHTS]
    xo_ref, h_ref, c_ref, a_sc, b_sc = refs[3 + L1_WEIGHTS:]

    @pl.when(_is_first())
    def _():
        h_ref[...] = h0_ref[...]
        c_ref[...] = c0_ref[...]

    _l1_body(x_ref, wts, xo_ref, h_ref, None, c_ref, a_sc, b_sc, nb, tt)


def _l1(x, h0, c0, w, nb, tt):
    n = x.shape[0]
    rows = nb * tt
    kern = functools.partial(_l1_kernel, nb=nb, tt=tt)
    cshape = _cache_shape(nb, tt, RNN_CONV - 1, RNN_W)
    f32 = lambda *shape: jax.ShapeDtypeStruct(shape, F32)
    return pl.pallas_call(
        kern,
        grid=(n // rows,),
        in_specs=[_row_spec(rows, D_MODEL), _const_spec((nb, RNN_W)), _const_spec(cshape)] + _l1_weight_specs(),
        out_specs=[_row_spec(rows, D_MODEL), pl.BlockSpec((nb, RNN_W), lambda i: (0, 0)),
                   pl.BlockSpec(cshape, lambda i: (0, 0))],
        out_shape=[f32(n, D_MODEL), f32(nb, RNN_W), f32(*cshape)],
        scratch_shapes=[pltpu.VMEM((rows, RNN_W), F32)] * 2,
        compiler_params=_params(),
        name="l1_mixer",
    )(x, h0, c0, *_l1_weights(w))


def _l1_side_kernel(*refs, nb, tt_meta, nb_s):
    xm_ref, xs_ref, h0s_ref, c0s_ref = refs[0:4]
    wts = refs[4:4 + L1_WEIGHTS]
    xom_ref, h_ref, c_ref, xos_ref, hs_ref, cs_ref, a_sc, b_sc = refs[4 + L1_WEIGHTS:]
    h_ref[...] = jnp.zeros(h_ref.shape, F32)
    c_ref[...] = jnp.zeros(c_ref.shape, F32)
    _l1_body(xm_ref, wts, xom_ref, h_ref, None, c_ref, a_sc, b_sc, nb, tt_meta)
    hs_ref[...] = h0s_ref[...]
    _l1_body(xs_ref, wts, xos_ref, hs_ref, c0s_ref, cs_ref, a_sc, b_sc, nb_s, 1)


def _l1_side(x_meta, x_samp, h0_s, c0_s, w, nb, tt_meta):
    mrows, nb_s = nb * tt_meta, x_samp.shape[0]
    assert mrows == nb_s
    kern = functools.partial(_l1_side_kernel, nb=nb, tt_meta=tt_meta, nb_s=nb_s)
    cshape = _cache_shape(nb, tt_meta, RNN_CONV - 1, RNN_W)
    cs_shape = _cache_shape(nb_s, 1, RNN_CONV - 1, RNN_W)
    f32 = lambda *shape: jax.ShapeDtypeStruct(shape, F32)
    full = lambda shape: pl.BlockSpec(shape, lambda i: (0,) * len(shape))
    return pl.pallas_call(
        kern,
        grid=(1,),
        in_specs=[full((mrows, D_MODEL)), full((nb_s, D_MODEL)), full((nb_s, RNN_W)), full(cs_shape)]
                 + _l1_weight_specs(),
        out_specs=[full((mrows, D_MODEL)), full((nb, RNN_W)), full(cshape), full((nb_s, D_MODEL)),
                   full((nb_s, RNN_W)), full(cs_shape)],
        out_shape=[f32(mrows, D_MODEL), f32(nb, RNN_W), f32(*cshape), f32(nb_s, D_MODEL), f32(nb_s, RNN_W),
                   f32(*cs_shape)],
        scratch_shapes=[pltpu.VMEM((mrows, RNN_W), F32)] * 2,
        compiler_params=_params(),
        name="l1_side",
    )(x_meta, x_samp, h0_s, c0_s, *_l1_weights(w))


def _pack_gate(wblk):
    eye = jnp.eye(RNN_BLOCKS, dtype=wblk.dtype)
    dense = (wblk[:, :, None, :] * eye[:, None, :, None]).reshape(RNN_W, RNN_W)
    tiles = [[dense[wi * GATE_WIN + ni * LANES:wi * GATE_WIN + ni * LANES + GATE_K,
                    wi * GATE_WIN + ni * GATE_N:wi * GATE_WIN + (ni + 1) * GATE_N]
              for ni in range(GATE_WIN // GATE_N)] for wi in range(RNN_W // GATE_WIN)]
    return jnp.stack([jnp.stack(r) for r in tiles]).astype(BF16)


def _prep_weights(p):
    w = {}
    row = lambda v: v.reshape(1, -1).astype(F32)
    w_in = p["w_in_0"]
    c = 2 * GLA_K + 2 * GLA_V
    w["w_in_0"] = w_in.astype(BF16)
    w["w_lr"] = jnp.pad(w_in[:, c:c + GLA_RANK], ((0, 0), (0, LANES - GLA_RANK))).astype(BF16)
    c += GLA_RANK
    w["w_u"] = w_in[:, c:c + S5_W].astype(BF16)
    w["w_alpha"] = jnp.pad(p["w_alpha_0"], ((0, LANES - GLA_RANK), (0, 0))).astype(BF16)
    w["b_alpha"] = row(p["b_alpha_0"])
    w["norm_mix_0"] = row(p["norm_mix_0"])
    w["gla_norm"] = row(p["gla_norm_0"])

    are, aim, bbre, bbim = _s5_prep(p["s5_lam_re"], p["s5_lam_im"], p["s5_log_dt"], p["s5_b_re"],
                                    p["s5_b_im"])
    npack = S5_GROUPS // S5_PACK
    eye = jnp.eye(S5_PACK, dtype=F32)[None, :, None, :, None]
    grouped = lambda m: m.reshape(npack, S5_PACK, S5_H, S5_P)
    pack_b = lambda m: (grouped(m)[:, :, :, None, :] * eye).reshape(
        npack, S5_PACK * S5_H, S5_PACK * S5_P).astype(BF16)
    pack_c = lambda m: (jnp.swapaxes(grouped(m), 2, 3)[:, :, :, None, :] * eye).reshape(
        npack, S5_PACK * S5_P, S5_PACK * S5_H).astype(BF16)
    w["s5_are"] = are.reshape(1, S5_N)
    w["s5_aim"] = aim.reshape(1, S5_N)
    w["s5_bbre"] = pack_b(bbre)
    w["s5_bbim"] = pack_b(bbim)
    w["s5_cre"] = pack_c(p["s5_c_re"])
    w["s5_cim"] = pack_c(p["s5_c_im"])
    w["s5_d"] = row(p["s5_d"])
    w["s5_w_glu"] = p["s5_w_glu"].astype(BF16)
    w["s5_b_glu"] = row(p["s5_b_glu"])
    w["w_out_0"] = p["w_out_0"].astype(BF16)

    w["norm_mix_1"] = row(p["norm_mix_1"])
    w["w_in_1"] = p["w_in_1"].astype(BF16)
    w["rnn_conv_w"] = p["rnn_conv_w"].astype(F32)
    w["rnn_conv_b"] = row(p["rnn_conv_b"])
    w["rnn_wa"] = _pack_gate(p["rnn_w_a"])
    w["rnn_wx"] = _pack_gate(p["rnn_w_x"])
    w["rnn_b_a"] = row(p["rnn_b_a"])
    w["rnn_b_x"] = row(p["rnn_b_x"])
    w["rnn_lam"] = row(p["rnn_lam"])
    w["w_out_1"] = p["w_out_1"].astype(BF16)

    depth = p["norm_ffn"].shape[0]
    w["norm_ffn"] = p["norm_ffn"].reshape(depth, 1, D_MODEL)
    w["ffn_w_up"] = p["ffn_w_up"].astype(BF16)
    w["ffn_conv_w"] = p["ffn_conv_w"]
    w["ffn_conv_b"] = p["ffn_conv_b"].reshape(depth, 1, D_FF)
    w["ffn_w_down"] = p["ffn_w_down"].astype(BF16)
    w["norm_final"] = row(p["norm_final"])
    return w


def _tile_steps():
    return dict(l0_in=64, gla=32, l0_out=64, ffn=64, l1=64)


def _batch_major(cache, nb):
    jb, c = cache.shape
    return jnp.transpose(cache.reshape(jb // nb, nb, c), (1, 0, 2))


def kernel(x_prompt, x_sample, state_gla, state_s5_re, state_s5_im, state_rglru, cache_rglru_conv,
           cache_ffn_conv, meta_tokens, norm_mix_0, w_in_0, w_alpha_0, b_alpha_0, gla_norm_0,
           s5_lam_re, s5_lam_im, s5_log_dt, s5_b_re, s5_b_im, s5_c_re, s5_c_im, s5_d, s5_w_glu,
           s5_b_glu, w_out_0, norm_mix_1, w_in_1, rnn_conv_w, rnn_conv_b, rnn_w_a, rnn_b_a, rnn_w_x,
           rnn_b_x, rnn_lam, w_out_1, norm_ffn, ffn_w_up, ffn_conv_w, ffn_conv_b, ffn_w_down, norm_final):
    w = _prep_weights(dict(
        norm_mix_0=norm_mix_0, w_in_0=w_in_0, w_alpha_0=w_alpha_0, b_alpha_0=b_alpha_0,
        gla_norm_0=gla_norm_0, s5_lam_re=s5_lam_re, s5_lam_im=s5_lam_im, s5_log_dt=s5_log_dt,
        s5_b_re=s5_b_re, s5_b_im=s5_b_im, s5_c_re=s5_c_re, s5_c_im=s5_c_im, s5_d=s5_d,
        s5_w_glu=s5_w_glu, s5_b_glu=s5_b_glu, w_out_0=w_out_0, norm_mix_1=norm_mix_1, w_in_1=w_in_1,
        rnn_conv_w=rnn_conv_w, rnn_conv_b=rnn_conv_b, rnn_w_a=rnn_w_a, rnn_b_a=rnn_b_a,
        rnn_w_x=rnn_w_x, rnn_b_x=rnn_b_x, rnn_lam=rnn_lam, w_out_1=w_out_1, norm_ffn=norm_ffn,
        ffn_w_up=ffn_w_up, ffn_conv_w=ffn_conv_w, ffn_conv_b=ffn_conv_b, ffn_w_down=ffn_w_down,
        norm_final=norm_final))

    bp = x_prompt.shape[0]
    bs = x_sample.shape[0]
    tt = _tile_steps()
    mrows = bp * N_META
    assert mrows == bs

    x_side = jnp.concatenate([jnp.repeat(meta_tokens.astype(F32), bp, axis=0),
                              x_sample.reshape(bs, D_MODEL)], axis=0)
    s5_re_s = state_s5_re.reshape(bs, S5_N)
    s5_im_s = state_s5_im.reshape(bs, S5_N)

    gla_in = lambda z: (z[0], z[1], z[2], z[4], z[3])
    gn = w["gla_norm"]

    side = _l0_in(x_side, w, mrows + bs, 1)
    og_m, gla_m = _gla_chunk(gla_in(side), gn, jnp.zeros((GLA_HEADS, GLA_DV, bp * GLA_DK), F32), bp, N_META,
                             row_blocks=(2, 0))
    og_s, gla_s = _gla_step(gla_in(side), gn, state_gla, mrows)
    x_m, re_m, im_m, x_s, re_s, im_s = _l0_out_side(side[5], og_m, og_s, x_side, s5_re_s, s5_im_s, w, bp, N_META)
    x_m, fc0_m, x_s, fc0_s = _ffn_side(x_m, x_s, cache_ffn_conv, w, 0, bp, N_META, False)
    x_m, h_m, rc_m, x_s, h_s, rc_s = _l1_side(x_m, x_s, state_rglru, cache_rglru_conv, w, bp, N_META)
    _, fc1_m, ys, fc1_s = _ffn_side(x_m, x_s, cache_ffn_conv, w, 1, bp, N_META, True)

    main = _l0_in(x_prompt, w, bp, tt["l0_in"])
    og, gla_p = _gla_chunk(gla_in(main), gn, gla_m, bp, tt["gla"], chunks=GLA_CHUNKS_PER_STEP)
    x, re_p, im_p = _l0_out(main[5], og, x_prompt, re_m, im_m, w, tt["l0_out"])
    x, fc0_p = _ffn(x, fc0_m, w, 0, bp, tt["ffn"], False)
    x, h_p, rc_p = _l1(x, h_m, rc_m, w, bp, tt["l1"])
    yp, fc1_p = _ffn(x, fc1_m, w, 1, bp, tt["ffn"], True, batch_major_out=True)

    grp = lambda z, nb: z.reshape(nb, S5_GROUPS, S5_P)
    return (yp, ys.reshape(bs, 1, D_MODEL), _gla_state_from_stacked(gla_p), gla_s,
            grp(re_p, bp), grp(re_s, bs), grp(im_p, bp), grp(im_s, bs), h_p, h_s,
            _batch_major(rc_p, bp), rc_s,
            jnp.stack([_batch_major(fc0_p, bp), _batch_major(fc1_p, bp)]), jnp.stack([fc0_s, fc1_s]))
```

```python
import functools

import jax
import jax.numpy as jnp
from jax import lax
from jax.experimental import pallas as pl
from jax.experimental.pallas import tpu as pltpu

F32 = jnp.float32
BF16 = jnp.bfloat16

D_MODEL = 1024
N_META = 16
EPS = 1e-6
F32_TINY = 1.1754944e-38
GLA_HEADS = 4
GLA_DK = 64
GLA_DV = 128
GLA_RANK = 16
GLA_TAU = 16.0
GLA_K = GLA_HEADS * GLA_DK
GLA_V = GLA_HEADS * GLA_DV
S5_GROUPS = 32
S5_H = 16
S5_P = 64
S5_W = S5_GROUPS * S5_H
S5_N = S5_GROUPS * S5_P
RNN_W = 1536
RNN_BLOCKS = 16
RNN_BW = RNN_W // RNN_BLOCKS
RNN_C = 8.0
RNN_CONV = 4
D_FF = 2816
FFN_CONV = 3

LANES = 128
FF_CHUNK = 256
L1_SPLIT = 2
GLA_CHUNKS_PER_STEP = 2
S5_PACK = 8
GATE_WIN = 768
GATE_K = 512
GATE_N = 256
VMEM_LIMIT = 56 * 1024 * 1024


def _rms(x, g):
    return x * lax.rsqrt(jnp.mean(x * x, axis=-1, keepdims=True) + EPS) * g


def _sigmoid(x):
    return 0.5 * jnp.tanh(0.5 * x) + 0.5


def _sqrt_nonneg(t):
    return t * lax.rsqrt(jnp.maximum(t, F32_TINY))


def _mm(a, w):
    return jnp.dot(a.astype(BF16), w, preferred_element_type=F32)


def _const_spec(shape, index=None):
    idx = tuple(index) if index is not None else (0,) * len(shape)
    return pl.BlockSpec(shape, lambda i: idx, pipeline_mode=pl.Buffered(1))


def _row_spec(rows, cols):
    return pl.BlockSpec((rows, cols), lambda i: (i, 0))


def _seq_spec(nb, tt, cols):
    return pl.BlockSpec((nb, tt, cols), lambda i: (0, i, 0))


def _tm_scratch(rows, cols):
    return pltpu.VMEM((cols // LANES, rows, LANES), F32)


def _load_time_major(x_ref, tm_sc):
    if tm_sc is None:
        return x_ref[...]
    nb, tt, cols = x_ref.shape
    for b in range(nb):
        for j in range(cols // LANES):
            tm_sc[j, pl.ds(b, tt, stride=nb), :] = x_ref[b, :, j * LANES:(j + 1) * LANES]
    return jnp.concatenate([tm_sc[j] for j in range(cols // LANES)], axis=-1)


def _store_time_major(o_ref, val, tm_sc):
    if tm_sc is None:
        o_ref[...] = val
        return
    nb, tt, cols = o_ref.shape
    for j in range(cols // LANES):
        tm_sc[j] = val[:, j * LANES:(j + 1) * LANES]
    for b in range(nb):
        for j in range(cols // LANES):
            o_ref[b, :, j * LANES:(j + 1) * LANES] = tm_sc[j, pl.ds(b, tt, stride=nb), :]


def _cache_shape(nb, tt, taps, width):
    return (nb, taps, width) if tt == 1 else (taps * nb, width)


def _params(sem="arbitrary"):
    return pltpu.CompilerParams(dimension_semantics=(sem,), vmem_limit_bytes=VMEM_LIMIT)


def _s5_prep_kernel(lr_ref, li_ref, ldt_ref, brt_ref, bit_ref, are_ref, aim_ref, bbre_ref, bbim_ref):
    lr = lr_ref[...]
    li = li_ref[...]
    dt = jnp.exp(ldt_ref[...])
    mag = jnp.exp(lr * dt)
    ab_re = mag * jnp.cos(li * dt)
    ab_im = mag * jnp.sin(li * dt)
    den = lr * lr + li * li
    nr = ab_re - 1.0
    ni = ab_im
    f_re = (nr * lr + ni * li) / den
    f_im = (ni * lr - nr * li) / den
    are_ref[...] = ab_re
    aim_ref[...] = ab_im
    brt = brt_ref[...]
    bit = bit_ref[...]
    bbre_ref[...] = f_re[:, None, :] * brt - f_im[:, None, :] * bit
    bbim_ref[...] = f_re[:, None, :] * bit + f_im[:, None, :] * brt


def _s5_prep(lam_re, lam_im, log_dt, b_re, b_im):
    g, p, h = b_re.shape
    brt = jnp.transpose(b_re, (0, 2, 1))
    bit = jnp.transpose(b_im, (0, 2, 1))
    return pl.pallas_call(
        _s5_prep_kernel,
        out_shape=(jax.ShapeDtypeStruct((g, p), F32), jax.ShapeDtypeStruct((g, p), F32),
                   jax.ShapeDtypeStruct((g, h, p), F32), jax.ShapeDtypeStruct((g, h, p), F32)),
        name="s5_prep",
    )(lam_re, lam_im, log_dt.reshape(g, 1), brt, bit)


def _is_first():
    return pl.program_id(0) == 0


def _side_spec(rows, cols, block=0):
    return pl.BlockSpec((rows, cols), lambda i: (block, 0))


def _l0_in_body(x, wts, outs):
    g_ref, wq_ref, wk_ref, wv_ref, wg_ref, wu_ref, wlr_ref, wal_ref, bal_ref = wts
    q_ref, k_ref, v_ref, gs_ref, la_ref, u_ref = outs
    xn = _rms(x, g_ref[...]).astype(BF16)
    q_ref[...] = _mm(xn, wq_ref[...]) * (GLA_DK ** -0.5)
    k_ref[...] = _mm(xn, wk_ref[...])
    v_ref[...] = _mm(xn, wv_ref[...])
    g = _mm(xn, wg_ref[...])
    gs_ref[...] = g * _sigmoid(g)
    u_ref[...] = _mm(xn, wu_ref[...])
    lr = _mm(xn, wlr_ref[...])
    pre = _mm(lr, wal_ref[...]) + bal_ref[...]
    la_ref[...] = jax.nn.log_sigmoid(pre) * (1.0 / GLA_TAU)


def _l0_in_kernel(*refs):
    x_ref, wts, outs, tm_sc = refs[0], refs[1:10], refs[10:16], refs[16:]
    _l0_in_body(_load_time_major(x_ref, tm_sc[0] if tm_sc else None), wts, outs)


L0_IN_COLS = (GLA_K, GLA_K, GLA_V, GLA_V, GLA_K, S5_W)


def _l0_in(x, w, nb, tt):
    rows = nb * tt
    batch_major = x.ndim == 3
    n = x.shape[0] * x.shape[1] if batch_major else x.shape[0]
    return pl.pallas_call(
        _l0_in_kernel,
        grid=(n // rows,),
        in_specs=[_seq_spec(nb, tt, D_MODEL) if batch_major else _row_spec(rows, D_MODEL),
                  _const_spec((1, D_MODEL)),
                  _const_spec((D_MODEL, GLA_K), (0, 0)), _const_spec((D_MODEL, GLA_K), (0, 1)),
                  _const_spec((D_MODEL, GLA_V), (0, 1)), _const_spec((D_MODEL, GLA_V), (0, 2)),
                  _const_spec((D_MODEL, S5_W)), _const_spec((D_MODEL, LANES)),
                  _const_spec((LANES, GLA_K)), _const_spec((1, GLA_K))],
        out_specs=[_row_spec(rows, c) for c in L0_IN_COLS],
        out_shape=[jax.ShapeDtypeStruct((n, c), F32) for c in L0_IN_COLS],
        scratch_shapes=[_tm_scratch(rows, D_MODEL)] if batch_major else [],
        compiler_params=_params("parallel"),
        name="l0_in",
    )(x, w["norm_mix_0"], w["w_in_0"], w["w_in_0"], w["w_in_0"], w["w_in_0"], w["w_u"], w["w_lr"],
      w["w_alpha"], w["b_alpha"])


def _gla_chunk_body(ins, gn_ref, o_ref, st_ref, b_sc, nb, c):
    q_ref, k_ref, v_ref, la_ref, gs_ref = ins
    rows = nb * c
    seq_mask = nb - 1

    def cum_body(t, run):
        rws = pl.ds(pl.multiple_of(t * nb, nb), nb)
        run = run + la_ref[rws, :]
        b_sc[rws, :] = run
        return run

    bl = lax.fori_loop(0, c, cum_body, jnp.zeros((nb, GLA_K), F32))
    b = b_sc[...]
    k = k_ref[...]
    qt = q_ref[...] * jnp.exp(b)
    kt = k * jnp.exp(-b)
    kh = k * jnp.exp(jnp.concatenate([bl] * c, axis=0) - b)
    gam = jnp.exp(bl)

    ri = lax.broadcasted_iota(jnp.int32, (rows, rows), 0)
    ci = lax.broadcasted_iota(jnp.int32, (rows, rows), 1)
    pair_ok = (ri >= ci) & (((ri - ci) & seq_mask) == 0)
    xw = nb * GLA_DK
    own_blk = (lax.broadcasted_iota(jnp.int32, (rows, xw), 1) // GLA_DK
               == (lax.broadcasted_iota(jnp.int32, (rows, xw), 0) & seq_mask))
    own_blk_seq = (lax.broadcasted_iota(jnp.int32, (nb, xw), 1) // GLA_DK
                   == lax.broadcasted_iota(jnp.int32, (nb, xw), 0))
    reps = xw // LANES

    def head_dup(z, h):
        blk = z[:, (h // 2) * LANES:(h // 2 + 1) * LANES]
        rolled = pltpu.roll(blk, GLA_DK, axis=1)
        low = lax.broadcasted_iota(jnp.int32, blk.shape, 1) < GLA_DK
        return jnp.where(low, blk, rolled) if h % 2 == 0 else jnp.where(low, rolled, blk)

    def expand(zd, own):
        return jnp.where(own, jnp.concatenate([zd] * reps, axis=1), 0.0)

    for h in range(GLA_HEADS):
        vs = slice(h * GLA_DV, (h + 1) * GLA_DV)
        qd, ktd, khd = head_dup(qt, h), head_dup(kt, h), head_dup(kh, h)
        vb = v_ref[:, vs].astype(BF16)
        st = st_ref[h]
        att = lax.dot_general(qd[:, :GLA_DK].astype(BF16), ktd[:, :GLA_DK].astype(BF16),
                              (((1,), (1,)), ((), ())), preferred_element_type=F32)
        att = jnp.where(pair_ok, att, 0.0).astype(BF16)
        o = jnp.dot(att, vb, preferred_element_type=F32) + lax.dot_general(
            expand(qd, own_blk).astype(BF16), st.astype(BF16), (((1,), (1,)), ((), ())),
            preferred_element_type=F32)
        upd = lax.dot_general(vb, expand(khd, own_blk).astype(BF16), (((0,), (0,)), ((), ())),
                              preferred_element_type=F32)
        gam_row = jnp.sum(expand(head_dup(gam, h), own_blk_seq), axis=0, keepdims=True)
        st_ref[h] = st * gam_row + upd
        o_ref[:, vs] = _rms(o, gn_ref[:, vs]) * gs_ref[:, vs]


def _gla_chunk_kernel(*refs, nb, c, chunks):
    ins, gn_ref, st0_ref, o_ref, st_ref, b_sc = refs[0:5], refs[5], refs[6], refs[7], refs[8], refs[9]

    @pl.when(_is_first())
    def _():
        st_ref[...] = st0_ref[...]

    for j in range(chunks):
        rws = pl.ds(j * nb * c, nb * c)
        _gla_chunk_body([r.at[rws] for r in ins], gn_ref, o_ref.at[rws], st_ref, b_sc, nb, c)


def _gla_chunk(ins, gn, st0, nb, c, chunks=1, row_blocks=None):
    assert nb & (nb - 1) == 0 and (nb * GLA_DK) % LANES == 0
    rows = nb * c * chunks
    n = ins[0].shape[0] // row_blocks[0] if row_blocks else ins[0].shape[0]
    first = (row_blocks[1] * n) // rows if row_blocks else 0
    sshape = (GLA_HEADS, GLA_DV, nb * GLA_DK)
    cols = (GLA_K, GLA_K, GLA_V, GLA_K, GLA_V)
    kern = functools.partial(_gla_chunk_kernel, nb=nb, c=c, chunks=chunks)
    return pl.pallas_call(
        kern,
        grid=(n // rows,),
        in_specs=[pl.BlockSpec((rows, w_), lambda i: (first + i, 0)) for w_ in cols]
                 + [_const_spec((1, GLA_V)), _const_spec(sshape)],
        out_specs=[_row_spec(rows, GLA_V), pl.BlockSpec(sshape, lambda i: (0, 0, 0))],
        out_shape=[jax.ShapeDtypeStruct((n, GLA_V), F32), jax.ShapeDtypeStruct(sshape, F32)],
        scratch_shapes=[pltpu.VMEM((nb * c, GLA_K), F32)],
        compiler_params=_params(),
        name="gla_chunk",
    )(*ins, gn, st0)


def _gla_state_from_stacked(st):
    nb = st.shape[2] // GLA_DK
    return jnp.transpose(st.reshape(GLA_HEADS, GLA_DV, nb, GLA_DK), (2, 0, 3, 1))


def _gla_step_kernel(q_ref, k_ref, la_ref, v_ref, gs_ref, gn_ref, s0_ref, o_ref, s_ref):
    qT = q_ref[...].T
    kT = k_ref[...].T
    aT = jnp.exp(la_ref[...]).T
    o_rows = []
    for b in range(q_ref.shape[0]):
        s_new = aT[:, b:b + 1] * s0_ref[b] + kT[:, b:b + 1] * v_ref[b:b + 1, :]
        s_ref[b] = s_new
        o_rows.append(jnp.sum(qT[:, b:b + 1] * s_new, axis=0, keepdims=True))
    o = jnp.concatenate(o_rows, axis=0)
    o_ref[...] = _rms(o, gn_ref[...]) * gs_ref[...]


def _gla_step(side, gn, s0, row0):
    nb = s0.shape[0]
    blk = row0 // nb
    q, k, v, la, gs = side
    heads = lambda z: jnp.transpose(z[row0:row0 + nb].reshape(nb, GLA_HEADS, GLA_DK), (1, 0, 2))
    hspec = pl.BlockSpec((None, nb, GLA_DK), lambda h: (h, 0, 0))
    vspec = pl.BlockSpec((nb, GLA_DV), lambda h: (blk, h))
    ospec = pl.BlockSpec((nb, GLA_DV), lambda h: (0, h))
    sspec = pl.BlockSpec((nb, None, GLA_DK, GLA_DV), lambda h: (0, h, 0, 0))
    return pl.pallas_call(
        _gla_step_kernel,
        grid=(GLA_HEADS,),
        in_specs=[hspec, hspec, hspec, vspec, vspec, pl.BlockSpec((1, GLA_DV), lambda h: (0, h)), sspec],
        out_specs=[ospec, sspec],
        out_shape=[jax.ShapeDtypeStruct((nb, GLA_V), F32),
                   jax.ShapeDtypeStruct((nb, GLA_HEADS, GLA_DK, GLA_DV), F32)],
        compiler_params=_params("parallel"),
        name="gla_step",
    )(heads(q), heads(k), heads(la), v, gs, gn, s0)


def _l0_out_body(u_ref, og_ref, x, wts, xo_ref, xre_ref, xim_ref, sr_sc, si_sc, nb, tt):
    (bbre_ref, bbim_ref, are_ref, aim_ref, cre_ref, cim_ref, d_ref, wglu_ref, bglu_ref, woa_ref,
     wob_ref) = wts
    u = u_ref[...]
    ub = u.astype(BF16)
    kin = S5_PACK * S5_H
    kst = S5_PACK * S5_P
    npack = S5_GROUPS // S5_PACK
    ys = []
    for j in range(npack):
        cs = slice(j * kst, (j + 1) * kst)
        uj = ub[:, j * kin:(j + 1) * kin]
        bur = jnp.dot(uj, bbre_ref[j], preferred_element_type=F32)
        bui = jnp.dot(uj, bbim_ref[j], preferred_element_type=F32)
        ar = jnp.broadcast_to(are_ref[:, cs], (nb, kst))
        ai = jnp.broadcast_to(aim_ref[:, cs], (nb, kst))
        xr = xre_ref[:, cs]
        xi = xim_ref[:, cs]
        for t in range(tt):
            rows = slice(t * nb, (t + 1) * nb)
            xr, xi = ar * xr - ai * xi + bur[rows], ar * xi + ai * xr + bui[rows]
            sr_sc[rows, cs] = xr
            si_sc[rows, cs] = xi
        xre_ref[:, cs] = xr
        xim_ref[:, cs] = xi
        ys.append(jnp.dot(sr_sc[:, cs].astype(BF16), cre_ref[j], preferred_element_type=F32)
                  - jnp.dot(si_sc[:, cs].astype(BF16), cim_ref[j], preferred_element_type=F32))
    y = jnp.concatenate(ys, axis=-1) + d_ref[...] * u
    y = jax.nn.gelu(y)
    y = y * _sigmoid(_mm(y, wglu_ref[...]) + bglu_ref[...])
    xo_ref[...] = x + _mm(og_ref[...], woa_ref[...]) + _mm(y, wob_ref[...])


L0_OUT_WEIGHTS = 11


def _l0_out_weight_specs():
    npack = S5_GROUPS // S5_PACK
    kin, kst = S5_PACK * S5_H, S5_PACK * S5_P
    return [_const_spec((npack, kin, kst)), _const_spec((npack, kin, kst)),
            _const_spec((1, S5_N)), _const_spec((1, S5_N)),
            _const_spec((npack, kst, kin)), _const_spec((npack, kst, kin)),
            _const_spec((1, S5_W)), _const_spec((S5_W, S5_W)), _const_spec((1, S5_W)),
            _const_spec((GLA_V, D_MODEL), (0, 0)), _const_spec((S5_W, D_MODEL), (1, 0))]


def _l0_out_weights(w):
    return (w["s5_bbre"], w["s5_bbim"], w["s5_are"], w["s5_aim"], w["s5_cre"], w["s5_cim"],
            w["s5_d"], w["s5_w_glu"], w["s5_b_glu"], w["w_out_0"], w["w_out_0"])


def _l0_out_kernel(*refs, nb, tt):
    u_ref, og_ref, x_ref, xr0_ref, xi0_ref = refs[0:5]
    wts = refs[5:5 + L0_OUT_WEIGHTS]
    xo_ref, xre_ref, xim_ref, sr_sc, si_sc, tm_sc = refs[5 + L0_OUT_WEIGHTS:]

    @pl.when(_is_first())
    def _():
        xre_ref[...] = xr0_ref[...]
        xim_ref[...] = xi0_ref[...]

    _l0_out_body(u_ref, og_ref, _load_time_major(x_ref, tm_sc), wts, xo_ref, xre_ref, xim_ref,
                 sr_sc, si_sc, nb, tt)


def _l0_out(u, og, x, xr0, xi0, w, tt):
    nb = x.shape[0]
    n = u.shape[0]
    rows = nb * tt
    kern = functools.partial(_l0_out_kernel, nb=nb, tt=tt)
    st_spec = pl.BlockSpec((nb, S5_N), lambda i: (0, 0))
    f32 = lambda *shape: jax.ShapeDtypeStruct(shape, F32)
    return pl.pallas_call(
        kern,
        grid=(n // rows,),
        in_specs=[_row_spec(rows, S5_W), _row_spec(rows, GLA_V), _seq_spec(nb, tt, D_MODEL),
                  _const_spec((nb, S5_N)), _const_spec((nb, S5_N))] + _l0_out_weight_specs(),
        out_specs=[_row_spec(rows, D_MODEL), st_spec, st_spec],
        out_shape=[f32(n, D_MODEL), f32(nb, S5_N), f32(nb, S5_N)],
        scratch_shapes=[pltpu.VMEM((rows, S5_N), F32)] * 2 + [_tm_scratch(rows, D_MODEL)],
        compiler_params=_params(),
        name="l0_out",
    )(u, og, x, xr0, xi0, *_l0_out_weights(w))


def _l0_out_side_kernel(*refs, nb, tt_meta, nb_s):
    um_ref, ogm_ref, xm_ref, us_ref, ogs_ref, xs_ref, xr0s_ref, xi0s_ref = refs[0:8]
    wts = refs[8:8 + L0_OUT_WEIGHTS]
    xom_ref, xre_ref, xim_ref, xos_ref, xres_ref, xims_ref, sr_sc, si_sc = refs[8 + L0_OUT_WEIGHTS:]
    xre_ref[...] = jnp.zeros(xre_ref.shape, F32)
    xim_ref[...] = jnp.zeros(xim_ref.shape, F32)
    _l0_out_body(um_ref, ogm_ref, xm_ref[...], wts, xom_ref, xre_ref, xim_ref, sr_sc, si_sc, nb, tt_meta)
    xres_ref[...] = xr0s_ref[...]
    xims_ref[...] = xi0s_ref[...]
    _l0_out_body(us_ref, ogs_ref, xs_ref[...], wts, xos_ref, xres_ref, xims_ref, sr_sc, si_sc, nb_s, 1)


def _l0_out_side(u_side, og_meta, og_samp, x_side, xr0_s, xi0_s, w, nb, tt_meta):
    mrows, nb_s = nb * tt_meta, xr0_s.shape[0]
    assert mrows == nb_s
    kern = functools.partial(_l0_out_side_kernel, nb=nb, tt_meta=tt_meta, nb_s=nb_s)
    f32 = lambda *shape: jax.ShapeDtypeStruct(shape, F32)
    full = lambda r, c: pl.BlockSpec((r, c), lambda i: (0, 0))
    return pl.pallas_call(
        kern,
        grid=(1,),
        in_specs=[_side_spec(mrows, S5_W, 0), full(mrows, GLA_V), _side_spec(mrows, D_MODEL, 0),
                  _side_spec(nb_s, S5_W, 1), full(nb_s, GLA_V), _side_spec(nb_s, D_MODEL, 1),
                  full(nb_s, S5_N), full(nb_s, S5_N)] + _l0_out_weight_specs(),
        out_specs=[full(mrows, D_MODEL), full(nb, S5_N), full(nb, S5_N), full(nb_s, D_MODEL),
                   full(nb_s, S5_N), full(nb_s, S5_N)],
        out_shape=[f32(mrows, D_MODEL), f32(nb, S5_N), f32(nb, S5_N), f32(nb_s, D_MODEL),
                   f32(nb_s, S5_N), f32(nb_s, S5_N)],
        scratch_shapes=[pltpu.VMEM((mrows, S5_N), F32)] * 2,
        compiler_params=_params(),
        name="l0_out_side",
    )(u_side, og_meta, x_side, u_side, og_samp, x_side, xr0_s, xi0_s, *_l0_out_weights(w))


def _ffn_body(x, wts, c0_ref, c_ref, hm_sc, nb, tt, final):
    g_ref, wg_ref, wv_ref, cw_ref, cb_ref, wd_ref, gf_ref = wts
    rows = nb * tt
    xn = _rms(x, g_ref[...]).astype(BF16)
    for ci in range(D_FF // FF_CHUNK):
        cs = slice(ci * FF_CHUNK, (ci + 1) * FF_CHUNK)
        gate = jnp.dot(xn, wg_ref[:, cs], preferred_element_type=F32)
        val = jnp.dot(xn, wv_ref[:, cs], preferred_element_type=F32)
        if tt == 1:
            taps = [c0_ref[:, j, cs] for j in range(FFN_CONV - 1)] + [gate]
            for j in range(FFN_CONV - 1):
                c_ref[:, j, cs] = taps[j + 1]
        else:
            ext = jnp.concatenate([c_ref[:, cs], gate], axis=0)
            taps = [ext[j * nb:j * nb + rows] for j in range(FFN_CONV)]
            c_ref[:, cs] = ext[tt * nb:(tt + FFN_CONV - 1) * nb]
        y = cb_ref[:, cs] + taps[0] * cw_ref[0:1, cs]
        for j in range(1, FFN_CONV):
            y = y + taps[j] * cw_ref[j:j + 1, cs]
        hm_sc[:, cs] = (jax.nn.gelu(y) * val).astype(BF16)
    out = x + jnp.dot(hm_sc[...], wd_ref[...], preferred_element_type=F32)
    return _rms(out, gf_ref[...]) if final else out


FFN_WEIGHTS = 7


def _ffn_weight_specs(layer):
    return [_const_spec((None, 1, D_MODEL), (layer, 0, 0)),
            _const_spec((None, D_MODEL, D_FF), (layer, 0, 0)),
            _const_spec((None, D_MODEL, D_FF), (layer, 0, 1)),
            _const_spec((None, FFN_CONV, D_FF), (layer, 0, 0)),
            _const_spec((None, 1, D_FF), (layer, 0, 0)),
            _const_spec((None, D_FF, D_MODEL), (layer, 0, 0)), _const_spec((1, D_MODEL))]


def _ffn_weights(w):
    return (w["norm_ffn"], w["ffn_w_up"], w["ffn_w_up"], w["ffn_conv_w"], w["ffn_conv_b"], w["ffn_w_down"],
            w["norm_final"])


def _ffn_kernel(*refs, nb, tt, final, batch_major_out):
    x_ref, c0_ref = refs[0:2]
    wts = refs[2:2 + FFN_WEIGHTS]
    xo_ref, c_ref, hm_sc = refs[2 + FFN_WEIGHTS:5 + FFN_WEIGHTS]
    tm_sc = refs[5 + FFN_WEIGHTS] if batch_major_out else None

    @pl.when(_is_first())
    def _():
        c_ref[...] = c0_ref[...]

    _store_time_major(xo_ref, _ffn_body(x_ref[...], wts, None, c_ref, hm_sc, nb, tt, final), tm_sc)


def _ffn(x, c0, w, layer, nb, tt, final, batch_major_out=False):
    n = x.shape[0]
    rows = nb * tt
    kern = functools.partial(_ffn_kernel, nb=nb, tt=tt, final=final, batch_major_out=batch_major_out)
    if batch_major_out:
        o_spec, o_shape = _seq_spec(nb, tt, D_MODEL), (nb, n // nb, D_MODEL)
    else:
        o_spec, o_shape = _row_spec(rows, D_MODEL), (n, D_MODEL)
    cshape = _cache_shape(nb, tt, FFN_CONV - 1, D_FF)
    f32 = lambda *shape: jax.ShapeDtypeStruct(shape, F32)
    return pl.pallas_call(
        kern,
        grid=(n // rows,),
        in_specs=[_row_spec(rows, D_MODEL), _const_spec(cshape)] + _ffn_weight_specs(layer),
        out_specs=[o_spec, pl.BlockSpec(cshape, lambda i: (0, 0))],
        out_shape=[f32(*o_shape), f32(*cshape)],
        scratch_shapes=[pltpu.VMEM((rows, D_FF), BF16)]
                       + ([_tm_scratch(rows, D_MODEL)] if batch_major_out else []),
        compiler_params=_params(),
        name="ffn%d" % layer,
    )(x, c0, *_ffn_weights(w))


def _ffn_side_kernel(*refs, nb, tt_meta, nb_s, final):
    xm_ref, xs_ref, c0s_ref = refs[0:3]
    wts = refs[3:3 + FFN_WEIGHTS]
    xom_ref, c_ref, xos_ref, cs_ref, hm_sc = refs[3 + FFN_WEIGHTS:]
    c_ref[...] = jnp.zeros(c_ref.shape, F32)
    xom_ref[...] = _ffn_body(xm_ref[...], wts, None, c_ref, hm_sc, nb, tt_meta, final)
    xos_ref[...] = _ffn_body(xs_ref[...], wts, c0s_ref, cs_ref, hm_sc, nb_s, 1, final)


def _ffn_side(x_meta, x_samp, c0_s, w, layer, nb, tt_meta, final):
    mrows, nb_s = nb * tt_meta, x_samp.shape[0]
    assert mrows == nb_s
    kern = functools.partial(_ffn_side_kernel, nb=nb, tt_meta=tt_meta, nb_s=nb_s, final=final)
    cshape = _cache_shape(nb, tt_meta, FFN_CONV - 1, D_FF)
    cs_shape = _cache_shape(nb_s, 1, FFN_CONV - 1, D_FF)
    f32 = lambda *shape: jax.ShapeDtypeStruct(shape, F32)
    full = lambda shape: pl.BlockSpec(shape, lambda i: (0,) * len(shape))
    return pl.pallas_call(
        kern,
        grid=(1,),
        in_specs=[full((mrows, D_MODEL)), full((nb_s, D_MODEL)),
                  pl.BlockSpec((None,) + cs_shape, lambda i: (layer, 0, 0, 0))] + _ffn_weight_specs(layer),
        out_specs=[full((mrows, D_MODEL)), full(cshape), full((nb_s, D_MODEL)), full(cs_shape)],
        out_shape=[f32(mrows, D_MODEL), f32(*cshape), f32(nb_s, D_MODEL), f32(*cs_shape)],
        scratch_shapes=[pltpu.VMEM((mrows, D_FF), BF16)],
        compiler_params=_params(),
        name="ffn%d_side" % layer,
    )(x_meta, x_samp, c0_s, *_ffn_weights(w))


def _l1_body(x_ref, wts, xo_ref, h_ref, c0_ref, c_ref, a_sc, b_sc, nb, tt):
    g_ref, wgt_ref, wxr_ref, cw_ref, cb_ref, wa_ref, ba_ref, wx_ref, bx_ref, lam_ref, wo_ref = wts
    sp = jax.nn.softplus(-lam_ref[...])
    nsplit = L1_SPLIT if tt % L1_SPLIT == 0 else 1
    th = tt // nsplit
    rows = nb * th
    carry = c_ref[...] if tt > 1 else None
    h = h_ref[...]
    for part in range(nsplit):
        prow = slice(part * rows, (part + 1) * rows)
        x = x_ref[prow, :]
        xn = _rms(x, g_ref[...]).astype(BF16)
        xr = jnp.dot(xn, wxr_ref[...], preferred_element_type=F32)
        if tt == 1:
            taps = [c0_ref[:, j, :] for j in range(RNN_CONV - 1)] + [xr]
            for j in range(RNN_CONV - 1):
                c_ref[:, j, :] = taps[j + 1]
        else:
            ext = jnp.concatenate([carry, xr], axis=0)
            taps = [ext[j * nb:j * nb + rows] for j in range(RNN_CONV)]
            carry = ext[th * nb:(th + RNN_CONV - 1) * nb]
        xc = cb_ref[...] + taps[0] * cw_ref[0:1, :]
        for j in range(1, RNN_CONV):
            xc = xc + taps[j] * cw_ref[j:j + 1, :]

        xcb = xc.astype(BF16)
        rs, gs = [], []
        for wi in range(RNN_W // GATE_WIN):
            for ni in range(GATE_WIN // GATE_N):
                k0 = wi * GATE_WIN + ni * LANES
                lhs = xcb[:, k0:k0 + GATE_K]
                rs.append(jnp.dot(lhs, wa_ref[wi, ni], preferred_element_type=F32))
                gs.append(jnp.dot(lhs, wx_ref[wi, ni], preferred_element_type=F32))
        r = _sigmoid(jnp.concatenate(rs, axis=-1) + ba_ref[...])
        ig = _sigmoid(jnp.concatenate(gs, axis=-1) + bx_ref[...])
        log_a = (-RNN_C) * r * sp
        a = jnp.exp(log_a)
        a_sc[prow, :] = a
        b_sc[prow, :] = _sqrt_nonneg(jnp.tanh(-log_a) * (a * a + 1.0)) * (ig * xc)
        gg = jax.nn.gelu(jnp.dot(xn, wgt_ref[...], preferred_element_type=F32))

        for t in range(part * th, (part + 1) * th):
            rws = slice(t * nb, (t + 1) * nb)
            h = a_sc[rws, :] * h + b_sc[rws, :]
            b_sc[rws, :] = h
        xo_ref[prow, :] = x + _mm(b_sc[prow, :] * gg, wo_ref[...])
    if tt > 1:
        c_ref[...] = carry
    h_ref[...] = h


L1_WEIGHTS = 11


def _l1_weight_specs():
    gshape = (RNN_W // GATE_WIN, GATE_WIN // GATE_N, GATE_K, GATE_N)
    return [_const_spec((1, D_MODEL)),
            _const_spec((D_MODEL, RNN_W), (0, 0)), _const_spec((D_MODEL, RNN_W), (0, 1)),
            _const_spec((RNN_CONV, RNN_W)), _const_spec((1, RNN_W)),
            _const_spec(gshape), _const_spec((1, RNN_W)),
            _const_spec(gshape), _const_spec((1, RNN_W)),
            _const_spec((1, RNN_W)), _const_spec((RNN_W, D_MODEL))]


def _l1_weights(w):
    return (w["norm_mix_1"], w["w_in_1"], w["w_in_1"], w["rnn_conv_w"], w["rnn_conv_b"], w["rnn_wa"],
            w["rnn_b_a"], w["rnn_wx"], w["rnn_b_x"], w["rnn_lam"], w["w_out_1"])


def _l1_kernel(*refs, nb, tt):
    x_ref, h0_ref, c0_ref = refs[0:3]
    wts = refs[3:3 + L1_WEIGHTS]
    xo_ref, h_ref, c_ref, a_sc, b_sc = refs[3 + L1_WEIGHTS:]

    @pl.when(_is_first())
    def _():
        h_ref[...] = h0_ref[...]
        c_ref[...] = c0_ref[...]

    _l1_body(x_ref, wts, xo_ref, h_ref, None, c_ref, a_sc, b_sc, nb, tt)


def _l1(x, h0, c0, w, nb, tt):
    n = x.shape[0]
    rows = nb * tt
    kern = functools.partial(_l1_kernel, nb=nb, tt=tt)
    cshape = _cache_shape(nb, tt, RNN_CONV - 1, RNN_W)
    f32 = lambda *shape: jax.ShapeDtypeStruct(shape, F32)
    return pl.pallas_call(
        kern,
        grid=(n // rows,),
        in_specs=[_row_spec(rows, D_MODEL), _const_spec((nb, RNN_W)), _const_spec(cshape)] + _l1_weight_specs(),
        out_specs=[_row_spec(rows, D_MODEL), pl.BlockSpec((nb, RNN_W), lambda i: (0, 0)),
                   pl.BlockSpec(cshape, lambda i: (0, 0))],
        out_shape=[f32(n, D_MODEL), f32(nb, RNN_W), f32(*cshape)],
        scratch_shapes=[pltpu.VMEM((rows, RNN_W), F32)] * 2,
        compiler_params=_params(),
        name="l1_mixer",
    )(x, h0, c0, *_l1_weights(w))


def _l1_side_kernel(*refs, nb, tt_meta, nb_s):
    xm_ref, xs_ref, h0s_ref, c0s_ref = refs[0:4]
    wts = refs[4:4 + L1_WEIGHTS]
    xom_ref, h_ref, c_ref, xos_ref, hs_ref, cs_ref, a_sc, b_sc = refs[4 + L1_WEIGHTS:]
    h_ref[...] = jnp.zeros(h_ref.shape, F32)
    c_ref[...] = jnp.zeros(c_ref.shape, F32)
    _l1_body(xm_ref, wts, xom_ref, h_ref, None, c_ref, a_sc, b_sc, nb, tt_meta)
    hs_ref[...] = h0s_ref[...]
    _l1_body(xs_ref, wts, xos_ref, hs_ref, c0s_ref, cs_ref, a_sc, b_sc, nb_s, 1)


def _l1_side(x_meta, x_samp, h0_s, c0_s, w, nb, tt_meta):
    mrows, nb_s = nb * tt_meta, x_samp.shape[0]
    assert mrows == nb_s
    kern = functools.partial(_l1_side_kernel, nb=nb, tt_meta=tt_meta, nb_s=nb_s)
    cshape = _cache_shape(nb, tt_meta, RNN_CONV - 1, RNN_W)
    cs_shape = _cache_shape(nb_s, 1, RNN_CONV - 1, RNN_W)
    f32 = lambda *shape: jax.ShapeDtypeStruct(shape, F32)
    full = lambda shape: pl.BlockSpec(shape, lambda i: (0,) * len(shape))
    return pl.pallas_call(
        kern,
        grid=(1,),
        in_specs=[full((mrows, D_MODEL)), full((nb_s, D_MODEL)), full((nb_s, RNN_W)), full(cs_shape)]
                 + _l1_weight_specs(),
        out_specs=[full((mrows, D_MODEL)), full((nb, RNN_W)), full(cshape), full((nb_s, D_MODEL)),
                   full((nb_s, RNN_W)), full(cs_shape)],
        out_shape=[f32(mrows, D_MODEL), f32(nb, RNN_W), f32(*cshape), f32(nb_s, D_MODEL), f32(nb_s, RNN_W),
                   f32(*cs_shape)],
        scratch_shapes=[pltpu.VMEM((mrows, RNN_W), F32)] * 2,
        compiler_params=_params(),
        name="l1_side",
    )(x_meta, x_samp, h0_s, c0_s, *_l1_weights(w))


def _pack_gate_kernel(wa_ref, wx_ref, oa_ref, ox_ref):
    tiles_per_win = GATE_WIN // GATE_N
    for w_ref, o_ref in ((wa_ref, oa_ref), (wx_ref, ox_ref)):
        o_ref[...] = jnp.zeros(o_ref.shape, o_ref.dtype)
        for n in range(RNN_BLOCKS):
            pos = n * RNN_BW
            wi = pos // GATE_WIN
            for ni in range(tiles_per_win):
                k0 = wi * GATE_WIN + ni * LANES
                n0 = wi * GATE_WIN + ni * GATE_N
                lo, hi = max(pos, n0), min(pos + RNN_BW, n0 + GATE_N)
                if lo < hi:
                    o_ref[wi, ni, pos - k0:pos - k0 + RNN_BW, lo - n0:hi - n0] = (
                        w_ref[n][:, lo - pos:hi - pos].astype(o_ref.dtype))


def _pack_gates(wa, wx):
    gshape = (RNN_W // GATE_WIN, GATE_WIN // GATE_N, GATE_K, GATE_N)
    return pl.pallas_call(
        _pack_gate_kernel,
        out_shape=(jax.ShapeDtypeStruct(gshape, BF16), jax.ShapeDtypeStruct(gshape, BF16)),
        name="pack_gates",
    )(wa, wx)


def _prep_weights(p):
    w = {}
    row = lambda v: v.reshape(1, -1).astype(F32)
    w_in = p["w_in_0"]
    c = 2 * GLA_K + 2 * GLA_V
    w["w_in_0"] = w_in.astype(BF16)
    w["w_lr"] = jnp.pad(w_in[:, c:c + GLA_RANK], ((0, 0), (0, LANES - GLA_RANK))).astype(BF16)
    c += GLA_RANK
    w["w_u"] = w_in[:, c:c + S5_W].astype(BF16)
    w["w_alpha"] = jnp.pad(p["w_alpha_0"], ((0, LANES - GLA_RANK), (0, 0))).astype(BF16)
    w["b_alpha"] = row(p["b_alpha_0"])
    w["norm_mix_0"] = row(p["norm_mix_0"])
    w["gla_norm"] = row(p["gla_norm_0"])

    are, aim, bbre, bbim = _s5_prep(p["s5_lam_re"], p["s5_lam_im"], p["s5_log_dt"], p["s5_b_re"],
                                    p["s5_b_im"])
    npack = S5_GROUPS // S5_PACK
    eye = jnp.eye(S5_PACK, dtype=F32)[None, :, None, :, None]
    grouped = lambda m: m.reshape(npack, S5_PACK, S5_H, S5_P)
    pack_b = lambda m: (grouped(m)[:, :, :, None, :] * eye).reshape(
        npack, S5_PACK * S5_H, S5_PACK * S5_P).astype(BF16)
    pack_c = lambda m: (jnp.swapaxes(grouped(m), 2, 3)[:, :, :, None, :] * eye).reshape(
        npack, S5_PACK * S5_P, S5_PACK * S5_H).astype(BF16)
    w["s5_are"] = are.reshape(1, S5_N)
    w["s5_aim"] = aim.reshape(1, S5_N)
    w["s5_bbre"] = pack_b(bbre)
    w["s5_bbim"] = pack_b(bbim)
    w["s5_cre"] = pack_c(p["s5_c_re"])
    w["s5_cim"] = pack_c(p["s5_c_im"])
    w["s5_d"] = row(p["s5_d"])
    w["s5_w_glu"] = p["s5_w_glu"].astype(BF16)
    w["s5_b_glu"] = row(p["s5_b_glu"])
    w["w_out_0"] = p["w_out_0"].astype(BF16)

    w["norm_mix_1"] = row(p["norm_mix_1"])
    w["w_in_1"] = p["w_in_1"].astype(BF16)
    w["rnn_conv_w"] = p["rnn_conv_w"].astype(F32)
    w["rnn_conv_b"] = row(p["rnn_conv_b"])
    w["rnn_wa"], w["rnn_wx"] = _pack_gates(p["rnn_w_a"], p["rnn_w_x"])
    w["rnn_b_a"] = row(p["rnn_b_a"])
    w["rnn_b_x"] = row(p["rnn_b_x"])
    w["rnn_lam"] = row(p["rnn_lam"])
    w["w_out_1"] = p["w_out_1"].astype(BF16)

    depth = p["norm_ffn"].shape[0]
    w["norm_ffn"] = p["norm_ffn"].reshape(depth, 1, D_MODEL)
    w["ffn_w_up"] = p["ffn_w_up"].astype(BF16)
    w["ffn_conv_w"] = p["ffn_conv_w"]
    w["ffn_conv_b"] = p["ffn_conv_b"].reshape(depth, 1, D_FF)
    w["ffn_w_down"] = p["ffn_w_down"].astype(BF16)
    w["norm_final"] = row(p["norm_final"])
    return w


def _tile_steps():
    return dict(l0_in=128, gla=32, l0_out=64, ffn=64, l1=64)


def _batch_major(cache, nb):
    jb, c = cache.shape
    return jnp.transpose(cache.reshape(jb // nb, nb, c), (1, 0, 2))


def kernel(x_prompt, x_sample, state_gla, state_s5_re, state_s5_im, state_rglru, cache_rglru_conv,
           cache_ffn_conv, meta_tokens, norm_mix_0, w_in_0, w_alpha_0, b_alpha_0, gla_norm_0,
           s5_lam_re, s5_lam_im, s5_log_dt, s5_b_re, s5_b_im, s5_c_re, s5_c_im, s5_d, s5_w_glu,
           s5_b_glu, w_out_0, norm_mix_1, w_in_1, rnn_conv_w, rnn_conv_b, rnn_w_a, rnn_b_a, rnn_w_x,
           rnn_b_x, rnn_lam, w_out_1, norm_ffn, ffn_w_up, ffn_conv_w, ffn_conv_b, ffn_w_down, norm_final):
    w = _prep_weights(dict(
        norm_mix_0=norm_mix_0, w_in_0=w_in_0, w_alpha_0=w_alpha_0, b_alpha_0=b_alpha_0,
        gla_norm_0=gla_norm_0, s5_lam_re=s5_lam_re, s5_lam_im=s5_lam_im, s5_log_dt=s5_log_dt,
        s5_b_re=s5_b_re, s5_b_im=s5_b_im, s5_c_re=s5_c_re, s5_c_im=s5_c_im, s5_d=s5_d,
        s5_w_glu=s5_w_glu, s5_b_glu=s5_b_glu, w_out_0=w_out_0, norm_mix_1=norm_mix_1, w_in_1=w_in_1,
        rnn_conv_w=rnn_conv_w, rnn_conv_b=rnn_conv_b, rnn_w_a=rnn_w_a, rnn_b_a=rnn_b_a,
        rnn_w_x=rnn_w_x, rnn_b_x=rnn_b_x, rnn_lam=rnn_lam, w_out_1=w_out_1, norm_ffn=norm_ffn,
        ffn_w_up=ffn_w_up, ffn_conv_w=ffn_conv_w, ffn_conv_b=ffn_conv_b, ffn_w_down=ffn_w_down,
        norm_final=norm_final))

    bp = x_prompt.shape[0]
    bs = x_sample.shape[0]
    tt = _tile_steps()
    mrows = bp * N_META
    assert mrows == bs

    x_side = jnp.concatenate([jnp.repeat(meta_tokens.astype(F32), bp, axis=0),
                              x_sample.reshape(bs, D_MODEL)], axis=0)
    s5_re_s = state_s5_re.reshape(bs, S5_N)
    s5_im_s = state_s5_im.reshape(bs, S5_N)

    gla_in = lambda z: (z[0], z[1], z[2], z[4], z[3])
    gn = w["gla_norm"]

    side = _l0_in(x_side, w, mrows + bs, 1)
    og_m, gla_m = _gla_chunk(gla_in(side), gn, jnp.zeros((GLA_HEADS, GLA_DV, bp * GLA_DK), F32), bp, N_META,
                             row_blocks=(2, 0))
    og_s, gla_s = _gla_step(gla_in(side), gn, state_gla, mrows)
    x_m, re_m, im_m, x_s, re_s, im_s = _l0_out_side(side[5], og_m, og_s, x_side, s5_re_s, s5_im_s, w, bp, N_META)
    x_m, fc0_m, x_s, fc0_s = _ffn_side(x_m, x_s, cache_ffn_conv, w, 0, bp, N_META, False)
    x_m, h_m, rc_m, x_s, h_s, rc_s = _l1_side(x_m, x_s, state_rglru, cache_rglru_conv, w, bp, N_META)
    _, fc1_m, ys, fc1_s = _ffn_side(x_m, x_s, cache_ffn_conv, w, 1, bp, N_META, True)

    main = _l0_in(x_prompt, w, bp, tt["l0_in"])
    og, gla_p = _gla_chunk(gla_in(main), gn, gla_m, bp, tt["gla"], chunks=GLA_CHUNKS_PER_STEP)
    x, re_p, im_p = _l0_out(main[5], og, x_prompt, re_m, im_m, w, tt["l0_out"])
    x, fc0_p = _ffn(x, fc0_m, w, 0, bp, tt["ffn"], False)
    x, h_p, rc_p = _l1(x, h_m, rc_m, w, bp, tt["l1"])
    yp, fc1_p = _ffn(x, fc1_m, w, 1, bp, tt["ffn"], True, batch_major_out=True)

    grp = lambda z, nb: z.reshape(nb, S5_GROUPS, S5_P)
    return (yp, ys.reshape(bs, 1, D_MODEL), _gla_state_from_stacked(gla_p), gla_s,
            grp(re_p, bp), grp(re_s, bs), grp(im_p, bp), grp(im_s, bs), h_p, h_s,
            _batch_major(rc_p, bp), rc_s,
            jnp.stack([_batch_major(fc0_p, bp), _batch_major(fc1_p, bp)]), jnp.stack([fc0_s, fc1_s]))
```

```python
import functools

import jax
import jax.numpy as jnp
from jax import lax
from jax.experimental import pallas as pl
from jax.experimental.pallas import tpu as pltpu

F32 = jnp.float32
BF16 = jnp.bfloat16

D_MODEL = 1024
N_META = 16
EPS = 1e-6
F32_TINY = 1.1754944e-38
GLA_HEADS = 4
GLA_DK = 64
GLA_DV = 128
GLA_RANK = 16
GLA_TAU = 16.0
GLA_K = GLA_HEADS * GLA_DK
GLA_V = GLA_HEADS * GLA_DV
S5_GROUPS = 32
S5_H = 16
S5_P = 64
S5_W = S5_GROUPS * S5_H
S5_N = S5_GROUPS * S5_P
RNN_W = 1536
RNN_BLOCKS = 16
RNN_BW = RNN_W // RNN_BLOCKS
RNN_C = 8.0
RNN_CONV = 4
D_FF = 2816
FFN_CONV = 3

LANES = 128
FF_CHUNK = 256
L1_SPLIT = 2
GLA_CHUNKS_PER_STEP = 2
S5_PACK = 8
GATE_WIN = 768
GATE_K = 512
GATE_N = 256
VMEM_LIMIT = 56 * 1024 * 1024


def _rms(x, g):
    return x * lax.rsqrt(jnp.mean(x * x, axis=-1, keepdims=True) + EPS) * g


def _sigmoid(x):
    return 0.5 * jnp.tanh(0.5 * x) + 0.5


def _sqrt_nonneg(t):
    return t * lax.rsqrt(jnp.maximum(t, F32_TINY))


def _mm(a, w):
    return jnp.dot(a.astype(BF16), w, preferred_element_type=F32)


def _const_spec(shape, index=None):
    idx = tuple(index) if index is not None else (0,) * len(shape)
    return pl.BlockSpec(shape, lambda i: idx, pipeline_mode=pl.Buffered(1))


def _row_spec(rows, cols):
    return pl.BlockSpec((rows, cols), lambda i: (i, 0))


def _seq_spec(nb, tt, cols):
    return pl.BlockSpec((nb, tt, cols), lambda i: (0, i, 0))


def _tm_scratch(rows, cols):
    return pltpu.VMEM((cols // LANES, rows, LANES), F32)


def _load_time_major(x_ref, tm_sc):
    if tm_sc is None:
        return x_ref[...]
    nb, tt, cols = x_ref.shape
    for b in range(nb):
        for j in range(cols // LANES):
            tm_sc[j, pl.ds(b, tt, stride=nb), :] = x_ref[b, :, j * LANES:(j + 1) * LANES]
    return jnp.concatenate([tm_sc[j] for j in range(cols // LANES)], axis=-1)


def _store_time_major(o_ref, val, tm_sc):
    if tm_sc is None:
        o_ref[...] = val
        return
    nb, tt, cols = o_ref.shape
    for j in range(cols // LANES):
        tm_sc[j] = val[:, j * LANES:(j + 1) * LANES]
    for b in range(nb):
        for j in range(cols // LANES):
            o_ref[b, :, j * LANES:(j + 1) * LANES] = tm_sc[j, pl.ds(b, tt, stride=nb), :]


def _cache_shape(nb, tt, taps, width):
    return (nb, taps, width) if tt == 1 else (taps * nb, width)


def _params(sem="arbitrary"):
    return pltpu.CompilerParams(dimension_semantics=(sem,), vmem_limit_bytes=VMEM_LIMIT)


def _s5_prep_kernel(lr_ref, li_ref, ldt_ref, brt_ref, bit_ref, are_ref, aim_ref, bbre_ref, bbim_ref):
    lr = lr_ref[...]
    li = li_ref[...]
    dt = jnp.exp(ldt_ref[...])
    mag = jnp.exp(lr * dt)
    ab_re = mag * jnp.cos(li * dt)
    ab_im = mag * jnp.sin(li * dt)
    den = lr * lr + li * li
    nr = ab_re - 1.0
    ni = ab_im
    f_re = (nr * lr + ni * li) / den
    f_im = (ni * lr - nr * li) / den
    are_ref[...] = ab_re
    aim_ref[...] = ab_im
    brt = brt_ref[...]
    bit = bit_ref[...]
    bbre_ref[...] = f_re[:, None, :] * brt - f_im[:, None, :] * bit
    bbim_ref[...] = f_re[:, None, :] * bit + f_im[:, None, :] * brt


def _s5_prep(lam_re, lam_im, log_dt, b_re, b_im):
    g, p, h = b_re.shape
    brt = jnp.transpose(b_re, (0, 2, 1))
    bit = jnp.transpose(b_im, (0, 2, 1))
    return pl.pallas_call(
        _s5_prep_kernel,
        out_shape=(jax.ShapeDtypeStruct((g, p), F32), jax.ShapeDtypeStruct((g, p), F32),
                   jax.ShapeDtypeStruct((g, h, p), F32), jax.ShapeDtypeStruct((g, h, p), F32)),
        name="s5_prep",
    )(lam_re, lam_im, log_dt.reshape(g, 1), brt, bit)


def _is_first():
    return pl.program_id(0) == 0


def _side_spec(rows, cols, block=0):
    return pl.BlockSpec((rows, cols), lambda i: (block, 0))


def _l0_in_body(x, wts, outs):
    g_ref, wq_ref, wk_ref, wv_ref, wg_ref, wu_ref, wlr_ref, wal_ref, bal_ref = wts
    q_ref, k_ref, v_ref, gs_ref, la_ref, u_ref = outs
    xn = _rms(x, g_ref[...]).astype(BF16)
    q_ref[...] = _mm(xn, wq_ref[...]) * (GLA_DK ** -0.5)
    k_ref[...] = _mm(xn, wk_ref[...])
    v_ref[...] = _mm(xn, wv_ref[...])
    g = _mm(xn, wg_ref[...])
    gs_ref[...] = g * _sigmoid(g)
    u_ref[...] = _mm(xn, wu_ref[...])
    lr = _mm(xn, wlr_ref[...])
    pre = _mm(lr, wal_ref[...]) + bal_ref[...]
    la_ref[...] = jax.nn.log_sigmoid(pre) * (1.0 / GLA_TAU)


def _l0_in_kernel(*refs):
    x_ref, wts, outs, tm_sc = refs[0], refs[1:10], refs[10:16], refs[16:]
    _l0_in_body(_load_time_major(x_ref, tm_sc[0] if tm_sc else None), wts, outs)


L0_IN_COLS = (GLA_K, GLA_K, GLA_V, GLA_V, GLA_K, S5_W)


def _l0_in_weight_specs():
    return [_const_spec((1, D_MODEL)),
            _const_spec((D_MODEL, GLA_K), (0, 0)), _const_spec((D_MODEL, GLA_K), (0, 1)),
            _const_spec((D_MODEL, GLA_V), (0, 1)), _const_spec((D_MODEL, GLA_V), (0, 2)),
            _const_spec((D_MODEL, S5_W)), _const_spec((D_MODEL, LANES)),
            _const_spec((LANES, GLA_K)), _const_spec((1, GLA_K))]


def _l0_in_weights(w):
    return (w["norm_mix_0"], w["w_in_0"], w["w_in_0"], w["w_in_0"], w["w_in_0"], w["w_u"], w["w_lr"],
            w["w_alpha"], w["b_alpha"])


def _l0_in(x, w, nb, tt):
    rows = nb * tt
    batch_major = x.ndim == 3
    n = x.shape[0] * x.shape[1] if batch_major else x.shape[0]
    return pl.pallas_call(
        _l0_in_kernel,
        grid=(n // rows,),
        in_specs=[_seq_spec(nb, tt, D_MODEL) if batch_major else _row_spec(rows, D_MODEL)]
                 + _l0_in_weight_specs(),
        out_specs=[_row_spec(rows, c) for c in L0_IN_COLS],
        out_shape=[jax.ShapeDtypeStruct((n, c), F32) for c in L0_IN_COLS],
        scratch_shapes=[_tm_scratch(rows, D_MODEL)] if batch_major else [],
        compiler_params=_params("parallel"),
        name="l0_in",
    )(x, *_l0_in_weights(w))


def _gla_chunk_body(ins, gn_ref, o_ref, st_ref, b_sc, nb, c):
    q_ref, k_ref, v_ref, la_ref, gs_ref = ins
    rows = nb * c
    seq_mask = nb - 1

    def cum_body(t, run):
        rws = pl.ds(pl.multiple_of(t * nb, nb), nb)
        run = run + la_ref[rws, :]
        b_sc[rws, :] = run
        return run

    bl = lax.fori_loop(0, c, cum_body, jnp.zeros((nb, GLA_K), F32))
    b = b_sc[...]
    k = k_ref[...]
    qt = q_ref[...] * jnp.exp(b)
    kt = k * jnp.exp(-b)
    kh = k * jnp.exp(jnp.concatenate([bl] * c, axis=0) - b)
    gam = jnp.exp(bl)

    ri = lax.broadcasted_iota(jnp.int32, (rows, rows), 0)
    ci = lax.broadcasted_iota(jnp.int32, (rows, rows), 1)
    pair_ok = (ri >= ci) & (((ri - ci) & seq_mask) == 0)
    xw = nb * GLA_DK
    own_blk = (lax.broadcasted_iota(jnp.int32, (rows, xw), 1) // GLA_DK
               == (lax.broadcasted_iota(jnp.int32, (rows, xw), 0) & seq_mask))
    own_blk_seq = (lax.broadcasted_iota(jnp.int32, (nb, xw), 1) // GLA_DK
                   == lax.broadcasted_iota(jnp.int32, (nb, xw), 0))
    reps = xw // LANES

    def head_dup(z, h):
        blk = z[:, (h // 2) * LANES:(h // 2 + 1) * LANES]
        rolled = pltpu.roll(blk, GLA_DK, axis=1)
        low = lax.broadcasted_iota(jnp.int32, blk.shape, 1) < GLA_DK
        return jnp.where(low, blk, rolled) if h % 2 == 0 else jnp.where(low, rolled, blk)

    def expand(zd, own):
        return jnp.where(own, jnp.concatenate([zd] * reps, axis=1), 0.0)

    for h in range(GLA_HEADS):
        vs = slice(h * GLA_DV, (h + 1) * GLA_DV)
        qd, ktd, khd = head_dup(qt, h), head_dup(kt, h), head_dup(kh, h)
        vb = v_ref[:, vs].astype(BF16)
        st = st_ref[h]
        att = lax.dot_general(qd[:, :GLA_DK].astype(BF16), ktd[:, :GLA_DK].astype(BF16),
                              (((1,), (1,)), ((), ())), preferred_element_type=F32)
        att = jnp.where(pair_ok, att, 0.0).astype(BF16)
        o = jnp.dot(att, vb, preferred_element_type=F32) + lax.dot_general(
            expand(qd, own_blk).astype(BF16), st.astype(BF16), (((1,), (1,)), ((), ())),
            preferred_element_type=F32)
        upd = lax.dot_general(vb, expand(khd, own_blk).astype(BF16), (((0,), (0,)), ((), ())),
                              preferred_element_type=F32)
        gam_row = jnp.sum(expand(head_dup(gam, h), own_blk_seq), axis=0, keepdims=True)
        st_ref[h] = st * gam_row + upd
        o_ref[:, vs] = _rms(o, gn_ref[:, vs]) * gs_ref[:, vs]


def _gla_chunk_kernel(*refs, nb, c, chunks):
    ins, gn_ref, st0_ref, o_ref, st_ref, b_sc = refs[0:5], refs[5], refs[6], refs[7], refs[8], refs[9]

    @pl.when(_is_first())
    def _():
        st_ref[...] = st0_ref[...]

    for j in range(chunks):
        rws = pl.ds(j * nb * c, nb * c)
        _gla_chunk_body([r.at[rws] for r in ins], gn_ref, o_ref.at[rws], st_ref, b_sc, nb, c)


def _gla_chunk(ins, gn, st0, nb, c, chunks=1, row_blocks=None):
    assert nb & (nb - 1) == 0 and (nb * GLA_DK) % LANES == 0
    rows = nb * c * chunks
    n = ins[0].shape[0] // row_blocks[0] if row_blocks else ins[0].shape[0]
    first = (row_blocks[1] * n) // rows if row_blocks else 0
    sshape = (GLA_HEADS, GLA_DV, nb * GLA_DK)
    cols = (GLA_K, GLA_K, GLA_V, GLA_K, GLA_V)
    kern = functools.partial(_gla_chunk_kernel, nb=nb, c=c, chunks=chunks)
    return pl.pallas_call(
        kern,
        grid=(n // rows,),
        in_specs=[pl.BlockSpec((rows, w_), lambda i: (first + i, 0)) for w_ in cols]
                 + [_const_spec((1, GLA_V)), _const_spec(sshape)],
        out_specs=[_row_spec(rows, GLA_V), pl.BlockSpec(sshape, lambda i: (0, 0, 0))],
        out_shape=[jax.ShapeDtypeStruct((n, GLA_V), F32), jax.ShapeDtypeStruct(sshape, F32)],
        scratch_shapes=[pltpu.VMEM((nb * c, GLA_K), F32)],
        compiler_params=_params(),
        name="gla_chunk",
    )(*ins, gn, st0)


def _gla_state_from_stacked(st):
    nb = st.shape[2] // GLA_DK
    return jnp.transpose(st.reshape(GLA_HEADS, GLA_DV, nb, GLA_DK), (2, 0, 3, 1))


def _gla_step_kernel(q_ref, k_ref, la_ref, v_ref, gs_ref, gn_ref, s0_ref, o_ref, s_ref):
    qT = q_ref[...].T
    kT = k_ref[...].T
    aT = jnp.exp(la_ref[...]).T
    o_rows = []
    for b in range(q_ref.shape[0]):
        s_new = aT[:, b:b + 1] * s0_ref[b] + kT[:, b:b + 1] * v_ref[b:b + 1, :]
        s_ref[b] = s_new
        o_rows.append(jnp.sum(qT[:, b:b + 1] * s_new, axis=0, keepdims=True))
    o = jnp.concatenate(o_rows, axis=0)
    o_ref[...] = _rms(o, gn_ref[...]) * gs_ref[...]


def _gla_step(side, gn, s0, row0):
    nb = s0.shape[0]
    blk = row0 // nb
    q, k, v, la, gs = side
    heads = lambda z: jnp.transpose(z[row0:row0 + nb].reshape(nb, GLA_HEADS, GLA_DK), (1, 0, 2))
    hspec = pl.BlockSpec((None, nb, GLA_DK), lambda h: (h, 0, 0))
    vspec = pl.BlockSpec((nb, GLA_DV), lambda h: (blk, h))
    ospec = pl.BlockSpec((nb, GLA_DV), lambda h: (0, h))
    sspec = pl.BlockSpec((nb, None, GLA_DK, GLA_DV), lambda h: (0, h, 0, 0))
    return pl.pallas_call(
        _gla_step_kernel,
        grid=(GLA_HEADS,),
        in_specs=[hspec, hspec, hspec, vspec, vspec, pl.BlockSpec((1, GLA_DV), lambda h: (0, h)), sspec],
        out_specs=[ospec, sspec],
        out_shape=[jax.ShapeDtypeStruct((nb, GLA_V), F32),
                   jax.ShapeDtypeStruct((nb, GLA_HEADS, GLA_DK, GLA_DV), F32)],
        compiler_params=_params("parallel"),
        name="gla_step",
    )(heads(q), heads(k), heads(la), v, gs, gn, s0)


def _l0_out_body(u_ref, og_ref, x, wts, xo_ref, xre_ref, xim_ref, sr_sc, si_sc, nb, tt):
    (bbre_ref, bbim_ref, are_ref, aim_ref, cre_ref, cim_ref, d_ref, wglu_ref, bglu_ref, woa_ref,
     wob_ref) = wts
    u = u_ref[...]
    ub = u.astype(BF16)
    kin = S5_PACK * S5_H
    kst = S5_PACK * S5_P
    npack = S5_GROUPS // S5_PACK
    ys = []
    for j in range(npack):
        cs = slice(j * kst, (j + 1) * kst)
        uj = ub[:, j * kin:(j + 1) * kin]
        bur = jnp.dot(uj, bbre_ref[j], preferred_element_type=F32)
        bui = jnp.dot(uj, bbim_ref[j], preferred_element_type=F32)
        ar = jnp.broadcast_to(are_ref[:, cs], (nb, kst))
        ai = jnp.broadcast_to(aim_ref[:, cs], (nb, kst))
        xr = xre_ref[:, cs]
        xi = xim_ref[:, cs]
        for t in range(tt):
            rows = slice(t * nb, (t + 1) * nb)
            xr, xi = ar * xr - ai * xi + bur[rows], ar * xi + ai * xr + bui[rows]
            sr_sc[rows, cs] = xr
            si_sc[rows, cs] = xi
        xre_ref[:, cs] = xr
        xim_ref[:, cs] = xi
        ys.append(jnp.dot(sr_sc[:, cs].astype(BF16), cre_ref[j], preferred_element_type=F32)
                  - jnp.dot(si_sc[:, cs].astype(BF16), cim_ref[j], preferred_element_type=F32))
    y = jnp.concatenate(ys, axis=-1) + d_ref[...] * u
    y = jax.nn.gelu(y)
    y = y * _sigmoid(_mm(y, wglu_ref[...]) + bglu_ref[...])
    xo_ref[...] = x + _mm(og_ref[...], woa_ref[...]) + _mm(y, wob_ref[...])


L0_OUT_WEIGHTS = 11


def _l0_out_weight_specs():
    npack = S5_GROUPS // S5_PACK
    kin, kst = S5_PACK * S5_H, S5_PACK * S5_P
    return [_const_spec((npack, kin, kst)), _const_spec((npack, kin, kst)),
            _const_spec((1, S5_N)), _const_spec((1, S5_N)),
            _const_spec((npack, kst, kin)), _const_spec((npack, kst, kin)),
            _const_spec((1, S5_W)), _const_spec((S5_W, S5_W)), _const_spec((1, S5_W)),
            _const_spec((GLA_V, D_MODEL), (0, 0)), _const_spec((S5_W, D_MODEL), (1, 0))]


def _l0_out_weights(w):
    return (w["s5_bbre"], w["s5_bbim"], w["s5_are"], w["s5_aim"], w["s5_cre"], w["s5_cim"],
            w["s5_d"], w["s5_w_glu"], w["s5_b_glu"], w["w_out_0"], w["w_out_0"])


def _l0_out_kernel(*refs, nb, tt):
    u_ref, og_ref, x_ref, xr0_ref, xi0_ref = refs[0:5]
    wts = refs[5:5 + L0_OUT_WEIGHTS]
    xo_ref, xre_ref, xim_ref, sr_sc, si_sc, tm_sc = refs[5 + L0_OUT_WEIGHTS:]

    @pl.when(_is_first())
    def _():
        xre_ref[...] = xr0_ref[...]
        xim_ref[...] = xi0_ref[...]

    _l0_out_body(u_ref, og_ref, _load_time_major(x_ref, tm_sc), wts, xo_ref, xre_ref, xim_ref,
                 sr_sc, si_sc, nb, tt)


def _l0_out(u, og, x, xr0, xi0, w, tt):
    nb = x.shape[0]
    n = u.shape[0]
    rows = nb * tt
    kern = functools.partial(_l0_out_kernel, nb=nb, tt=tt)
    st_spec = pl.BlockSpec((nb, S5_N), lambda i: (0, 0))
    f32 = lambda *shape: jax.ShapeDtypeStruct(shape, F32)
    return pl.pallas_call(
        kern,
        grid=(n // rows,),
        in_specs=[_row_spec(rows, S5_W), _row_spec(rows, GLA_V), _seq_spec(nb, tt, D_MODEL),
                  _const_spec((nb, S5_N)), _const_spec((nb, S5_N))] + _l0_out_weight_specs(),
        out_specs=[_row_spec(rows, D_MODEL), st_spec, st_spec],
        out_shape=[f32(n, D_MODEL), f32(nb, S5_N), f32(nb, S5_N)],
        scratch_shapes=[pltpu.VMEM((rows, S5_N), F32)] * 2 + [_tm_scratch(rows, D_MODEL)],
        compiler_params=_params(),
        name="l0_out",
    )(u, og, x, xr0, xi0, *_l0_out_weights(w))


def _l0_out_side_kernel(*refs, nb, tt_meta, nb_s):
    um_ref, ogm_ref, xm_ref, us_ref, ogs_ref, xs_ref, xr0s_ref, xi0s_ref = refs[0:8]
    wts = refs[8:8 + L0_OUT_WEIGHTS]
    xom_ref, xre_ref, xim_ref, xos_ref, xres_ref, xims_ref, sr_sc, si_sc = refs[8 + L0_OUT_WEIGHTS:]
    xre_ref[...] = jnp.zeros(xre_ref.shape, F32)
    xim_ref[...] = jnp.zeros(xim_ref.shape, F32)
    _l0_out_body(um_ref, ogm_ref, xm_ref[...], wts, xom_ref, xre_ref, xim_ref, sr_sc, si_sc, nb, tt_meta)
    xres_ref[...] = xr0s_ref[...]
    xims_ref[...] = xi0s_ref[...]
    _l0_out_body(us_ref, ogs_ref, xs_ref[...], wts, xos_ref, xres_ref, xims_ref, sr_sc, si_sc, nb_s, 1)


def _l0_out_side(u_side, og_meta, og_samp, x_side, xr0_s, xi0_s, w, nb, tt_meta):
    mrows, nb_s = nb * tt_meta, xr0_s.shape[0]
    assert mrows == nb_s
    kern = functools.partial(_l0_out_side_kernel, nb=nb, tt_meta=tt_meta, nb_s=nb_s)
    f32 = lambda *shape: jax.ShapeDtypeStruct(shape, F32)
    full = lambda r, c: pl.BlockSpec((r, c), lambda i: (0, 0))
    return pl.pallas_call(
        kern,
        grid=(1,),
        in_specs=[_side_spec(mrows, S5_W, 0), full(mrows, GLA_V), _side_spec(mrows, D_MODEL, 0),
                  _side_spec(nb_s, S5_W, 1), full(nb_s, GLA_V), _side_spec(nb_s, D_MODEL, 1),
                  full(nb_s, S5_N), full(nb_s, S5_N)] + _l0_out_weight_specs(),
        out_specs=[full(mrows, D_MODEL), full(nb, S5_N), full(nb, S5_N), full(nb_s, D_MODEL),
                   full(nb_s, S5_N), full(nb_s, S5_N)],
        out_shape=[f32(mrows, D_MODEL), f32(nb, S5_N), f32(nb, S5_N), f32(nb_s, D_MODEL),
                   f32(nb_s, S5_N), f32(nb_s, S5_N)],
        scratch_shapes=[pltpu.VMEM((mrows, S5_N), F32)] * 2,
        compiler_params=_params(),
        name="l0_out_side",
    )(u_side, og_meta, x_side, u_side, og_samp, x_side, xr0_s, xi0_s, *_l0_out_weights(w))


L0_IN_WEIGHTS = 9


def _l0_kernel(*refs, nb, tt, c):
    x_ref, st0_ref, xr0_ref, xi0_ref = refs[0:4]
    n_in = 4 + L0_IN_WEIGHTS
    in_wts, gn_ref, out_wts = refs[4:n_in], refs[n_in], refs[n_in + 1:n_in + 1 + L0_OUT_WEIGHTS]
    rest = refs[n_in + 1 + L0_OUT_WEIGHTS:]
    xo_ref, st_ref, xre_ref, xim_ref = rest[0:4]
    proj_sc, og_sc, b_sc, sr_sc, si_sc, tm_sc = rest[4:10], rest[10], rest[11], rest[12], rest[13], rest[14]

    @pl.when(_is_first())
    def _():
        st_ref[...] = st0_ref[...]
        xre_ref[...] = xr0_ref[...]
        xim_ref[...] = xi0_ref[...]

    x = _load_time_major(x_ref, tm_sc)
    _l0_in_body(x, in_wts, proj_sc)
    q_sc, k_sc, v_sc, gs_sc, la_sc, u_sc = proj_sc
    for j in range(tt // c):
        rws = pl.ds(j * nb * c, nb * c)
        _gla_chunk_body([r.at[rws] for r in (q_sc, k_sc, v_sc, la_sc, gs_sc)], gn_ref, og_sc.at[rws], st_ref,
                        b_sc, nb, c)
    _l0_out_body(u_sc, og_sc, x, out_wts, xo_ref, xre_ref, xim_ref, sr_sc, si_sc, nb, tt)


def _l0(x, st0, xr0, xi0, w, tt, c):
    nb, nt, _ = x.shape
    n, rows = nb * nt, nb * tt
    assert nb & (nb - 1) == 0 and (nb * GLA_DK) % LANES == 0 and tt % c == 0
    sshape = (GLA_HEADS, GLA_DV, nb * GLA_DK)
    kern = functools.partial(_l0_kernel, nb=nb, tt=tt, c=c)
    st_spec = pl.BlockSpec((nb, S5_N), lambda i: (0, 0))
    f32 = lambda *shape: jax.ShapeDtypeStruct(shape, F32)
    return pl.pallas_call(
        kern,
        grid=(n // rows,),
        in_specs=[_seq_spec(nb, tt, D_MODEL), _const_spec(sshape), _const_spec((nb, S5_N)),
                  _const_spec((nb, S5_N))] + _l0_in_weight_specs() + [_const_spec((1, GLA_V))]
                 + _l0_out_weight_specs(),
        out_specs=[_row_spec(rows, D_MODEL), pl.BlockSpec(sshape, lambda i: (0, 0, 0)), st_spec, st_spec],
        out_shape=[f32(n, D_MODEL), f32(*sshape), f32(nb, S5_N), f32(nb, S5_N)],
        scratch_shapes=[pltpu.VMEM((rows, cols), F32) for cols in L0_IN_COLS]
                       + [pltpu.VMEM((rows, GLA_V), F32), pltpu.VMEM((nb * c, GLA_K), F32),
                          pltpu.VMEM((rows, S5_N), F32), pltpu.VMEM((rows, S5_N), F32),
                          _tm_scratch(rows, D_MODEL)],
        compiler_params=_params(),
        name="l0_mixer",
    )(x, st0, xr0, xi0, *_l0_in_weights(w), w["gla_norm"], *_l0_out_weights(w))


def _ffn_body(x, wts, c0_ref, c_ref, hm_sc, nb, tt, final):
    g_ref, wg_ref, wv_ref, cw_ref, cb_ref, wd_ref, gf_ref = wts
    rows = nb * tt
    xn = _rms(x, g_ref[...]).astype(BF16)
    for ci in range(D_FF // FF_CHUNK):
        cs = slice(ci * FF_CHUNK, (ci + 1) * FF_CHUNK)
        gate = jnp.dot(xn, wg_ref[:, cs], preferred_element_type=F32)
        val = jnp.dot(xn, wv_ref[:, cs], preferred_element_type=F32)
        if tt == 1:
            taps = [c0_ref[:, j, cs] for j in range(FFN_CONV - 1)] + [gate]
            for j in range(FFN_CONV - 1):
                c_ref[:, j, cs] = taps[j + 1]
        else:
            ext = jnp.concatenate([c_ref[:, cs], gate], axis=0)
            taps = [ext[j * nb:j * nb + rows] for j in range(FFN_CONV)]
            c_ref[:, cs] = ext[tt * nb:(tt + FFN_CONV - 1) * nb]
        y = cb_ref[:, cs] + taps[0] * cw_ref[0:1, cs]
        for j in range(1, FFN_CONV):
            y = y + taps[j] * cw_ref[j:j + 1, cs]
        hm_sc[:, cs] = (jax.nn.gelu(y) * val).astype(BF16)
    out = x + jnp.dot(hm_sc[...], wd_ref[...], preferred_element_type=F32)
    return _rms(out, gf_ref[...]) if final else out


FFN_WEIGHTS = 7


def _ffn_weight_specs(layer):
    return [_const_spec((None, 1, D_MODEL), (layer, 0, 0)),
            _const_spec((None, D_MODEL, D_FF), (layer, 0, 0)),
            _const_spec((None, D_MODEL, D_FF), (layer, 0, 1)),
            _const_spec((None, FFN_CONV, D_FF), (layer, 0, 0)),
            _const_spec((None, 1, D_FF), (layer, 0, 0)),
            _const_spec((None, D_FF, D_MODEL), (layer, 0, 0)), _const_spec((1, D_MODEL))]


def _ffn_weights(w):
    return (w["norm_ffn"], w["ffn_w_up"], w["ffn_w_up"], w["ffn_conv_w"], w["ffn_conv_b"], w["ffn_w_down"],
            w["norm_final"])


def _ffn_kernel(*refs, nb, tt, final, batch_major_out):
    x_ref, c0_ref = refs[0:2]
    wts = refs[2:2 + FFN_WEIGHTS]
    xo_ref, c_ref, hm_sc = refs[2 + FFN_WEIGHTS:5 + FFN_WEIGHTS]
    tm_sc = refs[5 + FFN_WEIGHTS] if batch_major_out else None

    @pl.when(_is_first())
    def _():
        c_ref[...] = c0_ref[...]

    _store_time_major(xo_ref, _ffn_body(x_ref[...], wts, None, c_ref, hm_sc, nb, tt, final), tm_sc)


def _ffn(x, c0, w, layer, nb, tt, final, batch_major_out=False):
    n = x.shape[0]
    rows = nb * tt
    kern = functools.partial(_ffn_kernel, nb=nb, tt=tt, final=final, batch_major_out=batch_major_out)
    if batch_major_out:
        o_spec, o_shape = _seq_spec(nb, tt, D_MODEL), (nb, n // nb, D_MODEL)
    else:
        o_spec, o_shape = _row_spec(rows, D_MODEL), (n, D_MODEL)
    cshape = _cache_shape(nb, tt, FFN_CONV - 1, D_FF)
    f32 = lambda *shape: jax.ShapeDtypeStruct(shape, F32)
    return pl.pallas_call(
        kern,
        grid=(n // rows,),
        in_specs=[_row_spec(rows, D_MODEL), _const_spec(cshape)] + _ffn_weight_specs(layer),
        out_specs=[o_spec, pl.BlockSpec(cshape, lambda i: (0, 0))],
        out_shape=[f32(*o_shape), f32(*cshape)],
        scratch_shapes=[pltpu.VMEM((rows, D_FF), BF16)]
                       + ([_tm_scratch(rows, D_MODEL)] if batch_major_out else []),
        compiler_params=_params(),
        name="ffn%d" % layer,
    )(x, c0, *_ffn_weights(w))


def _ffn_side_kernel(*refs, nb, tt_meta, nb_s, final):
    xm_ref, xs_ref, c0s_ref = refs[0:3]
    wts = refs[3:3 + FFN_WEIGHTS]
    xom_ref, c_ref, xos_ref, cs_ref, hm_sc = refs[3 + FFN_WEIGHTS:]
    c_ref[...] = jnp.zeros(c_ref.shape, F32)
    xom_ref[...] = _ffn_body(xm_ref[...], wts, None, c_ref, hm_sc, nb, tt_meta, final)
    xos_ref[...] = _ffn_body(xs_ref[...], wts, c0s_ref, cs_ref, hm_sc, nb_s, 1, final)


def _ffn_side(x_meta, x_samp, c0_s, w, layer, nb, tt_meta, final):
    mrows, nb_s = nb * tt_meta, x_samp.shape[0]
    assert mrows == nb_s
    kern = functools.partial(_ffn_side_kernel, nb=nb, tt_meta=tt_meta, nb_s=nb_s, final=final)
    cshape = _cache_shape(nb, tt_meta, FFN_CONV - 1, D_FF)
    cs_shape = _cache_shape(nb_s, 1, FFN_CONV - 1, D_FF)
    f32 = lambda *shape: jax.ShapeDtypeStruct(shape, F32)
    full = lambda shape: pl.BlockSpec(shape, lambda i: (0,) * len(shape))
    return pl.pallas_call(
        kern,
        grid=(1,),
        in_specs=[full((mrows, D_MODEL)), full((nb_s, D_MODEL)),
                  pl.BlockSpec((None,) + cs_shape, lambda i: (layer, 0, 0, 0))] + _ffn_weight_specs(layer),
        out_specs=[full((mrows, D_MODEL)), full(cshape), full((nb_s, D_MODEL)), full(cs_shape)],
        out_shape=[f32(mrows, D_MODEL), f32(*cshape), f32(nb_s, D_MODEL), f32(*cs_shape)],
        scratch_shapes=[pltpu.VMEM((mrows, D_FF), BF16)],
        compiler_params=_params(),
        name="ffn%d_side" % layer,
    )(x_meta, x_samp, c0_s, *_ffn_weights(w))


def _l1_body(x_ref, wts, xo_ref, h_ref, c0_ref, c_ref, a_sc, b_sc, nb, tt):
    g_ref, wgt_ref, wxr_ref, cw_ref, cb_ref, wa_ref, ba_ref, wx_ref, bx_ref, lam_ref, wo_ref = wts
    sp = jax.nn.softplus(-lam_ref[...])
    nsplit = L1_SPLIT if tt % L1_SPLIT == 0 else 1
    th = tt // nsplit
    rows = nb * th
    carry = c_ref[...] if tt > 1 else None
    h = h_ref[...]
    for part in range(nsplit):
        prow = slice(part * rows, (part + 1) * rows)
        x = x_ref[prow, :]
        xn = _rms(x, g_ref[...]).astype(BF16)
        xr = jnp.dot(xn, wxr_ref[...], preferred_element_type=F32)
        if tt == 1:
            taps = [c0_ref[:, j, :] for j in range(RNN_CONV - 1)] + [xr]
            for j in range(RNN_CONV - 1):
                c_ref[:, j, :] = taps[j + 1]
        else:
            ext = jnp.concatenate([carry, xr], axis=0)
            taps = [ext[j * nb:j * nb + rows] for j in range(RNN_CONV)]
            carry = ext[th * nb:(th + RNN_CONV - 1) * nb]
        xc = cb_ref[...] + taps[0] * cw_ref[0:1, :]
        for j in range(1, RNN_CONV):
            xc = xc + taps[j] * cw_ref[j:j + 1, :]

        xcb = xc.astype(BF16)
        rs, gs = [], []
        for wi in range(RNN_W // GATE_WIN):
            for ni in range(GATE_WIN // GATE_N):
                k0 = wi * GATE_WIN + ni * LANES
                lhs = xcb[:, k0:k0 + GATE_K]
                rs.append(jnp.dot(lhs, wa_ref[wi, ni], preferred_element_type=F32))
                gs.append(jnp.dot(lhs, wx_ref[wi, ni], preferred_element_type=F32))
        r = _sigmoid(jnp.concatenate(rs, axis=-1) + ba_ref[...])
        ig = _sigmoid(jnp.concatenate(gs, axis=-1) + bx_ref[...])
        log_a = (-RNN_C) * r * sp
        a = jnp.exp(log_a)
        a_sc[prow, :] = a
        b_sc[prow, :] = _sqrt_nonneg(jnp.tanh(-log_a) * (a * a + 1.0)) * (ig * xc)
        gg = jax.nn.gelu(jnp.dot(xn, wgt_ref[...], preferred_element_type=F32))

        for t in range(part * th, (part + 1) * th):
            rws = slice(t * nb, (t + 1) * nb)
            h = a_sc[rws, :] * h + b_sc[rws, :]
            b_sc[rws, :] = h
        xo_ref[prow, :] = x + _mm(b_sc[prow, :] * gg, wo_ref[...])
    if tt > 1:
        c_ref[...] = carry
    h_ref[...] = h


L1_WEIGHTS = 11


def _l1_weight_specs():
    gshape = (RNN_W // GATE_WIN, GATE_WIN // GATE_N, GATE_K, GATE_N)
    return [_const_spec((1, D_MODEL)),
            _const_spec((D_MODEL, RNN_W), (0, 0)), _const_spec((D_MODEL, RNN_W), (0, 1)),
            _const_spec((RNN_CONV, RNN_W)), _const_spec((1, RNN_W)),
            _const_spec(gshape), _const_spec((1, RNN_W)),
            _const_spec(gshape), _const_spec((1, RNN_W)),
            _const_spec((1, RNN_W)), _const_spec((RNN_W, D_MODEL))]


def _l1_weights(w):
    return (w["norm_mix_1"], w["w_in_1"], w["w_in_1"], w["rnn_conv_w"], w["rnn_conv_b"], w["rnn_wa"],
            w["rnn_b_a"], w["rnn_wx"], w["rnn_b_x"], w["rnn_lam"], w["w_out_1"])


def _l1_kernel(*refs, nb, tt):
    x_ref, h0_ref, c0_ref = refs[0:3]
    wts = refs[3:3 + L1_WEIGHTS]
    xo_ref, h_ref, c_ref, a_sc, b_sc = refs[3 + L1_WEIGHTS:]

    @pl.when(_is_first())
    def _():
        h_ref[...] = h0_ref[...]
        c_ref[...] = c0_ref[...]

    _l1_body(x_ref, wts, xo_ref, h_ref, None, c_ref, a_sc, b_sc, nb, tt)


def _l1(x, h0, c0, w, nb, tt):
    n = x.shape[0]
    rows = nb * tt
    kern = functools.partial(_l1_kernel, nb=nb, tt=tt)
    cshape = _cache_shape(nb, tt, RNN_CONV - 1, RNN_W)
    f32 = lambda *shape: jax.ShapeDtypeStruct(shape, F32)
    return pl.pallas_call(
        kern,
        grid=(n // rows,),
        in_specs=[_row_spec(rows, D_MODEL), _const_spec((nb, RNN_W)), _const_spec(cshape)] + _l1_weight_specs(),
        out_specs=[_row_spec(rows, D_MODEL), pl.BlockSpec((nb, RNN_W), lambda i: (0, 0)),
                   pl.BlockSpec(cshape, lambda i: (0, 0))],
        out_shape=[f32(n, D_MODEL), f32(nb, RNN_W), f32(*cshape)],
        scratch_shapes=[pltpu.VMEM((rows, RNN_W), F32)] * 2,
        compiler_params=_params(),
        name="l1_mixer",
    )(x, h0, c0, *_l1_weights(w))


def _l1_side_kernel(*refs, nb, tt_meta, nb_s):
    xm_ref, xs_ref, h0s_ref, c0s_ref = refs[0:4]
    wts = refs[4:4 + L1_WEIGHTS]
    xom_ref, h_ref, c_ref, xos_ref, hs_ref, cs_ref, a_sc, b_sc = refs[4 + L1_WEIGHTS:]
    h_ref[...] = jnp.zeros(h_ref.shape, F32)
    c_ref[...] = jnp.zeros(c_ref.shape, F32)
    _l1_body(xm_ref, wts, xom_ref, h_ref, None, c_ref, a_sc, b_sc, nb, tt_meta)
    hs_ref[...] = h0s_ref[...]
    _l1_body(xs_ref, wts, xos_ref, hs_ref, c0s_ref, cs_ref, a_sc, b_sc, nb_s, 1)


def _l1_side(x_meta, x_samp, h0_s, c0_s, w, nb, tt_meta):
    mrows, nb_s = nb * tt_meta, x_samp.shape[0]
    assert mrows == nb_s
    kern = functools.partial(_l1_side_kernel, nb=nb, tt_meta=tt_meta, nb_s=nb_s)
    cshape = _cache_shape(nb, tt_meta, RNN_CONV - 1, RNN_W)
    cs_shape = _cache_shape(nb_s, 1, RNN_CONV - 1, RNN_W)
    f32 = lambda *shape: jax.ShapeDtypeStruct(shape, F32)
    full = lambda shape: pl.BlockSpec(shape, lambda i: (0,) * len(shape))
    return pl.pallas_call(
        kern,
        grid=(1,),
        in_specs=[full((mrows, D_MODEL)), full((nb_s, D_MODEL)), full((nb_s, RNN_W)), full(cs_shape)]
                 + _l1_weight_specs(),
        out_specs=[full((mrows, D_MODEL)), full((nb, RNN_W)), full(cshape), full((nb_s, D_MODEL)),
                   full((nb_s, RNN_W)), full(cs_shape)],
        out_shape=[f32(mrows, D_MODEL), f32(nb, RNN_W), f32(*cshape), f32(nb_s, D_MODEL), f32(nb_s, RNN_W),
                   f32(*cs_shape)],
        scratch_shapes=[pltpu.VMEM((mrows, RNN_W), F32)] * 2,
        compiler_params=_params(),
        name="l1_side",
    )(x_meta, x_samp, h0_s, c0_s, *_l1_weights(w))


def _pack_gate_kernel(wa_ref, wx_ref, oa_ref, ox_ref):
    tiles_per_win = GATE_WIN // GATE_N
    for w_ref, o_ref in ((wa_ref, oa_ref), (wx_ref, ox_ref)):
        o_ref[...] = jnp.zeros(o_ref.shape, o_ref.dtype)
        for n in range(RNN_BLOCKS):
            pos = n * RNN_BW
            wi = pos // GATE_WIN
            for ni in range(tiles_per_win):
                k0 = wi * GATE_WIN + ni * LANES
                n0 = wi * GATE_WIN + ni * GATE_N
                lo, hi = max(pos, n0), min(pos + RNN_BW, n0 + GATE_N)
                if lo < hi:
                    o_ref[wi, ni, pos - k0:pos - k0 + RNN_BW, lo - n0:hi - n0] = (
                        w_ref[n][:, lo - pos:hi - pos].astype(o_ref.dtype))


def _pack_gates(wa, wx):
    gshape = (RNN_W // GATE_WIN, GATE_WIN // GATE_N, GATE_K, GATE_N)
    return pl.pallas_call(
        _pack_gate_kernel,
        out_shape=(jax.ShapeDtypeStruct(gshape, BF16), jax.ShapeDtypeStruct(gshape, BF16)),
        name="pack_gates",
    )(wa, wx)


def _prep_weights(p):
    w = {}
    row = lambda v: v.reshape(1, -1).astype(F32)
    w_in = p["w_in_0"]
    c = 2 * GLA_K + 2 * GLA_V
    w["w_in_0"] = w_in.astype(BF16)
    w["w_lr"] = jnp.pad(w_in[:, c:c + GLA_RANK], ((0, 0), (0, LANES - GLA_RANK))).astype(BF16)
    c += GLA_RANK
    w["w_u"] = w_in[:, c:c + S5_W].astype(BF16)
    w["w_alpha"] = jnp.pad(p["w_alpha_0"], ((0, LANES - GLA_RANK), (0, 0))).astype(BF16)
    w["b_alpha"] = row(p["b_alpha_0"])
    w["norm_mix_0"] = row(p["norm_mix_0"])
    w["gla_norm"] = row(p["gla_norm_0"])

    are, aim, bbre, bbim = _s5_prep(p["s5_lam_re"], p["s5_lam_im"], p["s5_log_dt"], p["s5_b_re"],
                                    p["s5_b_im"])
    npack = S5_GROUPS // S5_PACK
    eye = jnp.eye(S5_PACK, dtype=F32)[None, :, None, :, None]
    grouped = lambda m: m.reshape(npack, S5_PACK, S5_H, S5_P)
    pack_b = lambda m: (grouped(m)[:, :, :, None, :] * eye).reshape(
        npack, S5_PACK * S5_H, S5_PACK * S5_P).astype(BF16)
    pack_c = lambda m: (jnp.swapaxes(grouped(m), 2, 3)[:, :, :, None, :] * eye).reshape(
        npack, S5_PACK * S5_P, S5_PACK * S5_H).astype(BF16)
    w["s5_are"] = are.reshape(1, S5_N)
    w["s5_aim"] = aim.reshape(1, S5_N)
    w["s5_bbre"] = pack_b(bbre)
    w["s5_bbim"] = pack_b(bbim)
    w["s5_cre"] = pack_c(p["s5_c_re"])
    w["s5_cim"] = pack_c(p["s5_c_im"])
    w["s5_d"] = row(p["s5_d"])
    w["s5_w_glu"] = p["s5_w_glu"].astype(BF16)
    w["s5_b_glu"] = row(p["s5_b_glu"])
    w["w_out_0"] = p["w_out_0"].astype(BF16)

    w["norm_mix_1"] = row(p["norm_mix_1"])
    w["w_in_1"] = p["w_in_1"].astype(BF16)
    w["rnn_conv_w"] = p["rnn_conv_w"].astype(F32)
    w["rnn_conv_b"] = row(p["rnn_conv_b"])
    w["rnn_wa"], w["rnn_wx"] = _pack_gates(p["rnn_w_a"], p["rnn_w_x"])
    w["rnn_b_a"] = row(p["rnn_b_a"])
    w["rnn_b_x"] = row(p["rnn_b_x"])
    w["rnn_lam"] = row(p["rnn_lam"])
    w["w_out_1"] = p["w_out_1"].astype(BF16)

    depth = p["norm_ffn"].shape[0]
    w["norm_ffn"] = p["norm_ffn"].reshape(depth, 1, D_MODEL)
    w["ffn_w_up"] = p["ffn_w_up"].astype(BF16)
    w["ffn_conv_w"] = p["ffn_conv_w"]
    w["ffn_conv_b"] = p["ffn_conv_b"].reshape(depth, 1, D_FF)
    w["ffn_w_down"] = p["ffn_w_down"].astype(BF16)
    w["norm_final"] = row(p["norm_final"])
    return w


def _tile_steps():
    return dict(l0=64, gla=32, ffn=64, l1=64)


def _batch_major(cache, nb):
    jb, c = cache.shape
    return jnp.transpose(cache.reshape(jb // nb, nb, c), (1, 0, 2))


def kernel(x_prompt, x_sample, state_gla, state_s5_re, state_s5_im, state_rglru, cache_rglru_conv,
           cache_ffn_conv, meta_tokens, norm_mix_0, w_in_0, w_alpha_0, b_alpha_0, gla_norm_0,
           s5_lam_re, s5_lam_im, s5_log_dt, s5_b_re, s5_b_im, s5_c_re, s5_c_im, s5_d, s5_w_glu,
           s5_b_glu, w_out_0, norm_mix_1, w_in_1, rnn_conv_w, rnn_conv_b, rnn_w_a, rnn_b_a, rnn_w_x,
           rnn_b_x, rnn_lam, w_out_1, norm_ffn, ffn_w_up, ffn_conv_w, ffn_conv_b, ffn_w_down, norm_final):
    w = _prep_weights(dict(
        norm_mix_0=norm_mix_0, w_in_0=w_in_0, w_alpha_0=w_alpha_0, b_alpha_0=b_alpha_0,
        gla_norm_0=gla_norm_0, s5_lam_re=s5_lam_re, s5_lam_im=s5_lam_im, s5_log_dt=s5_log_dt,
        s5_b_re=s5_b_re, s5_b_im=s5_b_im, s5_c_re=s5_c_re, s5_c_im=s5_c_im, s5_d=s5_d,
        s5_w_glu=s5_w_glu, s5_b_glu=s5_b_glu, w_out_0=w_out_0, norm_mix_1=norm_mix_1, w_in_1=w_in_1,
        rnn_conv_w=rnn_conv_w, rnn_conv_b=rnn_conv_b, rnn_w_a=rnn_w_a, rnn_b_a=rnn_b_a,
        rnn_w_x=rnn_w_x, rnn_b_x=rnn_b_x, rnn_lam=rnn_lam, w_out_1=w_out_1, norm_ffn=norm_ffn,
        ffn_w_up=ffn_w_up, ffn_conv_w=ffn_conv_w, ffn_conv_b=ffn_conv_b, ffn_w_down=ffn_w_down,
        norm_final=norm_final))

    bp = x_prompt.shape[0]
    bs = x_sample.shape[0]
    tt = _tile_steps()
    mrows = bp * N_META
    assert mrows == bs

    x_side = jnp.concatenate([jnp.repeat(meta_tokens.astype(F32), bp, axis=0),
                              x_sample.reshape(bs, D_MODEL)], axis=0)
    s5_re_s = state_s5_re.reshape(bs, S5_N)
    s5_im_s = state_s5_im.reshape(bs, S5_N)

    gla_in = lambda z: (z[0], z[1], z[2], z[4], z[3])
    gn = w["gla_norm"]

    side = _l0_in(x_side, w, mrows + bs, 1)
    og_m, gla_m = _gla_chunk(gla_in(side), gn, jnp.zeros((GLA_HEADS, GLA_DV, bp * GLA_DK), F32), bp, N_META,
                             row_blocks=(2, 0))
    og_s, gla_s = _gla_step(gla_in(side), gn, state_gla, mrows)
    x_m, re_m, im_m, x_s, re_s, im_s = _l0_out_side(side[5], og_m, og_s, x_side, s5_re_s, s5_im_s, w, bp, N_META)
    x_m, fc0_m, x_s, fc0_s = _ffn_side(x_m, x_s, cache_ffn_conv, w, 0, bp, N_META, False)
    x_m, h_m, rc_m, x_s, h_s, rc_s = _l1_side(x_m, x_s, state_rglru, cache_rglru_conv, w, bp, N_META)
    _, fc1_m, ys, fc1_s = _ffn_side(x_m, x_s, cache_ffn_conv, w, 1, bp, N_META, True)

    x, gla_p, re_p, im_p = _l0(x_prompt, gla_m, re_m, im_m, w, tt["l0"], tt["gla"])
    x, fc0_p = _ffn(x, fc0_m, w, 0, bp, tt["ffn"], False)
    x, h_p, rc_p = _l1(x, h_m, rc_m, w, bp, tt["l1"])
    yp, fc1_p = _ffn(x, fc1_m, w, 1, bp, tt["ffn"], True, batch_major_out=True)

    grp = lambda z, nb: z.reshape(nb, S5_GROUPS, S5_P)
    return (yp, ys.reshape(bs, 1, D_MODEL), _gla_state_from_stacked(gla_p), gla_s,
            grp(re_p, bp), grp(re_s, bs), grp(im_p, bp), grp(im_s, bs), h_p, h_s,
            _batch_major(rc_p, bp), rc_s,
            jnp.stack([_batch_major(fc0_p, bp), _batch_major(fc1_p, bp)]), jnp.stack([fc0_s, fc1_s]))
```

```python
import functools

import jax
import jax.numpy as jnp
from jax import lax
from jax.experimental import pallas as pl
from jax.experimental.pallas import tpu as pltpu

F32 = jnp.float32
BF16 = jnp.bfloat16

D_MODEL = 1024
N_META = 16
EPS = 1e-6
F32_TINY = 1.1754944e-38
GLA_HEADS = 4
GLA_DK = 64
GLA_DV = 128
GLA_RANK = 16
GLA_TAU = 16.0
GLA_K = GLA_HEADS * GLA_DK
GLA_V = GLA_HEADS * GLA_DV
S5_GROUPS = 32
S5_H = 16
S5_P = 64
S5_W = S5_GROUPS * S5_H
S5_N = S5_GROUPS * S5_P
RNN_W = 1536
RNN_BLOCKS = 16
RNN_BW = RNN_W // RNN_BLOCKS
RNN_C = 8.0
RNN_CONV = 4
D_FF = 2816
FFN_CONV = 3

LANES = 128
FF_CHUNK = 256
L1_SPLIT = 2
S5_PACK = 8
GATE_WIN = 768
GATE_K = 512
GATE_N = 256
VMEM_LIMIT = 56 * 1024 * 1024


def _rms(x, g):
    return x * lax.rsqrt(jnp.mean(x * x, axis=-1, keepdims=True) + EPS) * g


def _sigmoid(x):
    return 0.5 * jnp.tanh(0.5 * x) + 0.5


def _sqrt_nonneg(t):
    return t * lax.rsqrt(jnp.maximum(t, F32_TINY))


def _mm(a, w):
    return jnp.dot(a.astype(BF16), w, preferred_element_type=F32)


def _const_spec(shape, index=None):
    idx = tuple(index) if index is not None else (0,) * len(shape)
    return pl.BlockSpec(shape, lambda i: idx, pipeline_mode=pl.Buffered(1))


def _row_spec(rows, cols):
    return pl.BlockSpec((rows, cols), lambda i: (i, 0))


def _seq_spec(nb, tt, cols):
    return pl.BlockSpec((nb, tt, cols), lambda i: (0, i, 0))


def _tm_scratch(rows, cols):
    return pltpu.VMEM((cols // LANES, rows, LANES), F32)


def _load_time_major(x_ref, tm_sc):
    nb, tt, cols = x_ref.shape
    for b in range(nb):
        for j in range(cols // LANES):
            tm_sc[j, pl.ds(b, tt, stride=nb), :] = x_ref[b, :, j * LANES:(j + 1) * LANES]
    return jnp.concatenate([tm_sc[j] for j in range(cols // LANES)], axis=-1)


def _store_time_major(o_ref, val, tm_sc):
    if tm_sc is None:
        o_ref[...] = val
        return
    nb, tt, cols = o_ref.shape
    for j in range(cols // LANES):
        tm_sc[j] = val[:, j * LANES:(j + 1) * LANES]
    for b in range(nb):
        for j in range(cols // LANES):
            o_ref[b, :, j * LANES:(j + 1) * LANES] = tm_sc[j, pl.ds(b, tt, stride=nb), :]


def _cache_shape(nb, tt, taps, width):
    return (nb, taps, width) if tt == 1 else (taps * nb, width)


def _params(sem="arbitrary"):
    return pltpu.CompilerParams(dimension_semantics=(sem,), vmem_limit_bytes=VMEM_LIMIT)


def _s5_prep_kernel(lr_ref, li_ref, ldt_ref, brt_ref, bit_ref, are_ref, aim_ref, bbre_ref, bbim_ref):
    lr = lr_ref[...]
    li = li_ref[...]
    dt = jnp.exp(ldt_ref[...])
    mag = jnp.exp(lr * dt)
    ab_re = mag * jnp.cos(li * dt)
    ab_im = mag * jnp.sin(li * dt)
    den = lr * lr + li * li
    nr = ab_re - 1.0
    ni = ab_im
    f_re = (nr * lr + ni * li) / den
    f_im = (ni * lr - nr * li) / den
    are_ref[...] = ab_re
    aim_ref[...] = ab_im
    brt = brt_ref[...]
    bit = bit_ref[...]
    bbre_ref[...] = f_re[:, None, :] * brt - f_im[:, None, :] * bit
    bbim_ref[...] = f_re[:, None, :] * bit + f_im[:, None, :] * brt


def _s5_prep(lam_re, lam_im, log_dt, b_re, b_im):
    g, p, h = b_re.shape
    brt = jnp.transpose(b_re, (0, 2, 1))
    bit = jnp.transpose(b_im, (0, 2, 1))
    return pl.pallas_call(
        _s5_prep_kernel,
        out_shape=(jax.ShapeDtypeStruct((g, p), F32), jax.ShapeDtypeStruct((g, p), F32),
                   jax.ShapeDtypeStruct((g, h, p), F32), jax.ShapeDtypeStruct((g, h, p), F32)),
        name="s5_prep",
    )(lam_re, lam_im, log_dt.reshape(g, 1), brt, bit)


def _is_first():
    return pl.program_id(0) == 0


def _side_spec(rows, cols, block=0):
    return pl.BlockSpec((rows, cols), lambda i: (block, 0))


def _l0_in_body(x, wts, outs):
    g_ref, wq_ref, wk_ref, wv_ref, wg_ref, wu_ref, wlr_ref, wal_ref, bal_ref = wts
    q_ref, k_ref, v_ref, gs_ref, la_ref, u_ref = outs
    xn = _rms(x, g_ref[...]).astype(BF16)
    q_ref[...] = _mm(xn, wq_ref[...]) * (GLA_DK ** -0.5)
    k_ref[...] = _mm(xn, wk_ref[...])
    v_ref[...] = _mm(xn, wv_ref[...])
    g = _mm(xn, wg_ref[...])
    gs_ref[...] = g * _sigmoid(g)
    u_ref[...] = _mm(xn, wu_ref[...])
    lr = _mm(xn, wlr_ref[...])
    pre = _mm(lr, wal_ref[...]) + bal_ref[...]
    la_ref[...] = jax.nn.log_sigmoid(pre) * (1.0 / GLA_TAU)


def _l0_in_kernel(*refs):
    x_ref, wts, outs = refs[0], refs[1:1 + L0_IN_WEIGHTS], refs[1 + L0_IN_WEIGHTS:]
    _l0_in_body(x_ref[...], wts, outs)


L0_IN_COLS = (GLA_K, GLA_K, GLA_V, GLA_V, GLA_K, S5_W)
L0_IN_WEIGHTS = 9


def _l0_in_weight_specs():
    return [_const_spec((1, D_MODEL)),
            _const_spec((D_MODEL, GLA_K), (0, 0)), _const_spec((D_MODEL, GLA_K), (0, 1)),
            _const_spec((D_MODEL, GLA_V), (0, 1)), _const_spec((D_MODEL, GLA_V), (0, 2)),
            _const_spec((D_MODEL, S5_W)), _const_spec((D_MODEL, LANES)),
            _const_spec((LANES, GLA_K)), _const_spec((1, GLA_K))]


def _l0_in_weights(w):
    return (w["norm_mix_0"], w["w_in_0"], w["w_in_0"], w["w_in_0"], w["w_in_0"], w["w_u"], w["w_lr"],
            w["w_alpha"], w["b_alpha"])


def _l0_in(x, w):
    n = x.shape[0]
    return pl.pallas_call(
        _l0_in_kernel,
        grid=(1,),
        in_specs=[_row_spec(n, D_MODEL)] + _l0_in_weight_specs(),
        out_specs=[_row_spec(n, c) for c in L0_IN_COLS],
        out_shape=[jax.ShapeDtypeStruct((n, c), F32) for c in L0_IN_COLS],
        compiler_params=_params(),
        name="l0_in",
    )(x, *_l0_in_weights(w))


def _gla_chunk_body(ins, gn_ref, o_ref, st_ref, b_sc, nb, c):
    q_ref, k_ref, v_ref, la_ref, gs_ref = ins
    rows = nb * c
    seq_mask = nb - 1

    def cum_body(t, run):
        rws = pl.ds(pl.multiple_of(t * nb, nb), nb)
        run = run + la_ref[rws, :]
        b_sc[rws, :] = run
        return run

    bl = lax.fori_loop(0, c, cum_body, jnp.zeros((nb, GLA_K), F32))
    b = b_sc[...]
    k = k_ref[...]
    qt = q_ref[...] * jnp.exp(b)
    kt = k * jnp.exp(-b)
    kh = k * jnp.exp(jnp.concatenate([bl] * c, axis=0) - b)
    gam = jnp.exp(bl)

    ri = lax.broadcasted_iota(jnp.int32, (rows, rows), 0)
    ci = lax.broadcasted_iota(jnp.int32, (rows, rows), 1)
    pair_ok = (ri >= ci) & (((ri - ci) & seq_mask) == 0)
    xw = nb * GLA_DK
    own_blk = (lax.broadcasted_iota(jnp.int32, (rows, xw), 1) // GLA_DK
               == (lax.broadcasted_iota(jnp.int32, (rows, xw), 0) & seq_mask))
    own_blk_seq = (lax.broadcasted_iota(jnp.int32, (nb, xw), 1) // GLA_DK
                   == lax.broadcasted_iota(jnp.int32, (nb, xw), 0))
    reps = xw // LANES

    def head_dup(z, h):
        blk = z[:, (h // 2) * LANES:(h // 2 + 1) * LANES]
        rolled = pltpu.roll(blk, GLA_DK, axis=1)
        low = lax.broadcasted_iota(jnp.int32, blk.shape, 1) < GLA_DK
        return jnp.where(low, blk, rolled) if h % 2 == 0 else jnp.where(low, rolled, blk)

    def expand(zd, own):
        return jnp.where(own, jnp.concatenate([zd] * reps, axis=1), 0.0)

    for h in range(GLA_HEADS):
        vs = slice(h * GLA_DV, (h + 1) * GLA_DV)
        qd, ktd, khd = head_dup(qt, h), head_dup(kt, h), head_dup(kh, h)
        vb = v_ref[:, vs].astype(BF16)
        st = st_ref[h]
        att = lax.dot_general(qd[:, :GLA_DK].astype(BF16), ktd[:, :GLA_DK].astype(BF16),
                              (((1,), (1,)), ((), ())), preferred_element_type=F32)
        att = jnp.where(pair_ok, att, 0.0).astype(BF16)
        o = jnp.dot(att, vb, preferred_element_type=F32) + lax.dot_general(
            expand(qd, own_blk).astype(BF16), st.astype(BF16), (((1,), (1,)), ((), ())),
            preferred_element_type=F32)
        upd = lax.dot_general(vb, expand(khd, own_blk).astype(BF16), (((0,), (0,)), ((), ())),
                              preferred_element_type=F32)
        gam_row = jnp.sum(expand(head_dup(gam, h), own_blk_seq), axis=0, keepdims=True)
        st_ref[h] = st * gam_row + upd
        o_ref[:, vs] = _rms(o, gn_ref[:, vs]) * gs_ref[:, vs]


def _gla_chunk_kernel(*refs, nb, c):
    ins, gn_ref, st0_ref, o_ref, st_ref, b_sc = refs[0:5], refs[5], refs[6], refs[7], refs[8], refs[9]

    @pl.when(_is_first())
    def _():
        st_ref[...] = st0_ref[...]

    _gla_chunk_body(ins, gn_ref, o_ref, st_ref, b_sc, nb, c)


def _gla_chunk(ins, gn, st0, nb, c, row_blocks=None):
    assert nb & (nb - 1) == 0 and (nb * GLA_DK) % LANES == 0
    rows = nb * c
    n = ins[0].shape[0] // row_blocks[0] if row_blocks else ins[0].shape[0]
    first = (row_blocks[1] * n) // rows if row_blocks else 0
    sshape = (GLA_HEADS, GLA_DV, nb * GLA_DK)
    cols = (GLA_K, GLA_K, GLA_V, GLA_K, GLA_V)
    kern = functools.partial(_gla_chunk_kernel, nb=nb, c=c)
    return pl.pallas_call(
        kern,
        grid=(n // rows,),
        in_specs=[pl.BlockSpec((rows, w_), lambda i: (first + i, 0)) for w_ in cols]
                 + [_const_spec((1, GLA_V)), _const_spec(sshape)],
        out_specs=[_row_spec(rows, GLA_V), pl.BlockSpec(sshape, lambda i: (0, 0, 0))],
        out_shape=[jax.ShapeDtypeStruct((n, GLA_V), F32), jax.ShapeDtypeStruct(sshape, F32)],
        scratch_shapes=[pltpu.VMEM((rows, GLA_K), F32)],
        compiler_params=_params(),
        name="gla_chunk",
    )(*ins, gn, st0)


def _gla_state_from_stacked(st):
    nb = st.shape[2] // GLA_DK
    return jnp.transpose(st.reshape(GLA_HEADS, GLA_DV, nb, GLA_DK), (2, 0, 3, 1))


def _gla_step_kernel(q_ref, k_ref, la_ref, v_ref, gs_ref, gn_ref, s0_ref, o_ref, s_ref):
    qT = q_ref[...].T
    kT = k_ref[...].T
    aT = jnp.exp(la_ref[...]).T
    o_rows = []
    for b in range(q_ref.shape[0]):
        s_new = aT[:, b:b + 1] * s0_ref[b] + kT[:, b:b + 1] * v_ref[b:b + 1, :]
        s_ref[b] = s_new
        o_rows.append(jnp.sum(qT[:, b:b + 1] * s_new, axis=0, keepdims=True))
    o = jnp.concatenate(o_rows, axis=0)
    o_ref[...] = _rms(o, gn_ref[...]) * gs_ref[...]


def _gla_step(side, gn, s0, row0):
    nb = s0.shape[0]
    blk = row0 // nb
    q, k, v, la, gs = side
    heads = lambda z: jnp.transpose(z[row0:row0 + nb].reshape(nb, GLA_HEADS, GLA_DK), (1, 0, 2))
    hspec = pl.BlockSpec((None, nb, GLA_DK), lambda h: (h, 0, 0))
    vspec = pl.BlockSpec((nb, GLA_DV), lambda h: (blk, h))
    ospec = pl.BlockSpec((nb, GLA_DV), lambda h: (0, h))
    sspec = pl.BlockSpec((nb, None, GLA_DK, GLA_DV), lambda h: (0, h, 0, 0))
    return pl.pallas_call(
        _gla_step_kernel,
        grid=(GLA_HEADS,),
        in_specs=[hspec, hspec, hspec, vspec, vspec, pl.BlockSpec((1, GLA_DV), lambda h: (0, h)), sspec],
        out_specs=[ospec, sspec],
        out_shape=[jax.ShapeDtypeStruct((nb, GLA_V), F32),
                   jax.ShapeDtypeStruct((nb, GLA_HEADS, GLA_DK, GLA_DV), F32)],
        compiler_params=_params("parallel"),
        name="gla_step",
    )(heads(q), heads(k), heads(la), v, gs, gn, s0)


def _l0_out_body(u_ref, og_ref, x, wts, xo_ref, xre_ref, xim_ref, sr_sc, si_sc, nb, tt):
    (bbre_ref, bbim_ref, are_ref, aim_ref, cre_ref, cim_ref, d_ref, wglu_ref, bglu_ref, woa_ref,
     wob_ref) = wts
    u = u_ref[...]
    ub = u.astype(BF16)
    kin = S5_PACK * S5_H
    kst = S5_PACK * S5_P
    npack = S5_GROUPS // S5_PACK
    ys = []
    for j in range(npack):
        cs = slice(j * kst, (j + 1) * kst)
        uj = ub[:, j * kin:(j + 1) * kin]
        bur = jnp.dot(uj, bbre_ref[j], preferred_element_type=F32)
        bui = jnp.dot(uj, bbim_ref[j], preferred_element_type=F32)
        ar = jnp.broadcast_to(are_ref[:, cs], (nb, kst))
        ai = jnp.broadcast_to(aim_ref[:, cs], (nb, kst))
        xr = xre_ref[:, cs]
        xi = xim_ref[:, cs]
        for t in range(tt):
            rows = slice(t * nb, (t + 1) * nb)
            xr, xi = ar * xr - ai * xi + bur[rows], ar * xi + ai * xr + bui[rows]
            sr_sc[rows, cs] = xr
            si_sc[rows, cs] = xi
        xre_ref[:, cs] = xr
        xim_ref[:, cs] = xi
        ys.append(jnp.dot(sr_sc[:, cs].astype(BF16), cre_ref[j], preferred_element_type=F32)
                  - jnp.dot(si_sc[:, cs].astype(BF16), cim_ref[j], preferred_element_type=F32))
    y = jnp.concatenate(ys, axis=-1) + d_ref[...] * u
    y = jax.nn.gelu(y)
    y = y * _sigmoid(_mm(y, wglu_ref[...]) + bglu_ref[...])
    xo_ref[...] = x + _mm(og_ref[...], woa_ref[...]) + _mm(y, wob_ref[...])


L0_OUT_WEIGHTS = 11


def _l0_out_weight_specs():
    npack = S5_GROUPS // S5_PACK
    kin, kst = S5_PACK * S5_H, S5_PACK * S5_P
    return [_const_spec((npack, kin, kst)), _const_spec((npack, kin, kst)),
            _const_spec((1, S5_N)), _const_spec((1, S5_N)),
            _const_spec((npack, kst, kin)), _const_spec((npack, kst, kin)),
            _const_spec((1, S5_W)), _const_spec((S5_W, S5_W)), _const_spec((1, S5_W)),
            _const_spec((GLA_V, D_MODEL), (0, 0)), _const_spec((S5_W, D_MODEL), (1, 0))]


def _l0_out_weights(w):
    return (w["s5_bbre"], w["s5_bbim"], w["s5_are"], w["s5_aim"], w["s5_cre"], w["s5_cim"],
            w["s5_d"], w["s5_w_glu"], w["s5_b_glu"], w["w_out_0"], w["w_out_0"])


def _l0_out_side_kernel(*refs, nb, tt_meta, nb_s):
    um_ref, ogm_ref, xm_ref, us_ref, ogs_ref, xs_ref, xr0s_ref, xi0s_ref = refs[0:8]
    wts = refs[8:8 + L0_OUT_WEIGHTS]
    xom_ref, xre_ref, xim_ref, xos_ref, xres_ref, xims_ref, sr_sc, si_sc = refs[8 + L0_OUT_WEIGHTS:]
    xre_ref[...] = jnp.zeros(xre_ref.shape, F32)
    xim_ref[...] = jnp.zeros(xim_ref.shape, F32)
    _l0_out_body(um_ref, ogm_ref, xm_ref[...], wts, xom_ref, xre_ref, xim_ref, sr_sc, si_sc, nb, tt_meta)
    xres_ref[...] = xr0s_ref[...]
    xims_ref[...] = xi0s_ref[...]
    _l0_out_body(us_ref, ogs_ref, xs_ref[...], wts, xos_ref, xres_ref, xims_ref, sr_sc, si_sc, nb_s, 1)


def _l0_out_side(u_side, og_meta, og_samp, x_side, xr0_s, xi0_s, w, nb, tt_meta):
    mrows, nb_s = nb * tt_meta, xr0_s.shape[0]
    assert mrows == nb_s
    kern = functools.partial(_l0_out_side_kernel, nb=nb, tt_meta=tt_meta, nb_s=nb_s)
    f32 = lambda *shape: jax.ShapeDtypeStruct(shape, F32)
    full = lambda r, c: pl.BlockSpec((r, c), lambda i: (0, 0))
    return pl.pallas_call(
        kern,
        grid=(1,),
        in_specs=[_side_spec(mrows, S5_W, 0), full(mrows, GLA_V), _side_spec(mrows, D_MODEL, 0),
                  _side_spec(nb_s, S5_W, 1), full(nb_s, GLA_V), _side_spec(nb_s, D_MODEL, 1),
                  full(nb_s, S5_N), full(nb_s, S5_N)] + _l0_out_weight_specs(),
        out_specs=[full(mrows, D_MODEL), full(nb, S5_N), full(nb, S5_N), full(nb_s, D_MODEL),
                   full(nb_s, S5_N), full(nb_s, S5_N)],
        out_shape=[f32(mrows, D_MODEL), f32(nb, S5_N), f32(nb, S5_N), f32(nb_s, D_MODEL),
                   f32(nb_s, S5_N), f32(nb_s, S5_N)],
        scratch_shapes=[pltpu.VMEM((mrows, S5_N), F32)] * 2,
        compiler_params=_params(),
        name="l0_out_side",
    )(u_side, og_meta, x_side, u_side, og_samp, x_side, xr0_s, xi0_s, *_l0_out_weights(w))


def _l0_kernel(*refs, nb, tt, c):
    x_ref, st0_ref, xr0_ref, xi0_ref = refs[0:4]
    n_in = 4 + L0_IN_WEIGHTS
    in_wts, gn_ref, out_wts = refs[4:n_in], refs[n_in], refs[n_in + 1:n_in + 1 + L0_OUT_WEIGHTS]
    rest = refs[n_in + 1 + L0_OUT_WEIGHTS:]
    xo_ref, st_ref, xre_ref, xim_ref = rest[0:4]
    proj_sc, og_sc, b_sc, sr_sc, si_sc, tm_sc = rest[4:10], rest[10], rest[11], rest[12], rest[13], rest[14]

    @pl.when(_is_first())
    def _():
        st_ref[...] = st0_ref[...]
        xre_ref[...] = xr0_ref[...]
        xim_ref[...] = xi0_ref[...]

    x = _load_time_major(x_ref, tm_sc)
    _l0_in_body(x, in_wts, proj_sc)
    q_sc, k_sc, v_sc, gs_sc, la_sc, u_sc = proj_sc
    for j in range(tt // c):
        rws = pl.ds(j * nb * c, nb * c)
        _gla_chunk_body([r.at[rws] for r in (q_sc, k_sc, v_sc, la_sc, gs_sc)], gn_ref, og_sc.at[rws], st_ref,
                        b_sc, nb, c)
    _l0_out_body(u_sc, og_sc, x, out_wts, xo_ref, xre_ref, xim_ref, sr_sc, si_sc, nb, tt)


def _l0(x, st0, xr0, xi0, w, tt, c):
    nb, nt, _ = x.shape
    n, rows = nb * nt, nb * tt
    assert nb & (nb - 1) == 0 and (nb * GLA_DK) % LANES == 0 and tt % c == 0
    sshape = (GLA_HEADS, GLA_DV, nb * GLA_DK)
    kern = functools.partial(_l0_kernel, nb=nb, tt=tt, c=c)
    st_spec = pl.BlockSpec((nb, S5_N), lambda i: (0, 0))
    f32 = lambda *shape: jax.ShapeDtypeStruct(shape, F32)
    return pl.pallas_call(
        kern,
        grid=(n // rows,),
        in_specs=[_seq_spec(nb, tt, D_MODEL), _const_spec(sshape), _const_spec((nb, S5_N)),
                  _const_spec((nb, S5_N))] + _l0_in_weight_specs() + [_const_spec((1, GLA_V))]
                 + _l0_out_weight_specs(),
        out_specs=[_row_spec(rows, D_MODEL), pl.BlockSpec(sshape, lambda i: (0, 0, 0)), st_spec, st_spec],
        out_shape=[f32(n, D_MODEL), f32(*sshape), f32(nb, S5_N), f32(nb, S5_N)],
        scratch_shapes=[pltpu.VMEM((rows, cols), F32) for cols in L0_IN_COLS]
                       + [pltpu.VMEM((rows, GLA_V), F32), pltpu.VMEM((nb * c, GLA_K), F32),
                          pltpu.VMEM((rows, S5_N), F32), pltpu.VMEM((rows, S5_N), F32),
                          _tm_scratch(rows, D_MODEL)],
        compiler_params=_params(),
        name="l0_mixer",
    )(x, st0, xr0, xi0, *_l0_in_weights(w), w["gla_norm"], *_l0_out_weights(w))


def _ffn_body(x, wts, c_ref, hm_sc, nb, tt, final):
    g_ref, wg_ref, wv_ref, cw_ref, cb_ref, wd_ref, gf_ref = wts
    rows = nb * tt
    xn = _rms(x, g_ref[...]).astype(BF16)
    for ci in range(D_FF // FF_CHUNK):
        cs = slice(ci * FF_CHUNK, (ci + 1) * FF_CHUNK)
        gate = jnp.dot(xn, wg_ref[:, cs], preferred_element_type=F32)
        val = jnp.dot(xn, wv_ref[:, cs], preferred_element_type=F32)
        ext = jnp.concatenate([c_ref[:, cs], gate], axis=0)
        taps = [ext[j * nb:j * nb + rows] for j in range(FFN_CONV)]
        c_ref[:, cs] = ext[tt * nb:(tt + FFN_CONV - 1) * nb]
        y = cb_ref[:, cs] + taps[0] * cw_ref[0:1, cs]
        for j in range(1, FFN_CONV):
            y = y + taps[j] * cw_ref[j:j + 1, cs]
        hm_sc[:, cs] = (jax.nn.gelu(y) * val).astype(BF16)
    out = x + jnp.dot(hm_sc[...], wd_ref[...], preferred_element_type=F32)
    return _rms(out, gf_ref[...]) if final else out


FFN_WEIGHTS = 7


def _ffn_weight_specs(layer):
    return [_const_spec((None, 1, D_MODEL), (layer, 0, 0)),
            _const_spec((None, D_MODEL, D_FF), (layer, 0, 0)),
            _const_spec((None, D_MODEL, D_FF), (layer, 0, 1)),
            _const_spec((None, FFN_CONV, D_FF), (layer, 0, 0)),
            _const_spec((None, 1, D_FF), (layer, 0, 0)),
            _const_spec((None, D_FF, D_MODEL), (layer, 0, 0)), _const_spec((1, D_MODEL))]


def _ffn_weights(w):
    return (w["norm_ffn"], w["ffn_w_up"], w["ffn_w_up"], w["ffn_conv_w"], w["ffn_conv_b"], w["ffn_w_down"],
            w["norm_final"])


def _ffn_kernel(*refs, nb, tt, final, batch_major_out):
    x_ref, c0_ref = refs[0:2]
    wts = refs[2:2 + FFN_WEIGHTS]
    xo_ref, c_ref, hm_sc = refs[2 + FFN_WEIGHTS:5 + FFN_WEIGHTS]
    tm_sc = refs[5 + FFN_WEIGHTS] if batch_major_out else None

    @pl.when(_is_first())
    def _():
        c_ref[...] = c0_ref[...]

    _store_time_major(xo_ref, _ffn_body(x_ref[...], wts, c_ref, hm_sc, nb, tt, final), tm_sc)


def _ffn(x, c0, w, layer, nb, tt, final, batch_major_out=False):
    n = x.shape[0]
    rows = nb * tt
    kern = functools.partial(_ffn_kernel, nb=nb, tt=tt, final=final, batch_major_out=batch_major_out)
    if batch_major_out:
        o_spec, o_shape = _seq_spec(nb, tt, D_MODEL), (nb, n // nb, D_MODEL)
    else:
        o_spec, o_shape = _row_spec(rows, D_MODEL), (n, D_MODEL)
    cshape = _cache_shape(nb, tt, FFN_CONV - 1, D_FF)
    f32 = lambda *shape: jax.ShapeDtypeStruct(shape, F32)
    return pl.pallas_call(
        kern,
        grid=(n // rows,),
        in_specs=[_row_spec(rows, D_MODEL), _const_spec(cshape)] + _ffn_weight_specs(layer),
        out_specs=[o_spec, pl.BlockSpec(cshape, lambda i: (0, 0))],
        out_shape=[f32(*o_shape), f32(*cshape)],
        scratch_shapes=[pltpu.VMEM((rows, D_FF), BF16)]
                       + ([_tm_scratch(rows, D_MODEL)] if batch_major_out else []),
        compiler_params=_params(),
        name="ffn%d" % layer,
    )(x, c0, *_ffn_weights(w))


def _ffn_side_kernel(xm_ref, xs_ref, c0s_ref, g_ref, wg_ref, wv_ref, cw_ref, cb_ref, wd_ref, gf_ref,
                     xom_ref, c_ref, xos_ref, cs_ref, xn_sc, acc_sc, *, nb, tt_meta, final):
    ci = pl.program_id(0)
    mrows = nb * tt_meta

    @pl.when(ci == 0)
    def _():
        xn_sc[0:mrows, :] = _rms(xm_ref[...], g_ref[...]).astype(BF16)
        xn_sc[mrows:, :] = _rms(xs_ref[...], g_ref[...]).astype(BF16)
        acc_sc[...] = jnp.zeros(acc_sc.shape, F32)

    xn = xn_sc[...]
    gate = jnp.dot(xn, wg_ref[...], preferred_element_type=F32)
    val = jnp.dot(xn, wv_ref[...], preferred_element_type=F32)

    def conv(taps):
        y = cb_ref[...] + taps[0] * cw_ref[0:1, :]
        for j in range(1, FFN_CONV):
            y = y + taps[j] * cw_ref[j:j + 1, :]
        return y

    keep = (FFN_CONV - 1) * nb
    ext = jnp.concatenate([jnp.zeros((keep, gate.shape[1]), F32), gate[0:mrows]], axis=0)
    c_ref[...] = ext[mrows:mrows + keep]
    y_m = conv([ext[j * nb:j * nb + mrows] for j in range(FFN_CONV)])
    taps_s = [c0s_ref[:, j, :] for j in range(FFN_CONV - 1)] + [gate[mrows:]]
    for j in range(FFN_CONV - 1):
        cs_ref[:, j, :] = taps_s[j + 1]
    y_s = conv(taps_s)
    hm = (jax.nn.gelu(jnp.concatenate([y_m, y_s], axis=0)) * val).astype(BF16)
    acc_sc[...] += jnp.dot(hm, wd_ref[...], preferred_element_type=F32)

    @pl.when(ci == pl.num_programs(0) - 1)
    def _():
        out_m = xm_ref[...] + acc_sc[0:mrows, :]
        out_s = xs_ref[...] + acc_sc[mrows:, :]
        xom_ref[...] = _rms(out_m, gf_ref[...]) if final else out_m
        xos_ref[...] = _rms(out_s, gf_ref[...]) if final else out_s


def _ffn_side(x_meta, x_samp, c0_s, w, layer, nb, tt_meta, final):
    mrows, nb_s = nb * tt_meta, x_samp.shape[0]
    taps = FFN_CONV - 1
    nchunks = D_FF // FF_CHUNK
    kern = functools.partial(_ffn_side_kernel, nb=nb, tt_meta=tt_meta, final=final)
    cshape = _cache_shape(nb, tt_meta, taps, D_FF)
    f32 = lambda *shape: jax.ShapeDtypeStruct(shape, F32)
    whole = lambda shape: pl.BlockSpec(shape, lambda i: (0,) * len(shape))
    return pl.pallas_call(
        kern,
        grid=(nchunks,),
        in_specs=[whole((mrows, D_MODEL)), whole((nb_s, D_MODEL)),
                  pl.BlockSpec((None, nb_s, taps, FF_CHUNK), lambda i: (layer, 0, 0, i)),
                  pl.BlockSpec((None, 1, D_MODEL), lambda i: (layer, 0, 0)),
                  pl.BlockSpec((None, D_MODEL, FF_CHUNK), lambda i: (layer, 0, i)),
                  pl.BlockSpec((None, D_MODEL, FF_CHUNK), lambda i: (layer, 0, nchunks + i)),
                  pl.BlockSpec((None, FFN_CONV, FF_CHUNK), lambda i: (layer, 0, i)),
                  pl.BlockSpec((None, 1, FF_CHUNK), lambda i: (layer, 0, i)),
                  pl.BlockSpec((None, FF_CHUNK, D_MODEL), lambda i: (layer, i, 0)),
                  whole((1, D_MODEL))],
        out_specs=[whole((mrows, D_MODEL)), pl.BlockSpec((cshape[0], FF_CHUNK), lambda i: (0, i)),
                   whole((nb_s, D_MODEL)), pl.BlockSpec((nb_s, taps, FF_CHUNK), lambda i: (0, 0, i))],
        out_shape=[f32(mrows, D_MODEL), f32(*cshape), f32(nb_s, D_MODEL), f32(nb_s, taps, D_FF)],
        scratch_shapes=[pltpu.VMEM((mrows + nb_s, D_MODEL), BF16), pltpu.VMEM((mrows + nb_s, D_MODEL), F32)],
        compiler_params=_params(),
        name="ffn%d_side" % layer,
    )(x_meta, x_samp, c0_s, *_ffn_weights(w))


def _l1_body(x_ref, wts, xo_ref, h_ref, c0_ref, c_ref, a_sc, b_sc, nb, tt):
    g_ref, wgt_ref, wxr_ref, cw_ref, cb_ref, wa_ref, ba_ref, wx_ref, bx_ref, lam_ref, wo_ref = wts
    sp = jax.nn.softplus(-lam_ref[...])
    nsplit = L1_SPLIT if tt % L1_SPLIT == 0 else 1
    th = tt // nsplit
    rows = nb * th
    carry = c_ref[...] if tt > 1 else None
    h = h_ref[...]
    for part in range(nsplit):
        prow = slice(part * rows, (part + 1) * rows)
        x = x_ref[prow, :]
        xn = _rms(x, g_ref[...]).astype(BF16)
        xr = jnp.dot(xn, wxr_ref[...], preferred_element_type=F32)
        if tt == 1:
            taps = [c0_ref[:, j, :] for j in range(RNN_CONV - 1)] + [xr]
            for j in range(RNN_CONV - 1):
                c_ref[:, j, :] = taps[j + 1]
        else:
            ext = jnp.concatenate([carry, xr], axis=0)
            taps = [ext[j * nb:j * nb + rows] for j in range(RNN_CONV)]
            carry = ext[th * nb:(th + RNN_CONV - 1) * nb]
        xc = cb_ref[...] + taps[0] * cw_ref[0:1, :]
        for j in range(1, RNN_CONV):
            xc = xc + taps[j] * cw_ref[j:j + 1, :]

        xcb = xc.astype(BF16)
        rs, gs = [], []
        for wi in range(RNN_W // GATE_WIN):
            for ni in range(GATE_WIN // GATE_N):
                k0 = wi * GATE_WIN + ni * LANES
                lhs = xcb[:, k0:k0 + GATE_K]
                rs.append(jnp.dot(lhs, wa_ref[wi, ni], preferred_element_type=F32))
                gs.append(jnp.dot(lhs, wx_ref[wi, ni], preferred_element_type=F32))
        r = _sigmoid(jnp.concatenate(rs, axis=-1) + ba_ref[...])
        ig = _sigmoid(jnp.concatenate(gs, axis=-1) + bx_ref[...])
        log_a = (-RNN_C) * r * sp
        a = jnp.exp(log_a)
        a_sc[prow, :] = a
        b_sc[prow, :] = _sqrt_nonneg(jnp.tanh(-log_a) * (a * a + 1.0)) * (ig * xc)
        gg = jax.nn.gelu(jnp.dot(xn, wgt_ref[...], preferred_element_type=F32))

        for t in range(part * th, (part + 1) * th):
            rws = slice(t * nb, (t + 1) * nb)
            h = a_sc[rws, :] * h + b_sc[rws, :]
            b_sc[rws, :] = h
        xo_ref[prow, :] = x + _mm(b_sc[prow, :] * gg, wo_ref[...])
    if tt > 1:
        c_ref[...] = carry
    h_ref[...] = h


L1_WEIGHTS = 11


def _l1_weight_specs():
    gshape = (RNN_W // GATE_WIN, GATE_WIN // GATE_N, GATE_K, GATE_N)
    return [_const_spec((1, D_MODEL)),
            _const_spec((D_MODEL, RNN_W), (0, 0)), _const_spec((D_MODEL, RNN_W), (0, 1)),
            _const_spec((RNN_CONV, RNN_W)), _const_spec((1, RNN_W)),
            _const_spec(gshape), _const_spec((1, RNN_W)),
            _const_spec(gshape), _const_spec((1, RNN_W)),
            _const_spec((1, RNN_W)), _const_spec((RNN_W, D_MODEL))]


def _l1_weights(w):
    return (w["norm_mix_1"], w["w_in_1"], w["w_in_1"], w["rnn_conv_w"], w["rnn_conv_b"], w["rnn_wa"],
            w["rnn_b_a"], w["rnn_wx"], w["rnn_b_x"], w["rnn_lam"], w["w_out_1"])


def _l1_kernel(*refs, nb, tt):
    x_ref, h0_ref, c0_ref = refs[0:3]
    wts = refs[3:3 + L1_WEIGHTS]
    xo_ref, h_ref, c_ref, a_sc, b_sc = refs[3 + L1_WEIGHTS:]

    @pl.when(_is_first())
    def _():
        h_ref[...] = h0_ref[...]
        c_ref[...] = c0_ref[...]

    _l1_body(x_ref, wts, xo_ref, h_ref, None, c_ref, a_sc, b_sc, nb, tt)


def _l1(x, h0, c0, w, nb, tt):
    n = x.shape[0]
    rows = nb * tt
    kern = functools.partial(_l1_kernel, nb=nb, tt=tt)
    cshape = _cache_shape(nb, tt, RNN_CONV - 1, RNN_W)
    f32 = lambda *shape: jax.ShapeDtypeStruct(shape, F32)
    return pl.pallas_call(
        kern,
        grid=(n // rows,),
        in_specs=[_row_spec(rows, D_MODEL), _const_spec((nb, RNN_W)), _const_spec(cshape)] + _l1_weight_specs(),
        out_specs=[_row_spec(rows, D_MODEL), pl.BlockSpec((nb, RNN_W), lambda i: (0, 0)),
                   pl.BlockSpec(cshape, lambda i: (0, 0))],
        out_shape=[f32(n, D_MODEL), f32(nb, RNN_W), f32(*cshape)],
        scratch_shapes=[pltpu.VMEM((rows, RNN_W), F32)] * 2,
        compiler_params=_params(),
        name="l1_mixer",
    )(x, h0, c0, *_l1_weights(w))


def _l1_side_kernel(*refs, nb, tt_meta, nb_s):
    xm_ref, xs_ref, h0s_ref, c0s_ref = refs[0:4]
    wts = refs[4:4 + L1_WEIGHTS]
    xom_ref, h_ref, c_ref, xos_ref, hs_ref, cs_ref, a_sc, b_sc = refs[4 + L1_WEIGHTS:]
    h_ref[...] = jnp.zeros(h_ref.shape, F32)
    c_ref[...] = jnp.zeros(c_ref.shape, F32)
    _l1_body(xm_ref, wts, xom_ref, h_ref, None, c_ref, a_sc, b_sc, nb, tt_meta)
    hs_ref[...] = h0s_ref[...]
    _l1_body(xs_ref, wts, xos_ref, hs_ref, c0s_ref, cs_ref, a_sc, b_sc, nb_s, 1)


def _l1_side(x_meta, x_samp, h0_s, c0_s, w, nb, tt_meta):
    mrows, nb_s = nb * tt_meta, x_samp.shape[0]
    assert mrows == nb_s
    kern = functools.partial(_l1_side_kernel, nb=nb, tt_meta=tt_meta, nb_s=nb_s)
    cshape = _cache_shape(nb, tt_meta, RNN_CONV - 1, RNN_W)
    cs_shape = _cache_shape(nb_s, 1, RNN_CONV - 1, RNN_W)
    f32 = lambda *shape: jax.ShapeDtypeStruct(shape, F32)
    full = lambda shape: pl.BlockSpec(shape, lambda i: (0,) * len(shape))
    return pl.pallas_call(
        kern,
        grid=(1,),
        in_specs=[full((mrows, D_MODEL)), full((nb_s, D_MODEL)), full((nb_s, RNN_W)), full(cs_shape)]
                 + _l1_weight_specs(),
        out_specs=[full((mrows, D_MODEL)), full((nb, RNN_W)), full(cshape), full((nb_s, D_MODEL)),
                   full((nb_s, RNN_W)), full(cs_shape)],
        out_shape=[f32(mrows, D_MODEL), f32(nb, RNN_W), f32(*cshape), f32(nb_s, D_MODEL), f32(nb_s, RNN_W),
                   f32(*cs_shape)],
        scratch_shapes=[pltpu.VMEM((mrows, RNN_W), F32)] * 2,
        compiler_params=_params(),
        name="l1_side",
    )(x_meta, x_samp, h0_s, c0_s, *_l1_weights(w))


def _pack_gate_kernel(wa_ref, wx_ref, oa_ref, ox_ref):
    tiles_per_win = GATE_WIN // GATE_N
    for w_ref, o_ref in ((wa_ref, oa_ref), (wx_ref, ox_ref)):
        o_ref[...] = jnp.zeros(o_ref.shape, o_ref.dtype)
        for n in range(RNN_BLOCKS):
            pos = n * RNN_BW
            wi = pos // GATE_WIN
            for ni in range(tiles_per_win):
                k0 = wi * GATE_WIN + ni * LANES
                n0 = wi * GATE_WIN + ni * GATE_N
                lo, hi = max(pos, n0), min(pos + RNN_BW, n0 + GATE_N)
                if lo < hi:
                    o_ref[wi, ni, pos - k0:pos - k0 + RNN_BW, lo - n0:hi - n0] = (
                        w_ref[n][:, lo - pos:hi - pos].astype(o_ref.dtype))


def _pack_gates(wa, wx):
    gshape = (RNN_W // GATE_WIN, GATE_WIN // GATE_N, GATE_K, GATE_N)
    return pl.pallas_call(
        _pack_gate_kernel,
        out_shape=(jax.ShapeDtypeStruct(gshape, BF16), jax.ShapeDtypeStruct(gshape, BF16)),
        name="pack_gates",
    )(wa, wx)


def _prep_weights(p):
    w = {}
    row = lambda v: v.reshape(1, -1).astype(F32)
    w_in = p["w_in_0"]
    c = 2 * GLA_K + 2 * GLA_V
    w["w_in_0"] = w_in.astype(BF16)
    w["w_lr"] = jnp.pad(w_in[:, c:c + GLA_RANK], ((0, 0), (0, LANES - GLA_RANK))).astype(BF16)
    c += GLA_RANK
    w["w_u"] = w_in[:, c:c + S5_W].astype(BF16)
    w["w_alpha"] = jnp.pad(p["w_alpha_0"], ((0, LANES - GLA_RANK), (0, 0))).astype(BF16)
    w["b_alpha"] = row(p["b_alpha_0"])
    w["norm_mix_0"] = row(p["norm_mix_0"])
    w["gla_norm"] = row(p["gla_norm_0"])

    are, aim, bbre, bbim = _s5_prep(p["s5_lam_re"], p["s5_lam_im"], p["s5_log_dt"], p["s5_b_re"],
                                    p["s5_b_im"])
    npack = S5_GROUPS // S5_PACK
    eye = jnp.eye(S5_PACK, dtype=F32)[None, :, None, :, None]
    grouped = lambda m: m.reshape(npack, S5_PACK, S5_H, S5_P)
    pack_b = lambda m: (grouped(m)[:, :, :, None, :] * eye).reshape(
        npack, S5_PACK * S5_H, S5_PACK * S5_P).astype(BF16)
    pack_c = lambda m: (jnp.swapaxes(grouped(m), 2, 3)[:, :, :, None, :] * eye).reshape(
        npack, S5_PACK * S5_P, S5_PACK * S5_H).astype(BF16)
    w["s5_are"] = are.reshape(1, S5_N)
    w["s5_aim"] = aim.reshape(1, S5_N)
    w["s5_bbre"] = pack_b(bbre)
    w["s5_bbim"] = pack_b(bbim)
    w["s5_cre"] = pack_c(p["s5_c_re"])
    w["s5_cim"] = pack_c(p["s5_c_im"])
    w["s5_d"] = row(p["s5_d"])
    w["s5_w_glu"] = p["s5_w_glu"].astype(BF16)
    w["s5_b_glu"] = row(p["s5_b_glu"])
    w["w_out_0"] = p["w_out_0"].astype(BF16)

    w["norm_mix_1"] = row(p["norm_mix_1"])
    w["w_in_1"] = p["w_in_1"].astype(BF16)
    w["rnn_conv_w"] = p["rnn_conv_w"].astype(F32)
    w["rnn_conv_b"] = row(p["rnn_conv_b"])
    w["rnn_wa"], w["rnn_wx"] = _pack_gates(p["rnn_w_a"], p["rnn_w_x"])
    w["rnn_b_a"] = row(p["rnn_b_a"])
    w["rnn_b_x"] = row(p["rnn_b_x"])
    w["rnn_lam"] = row(p["rnn_lam"])
    w["w_out_1"] = p["w_out_1"].astype(BF16)

    depth = p["norm_ffn"].shape[0]
    w["norm_ffn"] = p["norm_ffn"].reshape(depth, 1, D_MODEL)
    w["ffn_w_up"] = p["ffn_w_up"].astype(BF16)
    w["ffn_conv_w"] = p["ffn_conv_w"]
    w["ffn_conv_b"] = p["ffn_conv_b"].reshape(depth, 1, D_FF)
    w["ffn_w_down"] = p["ffn_w_down"].astype(BF16)
    w["norm_final"] = row(p["norm_final"])
    return w


def _tile_steps():
    return dict(l0=64, gla=32, ffn=64, l1=64)


def _batch_major(cache, nb):
    jb, c = cache.shape
    return jnp.transpose(cache.reshape(jb // nb, nb, c), (1, 0, 2))


def kernel(x_prompt, x_sample, state_gla, state_s5_re, state_s5_im, state_rglru, cache_rglru_conv,
           cache_ffn_conv, meta_tokens, norm_mix_0, w_in_0, w_alpha_0, b_alpha_0, gla_norm_0,
           s5_lam_re, s5_lam_im, s5_log_dt, s5_b_re, s5_b_im, s5_c_re, s5_c_im, s5_d, s5_w_glu,
           s5_b_glu, w_out_0, norm_mix_1, w_in_1, rnn_conv_w, rnn_conv_b, rnn_w_a, rnn_b_a, rnn_w_x,
           rnn_b_x, rnn_lam, w_out_1, norm_ffn, ffn_w_up, ffn_conv_w, ffn_conv_b, ffn_w_down, norm_final):
    w = _prep_weights(dict(
        norm_mix_0=norm_mix_0, w_in_0=w_in_0, w_alpha_0=w_alpha_0, b_alpha_0=b_alpha_0,
        gla_norm_0=gla_norm_0, s5_lam_re=s5_lam_re, s5_lam_im=s5_lam_im, s5_log_dt=s5_log_dt,
        s5_b_re=s5_b_re, s5_b_im=s5_b_im, s5_c_re=s5_c_re, s5_c_im=s5_c_im, s5_d=s5_d,
        s5_w_glu=s5_w_glu, s5_b_glu=s5_b_glu, w_out_0=w_out_0, norm_mix_1=norm_mix_1, w_in_1=w_in_1,
        rnn_conv_w=rnn_conv_w, rnn_conv_b=rnn_conv_b, rnn_w_a=rnn_w_a, rnn_b_a=rnn_b_a,
        rnn_w_x=rnn_w_x, rnn_b_x=rnn_b_x, rnn_lam=rnn_lam, w_out_1=w_out_1, norm_ffn=norm_ffn,
        ffn_w_up=ffn_w_up, ffn_conv_w=ffn_conv_w, ffn_conv_b=ffn_conv_b, ffn_w_down=ffn_w_down,
        norm_final=norm_final))

    bp = x_prompt.shape[0]
    bs = x_sample.shape[0]
    tt = _tile_steps()
    mrows = bp * N_META
    assert mrows == bs

    x_side = jnp.concatenate([jnp.repeat(meta_tokens.astype(F32), bp, axis=0),
                              x_sample.reshape(bs, D_MODEL)], axis=0)
    s5_re_s = state_s5_re.reshape(bs, S5_N)
    s5_im_s = state_s5_im.reshape(bs, S5_N)

    gla_in = lambda z: (z[0], z[1], z[2], z[4], z[3])
    gn = w["gla_norm"]

    side = _l0_in(x_side, w)
    og_m, gla_m = _gla_chunk(gla_in(side), gn, jnp.zeros((GLA_HEADS, GLA_DV, bp * GLA_DK), F32), bp, N_META,
                             row_blocks=(2, 0))
    og_s, gla_s = _gla_step(gla_in(side), gn, state_gla, mrows)
    x_m, re_m, im_m, x_s, re_s, im_s = _l0_out_side(side[5], og_m, og_s, x_side, s5_re_s, s5_im_s, w, bp, N_META)
    x_m, fc0_m, x_s, fc0_s = _ffn_side(x_m, x_s, cache_ffn_conv, w, 0, bp, N_META, False)
    x_m, h_m, rc_m, x_s, h_s, rc_s = _l1_side(x_m, x_s, state_rglru, cache_rglru_conv, w, bp, N_META)
    _, fc1_m, ys, fc1_s = _ffn_side(x_m, x_s, cache_ffn_conv, w, 1, bp, N_META, True)

    x, gla_p, re_p, im_p = _l0(x_prompt, gla_m, re_m, im_m, w, tt["l0"], tt["gla"])
    x, fc0_p = _ffn(x, fc0_m, w, 0, bp, tt["ffn"], False)
    x, h_p, rc_p = _l1(x, h_m, rc_m, w, bp, tt["l1"])
    yp, fc1_p = _ffn(x, fc1_m, w, 1, bp, tt["ffn"], True, batch_major_out=True)

    grp = lambda z, nb: z.reshape(nb, S5_GROUPS, S5_P)
    return (yp, ys.reshape(bs, 1, D_MODEL), _gla_state_from_stacked(gla_p), gla_s,
            grp(re_p, bp), grp(re_s, bs), grp(im_p, bp), grp(im_s, bs), h_p, h_s,
            _batch_major(rc_p, bp), rc_s,
            jnp.stack([_batch_major(fc0_p, bp), _batch_major(fc1_p, bp)]), jnp.stack([fc0_s, fc1_s]))
```

```python
import functools

import jax
import jax.numpy as jnp
from jax import lax
from jax.experimental import pallas as pl
from jax.experimental.pallas import tpu as pltpu

F32 = jnp.float32
BF16 = jnp.bfloat16

D_MODEL = 1024
N_META = 16
EPS = 1e-6
F32_TINY = 1.1754944e-38
GLA_HEADS = 4
GLA_DK = 64
GLA_DV = 128
GLA_RANK = 16
GLA_TAU = 16.0
GLA_K = GLA_HEADS * GLA_DK
GLA_V = GLA_HEADS * GLA_DV
S5_GROUPS = 32
S5_H = 16
S5_P = 64
S5_W = S5_GROUPS * S5_H
S5_N = S5_GROUPS * S5_P
RNN_W = 1536
RNN_BLOCKS = 16
RNN_BW = RNN_W // RNN_BLOCKS
RNN_C = 8.0
RNN_CONV = 4
D_FF = 2816
FFN_CONV = 3

LANES = 128
FF_CHUNK = 256
L1_SPLIT = 2
S5_PACK = 8
GATE_WIN = 768
GATE_K = 512
GATE_N = 256
VMEM_LIMIT = 56 * 1024 * 1024


def _rms(x, g):
    return x * lax.rsqrt(jnp.mean(x * x, axis=-1, keepdims=True) + EPS) * g


def _sigmoid(x):
    return 0.5 * jnp.tanh(0.5 * x) + 0.5


def _sqrt_nonneg(t):
    return t * lax.rsqrt(jnp.maximum(t, F32_TINY))


def _mm(a, w):
    return jnp.dot(a.astype(BF16), w, preferred_element_type=F32)


def _const_spec(shape, index=None):
    idx = tuple(index) if index is not None else (0,) * len(shape)
    return pl.BlockSpec(shape, lambda i: idx, pipeline_mode=pl.Buffered(1))


def _row_spec(rows, cols):
    return pl.BlockSpec((rows, cols), lambda i: (i, 0))


def _seq_spec(nb, tt, cols):
    return pl.BlockSpec((nb, tt, cols), lambda i: (0, i, 0))


def _tm_scratch(rows, cols):
    return pltpu.VMEM((cols // LANES, rows, LANES), F32)


def _load_time_major(x_ref, tm_sc):
    nb, tt, cols = x_ref.shape
    for b in range(nb):
        for j in range(cols // LANES):
            tm_sc[j, pl.ds(b, tt, stride=nb), :] = x_ref[b, :, j * LANES:(j + 1) * LANES]
    return jnp.concatenate([tm_sc[j] for j in range(cols // LANES)], axis=-1)


def _store_time_major(o_ref, val, tm_sc):
    if tm_sc is None:
        o_ref[...] = val
        return
    nb, tt, cols = o_ref.shape
    for j in range(cols // LANES):
        tm_sc[j] = val[:, j * LANES:(j + 1) * LANES]
    for b in range(nb):
        for j in range(cols // LANES):
            o_ref[b, :, j * LANES:(j + 1) * LANES] = tm_sc[j, pl.ds(b, tt, stride=nb), :]


def _cache_shape(nb, tt, taps, width):
    return (nb, taps, width) if tt == 1 else (taps * nb, width)


def _params(sem="arbitrary"):
    return pltpu.CompilerParams(dimension_semantics=(sem,), vmem_limit_bytes=VMEM_LIMIT)


def _s5_prep_kernel(lr_ref, li_ref, ldt_ref, brt_ref, bit_ref, are_ref, aim_ref, bbre_ref, bbim_ref):
    lr = lr_ref[...]
    li = li_ref[...]
    dt = jnp.exp(ldt_ref[...])
    mag = jnp.exp(lr * dt)
    ab_re = mag * jnp.cos(li * dt)
    ab_im = mag * jnp.sin(li * dt)
    den = lr * lr + li * li
    nr = ab_re - 1.0
    ni = ab_im
    f_re = (nr * lr + ni * li) / den
    f_im = (ni * lr - nr * li) / den
    are_ref[...] = ab_re
    aim_ref[...] = ab_im
    brt = brt_ref[...]
    bit = bit_ref[...]
    bbre_ref[...] = f_re[:, None, :] * brt - f_im[:, None, :] * bit
    bbim_ref[...] = f_re[:, None, :] * bit + f_im[:, None, :] * brt


def _s5_prep(lam_re, lam_im, log_dt, b_re, b_im):
    g, p, h = b_re.shape
    brt = jnp.transpose(b_re, (0, 2, 1))
    bit = jnp.transpose(b_im, (0, 2, 1))
    return pl.pallas_call(
        _s5_prep_kernel,
        out_shape=(jax.ShapeDtypeStruct((g, p), F32), jax.ShapeDtypeStruct((g, p), F32),
                   jax.ShapeDtypeStruct((g, h, p), F32), jax.ShapeDtypeStruct((g, h, p), F32)),
        name="s5_prep",
    )(lam_re, lam_im, log_dt.reshape(g, 1), brt, bit)


def _is_first():
    return pl.program_id(0) == 0


def _side_spec(rows, cols, block=0):
    return pl.BlockSpec((rows, cols), lambda i: (block, 0))


def _l0_in_body(x, wts, outs):
    g_ref, wq_ref, wk_ref, wv_ref, wg_ref, wu_ref, wlr_ref, wal_ref, bal_ref = wts
    q_ref, k_ref, v_ref, gs_ref, la_ref, u_ref = outs
    xn = _rms(x, g_ref[...]).astype(BF16)
    q_ref[...] = _mm(xn, wq_ref[...]) * (GLA_DK ** -0.5)
    k_ref[...] = _mm(xn, wk_ref[...])
    v_ref[...] = _mm(xn, wv_ref[...])
    g = _mm(xn, wg_ref[...])
    gs_ref[...] = g * _sigmoid(g)
    u_ref[...] = _mm(xn, wu_ref[...])
    lr = _mm(xn, wlr_ref[...])
    pre = _mm(lr, wal_ref[...]) + bal_ref[...]
    la_ref[...] = jax.nn.log_sigmoid(pre) * (1.0 / GLA_TAU)


def _l0_in_kernel(*refs):
    x_ref, wts, outs = refs[0], refs[1:1 + L0_IN_WEIGHTS], refs[1 + L0_IN_WEIGHTS:]
    _l0_in_body(x_ref[...], wts, outs)


L0_IN_COLS = (GLA_K, GLA_K, GLA_V, GLA_V, GLA_K, S5_W)
L0_IN_WEIGHTS = 9


def _l0_in_weight_specs():
    return [_const_spec((1, D_MODEL)),
            _const_spec((D_MODEL, GLA_K), (0, 0)), _const_spec((D_MODEL, GLA_K), (0, 1)),
            _const_spec((D_MODEL, GLA_V), (0, 1)), _const_spec((D_MODEL, GLA_V), (0, 2)),
            _const_spec((D_MODEL, S5_W)), _const_spec((D_MODEL, LANES)),
            _const_spec((LANES, GLA_K)), _const_spec((1, GLA_K))]


def _l0_in_weights(w):
    return (w["norm_mix_0"], w["w_in_0"], w["w_in_0"], w["w_in_0"], w["w_in_0"], w["w_u"], w["w_lr"],
            w["w_alpha"], w["b_alpha"])


def _l0_in(x, w):
    n = x.shape[0]
    return pl.pallas_call(
        _l0_in_kernel,
        grid=(1,),
        in_specs=[_row_spec(n, D_MODEL)] + _l0_in_weight_specs(),
        out_specs=[_row_spec(n, c) for c in L0_IN_COLS],
        out_shape=[jax.ShapeDtypeStruct((n, c), F32) for c in L0_IN_COLS],
        compiler_params=_params(),
        name="l0_in",
    )(x, *_l0_in_weights(w))


def _gla_chunk_body(ins, gn_ref, o_ref, st_ref, b_sc, nb, c):
    q_ref, k_ref, v_ref, la_ref, gs_ref = ins
    rows = nb * c
    seq_mask = nb - 1

    def cum_body(t, run):
        rws = pl.ds(pl.multiple_of(t * nb, nb), nb)
        run = run + la_ref[rws, :]
        b_sc[rws, :] = run
        return run

    bl = lax.fori_loop(0, c, cum_body, jnp.zeros((nb, GLA_K), F32))
    b = b_sc[...]
    k = k_ref[...]
    qt = q_ref[...] * jnp.exp(b)
    kt = k * jnp.exp(-b)
    kh = k * jnp.exp(jnp.concatenate([bl] * c, axis=0) - b)
    gam = jnp.exp(bl)

    ri = lax.broadcasted_iota(jnp.int32, (rows, rows), 0)
    ci = lax.broadcasted_iota(jnp.int32, (rows, rows), 1)
    pair_ok = (ri >= ci) & (((ri - ci) & seq_mask) == 0)
    xw = nb * GLA_DK
    own_blk = (lax.broadcasted_iota(jnp.int32, (rows, xw), 1) // GLA_DK
               == (lax.broadcasted_iota(jnp.int32, (rows, xw), 0) & seq_mask))
    own_blk_seq = (lax.broadcasted_iota(jnp.int32, (nb, xw), 1) // GLA_DK
                   == lax.broadcasted_iota(jnp.int32, (nb, xw), 0))
    reps = xw // LANES

    def head_dup(z, h):
        blk = z[:, (h // 2) * LANES:(h // 2 + 1) * LANES]
        rolled = pltpu.roll(blk, GLA_DK, axis=1)
        low = lax.broadcasted_iota(jnp.int32, blk.shape, 1) < GLA_DK
        return jnp.where(low, blk, rolled) if h % 2 == 0 else jnp.where(low, rolled, blk)

    def expand(zd, own):
        return jnp.where(own, jnp.concatenate([zd] * reps, axis=1), 0.0)

    for h in range(GLA_HEADS):
        vs = slice(h * GLA_DV, (h + 1) * GLA_DV)
        qd, ktd, khd = head_dup(qt, h), head_dup(kt, h), head_dup(kh, h)
        vb = v_ref[:, vs].astype(BF16)
        st = st_ref[h]
        att = lax.dot_general(qd[:, :GLA_DK].astype(BF16), ktd[:, :GLA_DK].astype(BF16),
                              (((1,), (1,)), ((), ())), preferred_element_type=F32)
        att = jnp.where(pair_ok, att, 0.0).astype(BF16)
        o = jnp.dot(att, vb, preferred_element_type=F32) + lax.dot_general(
            expand(qd, own_blk).astype(BF16), st.astype(BF16), (((1,), (1,)), ((), ())),
            preferred_element_type=F32)
        upd = lax.dot_general(vb, expand(khd, own_blk).astype(BF16), (((0,), (0,)), ((), ())),
                              preferred_element_type=F32)
        gam_row = jnp.sum(expand(head_dup(gam, h), own_blk_seq), axis=0, keepdims=True)
        st_ref[h] = st * gam_row + upd
        o_ref[:, vs] = _rms(o, gn_ref[:, vs]) * gs_ref[:, vs]


def _gla_chunk_kernel(*refs, nb, c):
    ins, gn_ref, st0_ref, o_ref, st_ref, b_sc = refs[0:5], refs[5], refs[6], refs[7], refs[8], refs[9]

    @pl.when(_is_first())
    def _():
        st_ref[...] = st0_ref[...]

    _gla_chunk_body(ins, gn_ref, o_ref, st_ref, b_sc, nb, c)


def _gla_chunk(ins, gn, st0, nb, c, row_blocks=None):
    assert nb & (nb - 1) == 0 and (nb * GLA_DK) % LANES == 0
    rows = nb * c
    n = ins[0].shape[0] // row_blocks[0] if row_blocks else ins[0].shape[0]
    first = (row_blocks[1] * n) // rows if row_blocks else 0
    sshape = (GLA_HEADS, GLA_DV, nb * GLA_DK)
    cols = (GLA_K, GLA_K, GLA_V, GLA_K, GLA_V)
    kern = functools.partial(_gla_chunk_kernel, nb=nb, c=c)
    return pl.pallas_call(
        kern,
        grid=(n // rows,),
        in_specs=[pl.BlockSpec((rows, w_), lambda i: (first + i, 0)) for w_ in cols]
                 + [_const_spec((1, GLA_V)), _const_spec(sshape)],
        out_specs=[_row_spec(rows, GLA_V), pl.BlockSpec(sshape, lambda i: (0, 0, 0))],
        out_shape=[jax.ShapeDtypeStruct((n, GLA_V), F32), jax.ShapeDtypeStruct(sshape, F32)],
        scratch_shapes=[pltpu.VMEM((rows, GLA_K), F32)],
        compiler_params=_params(),
        name="gla_chunk",
    )(*ins, gn, st0)


def _gla_state_from_stacked(st):
    nb = st.shape[2] // GLA_DK
    return jnp.transpose(st.reshape(GLA_HEADS, GLA_DV, nb, GLA_DK), (2, 0, 3, 1))


def _gla_step_kernel(q_ref, k_ref, la_ref, v_ref, gs_ref, gn_ref, s0_ref, o_ref, s_ref):
    qT = q_ref[...].T
    kT = k_ref[...].T
    aT = jnp.exp(la_ref[...]).T
    o_rows = []
    for b in range(q_ref.shape[0]):
        s_new = aT[:, b:b + 1] * s0_ref[b] + kT[:, b:b + 1] * v_ref[b:b + 1, :]
        s_ref[b] = s_new
        o_rows.append(jnp.sum(qT[:, b:b + 1] * s_new, axis=0, keepdims=True))
    o = jnp.concatenate(o_rows, axis=0)
    o_ref[...] = _rms(o, gn_ref[...]) * gs_ref[...]


def _gla_step(side, gn, s0, row0):
    nb = s0.shape[0]
    blk = row0 // nb
    q, k, v, la, gs = side
    heads = lambda z: jnp.transpose(z[row0:row0 + nb].reshape(nb, GLA_HEADS, GLA_DK), (1, 0, 2))
    hspec = pl.BlockSpec((None, nb, GLA_DK), lambda h: (h, 0, 0))
    vspec = pl.BlockSpec((nb, GLA_DV), lambda h: (blk, h))
    ospec = pl.BlockSpec((nb, GLA_DV), lambda h: (0, h))
    sspec = pl.BlockSpec((nb, None, GLA_DK, GLA_DV), lambda h: (0, h, 0, 0))
    return pl.pallas_call(
        _gla_step_kernel,
        grid=(GLA_HEADS,),
        in_specs=[hspec, hspec, hspec, vspec, vspec, pl.BlockSpec((1, GLA_DV), lambda h: (0, h)), sspec],
        out_specs=[ospec, sspec],
        out_shape=[jax.ShapeDtypeStruct((nb, GLA_V), F32),
                   jax.ShapeDtypeStruct((nb, GLA_HEADS, GLA_DK, GLA_DV), F32)],
        compiler_params=_params("parallel"),
        name="gla_step",
    )(heads(q), heads(k), heads(la), v, gs, gn, s0)


def _l0_out_body(u_ref, og_ref, x, wts, xo_ref, xre_ref, xim_ref, sr_sc, si_sc, nb, tt):
    (bbre_ref, bbim_ref, are_ref, aim_ref, cre_ref, cim_ref, d_ref, wglu_ref, bglu_ref, woa_ref,
     wob_ref) = wts
    u = u_ref[...]
    ub = u.astype(BF16)
    kin = S5_PACK * S5_H
    kst = S5_PACK * S5_P
    npack = S5_GROUPS // S5_PACK
    ys = []
    for j in range(npack):
        cs = slice(j * kst, (j + 1) * kst)
        uj = ub[:, j * kin:(j + 1) * kin]
        bur = jnp.dot(uj, bbre_ref[j], preferred_element_type=F32)
        bui = jnp.dot(uj, bbim_ref[j], preferred_element_type=F32)
        ar = jnp.broadcast_to(are_ref[:, cs], (nb, kst))
        ai = jnp.broadcast_to(aim_ref[:, cs], (nb, kst))
        xr = xre_ref[:, cs]
        xi = xim_ref[:, cs]
        for t in range(tt):
            rows = slice(t * nb, (t + 1) * nb)
            xr, xi = ar * xr - ai * xi + bur[rows], ar * xi + ai * xr + bui[rows]
            sr_sc[rows, cs] = xr
            si_sc[rows, cs] = xi
        xre_ref[:, cs] = xr
        xim_ref[:, cs] = xi
        ys.append(jnp.dot(sr_sc[:, cs].astype(BF16), cre_ref[j], preferred_element_type=F32)
                  - jnp.dot(si_sc[:, cs].astype(BF16), cim_ref[j], preferred_element_type=F32))
    y = jnp.concatenate(ys, axis=-1) + d_ref[...] * u
    y = jax.nn.gelu(y)
    y = y * _sigmoid(_mm(y, wglu_ref[...]) + bglu_ref[...])
    xo_ref[...] = x + _mm(og_ref[...], woa_ref[...]) + _mm(y, wob_ref[...])


L0_OUT_WEIGHTS = 11


def _l0_out_weight_specs():
    npack = S5_GROUPS // S5_PACK
    kin, kst = S5_PACK * S5_H, S5_PACK * S5_P
    return [_const_spec((npack, kin, kst)), _const_spec((npack, kin, kst)),
            _const_spec((1, S5_N)), _const_spec((1, S5_N)),
            _const_spec((npack, kst, kin)), _const_spec((npack, kst, kin)),
            _const_spec((1, S5_W)), _const_spec((S5_W, S5_W)), _const_spec((1, S5_W)),
            _const_spec((GLA_V, D_MODEL), (0, 0)), _const_spec((S5_W, D_MODEL), (1, 0))]


def _l0_out_weights(w):
    return (w["s5_bbre"], w["s5_bbim"], w["s5_are"], w["s5_aim"], w["s5_cre"], w["s5_cim"],
            w["s5_d"], w["s5_w_glu"], w["s5_b_glu"], w["w_out_0"], w["w_out_0"])


def _l0_out_side_kernel(*refs, nb, tt_meta, nb_s):
    um_ref, ogm_ref, xm_ref, us_ref, ogs_ref, xs_ref, xr0s_ref, xi0s_ref = refs[0:8]
    wts = refs[8:8 + L0_OUT_WEIGHTS]
    xom_ref, xre_ref, xim_ref, xos_ref, xres_ref, xims_ref, sr_sc, si_sc = refs[8 + L0_OUT_WEIGHTS:]
    xre_ref[...] = jnp.zeros(xre_ref.shape, F32)
    xim_ref[...] = jnp.zeros(xim_ref.shape, F32)
    _l0_out_body(um_ref, ogm_ref, xm_ref[...], wts, xom_ref, xre_ref, xim_ref, sr_sc, si_sc, nb, tt_meta)
    xres_ref[...] = xr0s_ref[...]
    xims_ref[...] = xi0s_ref[...]
    _l0_out_body(us_ref, ogs_ref, xs_ref[...], wts, xos_ref, xres_ref, xims_ref, sr_sc, si_sc, nb_s, 1)


def _l0_out_side(u_side, og_meta, og_samp, x_side, xr0_s, xi0_s, w, nb, tt_meta):
    mrows, nb_s = nb * tt_meta, xr0_s.shape[0]
    assert mrows == nb_s
    kern = functools.partial(_l0_out_side_kernel, nb=nb, tt_meta=tt_meta, nb_s=nb_s)
    f32 = lambda *shape: jax.ShapeDtypeStruct(shape, F32)
    full = lambda r, c: pl.BlockSpec((r, c), lambda i: (0, 0))
    return pl.pallas_call(
        kern,
        grid=(1,),
        in_specs=[_side_spec(mrows, S5_W, 0), full(mrows, GLA_V), _side_spec(mrows, D_MODEL, 0),
                  _side_spec(nb_s, S5_W, 1), full(nb_s, GLA_V), _side_spec(nb_s, D_MODEL, 1),
                  full(nb_s, S5_N), full(nb_s, S5_N)] + _l0_out_weight_specs(),
        out_specs=[full(mrows, D_MODEL), full(nb, S5_N), full(nb, S5_N), full(nb_s, D_MODEL),
                   full(nb_s, S5_N), full(nb_s, S5_N)],
        out_shape=[f32(mrows, D_MODEL), f32(nb, S5_N), f32(nb, S5_N), f32(nb_s, D_MODEL),
                   f32(nb_s, S5_N), f32(nb_s, S5_N)],
        scratch_shapes=[pltpu.VMEM((mrows, S5_N), F32)] * 2,
        compiler_params=_params(),
        name="l0_out_side",
    )(u_side, og_meta, x_side, u_side, og_samp, x_side, xr0_s, xi0_s, *_l0_out_weights(w))


def _l0_kernel(*refs, nb, tt, c):
    x_ref, st0_ref, xr0_ref, xi0_ref = refs[0:4]
    n_in = 4 + L0_IN_WEIGHTS
    in_wts, gn_ref, out_wts = refs[4:n_in], refs[n_in], refs[n_in + 1:n_in + 1 + L0_OUT_WEIGHTS]
    rest = refs[n_in + 1 + L0_OUT_WEIGHTS:]
    xo_ref, st_ref, xre_ref, xim_ref = rest[0:4]
    proj_sc, og_sc, b_sc, sr_sc, si_sc, tm_sc = rest[4:10], rest[10], rest[11], rest[12], rest[13], rest[14]

    @pl.when(_is_first())
    def _():
        st_ref[...] = st0_ref[...]
        xre_ref[...] = xr0_ref[...]
        xim_ref[...] = xi0_ref[...]

    x = _load_time_major(x_ref, tm_sc)
    _l0_in_body(x, in_wts, proj_sc)
    q_sc, k_sc, v_sc, gs_sc, la_sc, u_sc = proj_sc
    for j in range(tt // c):
        rws = pl.ds(j * nb * c, nb * c)
        _gla_chunk_body([r.at[rws] for r in (q_sc, k_sc, v_sc, la_sc, gs_sc)], gn_ref, og_sc.at[rws], st_ref,
                        b_sc, nb, c)
    _l0_out_body(u_sc, og_sc, x, out_wts, xo_ref, xre_ref, xim_ref, sr_sc, si_sc, nb, tt)


def _l0(x, st0, xr0, xi0, w, tt, c):
    nb, nt, _ = x.shape
    n, rows = nb * nt, nb * tt
    assert nb & (nb - 1) == 0 and (nb * GLA_DK) % LANES == 0 and tt % c == 0
    sshape = (GLA_HEADS, GLA_DV, nb * GLA_DK)
    kern = functools.partial(_l0_kernel, nb=nb, tt=tt, c=c)
    st_spec = pl.BlockSpec((nb, S5_N), lambda i: (0, 0))
    f32 = lambda *shape: jax.ShapeDtypeStruct(shape, F32)
    return pl.pallas_call(
        kern,
        grid=(n // rows,),
        in_specs=[_seq_spec(nb, tt, D_MODEL), _const_spec(sshape), _const_spec((nb, S5_N)),
                  _const_spec((nb, S5_N))] + _l0_in_weight_specs() + [_const_spec((1, GLA_V))]
                 + _l0_out_weight_specs(),
        out_specs=[_row_spec(rows, D_MODEL), pl.BlockSpec(sshape, lambda i: (0, 0, 0)), st_spec, st_spec],
        out_shape=[f32(n, D_MODEL), f32(*sshape), f32(nb, S5_N), f32(nb, S5_N)],
        scratch_shapes=[pltpu.VMEM((rows, cols), F32) for cols in L0_IN_COLS]
                       + [pltpu.VMEM((rows, GLA_V), F32), pltpu.VMEM((nb * c, GLA_K), F32),
                          pltpu.VMEM((rows, S5_N), F32), pltpu.VMEM((rows, S5_N), F32),
                          _tm_scratch(rows, D_MODEL)],
        compiler_params=_params(),
        name="l0_mixer",
    )(x, st0, xr0, xi0, *_l0_in_weights(w), w["gla_norm"], *_l0_out_weights(w))


def _ffn_body(x, wts, c0_ref, c_ref, hm_sc, nb, tt, final):
    g_ref, wg_ref, wv_ref, cw_ref, cb_ref, wd_ref, gf_ref = wts
    rows = nb * tt
    xn = _rms(x, g_ref[...]).astype(BF16)
    for ci in range(D_FF // FF_CHUNK):
        cs = slice(ci * FF_CHUNK, (ci + 1) * FF_CHUNK)
        gate = jnp.dot(xn, wg_ref[:, cs], preferred_element_type=F32)
        val = jnp.dot(xn, wv_ref[:, cs], preferred_element_type=F32)
        if tt == 1:
            taps = [c0_ref[:, j, cs] for j in range(FFN_CONV - 1)] + [gate]
            for j in range(FFN_CONV - 1):
                c_ref[:, j, cs] = taps[j + 1]
        else:
            ext = jnp.concatenate([c_ref[:, cs], gate], axis=0)
            taps = [ext[j * nb:j * nb + rows] for j in range(FFN_CONV)]
            c_ref[:, cs] = ext[tt * nb:(tt + FFN_CONV - 1) * nb]
        y = cb_ref[:, cs] + taps[0] * cw_ref[0:1, cs]
        for j in range(1, FFN_CONV):
            y = y + taps[j] * cw_ref[j:j + 1, cs]
        hm_sc[:, cs] = (jax.nn.gelu(y) * val).astype(BF16)
    out = x + jnp.dot(hm_sc[...], wd_ref[...], preferred_element_type=F32)
    return _rms(out, gf_ref[...]) if final else out


FFN_WEIGHTS = 7


def _ffn_weight_specs(layer):
    return [_const_spec((None, 1, D_MODEL), (layer, 0, 0)),
            _const_spec((None, D_MODEL, D_FF), (layer, 0, 0)),
            _const_spec((None, D_MODEL, D_FF), (layer, 0, 1)),
            _const_spec((None, FFN_CONV, D_FF), (layer, 0, 0)),
            _const_spec((None, 1, D_FF), (layer, 0, 0)),
            _const_spec((None, D_FF, D_MODEL), (layer, 0, 0)), _const_spec((1, D_MODEL))]


def _ffn_weights(w):
    return (w["norm_ffn"], w["ffn_w_up"], w["ffn_w_up"], w["ffn_conv_w"], w["ffn_conv_b"], w["ffn_w_down"],
            w["norm_final"])


def _ffn_kernel(*refs, nb, tt, final, batch_major_out):
    x_ref, c0_ref = refs[0:2]
    wts = refs[2:2 + FFN_WEIGHTS]
    xo_ref, c_ref, hm_sc = refs[2 + FFN_WEIGHTS:5 + FFN_WEIGHTS]
    tm_sc = refs[5 + FFN_WEIGHTS] if batch_major_out else None

    @pl.when(_is_first())
    def _():
        c_ref[...] = c0_ref[...]

    _store_time_major(xo_ref, _ffn_body(x_ref[...], wts, None, c_ref, hm_sc, nb, tt, final), tm_sc)


def _ffn(x, c0, w, layer, nb, tt, final, batch_major_out=False):
    n = x.shape[0]
    rows = nb * tt
    kern = functools.partial(_ffn_kernel, nb=nb, tt=tt, final=final, batch_major_out=batch_major_out)
    if batch_major_out:
        o_spec, o_shape = _seq_spec(nb, tt, D_MODEL), (nb, n // nb, D_MODEL)
    else:
        o_spec, o_shape = _row_spec(rows, D_MODEL), (n, D_MODEL)
    cshape = _cache_shape(nb, tt, FFN_CONV - 1, D_FF)
    f32 = lambda *shape: jax.ShapeDtypeStruct(shape, F32)
    return pl.pallas_call(
        kern,
        grid=(n // rows,),
        in_specs=[_row_spec(rows, D_MODEL), _const_spec(cshape)] + _ffn_weight_specs(layer),
        out_specs=[o_spec, pl.BlockSpec(cshape, lambda i: (0, 0))],
        out_shape=[f32(*o_shape), f32(*cshape)],
        scratch_shapes=[pltpu.VMEM((rows, D_FF), BF16)]
                       + ([_tm_scratch(rows, D_MODEL)] if batch_major_out else []),
        compiler_params=_params(),
        name="ffn%d" % layer,
    )(x, c0, *_ffn_weights(w))


def _ffn_side_kernel(*refs, nb, tt_meta, nb_s, final):
    xm_ref, xs_ref, c0s_ref = refs[0:3]
    wts = refs[3:3 + FFN_WEIGHTS]
    xom_ref, c_ref, xos_ref, cs_ref, hm_sc = refs[3 + FFN_WEIGHTS:]
    c_ref[...] = jnp.zeros(c_ref.shape, F32)
    xom_ref[...] = _ffn_body(xm_ref[...], wts, None, c_ref, hm_sc, nb, tt_meta, final)
    xos_ref[...] = _ffn_body(xs_ref[...], wts, c0s_ref, cs_ref, hm_sc, nb_s, 1, final)


def _ffn_side(x_meta, x_samp, c0_s, w, layer, nb, tt_meta, final):
    mrows, nb_s = nb * tt_meta, x_samp.shape[0]
    assert mrows == nb_s
    kern = functools.partial(_ffn_side_kernel, nb=nb, tt_meta=tt_meta, nb_s=nb_s, final=final)
    cshape = _cache_shape(nb, tt_meta, FFN_CONV - 1, D_FF)
    cs_shape = _cache_shape(nb_s, 1, FFN_CONV - 1, D_FF)
    f32 = lambda *shape: jax.ShapeDtypeStruct(shape, F32)
    full = lambda shape: pl.BlockSpec(shape, lambda i: (0,) * len(shape))
    return pl.pallas_call(
        kern,
        grid=(1,),
        in_specs=[full((mrows, D_MODEL)), full((nb_s, D_MODEL)),
                  pl.BlockSpec((None,) + cs_shape, lambda i: (layer, 0, 0, 0))] + _ffn_weight_specs(layer),
        out_specs=[full((mrows, D_MODEL)), full(cshape), full((nb_s, D_MODEL)), full(cs_shape)],
        out_shape=[f32(mrows, D_MODEL), f32(*cshape), f32(nb_s, D_MODEL), f32(*cs_shape)],
        scratch_shapes=[pltpu.VMEM((mrows, D_FF), BF16)],
        compiler_params=_params(),
        name="ffn%d_side" % layer,
    )(x_meta, x_samp, c0_s, *_ffn_weights(w))


def _l1_body(x_ref, wts, xo_ref, h_ref, c0_ref, c_ref, a_sc, b_sc, nb, tt):
    g_ref, wgt_ref, wxr_ref, cw_ref, cb_ref, wa_ref, ba_ref, wx_ref, bx_ref, lam_ref, wo_ref = wts
    sp = jax.nn.softplus(-lam_ref[...])
    nsplit = L1_SPLIT if tt % L1_SPLIT == 0 else 1
    th = tt // nsplit
    rows = nb * th
    carry = c_ref[...] if tt > 1 else None
    h = h_ref[...]
    for part in range(nsplit):
        prow = slice(part * rows, (part + 1) * rows)
        x = x_ref[prow, :]
        xn = _rms(x, g_ref[...]).astype(BF16)
        xr = jnp.dot(xn, wxr_ref[...], preferred_element_type=F32)
        if tt == 1:
            taps = [c0_ref[:, j, :] for j in range(RNN_CONV - 1)] + [xr]
            for j in range(RNN_CONV - 1):
                c_ref[:, j, :] = taps[j + 1]
        else:
            ext = jnp.concatenate([carry, xr], axis=0)
            taps = [ext[j * nb:j * nb + rows] for j in range(RNN_CONV)]
            carry = ext[th * nb:(th + RNN_CONV - 1) * nb]
        xc = cb_ref[...] + taps[0] * cw_ref[0:1, :]
        for j in range(1, RNN_CONV):
            xc = xc + taps[j] * cw_ref[j:j + 1, :]

        xcb = xc.astype(BF16)
        rs, gs = [], []
        for wi in range(RNN_W // GATE_WIN):
            for ni in range(GATE_WIN // GATE_N):
                k0 = wi * GATE_WIN + ni * LANES
                lhs = xcb[:, k0:k0 + GATE_K]
                rs.append(jnp.dot(lhs, wa_ref[wi, ni], preferred_element_type=F32))
                gs.append(jnp.dot(lhs, wx_ref[wi, ni], preferred_element_type=F32))
        r = _sigmoid(jnp.concatenate(rs, axis=-1) + ba_ref[...])
        ig = _sigmoid(jnp.concatenate(gs, axis=-1) + bx_ref[...])
        log_a = (-RNN_C) * r * sp
        a = jnp.exp(log_a)
        a_sc[prow, :] = a
        b_sc[prow, :] = _sqrt_nonneg(jnp.tanh(-log_a) * (a * a + 1.0)) * (ig * xc)
        gg = jax.nn.gelu(jnp.dot(xn, wgt_ref[...], preferred_element_type=F32))

        for t in range(part * th, (part + 1) * th):
            rws = slice(t * nb, (t + 1) * nb)
            h = a_sc[rws, :] * h + b_sc[rws, :]
            b_sc[rws, :] = h
        xo_ref[prow, :] = x + _mm(b_sc[prow, :] * gg, wo_ref[...])
    if tt > 1:
        c_ref[...] = carry
    h_ref[...] = h


L1_WEIGHTS = 11


def _l1_weight_specs():
    gshape = (RNN_W // GATE_WIN, GATE_WIN // GATE_N, GATE_K, GATE_N)
    return [_const_spec((1, D_MODEL)),
            _const_spec((D_MODEL, RNN_W), (0, 0)), _const_spec((D_MODEL, RNN_W), (0, 1)),
            _const_spec((RNN_CONV, RNN_W)), _const_spec((1, RNN_W)),
            _const_spec(gshape), _const_spec((1, RNN_W)),
            _const_spec(gshape), _const_spec((1, RNN_W)),
            _const_spec((1, RNN_W)), _const_spec((RNN_W, D_MODEL))]


def _l1_weights(w):
    return (w["norm_mix_1"], w["w_in_1"], w["w_in_1"], w["rnn_conv_w"], w["rnn_conv_b"], w["rnn_wa"],
            w["rnn_b_a"], w["rnn_wx"], w["rnn_b_x"], w["rnn_lam"], w["w_out_1"])


def _l1_kernel(*refs, nb, tt):
    x_ref, h0_ref, c0_ref = refs[0:3]
    wts = refs[3:3 + L1_WEIGHTS]
    xo_ref, h_ref, c_ref, a_sc, b_sc = refs[3 + L1_WEIGHTS:]

    @pl.when(_is_first())
    def _():
        h_ref[...] = h0_ref[...]
        c_ref[...] = c0_ref[...]

    _l1_body(x_ref, wts, xo_ref, h_ref, None, c_ref, a_sc, b_sc, nb, tt)


def _l1(x, h0, c0, w, nb, tt):
    n = x.shape[0]
    rows = nb * tt
    kern = functools.partial(_l1_kernel, nb=nb, tt=tt)
    cshape = _cache_shape(nb, tt, RNN_CONV - 1, RNN_W)
    f32 = lambda *shape: jax.ShapeDtypeStruct(shape, F32)
    return pl.pallas_call(
        kern,
        grid=(n // rows,),
        in_specs=[_row_spec(rows, D_MODEL), _const_spec((nb, RNN_W)), _const_spec(cshape)] + _l1_weight_specs(),
        out_specs=[_row_spec(rows, D_MODEL), pl.BlockSpec((nb, RNN_W), lambda i: (0, 0)),
                   pl.BlockSpec(cshape, lambda i: (0, 0))],
        out_shape=[f32(n, D_MODEL), f32(nb, RNN_W), f32(*cshape)],
        scratch_shapes=[pltpu.VMEM((rows, RNN_W), F32)] * 2,
        compiler_params=_params(),
        name="l1_mixer",
    )(x, h0, c0, *_l1_weights(w))


def _l1_side_kernel(*refs, nb, tt_meta, nb_s):
    xm_ref, xs_ref, h0s_ref, c0s_ref = refs[0:4]
    wts = refs[4:4 + L1_WEIGHTS]
    xom_ref, h_ref, c_ref, xos_ref, hs_ref, cs_ref, a_sc, b_sc = refs[4 + L1_WEIGHTS:]
    h_ref[...] = jnp.zeros(h_ref.shape, F32)
    c_ref[...] = jnp.zeros(c_ref.shape, F32)
    _l1_body(xm_ref, wts, xom_ref, h_ref, None, c_ref, a_sc, b_sc, nb, tt_meta)
    hs_ref[...] = h0s_ref[...]
    _l1_body(xs_ref, wts, xos_ref, hs_ref, c0s_ref, cs_ref, a_sc, b_sc, nb_s, 1)


def _l1_side(x_meta, x_samp, h0_s, c0_s, w, nb, tt_meta):
    mrows, nb_s = nb * tt_meta, x_samp.shape[0]
    assert mrows == nb_s
    kern = functools.partial(_l1_side_kernel, nb=nb, tt_meta=tt_meta, nb_s=nb_s)
    cshape = _cache_shape(nb, tt_meta, RNN_CONV - 1, RNN_W)
    cs_shape = _cache_shape(nb_s, 1, RNN_CONV - 1, RNN_W)
    f32 = lambda *shape: jax.ShapeDtypeStruct(shape, F32)
    full = lambda shape: pl.BlockSpec(shape, lambda i: (0,) * len(shape))
    return pl.pallas_call(
        kern,
        grid=(1,),
        in_specs=[full((mrows, D_MODEL)), full((nb_s, D_MODEL)), full((nb_s, RNN_W)), full(cs_shape)]
                 + _l1_weight_specs(),
        out_specs=[full((mrows, D_MODEL)), full((nb, RNN_W)), full(cshape), full((nb_s, D_MODEL)),
                   full((nb_s, RNN_W)), full(cs_shape)],
        out_shape=[f32(mrows, D_MODEL), f32(nb, RNN_W), f32(*cshape), f32(nb_s, D_MODEL), f32(nb_s, RNN_W),
                   f32(*cs_shape)],
        scratch_shapes=[pltpu.VMEM((mrows, RNN_W), F32)] * 2,
        compiler_params=_params(),
        name="l1_side",
    )(x_meta, x_samp, h0_s, c0_s, *_l1_weights(w))


def _pack_gate_kernel(wa_ref, wx_ref, oa_ref, ox_ref):
    tiles_per_win = GATE_WIN // GATE_N
    for w_ref, o_ref in ((wa_ref, oa_ref), (wx_ref, ox_ref)):
        o_ref[...] = jnp.zeros(o_ref.shape, o_ref.dtype)
        for n in range(RNN_BLOCKS):
            pos = n * RNN_BW
            wi = pos // GATE_WIN
            for ni in range(tiles_per_win):
                k0 = wi * GATE_WIN + ni * LANES
                n0 = wi * GATE_WIN + ni * GATE_N
                lo, hi = max(pos, n0), min(pos + RNN_BW, n0 + GATE_N)
                if lo < hi:
                    o_ref[wi, ni, pos - k0:pos - k0 + RNN_BW, lo - n0:hi - n0] = (
                        w_ref[n][:, lo - pos:hi - pos].astype(o_ref.dtype))


def _pack_gates(wa, wx):
    gshape = (RNN_W // GATE_WIN, GATE_WIN // GATE_N, GATE_K, GATE_N)
    return pl.pallas_call(
        _pack_gate_kernel,
        out_shape=(jax.ShapeDtypeStruct(gshape, BF16), jax.ShapeDtypeStruct(gshape, BF16)),
        name="pack_gates",
    )(wa, wx)


def _prep_weights(p):
    w = {}
    row = lambda v: v.reshape(1, -1).astype(F32)
    w_in = p["w_in_0"]
    c = 2 * GLA_K + 2 * GLA_V
    w["w_in_0"] = w_in.astype(BF16)
    w["w_lr"] = jnp.pad(w_in[:, c:c + GLA_RANK], ((0, 0), (0, LANES - GLA_RANK))).astype(BF16)
    c += GLA_RANK
    w["w_u"] = w_in[:, c:c + S5_W].astype(BF16)
    w["w_alpha"] = jnp.pad(p["w_alpha_0"], ((0, LANES - GLA_RANK), (0, 0))).astype(BF16)
    w["b_alpha"] = row(p["b_alpha_0"])
    w["norm_mix_0"] = row(p["norm_mix_0"])
    w["gla_norm"] = row(p["gla_norm_0"])

    are, aim, bbre, bbim = _s5_prep(p["s5_lam_re"], p["s5_lam_im"], p["s5_log_dt"], p["s5_b_re"],
                                    p["s5_b_im"])
    npack = S5_GROUPS // S5_PACK
    eye = jnp.eye(S5_PACK, dtype=F32)[None, :, None, :, None]
    grouped = lambda m: m.reshape(npack, S5_PACK, S5_H, S5_P)
    pack_b = lambda m: (grouped(m)[:, :, :, None, :] * eye).reshape(
        npack, S5_PACK * S5_H, S5_PACK * S5_P).astype(BF16)
    pack_c = lambda m: (jnp.swapaxes(grouped(m), 2, 3)[:, :, :, None, :] * eye).reshape(
        npack, S5_PACK * S5_P, S5_PACK * S5_H).astype(BF16)
    w["s5_are"] = are.reshape(1, S5_N)
    w["s5_aim"] = aim.reshape(1, S5_N)
    w["s5_bbre"] = pack_b(bbre)
    w["s5_bbim"] = pack_b(bbim)
    w["s5_cre"] = pack_c(p["s5_c_re"])
    w["s5_cim"] = pack_c(p["s5_c_im"])
    w["s5_d"] = row(p["s5_d"])
    w["s5_w_glu"] = p["s5_w_glu"].astype(BF16)
    w["s5_b_glu"] = row(p["s5_b_glu"])
    w["w_out_0"] = p["w_out_0"].astype(BF16)

    w["norm_mix_1"] = row(p["norm_mix_1"])
    w["w_in_1"] = p["w_in_1"].astype(BF16)
    w["rnn_conv_w"] = p["rnn_conv_w"].astype(F32)
    w["rnn_conv_b"] = row(p["rnn_conv_b"])
    w["rnn_wa"], w["rnn_wx"] = _pack_gates(p["rnn_w_a"], p["rnn_w_x"])
    w["rnn_b_a"] = row(p["rnn_b_a"])
    w["rnn_b_x"] = row(p["rnn_b_x"])
    w["rnn_lam"] = row(p["rnn_lam"])
    w["w_out_1"] = p["w_out_1"].astype(BF16)

    depth = p["norm_ffn"].shape[0]
    w["norm_ffn"] = p["norm_ffn"].reshape(depth, 1, D_MODEL)
    w["ffn_w_up"] = p["ffn_w_up"].astype(BF16)
    w["ffn_conv_w"] = p["ffn_conv_w"]
    w["ffn_conv_b"] = p["ffn_conv_b"].reshape(depth, 1, D_FF)
    w["ffn_w_down"] = p["ffn_w_down"].astype(BF16)
    w["norm_final"] = row(p["norm_final"])
    return w


def _tile_steps():
    return dict(l0=64, gla=32, ffn=64, l1=64)


def _batch_major(cache, nb):
    jb, c = cache.shape
    return jnp.transpose(cache.reshape(jb // nb, nb, c), (1, 0, 2))


def kernel(x_prompt, x_sample, state_gla, state_s5_re, state_s5_im, state_rglru, cache_rglru_conv,
           cache_ffn_conv, meta_tokens, norm_mix_0, w_in_0, w_alpha_0, b_alpha_0, gla_norm_0,
           s5_lam_re, s5_lam_im, s5_log_dt, s5_b_re, s5_b_im, s5_c_re, s5_c_im, s5_d, s5_w_glu,
           s5_b_glu, w_out_0, norm_mix_1, w_in_1, rnn_conv_w, rnn_conv_b, rnn_w_a, rnn_b_a, rnn_w_x,
           rnn_b_x, rnn_lam, w_out_1, norm_ffn, ffn_w_up, ffn_conv_w, ffn_conv_b, ffn_w_down, norm_final):
    w = _prep_weights(dict(
        norm_mix_0=norm_mix_0, w_in_0=w_in_0, w_alpha_0=w_alpha_0, b_alpha_0=b_alpha_0,
        gla_norm_0=gla_norm_0, s5_lam_re=s5_lam_re, s5_lam_im=s5_lam_im, s5_log_dt=s5_log_dt,
        s5_b_re=s5_b_re, s5_b_im=s5_b_im, s5_c_re=s5_c_re, s5_c_im=s5_c_im, s5_d=s5_d,
        s5_w_glu=s5_w_glu, s5_b_glu=s5_b_glu, w_out_0=w_out_0, norm_mix_1=norm_mix_1, w_in_1=w_in_1,
        rnn_conv_w=rnn_conv_w, rnn_conv_b=rnn_conv_b, rnn_w_a=rnn_w_a, rnn_b_a=rnn_b_a,
        rnn_w_x=rnn_w_x, rnn_b_x=rnn_b_x, rnn_lam=rnn_lam, w_out_1=w_out_1, norm_ffn=norm_ffn,
        ffn_w_up=ffn_w_up, ffn_conv_w=ffn_conv_w, ffn_conv_b=ffn_conv_b, ffn_w_down=ffn_w_down,
        norm_final=norm_final))

    bp = x_prompt.shape[0]
    bs = x_sample.shape[0]
    tt = _tile_steps()
    mrows = bp * N_META
    assert mrows == bs

    x_side = jnp.concatenate([jnp.repeat(meta_tokens.astype(F32), bp, axis=0),
                              x_sample.reshape(bs, D_MODEL)], axis=0)
    s5_re_s = state_s5_re.reshape(bs, S5_N)
    s5_im_s = state_s5_im.reshape(bs, S5_N)

    gla_in = lambda z: (z[0], z[1], z[2], z[4], z[3])
    gn = w["gla_norm"]

    side = _l0_in(x_side, w)
    og_m, gla_m = _gla_chunk(gla_in(side), gn, jnp.zeros((GLA_HEADS, GLA_DV, bp * GLA_DK), F32), bp, N_META,
                             row_blocks=(2, 0))
    og_s, gla_s = _gla_step(gla_in(side), gn, state_gla, mrows)
    x_m, re_m, im_m, x_s, re_s, im_s = _l0_out_side(side[5], og_m, og_s, x_side, s5_re_s, s5_im_s, w, bp, N_META)
    x_m, fc0_m, x_s, fc0_s = _ffn_side(x_m, x_s, cache_ffn_conv, w, 0, bp, N_META, False)
    x_m, h_m, rc_m, x_s, h_s, rc_s = _l1_side(x_m, x_s, state_rglru, cache_rglru_conv, w, bp, N_META)
    _, fc1_m, ys, fc1_s = _ffn_side(x_m, x_s, cache_ffn_conv, w, 1, bp, N_META, True)

    x, gla_p, re_p, im_p = _l0(x_prompt, gla_m, re_m, im_m, w, tt["l0"], tt["gla"])
    x, fc0_p = _ffn(x, fc0_m, w, 0, bp, tt["ffn"], False)
    x, h_p, rc_p = _l1(x, h_m, rc_m, w, bp, tt["l1"])
    yp, fc1_p = _ffn(x, fc1_m, w, 1, bp, tt["ffn"], True, batch_major_out=True)

    grp = lambda z, nb: z.reshape(nb, S5_GROUPS, S5_P)
    return (yp, ys.reshape(bs, 1, D_MODEL), _gla_state_from_stacked(gla_p), gla_s,
            grp(re_p, bp), grp(re_s, bs), grp(im_p, bp), grp(im_s, bs), h_p, h_s,
            _batch_major(rc_p, bp), rc_s,
            jnp.stack([_batch_major(fc0_p, bp), _batch_major(fc1_p, bp)]), jnp.stack([fc0_s, fc1_s]))
```

```python
import functools

import jax
import jax.numpy as jnp
from jax import lax
from jax.experimental import pallas as pl
from jax.experimental.pallas import tpu as pltpu

F32 = jnp.float32
BF16 = jnp.bfloat16

D_MODEL = 1024
N_META = 16
EPS = 1e-6
F32_TINY = 1.1754944e-38
GLA_HEADS = 4
GLA_DK = 64
GLA_DV = 128
GLA_RANK = 16
GLA_TAU = 16.0
GLA_K = GLA_HEADS * GLA_DK
GLA_V = GLA_HEADS * GLA_DV
S5_GROUPS = 32
S5_H = 16
S5_P = 64
S5_W = S5_GROUPS * S5_H
S5_N = S5_GROUPS * S5_P
RNN_W = 1536
RNN_BLOCKS = 16
RNN_BW = RNN_W // RNN_BLOCKS
RNN_C = 8.0
RNN_CONV = 4
D_FF = 2816
FFN_CONV = 3

LANES = 128
FF_CHUNK = 256
L1_SPLIT = 2
S5_PACK = 8
GATE_WIN = 768
GATE_K = 512
GATE_N = 256
VMEM_LIMIT = 56 * 1024 * 1024


def _rms(x, g):
    return x * lax.rsqrt(jnp.mean(x * x, axis=-1, keepdims=True) + EPS) * g


def _sigmoid(x):
    return 0.5 * jnp.tanh(0.5 * x) + 0.5


def _sqrt_nonneg(t):
    return t * lax.rsqrt(jnp.maximum(t, F32_TINY))


def _mm(a, w):
    return jnp.dot(a.astype(BF16), w, preferred_element_type=F32)


def _const_spec(shape, index=None):
    idx = tuple(index) if index is not None else (0,) * len(shape)
    return pl.BlockSpec(shape, lambda i: idx, pipeline_mode=pl.Buffered(1))


def _row_spec(rows, cols):
    return pl.BlockSpec((rows, cols), lambda i: (i, 0))


def _seq_spec(nb, tt, cols):
    return pl.BlockSpec((nb, tt, cols), lambda i: (0, i, 0))


def _tm_scratch(rows, cols):
    return pltpu.VMEM((cols // LANES, rows, LANES), F32)


def _load_time_major(x_ref, tm_sc):
    nb, tt, cols = x_ref.shape
    for b in range(nb):
        for j in range(cols // LANES):
            tm_sc[j, pl.ds(b, tt, stride=nb), :] = x_ref[b, :, j * LANES:(j + 1) * LANES]
    return jnp.concatenate([tm_sc[j] for j in range(cols // LANES)], axis=-1)


def _store_time_major(o_ref, val, tm_sc):
    if tm_sc is None:
        o_ref[...] = val
        return
    nb, tt, cols = o_ref.shape
    for j in range(cols // LANES):
        tm_sc[j] = val[:, j * LANES:(j + 1) * LANES]
    for b in range(nb):
        for j in range(cols // LANES):
            o_ref[b, :, j * LANES:(j + 1) * LANES] = tm_sc[j, pl.ds(b, tt, stride=nb), :]


def _cache_shape(nb, tt, taps, width):
    return (nb, taps, width) if tt == 1 else (taps * nb, width)


def _params(sem="arbitrary"):
    return pltpu.CompilerParams(dimension_semantics=(sem,), vmem_limit_bytes=VMEM_LIMIT)


def _s5_prep_kernel(lr_ref, li_ref, ldt_ref, brt_ref, bit_ref, are_ref, aim_ref, bbre_ref, bbim_ref):
    lr = lr_ref[...]
    li = li_ref[...]
    dt = jnp.exp(ldt_ref[...])
    mag = jnp.exp(lr * dt)
    ab_re = mag * jnp.cos(li * dt)
    ab_im = mag * jnp.sin(li * dt)
    den = lr * lr + li * li
    nr = ab_re - 1.0
    ni = ab_im
    f_re = (nr * lr + ni * li) / den
    f_im = (ni * lr - nr * li) / den
    are_ref[...] = ab_re
    aim_ref[...] = ab_im
    brt = brt_ref[...]
    bit = bit_ref[...]
    bbre_ref[...] = f_re[:, None, :] * brt - f_im[:, None, :] * bit
    bbim_ref[...] = f_re[:, None, :] * bit + f_im[:, None, :] * brt


def _s5_prep(lam_re, lam_im, log_dt, b_re, b_im):
    g, p, h = b_re.shape
    brt = jnp.transpose(b_re, (0, 2, 1))
    bit = jnp.transpose(b_im, (0, 2, 1))
    return pl.pallas_call(
        _s5_prep_kernel,
        out_shape=(jax.ShapeDtypeStruct((g, p), F32), jax.ShapeDtypeStruct((g, p), F32),
                   jax.ShapeDtypeStruct((g, h, p), F32), jax.ShapeDtypeStruct((g, h, p), F32)),
        name="s5_prep",
    )(lam_re, lam_im, log_dt.reshape(g, 1), brt, bit)


def _is_first():
    return pl.program_id(0) == 0


def _side_spec(rows, cols, block=0):
    return pl.BlockSpec((rows, cols), lambda i: (block, 0))


def _l0_in_body(x, wts, outs):
    g_ref, wq_ref, wk_ref, wv_ref, wg_ref, wu_ref, wlr_ref, wal_ref, bal_ref = wts
    q_ref, k_ref, v_ref, gs_ref, la_ref, u_ref = outs
    xn = _rms(x, g_ref[...]).astype(BF16)
    u_ref[...] = _mm(xn, wu_ref[...])
    lr = _mm(xn, wlr_ref[...])
    pre = _mm(lr, wal_ref[...]) + bal_ref[...]
    la_ref[...] = jax.nn.log_sigmoid(pre) * (1.0 / GLA_TAU)
    k_ref[...] = _mm(xn, wk_ref[...])
    q_ref[...] = _mm(xn, wq_ref[...]) * (GLA_DK ** -0.5)
    v_ref[...] = _mm(xn, wv_ref[...])
    g = _mm(xn, wg_ref[...])
    gs_ref[...] = g * _sigmoid(g)


def _l0_in_kernel(*refs):
    x_ref, wts, outs = refs[0], refs[1:1 + L0_IN_WEIGHTS], refs[1 + L0_IN_WEIGHTS:]
    _l0_in_body(x_ref[...], wts, outs)


L0_IN_COLS = (GLA_K, GLA_K, GLA_V, GLA_V, GLA_K, S5_W)
L0_IN_WEIGHTS = 9


def _l0_in_weight_specs():
    return [_const_spec((1, D_MODEL)),
            _const_spec((D_MODEL, GLA_K), (0, 0)), _const_spec((D_MODEL, GLA_K), (0, 1)),
            _const_spec((D_MODEL, GLA_V), (0, 1)), _const_spec((D_MODEL, GLA_V), (0, 2)),
            _const_spec((D_MODEL, S5_W)), _const_spec((D_MODEL, LANES)),
            _const_spec((LANES, GLA_K)), _const_spec((1, GLA_K))]


def _l0_in_weights(w):
    return (w["norm_mix_0"], w["w_in_0"], w["w_in_0"], w["w_in_0"], w["w_in_0"], w["w_u"], w["w_lr"],
            w["w_alpha"], w["b_alpha"])


def _l0_in(x, w):
    n = x.shape[0]
    return pl.pallas_call(
        _l0_in_kernel,
        grid=(1,),
        in_specs=[_row_spec(n, D_MODEL)] + _l0_in_weight_specs(),
        out_specs=[_row_spec(n, c) for c in L0_IN_COLS],
        out_shape=[jax.ShapeDtypeStruct((n, c), F32) for c in L0_IN_COLS],
        compiler_params=_params(),
        name="l0_in",
    )(x, *_l0_in_weights(w))


def _gla_chunk_body(ins, gn_ref, o_ref, st_ref, b_sc, nb, c):
    q_ref, k_ref, v_ref, la_ref, gs_ref = ins
    rows = nb * c
    seq_mask = nb - 1

    def cum_body(t, run):
        rws = pl.ds(pl.multiple_of(t * nb, nb), nb)
        run = run + la_ref[rws, :]
        b_sc[rws, :] = run
        return run

    bl = lax.fori_loop(0, c, cum_body, jnp.zeros((nb, GLA_K), F32), unroll=True)
    b = b_sc[...]
    k = k_ref[...]
    qt = q_ref[...] * jnp.exp(b)
    kt = k * jnp.exp(-b)
    kh = k * jnp.exp(jnp.concatenate([bl] * c, axis=0) - b)
    gam = jnp.exp(bl)

    ri = lax.broadcasted_iota(jnp.int32, (rows, rows), 0)
    ci = lax.broadcasted_iota(jnp.int32, (rows, rows), 1)
    pair_ok = (ri >= ci) & (((ri - ci) & seq_mask) == 0)
    xw = nb * GLA_DK
    own_blk = (lax.broadcasted_iota(jnp.int32, (rows, xw), 1) // GLA_DK
               == (lax.broadcasted_iota(jnp.int32, (rows, xw), 0) & seq_mask))
    own_blk_seq = (lax.broadcasted_iota(jnp.int32, (nb, xw), 1) // GLA_DK
                   == lax.broadcasted_iota(jnp.int32, (nb, xw), 0))
    reps = xw // LANES

    def head_dup(z, h):
        blk = z[:, (h // 2) * LANES:(h // 2 + 1) * LANES]
        rolled = pltpu.roll(blk, GLA_DK, axis=1)
        low = lax.broadcasted_iota(jnp.int32, blk.shape, 1) < GLA_DK
        return jnp.where(low, blk, rolled) if h % 2 == 0 else jnp.where(low, rolled, blk)

    def expand(zd, own):
        return jnp.where(own, jnp.concatenate([zd] * reps, axis=1), 0.0)

    for h in range(GLA_HEADS):
        vs = slice(h * GLA_DV, (h + 1) * GLA_DV)
        qd, ktd, khd = head_dup(qt, h), head_dup(kt, h), head_dup(kh, h)
        vb = v_ref[:, vs].astype(BF16)
        st = st_ref[h]
        att = lax.dot_general(qd[:, :GLA_DK].astype(BF16), ktd[:, :GLA_DK].astype(BF16),
                              (((1,), (1,)), ((), ())), preferred_element_type=F32)
        att = jnp.where(pair_ok, att, 0.0).astype(BF16)
        o = jnp.dot(att, vb, preferred_element_type=F32) + lax.dot_general(
            expand(qd, own_blk).astype(BF16), st.astype(BF16), (((1,), (1,)), ((), ())),
            preferred_element_type=F32)
        upd = lax.dot_general(vb, expand(khd, own_blk).astype(BF16), (((0,), (0,)), ((), ())),
                              preferred_element_type=F32)
        gam_row = jnp.sum(expand(head_dup(gam, h), own_blk_seq), axis=0, keepdims=True)
        st_ref[h] = st * gam_row + upd
        o_ref[:, vs] = _rms(o, gn_ref[:, vs]) * gs_ref[:, vs]


def _gla_chunk_kernel(*refs, nb, c):
    ins, gn_ref, st0_ref, o_ref, st_ref, b_sc = refs[0:5], refs[5], refs[6], refs[7], refs[8], refs[9]

    @pl.when(_is_first())
    def _():
        st_ref[...] = st0_ref[...]

    _gla_chunk_body(ins, gn_ref, o_ref, st_ref, b_sc, nb, c)


def _gla_chunk(ins, gn, st0, nb, c, row_blocks=None):
    assert nb & (nb - 1) == 0 and (nb * GLA_DK) % LANES == 0
    rows = nb * c
    n = ins[0].shape[0] // row_blocks[0] if row_blocks else ins[0].shape[0]
    first = (row_blocks[1] * n) // rows if row_blocks else 0
    sshape = (GLA_HEADS, GLA_DV, nb * GLA_DK)
    cols = (GLA_K, GLA_K, GLA_V, GLA_K, GLA_V)
    kern = functools.partial(_gla_chunk_kernel, nb=nb, c=c)
    return pl.pallas_call(
        kern,
        grid=(n // rows,),
        in_specs=[pl.BlockSpec((rows, w_), lambda i: (first + i, 0)) for w_ in cols]
                 + [_const_spec((1, GLA_V)), _const_spec(sshape)],
        out_specs=[_row_spec(rows, GLA_V), pl.BlockSpec(sshape, lambda i: (0, 0, 0))],
        out_shape=[jax.ShapeDtypeStruct((n, GLA_V), F32), jax.ShapeDtypeStruct(sshape, F32)],
        scratch_shapes=[pltpu.VMEM((rows, GLA_K), F32)],
        compiler_params=_params(),
        name="gla_chunk",
    )(*ins, gn, st0)


def _gla_state_from_stacked(st):
    nb = st.shape[2] // GLA_DK
    return jnp.transpose(st.reshape(GLA_HEADS, GLA_DV, nb, GLA_DK), (2, 0, 3, 1))


def _gla_step_kernel(q_ref, k_ref, la_ref, v_ref, gs_ref, gn_ref, s0_ref, o_ref, s_ref):
    qT = q_ref[...].T
    kT = k_ref[...].T
    aT = jnp.exp(la_ref[...]).T
    o_rows = []
    for b in range(q_ref.shape[0]):
        s_new = aT[:, b:b + 1] * s0_ref[b] + kT[:, b:b + 1] * v_ref[b:b + 1, :]
        s_ref[b] = s_new
        o_rows.append(jnp.sum(qT[:, b:b + 1] * s_new, axis=0, keepdims=True))
    o = jnp.concatenate(o_rows, axis=0)
    o_ref[...] = _rms(o, gn_ref[...]) * gs_ref[...]


def _gla_step(side, gn, s0, row0):
    nb = s0.shape[0]
    blk = row0 // nb
    q, k, v, la, gs = side
    heads = lambda z: jnp.transpose(z[row0:row0 + nb].reshape(nb, GLA_HEADS, GLA_DK), (1, 0, 2))
    hspec = pl.BlockSpec((None, nb, GLA_DK), lambda h: (h, 0, 0))
    vspec = pl.BlockSpec((nb, GLA_DV), lambda h: (blk, h))
    ospec = pl.BlockSpec((nb, GLA_DV), lambda h: (0, h))
    sspec = pl.BlockSpec((nb, None, GLA_DK, GLA_DV), lambda h: (0, h, 0, 0))
    return pl.pallas_call(
        _gla_step_kernel,
        grid=(GLA_HEADS,),
        in_specs=[hspec, hspec, hspec, vspec, vspec, pl.BlockSpec((1, GLA_DV), lambda h: (0, h)), sspec],
        out_specs=[ospec, sspec],
        out_shape=[jax.ShapeDtypeStruct((nb, GLA_V), F32),
                   jax.ShapeDtypeStruct((nb, GLA_HEADS, GLA_DK, GLA_DV), F32)],
        compiler_params=_params("parallel"),
        name="gla_step",
    )(heads(q), heads(k), heads(la), v, gs, gn, s0)


def _l0_out_body(u_ref, og_ref, x, wts, xo_ref, xre_ref, xim_ref, sr_sc, si_sc, nb, tt):
    y = _s5_body(u_ref, wts, xre_ref, xim_ref, sr_sc, si_sc, nb, tt)
    _l0_tail_body(y, og_ref, x, wts, xo_ref)


def _s5_body(u_ref, wts, xre_ref, xim_ref, sr_sc, si_sc, nb, tt):
    bbre_ref, bbim_ref, are_ref, aim_ref, cre_ref, cim_ref, d_ref, wglu_ref, bglu_ref = wts[:9]
    u = u_ref[...]
    ub = u.astype(BF16)
    kin = S5_PACK * S5_H
    kst = S5_PACK * S5_P
    npack = S5_GROUPS // S5_PACK
    ys = []
    for j in range(npack):
        cs = slice(j * kst, (j + 1) * kst)
        uj = ub[:, j * kin:(j + 1) * kin]
        bur = jnp.dot(uj, bbre_ref[j], preferred_element_type=F32)
        bui = jnp.dot(uj, bbim_ref[j], preferred_element_type=F32)
        ar = jnp.broadcast_to(are_ref[:, cs], (nb, kst))
        ai = jnp.broadcast_to(aim_ref[:, cs], (nb, kst))
        xr = xre_ref[:, cs]
        xi = xim_ref[:, cs]
        for t in range(tt):
            rows = slice(t * nb, (t + 1) * nb)
            xr, xi = ar * xr - ai * xi + bur[rows], ar * xi + ai * xr + bui[rows]
            sr_sc[rows, cs] = xr
            si_sc[rows, cs] = xi
        xre_ref[:, cs] = xr
        xim_ref[:, cs] = xi
        ys.append(jnp.dot(sr_sc[:, cs].astype(BF16), cre_ref[j], preferred_element_type=F32)
                  - jnp.dot(si_sc[:, cs].astype(BF16), cim_ref[j], preferred_element_type=F32))
    y = jnp.concatenate(ys, axis=-1) + d_ref[...] * u
    y = jax.nn.gelu(y)
    return y * _sigmoid(_mm(y, wglu_ref[...]) + bglu_ref[...])


def _l0_tail_body(y, og_ref, x, wts, xo_ref):
    woa_ref, wob_ref = wts[9:11]
    xo_ref[...] = x + _mm(og_ref[...], woa_ref[...]) + _mm(y, wob_ref[...])


L0_OUT_WEIGHTS = 11


def _l0_out_weight_specs():
    npack = S5_GROUPS // S5_PACK
    kin, kst = S5_PACK * S5_H, S5_PACK * S5_P
    return [_const_spec((npack, kin, kst)), _const_spec((npack, kin, kst)),
            _const_spec((1, S5_N)), _const_spec((1, S5_N)),
            _const_spec((npack, kst, kin)), _const_spec((npack, kst, kin)),
            _const_spec((1, S5_W)), _const_spec((S5_W, S5_W)), _const_spec((1, S5_W)),
            _const_spec((GLA_V, D_MODEL), (0, 0)), _const_spec((S5_W, D_MODEL), (1, 0))]


def _l0_out_weights(w):
    return (w["s5_bbre"], w["s5_bbim"], w["s5_are"], w["s5_aim"], w["s5_cre"], w["s5_cim"],
            w["s5_d"], w["s5_w_glu"], w["s5_b_glu"], w["w_out_0"], w["w_out_0"])


def _l0_out_side_kernel(*refs, nb, tt_meta, nb_s):
    um_ref, ogm_ref, xm_ref, us_ref, ogs_ref, xs_ref, xr0s_ref, xi0s_ref = refs[0:8]
    wts = refs[8:8 + L0_OUT_WEIGHTS]
    xom_ref, xre_ref, xim_ref, xos_ref, xres_ref, xims_ref, sr_sc, si_sc = refs[8 + L0_OUT_WEIGHTS:]
    xre_ref[...] = jnp.zeros(xre_ref.shape, F32)
    xim_ref[...] = jnp.zeros(xim_ref.shape, F32)
    _l0_out_body(um_ref, ogm_ref, xm_ref[...], wts, xom_ref, xre_ref, xim_ref, sr_sc, si_sc, nb, tt_meta)
    xres_ref[...] = xr0s_ref[...]
    xims_ref[...] = xi0s_ref[...]
    _l0_out_body(us_ref, ogs_ref, xs_ref[...], wts, xos_ref, xres_ref, xims_ref, sr_sc, si_sc, nb_s, 1)


def _l0_out_side(u_side, og_meta, og_samp, x_side, xr0_s, xi0_s, w, nb, tt_meta):
    mrows, nb_s = nb * tt_meta, xr0_s.shape[0]
    assert mrows == nb_s
    kern = functools.partial(_l0_out_side_kernel, nb=nb, tt_meta=tt_meta, nb_s=nb_s)
    f32 = lambda *shape: jax.ShapeDtypeStruct(shape, F32)
    full = lambda r, c: pl.BlockSpec((r, c), lambda i: (0, 0))
    return pl.pallas_call(
        kern,
        grid=(1,),
        in_specs=[_side_spec(mrows, S5_W, 0), full(mrows, GLA_V), _side_spec(mrows, D_MODEL, 0),
                  _side_spec(nb_s, S5_W, 1), full(nb_s, GLA_V), _side_spec(nb_s, D_MODEL, 1),
                  full(nb_s, S5_N), full(nb_s, S5_N)] + _l0_out_weight_specs(),
        out_specs=[full(mrows, D_MODEL), full(nb, S5_N), full(nb, S5_N), full(nb_s, D_MODEL),
                   full(nb_s, S5_N), full(nb_s, S5_N)],
        out_shape=[f32(mrows, D_MODEL), f32(nb, S5_N), f32(nb, S5_N), f32(nb_s, D_MODEL),
                   f32(nb_s, S5_N), f32(nb_s, S5_N)],
        scratch_shapes=[pltpu.VMEM((mrows, S5_N), F32)] * 2,
        compiler_params=_params(),
        name="l0_out_side",
    )(u_side, og_meta, x_side, u_side, og_samp, x_side, xr0_s, xi0_s, *_l0_out_weights(w))


def _l0_kernel(*refs, nb, tt, c):
    x_ref, st0_ref, xr0_ref, xi0_ref = refs[0:4]
    n_in = 4 + L0_IN_WEIGHTS
    in_wts, gn_ref, out_wts = refs[4:n_in], refs[n_in], refs[n_in + 1:n_in + 1 + L0_OUT_WEIGHTS]
    rest = refs[n_in + 1 + L0_OUT_WEIGHTS:]
    xo_ref, st_ref, xre_ref, xim_ref = rest[0:4]
    proj_sc, og_sc, b_sc, sr_sc, si_sc, tm_sc = rest[4:10], rest[10], rest[11], rest[12], rest[13], rest[14]

    @pl.when(_is_first())
    def _():
        st_ref[...] = st0_ref[...]
        xre_ref[...] = xr0_ref[...]
        xim_ref[...] = xi0_ref[...]

    x = _load_time_major(x_ref, tm_sc)
    _l0_in_body(x, in_wts, proj_sc)
    q_sc, k_sc, v_sc, gs_sc, la_sc, u_sc = proj_sc
    y = _s5_body(u_sc, out_wts, xre_ref, xim_ref, sr_sc, si_sc, nb, tt)
    for j in range(tt // c):
        rws = pl.ds(j * nb * c, nb * c)
        _gla_chunk_body([r.at[rws] for r in (q_sc, k_sc, v_sc, la_sc, gs_sc)], gn_ref, og_sc.at[rws], st_ref,
                        b_sc, nb, c)
    _l0_tail_body(y, og_sc, x, out_wts, xo_ref)


def _l0(x, st0, xr0, xi0, w, tt, c):
    nb, nt, _ = x.shape
    n, rows = nb * nt, nb * tt
    assert nb & (nb - 1) == 0 and (nb * GLA_DK) % LANES == 0 and tt % c == 0
    sshape = (GLA_HEADS, GLA_DV, nb * GLA_DK)
    kern = functools.partial(_l0_kernel, nb=nb, tt=tt, c=c)
    st_spec = pl.BlockSpec((nb, S5_N), lambda i: (0, 0))
    f32 = lambda *shape: jax.ShapeDtypeStruct(shape, F32)
    return pl.pallas_call(
        kern,
        grid=(n // rows,),
        in_specs=[_seq_spec(nb, tt, D_MODEL), _const_spec(sshape), _const_spec((nb, S5_N)),
                  _const_spec((nb, S5_N))] + _l0_in_weight_specs() + [_const_spec((1, GLA_V))]
                 + _l0_out_weight_specs(),
        out_specs=[_row_spec(rows, D_MODEL), pl.BlockSpec(sshape, lambda i: (0, 0, 0)), st_spec, st_spec],
        out_shape=[f32(n, D_MODEL), f32(*sshape), f32(nb, S5_N), f32(nb, S5_N)],
        scratch_shapes=[pltpu.VMEM((rows, cols), F32) for cols in L0_IN_COLS]
                       + [pltpu.VMEM((rows, GLA_V), F32), pltpu.VMEM((nb * c, GLA_K), F32),
                          pltpu.VMEM((rows, S5_N), F32), pltpu.VMEM((rows, S5_N), F32),
                          _tm_scratch(rows, D_MODEL)],
        compiler_params=_params(),
        name="l0_mixer",
    )(x, st0, xr0, xi0, *_l0_in_weights(w), w["gla_norm"], *_l0_out_weights(w))


def _ffn_body(x, wts, c0_ref, c_ref, hm_sc, nb, tt, final):
    g_ref, wg_ref, wv_ref, cw_ref, cb_ref, wd_ref, gf_ref = wts
    rows = nb * tt
    xn = _rms(x, g_ref[...]).astype(BF16)
    for ci in range(D_FF // FF_CHUNK):
        cs = slice(ci * FF_CHUNK, (ci + 1) * FF_CHUNK)
        gate = jnp.dot(xn, wg_ref[:, cs], preferred_element_type=F32)
        val = jnp.dot(xn, wv_ref[:, cs], preferred_element_type=F32)
        if tt == 1:
            taps = [c0_ref[:, j, cs] for j in range(FFN_CONV - 1)] + [gate]
            for j in range(FFN_CONV - 1):
                c_ref[:, j, cs] = taps[j + 1]
        else:
            ext = jnp.concatenate([c_ref[:, cs], gate], axis=0)
            taps = [ext[j * nb:j * nb + rows] for j in range(FFN_CONV)]
            c_ref[:, cs] = ext[tt * nb:(tt + FFN_CONV - 1) * nb]
        y = cb_ref[:, cs] + taps[0] * cw_ref[0:1, cs]
        for j in range(1, FFN_CONV):
            y = y + taps[j] * cw_ref[j:j + 1, cs]
        hm_sc[:, cs] = (jax.nn.gelu(y) * val).astype(BF16)
    out = x + jnp.dot(hm_sc[...], wd_ref[...], preferred_element_type=F32)
    return _rms(out, gf_ref[...]) if final else out


FFN_WEIGHTS = 7


def _ffn_weight_specs(layer):
    return [_const_spec((None, 1, D_MODEL), (layer, 0, 0)),
            _const_spec((None, D_MODEL, D_FF), (layer, 0, 0)),
            _const_spec((None, D_MODEL, D_FF), (layer, 0, 1)),
            _const_spec((None, FFN_CONV, D_FF), (layer, 0, 0)),
            _const_spec((None, 1, D_FF), (layer, 0, 0)),
            _const_spec((None, D_FF, D_MODEL), (layer, 0, 0)), _const_spec((1, D_MODEL))]


def _ffn_weights(w):
    return (w["norm_ffn"], w["ffn_w_up"], w["ffn_w_up"], w["ffn_conv_w"], w["ffn_conv_b"], w["ffn_w_down"],
            w["norm_final"])


def _ffn_kernel(*refs, nb, tt, final, batch_major_out):
    x_ref, c0_ref = refs[0:2]
    wts = refs[2:2 + FFN_WEIGHTS]
    xo_ref, c_ref, hm_sc = refs[2 + FFN_WEIGHTS:5 + FFN_WEIGHTS]
    tm_sc = refs[5 + FFN_WEIGHTS] if batch_major_out else None

    @pl.when(_is_first())
    def _():
        c_ref[...] = c0_ref[...]

    _store_time_major(xo_ref, _ffn_body(x_ref[...], wts, None, c_ref, hm_sc, nb, tt, final), tm_sc)


def _ffn(x, c0, w, layer, nb, tt, final, batch_major_out=False):
    n = x.shape[0]
    rows = nb * tt
    kern = functools.partial(_ffn_kernel, nb=nb, tt=tt, final=final, batch_major_out=batch_major_out)
    if batch_major_out:
        o_spec, o_shape = _seq_spec(nb, tt, D_MODEL), (nb, n // nb, D_MODEL)
    else:
        o_spec, o_shape = _row_spec(rows, D_MODEL), (n, D_MODEL)
    cshape = _cache_shape(nb, tt, FFN_CONV - 1, D_FF)
    f32 = lambda *shape: jax.ShapeDtypeStruct(shape, F32)
    return pl.pallas_call(
        kern,
        grid=(n // rows,),
        in_specs=[_row_spec(rows, D_MODEL), _const_spec(cshape)] + _ffn_weight_specs(layer),
        out_specs=[o_spec, pl.BlockSpec(cshape, lambda i: (0, 0))],
        out_shape=[f32(*o_shape), f32(*cshape)],
        scratch_shapes=[pltpu.VMEM((rows, D_FF), BF16)]
                       + ([_tm_scratch(rows, D_MODEL)] if batch_major_out else []),
        compiler_params=_params(),
        name="ffn%d" % layer,
    )(x, c0, *_ffn_weights(w))


def _ffn_side_kernel(*refs, nb, tt_meta, nb_s, final):
    xm_ref, xs_ref, c0s_ref = refs[0:3]
    wts = refs[3:3 + FFN_WEIGHTS]
    xom_ref, c_ref, xos_ref, cs_ref, hm_sc = refs[3 + FFN_WEIGHTS:]
    c_ref[...] = jnp.zeros(c_ref.shape, F32)
    xom_ref[...] = _ffn_body(xm_ref[...], wts, None, c_ref, hm_sc, nb, tt_meta, final)
    xos_ref[...] = _ffn_body(xs_ref[...], wts, c0s_ref, cs_ref, hm_sc, nb_s, 1, final)


def _ffn_side(x_meta, x_samp, c0_s, w, layer, nb, tt_meta, final):
    mrows, nb_s = nb * tt_meta, x_samp.shape[0]
    assert mrows == nb_s
    kern = functools.partial(_ffn_side_kernel, nb=nb, tt_meta=tt_meta, nb_s=nb_s, final=final)
    cshape = _cache_shape(nb, tt_meta, FFN_CONV - 1, D_FF)
    cs_shape = _cache_shape(nb_s, 1, FFN_CONV - 1, D_FF)
    f32 = lambda *shape: jax.ShapeDtypeStruct(shape, F32)
    full = lambda shape: pl.BlockSpec(shape, lambda i: (0,) * len(shape))
    return pl.pallas_call(
        kern,
        grid=(1,),
        in_specs=[full((mrows, D_MODEL)), full((nb_s, D_MODEL)),
                  pl.BlockSpec((None,) + cs_shape, lambda i: (layer, 0, 0, 0))] + _ffn_weight_specs(layer),
        out_specs=[full((mrows, D_MODEL)), full(cshape), full((nb_s, D_MODEL)), full(cs_shape)],
        out_shape=[f32(mrows, D_MODEL), f32(*cshape), f32(nb_s, D_MODEL), f32(*cs_shape)],
        scratch_shapes=[pltpu.VMEM((mrows, D_FF), BF16)],
        compiler_params=_params(),
        name="ffn%d_side" % layer,
    )(x_meta, x_samp, c0_s, *_ffn_weights(w))


def _l1_body(x_ref, wts, xo_ref, h_ref, c0_ref, c_ref, a_sc, b_sc, nb, tt):
    g_ref, wgt_ref, wxr_ref, cw_ref, cb_ref, wa_ref, ba_ref, wx_ref, bx_ref, lam_ref, wo_ref = wts
    sp = jax.nn.softplus(-lam_ref[...])
    nsplit = L1_SPLIT if tt % L1_SPLIT == 0 else 1
    th = tt // nsplit
    rows = nb * th
    carry = c_ref[...] if tt > 1 else None
    xs, xns, ggs = [], [], []

    for part in range(nsplit):
        prow = slice(part * rows, (part + 1) * rows)
        x = x_ref[prow, :]
        xn = _rms(x, g_ref[...]).astype(BF16)
        xs.append(x)
        xns.append(xn)
        xr = jnp.dot(xn, wxr_ref[...], preferred_element_type=F32)
        if tt == 1:
            taps = [c0_ref[:, j, :] for j in range(RNN_CONV - 1)] + [xr]
            for j in range(RNN_CONV - 1):
                c_ref[:, j, :] = taps[j + 1]
        else:
            ext = jnp.concatenate([carry, xr], axis=0)
            taps = [ext[j * nb:j * nb + rows] for j in range(RNN_CONV)]
            carry = ext[th * nb:(th + RNN_CONV - 1) * nb]
        xc = cb_ref[...] + taps[0] * cw_ref[0:1, :]
        for j in range(1, RNN_CONV):
            xc = xc + taps[j] * cw_ref[j:j + 1, :]

        xcb = xc.astype(BF16)
        rs, gs = [], []
        for wi in range(RNN_W // GATE_WIN):
            for ni in range(GATE_WIN // GATE_N):
                k0 = wi * GATE_WIN + ni * LANES
                lhs = xcb[:, k0:k0 + GATE_K]
                rs.append(jnp.dot(lhs, wa_ref[wi, ni], preferred_element_type=F32))
                gs.append(jnp.dot(lhs, wx_ref[wi, ni], preferred_element_type=F32))
        r = _sigmoid(jnp.concatenate(rs, axis=-1) + ba_ref[...])
        ig = _sigmoid(jnp.concatenate(gs, axis=-1) + bx_ref[...])
        log_a = (-RNN_C) * r * sp
        a = jnp.exp(log_a)
        a_sc[prow, :] = a
        b_sc[prow, :] = _sqrt_nonneg(jnp.tanh(-log_a) * (a * a + 1.0)) * (ig * xc)
    if tt > 1:
        c_ref[...] = carry

    for part in range(nsplit):
        ggs.append(jax.nn.gelu(jnp.dot(xns[part], wgt_ref[...], preferred_element_type=F32)))

    h = h_ref[...]
    for part in range(nsplit):
        prow = slice(part * rows, (part + 1) * rows)
        for t in range(part * th, (part + 1) * th):
            rws = slice(t * nb, (t + 1) * nb)
            h = a_sc[rws, :] * h + b_sc[rws, :]
            b_sc[rws, :] = h
        xo_ref[prow, :] = xs[part] + _mm(b_sc[prow, :] * ggs[part], wo_ref[...])
    h_ref[...] = h


L1_WEIGHTS = 11


def _l1_weight_specs():
    gshape = (RNN_W // GATE_WIN, GATE_WIN // GATE_N, GATE_K, GATE_N)
    return [_const_spec((1, D_MODEL)),
            _const_spec((D_MODEL, RNN_W), (0, 0)), _const_spec((D_MODEL, RNN_W), (0, 1)),
            _const_spec((RNN_CONV, RNN_W)), _const_spec((1, RNN_W)),
            _const_spec(gshape), _const_spec((1, RNN_W)),
            _const_spec(gshape), _const_spec((1, RNN_W)),
            _const_spec((1, RNN_W)), _const_spec((RNN_W, D_MODEL))]


def _l1_weights(w):
    return (w["norm_mix_1"], w["w_in_1"], w["w_in_1"], w["rnn_conv_w"], w["rnn_conv_b"], w["rnn_wa"],
            w["rnn_b_a"], w["rnn_wx"], w["rnn_b_x"], w["rnn_lam"], w["w_out_1"])


def _l1_kernel(*refs, nb, tt):
    x_ref, h0_ref, c0_ref = refs[0:3]
    wts = refs[3:3 + L1_WEIGHTS]
    xo_ref, h_ref, c_ref, a_sc, b_sc = refs[3 + L1_WEIGHTS:]

    @pl.when(_is_first())
    def _():
        h_ref[...] = h0_ref[...]
        c_ref[...] = c0_ref[...]

    _l1_body(x_ref, wts, xo_ref, h_ref, None, c_ref, a_sc, b_sc, nb, tt)


def _l1(x, h0, c0, w, nb, tt):
    n = x.shape[0]
    rows = nb * tt
    kern = functools.partial(_l1_kernel, nb=nb, tt=tt)
    cshape = _cache_shape(nb, tt, RNN_CONV - 1, RNN_W)
    f32 = lambda *shape: jax.ShapeDtypeStruct(shape, F32)
    return pl.pallas_call(
        kern,
        grid=(n // rows,),
        in_specs=[_row_spec(rows, D_MODEL), _const_spec((nb, RNN_W)), _const_spec(cshape)] + _l1_weight_specs(),
        out_specs=[_row_spec(rows, D_MODEL), pl.BlockSpec((nb, RNN_W), lambda i: (0, 0)),
                   pl.BlockSpec(cshape, lambda i: (0, 0))],
        out_shape=[f32(n, D_MODEL), f32(nb, RNN_W), f32(*cshape)],
        scratch_shapes=[pltpu.VMEM((rows, RNN_W), F32)] * 2,
        compiler_params=_params(),
        name="l1_mixer",
    )(x, h0, c0, *_l1_weights(w))


def _l1_side_kernel(*refs, nb, tt_meta, nb_s):
    xm_ref, xs_ref, h0s_ref, c0s_ref = refs[0:4]
    wts = refs[4:4 + L1_WEIGHTS]
    xom_ref, h_ref, c_ref, xos_ref, hs_ref, cs_ref, a_sc, b_sc = refs[4 + L1_WEIGHTS:]
    h_ref[...] = jnp.zeros(h_ref.shape, F32)
    c_ref[...] = jnp.zeros(c_ref.shape, F32)
    _l1_body(xm_ref, wts, xom_ref, h_ref, None, c_ref, a_sc, b_sc, nb, tt_meta)
    hs_ref[...] = h0s_ref[...]
    _l1_body(xs_ref, wts, xos_ref, hs_ref, c0s_ref, cs_ref, a_sc, b_sc, nb_s, 1)


def _l1_side(x_meta, x_samp, h0_s, c0_s, w, nb, tt_meta):
    mrows, nb_s = nb * tt_meta, x_samp.shape[0]
    assert mrows == nb_s
    kern = functools.partial(_l1_side_kernel, nb=nb, tt_meta=tt_meta, nb_s=nb_s)
    cshape = _cache_shape(nb, tt_meta, RNN_CONV - 1, RNN_W)
    cs_shape = _cache_shape(nb_s, 1, RNN_CONV - 1, RNN_W)
    f32 = lambda *shape: jax.ShapeDtypeStruct(shape, F32)
    full = lambda shape: pl.BlockSpec(shape, lambda i: (0,) * len(shape))
    return pl.pallas_call(
        kern,
        grid=(1,),
        in_specs=[full((mrows, D_MODEL)), full((nb_s, D_MODEL)), full((nb_s, RNN_W)), full(cs_shape)]
                 + _l1_weight_specs(),
        out_specs=[full((mrows, D_MODEL)), full((nb, RNN_W)), full(cshape), full((nb_s, D_MODEL)),
                   full((nb_s, RNN_W)), full(cs_shape)],
        out_shape=[f32(mrows, D_MODEL), f32(nb, RNN_W), f32(*cshape), f32(nb_s, D_MODEL), f32(nb_s, RNN_W),
                   f32(*cs_shape)],
        scratch_shapes=[pltpu.VMEM((mrows, RNN_W), F32)] * 2,
        compiler_params=_params(),
        name="l1_side",
    )(x_meta, x_samp, h0_s, c0_s, *_l1_weights(w))


def _pack_gate_kernel(wa_ref, wx_ref, oa_ref, ox_ref):
    tiles_per_win = GATE_WIN // GATE_N
    for w_ref, o_ref in ((wa_ref, oa_ref), (wx_ref, ox_ref)):
        o_ref[...] = jnp.zeros(o_ref.shape, o_ref.dtype)
        for n in range(RNN_BLOCKS):
            pos = n * RNN_BW
            wi = pos // GATE_WIN
            for ni in range(tiles_per_win):
                k0 = wi * GATE_WIN + ni * LANES
                n0 = wi * GATE_WIN + ni * GATE_N
                lo, hi = max(pos, n0), min(pos + RNN_BW, n0 + GATE_N)
                if lo < hi:
                    o_ref[wi, ni, pos - k0:pos - k0 + RNN_BW, lo - n0:hi - n0] = (
                        w_ref[n][:, lo - pos:hi - pos].astype(o_ref.dtype))


def _pack_gates(wa, wx):
    gshape = (RNN_W // GATE_WIN, GATE_WIN // GATE_N, GATE_K, GATE_N)
    return pl.pallas_call(
        _pack_gate_kernel,
        out_shape=(jax.ShapeDtypeStruct(gshape, BF16), jax.ShapeDtypeStruct(gshape, BF16)),
        name="pack_gates",
    )(wa, wx)


def _prep_weights(p):
    w = {}
    row = lambda v: v.reshape(1, -1).astype(F32)
    w_in = p["w_in_0"]
    c = 2 * GLA_K + 2 * GLA_V
    w["w_in_0"] = w_in.astype(BF16)
    w["w_lr"] = jnp.pad(w_in[:, c:c + GLA_RANK], ((0, 0), (0, LANES - GLA_RANK))).astype(BF16)
    c += GLA_RANK
    w["w_u"] = w_in[:, c:c + S5_W].astype(BF16)
    w["w_alpha"] = jnp.pad(p["w_alpha_0"], ((0, LANES - GLA_RANK), (0, 0))).astype(BF16)
    w["b_alpha"] = row(p["b_alpha_0"])
    w["norm_mix_0"] = row(p["norm_mix_0"])
    w["gla_norm"] = row(p["gla_norm_0"])

    are, aim, bbre, bbim = _s5_prep(p["s5_lam_re"], p["s5_lam_im"], p["s5_log_dt"], p["s5_b_re"],
                                    p["s5_b_im"])
    npack = S5_GROUPS // S5_PACK
    eye = jnp.eye(S5_PACK, dtype=F32)[None, :, None, :, None]
    grouped = lambda m: m.reshape(npack, S5_PACK, S5_H, S5_P)
    pack_b = lambda m: (grouped(m)[:, :, :, None, :] * eye).reshape(
        npack, S5_PACK * S5_H, S5_PACK * S5_P).astype(BF16)
    pack_c = lambda m: (jnp.swapaxes(grouped(m), 2, 3)[:, :, :, None, :] * eye).reshape(
        npack, S5_PACK * S5_P, S5_PACK * S5_H).astype(BF16)
    w["s5_are"] = are.reshape(1, S5_N)
    w["s5_aim"] = aim.reshape(1, S5_N)
    w["s5_bbre"] = pack_b(bbre)
    w["s5_bbim"] = pack_b(bbim)
    w["s5_cre"] = pack_c(p["s5_c_re"])
    w["s5_cim"] = pack_c(p["s5_c_im"])
    w["s5_d"] = row(p["s5_d"])
    w["s5_w_glu"] = p["s5_w_glu"].astype(BF16)
    w["s5_b_glu"] = row(p["s5_b_glu"])
    w["w_out_0"] = p["w_out_0"].astype(BF16)

    w["norm_mix_1"] = row(p["norm_mix_1"])
    w["w_in_1"] = p["w_in_1"].astype(BF16)
    w["rnn_conv_w"] = p["rnn_conv_w"].astype(F32)
    w["rnn_conv_b"] = row(p["rnn_conv_b"])
    w["rnn_wa"], w["rnn_wx"] = _pack_gates(p["rnn_w_a"], p["rnn_w_x"])
    w["rnn_b_a"] = row(p["rnn_b_a"])
    w["rnn_b_x"] = row(p["rnn_b_x"])
    w["rnn_lam"] = row(p["rnn_lam"])
    w["w_out_1"] = p["w_out_1"].astype(BF16)

    depth = p["norm_ffn"].shape[0]
    w["norm_ffn"] = p["norm_ffn"].reshape(depth, 1, D_MODEL)
    w["ffn_w_up"] = p["ffn_w_up"].astype(BF16)
    w["ffn_conv_w"] = p["ffn_conv_w"]
    w["ffn_conv_b"] = p["ffn_conv_b"].reshape(depth, 1, D_FF)
    w["ffn_w_down"] = p["ffn_w_down"].astype(BF16)
    w["norm_final"] = row(p["norm_final"])
    return w


def _tile_steps():
    return dict(l0=64, gla=32, ffn=64, l1=64)


def _batch_major(cache, nb):
    jb, c = cache.shape
    return jnp.transpose(cache.reshape(jb // nb, nb, c), (1, 0, 2))


def kernel(x_prompt, x_sample, state_gla, state_s5_re, state_s5_im, state_rglru, cache_rglru_conv,
           cache_ffn_conv, meta_tokens, norm_mix_0, w_in_0, w_alpha_0, b_alpha_0, gla_norm_0,
           s5_lam_re, s5_lam_im, s5_log_dt, s5_b_re, s5_b_im, s5_c_re, s5_c_im, s5_d, s5_w_glu,
           s5_b_glu, w_out_0, norm_mix_1, w_in_1, rnn_conv_w, rnn_conv_b, rnn_w_a, rnn_b_a, rnn_w_x,
           rnn_b_x, rnn_lam, w_out_1, norm_ffn, ffn_w_up, ffn_conv_w, ffn_conv_b, ffn_w_down, norm_final):
    w = _prep_weights(dict(
        norm_mix_0=norm_mix_0, w_in_0=w_in_0, w_alpha_0=w_alpha_0, b_alpha_0=b_alpha_0,
        gla_norm_0=gla_norm_0, s5_lam_re=s5_lam_re, s5_lam_im=s5_lam_im, s5_log_dt=s5_log_dt,
        s5_b_re=s5_b_re, s5_b_im=s5_b_im, s5_c_re=s5_c_re, s5_c_im=s5_c_im, s5_d=s5_d,
        s5_w_glu=s5_w_glu, s5_b_glu=s5_b_glu, w_out_0=w_out_0, norm_mix_1=norm_mix_1, w_in_1=w_in_1,
        rnn_conv_w=rnn_conv_w, rnn_conv_b=rnn_conv_b, rnn_w_a=rnn_w_a, rnn_b_a=rnn_b_a,
        rnn_w_x=rnn_w_x, rnn_b_x=rnn_b_x, rnn_lam=rnn_lam, w_out_1=w_out_1, norm_ffn=norm_ffn,
        ffn_w_up=ffn_w_up, ffn_conv_w=ffn_conv_w, ffn_conv_b=ffn_conv_b, ffn_w_down=ffn_w_down,
        norm_final=norm_final))

    bp = x_prompt.shape[0]
    bs = x_sample.shape[0]
    tt = _tile_steps()
    mrows = bp * N_META
    assert mrows == bs

    x_side = jnp.concatenate([jnp.repeat(meta_tokens.astype(F32), bp, axis=0),
                              x_sample.reshape(bs, D_MODEL)], axis=0)
    s5_re_s = state_s5_re.reshape(bs, S5_N)
    s5_im_s = state_s5_im.reshape(bs, S5_N)

    gla_in = lambda z: (z[0], z[1], z[2], z[4], z[3])
    gn = w["gla_norm"]

    side = _l0_in(x_side, w)
    og_m, gla_m = _gla_chunk(gla_in(side), gn, jnp.zeros((GLA_HEADS, GLA_DV, bp * GLA_DK), F32), bp, N_META,
                             row_blocks=(2, 0))
    og_s, gla_s = _gla_step(gla_in(side), gn, state_gla, mrows)
    x_m, re_m, im_m, x_s, re_s, im_s = _l0_out_side(side[5], og_m, og_s, x_side, s5_re_s, s5_im_s, w, bp, N_META)
    x_m, fc0_m, x_s, fc0_s = _ffn_side(x_m, x_s, cache_ffn_conv, w, 0, bp, N_META, False)
    x_m, h_m, rc_m, x_s, h_s, rc_s = _l1_side(x_m, x_s, state_rglru, cache_rglru_conv, w, bp, N_META)
    _, fc1_m, ys, fc1_s = _ffn_side(x_m, x_s, cache_ffn_conv, w, 1, bp, N_META, True)

    x, gla_p, re_p, im_p = _l0(x_prompt, gla_m, re_m, im_m, w, tt["l0"], tt["gla"])
    x, fc0_p = _ffn(x, fc0_m, w, 0, bp, tt["ffn"], False)
    x, h_p, rc_p = _l1(x, h_m, rc_m, w, bp, tt["l1"])
    yp, fc1_p = _ffn(x, fc1_m, w, 1, bp, tt["ffn"], True, batch_major_out=True)

    grp = lambda z, nb: z.reshape(nb, S5_GROUPS, S5_P)
    return (yp, ys.reshape(bs, 1, D_MODEL), _gla_state_from_stacked(gla_p), gla_s,
            grp(re_p, bp), grp(re_s, bs), grp(im_p, bp), grp(im_s, bs), h_p, h_s,
            _batch_major(rc_p, bp), rc_s,
            jnp.stack([_batch_major(fc0_p, bp), _batch_major(fc1_p, bp)]), jnp.stack([fc0_s, fc1_s]))
```

```python
import functools

import jax
import jax.numpy as jnp
from jax import lax
from jax.experimental import pallas as pl
from jax.experimental.pallas import tpu as pltpu

F32 = jnp.float32
BF16 = jnp.bfloat16

D_MODEL = 1024
N_META = 16
EPS = 1e-6
F32_TINY = 1.1754944e-38
GLA_HEADS = 4
GLA_DK = 64
GLA_DV = 128
GLA_RANK = 16
GLA_TAU = 16.0
GLA_K = GLA_HEADS * GLA_DK
GLA_V = GLA_HEADS * GLA_DV
S5_GROUPS = 32
S5_H = 16
S5_P = 64
S5_W = S5_GROUPS * S5_H
S5_N = S5_GROUPS * S5_P
RNN_W = 1536
RNN_BLOCKS = 16
RNN_BW = RNN_W // RNN_BLOCKS
RNN_C = 8.0
RNN_CONV = 4
D_FF = 2816
FFN_CONV = 3

LANES = 128
FF_CHUNK = 256
L1_SPLIT = 2
S5_PACK = 8
GATE_WIN = 768
GATE_K = 512
GATE_N = 256
VMEM_LIMIT = 56 * 1024 * 1024


def _rms(x, g):
    return x * lax.rsqrt(jnp.mean(x * x, axis=-1, keepdims=True) + EPS) * g


def _sigmoid(x):
    return 0.5 * jnp.tanh(0.5 * x) + 0.5


def _sqrt_nonneg(t):
    return t * lax.rsqrt(jnp.maximum(t, F32_TINY))


def _mm(a, w):
    return jnp.dot(a.astype(BF16), w, preferred_element_type=F32)


def _const_spec(shape, index=None):
    idx = tuple(index) if index is not None else (0,) * len(shape)
    return pl.BlockSpec(shape, lambda i: idx, pipeline_mode=pl.Buffered(1))


def _row_spec(rows, cols):
    return pl.BlockSpec((rows, cols), lambda i: (i, 0))


def _seq_spec(nb, tt, cols):
    return pl.BlockSpec((nb, tt, cols), lambda i: (0, i, 0))


def _tm_scratch(rows, cols):
    return pltpu.VMEM((cols // LANES, rows, LANES), F32)


def _load_time_major(x_ref, tm_sc):
    nb, tt, cols = x_ref.shape
    for b in range(nb):
        for j in range(cols // LANES):
            tm_sc[j, pl.ds(b, tt, stride=nb), :] = x_ref[b, :, j * LANES:(j + 1) * LANES]
    return jnp.concatenate([tm_sc[j] for j in range(cols // LANES)], axis=-1)


def _store_time_major(o_ref, val, tm_sc):
    if tm_sc is None:
        o_ref[...] = val
        return
    nb, tt, cols = o_ref.shape
    for j in range(cols // LANES):
        tm_sc[j] = val[:, j * LANES:(j + 1) * LANES]
    for b in range(nb):
        for j in range(cols // LANES):
            o_ref[b, :, j * LANES:(j + 1) * LANES] = tm_sc[j, pl.ds(b, tt, stride=nb), :]


def _cache_shape(nb, tt, taps, width):
    return (nb, taps, width) if tt == 1 else (taps * nb, width)


def _params(sem="arbitrary"):
    return pltpu.CompilerParams(dimension_semantics=(sem,), vmem_limit_bytes=VMEM_LIMIT)


def _s5_prep_kernel(lr_ref, li_ref, ldt_ref, brt_ref, bit_ref, are_ref, aim_ref, bbre_ref, bbim_ref):
    lr = lr_ref[...]
    li = li_ref[...]
    dt = jnp.exp(ldt_ref[...])
    mag = jnp.exp(lr * dt)
    ab_re = mag * jnp.cos(li * dt)
    ab_im = mag * jnp.sin(li * dt)
    den = lr * lr + li * li
    nr = ab_re - 1.0
    ni = ab_im
    f_re = (nr * lr + ni * li) / den
    f_im = (ni * lr - nr * li) / den
    are_ref[...] = ab_re
    aim_ref[...] = ab_im
    brt = brt_ref[...]
    bit = bit_ref[...]
    bbre_ref[...] = f_re[:, None, :] * brt - f_im[:, None, :] * bit
    bbim_ref[...] = f_re[:, None, :] * bit + f_im[:, None, :] * brt


def _s5_prep(lam_re, lam_im, log_dt, b_re, b_im):
    g, p, h = b_re.shape
    brt = jnp.transpose(b_re, (0, 2, 1))
    bit = jnp.transpose(b_im, (0, 2, 1))
    return pl.pallas_call(
        _s5_prep_kernel,
        out_shape=(jax.ShapeDtypeStruct((g, p), F32), jax.ShapeDtypeStruct((g, p), F32),
                   jax.ShapeDtypeStruct((g, h, p), F32), jax.ShapeDtypeStruct((g, h, p), F32)),
        name="s5_prep",
    )(lam_re, lam_im, log_dt.reshape(g, 1), brt, bit)


def _is_first():
    return pl.program_id(0) == 0


def _side_spec(rows, cols, block=0):
    return pl.BlockSpec((rows, cols), lambda i: (block, 0))


def _l0_in_body(x, wts, outs):
    g_ref, wq_ref, wk_ref, wv_ref, wg_ref, wu_ref, wlr_ref, wal_ref, bal_ref = wts
    q_ref, k_ref, v_ref, gs_ref, la_ref, u_ref = outs
    xn = _rms(x, g_ref[...]).astype(BF16)
    u_ref[...] = _mm(xn, wu_ref[...])
    lr = _mm(xn, wlr_ref[...])
    pre = _mm(lr, wal_ref[...]) + bal_ref[...]
    la_ref[...] = jax.nn.log_sigmoid(pre) * (1.0 / GLA_TAU)
    k_ref[...] = _mm(xn, wk_ref[...])
    q_ref[...] = _mm(xn, wq_ref[...]) * (GLA_DK ** -0.5)
    v_ref[...] = _mm(xn, wv_ref[...])
    g = _mm(xn, wg_ref[...])
    gs_ref[...] = g * _sigmoid(g)


def _l0_in_kernel(*refs):
    x_ref, wts, outs = refs[0], refs[1:1 + L0_IN_WEIGHTS], refs[1 + L0_IN_WEIGHTS:]
    _l0_in_body(x_ref[...], wts, outs)


L0_IN_COLS = (GLA_K, GLA_K, GLA_V, GLA_V, GLA_K, S5_W)
L0_IN_WEIGHTS = 9


def _l0_in_weight_specs():
    return [_const_spec((1, D_MODEL)),
            _const_spec((D_MODEL, GLA_K), (0, 0)), _const_spec((D_MODEL, GLA_K), (0, 1)),
            _const_spec((D_MODEL, GLA_V), (0, 1)), _const_spec((D_MODEL, GLA_V), (0, 2)),
            _const_spec((D_MODEL, S5_W)), _const_spec((D_MODEL, LANES)),
            _const_spec((LANES, GLA_K)), _const_spec((1, GLA_K))]


def _l0_in_weights(w):
    return (w["norm_mix_0"], w["w_in_0"], w["w_in_0"], w["w_in_0"], w["w_in_0"], w["w_u"], w["w_lr"],
            w["w_alpha"], w["b_alpha"])


def _l0_in(x, w):
    n = x.shape[0]
    return pl.pallas_call(
        _l0_in_kernel,
        grid=(1,),
        in_specs=[_row_spec(n, D_MODEL)] + _l0_in_weight_specs(),
        out_specs=[_row_spec(n, c) for c in L0_IN_COLS],
        out_shape=[jax.ShapeDtypeStruct((n, c), F32) for c in L0_IN_COLS],
        compiler_params=_params(),
        name="l0_in",
    )(x, *_l0_in_weights(w))


def _gla_chunk_prep(ins, b_sc, nb, c):
    q_ref, k_ref, _, la_ref, _ = ins
    rows = nb * c
    seq_mask = nb - 1

    def cum_body(t, run):
        rws = pl.ds(pl.multiple_of(t * nb, nb), nb)
        run = run + la_ref[rws, :]
        b_sc[rws, :] = run
        return run

    bl = lax.fori_loop(0, c, cum_body, jnp.zeros((nb, GLA_K), F32), unroll=True)
    b = b_sc[...]
    k = k_ref[...]
    ri = lax.broadcasted_iota(jnp.int32, (rows, rows), 0)
    ci = lax.broadcasted_iota(jnp.int32, (rows, rows), 1)
    xw = nb * GLA_DK
    return dict(
        qt=q_ref[...] * jnp.exp(b), kt=k * jnp.exp(-b),
        kh=k * jnp.exp(jnp.concatenate([bl] * c, axis=0) - b),
        gam=jnp.exp(bl),
        pair_ok=(ri >= ci) & (((ri - ci) & seq_mask) == 0),
        own_blk=(lax.broadcasted_iota(jnp.int32, (rows, xw), 1) // GLA_DK
                 == (lax.broadcasted_iota(jnp.int32, (rows, xw), 0) & seq_mask)),
        own_blk_seq=(lax.broadcasted_iota(jnp.int32, (nb, xw), 1) // GLA_DK
                     == lax.broadcasted_iota(jnp.int32, (nb, xw), 0)),
        reps=xw // LANES)


def _gla_chunk_head(p, h, ins, gn_ref, o_ref, st_ref):
    v_ref, gs_ref = ins[2], ins[4]

    def head_dup(z):
        blk = z[:, (h // 2) * LANES:(h // 2 + 1) * LANES]
        rolled = pltpu.roll(blk, GLA_DK, axis=1)
        low = lax.broadcasted_iota(jnp.int32, blk.shape, 1) < GLA_DK
        return jnp.where(low, blk, rolled) if h % 2 == 0 else jnp.where(low, rolled, blk)

    def expand(zd, own):
        return jnp.where(own, jnp.concatenate([zd] * p["reps"], axis=1), 0.0)

    vs = slice(h * GLA_DV, (h + 1) * GLA_DV)
    qd, ktd, khd = head_dup(p["qt"]), head_dup(p["kt"]), head_dup(p["kh"])
    vb = v_ref[:, vs].astype(BF16)
    st = st_ref[h]
    att = lax.dot_general(qd[:, :GLA_DK].astype(BF16), ktd[:, :GLA_DK].astype(BF16),
                          (((1,), (1,)), ((), ())), preferred_element_type=F32)
    att = jnp.where(p["pair_ok"], att, 0.0).astype(BF16)
    o = jnp.dot(att, vb, preferred_element_type=F32) + lax.dot_general(
        expand(qd, p["own_blk"]).astype(BF16), st.astype(BF16), (((1,), (1,)), ((), ())),
        preferred_element_type=F32)
    upd = lax.dot_general(vb, expand(khd, p["own_blk"]).astype(BF16), (((0,), (0,)), ((), ())),
                          preferred_element_type=F32)
    gam_row = jnp.sum(expand(head_dup(p["gam"]), p["own_blk_seq"]), axis=0, keepdims=True)
    st_ref[h] = st * gam_row + upd
    o_ref[:, vs] = _rms(o, gn_ref[:, vs]) * gs_ref[:, vs]


def _gla_chunk_body(ins, gn_ref, o_ref, st_ref, b_sc, nb, c):
    p = _gla_chunk_prep(ins, b_sc, nb, c)
    for h in range(GLA_HEADS):
        _gla_chunk_head(p, h, ins, gn_ref, o_ref, st_ref)


def _gla_chunk_kernel(*refs, nb, c):
    ins, gn_ref, st0_ref, o_ref, st_ref, b_sc = refs[0:5], refs[5], refs[6], refs[7], refs[8], refs[9]

    @pl.when(_is_first())
    def _():
        st_ref[...] = st0_ref[...]

    _gla_chunk_body(ins, gn_ref, o_ref, st_ref, b_sc, nb, c)


def _gla_chunk(ins, gn, st0, nb, c, row_blocks=None):
    assert nb & (nb - 1) == 0 and (nb * GLA_DK) % LANES == 0
    rows = nb * c
    n = ins[0].shape[0] // row_blocks[0] if row_blocks else ins[0].shape[0]
    first = (row_blocks[1] * n) // rows if row_blocks else 0
    sshape = (GLA_HEADS, GLA_DV, nb * GLA_DK)
    cols = (GLA_K, GLA_K, GLA_V, GLA_K, GLA_V)
    kern = functools.partial(_gla_chunk_kernel, nb=nb, c=c)
    return pl.pallas_call(
        kern,
        grid=(n // rows,),
        in_specs=[pl.BlockSpec((rows, w_), lambda i: (first + i, 0)) for w_ in cols]
                 + [_const_spec((1, GLA_V)), _const_spec(sshape)],
        out_specs=[_row_spec(rows, GLA_V), pl.BlockSpec(sshape, lambda i: (0, 0, 0))],
        out_shape=[jax.ShapeDtypeStruct((n, GLA_V), F32), jax.ShapeDtypeStruct(sshape, F32)],
        scratch_shapes=[pltpu.VMEM((rows, GLA_K), F32)],
        compiler_params=_params(),
        name="gla_chunk",
    )(*ins, gn, st0)


def _gla_state_from_stacked(st):
    nb = st.shape[2] // GLA_DK
    return jnp.transpose(st.reshape(GLA_HEADS, GLA_DV, nb, GLA_DK), (2, 0, 3, 1))


def _gla_step_kernel(q_ref, k_ref, la_ref, v_ref, gs_ref, gn_ref, s0_ref, o_ref, s_ref):
    qT = q_ref[...].T
    kT = k_ref[...].T
    aT = jnp.exp(la_ref[...]).T
    o_rows = []
    for b in range(q_ref.shape[0]):
        s_new = aT[:, b:b + 1] * s0_ref[b] + kT[:, b:b + 1] * v_ref[b:b + 1, :]
        s_ref[b] = s_new
        o_rows.append(jnp.sum(qT[:, b:b + 1] * s_new, axis=0, keepdims=True))
    o = jnp.concatenate(o_rows, axis=0)
    o_ref[...] = _rms(o, gn_ref[...]) * gs_ref[...]


def _gla_step(side, gn, s0, row0):
    nb = s0.shape[0]
    blk = row0 // nb
    q, k, v, la, gs = side
    heads = lambda z: jnp.transpose(z[row0:row0 + nb].reshape(nb, GLA_HEADS, GLA_DK), (1, 0, 2))
    hspec = pl.BlockSpec((None, nb, GLA_DK), lambda h: (h, 0, 0))
    vspec = pl.BlockSpec((nb, GLA_DV), lambda h: (blk, h))
    ospec = pl.BlockSpec((nb, GLA_DV), lambda h: (0, h))
    sspec = pl.BlockSpec((nb, None, GLA_DK, GLA_DV), lambda h: (0, h, 0, 0))
    return pl.pallas_call(
        _gla_step_kernel,
        grid=(GLA_HEADS,),
        in_specs=[hspec, hspec, hspec, vspec, vspec, pl.BlockSpec((1, GLA_DV), lambda h: (0, h)), sspec],
        out_specs=[ospec, sspec],
        out_shape=[jax.ShapeDtypeStruct((nb, GLA_V), F32),
                   jax.ShapeDtypeStruct((nb, GLA_HEADS, GLA_DK, GLA_DV), F32)],
        compiler_params=_params("parallel"),
        name="gla_step",
    )(heads(q), heads(k), heads(la), v, gs, gn, s0)


def _l0_out_body(u_ref, og_ref, x, wts, xo_ref, xre_ref, xim_ref, sr_sc, si_sc, nb, tt):
    npack = S5_GROUPS // S5_PACK
    for j in range(npack):
        _s5_input(u_ref, wts, sr_sc, si_sc, j)
    ys = []
    for j in range(npack):
        _s5_scan(wts, xre_ref, xim_ref, sr_sc, si_sc, j, nb, tt)
        ys.append(_s5_output(wts, sr_sc, si_sc, j))
    _l0_tail_body(_s5_glu(ys, u_ref, wts), og_ref, x, wts, xo_ref)


S5_KIN = S5_PACK * S5_H
S5_KST = S5_PACK * S5_P


def _s5_input(u_ref, wts, sr_sc, si_sc, j):
    bbre_ref, bbim_ref = wts[0:2]
    cs = slice(j * S5_KST, (j + 1) * S5_KST)
    uj = u_ref[:, j * S5_KIN:(j + 1) * S5_KIN].astype(BF16)
    sr_sc[:, cs] = jnp.dot(uj, bbre_ref[j], preferred_element_type=F32)
    si_sc[:, cs] = jnp.dot(uj, bbim_ref[j], preferred_element_type=F32)


def _s5_scan(wts, xre_ref, xim_ref, sr_sc, si_sc, j, nb, tt):
    are_ref, aim_ref = wts[2:4]
    cs = slice(j * S5_KST, (j + 1) * S5_KST)
    ar = jnp.broadcast_to(are_ref[:, cs], (nb, S5_KST))
    ai = jnp.broadcast_to(aim_ref[:, cs], (nb, S5_KST))
    xr = xre_ref[:, cs]
    xi = xim_ref[:, cs]
    for t in range(tt):
        rows = slice(t * nb, (t + 1) * nb)
        xr, xi = ar * xr - ai * xi + sr_sc[rows, cs], ar * xi + ai * xr + si_sc[rows, cs]
        sr_sc[rows, cs] = xr
        si_sc[rows, cs] = xi
    xre_ref[:, cs] = xr
    xim_ref[:, cs] = xi


def _s5_output(wts, sr_sc, si_sc, j):
    cre_ref, cim_ref = wts[4:6]
    cs = slice(j * S5_KST, (j + 1) * S5_KST)
    return (jnp.dot(sr_sc[:, cs].astype(BF16), cre_ref[j], preferred_element_type=F32)
            - jnp.dot(si_sc[:, cs].astype(BF16), cim_ref[j], preferred_element_type=F32))


def _s5_glu(ys, u_ref, wts):
    d_ref, wglu_ref, bglu_ref = wts[6:9]
    y = jax.nn.gelu(jnp.concatenate(ys, axis=-1) + d_ref[...] * u_ref[...])
    return y * _sigmoid(_mm(y, wglu_ref[...]) + bglu_ref[...])


def _l0_tail_body(y, og_ref, x, wts, xo_ref):
    woa_ref, wob_ref = wts[9:11]
    xo_ref[...] = x + _mm(og_ref[...], woa_ref[...]) + _mm(y, wob_ref[...])


L0_OUT_WEIGHTS = 11


def _l0_out_weight_specs():
    npack = S5_GROUPS // S5_PACK
    kin, kst = S5_PACK * S5_H, S5_PACK * S5_P
    return [_const_spec((npack, kin, kst)), _const_spec((npack, kin, kst)),
            _const_spec((1, S5_N)), _const_spec((1, S5_N)),
            _const_spec((npack, kst, kin)), _const_spec((npack, kst, kin)),
            _const_spec((1, S5_W)), _const_spec((S5_W, S5_W)), _const_spec((1, S5_W)),
            _const_spec((GLA_V, D_MODEL), (0, 0)), _const_spec((S5_W, D_MODEL), (1, 0))]


def _l0_out_weights(w):
    return (w["s5_bbre"], w["s5_bbim"], w["s5_are"], w["s5_aim"], w["s5_cre"], w["s5_cim"],
            w["s5_d"], w["s5_w_glu"], w["s5_b_glu"], w["w_out_0"], w["w_out_0"])


def _l0_out_side_kernel(*refs, nb, tt_meta, nb_s):
    um_ref, ogm_ref, xm_ref, us_ref, ogs_ref, xs_ref, xr0s_ref, xi0s_ref = refs[0:8]
    wts = refs[8:8 + L0_OUT_WEIGHTS]
    xom_ref, xre_ref, xim_ref, xos_ref, xres_ref, xims_ref, sr_sc, si_sc = refs[8 + L0_OUT_WEIGHTS:]
    xre_ref[...] = jnp.zeros(xre_ref.shape, F32)
    xim_ref[...] = jnp.zeros(xim_ref.shape, F32)
    _l0_out_body(um_ref, ogm_ref, xm_ref[...], wts, xom_ref, xre_ref, xim_ref, sr_sc, si_sc, nb, tt_meta)
    xres_ref[...] = xr0s_ref[...]
    xims_ref[...] = xi0s_ref[...]
    _l0_out_body(us_ref, ogs_ref, xs_ref[...], wts, xos_ref, xres_ref, xims_ref, sr_sc, si_sc, nb_s, 1)


def _l0_out_side(u_side, og_meta, og_samp, x_side, xr0_s, xi0_s, w, nb, tt_meta):
    mrows, nb_s = nb * tt_meta, xr0_s.shape[0]
    assert mrows == nb_s
    kern = functools.partial(_l0_out_side_kernel, nb=nb, tt_meta=tt_meta, nb_s=nb_s)
    f32 = lambda *shape: jax.ShapeDtypeStruct(shape, F32)
    full = lambda r, c: pl.BlockSpec((r, c), lambda i: (0, 0))
    return pl.pallas_call(
        kern,
        grid=(1,),
        in_specs=[_side_spec(mrows, S5_W, 0), full(mrows, GLA_V), _side_spec(mrows, D_MODEL, 0),
                  _side_spec(nb_s, S5_W, 1), full(nb_s, GLA_V), _side_spec(nb_s, D_MODEL, 1),
                  full(nb_s, S5_N), full(nb_s, S5_N)] + _l0_out_weight_specs(),
        out_specs=[full(mrows, D_MODEL), full(nb, S5_N), full(nb, S5_N), full(nb_s, D_MODEL),
                   full(nb_s, S5_N), full(nb_s, S5_N)],
        out_shape=[f32(mrows, D_MODEL), f32(nb, S5_N), f32(nb, S5_N), f32(nb_s, D_MODEL),
                   f32(nb_s, S5_N), f32(nb_s, S5_N)],
        scratch_shapes=[pltpu.VMEM((mrows, S5_N), F32)] * 2,
        compiler_params=_params(),
        name="l0_out_side",
    )(u_side, og_meta, x_side, u_side, og_samp, x_side, xr0_s, xi0_s, *_l0_out_weights(w))


def _l0_kernel(*refs, nb, tt, c):
    x_ref, st0_ref, xr0_ref, xi0_ref = refs[0:4]
    n_in = 4 + L0_IN_WEIGHTS
    in_wts, gn_ref, out_wts = refs[4:n_in], refs[n_in], refs[n_in + 1:n_in + 1 + L0_OUT_WEIGHTS]
    rest = refs[n_in + 1 + L0_OUT_WEIGHTS:]
    xo_ref, st_ref, xre_ref, xim_ref = rest[0:4]
    proj_sc, og_sc, b_sc, sr_sc, si_sc, tm_sc = rest[4:10], rest[10], rest[11], rest[12], rest[13], rest[14]

    @pl.when(_is_first())
    def _():
        st_ref[...] = st0_ref[...]
        xre_ref[...] = xr0_ref[...]
        xim_ref[...] = xi0_ref[...]

    x = _load_time_major(x_ref, tm_sc)
    _l0_in_body(x, in_wts, proj_sc)
    q_sc, k_sc, v_sc, gs_sc, la_sc, u_sc = proj_sc
    npack = S5_GROUPS // S5_PACK
    nchunk = tt // c
    units = [(j, hp) for j in range(nchunk) for hp in range(GLA_HEADS // 2)]
    assert len(units) == npack
    for j in range(npack):
        _s5_input(u_sc, out_wts, sr_sc, si_sc, j)
    ys, prep = [], {}
    for i, (j, hp) in enumerate(units):
        _s5_scan(out_wts, xre_ref, xim_ref, sr_sc, si_sc, i, nb, tt)
        rws = pl.ds(j * nb * c, nb * c)
        ins = [r.at[rws] for r in (q_sc, k_sc, v_sc, la_sc, gs_sc)]
        if hp == 0:
            prep = _gla_chunk_prep(ins, b_sc, nb, c)
        for h in (2 * hp, 2 * hp + 1):
            _gla_chunk_head(prep, h, ins, gn_ref, og_sc.at[rws], st_ref)
        ys.append(_s5_output(out_wts, sr_sc, si_sc, i))
    _l0_tail_body(_s5_glu(ys, u_sc, out_wts), og_sc, x, out_wts, xo_ref)


def _l0(x, st0, xr0, xi0, w, tt, c):
    nb, nt, _ = x.shape
    n, rows = nb * nt, nb * tt
    assert nb & (nb - 1) == 0 and (nb * GLA_DK) % LANES == 0 and tt % c == 0
    sshape = (GLA_HEADS, GLA_DV, nb * GLA_DK)
    kern = functools.partial(_l0_kernel, nb=nb, tt=tt, c=c)
    st_spec = pl.BlockSpec((nb, S5_N), lambda i: (0, 0))
    f32 = lambda *shape: jax.ShapeDtypeStruct(shape, F32)
    return pl.pallas_call(
        kern,
        grid=(n // rows,),
        in_specs=[_seq_spec(nb, tt, D_MODEL), _const_spec(sshape), _const_spec((nb, S5_N)),
                  _const_spec((nb, S5_N))] + _l0_in_weight_specs() + [_const_spec((1, GLA_V))]
                 + _l0_out_weight_specs(),
        out_specs=[_row_spec(rows, D_MODEL), pl.BlockSpec(sshape, lambda i: (0, 0, 0)), st_spec, st_spec],
        out_shape=[f32(n, D_MODEL), f32(*sshape), f32(nb, S5_N), f32(nb, S5_N)],
        scratch_shapes=[pltpu.VMEM((rows, cols), F32) for cols in L0_IN_COLS]
                       + [pltpu.VMEM((rows, GLA_V), F32), pltpu.VMEM((nb * c, GLA_K), F32),
                          pltpu.VMEM((rows, S5_N), F32), pltpu.VMEM((rows, S5_N), F32),
                          _tm_scratch(rows, D_MODEL)],
        compiler_params=_params(),
        name="l0_mixer",
    )(x, st0, xr0, xi0, *_l0_in_weights(w), w["gla_norm"], *_l0_out_weights(w))


def _ffn_body(x, wts, c0_ref, c_ref, hm_sc, nb, tt, final):
    g_ref, wg_ref, wv_ref, cw_ref, cb_ref, wd_ref, gf_ref = wts
    rows = nb * tt
    xn = _rms(x, g_ref[...]).astype(BF16)
    for ci in range(D_FF // FF_CHUNK):
        cs = slice(ci * FF_CHUNK, (ci + 1) * FF_CHUNK)
        gate = jnp.dot(xn, wg_ref[:, cs], preferred_element_type=F32)
        val = jnp.dot(xn, wv_ref[:, cs], preferred_element_type=F32)
        if tt == 1:
            taps = [c0_ref[:, j, cs] for j in range(FFN_CONV - 1)] + [gate]
            for j in range(FFN_CONV - 1):
                c_ref[:, j, cs] = taps[j + 1]
        else:
            ext = jnp.concatenate([c_ref[:, cs], gate], axis=0)
            taps = [ext[j * nb:j * nb + rows] for j in range(FFN_CONV)]
            c_ref[:, cs] = ext[tt * nb:(tt + FFN_CONV - 1) * nb]
        y = cb_ref[:, cs] + taps[0] * cw_ref[0:1, cs]
        for j in range(1, FFN_CONV):
            y = y + taps[j] * cw_ref[j:j + 1, cs]
        hm_sc[:, cs] = (jax.nn.gelu(y) * val).astype(BF16)
    out = x + jnp.dot(hm_sc[...], wd_ref[...], preferred_element_type=F32)
    return _rms(out, gf_ref[...]) if final else out


FFN_WEIGHTS = 7


def _ffn_weight_specs(layer):
    return [_const_spec((None, 1, D_MODEL), (layer, 0, 0)),
            _const_spec((None, D_MODEL, D_FF), (layer, 0, 0)),
            _const_spec((None, D_MODEL, D_FF), (layer, 0, 1)),
            _const_spec((None, FFN_CONV, D_FF), (layer, 0, 0)),
            _const_spec((None, 1, D_FF), (layer, 0, 0)),
            _const_spec((None, D_FF, D_MODEL), (layer, 0, 0)), _const_spec((1, D_MODEL))]


def _ffn_weights(w):
    return (w["norm_ffn"], w["ffn_w_up"], w["ffn_w_up"], w["ffn_conv_w"], w["ffn_conv_b"], w["ffn_w_down"],
            w["norm_final"])


def _ffn_kernel(*refs, nb, tt, final, batch_major_out):
    x_ref, c0_ref = refs[0:2]
    wts = refs[2:2 + FFN_WEIGHTS]
    xo_ref, c_ref, hm_sc = refs[2 + FFN_WEIGHTS:5 + FFN_WEIGHTS]
    tm_sc = refs[5 + FFN_WEIGHTS] if batch_major_out else None

    @pl.when(_is_first())
    def _():
        c_ref[...] = c0_ref[...]

    _store_time_major(xo_ref, _ffn_body(x_ref[...], wts, None, c_ref, hm_sc, nb, tt, final), tm_sc)


def _ffn(x, c0, w, layer, nb, tt, final, batch_major_out=False):
    n = x.shape[0]
    rows = nb * tt
    kern = functools.partial(_ffn_kernel, nb=nb, tt=tt, final=final, batch_major_out=batch_major_out)
    if batch_major_out:
        o_spec, o_shape = _seq_spec(nb, tt, D_MODEL), (nb, n // nb, D_MODEL)
    else:
        o_spec, o_shape = _row_spec(rows, D_MODEL), (n, D_MODEL)
    cshape = _cache_shape(nb, tt, FFN_CONV - 1, D_FF)
    f32 = lambda *shape: jax.ShapeDtypeStruct(shape, F32)
    return pl.pallas_call(
        kern,
        grid=(n // rows,),
        in_specs=[_row_spec(rows, D_MODEL), _const_spec(cshape)] + _ffn_weight_specs(layer),
        out_specs=[o_spec, pl.BlockSpec(cshape, lambda i: (0, 0))],
        out_shape=[f32(*o_shape), f32(*cshape)],
        scratch_shapes=[pltpu.VMEM((rows, D_FF), BF16)]
                       + ([_tm_scratch(rows, D_MODEL)] if batch_major_out else []),
        compiler_params=_params(),
        name="ffn%d" % layer,
    )(x, c0, *_ffn_weights(w))


def _ffn_side_kernel(*refs, nb, tt_meta, nb_s, final):
    xm_ref, xs_ref, c0s_ref = refs[0:3]
    wts = refs[3:3 + FFN_WEIGHTS]
    xom_ref, c_ref, xos_ref, cs_ref, hm_sc = refs[3 + FFN_WEIGHTS:]
    c_ref[...] = jnp.zeros(c_ref.shape, F32)
    xom_ref[...] = _ffn_body(xm_ref[...], wts, None, c_ref, hm_sc, nb, tt_meta, final)
    xos_ref[...] = _ffn_body(xs_ref[...], wts, c0s_ref, cs_ref, hm_sc, nb_s, 1, final)


def _ffn_side(x_meta, x_samp, c0_s, w, layer, nb, tt_meta, final):
    mrows, nb_s = nb * tt_meta, x_samp.shape[0]
    assert mrows == nb_s
    kern = functools.partial(_ffn_side_kernel, nb=nb, tt_meta=tt_meta, nb_s=nb_s, final=final)
    cshape = _cache_shape(nb, tt_meta, FFN_CONV - 1, D_FF)
    cs_shape = _cache_shape(nb_s, 1, FFN_CONV - 1, D_FF)
    f32 = lambda *shape: jax.ShapeDtypeStruct(shape, F32)
    full = lambda shape: pl.BlockSpec(shape, lambda i: (0,) * len(shape))
    return pl.pallas_call(
        kern,
        grid=(1,),
        in_specs=[full((mrows, D_MODEL)), full((nb_s, D_MODEL)),
                  pl.BlockSpec((None,) + cs_shape, lambda i: (layer, 0, 0, 0))] + _ffn_weight_specs(layer),
        out_specs=[full((mrows, D_MODEL)), full(cshape), full((nb_s, D_MODEL)), full(cs_shape)],
        out_shape=[f32(mrows, D_MODEL), f32(*cshape), f32(nb_s, D_MODEL), f32(*cs_shape)],
        scratch_shapes=[pltpu.VMEM((mrows, D_FF), BF16)],
        compiler_params=_params(),
        name="ffn%d_side" % layer,
    )(x_meta, x_samp, c0_s, *_ffn_weights(w))


def _l1_body(x_ref, wts, xo_ref, h_ref, c0_ref, c_ref, a_sc, b_sc, nb, tt):
    g_ref, wgt_ref, wxr_ref, cw_ref, cb_ref, wa_ref, ba_ref, wx_ref, bx_ref, lam_ref, wo_ref = wts
    sp = jax.nn.softplus(-lam_ref[...])
    nsplit = L1_SPLIT if tt % L1_SPLIT == 0 else 1
    th = tt // nsplit
    rows = nb * th
    carry = c_ref[...] if tt > 1 else None
    xs, xns, ggs = [], [], []

    for part in range(nsplit):
        prow = slice(part * rows, (part + 1) * rows)
        x = x_ref[prow, :]
        xn = _rms(x, g_ref[...]).astype(BF16)
        xs.append(x)
        xns.append(xn)
        xr = jnp.dot(xn, wxr_ref[...], preferred_element_type=F32)
        if tt == 1:
            taps = [c0_ref[:, j, :] for j in range(RNN_CONV - 1)] + [xr]
            for j in range(RNN_CONV - 1):
                c_ref[:, j, :] = taps[j + 1]
        else:
            ext = jnp.concatenate([carry, xr], axis=0)
            taps = [ext[j * nb:j * nb + rows] for j in range(RNN_CONV)]
            carry = ext[th * nb:(th + RNN_CONV - 1) * nb]
        xc = cb_ref[...] + taps[0] * cw_ref[0:1, :]
        for j in range(1, RNN_CONV):
            xc = xc + taps[j] * cw_ref[j:j + 1, :]

        xcb = xc.astype(BF16)
        rs, gs = [], []
        for wi in range(RNN_W // GATE_WIN):
            for ni in range(GATE_WIN // GATE_N):
                k0 = wi * GATE_WIN + ni * LANES
                lhs = xcb[:, k0:k0 + GATE_K]
                rs.append(jnp.dot(lhs, wa_ref[wi, ni], preferred_element_type=F32))
                gs.append(jnp.dot(lhs, wx_ref[wi, ni], preferred_element_type=F32))
        r = _sigmoid(jnp.concatenate(rs, axis=-1) + ba_ref[...])
        ig = _sigmoid(jnp.concatenate(gs, axis=-1) + bx_ref[...])
        log_a = (-RNN_C) * r * sp
        a = jnp.exp(log_a)
        a_sc[prow, :] = a
        b_sc[prow, :] = _sqrt_nonneg(jnp.tanh(-log_a) * (a * a + 1.0)) * (ig * xc)
    if tt > 1:
        c_ref[...] = carry

    for part in range(nsplit):
        ggs.append(jax.nn.gelu(jnp.dot(xns[part], wgt_ref[...], preferred_element_type=F32)))

    h = h_ref[...]
    for part in range(nsplit):
        prow = slice(part * rows, (part + 1) * rows)
        for t in range(part * th, (part + 1) * th):
            rws = slice(t * nb, (t + 1) * nb)
            h = a_sc[rws, :] * h + b_sc[rws, :]
            b_sc[rws, :] = h
        xo_ref[prow, :] = xs[part] + _mm(b_sc[prow, :] * ggs[part], wo_ref[...])
    h_ref[...] = h


L1_WEIGHTS = 11


def _l1_weight_specs():
    gshape = (RNN_W // GATE_WIN, GATE_WIN // GATE_N, GATE_K, GATE_N)
    return [_const_spec((1, D_MODEL)),
            _const_spec((D_MODEL, RNN_W), (0, 0)), _const_spec((D_MODEL, RNN_W), (0, 1)),
            _const_spec((RNN_CONV, RNN_W)), _const_spec((1, RNN_W)),
            _const_spec(gshape), _const_spec((1, RNN_W)),
            _const_spec(gshape), _const_spec((1, RNN_W)),
            _const_spec((1, RNN_W)), _const_spec((RNN_W, D_MODEL))]


def _l1_weights(w):
    return (w["norm_mix_1"], w["w_in_1"], w["w_in_1"], w["rnn_conv_w"], w["rnn_conv_b"], w["rnn_wa"],
            w["rnn_b_a"], w["rnn_wx"], w["rnn_b_x"], w["rnn_lam"], w["w_out_1"])


def _l1_kernel(*refs, nb, tt):
    x_ref, h0_ref, c0_ref = refs[0:3]
    wts = refs[3:3 + L1_WEIGHTS]
    xo_ref, h_ref, c_ref, a_sc, b_sc = refs[3 + L1_WEIGHTS:]

    @pl.when(_is_first())
    def _():
        h_ref[...] = h0_ref[...]
        c_ref[...] = c0_ref[...]

    _l1_body(x_ref, wts, xo_ref, h_ref, None, c_ref, a_sc, b_sc, nb, tt)


def _l1(x, h0, c0, w, nb, tt):
    n = x.shape[0]
    rows = nb * tt
    kern = functools.partial(_l1_kernel, nb=nb, tt=tt)
    cshape = _cache_shape(nb, tt, RNN_CONV - 1, RNN_W)
    f32 = lambda *shape: jax.ShapeDtypeStruct(shape, F32)
    return pl.pallas_call(
        kern,
        grid=(n // rows,),
        in_specs=[_row_spec(rows, D_MODEL), _const_spec((nb, RNN_W)), _const_spec(cshape)] + _l1_weight_specs(),
        out_specs=[_row_spec(rows, D_MODEL), pl.BlockSpec((nb, RNN_W), lambda i: (0, 0)),
                   pl.BlockSpec(cshape, lambda i: (0, 0))],
        out_shape=[f32(n, D_MODEL), f32(nb, RNN_W), f32(*cshape)],
        scratch_shapes=[pltpu.VMEM((rows, RNN_W), F32)] * 2,
        compiler_params=_params(),
        name="l1_mixer",
    )(x, h0, c0, *_l1_weights(w))


def _l1_side_kernel(*refs, nb, tt_meta, nb_s):
    xm_ref, xs_ref, h0s_ref, c0s_ref = refs[0:4]
    wts = refs[4:4 + L1_WEIGHTS]
    xom_ref, h_ref, c_ref, xos_ref, hs_ref, cs_ref, a_sc, b_sc = refs[4 + L1_WEIGHTS:]
    h_ref[...] = jnp.zeros(h_ref.shape, F32)
    c_ref[...] = jnp.zeros(c_ref.shape, F32)
    _l1_body(xm_ref, wts, xom_ref, h_ref, None, c_ref, a_sc, b_sc, nb, tt_meta)
    hs_ref[...] = h0s_ref[...]
    _l1_body(xs_ref, wts, xos_ref, hs_ref, c0s_ref, cs_ref, a_sc, b_sc, nb_s, 1)


def _l1_side(x_meta, x_samp, h0_s, c0_s, w, nb, tt_meta):
    mrows, nb_s = nb * tt_meta, x_samp.shape[0]
    assert mrows == nb_s
    kern = functools.partial(_l1_side_kernel, nb=nb, tt_meta=tt_meta, nb_s=nb_s)
    cshape = _cache_shape(nb, tt_meta, RNN_CONV - 1, RNN_W)
    cs_shape = _cache_shape(nb_s, 1, RNN_CONV - 1, RNN_W)
    f32 = lambda *shape: jax.ShapeDtypeStruct(shape, F32)
    full = lambda shape: pl.BlockSpec(shape, lambda i: (0,) * len(shape))
    return pl.pallas_call(
        kern,
        grid=(1,),
        in_specs=[full((mrows, D_MODEL)), full((nb_s, D_MODEL)), full((nb_s, RNN_W)), full(cs_shape)]
                 + _l1_weight_specs(),
        out_specs=[full((mrows, D_MODEL)), full((nb, RNN_W)), full(cshape), full((nb_s, D_MODEL)),
                   full((nb_s, RNN_W)), full(cs_shape)],
        out_shape=[f32(mrows, D_MODEL), f32(nb, RNN_W), f32(*cshape), f32(nb_s, D_MODEL), f32(nb_s, RNN_W),
                   f32(*cs_shape)],
        scratch_shapes=[pltpu.VMEM((mrows, RNN_W), F32)] * 2,
        compiler_params=_params(),
        name="l1_side",
    )(x_meta, x_samp, h0_s, c0_s, *_l1_weights(w))


def _pack_gate_kernel(wa_ref, wx_ref, oa_ref, ox_ref):
    tiles_per_win = GATE_WIN // GATE_N
    for w_ref, o_ref in ((wa_ref, oa_ref), (wx_ref, ox_ref)):
        o_ref[...] = jnp.zeros(o_ref.shape, o_ref.dtype)
        for n in range(RNN_BLOCKS):
            pos = n * RNN_BW
            wi = pos // GATE_WIN
            for ni in range(tiles_per_win):
                k0 = wi * GATE_WIN + ni * LANES
                n0 = wi * GATE_WIN + ni * GATE_N
                lo, hi = max(pos, n0), min(pos + RNN_BW, n0 + GATE_N)
                if lo < hi:
                    o_ref[wi, ni, pos - k0:pos - k0 + RNN_BW, lo - n0:hi - n0] = (
                        w_ref[n][:, lo - pos:hi - pos].astype(o_ref.dtype))


def _pack_gates(wa, wx):
    gshape = (RNN_W // GATE_WIN, GATE_WIN // GATE_N, GATE_K, GATE_N)
    return pl.pallas_call(
        _pack_gate_kernel,
        out_shape=(jax.ShapeDtypeStruct(gshape, BF16), jax.ShapeDtypeStruct(gshape, BF16)),
        name="pack_gates",
    )(wa, wx)


def _prep_weights(p):
    w = {}
    row = lambda v: v.reshape(1, -1).astype(F32)
    w_in = p["w_in_0"]
    c = 2 * GLA_K + 2 * GLA_V
    w["w_in_0"] = w_in.astype(BF16)
    w["w_lr"] = jnp.pad(w_in[:, c:c + GLA_RANK], ((0, 0), (0, LANES - GLA_RANK))).astype(BF16)
    c += GLA_RANK
    w["w_u"] = w_in[:, c:c + S5_W].astype(BF16)
    w["w_alpha"] = jnp.pad(p["w_alpha_0"], ((0, LANES - GLA_RANK), (0, 0))).astype(BF16)
    w["b_alpha"] = row(p["b_alpha_0"])
    w["norm_mix_0"] = row(p["norm_mix_0"])
    w["gla_norm"] = row(p["gla_norm_0"])

    are, aim, bbre, bbim = _s5_prep(p["s5_lam_re"], p["s5_lam_im"], p["s5_log_dt"], p["s5_b_re"],
                                    p["s5_b_im"])
    npack = S5_GROUPS // S5_PACK
    eye = jnp.eye(S5_PACK, dtype=F32)[None, :, None, :, None]
    grouped = lambda m: m.reshape(npack, S5_PACK, S5_H, S5_P)
    pack_b = lambda m: (grouped(m)[:, :, :, None, :] * eye).reshape(
        npack, S5_PACK * S5_H, S5_PACK * S5_P).astype(BF16)
    pack_c = lambda m: (jnp.swapaxes(grouped(m), 2, 3)[:, :, :, None, :] * eye).reshape(
        npack, S5_PACK * S5_P, S5_PACK * S5_H).astype(BF16)
    w["s5_are"] = are.reshape(1, S5_N)
    w["s5_aim"] = aim.reshape(1, S5_N)
    w["s5_bbre"] = pack_b(bbre)
    w["s5_bbim"] = pack_b(bbim)
    w["s5_cre"] = pack_c(p["s5_c_re"])
    w["s5_cim"] = pack_c(p["s5_c_im"])
    w["s5_d"] = row(p["s5_d"])
    w["s5_w_glu"] = p["s5_w_glu"].astype(BF16)
    w["s5_b_glu"] = row(p["s5_b_glu"])
    w["w_out_0"] = p["w_out_0"].astype(BF16)

    w["norm_mix_1"] = row(p["norm_mix_1"])
    w["w_in_1"] = p["w_in_1"].astype(BF16)
    w["rnn_conv_w"] = p["rnn_conv_w"].astype(F32)
    w["rnn_conv_b"] = row(p["rnn_conv_b"])
    w["rnn_wa"], w["rnn_wx"] = _pack_gates(p["rnn_w_a"], p["rnn_w_x"])
    w["rnn_b_a"] = row(p["rnn_b_a"])
    w["rnn_b_x"] = row(p["rnn_b_x"])
    w["rnn_lam"] = row(p["rnn_lam"])
    w["w_out_1"] = p["w_out_1"].astype(BF16)

    depth = p["norm_ffn"].shape[0]
    w["norm_ffn"] = p["norm_ffn"].reshape(depth, 1, D_MODEL)
    w["ffn_w_up"] = p["ffn_w_up"].astype(BF16)
    w["ffn_conv_w"] = p["ffn_conv_w"]
    w["ffn_conv_b"] = p["ffn_conv_b"].reshape(depth, 1, D_FF)
    w["ffn_w_down"] = p["ffn_w_down"].astype(BF16)
    w["norm_final"] = row(p["norm_final"])
    return w


def _tile_steps():
    return dict(l0=64, gla=32, ffn=64, l1=64)


def _batch_major(cache, nb):
    jb, c = cache.shape
    return jnp.transpose(cache.reshape(jb // nb, nb, c), (1, 0, 2))


def kernel(x_prompt, x_sample, state_gla, state_s5_re, state_s5_im, state_rglru, cache_rglru_conv,
           cache_ffn_conv, meta_tokens, norm_mix_0, w_in_0, w_alpha_0, b_alpha_0, gla_norm_0,
           s5_lam_re, s5_lam_im, s5_log_dt, s5_b_re, s5_b_im, s5_c_re, s5_c_im, s5_d, s5_w_glu,
           s5_b_glu, w_out_0, norm_mix_1, w_in_1, rnn_conv_w, rnn_conv_b, rnn_w_a, rnn_b_a, rnn_w_x,
           rnn_b_x, rnn_lam, w_out_1, norm_ffn, ffn_w_up, ffn_conv_w, ffn_conv_b, ffn_w_down, norm_final):
    w = _prep_weights(dict(
        norm_mix_0=norm_mix_0, w_in_0=w_in_0, w_alpha_0=w_alpha_0, b_alpha_0=b_alpha_0,
        gla_norm_0=gla_norm_0, s5_lam_re=s5_lam_re, s5_lam_im=s5_lam_im, s5_log_dt=s5_log_dt,
        s5_b_re=s5_b_re, s5_b_im=s5_b_im, s5_c_re=s5_c_re, s5_c_im=s5_c_im, s5_d=s5_d,
        s5_w_glu=s5_w_glu, s5_b_glu=s5_b_glu, w_out_0=w_out_0, norm_mix_1=norm_mix_1, w_in_1=w_in_1,
        rnn_conv_w=rnn_conv_w, rnn_conv_b=rnn_conv_b, rnn_w_a=rnn_w_a, rnn_b_a=rnn_b_a,
        rnn_w_x=rnn_w_x, rnn_b_x=rnn_b_x, rnn_lam=rnn_lam, w_out_1=w_out_1, norm_ffn=norm_ffn,
        ffn_w_up=ffn_w_up, ffn_conv_w=ffn_conv_w, ffn_conv_b=ffn_conv_b, ffn_w_down=ffn_w_down,
        norm_final=norm_final))

    bp = x_prompt.shape[0]
    bs = x_sample.shape[0]
    tt = _tile_steps()
    mrows = bp * N_META
    assert mrows == bs

    x_side = jnp.concatenate([jnp.repeat(meta_tokens.astype(F32), bp, axis=0),
                              x_sample.reshape(bs, D_MODEL)], axis=0)
    s5_re_s = state_s5_re.reshape(bs, S5_N)
    s5_im_s = state_s5_im.reshape(bs, S5_N)

    gla_in = lambda z: (z[0], z[1], z[2], z[4], z[3])
    gn = w["gla_norm"]

    side = _l0_in(x_side, w)
    og_m, gla_m = _gla_chunk(gla_in(side), gn, jnp.zeros((GLA_HEADS, GLA_DV, bp * GLA_DK), F32), bp, N_META,
                             row_blocks=(2, 0))
    og_s, gla_s = _gla_step(gla_in(side), gn, state_gla, mrows)
    x_m, re_m, im_m, x_s, re_s, im_s = _l0_out_side(side[5], og_m, og_s, x_side, s5_re_s, s5_im_s, w, bp, N_META)
    x_m, fc0_m, x_s, fc0_s = _ffn_side(x_m, x_s, cache_ffn_conv, w, 0, bp, N_META, False)
    x_m, h_m, rc_m, x_s, h_s, rc_s = _l1_side(x_m, x_s, state_rglru, cache_rglru_conv, w, bp, N_META)
    _, fc1_m, ys, fc1_s = _ffn_side(x_m, x_s, cache_ffn_conv, w, 1, bp, N_META, True)

    x, gla_p, re_p, im_p = _l0(x_prompt, gla_m, re_m, im_m, w, tt["l0"], tt["gla"])
    x, fc0_p = _ffn(x, fc0_m, w, 0, bp, tt["ffn"], False)
    x, h_p, rc_p = _l1(x, h_m, rc_m, w, bp, tt["l1"])
    yp, fc1_p = _ffn(x, fc1_m, w, 1, bp, tt["ffn"], True, batch_major_out=True)

    grp = lambda z, nb: z.reshape(nb, S5_GROUPS, S5_P)
    return (yp, ys.reshape(bs, 1, D_MODEL), _gla_state_from_stacked(gla_p), gla_s,
            grp(re_p, bp), grp(re_s, bs), grp(im_p, bp), grp(im_s, bs), h_p, h_s,
            _batch_major(rc_p, bp), rc_s,
            jnp.stack([_batch_major(fc0_p, bp), _batch_major(fc1_p, bp)]), jnp.stack([fc0_s, fc1_s]))
```

```python
import functools

import jax
import jax.numpy as jnp
from jax import lax
from jax.experimental import pallas as pl
from jax.experimental.pallas import tpu as pltpu

F32 = jnp.float32
BF16 = jnp.bfloat16

D_MODEL = 1024
N_META = 16
EPS = 1e-6
F32_TINY = 1.1754944e-38
GLA_HEADS = 4
GLA_DK = 64
GLA_DV = 128
GLA_RANK = 16
GLA_TAU = 16.0
GLA_K = GLA_HEADS * GLA_DK
GLA_V = GLA_HEADS * GLA_DV
S5_GROUPS = 32
S5_H = 16
S5_P = 64
S5_W = S5_GROUPS * S5_H
S5_N = S5_GROUPS * S5_P
RNN_W = 1536
RNN_BLOCKS = 16
RNN_BW = RNN_W // RNN_BLOCKS
RNN_C = 8.0
RNN_CONV = 4
D_FF = 2816
FFN_CONV = 3

LANES = 128
FF_CHUNK = 256
L1_SPLIT = 2
S5_PACK = 8
GATE_WIN = 768
GATE_K = 512
GATE_N = 256
VMEM_LIMIT = 56 * 1024 * 1024


def _rms(x, g):
    return x * lax.rsqrt(jnp.mean(x * x, axis=-1, keepdims=True) + EPS) * g


def _sigmoid(x):
    return 0.5 * jnp.tanh(0.5 * x) + 0.5


def _sqrt_nonneg(t):
    return t * lax.rsqrt(jnp.maximum(t, F32_TINY))


def _mm(a, w):
    return jnp.dot(a.astype(BF16), w, preferred_element_type=F32)


def _const_spec(shape, index=None):
    idx = tuple(index) if index is not None else (0,) * len(shape)
    return pl.BlockSpec(shape, lambda i: idx, pipeline_mode=pl.Buffered(1))


def _row_spec(rows, cols):
    return pl.BlockSpec((rows, cols), lambda i: (i, 0))


def _seq_spec(nb, tt, cols):
    return pl.BlockSpec((nb, tt, cols), lambda i: (0, i, 0))


def _tm_scratch(rows, cols):
    return pltpu.VMEM((cols // LANES, rows, LANES), F32)


def _load_time_major(x_ref, tm_sc):
    nb, tt, cols = x_ref.shape
    for b in range(nb):
        for j in range(cols // LANES):
            tm_sc[j, pl.ds(b, tt, stride=nb), :] = x_ref[b, :, j * LANES:(j + 1) * LANES]
    return jnp.concatenate([tm_sc[j] for j in range(cols // LANES)], axis=-1)


def _store_time_major(o_ref, val, tm_sc):
    if tm_sc is None:
        o_ref[...] = val
        return
    nb, tt, cols = o_ref.shape
    for j in range(cols // LANES):
        tm_sc[j] = val[:, j * LANES:(j + 1) * LANES]
    for b in range(nb):
        for j in range(cols // LANES):
            o_ref[b, :, j * LANES:(j + 1) * LANES] = tm_sc[j, pl.ds(b, tt, stride=nb), :]


def _cache_shape(nb, tt, taps, width):
    return (nb, taps, width) if tt == 1 else (taps * nb, width)


def _params(sem="arbitrary"):
    return pltpu.CompilerParams(dimension_semantics=(sem,), vmem_limit_bytes=VMEM_LIMIT)


def _s5_prep_kernel(lr_ref, li_ref, ldt_ref, brt_ref, bit_ref, are_ref, aim_ref, bbre_ref, bbim_ref):
    lr = lr_ref[...]
    li = li_ref[...]
    dt = jnp.exp(ldt_ref[...])
    mag = jnp.exp(lr * dt)
    ab_re = mag * jnp.cos(li * dt)
    ab_im = mag * jnp.sin(li * dt)
    den = lr * lr + li * li
    nr = ab_re - 1.0
    ni = ab_im
    f_re = (nr * lr + ni * li) / den
    f_im = (ni * lr - nr * li) / den
    are_ref[...] = ab_re
    aim_ref[...] = ab_im
    brt = brt_ref[...]
    bit = bit_ref[...]
    bbre_ref[...] = f_re[:, None, :] * brt - f_im[:, None, :] * bit
    bbim_ref[...] = f_re[:, None, :] * bit + f_im[:, None, :] * brt


def _s5_prep(lam_re, lam_im, log_dt, b_re, b_im):
    g, p, h = b_re.shape
    brt = jnp.transpose(b_re, (0, 2, 1))
    bit = jnp.transpose(b_im, (0, 2, 1))
    return pl.pallas_call(
        _s5_prep_kernel,
        out_shape=(jax.ShapeDtypeStruct((g, p), F32), jax.ShapeDtypeStruct((g, p), F32),
                   jax.ShapeDtypeStruct((g, h, p), F32), jax.ShapeDtypeStruct((g, h, p), F32)),
        name="s5_prep",
    )(lam_re, lam_im, log_dt.reshape(g, 1), brt, bit)


def _is_first():
    return pl.program_id(0) == 0


def _side_spec(rows, cols, block=0):
    return pl.BlockSpec((rows, cols), lambda i: (block, 0))


def _l0_in_body(x, wts, outs):
    g_ref, wq_ref, wk_ref, wv_ref, wg_ref, wu_ref, wlr_ref, wal_ref, bal_ref = wts
    q_ref, k_ref, v_ref, gs_ref, la_ref, u_ref = outs
    xn = _rms(x, g_ref[...]).astype(BF16)
    u_ref[...] = _mm(xn, wu_ref[...])
    lr = _mm(xn, wlr_ref[...])
    pre = _mm(lr, wal_ref[...]) + bal_ref[...]
    la_ref[...] = jax.nn.log_sigmoid(pre) * (1.0 / GLA_TAU)
    k_ref[...] = _mm(xn, wk_ref[...])
    q_ref[...] = _mm(xn, wq_ref[...]) * (GLA_DK ** -0.5)
    v_ref[...] = _mm(xn, wv_ref[...])
    g = _mm(xn, wg_ref[...])
    gs_ref[...] = g * _sigmoid(g)


def _l0_in_kernel(*refs):
    x_ref, wts, outs = refs[0], refs[1:1 + L0_IN_WEIGHTS], refs[1 + L0_IN_WEIGHTS:]
    _l0_in_body(x_ref[...], wts, outs)


L0_IN_COLS = (GLA_K, GLA_K, GLA_V, GLA_V, GLA_K, S5_W)
L0_IN_WEIGHTS = 9


def _l0_in_weight_specs():
    return [_const_spec((1, D_MODEL)),
            _const_spec((D_MODEL, GLA_K), (0, 0)), _const_spec((D_MODEL, GLA_K), (0, 1)),
            _const_spec((D_MODEL, GLA_V), (0, 1)), _const_spec((D_MODEL, GLA_V), (0, 2)),
            _const_spec((D_MODEL, S5_W)), _const_spec((D_MODEL, LANES)),
            _const_spec((LANES, GLA_K)), _const_spec((1, GLA_K))]


def _l0_in_weights(w):
    return (w["norm_mix_0"], w["w_in_0"], w["w_in_0"], w["w_in_0"], w["w_in_0"], w["w_u"], w["w_lr"],
            w["w_alpha"], w["b_alpha"])


def _l0_in(x, w):
    n = x.shape[0]
    return pl.pallas_call(
        _l0_in_kernel,
        grid=(1,),
        in_specs=[_row_spec(n, D_MODEL)] + _l0_in_weight_specs(),
        out_specs=[_row_spec(n, c) for c in L0_IN_COLS],
        out_shape=[jax.ShapeDtypeStruct((n, c), F32) for c in L0_IN_COLS],
        compiler_params=_params(),
        name="l0_in",
    )(x, *_l0_in_weights(w))


def _gla_chunk_prep(ins, b_sc, nb, c):
    q_ref, k_ref, _, la_ref, _ = ins
    rows = nb * c
    seq_mask = nb - 1

    def cum_body(t, run):
        rws = pl.ds(pl.multiple_of(t * nb, nb), nb)
        run = run + la_ref[rws, :]
        b_sc[rws, :] = run
        return run

    bl = lax.fori_loop(0, c, cum_body, jnp.zeros((nb, GLA_K), F32), unroll=True)
    b = b_sc[...]
    q = q_ref[...]
    k = k_ref[...]
    bm = jnp.concatenate([b_sc[pl.ds((c // 2) * nb, nb), :]] * c, axis=0)
    ri = lax.broadcasted_iota(jnp.int32, (rows, rows), 0)
    ci = lax.broadcasted_iota(jnp.int32, (rows, rows), 1)
    xw = nb * GLA_DK
    return dict(
        qa=q * jnp.exp(b - bm), kt=k * jnp.exp(bm - b), qt=q * jnp.exp(b),
        kh=k * jnp.exp(jnp.concatenate([bl] * c, axis=0) - b),
        gam=jnp.exp(bl),
        pair_ok=(ri >= ci) & (((ri - ci) & seq_mask) == 0),
        own_blk=(lax.broadcasted_iota(jnp.int32, (rows, xw), 1) // GLA_DK
                 == (lax.broadcasted_iota(jnp.int32, (rows, xw), 0) & seq_mask)),
        own_blk_seq=(lax.broadcasted_iota(jnp.int32, (nb, xw), 1) // GLA_DK
                     == lax.broadcasted_iota(jnp.int32, (nb, xw), 0)),
        reps=xw // LANES)


def _gla_chunk_head(p, h, ins, gn_ref, o_ref, st_ref):
    v_ref, gs_ref = ins[2], ins[4]

    def head_dup(z):
        blk = z[:, (h // 2) * LANES:(h // 2 + 1) * LANES]
        rolled = pltpu.roll(blk, GLA_DK, axis=1)
        low = lax.broadcasted_iota(jnp.int32, blk.shape, 1) < GLA_DK
        return jnp.where(low, blk, rolled) if h % 2 == 0 else jnp.where(low, rolled, blk)

    def expand(zd, own):
        return jnp.where(own, jnp.concatenate([zd] * p["reps"], axis=1), 0.0)

    vs = slice(h * GLA_DV, (h + 1) * GLA_DV)
    qad, qd, ktd, khd = head_dup(p["qa"]), head_dup(p["qt"]), head_dup(p["kt"]), head_dup(p["kh"])
    vb = v_ref[:, vs].astype(BF16)
    st = st_ref[h]
    att = lax.dot_general(qad[:, :GLA_DK].astype(BF16), ktd[:, :GLA_DK].astype(BF16),
                          (((1,), (1,)), ((), ())), preferred_element_type=F32)
    att = jnp.where(p["pair_ok"], att, 0.0).astype(BF16)
    o = jnp.dot(att, vb, preferred_element_type=F32) + lax.dot_general(
        expand(qd, p["own_blk"]).astype(BF16), st.astype(BF16), (((1,), (1,)), ((), ())),
        preferred_element_type=F32)
    upd = lax.dot_general(vb, expand(khd, p["own_blk"]).astype(BF16), (((0,), (0,)), ((), ())),
                          preferred_element_type=F32)
    gam_row = jnp.sum(expand(head_dup(p["gam"]), p["own_blk_seq"]), axis=0, keepdims=True)
    st_ref[h] = st * gam_row + upd
    o_ref[:, vs] = _rms(o, gn_ref[:, vs]) * gs_ref[:, vs]


def _gla_chunk_body(ins, gn_ref, o_ref, st_ref, b_sc, nb, c):
    p = _gla_chunk_prep(ins, b_sc, nb, c)
    for h in range(GLA_HEADS):
        _gla_chunk_head(p, h, ins, gn_ref, o_ref, st_ref)


def _gla_chunk_kernel(*refs, nb, c):
    ins, gn_ref, st0_ref, o_ref, st_ref, b_sc = refs[0:5], refs[5], refs[6], refs[7], refs[8], refs[9]

    @pl.when(_is_first())
    def _():
        st_ref[...] = st0_ref[...]

    _gla_chunk_body(ins, gn_ref, o_ref, st_ref, b_sc, nb, c)


def _gla_chunk(ins, gn, st0, nb, c, row_blocks=None):
    assert nb & (nb - 1) == 0 and (nb * GLA_DK) % LANES == 0
    rows = nb * c
    n = ins[0].shape[0] // row_blocks[0] if row_blocks else ins[0].shape[0]
    first = (row_blocks[1] * n) // rows if row_blocks else 0
    sshape = (GLA_HEADS, GLA_DV, nb * GLA_DK)
    cols = (GLA_K, GLA_K, GLA_V, GLA_K, GLA_V)
    kern = functools.partial(_gla_chunk_kernel, nb=nb, c=c)
    return pl.pallas_call(
        kern,
        grid=(n // rows,),
        in_specs=[pl.BlockSpec((rows, w_), lambda i: (first + i, 0)) for w_ in cols]
                 + [_const_spec((1, GLA_V)), _const_spec(sshape)],
        out_specs=[_row_spec(rows, GLA_V), pl.BlockSpec(sshape, lambda i: (0, 0, 0))],
        out_shape=[jax.ShapeDtypeStruct((n, GLA_V), F32), jax.ShapeDtypeStruct(sshape, F32)],
        scratch_shapes=[pltpu.VMEM((rows, GLA_K), F32)],
        compiler_params=_params(),
        name="gla_chunk",
    )(*ins, gn, st0)


def _gla_state_from_stacked(st):
    nb = st.shape[2] // GLA_DK
    return jnp.transpose(st.reshape(GLA_HEADS, GLA_DV, nb, GLA_DK), (2, 0, 3, 1))


def _gla_step_kernel(q_ref, k_ref, la_ref, v_ref, gs_ref, gn_ref, s0_ref, o_ref, s_ref):
    qT = q_ref[...].T
    kT = k_ref[...].T
    aT = jnp.exp(la_ref[...]).T
    o_rows = []
    for b in range(q_ref.shape[0]):
        s_new = aT[:, b:b + 1] * s0_ref[b] + kT[:, b:b + 1] * v_ref[b:b + 1, :]
        s_ref[b] = s_new
        o_rows.append(jnp.sum(qT[:, b:b + 1] * s_new, axis=0, keepdims=True))
    o = jnp.concatenate(o_rows, axis=0)
    o_ref[...] = _rms(o, gn_ref[...]) * gs_ref[...]


def _gla_step(side, gn, s0, row0):
    nb = s0.shape[0]
    blk = row0 // nb
    q, k, v, la, gs = side
    heads = lambda z: jnp.transpose(z[row0:row0 + nb].reshape(nb, GLA_HEADS, GLA_DK), (1, 0, 2))
    hspec = pl.BlockSpec((None, nb, GLA_DK), lambda h: (h, 0, 0))
    vspec = pl.BlockSpec((nb, GLA_DV), lambda h: (blk, h))
    ospec = pl.BlockSpec((nb, GLA_DV), lambda h: (0, h))
    sspec = pl.BlockSpec((nb, None, GLA_DK, GLA_DV), lambda h: (0, h, 0, 0))
    return pl.pallas_call(
        _gla_step_kernel,
        grid=(GLA_HEADS,),
        in_specs=[hspec, hspec, hspec, vspec, vspec, pl.BlockSpec((1, GLA_DV), lambda h: (0, h)), sspec],
        out_specs=[ospec, sspec],
        out_shape=[jax.ShapeDtypeStruct((nb, GLA_V), F32),
                   jax.ShapeDtypeStruct((nb, GLA_HEADS, GLA_DK, GLA_DV), F32)],
        compiler_params=_params("parallel"),
        name="gla_step",
    )(heads(q), heads(k), heads(la), v, gs, gn, s0)


def _l0_out_body(u_ref, og_ref, x, wts, xo_ref, xre_ref, xim_ref, sr_sc, si_sc, nb, tt):
    npack = S5_GROUPS // S5_PACK
    for j in range(npack):
        _s5_input(u_ref, wts, sr_sc, si_sc, j)
    ys = []
    for j in range(npack):
        _s5_scan(wts, xre_ref, xim_ref, sr_sc, si_sc, j, nb, tt)
        ys.append(_s5_output(wts, sr_sc, si_sc, j))
    _l0_tail_body(_s5_glu(ys, u_ref, wts), og_ref, x, wts, xo_ref)


S5_KIN = S5_PACK * S5_H
S5_KST = S5_PACK * S5_P


def _s5_input(u_ref, wts, sr_sc, si_sc, j):
    bbre_ref, bbim_ref = wts[0:2]
    cs = slice(j * S5_KST, (j + 1) * S5_KST)
    uj = u_ref[:, j * S5_KIN:(j + 1) * S5_KIN].astype(BF16)
    sr_sc[:, cs] = jnp.dot(uj, bbre_ref[j], preferred_element_type=F32)
    si_sc[:, cs] = jnp.dot(uj, bbim_ref[j], preferred_element_type=F32)


def _s5_scan(wts, xre_ref, xim_ref, sr_sc, si_sc, j, nb, tt):
    are_ref, aim_ref = wts[2:4]
    cs = slice(j * S5_KST, (j + 1) * S5_KST)
    ar = jnp.broadcast_to(are_ref[:, cs], (nb, S5_KST))
    ai = jnp.broadcast_to(aim_ref[:, cs], (nb, S5_KST))
    xr = xre_ref[:, cs]
    xi = xim_ref[:, cs]
    for t in range(tt):
        rows = slice(t * nb, (t + 1) * nb)
        xr, xi = ar * xr - ai * xi + sr_sc[rows, cs], ar * xi + ai * xr + si_sc[rows, cs]
        sr_sc[rows, cs] = xr
        si_sc[rows, cs] = xi
    xre_ref[:, cs] = xr
    xim_ref[:, cs] = xi


def _s5_output(wts, sr_sc, si_sc, j):
    cre_ref, cim_ref = wts[4:6]
    cs = slice(j * S5_KST, (j + 1) * S5_KST)
    return (jnp.dot(sr_sc[:, cs].astype(BF16), cre_ref[j], preferred_element_type=F32)
            - jnp.dot(si_sc[:, cs].astype(BF16), cim_ref[j], preferred_element_type=F32))


def _s5_glu(ys, u_ref, wts):
    d_ref, wglu_ref, bglu_ref = wts[6:9]
    y = jax.nn.gelu(jnp.concatenate(ys, axis=-1) + d_ref[...] * u_ref[...])
    return y * _sigmoid(_mm(y, wglu_ref[...]) + bglu_ref[...])


def _l0_tail_body(y, og_ref, x, wts, xo_ref):
    woa_ref, wob_ref = wts[9:11]
    xo_ref[...] = x + _mm(og_ref[...], woa_ref[...]) + _mm(y, wob_ref[...])


L0_OUT_WEIGHTS = 11


def _l0_out_weight_specs():
    npack = S5_GROUPS // S5_PACK
    kin, kst = S5_PACK * S5_H, S5_PACK * S5_P
    return [_const_spec((npack, kin, kst)), _const_spec((npack, kin, kst)),
            _const_spec((1, S5_N)), _const_spec((1, S5_N)),
            _const_spec((npack, kst, kin)), _const_spec((npack, kst, kin)),
            _const_spec((1, S5_W)), _const_spec((S5_W, S5_W)), _const_spec((1, S5_W)),
            _const_spec((GLA_V, D_MODEL), (0, 0)), _const_spec((S5_W, D_MODEL), (1, 0))]


def _l0_out_weights(w):
    return (w["s5_bbre"], w["s5_bbim"], w["s5_are"], w["s5_aim"], w["s5_cre"], w["s5_cim"],
            w["s5_d"], w["s5_w_glu"], w["s5_b_glu"], w["w_out_0"], w["w_out_0"])


def _l0_out_side_kernel(*refs, nb, tt_meta, nb_s):
    um_ref, ogm_ref, xm_ref, us_ref, ogs_ref, xs_ref, xr0s_ref, xi0s_ref = refs[0:8]
    wts = refs[8:8 + L0_OUT_WEIGHTS]
    xom_ref, xre_ref, xim_ref, xos_ref, xres_ref, xims_ref, sr_sc, si_sc = refs[8 + L0_OUT_WEIGHTS:]
    xre_ref[...] = jnp.zeros(xre_ref.shape, F32)
    xim_ref[...] = jnp.zeros(xim_ref.shape, F32)
    _l0_out_body(um_ref, ogm_ref, xm_ref[...], wts, xom_ref, xre_ref, xim_ref, sr_sc, si_sc, nb, tt_meta)
    xres_ref[...] = xr0s_ref[...]
    xims_ref[...] = xi0s_ref[...]
    _l0_out_body(us_ref, ogs_ref, xs_ref[...], wts, xos_ref, xres_ref, xims_ref, sr_sc, si_sc, nb_s, 1)


def _l0_out_side(u_side, og_meta, og_samp, x_side, xr0_s, xi0_s, w, nb, tt_meta):
    mrows, nb_s = nb * tt_meta, xr0_s.shape[0]
    assert mrows == nb_s
    kern = functools.partial(_l0_out_side_kernel, nb=nb, tt_meta=tt_meta, nb_s=nb_s)
    f32 = lambda *shape: jax.ShapeDtypeStruct(shape, F32)
    full = lambda r, c: pl.BlockSpec((r, c), lambda i: (0, 0))
    return pl.pallas_call(
        kern,
        grid=(1,),
        in_specs=[_side_spec(mrows, S5_W, 0), full(mrows, GLA_V), _side_spec(mrows, D_MODEL, 0),
                  _side_spec(nb_s, S5_W, 1), full(nb_s, GLA_V), _side_spec(nb_s, D_MODEL, 1),
                  full(nb_s, S5_N), full(nb_s, S5_N)] + _l0_out_weight_specs(),
        out_specs=[full(mrows, D_MODEL), full(nb, S5_N), full(nb, S5_N), full(nb_s, D_MODEL),
                   full(nb_s, S5_N), full(nb_s, S5_N)],
        out_shape=[f32(mrows, D_MODEL), f32(nb, S5_N), f32(nb, S5_N), f32(nb_s, D_MODEL),
                   f32(nb_s, S5_N), f32(nb_s, S5_N)],
        scratch_shapes=[pltpu.VMEM((mrows, S5_N), F32)] * 2,
        compiler_params=_params(),
        name="l0_out_side",
    )(u_side, og_meta, x_side, u_side, og_samp, x_side, xr0_s, xi0_s, *_l0_out_weights(w))


def _l0_kernel(*refs, nb, tt, c):
    x_ref, st0_ref, xr0_ref, xi0_ref = refs[0:4]
    n_in = 4 + L0_IN_WEIGHTS
    in_wts, gn_ref, out_wts = refs[4:n_in], refs[n_in], refs[n_in + 1:n_in + 1 + L0_OUT_WEIGHTS]
    rest = refs[n_in + 1 + L0_OUT_WEIGHTS:]
    xo_ref, st_ref, xre_ref, xim_ref = rest[0:4]
    proj_sc, og_sc, b_sc, sr_sc, si_sc, tm_sc = rest[4:10], rest[10], rest[11], rest[12], rest[13], rest[14]

    @pl.when(_is_first())
    def _():
        st_ref[...] = st0_ref[...]
        xre_ref[...] = xr0_ref[...]
        xim_ref[...] = xi0_ref[...]

    x = _load_time_major(x_ref, tm_sc)
    _l0_in_body(x, in_wts, proj_sc)
    q_sc, k_sc, v_sc, gs_sc, la_sc, u_sc = proj_sc
    npack = S5_GROUPS // S5_PACK
    nchunk = tt // c
    units = [(j, hp) for j in range(nchunk) for hp in range(GLA_HEADS // 2)]
    assert len(units) == npack
    for j in range(npack):
        _s5_input(u_sc, out_wts, sr_sc, si_sc, j)
    ys, prep = [], {}
    for i, (j, hp) in enumerate(units):
        _s5_scan(out_wts, xre_ref, xim_ref, sr_sc, si_sc, i, nb, tt)
        rws = pl.ds(j * nb * c, nb * c)
        ins = [r.at[rws] for r in (q_sc, k_sc, v_sc, la_sc, gs_sc)]
        if hp == 0:
            prep = _gla_chunk_prep(ins, b_sc, nb, c)
        for h in (2 * hp, 2 * hp + 1):
            _gla_chunk_head(prep, h, ins, gn_ref, og_sc.at[rws], st_ref)
        ys.append(_s5_output(out_wts, sr_sc, si_sc, i))
    _l0_tail_body(_s5_glu(ys, u_sc, out_wts), og_sc, x, out_wts, xo_ref)


def _l0(x, st0, xr0, xi0, w, tt, c):
    nb, nt, _ = x.shape
    n, rows = nb * nt, nb * tt
    assert nb & (nb - 1) == 0 and (nb * GLA_DK) % LANES == 0 and tt % c == 0
    sshape = (GLA_HEADS, GLA_DV, nb * GLA_DK)
    kern = functools.partial(_l0_kernel, nb=nb, tt=tt, c=c)
    st_spec = pl.BlockSpec((nb, S5_N), lambda i: (0, 0))
    f32 = lambda *shape: jax.ShapeDtypeStruct(shape, F32)
    return pl.pallas_call(
        kern,
        grid=(n // rows,),
        in_specs=[_seq_spec(nb, tt, D_MODEL), _const_spec(sshape), _const_spec((nb, S5_N)),
                  _const_spec((nb, S5_N))] + _l0_in_weight_specs() + [_const_spec((1, GLA_V))]
                 + _l0_out_weight_specs(),
        out_specs=[_row_spec(rows, D_MODEL), pl.BlockSpec(sshape, lambda i: (0, 0, 0)), st_spec, st_spec],
        out_shape=[f32(n, D_MODEL), f32(*sshape), f32(nb, S5_N), f32(nb, S5_N)],
        scratch_shapes=[pltpu.VMEM((rows, cols), F32) for cols in L0_IN_COLS]
                       + [pltpu.VMEM((rows, GLA_V), F32), pltpu.VMEM((nb * c, GLA_K), F32),
                          pltpu.VMEM((rows, S5_N), F32), pltpu.VMEM((rows, S5_N), F32),
                          _tm_scratch(rows, D_MODEL)],
        compiler_params=_params(),
        name="l0_mixer",
    )(x, st0, xr0, xi0, *_l0_in_weights(w), w["gla_norm"], *_l0_out_weights(w))


def _ffn_body(x, wts, c0_ref, c_ref, hm_sc, nb, tt, final):
    g_ref, wg_ref, wv_ref, cw_ref, cb_ref, wd_ref, gf_ref = wts
    rows = nb * tt
    xn = _rms(x, g_ref[...]).astype(BF16)
    for ci in range(D_FF // FF_CHUNK):
        cs = slice(ci * FF_CHUNK, (ci + 1) * FF_CHUNK)
        gate = jnp.dot(xn, wg_ref[:, cs], preferred_element_type=F32)
        val = jnp.dot(xn, wv_ref[:, cs], preferred_element_type=F32)
        if tt == 1:
            taps = [c0_ref[:, j, cs] for j in range(FFN_CONV - 1)] + [gate]
            for j in range(FFN_CONV - 1):
                c_ref[:, j, cs] = taps[j + 1]
        else:
            ext = jnp.concatenate([c_ref[:, cs], gate], axis=0)
            taps = [ext[j * nb:j * nb + rows] for j in range(FFN_CONV)]
            c_ref[:, cs] = ext[tt * nb:(tt + FFN_CONV - 1) * nb]
        y = cb_ref[:, cs] + taps[0] * cw_ref[0:1, cs]
        for j in range(1, FFN_CONV):
            y = y + taps[j] * cw_ref[j:j + 1, cs]
        hm_sc[:, cs] = (jax.nn.gelu(y) * val).astype(BF16)
    out = x + jnp.dot(hm_sc[...], wd_ref[...], preferred_element_type=F32)
    return _rms(out, gf_ref[...]) if final else out


FFN_WEIGHTS = 7


def _ffn_weight_specs(layer):
    return [_const_spec((None, 1, D_MODEL), (layer, 0, 0)),
            _const_spec((None, D_MODEL, D_FF), (layer, 0, 0)),
            _const_spec((None, D_MODEL, D_FF), (layer, 0, 1)),
            _const_spec((None, FFN_CONV, D_FF), (layer, 0, 0)),
            _const_spec((None, 1, D_FF), (layer, 0, 0)),
            _const_spec((None, D_FF, D_MODEL), (layer, 0, 0)), _const_spec((1, D_MODEL))]


def _ffn_weights(w):
    return (w["norm_ffn"], w["ffn_w_up"], w["ffn_w_up"], w["ffn_conv_w"], w["ffn_conv_b"], w["ffn_w_down"],
            w["norm_final"])


def _ffn_kernel(*refs, nb, tt, final, batch_major_out):
    x_ref, c0_ref = refs[0:2]
    wts = refs[2:2 + FFN_WEIGHTS]
    xo_ref, c_ref, hm_sc = refs[2 + FFN_WEIGHTS:5 + FFN_WEIGHTS]
    tm_sc = refs[5 + FFN_WEIGHTS] if batch_major_out else None

    @pl.when(_is_first())
    def _():
        c_ref[...] = c0_ref[...]

    _store_time_major(xo_ref, _ffn_body(x_ref[...], wts, None, c_ref, hm_sc, nb, tt, final), tm_sc)


def _ffn(x, c0, w, layer, nb, tt, final, batch_major_out=False):
    n = x.shape[0]
    rows = nb * tt
    kern = functools.partial(_ffn_kernel, nb=nb, tt=tt, final=final, batch_major_out=batch_major_out)
    if batch_major_out:
        o_spec, o_shape = _seq_spec(nb, tt, D_MODEL), (nb, n // nb, D_MODEL)
    else:
        o_spec, o_shape = _row_spec(rows, D_MODEL), (n, D_MODEL)
    cshape = _cache_shape(nb, tt, FFN_CONV - 1, D_FF)
    f32 = lambda *shape: jax.ShapeDtypeStruct(shape, F32)
    return pl.pallas_call(
        kern,
        grid=(n // rows,),
        in_specs=[_row_spec(rows, D_MODEL), _const_spec(cshape)] + _ffn_weight_specs(layer),
        out_specs=[o_spec, pl.BlockSpec(cshape, lambda i: (0, 0))],
        out_shape=[f32(*o_shape), f32(*cshape)],
        scratch_shapes=[pltpu.VMEM((rows, D_FF), BF16)]
                       + ([_tm_scratch(rows, D_MODEL)] if batch_major_out else []),
        compiler_params=_params(),
        name="ffn%d" % layer,
    )(x, c0, *_ffn_weights(w))


def _ffn_side_kernel(*refs, nb, tt_meta, nb_s, final):
    xm_ref, xs_ref, c0s_ref = refs[0:3]
    wts = refs[3:3 + FFN_WEIGHTS]
    xom_ref, c_ref, xos_ref, cs_ref, hm_sc = refs[3 + FFN_WEIGHTS:]
    c_ref[...] = jnp.zeros(c_ref.shape, F32)
    xom_ref[...] = _ffn_body(xm_ref[...], wts, None, c_ref, hm_sc, nb, tt_meta, final)
    xos_ref[...] = _ffn_body(xs_ref[...], wts, c0s_ref, cs_ref, hm_sc, nb_s, 1, final)


def _ffn_side(x_meta, x_samp, c0_s, w, layer, nb, tt_meta, final):
    mrows, nb_s = nb * tt_meta, x_samp.shape[0]
    assert mrows == nb_s
    kern = functools.partial(_ffn_side_kernel, nb=nb, tt_meta=tt_meta, nb_s=nb_s, final=final)
    cshape = _cache_shape(nb, tt_meta, FFN_CONV - 1, D_FF)
    cs_shape = _cache_shape(nb_s, 1, FFN_CONV - 1, D_FF)
    f32 = lambda *shape: jax.ShapeDtypeStruct(shape, F32)
    full = lambda shape: pl.BlockSpec(shape, lambda i: (0,) * len(shape))
    return pl.pallas_call(
        kern,
        grid=(1,),
        in_specs=[full((mrows, D_MODEL)), full((nb_s, D_MODEL)),
                  pl.BlockSpec((None,) + cs_shape, lambda i: (layer, 0, 0, 0))] + _ffn_weight_specs(layer),
        out_specs=[full((mrows, D_MODEL)), full(cshape), full((nb_s, D_MODEL)), full(cs_shape)],
        out_shape=[f32(mrows, D_MODEL), f32(*cshape), f32(nb_s, D_MODEL), f32(*cs_shape)],
        scratch_shapes=[pltpu.VMEM((mrows, D_FF), BF16)],
        compiler_params=_params(),
        name="ffn%d_side" % layer,
    )(x_meta, x_samp, c0_s, *_ffn_weights(w))


def _l1_body(x_ref, wts, xo_ref, h_ref, c0_ref, c_ref, a_sc, b_sc, nb, tt):
    g_ref, wgt_ref, wxr_ref, cw_ref, cb_ref, wa_ref, ba_ref, wx_ref, bx_ref, lam_ref, wo_ref = wts
    sp = jax.nn.softplus(-lam_ref[...])
    nsplit = L1_SPLIT if tt % L1_SPLIT == 0 else 1
    th = tt // nsplit
    rows = nb * th
    carry = c_ref[...] if tt > 1 else None
    xs, xns, ggs = [], [], []

    for part in range(nsplit):
        prow = slice(part * rows, (part + 1) * rows)
        x = x_ref[prow, :]
        xn = _rms(x, g_ref[...]).astype(BF16)
        xs.append(x)
        xns.append(xn)
        xr = jnp.dot(xn, wxr_ref[...], preferred_element_type=F32)
        if tt == 1:
            taps = [c0_ref[:, j, :] for j in range(RNN_CONV - 1)] + [xr]
            for j in range(RNN_CONV - 1):
                c_ref[:, j, :] = taps[j + 1]
        else:
            ext = jnp.concatenate([carry, xr], axis=0)
            taps = [ext[j * nb:j * nb + rows] for j in range(RNN_CONV)]
            carry = ext[th * nb:(th + RNN_CONV - 1) * nb]
        xc = cb_ref[...] + taps[0] * cw_ref[0:1, :]
        for j in range(1, RNN_CONV):
            xc = xc + taps[j] * cw_ref[j:j + 1, :]

        xcb = xc.astype(BF16)
        rs, gs = [], []
        for wi in range(RNN_W // GATE_WIN):
            for ni in range(GATE_WIN // GATE_N):
                k0 = wi * GATE_WIN + ni * LANES
                lhs = xcb[:, k0:k0 + GATE_K]
                rs.append(jnp.dot(lhs, wa_ref[wi, ni], preferred_element_type=F32))
                gs.append(jnp.dot(lhs, wx_ref[wi, ni], preferred_element_type=F32))
        r = _sigmoid(jnp.concatenate(rs, axis=-1) + ba_ref[...])
        ig = _sigmoid(jnp.concatenate(gs, axis=-1) + bx_ref[...])
        log_a = (-RNN_C) * r * sp
        a = jnp.exp(log_a)
        a_sc[prow, :] = a
        b_sc[prow, :] = _sqrt_nonneg(jnp.tanh(-log_a) * (a * a + 1.0)) * (ig * xc)
    if tt > 1:
        c_ref[...] = carry

    for part in range(nsplit):
        ggs.append(jax.nn.gelu(jnp.dot(xns[part], wgt_ref[...], preferred_element_type=F32)))

    h = h_ref[...]
    for part in range(nsplit):
        prow = slice(part * rows, (part + 1) * rows)
        for t in range(part * th, (part + 1) * th):
            rws = slice(t * nb, (t + 1) * nb)
            h = a_sc[rws, :] * h + b_sc[rws, :]
            b_sc[rws, :] = h
        xo_ref[prow, :] = xs[part] + _mm(b_sc[prow, :] * ggs[part], wo_ref[...])
    h_ref[...] = h


L1_WEIGHTS = 11


def _l1_weight_specs():
    gshape = (RNN_W // GATE_WIN, GATE_WIN // GATE_N, GATE_K, GATE_N)
    return [_const_spec((1, D_MODEL)),
            _const_spec((D_MODEL, RNN_W), (0, 0)), _const_spec((D_MODEL, RNN_W), (0, 1)),
            _const_spec((RNN_CONV, RNN_W)), _const_spec((1, RNN_W)),
            _const_spec(gshape), _const_spec((1, RNN_W)),
            _const_spec(gshape), _const_spec((1, RNN_W)),
            _const_spec((1, RNN_W)), _const_spec((RNN_W, D_MODEL))]


def _l1_weights(w):
    return (w["norm_mix_1"], w["w_in_1"], w["w_in_1"], w["rnn_conv_w"], w["rnn_conv_b"], w["rnn_wa"],
            w["rnn_b_a"], w["rnn_wx"], w["rnn_b_x"], w["rnn_lam"], w["w_out_1"])


def _l1_kernel(*refs, nb, tt):
    x_ref, h0_ref, c0_ref = refs[0:3]
    wts = refs[3:3 + L1_WEIGHTS]
    xo_ref, h_ref, c_ref, a_sc, b_sc = refs[3 + L1_WEIGHTS:]

    @pl.when(_is_first())
    def _():
        h_ref[...] = h0_ref[...]
        c_ref[...] = c0_ref[...]

    _l1_body(x_ref, wts, xo_ref, h_ref, None, c_ref, a_sc, b_sc, nb, tt)


def _l1(x, h0, c0, w, nb, tt):
    n = x.shape[0]
    rows = nb * tt
    kern = functools.partial(_l1_kernel, nb=nb, tt=tt)
    cshape = _cache_shape(nb, tt, RNN_CONV - 1, RNN_W)
    f32 = lambda *shape: jax.ShapeDtypeStruct(shape, F32)
    return pl.pallas_call(
        kern,
        grid=(n // rows,),
        in_specs=[_row_spec(rows, D_MODEL), _const_spec((nb, RNN_W)), _const_spec(cshape)] + _l1_weight_specs(),
        out_specs=[_row_spec(rows, D_MODEL), pl.BlockSpec((nb, RNN_W), lambda i: (0, 0)),
                   pl.BlockSpec(cshape, lambda i: (0, 0))],
        out_shape=[f32(n, D_MODEL), f32(nb, RNN_W), f32(*cshape)],
        scratch_shapes=[pltpu.VMEM((rows, RNN_W), F32)] * 2,
        compiler_params=_params(),
        name="l1_mixer",
    )(x, h0, c0, *_l1_weights(w))


def _l1_side_kernel(*refs, nb, tt_meta, nb_s):
    xm_ref, xs_ref, h0s_ref, c0s_ref = refs[0:4]
    wts = refs[4:4 + L1_WEIGHTS]
    xom_ref, h_ref, c_ref, xos_ref, hs_ref, cs_ref, a_sc, b_sc = refs[4 + L1_WEIGHTS:]
    h_ref[...] = jnp.zeros(h_ref.shape, F32)
    c_ref[...] = jnp.zeros(c_ref.shape, F32)
    _l1_body(xm_ref, wts, xom_ref, h_ref, None, c_ref, a_sc, b_sc, nb, tt_meta)
    hs_ref[...] = h0s_ref[...]
    _l1_body(xs_ref, wts, xos_ref, hs_ref, c0s_ref, cs_ref, a_sc, b_sc, nb_s, 1)


def _l1_side(x_meta, x_samp, h0_s, c0_s, w, nb, tt_meta):
    mrows, nb_s = nb * tt_meta, x_samp.shape[0]
    assert mrows == nb_s
    kern = functools.partial(_l1_side_kernel, nb=nb, tt_meta=tt_meta, nb_s=nb_s)
    cshape = _cache_shape(nb, tt_meta, RNN_CONV - 1, RNN_W)
    cs_shape = _cache_shape(nb_s, 1, RNN_CONV - 1, RNN_W)
    f32 = lambda *shape: jax.ShapeDtypeStruct(shape, F32)
    full = lambda shape: pl.BlockSpec(shape, lambda i: (0,) * len(shape))
    return pl.pallas_call(
        kern,
        grid=(1,),
        in_specs=[full((mrows, D_MODEL)), full((nb_s, D_MODEL)), full((nb_s, RNN_W)), full(cs_shape)]
                 + _l1_weight_specs(),
        out_specs=[full((mrows, D_MODEL)), full((nb, RNN_W)), full(cshape), full((nb_s, D_MODEL)),
                   full((nb_s, RNN_W)), full(cs_shape)],
        out_shape=[f32(mrows, D_MODEL), f32(nb, RNN_W), f32(*cshape), f32(nb_s, D_MODEL), f32(nb_s, RNN_W),
                   f32(*cs_shape)],
        scratch_shapes=[pltpu.VMEM((mrows, RNN_W), F32)] * 2,
        compiler_params=_params(),
        name="l1_side",
    )(x_meta, x_samp, h0_s, c0_s, *_l1_weights(w))


def _pack_gate_kernel(wa_ref, wx_ref, oa_ref, ox_ref):
    tiles_per_win = GATE_WIN // GATE_N
    for w_ref, o_ref in ((wa_ref, oa_ref), (wx_ref, ox_ref)):
        o_ref[...] = jnp.zeros(o_ref.shape, o_ref.dtype)
        for n in range(RNN_BLOCKS):
            pos = n * RNN_BW
            wi = pos // GATE_WIN
            for ni in range(tiles_per_win):
                k0 = wi * GATE_WIN + ni * LANES
                n0 = wi * GATE_WIN + ni * GATE_N
                lo, hi = max(pos, n0), min(pos + RNN_BW, n0 + GATE_N)
                if lo < hi:
                    o_ref[wi, ni, pos - k0:pos - k0 + RNN_BW, lo - n0:hi - n0] = (
                        w_ref[n][:, lo - pos:hi - pos].astype(o_ref.dtype))


def _pack_gates(wa, wx):
    gshape = (RNN_W // GATE_WIN, GATE_WIN // GATE_N, GATE_K, GATE_N)
    return pl.pallas_call(
        _pack_gate_kernel,
        out_shape=(jax.ShapeDtypeStruct(gshape, BF16), jax.ShapeDtypeStruct(gshape, BF16)),
        name="pack_gates",
    )(wa, wx)


def _prep_weights(p):
    w = {}
    row = lambda v: v.reshape(1, -1).astype(F32)
    w_in = p["w_in_0"]
    c = 2 * GLA_K + 2 * GLA_V
    w["w_in_0"] = w_in.astype(BF16)
    w["w_lr"] = jnp.pad(w_in[:, c:c + GLA_RANK], ((0, 0), (0, LANES - GLA_RANK))).astype(BF16)
    c += GLA_RANK
    w["w_u"] = w_in[:, c:c + S5_W].astype(BF16)
    w["w_alpha"] = jnp.pad(p["w_alpha_0"], ((0, LANES - GLA_RANK), (0, 0))).astype(BF16)
    w["b_alpha"] = row(p["b_alpha_0"])
    w["norm_mix_0"] = row(p["norm_mix_0"])
    w["gla_norm"] = row(p["gla_norm_0"])

    are, aim, bbre, bbim = _s5_prep(p["s5_lam_re"], p["s5_lam_im"], p["s5_log_dt"], p["s5_b_re"],
                                    p["s5_b_im"])
    npack = S5_GROUPS // S5_PACK
    eye = jnp.eye(S5_PACK, dtype=F32)[None, :, None, :, None]
    grouped = lambda m: m.reshape(npack, S5_PACK, S5_H, S5_P)
    pack_b = lambda m: (grouped(m)[:, :, :, None, :] * eye).reshape(
        npack, S5_PACK * S5_H, S5_PACK * S5_P).astype(BF16)
    pack_c = lambda m: (jnp.swapaxes(grouped(m), 2, 3)[:, :, :, None, :] * eye).reshape(
        npack, S5_PACK * S5_P, S5_PACK * S5_H).astype(BF16)
    w["s5_are"] = are.reshape(1, S5_N)
    w["s5_aim"] = aim.reshape(1, S5_N)
    w["s5_bbre"] = pack_b(bbre)
    w["s5_bbim"] = pack_b(bbim)
    w["s5_cre"] = pack_c(p["s5_c_re"])
    w["s5_cim"] = pack_c(p["s5_c_im"])
    w["s5_d"] = row(p["s5_d"])
    w["s5_w_glu"] = p["s5_w_glu"].astype(BF16)
    w["s5_b_glu"] = row(p["s5_b_glu"])
    w["w_out_0"] = p["w_out_0"].astype(BF16)

    w["norm_mix_1"] = row(p["norm_mix_1"])
    w["w_in_1"] = p["w_in_1"].astype(BF16)
    w["rnn_conv_w"] = p["rnn_conv_w"].astype(F32)
    w["rnn_conv_b"] = row(p["rnn_conv_b"])
    w["rnn_wa"], w["rnn_wx"] = _pack_gates(p["rnn_w_a"], p["rnn_w_x"])
    w["rnn_b_a"] = row(p["rnn_b_a"])
    w["rnn_b_x"] = row(p["rnn_b_x"])
    w["rnn_lam"] = row(p["rnn_lam"])
    w["w_out_1"] = p["w_out_1"].astype(BF16)

    depth = p["norm_ffn"].shape[0]
    w["norm_ffn"] = p["norm_ffn"].reshape(depth, 1, D_MODEL)
    w["ffn_w_up"] = p["ffn_w_up"].astype(BF16)
    w["ffn_conv_w"] = p["ffn_conv_w"]
    w["ffn_conv_b"] = p["ffn_conv_b"].reshape(depth, 1, D_FF)
    w["ffn_w_down"] = p["ffn_w_down"].astype(BF16)
    w["norm_final"] = row(p["norm_final"])
    return w


def _tile_steps():
    return dict(l0=64, gla=32, ffn=64, l1=64)


def _batch_major(cache, nb):
    jb, c = cache.shape
    return jnp.transpose(cache.reshape(jb // nb, nb, c), (1, 0, 2))


def kernel(x_prompt, x_sample, state_gla, state_s5_re, state_s5_im, state_rglru, cache_rglru_conv,
           cache_ffn_conv, meta_tokens, norm_mix_0, w_in_0, w_alpha_0, b_alpha_0, gla_norm_0,
           s5_lam_re, s5_lam_im, s5_log_dt, s5_b_re, s5_b_im, s5_c_re, s5_c_im, s5_d, s5_w_glu,
           s5_b_glu, w_out_0, norm_mix_1, w_in_1, rnn_conv_w, rnn_conv_b, rnn_w_a, rnn_b_a, rnn_w_x,
           rnn_b_x, rnn_lam, w_out_1, norm_ffn, ffn_w_up, ffn_conv_w, ffn_conv_b, ffn_w_down, norm_final):
    w = _prep_weights(dict(
        norm_mix_0=norm_mix_0, w_in_0=w_in_0, w_alpha_0=w_alpha_0, b_alpha_0=b_alpha_0,
        gla_norm_0=gla_norm_0, s5_lam_re=s5_lam_re, s5_lam_im=s5_lam_im, s5_log_dt=s5_log_dt,
        s5_b_re=s5_b_re, s5_b_im=s5_b_im, s5_c_re=s5_c_re, s5_c_im=s5_c_im, s5_d=s5_d,
        s5_w_glu=s5_w_glu, s5_b_glu=s5_b_glu, w_out_0=w_out_0, norm_mix_1=norm_mix_1, w_in_1=w_in_1,
        rnn_conv_w=rnn_conv_w, rnn_conv_b=rnn_conv_b, rnn_w_a=rnn_w_a, rnn_b_a=rnn_b_a,
        rnn_w_x=rnn_w_x, rnn_b_x=rnn_b_x, rnn_lam=rnn_lam, w_out_1=w_out_1, norm_ffn=norm_ffn,
        ffn_w_up=ffn_w_up, ffn_conv_w=ffn_conv_w, ffn_conv_b=ffn_conv_b, ffn_w_down=ffn_w_down,
        norm_final=norm_final))

    bp = x_prompt.shape[0]
    bs = x_sample.shape[0]
    tt = _tile_steps()
    mrows = bp * N_META
    assert mrows == bs

    x_side = jnp.concatenate([jnp.repeat(meta_tokens.astype(F32), bp, axis=0),
                              x_sample.reshape(bs, D_MODEL)], axis=0)
    s5_re_s = state_s5_re.reshape(bs, S5_N)
    s5_im_s = state_s5_im.reshape(bs, S5_N)

    gla_in = lambda z: (z[0], z[1], z[2], z[4], z[3])
    gn = w["gla_norm"]

    side = _l0_in(x_side, w)
    og_m, gla_m = _gla_chunk(gla_in(side), gn, jnp.zeros((GLA_HEADS, GLA_DV, bp * GLA_DK), F32), bp, N_META,
                             row_blocks=(2, 0))
    og_s, gla_s = _gla_step(gla_in(side), gn, state_gla, mrows)
    x_m, re_m, im_m, x_s, re_s, im_s = _l0_out_side(side[5], og_m, og_s, x_side, s5_re_s, s5_im_s, w, bp, N_META)
    x_m, fc0_m, x_s, fc0_s = _ffn_side(x_m, x_s, cache_ffn_conv, w, 0, bp, N_META, False)
    x_m, h_m, rc_m, x_s, h_s, rc_s = _l1_side(x_m, x_s, state_rglru, cache_rglru_conv, w, bp, N_META)
    _, fc1_m, ys, fc1_s = _ffn_side(x_m, x_s, cache_ffn_conv, w, 1, bp, N_META, True)

    x, gla_p, re_p, im_p = _l0(x_prompt, gla_m, re_m, im_m, w, tt["l0"], tt["gla"])
    x, fc0_p = _ffn(x, fc0_m, w, 0, bp, tt["ffn"], False)
    x, h_p, rc_p = _l1(x, h_m, rc_m, w, bp, tt["l1"])
    yp, fc1_p = _ffn(x, fc1_m, w, 1, bp, tt["ffn"], True, batch_major_out=True)

    grp = lambda z, nb: z.reshape(nb, S5_GROUPS, S5_P)
    return (yp, ys.reshape(bs, 1, D_MODEL), _gla_state_from_stacked(gla_p), gla_s,
            grp(re_p, bp), grp(re_s, bs), grp(im_p, bp), grp(im_s, bs), h_p, h_s,
            _batch_major(rc_p, bp), rc_s,
            jnp.stack([_batch_major(fc0_p, bp), _batch_major(fc1_p, bp)]), jnp.stack([fc0_s, fc1_s]))
```

```python
import functools

import jax
import jax.numpy as jnp
from jax import lax
from jax.experimental import pallas as pl
from jax.experimental.pallas import tpu as pltpu

F32 = jnp.float32
BF16 = jnp.bfloat16

D_MODEL = 1024
N_META = 16
EPS = 1e-6
F32_TINY = 1.1754944e-38
GLA_HEADS = 4
GLA_DK = 64
GLA_DV = 128
GLA_RANK = 16
GLA_TAU = 16.0
GLA_K = GLA_HEADS * GLA_DK
GLA_V = GLA_HEADS * GLA_DV
S5_GROUPS = 32
S5_H = 16
S5_P = 64
S5_W = S5_GROUPS * S5_H
S5_N = S5_GROUPS * S5_P
RNN_W = 1536
RNN_BLOCKS = 16
RNN_BW = RNN_W // RNN_BLOCKS
RNN_C = 8.0
RNN_CONV = 4
D_FF = 2816
FFN_CONV = 3

LANES = 128
FF_CHUNK = 256
L1_SPLIT = 2
S5_PACK = 8
GATE_WIN = 768
GATE_K = 512
GATE_N = 256
VMEM_LIMIT = 56 * 1024 * 1024


def _rms(x, g):
    return x * lax.rsqrt(jnp.mean(x * x, axis=-1, keepdims=True) + EPS) * g


def _sigmoid(x):
    return 0.5 * jnp.tanh(0.5 * x) + 0.5


def _sqrt_nonneg(t):
    return t * lax.rsqrt(jnp.maximum(t, F32_TINY))


def _mm(a, w):
    return jnp.dot(a.astype(BF16), w, preferred_element_type=F32)


def _const_spec(shape, index=None):
    idx = tuple(index) if index is not None else (0,) * len(shape)
    return pl.BlockSpec(shape, lambda i: idx, pipeline_mode=pl.Buffered(1))


def _row_spec(rows, cols):
    return pl.BlockSpec((rows, cols), lambda i: (i, 0))


def _seq_spec(nb, tt, cols):
    return pl.BlockSpec((nb, tt, cols), lambda i: (0, i, 0))


def _tm_scratch(rows, cols):
    return pltpu.VMEM((cols // LANES, rows, LANES), F32)


def _load_time_major(x_ref, tm_sc):
    nb, tt, cols = x_ref.shape
    for b in range(nb):
        for j in range(cols // LANES):
            tm_sc[j, pl.ds(b, tt, stride=nb), :] = x_ref[b, :, j * LANES:(j + 1) * LANES]
    return jnp.concatenate([tm_sc[j] for j in range(cols // LANES)], axis=-1)


def _store_time_major(o_ref, val, tm_sc):
    if tm_sc is None:
        o_ref[...] = val
        return
    nb, tt, cols = o_ref.shape
    for j in range(cols // LANES):
        tm_sc[j] = val[:, j * LANES:(j + 1) * LANES]
    for b in range(nb):
        for j in range(cols // LANES):
            o_ref[b, :, j * LANES:(j + 1) * LANES] = tm_sc[j, pl.ds(b, tt, stride=nb), :]


def _cache_shape(nb, tt, taps, width):
    return (nb, taps, width) if tt == 1 else (taps * nb, width)


def _params(sem="arbitrary"):
    return pltpu.CompilerParams(dimension_semantics=(sem,), vmem_limit_bytes=VMEM_LIMIT)


def _s5_prep_kernel(lr_ref, li_ref, ldt_ref, brt_ref, bit_ref, are_ref, aim_ref, bbre_ref, bbim_ref):
    lr = lr_ref[...]
    li = li_ref[...]
    dt = jnp.exp(ldt_ref[...])
    mag = jnp.exp(lr * dt)
    ab_re = mag * jnp.cos(li * dt)
    ab_im = mag * jnp.sin(li * dt)
    den = lr * lr + li * li
    nr = ab_re - 1.0
    ni = ab_im
    f_re = (nr * lr + ni * li) / den
    f_im = (ni * lr - nr * li) / den
    are_ref[...] = ab_re
    aim_ref[...] = ab_im
    brt = brt_ref[...]
    bit = bit_ref[...]
    bbre_ref[...] = f_re[:, None, :] * brt - f_im[:, None, :] * bit
    bbim_ref[...] = f_re[:, None, :] * bit + f_im[:, None, :] * brt


def _s5_prep(lam_re, lam_im, log_dt, b_re, b_im):
    g, p, h = b_re.shape
    brt = jnp.transpose(b_re, (0, 2, 1))
    bit = jnp.transpose(b_im, (0, 2, 1))
    return pl.pallas_call(
        _s5_prep_kernel,
        out_shape=(jax.ShapeDtypeStruct((g, p), F32), jax.ShapeDtypeStruct((g, p), F32),
                   jax.ShapeDtypeStruct((g, h, p), F32), jax.ShapeDtypeStruct((g, h, p), F32)),
        name="s5_prep",
    )(lam_re, lam_im, log_dt.reshape(g, 1), brt, bit)


def _is_first():
    return pl.program_id(0) == 0


def _side_spec(rows, cols, block=0):
    return pl.BlockSpec((rows, cols), lambda i: (block, 0))


def _l0_in_body(x, wts, outs):
    g_ref, wqkvg_ref, wu_ref, wlr_ref, wal_ref, bal_ref = wts
    q_ref, k_ref, v_ref, gs_ref, la_ref, u_ref = outs
    xn = _rms(x, g_ref[...]).astype(BF16)
    u_ref[...] = _mm(xn, wu_ref[...])
    lr = _mm(xn, wlr_ref[...])
    pre = _mm(lr, wal_ref[...]) + bal_ref[...]
    la_ref[...] = jax.nn.log_sigmoid(pre) * (1.0 / GLA_TAU)
    k_ref[...] = _mm(xn, wqkvg_ref[:, GLA_K:2 * GLA_K])
    q_ref[...] = _mm(xn, wqkvg_ref[:, 0:GLA_K]) * (GLA_DK ** -0.5)
    v_ref[...] = _mm(xn, wqkvg_ref[:, 2 * GLA_K:2 * GLA_K + GLA_V])
    g = _mm(xn, wqkvg_ref[:, 2 * GLA_K + GLA_V:2 * GLA_K + 2 * GLA_V])
    gs_ref[...] = g * _sigmoid(g)


def _l0_in_kernel(*refs):
    x_ref, wts, outs = refs[0], refs[1:1 + L0_IN_WEIGHTS], refs[1 + L0_IN_WEIGHTS:]
    _l0_in_body(x_ref[...], wts, outs)


L0_IN_COLS = (GLA_K, GLA_K, GLA_V, GLA_V, GLA_K, S5_W)
L0_IN_WEIGHTS = 6


def _l0_in_weight_specs():
    return [_const_spec((1, D_MODEL)), _const_spec((D_MODEL, 2 * GLA_K + 2 * GLA_V)),
            _const_spec((D_MODEL, S5_W)), _const_spec((D_MODEL, LANES)),
            _const_spec((LANES, GLA_K)), _const_spec((1, GLA_K))]


def _l0_in_weights(w):
    return (w["norm_mix_0"], w["w_qkvg"], w["w_u"], w["w_lr"], w["w_alpha"], w["b_alpha"])


def _l0_in(x, w):
    n = x.shape[0]
    return pl.pallas_call(
        _l0_in_kernel,
        grid=(1,),
        in_specs=[_row_spec(n, D_MODEL)] + _l0_in_weight_specs(),
        out_specs=[_row_spec(n, c) for c in L0_IN_COLS],
        out_shape=[jax.ShapeDtypeStruct((n, c), F32) for c in L0_IN_COLS],
        compiler_params=_params(),
        name="l0_in",
    )(x, *_l0_in_weights(w))


def _gla_chunk_prep(ins, b_sc, nb, c):
    q_ref, k_ref, _, la_ref, _ = ins
    rows = nb * c
    seq_mask = nb - 1

    def cum_body(t, run):
        rws = pl.ds(pl.multiple_of(t * nb, nb), nb)
        run = run + la_ref[rws, :]
        b_sc[rws, :] = run
        return run

    bl = lax.fori_loop(0, c, cum_body, jnp.zeros((nb, GLA_K), F32), unroll=True)
    b = b_sc[...]
    q = q_ref[...]
    k = k_ref[...]
    bm = jnp.concatenate([b_sc[pl.ds((c // 2) * nb, nb), :]] * c, axis=0)
    ri = lax.broadcasted_iota(jnp.int32, (rows, rows), 0)
    ci = lax.broadcasted_iota(jnp.int32, (rows, rows), 1)
    xw = nb * GLA_DK
    return dict(
        qa=q * jnp.exp(b - bm), kt=k * jnp.exp(bm - b), qt=q * jnp.exp(b),
        kh=k * jnp.exp(jnp.concatenate([bl] * c, axis=0) - b),
        gam=jnp.exp(bl),
        pair_ok=(ri >= ci) & (((ri - ci) & seq_mask) == 0),
        own_blk=(lax.broadcasted_iota(jnp.int32, (rows, xw), 1) // GLA_DK
                 == (lax.broadcasted_iota(jnp.int32, (rows, xw), 0) & seq_mask)),
        own_blk_seq=(lax.broadcasted_iota(jnp.int32, (nb, xw), 1) // GLA_DK
                     == lax.broadcasted_iota(jnp.int32, (nb, xw), 0)),
        reps=xw // LANES)


def _gla_chunk_head(p, h, ins, gn_ref, o_ref, st_ref):
    v_ref, gs_ref = ins[2], ins[4]

    def head_dup(z):
        blk = z[:, (h // 2) * LANES:(h // 2 + 1) * LANES]
        rolled = pltpu.roll(blk, GLA_DK, axis=1)
        low = lax.broadcasted_iota(jnp.int32, blk.shape, 1) < GLA_DK
        return jnp.where(low, blk, rolled) if h % 2 == 0 else jnp.where(low, rolled, blk)

    def expand(zd, own):
        return jnp.where(own, jnp.concatenate([zd] * p["reps"], axis=1), 0.0)

    vs = slice(h * GLA_DV, (h + 1) * GLA_DV)
    qad, qd, ktd, khd = head_dup(p["qa"]), head_dup(p["qt"]), head_dup(p["kt"]), head_dup(p["kh"])
    vb = v_ref[:, vs].astype(BF16)
    st = st_ref[h]
    att = lax.dot_general(qad[:, :GLA_DK].astype(BF16), ktd[:, :GLA_DK].astype(BF16),
                          (((1,), (1,)), ((), ())), preferred_element_type=F32)
    att = jnp.where(p["pair_ok"], att, 0.0).astype(BF16)
    o = jnp.dot(att, vb, preferred_element_type=F32) + lax.dot_general(
        expand(qd, p["own_blk"]).astype(BF16), st.astype(BF16), (((1,), (1,)), ((), ())),
        preferred_element_type=F32)
    upd = lax.dot_general(vb, expand(khd, p["own_blk"]).astype(BF16), (((0,), (0,)), ((), ())),
                          preferred_element_type=F32)
    gam_row = jnp.sum(expand(head_dup(p["gam"]), p["own_blk_seq"]), axis=0, keepdims=True)
    st_ref[h] = st * gam_row + upd
    o_ref[:, vs] = _rms(o, gn_ref[:, vs]) * gs_ref[:, vs]


def _gla_chunk_body(ins, gn_ref, o_ref, st_ref, b_sc, nb, c):
    p = _gla_chunk_prep(ins, b_sc, nb, c)
    for h in range(GLA_HEADS):
        _gla_chunk_head(p, h, ins, gn_ref, o_ref, st_ref)


def _gla_chunk_kernel(*refs, nb, c):
    ins, gn_ref, st0_ref, o_ref, st_ref, b_sc = refs[0:5], refs[5], refs[6], refs[7], refs[8], refs[9]

    @pl.when(_is_first())
    def _():
        st_ref[...] = st0_ref[...]

    _gla_chunk_body(ins, gn_ref, o_ref, st_ref, b_sc, nb, c)


def _gla_chunk(ins, gn, st0, nb, c, row_blocks=None):
    assert nb & (nb - 1) == 0 and (nb * GLA_DK) % LANES == 0
    rows = nb * c
    n = ins[0].shape[0] // row_blocks[0] if row_blocks else ins[0].shape[0]
    first = (row_blocks[1] * n) // rows if row_blocks else 0
    sshape = (GLA_HEADS, GLA_DV, nb * GLA_DK)
    cols = (GLA_K, GLA_K, GLA_V, GLA_K, GLA_V)
    kern = functools.partial(_gla_chunk_kernel, nb=nb, c=c)
    return pl.pallas_call(
        kern,
        grid=(n // rows,),
        in_specs=[pl.BlockSpec((rows, w_), lambda i: (first + i, 0)) for w_ in cols]
                 + [_const_spec((1, GLA_V)), _const_spec(sshape)],
        out_specs=[_row_spec(rows, GLA_V), pl.BlockSpec(sshape, lambda i: (0, 0, 0))],
        out_shape=[jax.ShapeDtypeStruct((n, GLA_V), F32), jax.ShapeDtypeStruct(sshape, F32)],
        scratch_shapes=[pltpu.VMEM((rows, GLA_K), F32)],
        compiler_params=_params(),
        name="gla_chunk",
    )(*ins, gn, st0)


def _gla_state_from_stacked(st):
    nb = st.shape[2] // GLA_DK
    return jnp.transpose(st.reshape(GLA_HEADS, GLA_DV, nb, GLA_DK), (2, 0, 3, 1))


def _gla_step_kernel(q_ref, k_ref, la_ref, v_ref, gs_ref, gn_ref, s0_ref, o_ref, s_ref):
    qT = q_ref[...].T
    kT = k_ref[...].T
    aT = jnp.exp(la_ref[...]).T
    o_rows = []
    for b in range(q_ref.shape[0]):
        s_new = aT[:, b:b + 1] * s0_ref[b] + kT[:, b:b + 1] * v_ref[b:b + 1, :]
        s_ref[b] = s_new
        o_rows.append(jnp.sum(qT[:, b:b + 1] * s_new, axis=0, keepdims=True))
    o = jnp.concatenate(o_rows, axis=0)
    o_ref[...] = _rms(o, gn_ref[...]) * gs_ref[...]


def _gla_step(side, gn, s0, row0):
    nb = s0.shape[0]
    blk = row0 // nb
    q, k, v, la, gs = side
    heads = lambda z: jnp.transpose(z[row0:row0 + nb].reshape(nb, GLA_HEADS, GLA_DK), (1, 0, 2))
    hspec = pl.BlockSpec((None, nb, GLA_DK), lambda h: (h, 0, 0))
    vspec = pl.BlockSpec((nb, GLA_DV), lambda h: (blk, h))
    ospec = pl.BlockSpec((nb, GLA_DV), lambda h: (0, h))
    sspec = pl.BlockSpec((nb, None, GLA_DK, GLA_DV), lambda h: (0, h, 0, 0))
    return pl.pallas_call(
        _gla_step_kernel,
        grid=(GLA_HEADS,),
        in_specs=[hspec, hspec, hspec, vspec, vspec, pl.BlockSpec((1, GLA_DV), lambda h: (0, h)), sspec],
        out_specs=[ospec, sspec],
        out_shape=[jax.ShapeDtypeStruct((nb, GLA_V), F32),
                   jax.ShapeDtypeStruct((nb, GLA_HEADS, GLA_DK, GLA_DV), F32)],
        compiler_params=_params("parallel"),
        name="gla_step",
    )(heads(q), heads(k), heads(la), v, gs, gn, s0)


def _l0_out_body(u_ref, og_ref, x, wts, xo_ref, xre_ref, xim_ref, sr_sc, si_sc, nb, tt):
    npack = S5_GROUPS // S5_PACK
    for j in range(npack):
        _s5_input(u_ref, wts, sr_sc, si_sc, j)
    ys = []
    for j in range(npack):
        _s5_scan(wts, xre_ref, xim_ref, sr_sc, si_sc, j, nb, tt)
        ys.append(_s5_output(wts, sr_sc, si_sc, j))
    _l0_tail_body(_s5_glu(ys, u_ref, wts), og_ref, x, wts, xo_ref)


S5_KIN = S5_PACK * S5_H
S5_KST = S5_PACK * S5_P


def _s5_input(u_ref, wts, sr_sc, si_sc, j):
    bbre_ref, bbim_ref = wts[0:2]
    cs = slice(j * S5_KST, (j + 1) * S5_KST)
    uj = u_ref[:, j * S5_KIN:(j + 1) * S5_KIN].astype(BF16)
    sr_sc[:, cs] = jnp.dot(uj, bbre_ref[j], preferred_element_type=F32)
    si_sc[:, cs] = jnp.dot(uj, bbim_ref[j], preferred_element_type=F32)


def _s5_scan(wts, xre_ref, xim_ref, sr_sc, si_sc, j, nb, tt):
    are_ref, aim_ref = wts[2:4]
    cs = slice(j * S5_KST, (j + 1) * S5_KST)
    ar = jnp.broadcast_to(are_ref[:, cs], (nb, S5_KST))
    ai = jnp.broadcast_to(aim_ref[:, cs], (nb, S5_KST))
    xr = xre_ref[:, cs]
    xi = xim_ref[:, cs]
    for t in range(tt):
        rows = slice(t * nb, (t + 1) * nb)
        xr, xi = ar * xr - ai * xi + sr_sc[rows, cs], ar * xi + ai * xr + si_sc[rows, cs]
        sr_sc[rows, cs] = xr
        si_sc[rows, cs] = xi
    xre_ref[:, cs] = xr
    xim_ref[:, cs] = xi


def _s5_output(wts, sr_sc, si_sc, j):
    cre_ref, cim_ref = wts[4:6]
    cs = slice(j * S5_KST, (j + 1) * S5_KST)
    return (jnp.dot(sr_sc[:, cs].astype(BF16), cre_ref[j], preferred_element_type=F32)
            - jnp.dot(si_sc[:, cs].astype(BF16), cim_ref[j], preferred_element_type=F32))


def _s5_glu(ys, u_ref, wts):
    d_ref, wglu_ref, bglu_ref = wts[6:9]
    y = jax.nn.gelu(jnp.concatenate(ys, axis=-1) + d_ref[...] * u_ref[...])
    return y * _sigmoid(_mm(y, wglu_ref[...]) + bglu_ref[...])


def _l0_tail_body(y, og_ref, x, wts, xo_ref):
    woa_ref, wob_ref = wts[9:11]
    xo_ref[...] = x + _mm(og_ref[...], woa_ref[...]) + _mm(y, wob_ref[...])


L0_OUT_WEIGHTS = 11


def _l0_out_weight_specs():
    npack = S5_GROUPS // S5_PACK
    kin, kst = S5_PACK * S5_H, S5_PACK * S5_P
    return [_const_spec((npack, kin, kst)), _const_spec((npack, kin, kst)),
            _const_spec((1, S5_N)), _const_spec((1, S5_N)),
            _const_spec((npack, kst, kin)), _const_spec((npack, kst, kin)),
            _const_spec((1, S5_W)), _const_spec((S5_W, S5_W)), _const_spec((1, S5_W)),
            _const_spec((GLA_V, D_MODEL), (0, 0)), _const_spec((S5_W, D_MODEL), (1, 0))]


def _l0_out_weights(w):
    return (w["s5_bbre"], w["s5_bbim"], w["s5_are"], w["s5_aim"], w["s5_cre"], w["s5_cim"],
            w["s5_d"], w["s5_w_glu"], w["s5_b_glu"], w["w_out_0"], w["w_out_0"])


def _l0_out_side_kernel(*refs, nb, tt_meta, nb_s):
    um_ref, ogm_ref, xm_ref, us_ref, ogs_ref, xs_ref, xr0s_ref, xi0s_ref = refs[0:8]
    wts = refs[8:8 + L0_OUT_WEIGHTS]
    xom_ref, xre_ref, xim_ref, xos_ref, xres_ref, xims_ref, sr_sc, si_sc = refs[8 + L0_OUT_WEIGHTS:]
    xre_ref[...] = jnp.zeros(xre_ref.shape, F32)
    xim_ref[...] = jnp.zeros(xim_ref.shape, F32)
    _l0_out_body(um_ref, ogm_ref, xm_ref[...], wts, xom_ref, xre_ref, xim_ref, sr_sc, si_sc, nb, tt_meta)
    xres_ref[...] = xr0s_ref[...]
    xims_ref[...] = xi0s_ref[...]
    _l0_out_body(us_ref, ogs_ref, xs_ref[...], wts, xos_ref, xres_ref, xims_ref, sr_sc, si_sc, nb_s, 1)


def _l0_out_side(u_side, og_meta, og_samp, x_side, xr0_s, xi0_s, w, nb, tt_meta):
    mrows, nb_s = nb * tt_meta, xr0_s.shape[0]
    assert mrows == nb_s
    kern = functools.partial(_l0_out_side_kernel, nb=nb, tt_meta=tt_meta, nb_s=nb_s)
    f32 = lambda *shape: jax.ShapeDtypeStruct(shape, F32)
    full = lambda r, c: pl.BlockSpec((r, c), lambda i: (0, 0))
    return pl.pallas_call(
        kern,
        grid=(1,),
        in_specs=[_side_spec(mrows, S5_W, 0), full(mrows, GLA_V), _side_spec(mrows, D_MODEL, 0),
                  _side_spec(nb_s, S5_W, 1), full(nb_s, GLA_V), _side_spec(nb_s, D_MODEL, 1),
                  full(nb_s, S5_N), full(nb_s, S5_N)] + _l0_out_weight_specs(),
        out_specs=[full(mrows, D_MODEL), full(nb, S5_N), full(nb, S5_N), full(nb_s, D_MODEL),
                   full(nb_s, S5_N), full(nb_s, S5_N)],
        out_shape=[f32(mrows, D_MODEL), f32(nb, S5_N), f32(nb, S5_N), f32(nb_s, D_MODEL),
                   f32(nb_s, S5_N), f32(nb_s, S5_N)],
        scratch_shapes=[pltpu.VMEM((mrows, S5_N), F32)] * 2,
        compiler_params=_params(),
        name="l0_out_side",
    )(u_side, og_meta, x_side, u_side, og_samp, x_side, xr0_s, xi0_s, *_l0_out_weights(w))


def _l0_kernel(*refs, nb, tt, c):
    x_ref, st0_ref, xr0_ref, xi0_ref = refs[0:4]
    n_in = 4 + L0_IN_WEIGHTS
    in_wts, gn_ref, out_wts = refs[4:n_in], refs[n_in], refs[n_in + 1:n_in + 1 + L0_OUT_WEIGHTS]
    rest = refs[n_in + 1 + L0_OUT_WEIGHTS:]
    xo_ref, st_ref, xre_ref, xim_ref = rest[0:4]
    proj_sc, og_sc, b_sc, sr_sc, si_sc, tm_sc = rest[4:10], rest[10], rest[11], rest[12], rest[13], rest[14]

    @pl.when(_is_first())
    def _():
        st_ref[...] = st0_ref[...]
        xre_ref[...] = xr0_ref[...]
        xim_ref[...] = xi0_ref[...]

    x = _load_time_major(x_ref, tm_sc)
    _l0_in_body(x, in_wts, proj_sc)
    q_sc, k_sc, v_sc, gs_sc, la_sc, u_sc = proj_sc
    npack = S5_GROUPS // S5_PACK
    nchunk = tt // c
    units = [(j, hp) for j in range(nchunk) for hp in range(GLA_HEADS // 2)]
    assert len(units) == npack
    for j in range(npack):
        _s5_input(u_sc, out_wts, sr_sc, si_sc, j)
    ys, prep = [], {}
    for i, (j, hp) in enumerate(units):
        _s5_scan(out_wts, xre_ref, xim_ref, sr_sc, si_sc, i, nb, tt)
        rws = pl.ds(j * nb * c, nb * c)
        ins = [r.at[rws] for r in (q_sc, k_sc, v_sc, la_sc, gs_sc)]
        if hp == 0:
            prep = _gla_chunk_prep(ins, b_sc, nb, c)
        for h in (2 * hp, 2 * hp + 1):
            _gla_chunk_head(prep, h, ins, gn_ref, og_sc.at[rws], st_ref)
        ys.append(_s5_output(out_wts, sr_sc, si_sc, i))
    _l0_tail_body(_s5_glu(ys, u_sc, out_wts), og_sc, x, out_wts, xo_ref)


def _l0(x, st0, xr0, xi0, w, tt, c):
    nb, nt, _ = x.shape
    n, rows = nb * nt, nb * tt
    assert nb & (nb - 1) == 0 and (nb * GLA_DK) % LANES == 0 and tt % c == 0
    sshape = (GLA_HEADS, GLA_DV, nb * GLA_DK)
    kern = functools.partial(_l0_kernel, nb=nb, tt=tt, c=c)
    st_spec = pl.BlockSpec((nb, S5_N), lambda i: (0, 0))
    f32 = lambda *shape: jax.ShapeDtypeStruct(shape, F32)
    return pl.pallas_call(
        kern,
        grid=(n // rows,),
        in_specs=[_seq_spec(nb, tt, D_MODEL), _const_spec(sshape), _const_spec((nb, S5_N)),
                  _const_spec((nb, S5_N))] + _l0_in_weight_specs() + [_const_spec((1, GLA_V))]
                 + _l0_out_weight_specs(),
        out_specs=[_row_spec(rows, D_MODEL), pl.BlockSpec(sshape, lambda i: (0, 0, 0)), st_spec, st_spec],
        out_shape=[f32(n, D_MODEL), f32(*sshape), f32(nb, S5_N), f32(nb, S5_N)],
        scratch_shapes=[pltpu.VMEM((rows, cols), F32) for cols in L0_IN_COLS]
                       + [pltpu.VMEM((rows, GLA_V), F32), pltpu.VMEM((nb * c, GLA_K), F32),
                          pltpu.VMEM((rows, S5_N), F32), pltpu.VMEM((rows, S5_N), F32),
                          _tm_scratch(rows, D_MODEL)],
        compiler_params=_params(),
        name="l0_mixer",
    )(x, st0, xr0, xi0, *_l0_in_weights(w), w["gla_norm"], *_l0_out_weights(w))


def _ffn_body(x, wts, c0_ref, c_ref, hm_sc, nb, tt, final):
    g_ref, wg_ref, wv_ref, cw_ref, cb_ref, wd_ref, gf_ref = wts
    rows = nb * tt
    xn = _rms(x, g_ref[...]).astype(BF16)
    for ci in range(D_FF // FF_CHUNK):
        cs = slice(ci * FF_CHUNK, (ci + 1) * FF_CHUNK)
        gate = jnp.dot(xn, wg_ref[:, cs], preferred_element_type=F32)
        val = jnp.dot(xn, wv_ref[:, cs], preferred_element_type=F32)
        if tt == 1:
            taps = [c0_ref[:, j, cs] for j in range(FFN_CONV - 1)] + [gate]
            for j in range(FFN_CONV - 1):
                c_ref[:, j, cs] = taps[j + 1]
        else:
            ext = jnp.concatenate([c_ref[:, cs], gate], axis=0)
            taps = [ext[j * nb:j * nb + rows] for j in range(FFN_CONV)]
            c_ref[:, cs] = ext[tt * nb:(tt + FFN_CONV - 1) * nb]
        y = cb_ref[:, cs] + taps[0] * cw_ref[0:1, cs]
        for j in range(1, FFN_CONV):
            y = y + taps[j] * cw_ref[j:j + 1, cs]
        hm_sc[:, cs] = (jax.nn.gelu(y) * val).astype(BF16)
    out = x + jnp.dot(hm_sc[...], wd_ref[...], preferred_element_type=F32)
    return _rms(out, gf_ref[...]) if final else out


FFN_WEIGHTS = 7


def _ffn_weight_specs(layer):
    return [_const_spec((None, 1, D_MODEL), (layer, 0, 0)),
            _const_spec((None, D_MODEL, D_FF), (layer, 0, 0)),
            _const_spec((None, D_MODEL, D_FF), (layer, 0, 1)),
            _const_spec((None, FFN_CONV, D_FF), (layer, 0, 0)),
            _const_spec((None, 1, D_FF), (layer, 0, 0)),
            _const_spec((None, D_FF, D_MODEL), (layer, 0, 0)), _const_spec((1, D_MODEL))]


def _ffn_weights(w):
    return (w["norm_ffn"], w["ffn_w_up"], w["ffn_w_up"], w["ffn_conv_w"], w["ffn_conv_b"], w["ffn_w_down"],
            w["norm_final"])


def _ffn_kernel(*refs, nb, tt, final, batch_major_out):
    x_ref, c0_ref = refs[0:2]
    wts = refs[2:2 + FFN_WEIGHTS]
    xo_ref, c_ref, hm_sc = refs[2 + FFN_WEIGHTS:5 + FFN_WEIGHTS]
    tm_sc = refs[5 + FFN_WEIGHTS] if batch_major_out else None

    @pl.when(_is_first())
    def _():
        c_ref[...] = c0_ref[...]

    _store_time_major(xo_ref, _ffn_body(x_ref[...], wts, None, c_ref, hm_sc, nb, tt, final), tm_sc)


def _ffn(x, c0, w, layer, nb, tt, final, batch_major_out=False):
    n = x.shape[0]
    rows = nb * tt
    kern = functools.partial(_ffn_kernel, nb=nb, tt=tt, final=final, batch_major_out=batch_major_out)
    if batch_major_out:
        o_spec, o_shape = _seq_spec(nb, tt, D_MODEL), (nb, n // nb, D_MODEL)
    else:
        o_spec, o_shape = _row_spec(rows, D_MODEL), (n, D_MODEL)
    cshape = _cache_shape(nb, tt, FFN_CONV - 1, D_FF)
    f32 = lambda *shape: jax.ShapeDtypeStruct(shape, F32)
    return pl.pallas_call(
        kern,
        grid=(n // rows,),
        in_specs=[_row_spec(rows, D_MODEL), _const_spec(cshape)] + _ffn_weight_specs(layer),
        out_specs=[o_spec, pl.BlockSpec(cshape, lambda i: (0, 0))],
        out_shape=[f32(*o_shape), f32(*cshape)],
        scratch_shapes=[pltpu.VMEM((rows, D_FF), BF16)]
                       + ([_tm_scratch(rows, D_MODEL)] if batch_major_out else []),
        compiler_params=_params(),
        name="ffn%d" % layer,
    )(x, c0, *_ffn_weights(w))


def _ffn_side_kernel(*refs, nb, tt_meta, nb_s, final):
    xm_ref, xs_ref, c0s_ref = refs[0:3]
    wts = refs[3:3 + FFN_WEIGHTS]
    xom_ref, c_ref, xos_ref, cs_ref, hm_sc = refs[3 + FFN_WEIGHTS:]
    c_ref[...] = jnp.zeros(c_ref.shape, F32)
    xom_ref[...] = _ffn_body(xm_ref[...], wts, None, c_ref, hm_sc, nb, tt_meta, final)
    xos_ref[...] = _ffn_body(xs_ref[...], wts, c0s_ref, cs_ref, hm_sc, nb_s, 1, final)


def _ffn_side(x_meta, x_samp, c0_s, w, layer, nb, tt_meta, final):
    mrows, nb_s = nb * tt_meta, x_samp.shape[0]
    assert mrows == nb_s
    kern = functools.partial(_ffn_side_kernel, nb=nb, tt_meta=tt_meta, nb_s=nb_s, final=final)
    cshape = _cache_shape(nb, tt_meta, FFN_CONV - 1, D_FF)
    cs_shape = _cache_shape(nb_s, 1, FFN_CONV - 1, D_FF)
    f32 = lambda *shape: jax.ShapeDtypeStruct(shape, F32)
    full = lambda shape: pl.BlockSpec(shape, lambda i: (0,) * len(shape))
    return pl.pallas_call(
        kern,
        grid=(1,),
        in_specs=[full((mrows, D_MODEL)), full((nb_s, D_MODEL)),
                  pl.BlockSpec((None,) + cs_shape, lambda i: (layer, 0, 0, 0))] + _ffn_weight_specs(layer),
        out_specs=[full((mrows, D_MODEL)), full(cshape), full((nb_s, D_MODEL)), full(cs_shape)],
        out_shape=[f32(mrows, D_MODEL), f32(*cshape), f32(nb_s, D_MODEL), f32(*cs_shape)],
        scratch_shapes=[pltpu.VMEM((mrows, D_FF), BF16)],
        compiler_params=_params(),
        name="ffn%d_side" % layer,
    )(x_meta, x_samp, c0_s, *_ffn_weights(w))


def _l1_body(x_ref, wts, xo_ref, h_ref, c0_ref, c_ref, a_sc, b_sc, nb, tt):
    g_ref, wgt_ref, wxr_ref, cw_ref, cb_ref, wa_ref, ba_ref, wx_ref, bx_ref, lam_ref, wo_ref = wts
    sp = jax.nn.softplus(-lam_ref[...])
    nsplit = L1_SPLIT if tt % L1_SPLIT == 0 else 1
    th = tt // nsplit
    rows = nb * th
    carry = c_ref[...] if tt > 1 else None
    xs, xns, ggs = [], [], []

    for part in range(nsplit):
        prow = slice(part * rows, (part + 1) * rows)
        x = x_ref[prow, :]
        xn = _rms(x, g_ref[...]).astype(BF16)
        xs.append(x)
        xns.append(xn)
        xr = jnp.dot(xn, wxr_ref[...], preferred_element_type=F32)
        if tt == 1:
            taps = [c0_ref[:, j, :] for j in range(RNN_CONV - 1)] + [xr]
            for j in range(RNN_CONV - 1):
                c_ref[:, j, :] = taps[j + 1]
        else:
            ext = jnp.concatenate([carry, xr], axis=0)
            taps = [ext[j * nb:j * nb + rows] for j in range(RNN_CONV)]
            carry = ext[th * nb:(th + RNN_CONV - 1) * nb]
        xc = cb_ref[...] + taps[0] * cw_ref[0:1, :]
        for j in range(1, RNN_CONV):
            xc = xc + taps[j] * cw_ref[j:j + 1, :]

        xcb = xc.astype(BF16)
        rs, gs = [], []
        for wi in range(RNN_W // GATE_WIN):
            for ni in range(GATE_WIN // GATE_N):
                k0 = wi * GATE_WIN + ni * LANES
                lhs = xcb[:, k0:k0 + GATE_K]
                rs.append(jnp.dot(lhs, wa_ref[wi, ni], preferred_element_type=F32))
                gs.append(jnp.dot(lhs, wx_ref[wi, ni], preferred_element_type=F32))
        r = _sigmoid(jnp.concatenate(rs, axis=-1) + ba_ref[...])
        ig = _sigmoid(jnp.concatenate(gs, axis=-1) + bx_ref[...])
        log_a = (-RNN_C) * r * sp
        a = jnp.exp(log_a)
        a_sc[prow, :] = a
        b_sc[prow, :] = _sqrt_nonneg(jnp.tanh(-log_a) * (a * a + 1.0)) * (ig * xc)
    if tt > 1:
        c_ref[...] = carry

    for part in range(nsplit):
        ggs.append(jax.nn.gelu(jnp.dot(xns[part], wgt_ref[...], preferred_element_type=F32)))

    h = h_ref[...]
    for part in range(nsplit):
        prow = slice(part * rows, (part + 1) * rows)
        for t in range(part * th, (part + 1) * th):
            rws = slice(t * nb, (t + 1) * nb)
            h = a_sc[rws, :] * h + b_sc[rws, :]
            b_sc[rws, :] = h
        xo_ref[prow, :] = xs[part] + _mm(b_sc[prow, :] * ggs[part], wo_ref[...])
    h_ref[...] = h


L1_WEIGHTS = 11


def _l1_weight_specs():
    gshape = (RNN_W // GATE_WIN, GATE_WIN // GATE_N, GATE_K, GATE_N)
    return [_const_spec((1, D_MODEL)),
            _const_spec((D_MODEL, RNN_W), (0, 0)), _const_spec((D_MODEL, RNN_W), (0, 1)),
            _const_spec((RNN_CONV, RNN_W)), _const_spec((1, RNN_W)),
            _const_spec(gshape), _const_spec((1, RNN_W)),
            _const_spec(gshape), _const_spec((1, RNN_W)),
            _const_spec((1, RNN_W)), _const_spec((RNN_W, D_MODEL))]


def _l1_weights(w):
    return (w["norm_mix_1"], w["w_in_1"], w["w_in_1"], w["rnn_conv_w"], w["rnn_conv_b"], w["rnn_wa"],
            w["rnn_b_a"], w["rnn_wx"], w["rnn_b_x"], w["rnn_lam"], w["w_out_1"])


def _l1_kernel(*refs, nb, tt):
    x_ref, h0_ref, c0_ref = refs[0:3]
    wts = refs[3:3 + L1_WEIGHTS]
    xo_ref, h_ref, c_ref, a_sc, b_sc = refs[3 + L1_WEIGHTS:]

    @pl.when(_is_first())
    def _():
        h_ref[...] = h0_ref[...]
        c_ref[...] = c0_ref[...]

    _l1_body(x_ref, wts, xo_ref, h_ref, None, c_ref, a_sc, b_sc, nb, tt)


def _l1(x, h0, c0, w, nb, tt):
    n = x.shape[0]
    rows = nb * tt
    kern = functools.partial(_l1_kernel, nb=nb, tt=tt)
    cshape = _cache_shape(nb, tt, RNN_CONV - 1, RNN_W)
    f32 = lambda *shape: jax.ShapeDtypeStruct(shape, F32)
    return pl.pallas_call(
        kern,
        grid=(n // rows,),
        in_specs=[_row_spec(rows, D_MODEL), _const_spec((nb, RNN_W)), _const_spec(cshape)] + _l1_weight_specs(),
        out_specs=[_row_spec(rows, D_MODEL), pl.BlockSpec((nb, RNN_W), lambda i: (0, 0)),
                   pl.BlockSpec(cshape, lambda i: (0, 0))],
        out_shape=[f32(n, D_MODEL), f32(nb, RNN_W), f32(*cshape)],
        scratch_shapes=[pltpu.VMEM((rows, RNN_W), F32)] * 2,
        compiler_params=_params(),
        name="l1_mixer",
    )(x, h0, c0, *_l1_weights(w))


def _l1_side_kernel(*refs, nb, tt_meta, nb_s):
    xm_ref, xs_ref, h0s_ref, c0s_ref = refs[0:4]
    wts = refs[4:4 + L1_WEIGHTS]
    xom_ref, h_ref, c_ref, xos_ref, hs_ref, cs_ref, a_sc, b_sc = refs[4 + L1_WEIGHTS:]
    h_ref[...] = jnp.zeros(h_ref.shape, F32)
    c_ref[...] = jnp.zeros(c_ref.shape, F32)
    _l1_body(xm_ref, wts, xom_ref, h_ref, None, c_ref, a_sc, b_sc, nb, tt_meta)
    hs_ref[...] = h0s_ref[...]
    _l1_body(xs_ref, wts, xos_ref, hs_ref, c0s_ref, cs_ref, a_sc, b_sc, nb_s, 1)


def _l1_side(x_meta, x_samp, h0_s, c0_s, w, nb, tt_meta):
    mrows, nb_s = nb * tt_meta, x_samp.shape[0]
    assert mrows == nb_s
    kern = functools.partial(_l1_side_kernel, nb=nb, tt_meta=tt_meta, nb_s=nb_s)
    cshape = _cache_shape(nb, tt_meta, RNN_CONV - 1, RNN_W)
    cs_shape = _cache_shape(nb_s, 1, RNN_CONV - 1, RNN_W)
    f32 = lambda *shape: jax.ShapeDtypeStruct(shape, F32)
    full = lambda shape: pl.BlockSpec(shape, lambda i: (0,) * len(shape))
    return pl.pallas_call(
        kern,
        grid=(1,),
        in_specs=[full((mrows, D_MODEL)), full((nb_s, D_MODEL)), full((nb_s, RNN_W)), full(cs_shape)]
                 + _l1_weight_specs(),
        out_specs=[full((mrows, D_MODEL)), full((nb, RNN_W)), full(cshape), full((nb_s, D_MODEL)),
                   full((nb_s, RNN_W)), full(cs_shape)],
        out_shape=[f32(mrows, D_MODEL), f32(nb, RNN_W), f32(*cshape), f32(nb_s, D_MODEL), f32(nb_s, RNN_W),
                   f32(*cs_shape)],
        scratch_shapes=[pltpu.VMEM((mrows, RNN_W), F32)] * 2,
        compiler_params=_params(),
        name="l1_side",
    )(x_meta, x_samp, h0_s, c0_s, *_l1_weights(w))


def _pack_gate_kernel(wa_ref, wx_ref, oa_ref, ox_ref):
    tiles_per_win = GATE_WIN // GATE_N
    for w_ref, o_ref in ((wa_ref, oa_ref), (wx_ref, ox_ref)):
        o_ref[...] = jnp.zeros(o_ref.shape, o_ref.dtype)
        for n in range(RNN_BLOCKS):
            pos = n * RNN_BW
            wi = pos // GATE_WIN
            for ni in range(tiles_per_win):
                k0 = wi * GATE_WIN + ni * LANES
                n0 = wi * GATE_WIN + ni * GATE_N
                lo, hi = max(pos, n0), min(pos + RNN_BW, n0 + GATE_N)
                if lo < hi:
                    o_ref[wi, ni, pos - k0:pos - k0 + RNN_BW, lo - n0:hi - n0] = (
                        w_ref[n][:, lo - pos:hi - pos].astype(o_ref.dtype))


def _pack_gates(wa, wx):
    gshape = (RNN_W // GATE_WIN, GATE_WIN // GATE_N, GATE_K, GATE_N)
    return pl.pallas_call(
        _pack_gate_kernel,
        out_shape=(jax.ShapeDtypeStruct(gshape, BF16), jax.ShapeDtypeStruct(gshape, BF16)),
        name="pack_gates",
    )(wa, wx)


def _prep_weights(p):
    w = {}
    row = lambda v: v.reshape(1, -1).astype(F32)
    w_in = p["w_in_0"]
    c = 2 * GLA_K + 2 * GLA_V
    w["w_qkvg"] = w_in[:, :c].astype(BF16)
    w["w_lr"] = jnp.pad(w_in[:, c:c + GLA_RANK], ((0, 0), (0, LANES - GLA_RANK))).astype(BF16)
    c += GLA_RANK
    w["w_u"] = w_in[:, c:c + S5_W].astype(BF16)
    w["w_alpha"] = jnp.pad(p["w_alpha_0"], ((0, LANES - GLA_RANK), (0, 0))).astype(BF16)
    w["b_alpha"] = row(p["b_alpha_0"])
    w["norm_mix_0"] = row(p["norm_mix_0"])
    w["gla_norm"] = row(p["gla_norm_0"])

    are, aim, bbre, bbim = _s5_prep(p["s5_lam_re"], p["s5_lam_im"], p["s5_log_dt"], p["s5_b_re"],
                                    p["s5_b_im"])
    npack = S5_GROUPS // S5_PACK
    eye = jnp.eye(S5_PACK, dtype=F32)[None, :, None, :, None]
    grouped = lambda m: m.reshape(npack, S5_PACK, S5_H, S5_P)
    pack_b = lambda m: (grouped(m)[:, :, :, None, :] * eye).reshape(
        npack, S5_PACK * S5_H, S5_PACK * S5_P).astype(BF16)
    pack_c = lambda m: (jnp.swapaxes(grouped(m), 2, 3)[:, :, :, None, :] * eye).reshape(
        npack, S5_PACK * S5_P, S5_PACK * S5_H).astype(BF16)
    w["s5_are"] = are.reshape(1, S5_N)
    w["s5_aim"] = aim.reshape(1, S5_N)
    w["s5_bbre"] = pack_b(bbre)
    w["s5_bbim"] = pack_b(bbim)
    w["s5_cre"] = pack_c(p["s5_c_re"])
    w["s5_cim"] = pack_c(p["s5_c_im"])
    w["s5_d"] = row(p["s5_d"])
    w["s5_w_glu"] = p["s5_w_glu"].astype(BF16)
    w["s5_b_glu"] = row(p["s5_b_glu"])
    w["w_out_0"] = p["w_out_0"].astype(BF16)

    w["norm_mix_1"] = row(p["norm_mix_1"])
    w["w_in_1"] = p["w_in_1"].astype(BF16)
    w["rnn_conv_w"] = p["rnn_conv_w"].astype(F32)
    w["rnn_conv_b"] = row(p["rnn_conv_b"])
    w["rnn_wa"], w["rnn_wx"] = _pack_gates(p["rnn_w_a"], p["rnn_w_x"])
    w["rnn_b_a"] = row(p["rnn_b_a"])
    w["rnn_b_x"] = row(p["rnn_b_x"])
    w["rnn_lam"] = row(p["rnn_lam"])
    w["w_out_1"] = p["w_out_1"].astype(BF16)

    depth = p["norm_ffn"].shape[0]
    w["norm_ffn"] = p["norm_ffn"].reshape(depth, 1, D_MODEL)
    w["ffn_w_up"] = p["ffn_w_up"].astype(BF16)
    w["ffn_conv_w"] = p["ffn_conv_w"]
    w["ffn_conv_b"] = p["ffn_conv_b"].reshape(depth, 1, D_FF)
    w["ffn_w_down"] = p["ffn_w_down"].astype(BF16)
    w["norm_final"] = row(p["norm_final"])
    return w


def _tile_steps():
    return dict(l0=64, gla=32, ffn=64, l1=64)


def _batch_major(cache, nb):
    jb, c = cache.shape
    return jnp.transpose(cache.reshape(jb // nb, nb, c), (1, 0, 2))


def kernel(x_prompt, x_sample, state_gla, state_s5_re, state_s5_im, state_rglru, cache_rglru_conv,
           cache_ffn_conv, meta_tokens, norm_mix_0, w_in_0, w_alpha_0, b_alpha_0, gla_norm_0,
           s5_lam_re, s5_lam_im, s5_log_dt, s5_b_re, s5_b_im, s5_c_re, s5_c_im, s5_d, s5_w_glu,
           s5_b_glu, w_out_0, norm_mix_1, w_in_1, rnn_conv_w, rnn_conv_b, rnn_w_a, rnn_b_a, rnn_w_x,
           rnn_b_x, rnn_lam, w_out_1, norm_ffn, ffn_w_up, ffn_conv_w, ffn_conv_b, ffn_w_down, norm_final):
    w = _prep_weights(dict(
        norm_mix_0=norm_mix_0, w_in_0=w_in_0, w_alpha_0=w_alpha_0, b_alpha_0=b_alpha_0,
        gla_norm_0=gla_norm_0, s5_lam_re=s5_lam_re, s5_lam_im=s5_lam_im, s5_log_dt=s5_log_dt,
        s5_b_re=s5_b_re, s5_b_im=s5_b_im, s5_c_re=s5_c_re, s5_c_im=s5_c_im, s5_d=s5_d,
        s5_w_glu=s5_w_glu, s5_b_glu=s5_b_glu, w_out_0=w_out_0, norm_mix_1=norm_mix_1, w_in_1=w_in_1,
        rnn_conv_w=rnn_conv_w, rnn_conv_b=rnn_conv_b, rnn_w_a=rnn_w_a, rnn_b_a=rnn_b_a,
        rnn_w_x=rnn_w_x, rnn_b_x=rnn_b_x, rnn_lam=rnn_lam, w_out_1=w_out_1, norm_ffn=norm_ffn,
        ffn_w_up=ffn_w_up, ffn_conv_w=ffn_conv_w, ffn_conv_b=ffn_conv_b, ffn_w_down=ffn_w_down,
        norm_final=norm_final))

    bp = x_prompt.shape[0]
    bs = x_sample.shape[0]
    tt = _tile_steps()
    mrows = bp * N_META
    assert mrows == bs

    x_side = jnp.concatenate([jnp.repeat(meta_tokens.astype(F32), bp, axis=0),
                              x_sample.reshape(bs, D_MODEL)], axis=0)
    s5_re_s = state_s5_re.reshape(bs, S5_N)
    s5_im_s = state_s5_im.reshape(bs, S5_N)

    gla_in = lambda z: (z[0], z[1], z[2], z[4], z[3])
    gn = w["gla_norm"]

    side = _l0_in(x_side, w)
    og_m, gla_m = _gla_chunk(gla_in(side), gn, jnp.zeros((GLA_HEADS, GLA_DV, bp * GLA_DK), F32), bp, N_META,
                             row_blocks=(2, 0))
    og_s, gla_s = _gla_step(gla_in(side), gn, state_gla, mrows)
    x_m, re_m, im_m, x_s, re_s, im_s = _l0_out_side(side[5], og_m, og_s, x_side, s5_re_s, s5_im_s, w, bp, N_META)
    x_m, fc0_m, x_s, fc0_s = _ffn_side(x_m, x_s, cache_ffn_conv, w, 0, bp, N_META, False)
    x_m, h_m, rc_m, x_s, h_s, rc_s = _l1_side(x_m, x_s, state_rglru, cache_rglru_conv, w, bp, N_META)
    _, fc1_m, ys, fc1_s = _ffn_side(x_m, x_s, cache_ffn_conv, w, 1, bp, N_META, True)

    x, gla_p, re_p, im_p = _l0(x_prompt, gla_m, re_m, im_m, w, tt["l0"], tt["gla"])
    x, fc0_p = _ffn(x, fc0_m, w, 0, bp, tt["ffn"], False)
    x, h_p, rc_p = _l1(x, h_m, rc_m, w, bp, tt["l1"])
    yp, fc1_p = _ffn(x, fc1_m, w, 1, bp, tt["ffn"], True, batch_major_out=True)

    grp = lambda z, nb: z.reshape(nb, S5_GROUPS, S5_P)
    return (yp, ys.reshape(bs, 1, D_MODEL), _gla_state_from_stacked(gla_p), gla_s,
            grp(re_p, bp), grp(re_s, bs), grp(im_p, bp), grp(im_s, bs), h_p, h_s,
            _batch_major(rc_p, bp), rc_s,
            jnp.stack([_batch_major(fc0_p, bp), _batch_major(fc1_p, bp)]), jnp.stack([fc0_s, fc1_s]))
```

```python
import functools

import jax
import jax.numpy as jnp
from jax import lax
from jax.experimental import pallas as pl
from jax.experimental.pallas import tpu as pltpu

F32 = jnp.float32
BF16 = jnp.bfloat16

D_MODEL = 1024
N_META = 16
EPS = 1e-6
F32_TINY = 1.1754944e-38
GLA_HEADS = 4
GLA_DK = 64
GLA_DV = 128
GLA_RANK = 16
GLA_TAU = 16.0
GLA_K = GLA_HEADS * GLA_DK
GLA_V = GLA_HEADS * GLA_DV
S5_GROUPS = 32
S5_H = 16
S5_P = 64
S5_W = S5_GROUPS * S5_H
S5_N = S5_GROUPS * S5_P
RNN_W = 1536
RNN_BLOCKS = 16
RNN_BW = RNN_W // RNN_BLOCKS
RNN_C = 8.0
RNN_CONV = 4
D_FF = 2816
FFN_CONV = 3

LANES = 128
FF_CHUNK = 256
L1_SPLIT = 2
S5_PACK = 8
GATE_WIN = 768
GATE_K = 512
GATE_N = 256
VMEM_LIMIT = 56 * 1024 * 1024


def _rms(x, g):
    return x * lax.rsqrt(jnp.mean(x * x, axis=-1, keepdims=True) + EPS) * g


def _sigmoid(x):
    return 0.5 * jnp.tanh(0.5 * x) + 0.5


def _sqrt_nonneg(t):
    return t * lax.rsqrt(jnp.maximum(t, F32_TINY))


def _mm(a, w):
    return jnp.dot(a.astype(BF16), w, preferred_element_type=F32)


def _const_spec(shape, index=None):
    idx = tuple(index) if index is not None else (0,) * len(shape)
    return pl.BlockSpec(shape, lambda i: idx, pipeline_mode=pl.Buffered(1))


def _row_spec(rows, cols):
    return pl.BlockSpec((rows, cols), lambda i: (i, 0))


def _seq_spec(nb, tt, cols):
    return pl.BlockSpec((nb, tt, cols), lambda i: (0, i, 0))


def _tm_scratch(rows, cols):
    return pltpu.VMEM((cols // LANES, rows, LANES), F32)


def _load_time_major(x_ref, tm_sc):
    nb, tt, cols = x_ref.shape
    for b in range(nb):
        for j in range(cols // LANES):
            tm_sc[j, pl.ds(b, tt, stride=nb), :] = x_ref[b, :, j * LANES:(j + 1) * LANES]
    return jnp.concatenate([tm_sc[j] for j in range(cols // LANES)], axis=-1)


def _store_time_major(o_ref, val, tm_sc):
    if tm_sc is None:
        o_ref[...] = val
        return
    nb, tt, cols = o_ref.shape
    for j in range(cols // LANES):
        tm_sc[j] = val[:, j * LANES:(j + 1) * LANES]
    for b in range(nb):
        for j in range(cols // LANES):
            o_ref[b, :, j * LANES:(j + 1) * LANES] = tm_sc[j, pl.ds(b, tt, stride=nb), :]


def _cache_shape(nb, tt, taps, width):
    return (nb, taps, width) if tt == 1 else (taps * nb, width)


def _params(sem="arbitrary"):
    return pltpu.CompilerParams(dimension_semantics=(sem,), vmem_limit_bytes=VMEM_LIMIT)


def _s5_prep_kernel(lr_ref, li_ref, ldt_ref, brt_ref, bit_ref, are_ref, aim_ref, bbre_ref, bbim_ref):
    lr = lr_ref[...]
    li = li_ref[...]
    dt = jnp.exp(ldt_ref[...])
    mag = jnp.exp(lr * dt)
    ab_re = mag * jnp.cos(li * dt)
    ab_im = mag * jnp.sin(li * dt)
    den = lr * lr + li * li
    nr = ab_re - 1.0
    ni = ab_im
    f_re = (nr * lr + ni * li) / den
    f_im = (ni * lr - nr * li) / den
    are_ref[...] = ab_re
    aim_ref[...] = ab_im
    brt = brt_ref[...]
    bit = bit_ref[...]
    bbre_ref[...] = f_re[:, None, :] * brt - f_im[:, None, :] * bit
    bbim_ref[...] = f_re[:, None, :] * bit + f_im[:, None, :] * brt


def _s5_prep(lam_re, lam_im, log_dt, b_re, b_im):
    g, p, h = b_re.shape
    brt = jnp.transpose(b_re, (0, 2, 1))
    bit = jnp.transpose(b_im, (0, 2, 1))
    return pl.pallas_call(
        _s5_prep_kernel,
        out_shape=(jax.ShapeDtypeStruct((g, p), F32), jax.ShapeDtypeStruct((g, p), F32),
                   jax.ShapeDtypeStruct((g, h, p), F32), jax.ShapeDtypeStruct((g, h, p), F32)),
        name="s5_prep",
    )(lam_re, lam_im, log_dt.reshape(g, 1), brt, bit)


def _is_first():
    return pl.program_id(0) == 0


def _side_spec(rows, cols, block=0):
    return pl.BlockSpec((rows, cols), lambda i: (block, 0))


def _l0_in_body(x, wts, outs):
    g_ref, wq_ref, wk_ref, wv_ref, wg_ref, wu_ref, wlr_ref, wal_ref, bal_ref = wts
    q_ref, k_ref, v_ref, gs_ref, la_ref, u_ref = outs
    xn = _rms(x, g_ref[...]).astype(BF16)
    u_ref[...] = _mm(xn, wu_ref[...])
    lr = _mm(xn, wlr_ref[...])
    pre = _mm(lr, wal_ref[...]) + bal_ref[...]
    la_ref[...] = jax.nn.log_sigmoid(pre) * (1.0 / GLA_TAU)
    k_ref[...] = _mm(xn, wk_ref[...])
    q_ref[...] = _mm(xn, wq_ref[...]) * (GLA_DK ** -0.5)
    v_ref[...] = _mm(xn, wv_ref[...])
    g = _mm(xn, wg_ref[...])
    gs_ref[...] = g * _sigmoid(g)


def _l0_in_kernel(*refs):
    x_ref, wts, outs = refs[0], refs[1:1 + L0_IN_WEIGHTS], refs[1 + L0_IN_WEIGHTS:]
    _l0_in_body(x_ref[...], wts, outs)


L0_IN_COLS = (GLA_K, GLA_K, GLA_V, GLA_V, GLA_K, S5_W)
L0_IN_WEIGHTS = 9


def _l0_in_weight_specs():
    return [_const_spec((1, D_MODEL)),
            _const_spec((D_MODEL, GLA_K), (0, 0)), _const_spec((D_MODEL, GLA_K), (0, 1)),
            _const_spec((D_MODEL, GLA_V), (0, 1)), _const_spec((D_MODEL, GLA_V), (0, 2)),
            _const_spec((D_MODEL, S5_W)), _const_spec((D_MODEL, LANES)),
            _const_spec((LANES, GLA_K)), _const_spec((1, GLA_K))]


def _l0_in_weights(w):
    return (w["norm_mix_0"], w["w_in_0"], w["w_in_0"], w["w_in_0"], w["w_in_0"], w["w_u"], w["w_lr"],
            w["w_alpha"], w["b_alpha"])


def _l0_in(x, w):
    n = x.shape[0]
    return pl.pallas_call(
        _l0_in_kernel,
        grid=(1,),
        in_specs=[_row_spec(n, D_MODEL)] + _l0_in_weight_specs(),
        out_specs=[_row_spec(n, c) for c in L0_IN_COLS],
        out_shape=[jax.ShapeDtypeStruct((n, c), F32) for c in L0_IN_COLS],
        compiler_params=_params(),
        name="l0_in",
    )(x, *_l0_in_weights(w))


def _gla_chunk_prep(ins, b_sc, nb, c):
    q_ref, k_ref, _, la_ref, _ = ins
    rows = nb * c
    seq_mask = nb - 1

    def cum_body(t, run):
        rws = pl.ds(pl.multiple_of(t * nb, nb), nb)
        run = run + la_ref[rws, :]
        b_sc[rws, :] = run
        return run

    bl = lax.fori_loop(0, c, cum_body, jnp.zeros((nb, GLA_K), F32), unroll=True)
    b = b_sc[...]
    q = q_ref[...]
    k = k_ref[...]
    bm = jnp.concatenate([b_sc[pl.ds((c // 2) * nb, nb), :]] * c, axis=0)
    ri = lax.broadcasted_iota(jnp.int32, (rows, rows), 0)
    ci = lax.broadcasted_iota(jnp.int32, (rows, rows), 1)
    xw = nb * GLA_DK
    return dict(
        qa=q * jnp.exp(b - bm), kt=k * jnp.exp(bm - b), qt=q * jnp.exp(b),
        kh=k * jnp.exp(jnp.concatenate([bl] * c, axis=0) - b),
        gam=jnp.exp(bl),
        pair_ok=(ri >= ci) & (((ri - ci) & seq_mask) == 0),
        own_blk=(lax.broadcasted_iota(jnp.int32, (rows, xw), 1) // GLA_DK
                 == (lax.broadcasted_iota(jnp.int32, (rows, xw), 0) & seq_mask)),
        own_blk_seq=(lax.broadcasted_iota(jnp.int32, (nb, xw), 1) // GLA_DK
                     == lax.broadcasted_iota(jnp.int32, (nb, xw), 0)),
        reps=xw // LANES)


def _gla_chunk_head(p, h, ins, gn_ref, o_ref, st_ref):
    v_ref, gs_ref = ins[2], ins[4]

    def head_dup(z):
        blk = z[:, (h // 2) * LANES:(h // 2 + 1) * LANES]
        rolled = pltpu.roll(blk, GLA_DK, axis=1)
        low = lax.broadcasted_iota(jnp.int32, blk.shape, 1) < GLA_DK
        return jnp.where(low, blk, rolled) if h % 2 == 0 else jnp.where(low, rolled, blk)

    def expand(zd, own):
        return jnp.where(own, jnp.concatenate([zd] * p["reps"], axis=1), 0.0)

    vs = slice(h * GLA_DV, (h + 1) * GLA_DV)
    qad, qd, ktd, khd = head_dup(p["qa"]), head_dup(p["qt"]), head_dup(p["kt"]), head_dup(p["kh"])
    vb = v_ref[:, vs].astype(BF16)
    st = st_ref[h]
    att = lax.dot_general(qad[:, :GLA_DK].astype(BF16), ktd[:, :GLA_DK].astype(BF16),
                          (((1,), (1,)), ((), ())), preferred_element_type=F32)
    att = jnp.where(p["pair_ok"], att, 0.0).astype(BF16)
    o = jnp.dot(att, vb, preferred_element_type=F32) + lax.dot_general(
        expand(qd, p["own_blk"]).astype(BF16), st.astype(BF16), (((1,), (1,)), ((), ())),
        preferred_element_type=F32)
    upd = lax.dot_general(vb, expand(khd, p["own_blk"]).astype(BF16), (((0,), (0,)), ((), ())),
                          preferred_element_type=F32)
    gam_row = jnp.sum(expand(head_dup(p["gam"]), p["own_blk_seq"]), axis=0, keepdims=True)
    st_ref[h] = st * gam_row + upd
    o_ref[:, vs] = _rms(o, gn_ref[:, vs]) * gs_ref[:, vs]


def _gla_chunk_body(ins, gn_ref, o_ref, st_ref, b_sc, nb, c):
    p = _gla_chunk_prep(ins, b_sc, nb, c)
    for h in range(GLA_HEADS):
        _gla_chunk_head(p, h, ins, gn_ref, o_ref, st_ref)


def _gla_chunk_kernel(*refs, nb, c):
    ins, gn_ref, st0_ref, o_ref, st_ref, b_sc = refs[0:5], refs[5], refs[6], refs[7], refs[8], refs[9]

    @pl.when(_is_first())
    def _():
        st_ref[...] = st0_ref[...]

    _gla_chunk_body(ins, gn_ref, o_ref, st_ref, b_sc, nb, c)


def _gla_chunk(ins, gn, st0, nb, c, row_blocks=None):
    assert nb & (nb - 1) == 0 and (nb * GLA_DK) % LANES == 0
    rows = nb * c
    n = ins[0].shape[0] // row_blocks[0] if row_blocks else ins[0].shape[0]
    first = (row_blocks[1] * n) // rows if row_blocks else 0
    sshape = (GLA_HEADS, GLA_DV, nb * GLA_DK)
    cols = (GLA_K, GLA_K, GLA_V, GLA_K, GLA_V)
    kern = functools.partial(_gla_chunk_kernel, nb=nb, c=c)
    return pl.pallas_call(
        kern,
        grid=(n // rows,),
        in_specs=[pl.BlockSpec((rows, w_), lambda i: (first + i, 0)) for w_ in cols]
                 + [_const_spec((1, GLA_V)), _const_spec(sshape)],
        out_specs=[_row_spec(rows, GLA_V), pl.BlockSpec(sshape, lambda i: (0, 0, 0))],
        out_shape=[jax.ShapeDtypeStruct((n, GLA_V), F32), jax.ShapeDtypeStruct(sshape, F32)],
        scratch_shapes=[pltpu.VMEM((rows, GLA_K), F32)],
        compiler_params=_params(),
        name="gla_chunk",
    )(*ins, gn, st0)


def _gla_state_from_stacked(st):
    nb = st.shape[2] // GLA_DK
    return jnp.transpose(st.reshape(GLA_HEADS, GLA_DV, nb, GLA_DK), (2, 0, 3, 1))


def _gla_step_kernel(q_ref, k_ref, la_ref, v_ref, gs_ref, gn_ref, s0_ref, o_ref, s_ref):
    qT = q_ref[...].T
    kT = k_ref[...].T
    aT = jnp.exp(la_ref[...]).T
    o_rows = []
    for b in range(q_ref.shape[0]):
        s_new = aT[:, b:b + 1] * s0_ref[b] + kT[:, b:b + 1] * v_ref[b:b + 1, :]
        s_ref[b] = s_new
        o_rows.append(jnp.sum(qT[:, b:b + 1] * s_new, axis=0, keepdims=True))
    o = jnp.concatenate(o_rows, axis=0)
    o_ref[...] = _rms(o, gn_ref[...]) * gs_ref[...]


def _gla_step(side, gn, s0, row0):
    nb = s0.shape[0]
    blk = row0 // nb
    q, k, v, la, gs = side
    heads = lambda z: jnp.transpose(z[row0:row0 + nb].reshape(nb, GLA_HEADS, GLA_DK), (1, 0, 2))
    hspec = pl.BlockSpec((None, nb, GLA_DK), lambda h: (h, 0, 0))
    vspec = pl.BlockSpec((nb, GLA_DV), lambda h: (blk, h))
    ospec = pl.BlockSpec((nb, GLA_DV), lambda h: (0, h))
    sspec = pl.BlockSpec((nb, None, GLA_DK, GLA_DV), lambda h: (0, h, 0, 0))
    return pl.pallas_call(
        _gla_step_kernel,
        grid=(GLA_HEADS,),
        in_specs=[hspec, hspec, hspec, vspec, vspec, pl.BlockSpec((1, GLA_DV), lambda h: (0, h)), sspec],
        out_specs=[ospec, sspec],
        out_shape=[jax.ShapeDtypeStruct((nb, GLA_V), F32),
                   jax.ShapeDtypeStruct((nb, GLA_HEADS, GLA_DK, GLA_DV), F32)],
        compiler_params=_params("parallel"),
        name="gla_step",
    )(heads(q), heads(k), heads(la), v, gs, gn, s0)


def _l0_out_body(u_ref, og_ref, x, wts, xo_ref, xre_ref, xim_ref, sr_sc, si_sc, nb, tt):
    npack = S5_GROUPS // S5_PACK
    for j in range(npack):
        _s5_input(u_ref, wts, sr_sc, si_sc, j)
    ys = []
    for j in range(npack):
        _s5_scan(wts, xre_ref, xim_ref, sr_sc, si_sc, j, nb, tt)
        ys.append(_s5_output(wts, sr_sc, si_sc, j))
    _l0_tail_body(_s5_glu(ys, u_ref, wts), og_ref, x, wts, xo_ref)


S5_KIN = S5_PACK * S5_H
S5_KST = S5_PACK * S5_P


def _s5_input(u_ref, wts, sr_sc, si_sc, j):
    bbre_ref, bbim_ref = wts[0:2]
    cs = slice(j * S5_KST, (j + 1) * S5_KST)
    uj = u_ref[:, j * S5_KIN:(j + 1) * S5_KIN].astype(BF16)
    sr_sc[:, cs] = jnp.dot(uj, bbre_ref[j], preferred_element_type=F32)
    si_sc[:, cs] = jnp.dot(uj, bbim_ref[j], preferred_element_type=F32)


def _s5_scan(wts, xre_ref, xim_ref, sr_sc, si_sc, j, nb, tt):
    are_ref, aim_ref = wts[2:4]
    cs = slice(j * S5_KST, (j + 1) * S5_KST)
    ar = jnp.broadcast_to(are_ref[:, cs], (nb, S5_KST))
    ai = jnp.broadcast_to(aim_ref[:, cs], (nb, S5_KST))
    xr = xre_ref[:, cs]
    xi = xim_ref[:, cs]
    for t in range(tt):
        rows = slice(t * nb, (t + 1) * nb)
        xr, xi = ar * xr - ai * xi + sr_sc[rows, cs], ar * xi + ai * xr + si_sc[rows, cs]
        sr_sc[rows, cs] = xr
        si_sc[rows, cs] = xi
    xre_ref[:, cs] = xr
    xim_ref[:, cs] = xi


def _s5_output(wts, sr_sc, si_sc, j):
    cre_ref, cim_ref = wts[4:6]
    cs = slice(j * S5_KST, (j + 1) * S5_KST)
    return (jnp.dot(sr_sc[:, cs].astype(BF16), cre_ref[j], preferred_element_type=F32)
            - jnp.dot(si_sc[:, cs].astype(BF16), cim_ref[j], preferred_element_type=F32))


def _s5_glu(ys, u_ref, wts):
    d_ref, wglu_ref, bglu_ref = wts[6:9]
    y = jax.nn.gelu(jnp.concatenate(ys, axis=-1) + d_ref[...] * u_ref[...])
    return y * _sigmoid(_mm(y, wglu_ref[...]) + bglu_ref[...])


def _l0_tail_body(y, og_ref, x, wts, xo_ref):
    woa_ref, wob_ref = wts[9:11]
    xo_ref[...] = x + _mm(og_ref[...], woa_ref[...]) + _mm(y, wob_ref[...])


L0_OUT_WEIGHTS = 11


def _l0_out_weight_specs():
    npack = S5_GROUPS // S5_PACK
    kin, kst = S5_PACK * S5_H, S5_PACK * S5_P
    return [_const_spec((npack, kin, kst)), _const_spec((npack, kin, kst)),
            _const_spec((1, S5_N)), _const_spec((1, S5_N)),
            _const_spec((npack, kst, kin)), _const_spec((npack, kst, kin)),
            _const_spec((1, S5_W)), _const_spec((S5_W, S5_W)), _const_spec((1, S5_W)),
            _const_spec((GLA_V, D_MODEL), (0, 0)), _const_spec((S5_W, D_MODEL), (1, 0))]


def _l0_out_weights(w):
    return (w["s5_bbre"], w["s5_bbim"], w["s5_are"], w["s5_aim"], w["s5_cre"], w["s5_cim"],
            w["s5_d"], w["s5_w_glu"], w["s5_b_glu"], w["w_out_0"], w["w_out_0"])


def _l0_out_side_kernel(*refs, nb, tt_meta, nb_s):
    um_ref, ogm_ref, xm_ref, us_ref, ogs_ref, xs_ref, xr0s_ref, xi0s_ref = refs[0:8]
    wts = refs[8:8 + L0_OUT_WEIGHTS]
    xom_ref, xre_ref, xim_ref, xos_ref, xres_ref, xims_ref, sr_sc, si_sc = refs[8 + L0_OUT_WEIGHTS:]
    xre_ref[...] = jnp.zeros(xre_ref.shape, F32)
    xim_ref[...] = jnp.zeros(xim_ref.shape, F32)
    _l0_out_body(um_ref, ogm_ref, xm_ref[...], wts, xom_ref, xre_ref, xim_ref, sr_sc, si_sc, nb, tt_meta)
    xres_ref[...] = xr0s_ref[...]
    xims_ref[...] = xi0s_ref[...]
    _l0_out_body(us_ref, ogs_ref, xs_ref[...], wts, xos_ref, xres_ref, xims_ref, sr_sc, si_sc, nb_s, 1)


def _l0_out_side(u_side, og_meta, og_samp, x_side, xr0_s, xi0_s, w, nb, tt_meta):
    mrows, nb_s = nb * tt_meta, xr0_s.shape[0]
    assert mrows == nb_s
    kern = functools.partial(_l0_out_side_kernel, nb=nb, tt_meta=tt_meta, nb_s=nb_s)
    f32 = lambda *shape: jax.ShapeDtypeStruct(shape, F32)
    full = lambda r, c: pl.BlockSpec((r, c), lambda i: (0, 0))
    return pl.pallas_call(
        kern,
        grid=(1,),
        in_specs=[_side_spec(mrows, S5_W, 0), full(mrows, GLA_V), _side_spec(mrows, D_MODEL, 0),
                  _side_spec(nb_s, S5_W, 1), full(nb_s, GLA_V), _side_spec(nb_s, D_MODEL, 1),
                  full(nb_s, S5_N), full(nb_s, S5_N)] + _l0_out_weight_specs(),
        out_specs=[full(mrows, D_MODEL), full(nb, S5_N), full(nb, S5_N), full(nb_s, D_MODEL),
                   full(nb_s, S5_N), full(nb_s, S5_N)],
        out_shape=[f32(mrows, D_MODEL), f32(nb, S5_N), f32(nb, S5_N), f32(nb_s, D_MODEL),
                   f32(nb_s, S5_N), f32(nb_s, S5_N)],
        scratch_shapes=[pltpu.VMEM((mrows, S5_N), F32)] * 2,
        compiler_params=_params(),
        name="l0_out_side",
    )(u_side, og_meta, x_side, u_side, og_samp, x_side, xr0_s, xi0_s, *_l0_out_weights(w))


def _l0_kernel(*refs, nb, tt, c):
    x_ref, st0_ref, xr0_ref, xi0_ref = refs[0:4]
    n_in = 4 + L0_IN_WEIGHTS
    in_wts, gn_ref, out_wts = refs[4:n_in], refs[n_in], refs[n_in + 1:n_in + 1 + L0_OUT_WEIGHTS]
    rest = refs[n_in + 1 + L0_OUT_WEIGHTS:]
    xo_ref, st_ref, xre_ref, xim_ref = rest[0:4]
    proj_sc, og_sc, b_sc, sr_sc, si_sc, tm_sc = rest[4:10], rest[10], rest[11], rest[12], rest[13], rest[14]

    @pl.when(_is_first())
    def _():
        st_ref[...] = st0_ref[...]
        xre_ref[...] = xr0_ref[...]
        xim_ref[...] = xi0_ref[...]

    x = _load_time_major(x_ref, tm_sc)
    _l0_in_body(x, in_wts, proj_sc)
    q_sc, k_sc, v_sc, gs_sc, la_sc, u_sc = proj_sc
    npack = S5_GROUPS // S5_PACK
    nchunk = tt // c
    units = [(j, hp) for j in range(nchunk) for hp in range(GLA_HEADS // 2)]
    assert len(units) == npack
    for j in range(npack):
        _s5_input(u_sc, out_wts, sr_sc, si_sc, j)
    ys, prep = [], {}
    for i, (j, hp) in enumerate(units):
        _s5_scan(out_wts, xre_ref, xim_ref, sr_sc, si_sc, i, nb, tt)
        rws = pl.ds(j * nb * c, nb * c)
        ins = [r.at[rws] for r in (q_sc, k_sc, v_sc, la_sc, gs_sc)]
        if hp == 0:
            prep = _gla_chunk_prep(ins, b_sc, nb, c)
        for h in (2 * hp, 2 * hp + 1):
            _gla_chunk_head(prep, h, ins, gn_ref, og_sc.at[rws], st_ref)
        ys.append(_s5_output(out_wts, sr_sc, si_sc, i))
    _l0_tail_body(_s5_glu(ys, u_sc, out_wts), og_sc, x, out_wts, xo_ref)


def _l0(x, st0, xr0, xi0, w, tt, c):
    nb, nt, _ = x.shape
    n, rows = nb * nt, nb * tt
    assert nb & (nb - 1) == 0 and (nb * GLA_DK) % LANES == 0 and tt % c == 0
    sshape = (GLA_HEADS, GLA_DV, nb * GLA_DK)
    kern = functools.partial(_l0_kernel, nb=nb, tt=tt, c=c)
    st_spec = pl.BlockSpec((nb, S5_N), lambda i: (0, 0))
    f32 = lambda *shape: jax.ShapeDtypeStruct(shape, F32)
    return pl.pallas_call(
        kern,
        grid=(n // rows,),
        in_specs=[_seq_spec(nb, tt, D_MODEL), _const_spec(sshape), _const_spec((nb, S5_N)),
                  _const_spec((nb, S5_N))] + _l0_in_weight_specs() + [_const_spec((1, GLA_V))]
                 + _l0_out_weight_specs(),
        out_specs=[_row_spec(rows, D_MODEL), pl.BlockSpec(sshape, lambda i: (0, 0, 0)), st_spec, st_spec],
        out_shape=[f32(n, D_MODEL), f32(*sshape), f32(nb, S5_N), f32(nb, S5_N)],
        scratch_shapes=[pltpu.VMEM((rows, cols), F32) for cols in L0_IN_COLS]
                       + [pltpu.VMEM((rows, GLA_V), F32), pltpu.VMEM((nb * c, GLA_K), F32),
                          pltpu.VMEM((rows, S5_N), F32), pltpu.VMEM((rows, S5_N), F32),
                          _tm_scratch(rows, D_MODEL)],
        compiler_params=_params(),
        name="l0_mixer",
    )(x, st0, xr0, xi0, *_l0_in_weights(w), w["gla_norm"], *_l0_out_weights(w))


def _ffn_body(x, wts, c0_ref, c_ref, hm_sc, nb, tt, final):
    g_ref, wg_ref, wv_ref, cw_ref, cb_ref, wd_ref, gf_ref = wts
    rows = nb * tt
    xn = _rms(x, g_ref[...]).astype(BF16)
    for ci in range(D_FF // FF_CHUNK):
        cs = slice(ci * FF_CHUNK, (ci + 1) * FF_CHUNK)
        gate = jnp.dot(xn, wg_ref[:, cs], preferred_element_type=F32)
        val = jnp.dot(xn, wv_ref[:, cs], preferred_element_type=F32)
        if tt == 1:
            taps = [c0_ref[:, j, cs] for j in range(FFN_CONV - 1)] + [gate]
            for j in range(FFN_CONV - 1):
                c_ref[:, j, cs] = taps[j + 1]
        else:
            ext = jnp.concatenate([c_ref[:, cs], gate], axis=0)
            taps = [ext[j * nb:j * nb + rows] for j in range(FFN_CONV)]
            c_ref[:, cs] = ext[tt * nb:(tt + FFN_CONV - 1) * nb]
        y = cb_ref[:, cs] + taps[0] * cw_ref[0:1, cs]
        for j in range(1, FFN_CONV):
            y = y + taps[j] * cw_ref[j:j + 1, cs]
        hm_sc[:, cs] = (jax.nn.gelu(y) * val).astype(BF16)
    out = x + jnp.dot(hm_sc[...], wd_ref[...], preferred_element_type=F32)
    return _rms(out, gf_ref[...]) if final else out


FFN_WEIGHTS = 7


def _ffn_weight_specs(layer):
    return [_const_spec((None, 1, D_MODEL), (layer, 0, 0)),
            _const_spec((None, D_MODEL, D_FF), (layer, 0, 0)),
            _const_spec((None, D_MODEL, D_FF), (layer, 0, 1)),
            _const_spec((None, FFN_CONV, D_FF), (layer, 0, 0)),
            _const_spec((None, 1, D_FF), (layer, 0, 0)),
            _const_spec((None, D_FF, D_MODEL), (layer, 0, 0)), _const_spec((1, D_MODEL))]


def _ffn_weights(w):
    return (w["norm_ffn"], w["ffn_w_up"], w["ffn_w_up"], w["ffn_conv_w"], w["ffn_conv_b"], w["ffn_w_down"],
            w["norm_final"])


def _ffn_kernel(*refs, nb, tt, final, batch_major_out):
    x_ref, c0_ref = refs[0:2]
    wts = refs[2:2 + FFN_WEIGHTS]
    xo_ref, c_ref, hm_sc = refs[2 + FFN_WEIGHTS:5 + FFN_WEIGHTS]
    tm_sc = refs[5 + FFN_WEIGHTS] if batch_major_out else None

    @pl.when(_is_first())
    def _():
        c_ref[...] = c0_ref[...]

    _store_time_major(xo_ref, _ffn_body(x_ref[...], wts, None, c_ref, hm_sc, nb, tt, final), tm_sc)


def _ffn(x, c0, w, layer, nb, tt, final, batch_major_out=False):
    n = x.shape[0]
    rows = nb * tt
    kern = functools.partial(_ffn_kernel, nb=nb, tt=tt, final=final, batch_major_out=batch_major_out)
    if batch_major_out:
        o_spec, o_shape = _seq_spec(nb, tt, D_MODEL), (nb, n // nb, D_MODEL)
    else:
        o_spec, o_shape = _row_spec(rows, D_MODEL), (n, D_MODEL)
    cshape = _cache_shape(nb, tt, FFN_CONV - 1, D_FF)
    f32 = lambda *shape: jax.ShapeDtypeStruct(shape, F32)
    return pl.pallas_call(
        kern,
        grid=(n // rows,),
        in_specs=[_row_spec(rows, D_MODEL), _const_spec(cshape)] + _ffn_weight_specs(layer),
        out_specs=[o_spec, pl.BlockSpec(cshape, lambda i: (0, 0))],
        out_shape=[f32(*o_shape), f32(*cshape)],
        scratch_shapes=[pltpu.VMEM((rows, D_FF), BF16)]
                       + ([_tm_scratch(rows, D_MODEL)] if batch_major_out else []),
        compiler_params=_params(),
        name="ffn%d" % layer,
    )(x, c0, *_ffn_weights(w))


def _ffn_side_kernel(*refs, nb, tt_meta, nb_s, final):
    xm_ref, xs_ref, c0s_ref = refs[0:3]
    wts = refs[3:3 + FFN_WEIGHTS]
    xom_ref, c_ref, xos_ref, cs_ref, hm_sc = refs[3 + FFN_WEIGHTS:]
    c_ref[...] = jnp.zeros(c_ref.shape, F32)
    xom_ref[...] = _ffn_body(xm_ref[...], wts, None, c_ref, hm_sc, nb, tt_meta, final)
    xos_ref[...] = _ffn_body(xs_ref[...], wts, c0s_ref, cs_ref, hm_sc, nb_s, 1, final)


def _ffn_side(x_meta, x_samp, c0_s, w, layer, nb, tt_meta, final):
    mrows, nb_s = nb * tt_meta, x_samp.shape[0]
    assert mrows == nb_s
    kern = functools.partial(_ffn_side_kernel, nb=nb, tt_meta=tt_meta, nb_s=nb_s, final=final)
    cshape = _cache_shape(nb, tt_meta, FFN_CONV - 1, D_FF)
    cs_shape = _cache_shape(nb_s, 1, FFN_CONV - 1, D_FF)
    f32 = lambda *shape: jax.ShapeDtypeStruct(shape, F32)
    full = lambda shape: pl.BlockSpec(shape, lambda i: (0,) * len(shape))
    return pl.pallas_call(
        kern,
        grid=(1,),
        in_specs=[full((mrows, D_MODEL)), full((nb_s, D_MODEL)),
                  pl.BlockSpec((None,) + cs_shape, lambda i: (layer, 0, 0, 0))] + _ffn_weight_specs(layer),
        out_specs=[full((mrows, D_MODEL)), full(cshape), full((nb_s, D_MODEL)), full(cs_shape)],
        out_shape=[f32(mrows, D_MODEL), f32(*cshape), f32(nb_s, D_MODEL), f32(*cs_shape)],
        scratch_shapes=[pltpu.VMEM((mrows, D_FF), BF16)],
        compiler_params=_params(),
        name="ffn%d_side" % layer,
    )(x_meta, x_samp, c0_s, *_ffn_weights(w))


def _l1_body(x_ref, wts, xo_ref, h_ref, c0_ref, c_ref, a_sc, b_sc, nb, tt):
    g_ref, wgt_ref, wxr_ref, cw_ref, cb_ref, wa_ref, ba_ref, wx_ref, bx_ref, lam_ref, wo_ref = wts
    sp = jax.nn.softplus(-lam_ref[...])
    nsplit = L1_SPLIT if tt % L1_SPLIT == 0 else 1
    th = tt // nsplit
    rows = nb * th
    carry = c_ref[...] if tt > 1 else None
    xs, xns, ggs = [], [], []

    for part in range(nsplit):
        prow = slice(part * rows, (part + 1) * rows)
        x = x_ref[prow, :]
        xn = _rms(x, g_ref[...]).astype(BF16)
        xs.append(x)
        xns.append(xn)
        xr = jnp.dot(xn, wxr_ref[...], preferred_element_type=F32)
        if tt == 1:
            taps = [c0_ref[:, j, :] for j in range(RNN_CONV - 1)] + [xr]
            for j in range(RNN_CONV - 1):
                c_ref[:, j, :] = taps[j + 1]
        else:
            ext = jnp.concatenate([carry, xr], axis=0)
            taps = [ext[j * nb:j * nb + rows] for j in range(RNN_CONV)]
            carry = ext[th * nb:(th + RNN_CONV - 1) * nb]
        xc = cb_ref[...] + taps[0] * cw_ref[0:1, :]
        for j in range(1, RNN_CONV):
            xc = xc + taps[j] * cw_ref[j:j + 1, :]

        xcb = xc.astype(BF16)
        rs, gs = [], []
        for wi in range(RNN_W // GATE_WIN):
            for ni in range(GATE_WIN // GATE_N):
                k0 = wi * GATE_WIN + ni * LANES
                lhs = xcb[:, k0:k0 + GATE_K]
                rs.append(jnp.dot(lhs, wa_ref[wi, ni], preferred_element_type=F32))
                gs.append(jnp.dot(lhs, wx_ref[wi, ni], preferred_element_type=F32))
        r = _sigmoid(jnp.concatenate(rs, axis=-1) + ba_ref[...])
        ig = _sigmoid(jnp.concatenate(gs, axis=-1) + bx_ref[...])
        log_a = (-RNN_C) * r * sp
        a = jnp.exp(log_a)
        a_sc[prow, :] = a
        b_sc[prow, :] = _sqrt_nonneg(jnp.tanh(-log_a) * (a * a + 1.0)) * (ig * xc)
    if tt > 1:
        c_ref[...] = carry

    for part in range(nsplit):
        ggs.append(jax.nn.gelu(jnp.dot(xns[part], wgt_ref[...], preferred_element_type=F32)))

    h = h_ref[...]
    for part in range(nsplit):
        prow = slice(part * rows, (part + 1) * rows)
        for t in range(part * th, (part + 1) * th):
            rws = slice(t * nb, (t + 1) * nb)
            h = a_sc[rws, :] * h + b_sc[rws, :]
            b_sc[rws, :] = h
        xo_ref[prow, :] = xs[part] + _mm(b_sc[prow, :] * ggs[part], wo_ref[...])
    h_ref[...] = h


L1_WEIGHTS = 11


def _l1_weight_specs():
    gshape = (RNN_W // GATE_WIN, GATE_WIN // GATE_N, GATE_K, GATE_N)
    return [_const_spec((1, D_MODEL)),
            _const_spec((D_MODEL, RNN_W), (0, 0)), _const_spec((D_MODEL, RNN_W), (0, 1)),
            _const_spec((RNN_CONV, RNN_W)), _const_spec((1, RNN_W)),
            _const_spec(gshape), _const_spec((1, RNN_W)),
            _const_spec(gshape), _const_spec((1, RNN_W)),
            _const_spec((1, RNN_W)), _const_spec((RNN_W, D_MODEL))]


def _l1_weights(w):
    return (w["norm_mix_1"], w["w_in_1"], w["w_in_1"], w["rnn_conv_w"], w["rnn_conv_b"], w["rnn_wa"],
            w["rnn_b_a"], w["rnn_wx"], w["rnn_b_x"], w["rnn_lam"], w["w_out_1"])


def _l1_kernel(*refs, nb, tt):
    x_ref, h0_ref, c0_ref = refs[0:3]
    wts = refs[3:3 + L1_WEIGHTS]
    xo_ref, h_ref, c_ref, a_sc, b_sc = refs[3 + L1_WEIGHTS:]

    @pl.when(_is_first())
    def _():
        h_ref[...] = h0_ref[...]
        c_ref[...] = c0_ref[...]

    _l1_body(x_ref, wts, xo_ref, h_ref, None, c_ref, a_sc, b_sc, nb, tt)


def _l1(x, h0, c0, w, nb, tt):
    n = x.shape[0]
    rows = nb * tt
    kern = functools.partial(_l1_kernel, nb=nb, tt=tt)
    cshape = _cache_shape(nb, tt, RNN_CONV - 1, RNN_W)
    f32 = lambda *shape: jax.ShapeDtypeStruct(shape, F32)
    return pl.pallas_call(
        kern,
        grid=(n // rows,),
        in_specs=[_row_spec(rows, D_MODEL), _const_spec((nb, RNN_W)), _const_spec(cshape)] + _l1_weight_specs(),
        out_specs=[_row_spec(rows, D_MODEL), pl.BlockSpec((nb, RNN_W), lambda i: (0, 0)),
                   pl.BlockSpec(cshape, lambda i: (0, 0))],
        out_shape=[f32(n, D_MODEL), f32(nb, RNN_W), f32(*cshape)],
        scratch_shapes=[pltpu.VMEM((rows, RNN_W), F32)] * 2,
        compiler_params=_params(),
        name="l1_mixer",
    )(x, h0, c0, *_l1_weights(w))


def _l1_side_kernel(*refs, nb, tt_meta, nb_s):
    xm_ref, xs_ref, h0s_ref, c0s_ref = refs[0:4]
    wts = refs[4:4 + L1_WEIGHTS]
    xom_ref, h_ref, c_ref, xos_ref, hs_ref, cs_ref, a_sc, b_sc = refs[4 + L1_WEIGHTS:]
    h_ref[...] = jnp.zeros(h_ref.shape, F32)
    c_ref[...] = jnp.zeros(c_ref.shape, F32)
    _l1_body(xm_ref, wts, xom_ref, h_ref, None, c_ref, a_sc, b_sc, nb, tt_meta)
    hs_ref[...] = h0s_ref[...]
    _l1_body(xs_ref, wts, xos_ref, hs_ref, c0s_ref, cs_ref, a_sc, b_sc, nb_s, 1)


def _l1_side(x_meta, x_samp, h0_s, c0_s, w, nb, tt_meta):
    mrows, nb_s = nb * tt_meta, x_samp.shape[0]
    assert mrows == nb_s
    kern = functools.partial(_l1_side_kernel, nb=nb, tt_meta=tt_meta, nb_s=nb_s)
    cshape = _cache_shape(nb, tt_meta, RNN_CONV - 1, RNN_W)
    cs_shape = _cache_shape(nb_s, 1, RNN_CONV - 1, RNN_W)
    f32 = lambda *shape: jax.ShapeDtypeStruct(shape, F32)
    full = lambda shape: pl.BlockSpec(shape, lambda i: (0,) * len(shape))
    return pl.pallas_call(
        kern,
        grid=(1,),
        in_specs=[full((mrows, D_MODEL)), full((nb_s, D_MODEL)), full((nb_s, RNN_W)), full(cs_shape)]
                 + _l1_weight_specs(),
        out_specs=[full((mrows, D_MODEL)), full((nb, RNN_W)), full(cshape), full((nb_s, D_MODEL)),
                   full((nb_s, RNN_W)), full(cs_shape)],
        out_shape=[f32(mrows, D_MODEL), f32(nb, RNN_W), f32(*cshape), f32(nb_s, D_MODEL), f32(nb_s, RNN_W),
                   f32(*cs_shape)],
        scratch_shapes=[pltpu.VMEM((mrows, RNN_W), F32)] * 2,
        compiler_params=_params(),
        name="l1_side",
    )(x_meta, x_samp, h0_s, c0_s, *_l1_weights(w))


def _pack_gate_kernel(wa_ref, wx_ref, oa_ref, ox_ref):
    tiles_per_win = GATE_WIN // GATE_N
    for w_ref, o_ref in ((wa_ref, oa_ref), (wx_ref, ox_ref)):
        o_ref[...] = jnp.zeros(o_ref.shape, o_ref.dtype)
        for n in range(RNN_BLOCKS):
            pos = n * RNN_BW
            wi = pos // GATE_WIN
            for ni in range(tiles_per_win):
                k0 = wi * GATE_WIN + ni * LANES
                n0 = wi * GATE_WIN + ni * GATE_N
                lo, hi = max(pos, n0), min(pos + RNN_BW, n0 + GATE_N)
                if lo < hi:
                    o_ref[wi, ni, pos - k0:pos - k0 + RNN_BW, lo - n0:hi - n0] = (
                        w_ref[n][:, lo - pos:hi - pos].astype(o_ref.dtype))


def _pack_gates(wa, wx):
    gshape = (RNN_W // GATE_WIN, GATE_WIN // GATE_N, GATE_K, GATE_N)
    return pl.pallas_call(
        _pack_gate_kernel,
        out_shape=(jax.ShapeDtypeStruct(gshape, BF16), jax.ShapeDtypeStruct(gshape, BF16)),
        name="pack_gates",
    )(wa, wx)


def _prep_weights(p):
    w = {}
    row = lambda v: v.reshape(1, -1).astype(F32)
    w_in = p["w_in_0"]
    c = 2 * GLA_K + 2 * GLA_V
    w["w_in_0"] = w_in.astype(BF16)
    w["w_lr"] = jnp.pad(w_in[:, c:c + GLA_RANK], ((0, 0), (0, LANES - GLA_RANK))).astype(BF16)
    c += GLA_RANK
    w["w_u"] = w_in[:, c:c + S5_W].astype(BF16)
    w["w_alpha"] = jnp.pad(p["w_alpha_0"], ((0, LANES - GLA_RANK), (0, 0))).astype(BF16)
    w["b_alpha"] = row(p["b_alpha_0"])
    w["norm_mix_0"] = row(p["norm_mix_0"])
    w["gla_norm"] = row(p["gla_norm_0"])

    are, aim, bbre, bbim = _s5_prep(p["s5_lam_re"], p["s5_lam_im"], p["s5_log_dt"], p["s5_b_re"],
                                    p["s5_b_im"])
    npack = S5_GROUPS // S5_PACK
    eye = jnp.eye(S5_PACK, dtype=F32)[None, :, None, :, None]
    grouped = lambda m: m.reshape(npack, S5_PACK, S5_H, S5_P)
    pack_b = lambda m: (grouped(m)[:, :, :, None, :] * eye).reshape(
        npack, S5_PACK * S5_H, S5_PACK * S5_P).astype(BF16)
    pack_c = lambda m: (jnp.swapaxes(grouped(m), 2, 3)[:, :, :, None, :] * eye).reshape(
        npack, S5_PACK * S5_P, S5_PACK * S5_H).astype(BF16)
    w["s5_are"] = are.reshape(1, S5_N)
    w["s5_aim"] = aim.reshape(1, S5_N)
    w["s5_bbre"] = pack_b(bbre)
    w["s5_bbim"] = pack_b(bbim)
    w["s5_cre"] = pack_c(p["s5_c_re"])
    w["s5_cim"] = pack_c(p["s5_c_im"])
    w["s5_d"] = row(p["s5_d"])
    w["s5_w_glu"] = p["s5_w_glu"].astype(BF16)
    w["s5_b_glu"] = row(p["s5_b_glu"])
    w["w_out_0"] = p["w_out_0"].astype(BF16)

    w["norm_mix_1"] = row(p["norm_mix_1"])
    w["w_in_1"] = p["w_in_1"].astype(BF16)
    w["rnn_conv_w"] = p["rnn_conv_w"].astype(F32)
    w["rnn_conv_b"] = row(p["rnn_conv_b"])
    w["rnn_wa"], w["rnn_wx"] = _pack_gates(p["rnn_w_a"], p["rnn_w_x"])
    w["rnn_b_a"] = row(p["rnn_b_a"])
    w["rnn_b_x"] = row(p["rnn_b_x"])
    w["rnn_lam"] = row(p["rnn_lam"])
    w["w_out_1"] = p["w_out_1"].astype(BF16)

    depth = p["norm_ffn"].shape[0]
    w["norm_ffn"] = p["norm_ffn"].reshape(depth, 1, D_MODEL)
    w["ffn_w_up"] = p["ffn_w_up"].astype(BF16)
    w["ffn_conv_w"] = p["ffn_conv_w"]
    w["ffn_conv_b"] = p["ffn_conv_b"].reshape(depth, 1, D_FF)
    w["ffn_w_down"] = p["ffn_w_down"].astype(BF16)
    w["norm_final"] = row(p["norm_final"])
    return w


def _tile_steps():
    return dict(l0=64, gla=32, ffn=128, l1=64)


def _batch_major(cache, nb):
    jb, c = cache.shape
    return jnp.transpose(cache.reshape(jb // nb, nb, c), (1, 0, 2))


def kernel(x_prompt, x_sample, state_gla, state_s5_re, state_s5_im, state_rglru, cache_rglru_conv,
           cache_ffn_conv, meta_tokens, norm_mix_0, w_in_0, w_alpha_0, b_alpha_0, gla_norm_0,
           s5_lam_re, s5_lam_im, s5_log_dt, s5_b_re, s5_b_im, s5_c_re, s5_c_im, s5_d, s5_w_glu,
           s5_b_glu, w_out_0, norm_mix_1, w_in_1, rnn_conv_w, rnn_conv_b, rnn_w_a, rnn_b_a, rnn_w_x,
           rnn_b_x, rnn_lam, w_out_1, norm_ffn, ffn_w_up, ffn_conv_w, ffn_conv_b, ffn_w_down, norm_final):
    w = _prep_weights(dict(
        norm_mix_0=norm_mix_0, w_in_0=w_in_0, w_alpha_0=w_alpha_0, b_alpha_0=b_alpha_0,
        gla_norm_0=gla_norm_0, s5_lam_re=s5_lam_re, s5_lam_im=s5_lam_im, s5_log_dt=s5_log_dt,
        s5_b_re=s5_b_re, s5_b_im=s5_b_im, s5_c_re=s5_c_re, s5_c_im=s5_c_im, s5_d=s5_d,
        s5_w_glu=s5_w_glu, s5_b_glu=s5_b_glu, w_out_0=w_out_0, norm_mix_1=norm_mix_1, w_in_1=w_in_1,
        rnn_conv_w=rnn_conv_w, rnn_conv_b=rnn_conv_b, rnn_w_a=rnn_w_a, rnn_b_a=rnn_b_a,
        rnn_w_x=rnn_w_x, rnn_b_x=rnn_b_x, rnn_lam=rnn_lam, w_out_1=w_out_1, norm_ffn=norm_ffn,
        ffn_w_up=ffn_w_up, ffn_conv_w=ffn_conv_w, ffn_conv_b=ffn_conv_b, ffn_w_down=ffn_w_down,
        norm_final=norm_final))

    bp = x_prompt.shape[0]
    bs = x_sample.shape[0]
    tt = _tile_steps()
    mrows = bp * N_META
    assert mrows == bs

    x_side = jnp.concatenate([jnp.repeat(meta_tokens.astype(F32), bp, axis=0),
                              x_sample.reshape(bs, D_MODEL)], axis=0)
    s5_re_s = state_s5_re.reshape(bs, S5_N)
    s5_im_s = state_s5_im.reshape(bs, S5_N)

    gla_in = lambda z: (z[0], z[1], z[2], z[4], z[3])
    gn = w["gla_norm"]

    side = _l0_in(x_side, w)
    og_m, gla_m = _gla_chunk(gla_in(side), gn, jnp.zeros((GLA_HEADS, GLA_DV, bp * GLA_DK), F32), bp, N_META,
                             row_blocks=(2, 0))
    og_s, gla_s = _gla_step(gla_in(side), gn, state_gla, mrows)
    x_m, re_m, im_m, x_s, re_s, im_s = _l0_out_side(side[5], og_m, og_s, x_side, s5_re_s, s5_im_s, w, bp, N_META)
    x_m, fc0_m, x_s, fc0_s = _ffn_side(x_m, x_s, cache_ffn_conv, w, 0, bp, N_META, False)
    x_m, h_m, rc_m, x_s, h_s, rc_s = _l1_side(x_m, x_s, state_rglru, cache_rglru_conv, w, bp, N_META)
    _, fc1_m, ys, fc1_s = _ffn_side(x_m, x_s, cache_ffn_conv, w, 1, bp, N_META, True)

    x, gla_p, re_p, im_p = _l0(x_prompt, gla_m, re_m, im_m, w, tt["l0"], tt["gla"])
    x, fc0_p = _ffn(x, fc0_m, w, 0, bp, tt["ffn"], False)
    x, h_p, rc_p = _l1(x, h_m, rc_m, w, bp, tt["l1"])
    yp, fc1_p = _ffn(x, fc1_m, w, 1, bp, tt["ffn"], True, batch_major_out=True)

    grp = lambda z, nb: z.reshape(nb, S5_GROUPS, S5_P)
    return (yp, ys.reshape(bs, 1, D_MODEL), _gla_state_from_stacked(gla_p), gla_s,
            grp(re_p, bp), grp(re_s, bs), grp(im_p, bp), grp(im_s, bs), h_p, h_s,
            _batch_major(rc_p, bp), rc_s,
            jnp.stack([_batch_major(fc0_p, bp), _batch_major(fc1_p, bp)]), jnp.stack([fc0_s, fc1_s]))
```

```python
import functools

import jax
import jax.numpy as jnp
from jax import lax
from jax.experimental import pallas as pl
from jax.experimental.pallas import tpu as pltpu

F32 = jnp.float32
BF16 = jnp.bfloat16

D_MODEL = 1024
N_META = 16
EPS = 1e-6
F32_TINY = 1.1754944e-38
GLA_HEADS = 4
GLA_DK = 64
GLA_DV = 128
GLA_RANK = 16
GLA_TAU = 16.0
GLA_K = GLA_HEADS * GLA_DK
GLA_V = GLA_HEADS * GLA_DV
S5_GROUPS = 32
S5_H = 16
S5_P = 64
S5_W = S5_GROUPS * S5_H
S5_N = S5_GROUPS * S5_P
RNN_W = 1536
RNN_BLOCKS = 16
RNN_BW = RNN_W // RNN_BLOCKS
RNN_C = 8.0
RNN_CONV = 4
D_FF = 2816
FFN_CONV = 3

LANES = 128
FF_CHUNK = 256
L1_SPLIT = 2
S5_PACK = 8
GATE_WIN = 768
GATE_K = 512
GATE_N = 256
VMEM_LIMIT = 56 * 1024 * 1024


def _rms(x, g):
    return x * lax.rsqrt(jnp.mean(x * x, axis=-1, keepdims=True) + EPS) * g


def _sigmoid(x):
    return 0.5 * jnp.tanh(0.5 * x) + 0.5


def _sqrt_nonneg(t):
    return t * lax.rsqrt(jnp.maximum(t, F32_TINY))


def _mm(a, w):
    return jnp.dot(a.astype(BF16), w, preferred_element_type=F32)


def _const_spec(shape, index=None):
    idx = tuple(index) if index is not None else (0,) * len(shape)
    return pl.BlockSpec(shape, lambda i: idx, pipeline_mode=pl.Buffered(1))


def _row_spec(rows, cols):
    return pl.BlockSpec((rows, cols), lambda i: (i, 0))


def _seq_spec(nb, tt, cols):
    return pl.BlockSpec((nb, tt, cols), lambda i: (0, i, 0))


def _tm_scratch(rows, cols):
    return pltpu.VMEM((cols // LANES, rows, LANES), F32)


def _load_time_major(x_ref, tm_sc):
    nb, tt, cols = x_ref.shape
    for b in range(nb):
        for j in range(cols // LANES):
            tm_sc[j, pl.ds(b, tt, stride=nb), :] = x_ref[b, :, j * LANES:(j + 1) * LANES]
    return jnp.concatenate([tm_sc[j] for j in range(cols // LANES)], axis=-1)


def _store_time_major(o_ref, val, tm_sc):
    if tm_sc is None:
        o_ref[...] = val
        return
    nb, tt, cols = o_ref.shape
    for j in range(cols // LANES):
        tm_sc[j] = val[:, j * LANES:(j + 1) * LANES]
    for b in range(nb):
        for j in range(cols // LANES):
            o_ref[b, :, j * LANES:(j + 1) * LANES] = tm_sc[j, pl.ds(b, tt, stride=nb), :]


def _cache_shape(nb, tt, taps, width):
    return (nb, taps, width) if tt == 1 else (taps * nb, width)


def _params(sem="arbitrary"):
    return pltpu.CompilerParams(dimension_semantics=(sem,), vmem_limit_bytes=VMEM_LIMIT)


def _s5_prep_kernel(lr_ref, li_ref, ldt_ref, brt_ref, bit_ref, are_ref, aim_ref, bbre_ref, bbim_ref):
    lr = lr_ref[...]
    li = li_ref[...]
    dt = jnp.exp(ldt_ref[...])
    mag = jnp.exp(lr * dt)
    ab_re = mag * jnp.cos(li * dt)
    ab_im = mag * jnp.sin(li * dt)
    den = lr * lr + li * li
    nr = ab_re - 1.0
    ni = ab_im
    f_re = (nr * lr + ni * li) / den
    f_im = (ni * lr - nr * li) / den
    are_ref[...] = ab_re
    aim_ref[...] = ab_im
    brt = brt_ref[...]
    bit = bit_ref[...]
    bbre_ref[...] = f_re[:, None, :] * brt - f_im[:, None, :] * bit
    bbim_ref[...] = f_re[:, None, :] * bit + f_im[:, None, :] * brt


def _s5_prep(lam_re, lam_im, log_dt, b_re, b_im):
    g, p, h = b_re.shape
    brt = jnp.transpose(b_re, (0, 2, 1))
    bit = jnp.transpose(b_im, (0, 2, 1))
    return pl.pallas_call(
        _s5_prep_kernel,
        out_shape=(jax.ShapeDtypeStruct((g, p), F32), jax.ShapeDtypeStruct((g, p), F32),
                   jax.ShapeDtypeStruct((g, h, p), F32), jax.ShapeDtypeStruct((g, h, p), F32)),
        name="s5_prep",
    )(lam_re, lam_im, log_dt.reshape(g, 1), brt, bit)


def _is_first():
    return pl.program_id(0) == 0


def _side_spec(rows, cols, block=0):
    return pl.BlockSpec((rows, cols), lambda i: (block, 0))


def _l0_in_body(x, wts, outs):
    g_ref, wq_ref, wk_ref, wv_ref, wg_ref, wu_ref, wlr_ref, wal_ref, bal_ref = wts
    q_ref, k_ref, v_ref, gs_ref, la_ref, u_ref = outs
    xn = _rms(x, g_ref[...]).astype(BF16)
    u_ref[...] = _mm(xn, wu_ref[...])
    lr = _mm(xn, wlr_ref[...])
    pre = _mm(lr, wal_ref[...]) + bal_ref[...]
    la_ref[...] = jax.nn.log_sigmoid(pre) * (1.0 / GLA_TAU)
    k_ref[...] = _mm(xn, wk_ref[...])
    q_ref[...] = _mm(xn, wq_ref[...]) * (GLA_DK ** -0.5)
    v_ref[...] = _mm(xn, wv_ref[...])
    g = _mm(xn, wg_ref[...])
    gs_ref[...] = g * _sigmoid(g)


def _l0_in_kernel(*refs):
    x_ref, wts, outs = refs[0], refs[1:1 + L0_IN_WEIGHTS], refs[1 + L0_IN_WEIGHTS:]
    _l0_in_body(x_ref[...], wts, outs)


L0_IN_COLS = (GLA_K, GLA_K, GLA_V, GLA_V, GLA_K, S5_W)
L0_IN_WEIGHTS = 9


def _l0_in_weight_specs():
    return [_const_spec((1, D_MODEL)),
            _const_spec((D_MODEL, GLA_K), (0, 0)), _const_spec((D_MODEL, GLA_K), (0, 1)),
            _const_spec((D_MODEL, GLA_V), (0, 1)), _const_spec((D_MODEL, GLA_V), (0, 2)),
            _const_spec((D_MODEL, S5_W)), _const_spec((D_MODEL, LANES)),
            _const_spec((LANES, GLA_K)), _const_spec((1, GLA_K))]


def _l0_in_weights(w):
    return (w["norm_mix_0"], w["w_in_0"], w["w_in_0"], w["w_in_0"], w["w_in_0"], w["w_u"], w["w_lr"],
            w["w_alpha"], w["b_alpha"])


def _l0_in(x, w):
    n = x.shape[0]
    return pl.pallas_call(
        _l0_in_kernel,
        grid=(1,),
        in_specs=[_row_spec(n, D_MODEL)] + _l0_in_weight_specs(),
        out_specs=[_row_spec(n, c) for c in L0_IN_COLS],
        out_shape=[jax.ShapeDtypeStruct((n, c), F32) for c in L0_IN_COLS],
        compiler_params=_params(),
        name="l0_in",
    )(x, *_l0_in_weights(w))


def _gla_chunk_prep(ins, b_sc, nb, c):
    q_ref, k_ref, _, la_ref, _ = ins
    rows = nb * c
    seq_mask = nb - 1

    def cum_body(t, run):
        rws = pl.ds(pl.multiple_of(t * nb, nb), nb)
        run = run + la_ref[rws, :]
        b_sc[rws, :] = run
        return run

    bl = lax.fori_loop(0, c, cum_body, jnp.zeros((nb, GLA_K), F32), unroll=True)
    b = b_sc[...]
    q = q_ref[...]
    k = k_ref[...]
    bm = jnp.concatenate([b_sc[pl.ds((c // 2) * nb, nb), :]] * c, axis=0)
    ri = lax.broadcasted_iota(jnp.int32, (rows, rows), 0)
    ci = lax.broadcasted_iota(jnp.int32, (rows, rows), 1)
    xw = nb * GLA_DK
    return dict(
        qa=q * jnp.exp(b - bm), kt=k * jnp.exp(bm - b), qt=q * jnp.exp(b),
        kh=k * jnp.exp(jnp.concatenate([bl] * c, axis=0) - b),
        gam=jnp.exp(bl),
        pair_ok=(ri >= ci) & (((ri - ci) & seq_mask) == 0),
        own_blk=(lax.broadcasted_iota(jnp.int32, (rows, xw), 1) // GLA_DK
                 == (lax.broadcasted_iota(jnp.int32, (rows, xw), 0) & seq_mask)),
        own_blk_seq=(lax.broadcasted_iota(jnp.int32, (nb, xw), 1) // GLA_DK
                     == lax.broadcasted_iota(jnp.int32, (nb, xw), 0)),
        reps=xw // LANES)


def _gla_chunk_head(p, h, ins, gn_ref, o_ref, st_ref):
    v_ref, gs_ref = ins[2], ins[4]

    def head_dup(z):
        blk = z[:, (h // 2) * LANES:(h // 2 + 1) * LANES]
        rolled = pltpu.roll(blk, GLA_DK, axis=1)
        low = lax.broadcasted_iota(jnp.int32, blk.shape, 1) < GLA_DK
        return jnp.where(low, blk, rolled) if h % 2 == 0 else jnp.where(low, rolled, blk)

    def expand(zd, own):
        return jnp.where(own, jnp.concatenate([zd] * p["reps"], axis=1), 0.0)

    vs = slice(h * GLA_DV, (h + 1) * GLA_DV)
    qad, qd, ktd, khd = head_dup(p["qa"]), head_dup(p["qt"]), head_dup(p["kt"]), head_dup(p["kh"])
    vb = v_ref[:, vs].astype(BF16)
    st = st_ref[h]
    att = lax.dot_general(qad[:, :GLA_DK].astype(BF16), ktd[:, :GLA_DK].astype(BF16),
                          (((1,), (1,)), ((), ())), preferred_element_type=F32)
    att = jnp.where(p["pair_ok"], att, 0.0).astype(BF16)
    o = jnp.dot(att, vb, preferred_element_type=F32) + lax.dot_general(
        expand(qd, p["own_blk"]).astype(BF16), st.astype(BF16), (((1,), (1,)), ((), ())),
        preferred_element_type=F32)
    upd = lax.dot_general(vb, expand(khd, p["own_blk"]).astype(BF16), (((0,), (0,)), ((), ())),
                          preferred_element_type=F32)
    gam_row = jnp.sum(expand(head_dup(p["gam"]), p["own_blk_seq"]), axis=0, keepdims=True)
    st_ref[h] = st * gam_row + upd
    o_ref[:, vs] = _rms(o, gn_ref[:, vs]) * gs_ref[:, vs]


def _gla_chunk_body(ins, gn_ref, o_ref, st_ref, b_sc, nb, c):
    p = _gla_chunk_prep(ins, b_sc, nb, c)
    for h in range(GLA_HEADS):
        _gla_chunk_head(p, h, ins, gn_ref, o_ref, st_ref)


def _gla_chunk_kernel(*refs, nb, c):
    ins, gn_ref, st0_ref, o_ref, st_ref, b_sc = refs[0:5], refs[5], refs[6], refs[7], refs[8], refs[9]

    @pl.when(_is_first())
    def _():
        st_ref[...] = st0_ref[...]

    _gla_chunk_body(ins, gn_ref, o_ref, st_ref, b_sc, nb, c)


def _gla_chunk(ins, gn, st0, nb, c, row_blocks=None):
    assert nb & (nb - 1) == 0 and (nb * GLA_DK) % LANES == 0
    rows = nb * c
    n = ins[0].shape[0] // row_blocks[0] if row_blocks else ins[0].shape[0]
    first = (row_blocks[1] * n) // rows if row_blocks else 0
    sshape = (GLA_HEADS, GLA_DV, nb * GLA_DK)
    cols = (GLA_K, GLA_K, GLA_V, GLA_K, GLA_V)
    kern = functools.partial(_gla_chunk_kernel, nb=nb, c=c)
    return pl.pallas_call(
        kern,
        grid=(n // rows,),
        in_specs=[pl.BlockSpec((rows, w_), lambda i: (first + i, 0)) for w_ in cols]
                 + [_const_spec((1, GLA_V)), _const_spec(sshape)],
        out_specs=[_row_spec(rows, GLA_V), pl.BlockSpec(sshape, lambda i: (0, 0, 0))],
        out_shape=[jax.ShapeDtypeStruct((n, GLA_V), F32), jax.ShapeDtypeStruct(sshape, F32)],
        scratch_shapes=[pltpu.VMEM((rows, GLA_K), F32)],
        compiler_params=_params(),
        name="gla_chunk",
    )(*ins, gn, st0)


def _gla_state_from_stacked(st):
    nb = st.shape[2] // GLA_DK
    return jnp.transpose(st.reshape(GLA_HEADS, GLA_DV, nb, GLA_DK), (2, 0, 3, 1))


def _gla_step_kernel(q_ref, k_ref, la_ref, v_ref, gs_ref, gn_ref, s0_ref, o_ref, s_ref):
    qT = q_ref[...].T
    kT = k_ref[...].T
    aT = jnp.exp(la_ref[...]).T
    o_rows = []
    for b in range(q_ref.shape[0]):
        s_new = aT[:, b:b + 1] * s0_ref[b] + kT[:, b:b + 1] * v_ref[b:b + 1, :]
        s_ref[b] = s_new
        o_rows.append(jnp.sum(qT[:, b:b + 1] * s_new, axis=0, keepdims=True))
    o = jnp.concatenate(o_rows, axis=0)
    o_ref[...] = _rms(o, gn_ref[...]) * gs_ref[...]


def _gla_step(side, gn, s0, row0):
    nb = s0.shape[0]
    blk = row0 // nb
    q, k, v, la, gs = side
    heads = lambda z: jnp.transpose(z[row0:row0 + nb].reshape(nb, GLA_HEADS, GLA_DK), (1, 0, 2))
    hspec = pl.BlockSpec((None, nb, GLA_DK), lambda h: (h, 0, 0))
    vspec = pl.BlockSpec((nb, GLA_DV), lambda h: (blk, h))
    ospec = pl.BlockSpec((nb, GLA_DV), lambda h: (0, h))
    sspec = pl.BlockSpec((nb, None, GLA_DK, GLA_DV), lambda h: (0, h, 0, 0))
    return pl.pallas_call(
        _gla_step_kernel,
        grid=(GLA_HEADS,),
        in_specs=[hspec, hspec, hspec, vspec, vspec, pl.BlockSpec((1, GLA_DV), lambda h: (0, h)), sspec],
        out_specs=[ospec, sspec],
        out_shape=[jax.ShapeDtypeStruct((nb, GLA_V), F32),
                   jax.ShapeDtypeStruct((nb, GLA_HEADS, GLA_DK, GLA_DV), F32)],
        compiler_params=_params("parallel"),
        name="gla_step",
    )(heads(q), heads(k), heads(la), v, gs, gn, s0)


def _l0_out_body(u_ref, og_ref, x, wts, xo_ref, xre_ref, xim_ref, sr_sc, si_sc, nb, tt):
    npack = S5_GROUPS // S5_PACK
    for j in range(npack):
        _s5_input(u_ref, wts, sr_sc, si_sc, j)
    ys = []
    for j in range(npack):
        _s5_scan(wts, xre_ref, xim_ref, sr_sc, si_sc, j, nb, tt)
        ys.append(_s5_output(wts, sr_sc, si_sc, j))
    _l0_tail_body(_s5_glu(ys, u_ref, wts), og_ref, x, wts, xo_ref)


S5_KIN = S5_PACK * S5_H
S5_KST = S5_PACK * S5_P


def _s5_input(u_ref, wts, sr_sc, si_sc, j):
    bbre_ref, bbim_ref = wts[0:2]
    cs = slice(j * S5_KST, (j + 1) * S5_KST)
    uj = u_ref[:, j * S5_KIN:(j + 1) * S5_KIN].astype(BF16)
    sr_sc[:, cs] = jnp.dot(uj, bbre_ref[j], preferred_element_type=F32)
    si_sc[:, cs] = jnp.dot(uj, bbim_ref[j], preferred_element_type=F32)


def _s5_scan(wts, xre_ref, xim_ref, sr_sc, si_sc, j, nb, tt):
    are_ref, aim_ref = wts[2:4]
    cs = slice(j * S5_KST, (j + 1) * S5_KST)
    ar = jnp.broadcast_to(are_ref[:, cs], (nb, S5_KST))
    ai = jnp.broadcast_to(aim_ref[:, cs], (nb, S5_KST))
    xr = xre_ref[:, cs]
    xi = xim_ref[:, cs]
    for t in range(tt):
        rows = slice(t * nb, (t + 1) * nb)
        xr, xi = ar * xr - ai * xi + sr_sc[rows, cs], ar * xi + ai * xr + si_sc[rows, cs]
        sr_sc[rows, cs] = xr
        si_sc[rows, cs] = xi
    xre_ref[:, cs] = xr
    xim_ref[:, cs] = xi


def _s5_output(wts, sr_sc, si_sc, j):
    cre_ref, cim_ref = wts[4:6]
    cs = slice(j * S5_KST, (j + 1) * S5_KST)
    return (jnp.dot(sr_sc[:, cs].astype(BF16), cre_ref[j], preferred_element_type=F32)
            - jnp.dot(si_sc[:, cs].astype(BF16), cim_ref[j], preferred_element_type=F32))


def _s5_glu(ys, u_ref, wts):
    d_ref, wglu_ref, bglu_ref = wts[6:9]
    y = jax.nn.gelu(jnp.concatenate(ys, axis=-1) + d_ref[...] * u_ref[...])
    return y * _sigmoid(_mm(y, wglu_ref[...]) + bglu_ref[...])


def _l0_tail_body(y, og_ref, x, wts, xo_ref):
    woa_ref, wob_ref = wts[9:11]
    xo_ref[...] = x + _mm(og_ref[...], woa_ref[...]) + _mm(y, wob_ref[...])


L0_OUT_WEIGHTS = 11


def _l0_out_weight_specs():
    npack = S5_GROUPS // S5_PACK
    kin, kst = S5_PACK * S5_H, S5_PACK * S5_P
    return [_const_spec((npack, kin, kst)), _const_spec((npack, kin, kst)),
            _const_spec((1, S5_N)), _const_spec((1, S5_N)),
            _const_spec((npack, kst, kin)), _const_spec((npack, kst, kin)),
            _const_spec((1, S5_W)), _const_spec((S5_W, S5_W)), _const_spec((1, S5_W)),
            _const_spec((GLA_V, D_MODEL), (0, 0)), _const_spec((S5_W, D_MODEL), (1, 0))]


def _l0_out_weights(w):
    return (w["s5_bbre"], w["s5_bbim"], w["s5_are"], w["s5_aim"], w["s5_cre"], w["s5_cim"],
            w["s5_d"], w["s5_w_glu"], w["s5_b_glu"], w["w_out_0"], w["w_out_0"])


def _l0_out_side_kernel(*refs, nb, tt_meta, nb_s):
    um_ref, ogm_ref, xm_ref, us_ref, ogs_ref, xs_ref, xr0s_ref, xi0s_ref = refs[0:8]
    wts = refs[8:8 + L0_OUT_WEIGHTS]
    xom_ref, xre_ref, xim_ref, xos_ref, xres_ref, xims_ref, sr_sc, si_sc = refs[8 + L0_OUT_WEIGHTS:]
    xre_ref[...] = jnp.zeros(xre_ref.shape, F32)
    xim_ref[...] = jnp.zeros(xim_ref.shape, F32)
    _l0_out_body(um_ref, ogm_ref, xm_ref[...], wts, xom_ref, xre_ref, xim_ref, sr_sc, si_sc, nb, tt_meta)
    xres_ref[...] = xr0s_ref[...]
    xims_ref[...] = xi0s_ref[...]
    _l0_out_body(us_ref, ogs_ref, xs_ref[...], wts, xos_ref, xres_ref, xims_ref, sr_sc, si_sc, nb_s, 1)


def _l0_out_side(u_side, og_meta, og_samp, x_side, xr0_s, xi0_s, w, nb, tt_meta):
    mrows, nb_s = nb * tt_meta, xr0_s.shape[0]
    assert mrows == nb_s
    kern = functools.partial(_l0_out_side_kernel, nb=nb, tt_meta=tt_meta, nb_s=nb_s)
    f32 = lambda *shape: jax.ShapeDtypeStruct(shape, F32)
    full = lambda r, c: pl.BlockSpec((r, c), lambda i: (0, 0))
    return pl.pallas_call(
        kern,
        grid=(1,),
        in_specs=[_side_spec(mrows, S5_W, 0), full(mrows, GLA_V), _side_spec(mrows, D_MODEL, 0),
                  _side_spec(nb_s, S5_W, 1), full(nb_s, GLA_V), _side_spec(nb_s, D_MODEL, 1),
                  full(nb_s, S5_N), full(nb_s, S5_N)] + _l0_out_weight_specs(),
        out_specs=[full(mrows, D_MODEL), full(nb, S5_N), full(nb, S5_N), full(nb_s, D_MODEL),
                   full(nb_s, S5_N), full(nb_s, S5_N)],
        out_shape=[f32(mrows, D_MODEL), f32(nb, S5_N), f32(nb, S5_N), f32(nb_s, D_MODEL),
                   f32(nb_s, S5_N), f32(nb_s, S5_N)],
        scratch_shapes=[pltpu.VMEM((mrows, S5_N), F32)] * 2,
        compiler_params=_params(),
        name="l0_out_side",
    )(u_side, og_meta, x_side, u_side, og_samp, x_side, xr0_s, xi0_s, *_l0_out_weights(w))


def _l0_kernel(*refs, nb, tt, c):
    x_ref, st0_ref, xr0_ref, xi0_ref = refs[0:4]
    n_in = 4 + L0_IN_WEIGHTS
    in_wts, gn_ref, out_wts = refs[4:n_in], refs[n_in], refs[n_in + 1:n_in + 1 + L0_OUT_WEIGHTS]
    rest = refs[n_in + 1 + L0_OUT_WEIGHTS:]
    xo_ref, st_ref, xre_ref, xim_ref = rest[0:4]
    proj_sc, og_sc, b_sc, sr_sc, si_sc, tm_sc = rest[4:10], rest[10], rest[11], rest[12], rest[13], rest[14]

    @pl.when(_is_first())
    def _():
        st_ref[...] = st0_ref[...]
        xre_ref[...] = xr0_ref[...]
        xim_ref[...] = xi0_ref[...]

    x = _load_time_major(x_ref, tm_sc)
    _l0_in_body(x, in_wts, proj_sc)
    q_sc, k_sc, v_sc, gs_sc, la_sc, u_sc = proj_sc
    npack = S5_GROUPS // S5_PACK
    nchunk = tt // c
    units = [(j, hp) for j in range(nchunk) for hp in range(GLA_HEADS // 2)]
    assert len(units) == npack
    for j in range(npack):
        _s5_input(u_sc, out_wts, sr_sc, si_sc, j)
    ys, prep = [], {}
    for i, (j, hp) in enumerate(units):
        _s5_scan(out_wts, xre_ref, xim_ref, sr_sc, si_sc, i, nb, tt)
        rws = pl.ds(j * nb * c, nb * c)
        ins = [r.at[rws] for r in (q_sc, k_sc, v_sc, la_sc, gs_sc)]
        if hp == 0:
            prep = _gla_chunk_prep(ins, b_sc, nb, c)
        for h in (2 * hp, 2 * hp + 1):
            _gla_chunk_head(prep, h, ins, gn_ref, og_sc.at[rws], st_ref)
        ys.append(_s5_output(out_wts, sr_sc, si_sc, i))
    _l0_tail_body(_s5_glu(ys, u_sc, out_wts), og_sc, x, out_wts, xo_ref)


def _l0(x, st0, xr0, xi0, w, tt, c):
    nb, nt, _ = x.shape
    n, rows = nb * nt, nb * tt
    assert nb & (nb - 1) == 0 and (nb * GLA_DK) % LANES == 0 and tt % c == 0
    sshape = (GLA_HEADS, GLA_DV, nb * GLA_DK)
    kern = functools.partial(_l0_kernel, nb=nb, tt=tt, c=c)
    st_spec = pl.BlockSpec((nb, S5_N), lambda i: (0, 0))
    f32 = lambda *shape: jax.ShapeDtypeStruct(shape, F32)
    return pl.pallas_call(
        kern,
        grid=(n // rows,),
        in_specs=[_seq_spec(nb, tt, D_MODEL), _const_spec(sshape), _const_spec((nb, S5_N)),
                  _const_spec((nb, S5_N))] + _l0_in_weight_specs() + [_const_spec((1, GLA_V))]
                 + _l0_out_weight_specs(),
        out_specs=[_row_spec(rows, D_MODEL), pl.BlockSpec(sshape, lambda i: (0, 0, 0)), st_spec, st_spec],
        out_shape=[f32(n, D_MODEL), f32(*sshape), f32(nb, S5_N), f32(nb, S5_N)],
        scratch_shapes=[pltpu.VMEM((rows, cols), F32) for cols in L0_IN_COLS]
                       + [pltpu.VMEM((rows, GLA_V), F32), pltpu.VMEM((nb * c, GLA_K), F32),
                          pltpu.VMEM((rows, S5_N), F32), pltpu.VMEM((rows, S5_N), F32),
                          _tm_scratch(rows, D_MODEL)],
        compiler_params=_params(),
        name="l0_mixer",
    )(x, st0, xr0, xi0, *_l0_in_weights(w), w["gla_norm"], *_l0_out_weights(w))


def _ffn_body(x, wts, c0_ref, c_ref, hm_sc, nb, tt, final):
    g_ref, wg_ref, wv_ref, cw_ref, cb_ref, wd_ref, gf_ref = wts
    rows = nb * tt
    xn = _rms(x, g_ref[...]).astype(BF16)
    for ci in range(D_FF // FF_CHUNK):
        cs = slice(ci * FF_CHUNK, (ci + 1) * FF_CHUNK)
        gate = jnp.dot(xn, wg_ref[:, cs], preferred_element_type=F32)
        val = jnp.dot(xn, wv_ref[:, cs], preferred_element_type=F32)
        if tt == 1:
            taps = [c0_ref[:, j, cs] for j in range(FFN_CONV - 1)] + [gate]
            for j in range(FFN_CONV - 1):
                c_ref[:, j, cs] = taps[j + 1]
        else:
            ext = jnp.concatenate([c_ref[:, cs], gate], axis=0)
            taps = [ext[j * nb:j * nb + rows] for j in range(FFN_CONV)]
            c_ref[:, cs] = ext[tt * nb:(tt + FFN_CONV - 1) * nb]
        y = cb_ref[:, cs] + taps[0] * cw_ref[0:1, cs]
        for j in range(1, FFN_CONV):
            y = y + taps[j] * cw_ref[j:j + 1, cs]
        hm_sc[:, cs] = (jax.nn.gelu(y) * val).astype(BF16)
    out = x + jnp.dot(hm_sc[...], wd_ref[...], preferred_element_type=F32)
    return _rms(out, gf_ref[...]) if final else out


FFN_WEIGHTS = 7


def _ffn_weight_specs(layer):
    return [_const_spec((None, 1, D_MODEL), (layer, 0, 0)),
            _const_spec((None, D_MODEL, D_FF), (layer, 0, 0)),
            _const_spec((None, D_MODEL, D_FF), (layer, 0, 1)),
            _const_spec((None, FFN_CONV, D_FF), (layer, 0, 0)),
            _const_spec((None, 1, D_FF), (layer, 0, 0)),
            _const_spec((None, D_FF, D_MODEL), (layer, 0, 0)), _const_spec((1, D_MODEL))]


def _ffn_weights(w):
    return (w["norm_ffn"], w["ffn_w_up"], w["ffn_w_up"], w["ffn_conv_w"], w["ffn_conv_b"], w["ffn_w_down"],
            w["norm_final"])


def _ffn_kernel(*refs, nb, tt, final, batch_major_out):
    x_ref, c0_ref = refs[0:2]
    wts = refs[2:2 + FFN_WEIGHTS]
    xo_ref, c_ref, hm_sc = refs[2 + FFN_WEIGHTS:5 + FFN_WEIGHTS]
    tm_sc = refs[5 + FFN_WEIGHTS] if batch_major_out else None

    @pl.when(_is_first())
    def _():
        c_ref[...] = c0_ref[...]

    _store_time_major(xo_ref, _ffn_body(x_ref[...], wts, None, c_ref, hm_sc, nb, tt, final), tm_sc)


def _ffn(x, c0, w, layer, nb, tt, final, batch_major_out=False):
    n = x.shape[0]
    rows = nb * tt
    kern = functools.partial(_ffn_kernel, nb=nb, tt=tt, final=final, batch_major_out=batch_major_out)
    if batch_major_out:
        o_spec, o_shape = _seq_spec(nb, tt, D_MODEL), (nb, n // nb, D_MODEL)
    else:
        o_spec, o_shape = _row_spec(rows, D_MODEL), (n, D_MODEL)
    cshape = _cache_shape(nb, tt, FFN_CONV - 1, D_FF)
    f32 = lambda *shape: jax.ShapeDtypeStruct(shape, F32)
    return pl.pallas_call(
        kern,
        grid=(n // rows,),
        in_specs=[_row_spec(rows, D_MODEL), _const_spec(cshape)] + _ffn_weight_specs(layer),
        out_specs=[o_spec, pl.BlockSpec(cshape, lambda i: (0, 0))],
        out_shape=[f32(*o_shape), f32(*cshape)],
        scratch_shapes=[pltpu.VMEM((rows, D_FF), BF16)]
                       + ([_tm_scratch(rows, D_MODEL)] if batch_major_out else []),
        compiler_params=_params(),
        name="ffn%d" % layer,
    )(x, c0, *_ffn_weights(w))


def _ffn_side_kernel(*refs, nb, tt_meta, nb_s, final):
    xm_ref, xs_ref, c0s_ref = refs[0:3]
    wts = refs[3:3 + FFN_WEIGHTS]
    xom_ref, c_ref, xos_ref, cs_ref, hm_sc = refs[3 + FFN_WEIGHTS:]
    c_ref[...] = jnp.zeros(c_ref.shape, F32)
    xom_ref[...] = _ffn_body(xm_ref[...], wts, None, c_ref, hm_sc, nb, tt_meta, final)
    xos_ref[...] = _ffn_body(xs_ref[...], wts, c0s_ref, cs_ref, hm_sc, nb_s, 1, final)


def _ffn_side(x_meta, x_samp, c0_s, w, layer, nb, tt_meta, final):
    mrows, nb_s = nb * tt_meta, x_samp.shape[0]
    assert mrows == nb_s
    kern = functools.partial(_ffn_side_kernel, nb=nb, tt_meta=tt_meta, nb_s=nb_s, final=final)
    cshape = _cache_shape(nb, tt_meta, FFN_CONV - 1, D_FF)
    cs_shape = _cache_shape(nb_s, 1, FFN_CONV - 1, D_FF)
    f32 = lambda *shape: jax.ShapeDtypeStruct(shape, F32)
    full = lambda shape: pl.BlockSpec(shape, lambda i: (0,) * len(shape))
    return pl.pallas_call(
        kern,
        grid=(1,),
        in_specs=[full((mrows, D_MODEL)), full((nb_s, D_MODEL)),
                  pl.BlockSpec((None,) + cs_shape, lambda i: (layer, 0, 0, 0))] + _ffn_weight_specs(layer),
        out_specs=[full((mrows, D_MODEL)), full(cshape), full((nb_s, D_MODEL)), full(cs_shape)],
        out_shape=[f32(mrows, D_MODEL), f32(*cshape), f32(nb_s, D_MODEL), f32(*cs_shape)],
        scratch_shapes=[pltpu.VMEM((mrows, D_FF), BF16)],
        compiler_params=_params(),
        name="ffn%d_side" % layer,
    )(x_meta, x_samp, c0_s, *_ffn_weights(w))


def _l1_body(x_ref, wts, xo_ref, h_ref, c0_ref, c_ref, a_sc, b_sc, nb, tt):
    g_ref, wgt_ref, wxr_ref, cw_ref, cb_ref, wa_ref, ba_ref, wx_ref, bx_ref, lam_ref, wo_ref = wts
    sp = jax.nn.softplus(-lam_ref[...])
    nsplit = L1_SPLIT if tt % L1_SPLIT == 0 else 1
    th = tt // nsplit
    rows = nb * th
    carry = c_ref[...] if tt > 1 else None
    xs, xns, ggs = [], [], []

    for part in range(nsplit):
        prow = slice(part * rows, (part + 1) * rows)
        x = x_ref[prow, :]
        xn = _rms(x, g_ref[...]).astype(BF16)
        xs.append(x)
        xns.append(xn)
        xr = jnp.dot(xn, wxr_ref[...], preferred_element_type=F32)
        if tt == 1:
            taps = [c0_ref[:, j, :] for j in range(RNN_CONV - 1)] + [xr]
            for j in range(RNN_CONV - 1):
                c_ref[:, j, :] = taps[j + 1]
        else:
            ext = jnp.concatenate([carry, xr], axis=0)
            taps = [ext[j * nb:j * nb + rows] for j in range(RNN_CONV)]
            carry = ext[th * nb:(th + RNN_CONV - 1) * nb]
        xc = cb_ref[...] + taps[0] * cw_ref[0:1, :]
        for j in range(1, RNN_CONV):
            xc = xc + taps[j] * cw_ref[j:j + 1, :]

        xcb = xc.astype(BF16)
        rs, gs = [], []
        for wi in range(RNN_W // GATE_WIN):
            for ni in range(GATE_WIN // GATE_N):
                k0 = wi * GATE_WIN + ni * LANES
                lhs = xcb[:, k0:k0 + GATE_K]
                rs.append(jnp.dot(lhs, wa_ref[wi, ni], preferred_element_type=F32))
                gs.append(jnp.dot(lhs, wx_ref[wi, ni], preferred_element_type=F32))
        r = _sigmoid(jnp.concatenate(rs, axis=-1) + ba_ref[...])
        ig = _sigmoid(jnp.concatenate(gs, axis=-1) + bx_ref[...])
        log_a = (-RNN_C) * r * sp
        a = jnp.exp(log_a)
        a_sc[prow, :] = a
        b_sc[prow, :] = _sqrt_nonneg(jnp.tanh(-log_a) * (a * a + 1.0)) * (ig * xc)
    if tt > 1:
        c_ref[...] = carry

    for part in range(nsplit):
        ggs.append(jax.nn.gelu(jnp.dot(xns[part], wgt_ref[...], preferred_element_type=F32)))

    h = h_ref[...]
    for part in range(nsplit):
        prow = slice(part * rows, (part + 1) * rows)
        for t in range(part * th, (part + 1) * th):
            rws = slice(t * nb, (t + 1) * nb)
            h = a_sc[rws, :] * h + b_sc[rws, :]
            b_sc[rws, :] = h
        xo_ref[prow, :] = xs[part] + _mm(b_sc[prow, :] * ggs[part], wo_ref[...])
    h_ref[...] = h


L1_WEIGHTS = 11


def _l1_weight_specs():
    gshape = (RNN_W // GATE_WIN, GATE_WIN // GATE_N, GATE_K, GATE_N)
    return [_const_spec((1, D_MODEL)),
            _const_spec((D_MODEL, RNN_W), (0, 0)), _const_spec((D_MODEL, RNN_W), (0, 1)),
            _const_spec((RNN_CONV, RNN_W)), _const_spec((1, RNN_W)),
            _const_spec(gshape), _const_spec((1, RNN_W)),
            _const_spec(gshape), _const_spec((1, RNN_W)),
            _const_spec((1, RNN_W)), _const_spec((RNN_W, D_MODEL))]


def _l1_weights(w):
    return (w["norm_mix_1"], w["w_in_1"], w["w_in_1"], w["rnn_conv_w"], w["rnn_conv_b"], w["rnn_wa"],
            w["rnn_b_a"], w["rnn_wx"], w["rnn_b_x"], w["rnn_lam"], w["w_out_1"])


def _l1_kernel(*refs, nb, tt):
    x_ref, h0_ref, c0_ref = refs[0:3]
    wts = refs[3:3 + L1_WEIGHTS]
    xo_ref, h_ref, c_ref, a_sc, b_sc = refs[3 + L1_WEIGHTS:]

    @pl.when(_is_first())
    def _():
        h_ref[...] = h0_ref[...]
        c_ref[...] = c0_ref[...]

    _l1_body(x_ref, wts, xo_ref, h_ref, None, c_ref, a_sc, b_sc, nb, tt)


def _l1(x, h0, c0, w, nb, tt):
    n = x.shape[0]
    rows = nb * tt
    kern = functools.partial(_l1_kernel, nb=nb, tt=tt)
    cshape = _cache_shape(nb, tt, RNN_CONV - 1, RNN_W)
    f32 = lambda *shape: jax.ShapeDtypeStruct(shape, F32)
    return pl.pallas_call(
        kern,
        grid=(n // rows,),
        in_specs=[_row_spec(rows, D_MODEL), _const_spec((nb, RNN_W)), _const_spec(cshape)] + _l1_weight_specs(),
        out_specs=[_row_spec(rows, D_MODEL), pl.BlockSpec((nb, RNN_W), lambda i: (0, 0)),
                   pl.BlockSpec(cshape, lambda i: (0, 0))],
        out_shape=[f32(n, D_MODEL), f32(nb, RNN_W), f32(*cshape)],
        scratch_shapes=[pltpu.VMEM((rows, RNN_W), F32)] * 2,
        compiler_params=_params(),
        name="l1_mixer",
    )(x, h0, c0, *_l1_weights(w))


def _l1_side_kernel(*refs, nb, tt_meta, nb_s):
    xm_ref, xs_ref, h0s_ref, c0s_ref = refs[0:4]
    wts = refs[4:4 + L1_WEIGHTS]
    xom_ref, h_ref, c_ref, xos_ref, hs_ref, cs_ref, a_sc, b_sc = refs[4 + L1_WEIGHTS:]
    h_ref[...] = jnp.zeros(h_ref.shape, F32)
    c_ref[...] = jnp.zeros(c_ref.shape, F32)
    _l1_body(xm_ref, wts, xom_ref, h_ref, None, c_ref, a_sc, b_sc, nb, tt_meta)
    hs_ref[...] = h0s_ref[...]
    _l1_body(xs_ref, wts, xos_ref, hs_ref, c0s_ref, cs_ref, a_sc, b_sc, nb_s, 1)


def _l1_side(x_meta, x_samp, h0_s, c0_s, w, nb, tt_meta):
    mrows, nb_s = nb * tt_meta, x_samp.shape[0]
    assert mrows == nb_s
    kern = functools.partial(_l1_side_kernel, nb=nb, tt_meta=tt_meta, nb_s=nb_s)
    cshape = _cache_shape(nb, tt_meta, RNN_CONV - 1, RNN_W)
    cs_shape = _cache_shape(nb_s, 1, RNN_CONV - 1, RNN_W)
    f32 = lambda *shape: jax.ShapeDtypeStruct(shape, F32)
    full = lambda shape: pl.BlockSpec(shape, lambda i: (0,) * len(shape))
    return pl.pallas_call(
        kern,
        grid=(1,),
        in_specs=[full((mrows, D_MODEL)), full((nb_s, D_MODEL)), full((nb_s, RNN_W)), full(cs_shape)]
                 + _l1_weight_specs(),
        out_specs=[full((mrows, D_MODEL)), full((nb, RNN_W)), full(cshape), full((nb_s, D_MODEL)),
                   full((nb_s, RNN_W)), full(cs_shape)],
        out_shape=[f32(mrows, D_MODEL), f32(nb, RNN_W), f32(*cshape), f32(nb_s, D_MODEL), f32(nb_s, RNN_W),
                   f32(*cs_shape)],
        scratch_shapes=[pltpu.VMEM((mrows, RNN_W), F32)] * 2,
        compiler_params=_params(),
        name="l1_side",
    )(x_meta, x_samp, h0_s, c0_s, *_l1_weights(w))


def _pack_gate_kernel(wa_ref, wx_ref, oa_ref, ox_ref):
    tiles_per_win = GATE_WIN // GATE_N
    for w_ref, o_ref in ((wa_ref, oa_ref), (wx_ref, ox_ref)):
        o_ref[...] = jnp.zeros(o_ref.shape, o_ref.dtype)
        for n in range(RNN_BLOCKS):
            pos = n * RNN_BW
            wi = pos // GATE_WIN
            for ni in range(tiles_per_win):
                k0 = wi * GATE_WIN + ni * LANES
                n0 = wi * GATE_WIN + ni * GATE_N
                lo, hi = max(pos, n0), min(pos + RNN_BW, n0 + GATE_N)
                if lo < hi:
                    o_ref[wi, ni, pos - k0:pos - k0 + RNN_BW, lo - n0:hi - n0] = (
                        w_ref[n][:, lo - pos:hi - pos].astype(o_ref.dtype))


def _pack_gates(wa, wx):
    gshape = (RNN_W // GATE_WIN, GATE_WIN // GATE_N, GATE_K, GATE_N)
    return pl.pallas_call(
        _pack_gate_kernel,
        out_shape=(jax.ShapeDtypeStruct(gshape, BF16), jax.ShapeDtypeStruct(gshape, BF16)),
        name="pack_gates",
    )(wa, wx)


def _prep_weights(p):
    w = {}
    row = lambda v: v.reshape(1, -1).astype(F32)
    w_in = p["w_in_0"]
    c = 2 * GLA_K + 2 * GLA_V
    w["w_in_0"] = w_in.astype(BF16)
    w["w_lr"] = jnp.pad(w_in[:, c:c + GLA_RANK], ((0, 0), (0, LANES - GLA_RANK))).astype(BF16)
    c += GLA_RANK
    w["w_u"] = w_in[:, c:c + S5_W].astype(BF16)
    w["w_alpha"] = jnp.pad(p["w_alpha_0"], ((0, LANES - GLA_RANK), (0, 0))).astype(BF16)
    w["b_alpha"] = row(p["b_alpha_0"])
    w["norm_mix_0"] = row(p["norm_mix_0"])
    w["gla_norm"] = row(p["gla_norm_0"])

    are, aim, bbre, bbim = _s5_prep(p["s5_lam_re"], p["s5_lam_im"], p["s5_log_dt"], p["s5_b_re"],
                                    p["s5_b_im"])
    npack = S5_GROUPS // S5_PACK
    eye = jnp.eye(S5_PACK, dtype=F32)[None, :, None, :, None]
    grouped = lambda m: m.reshape(npack, S5_PACK, S5_H, S5_P)
    pack_b = lambda m: (grouped(m)[:, :, :, None, :] * eye).reshape(
        npack, S5_PACK * S5_H, S5_PACK * S5_P).astype(BF16)
    pack_c = lambda m: (jnp.swapaxes(grouped(m), 2, 3)[:, :, :, None, :] * eye).reshape(
        npack, S5_PACK * S5_P, S5_PACK * S5_H).astype(BF16)
    w["s5_are"] = are.reshape(1, S5_N)
    w["s5_aim"] = aim.reshape(1, S5_N)
    w["s5_bbre"] = pack_b(bbre)
    w["s5_bbim"] = pack_b(bbim)
    w["s5_cre"] = pack_c(p["s5_c_re"])
    w["s5_cim"] = pack_c(p["s5_c_im"])
    w["s5_d"] = row(p["s5_d"])
    w["s5_w_glu"] = p["s5_w_glu"].astype(BF16)
    w["s5_b_glu"] = row(p["s5_b_glu"])
    w["w_out_0"] = p["w_out_0"].astype(BF16)

    w["norm_mix_1"] = row(p["norm_mix_1"])
    w["w_in_1"] = p["w_in_1"].astype(BF16)
    w["rnn_conv_w"] = p["rnn_conv_w"].astype(F32)
    w["rnn_conv_b"] = row(p["rnn_conv_b"])
    w["rnn_wa"], w["rnn_wx"] = _pack_gates(p["rnn_w_a"], p["rnn_w_x"])
    w["rnn_b_a"] = row(p["rnn_b_a"])
    w["rnn_b_x"] = row(p["rnn_b_x"])
    w["rnn_lam"] = row(p["rnn_lam"])
    w["w_out_1"] = p["w_out_1"].astype(BF16)

    depth = p["norm_ffn"].shape[0]
    w["norm_ffn"] = p["norm_ffn"].reshape(depth, 1, D_MODEL)
    w["ffn_w_up"] = p["ffn_w_up"].astype(BF16)
    w["ffn_conv_w"] = p["ffn_conv_w"]
    w["ffn_conv_b"] = p["ffn_conv_b"].reshape(depth, 1, D_FF)
    w["ffn_w_down"] = p["ffn_w_down"].astype(BF16)
    w["norm_final"] = row(p["norm_final"])
    return w


def _tile_steps():
    return dict(l0=64, gla=32, ffn=128, l1=128)


def _batch_major(cache, nb):
    jb, c = cache.shape
    return jnp.transpose(cache.reshape(jb // nb, nb, c), (1, 0, 2))


def kernel(x_prompt, x_sample, state_gla, state_s5_re, state_s5_im, state_rglru, cache_rglru_conv,
           cache_ffn_conv, meta_tokens, norm_mix_0, w_in_0, w_alpha_0, b_alpha_0, gla_norm_0,
           s5_lam_re, s5_lam_im, s5_log_dt, s5_b_re, s5_b_im, s5_c_re, s5_c_im, s5_d, s5_w_glu,
           s5_b_glu, w_out_0, norm_mix_1, w_in_1, rnn_conv_w, rnn_conv_b, rnn_w_a, rnn_b_a, rnn_w_x,
           rnn_b_x, rnn_lam, w_out_1, norm_ffn, ffn_w_up, ffn_conv_w, ffn_conv_b, ffn_w_down, norm_final):
    w = _prep_weights(dict(
        norm_mix_0=norm_mix_0, w_in_0=w_in_0, w_alpha_0=w_alpha_0, b_alpha_0=b_alpha_0,
        gla_norm_0=gla_norm_0, s5_lam_re=s5_lam_re, s5_lam_im=s5_lam_im, s5_log_dt=s5_log_dt,
        s5_b_re=s5_b_re, s5_b_im=s5_b_im, s5_c_re=s5_c_re, s5_c_im=s5_c_im, s5_d=s5_d,
        s5_w_glu=s5_w_glu, s5_b_glu=s5_b_glu, w_out_0=w_out_0, norm_mix_1=norm_mix_1, w_in_1=w_in_1,
        rnn_conv_w=rnn_conv_w, rnn_conv_b=rnn_conv_b, rnn_w_a=rnn_w_a, rnn_b_a=rnn_b_a,
        rnn_w_x=rnn_w_x, rnn_b_x=rnn_b_x, rnn_lam=rnn_lam, w_out_1=w_out_1, norm_ffn=norm_ffn,
        ffn_w_up=ffn_w_up, ffn_conv_w=ffn_conv_w, ffn_conv_b=ffn_conv_b, ffn_w_down=ffn_w_down,
        norm_final=norm_final))

    bp = x_prompt.shape[0]
    bs = x_sample.shape[0]
    tt = _tile_steps()
    mrows = bp * N_META
    assert mrows == bs

    x_side = jnp.concatenate([jnp.repeat(meta_tokens.astype(F32), bp, axis=0),
                              x_sample.reshape(bs, D_MODEL)], axis=0)
    s5_re_s = state_s5_re.reshape(bs, S5_N)
    s5_im_s = state_s5_im.reshape(bs, S5_N)

    gla_in = lambda z: (z[0], z[1], z[2], z[4], z[3])
    gn = w["gla_norm"]

    side = _l0_in(x_side, w)
    og_m, gla_m = _gla_chunk(gla_in(side), gn, jnp.zeros((GLA_HEADS, GLA_DV, bp * GLA_DK), F32), bp, N_META,
                             row_blocks=(2, 0))
    og_s, gla_s = _gla_step(gla_in(side), gn, state_gla, mrows)
    x_m, re_m, im_m, x_s, re_s, im_s = _l0_out_side(side[5], og_m, og_s, x_side, s5_re_s, s5_im_s, w, bp, N_META)
    x_m, fc0_m, x_s, fc0_s = _ffn_side(x_m, x_s, cache_ffn_conv, w, 0, bp, N_META, False)
    x_m, h_m, rc_m, x_s, h_s, rc_s = _l1_side(x_m, x_s, state_rglru, cache_rglru_conv, w, bp, N_META)
    _, fc1_m, ys, fc1_s = _ffn_side(x_m, x_s, cache_ffn_conv, w, 1, bp, N_META, True)

    x, gla_p, re_p, im_p = _l0(x_prompt, gla_m, re_m, im_m, w, tt["l0"], tt["gla"])
    x, fc0_p = _ffn(x, fc0_m, w, 0, bp, tt["ffn"], False)
    x, h_p, rc_p = _l1(x, h_m, rc_m, w, bp, tt["l1"])
    yp, fc1_p = _ffn(x, fc1_m, w, 1, bp, tt["ffn"], True, batch_major_out=True)

    grp = lambda z, nb: z.reshape(nb, S5_GROUPS, S5_P)
    return (yp, ys.reshape(bs, 1, D_MODEL), _gla_state_from_stacked(gla_p), gla_s,
            grp(re_p, bp), grp(re_s, bs), grp(im_p, bp), grp(im_s, bs), h_p, h_s,
            _batch_major(rc_p, bp), rc_s,
            jnp.stack([_batch_major(fc0_p, bp), _batch_major(fc1_p, bp)]), jnp.stack([fc0_s, fc1_s]))
```

```python
import functools

import jax
import jax.numpy as jnp
from jax import lax
from jax.experimental import pallas as pl
from jax.experimental.pallas import tpu as pltpu

F32 = jnp.float32
BF16 = jnp.bfloat16

D_MODEL = 1024
N_META = 16
EPS = 1e-6
F32_TINY = 1.1754944e-38
GLA_HEADS = 4
GLA_DK = 64
GLA_DV = 128
GLA_RANK = 16
GLA_TAU = 16.0
GLA_K = GLA_HEADS * GLA_DK
GLA_V = GLA_HEADS * GLA_DV
S5_GROUPS = 32
S5_H = 16
S5_P = 64
S5_W = S5_GROUPS * S5_H
S5_N = S5_GROUPS * S5_P
RNN_W = 1536
RNN_BLOCKS = 16
RNN_BW = RNN_W // RNN_BLOCKS
RNN_C = 8.0
RNN_CONV = 4
D_FF = 2816
FFN_CONV = 3

LANES = 128
FF_CHUNK = 256
L1_SPLIT = 4
S5_PACK = 8
GATE_WIN = 768
GATE_K = 512
GATE_N = 256
VMEM_LIMIT = 56 * 1024 * 1024


def _rms(x, g):
    return x * lax.rsqrt(jnp.mean(x * x, axis=-1, keepdims=True) + EPS) * g


def _sigmoid(x):
    return 0.5 * jnp.tanh(0.5 * x) + 0.5


def _sqrt_nonneg(t):
    return t * lax.rsqrt(jnp.maximum(t, F32_TINY))


def _mm(a, w):
    return jnp.dot(a.astype(BF16), w, preferred_element_type=F32)


def _const_spec(shape, index=None):
    idx = tuple(index) if index is not None else (0,) * len(shape)
    return pl.BlockSpec(shape, lambda i: idx, pipeline_mode=pl.Buffered(1))


def _row_spec(rows, cols):
    return pl.BlockSpec((rows, cols), lambda i: (i, 0))


def _seq_spec(nb, tt, cols):
    return pl.BlockSpec((nb, tt, cols), lambda i: (0, i, 0))


def _tm_scratch(rows, cols):
    return pltpu.VMEM((cols // LANES, rows, LANES), F32)


def _load_time_major(x_ref, tm_sc):
    nb, tt, cols = x_ref.shape
    for b in range(nb):
        for j in range(cols // LANES):
            tm_sc[j, pl.ds(b, tt, stride=nb), :] = x_ref[b, :, j * LANES:(j + 1) * LANES]
    return jnp.concatenate([tm_sc[j] for j in range(cols // LANES)], axis=-1)


def _store_time_major(o_ref, val, tm_sc):
    if tm_sc is None:
        o_ref[...] = val
        return
    nb, tt, cols = o_ref.shape
    for j in range(cols // LANES):
        tm_sc[j] = val[:, j * LANES:(j + 1) * LANES]
    for b in range(nb):
        for j in range(cols // LANES):
            o_ref[b, :, j * LANES:(j + 1) * LANES] = tm_sc[j, pl.ds(b, tt, stride=nb), :]


def _cache_shape(nb, tt, taps, width):
    return (nb, taps, width) if tt == 1 else (taps * nb, width)


def _params(sem="arbitrary"):
    return pltpu.CompilerParams(dimension_semantics=(sem,), vmem_limit_bytes=VMEM_LIMIT)


def _s5_prep_kernel(lr_ref, li_ref, ldt_ref, brt_ref, bit_ref, are_ref, aim_ref, bbre_ref, bbim_ref):
    lr = lr_ref[...]
    li = li_ref[...]
    dt = jnp.exp(ldt_ref[...])
    mag = jnp.exp(lr * dt)
    ab_re = mag * jnp.cos(li * dt)
    ab_im = mag * jnp.sin(li * dt)
    den = lr * lr + li * li
    nr = ab_re - 1.0
    ni = ab_im
    f_re = (nr * lr + ni * li) / den
    f_im = (ni * lr - nr * li) / den
    are_ref[...] = ab_re
    aim_ref[...] = ab_im
    brt = brt_ref[...]
    bit = bit_ref[...]
    bbre_ref[...] = f_re[:, None, :] * brt - f_im[:, None, :] * bit
    bbim_ref[...] = f_re[:, None, :] * bit + f_im[:, None, :] * brt


def _s5_prep(lam_re, lam_im, log_dt, b_re, b_im):
    g, p, h = b_re.shape
    brt = jnp.transpose(b_re, (0, 2, 1))
    bit = jnp.transpose(b_im, (0, 2, 1))
    return pl.pallas_call(
        _s5_prep_kernel,
        out_shape=(jax.ShapeDtypeStruct((g, p), F32), jax.ShapeDtypeStruct((g, p), F32),
                   jax.ShapeDtypeStruct((g, h, p), F32), jax.ShapeDtypeStruct((g, h, p), F32)),
        name="s5_prep",
    )(lam_re, lam_im, log_dt.reshape(g, 1), brt, bit)


def _is_first():
    return pl.program_id(0) == 0


def _side_spec(rows, cols, block=0):
    return pl.BlockSpec((rows, cols), lambda i: (block, 0))


def _l0_in_body(x, wts, outs):
    g_ref, wq_ref, wk_ref, wv_ref, wg_ref, wu_ref, wlr_ref, wal_ref, bal_ref = wts
    q_ref, k_ref, v_ref, gs_ref, la_ref, u_ref = outs
    xn = _rms(x, g_ref[...]).astype(BF16)
    u_ref[...] = _mm(xn, wu_ref[...])
    lr = _mm(xn, wlr_ref[...])
    pre = _mm(lr, wal_ref[...]) + bal_ref[...]
    la_ref[...] = jax.nn.log_sigmoid(pre) * (1.0 / GLA_TAU)
    k_ref[...] = _mm(xn, wk_ref[...])
    q_ref[...] = _mm(xn, wq_ref[...]) * (GLA_DK ** -0.5)
    v_ref[...] = _mm(xn, wv_ref[...])
    g = _mm(xn, wg_ref[...])
    gs_ref[...] = g * _sigmoid(g)


def _l0_in_kernel(*refs):
    x_ref, wts, outs = refs[0], refs[1:1 + L0_IN_WEIGHTS], refs[1 + L0_IN_WEIGHTS:]
    _l0_in_body(x_ref[...], wts, outs)


L0_IN_COLS = (GLA_K, GLA_K, GLA_V, GLA_V, GLA_K, S5_W)
L0_IN_WEIGHTS = 9


def _l0_in_weight_specs():
    return [_const_spec((1, D_MODEL)),
            _const_spec((D_MODEL, GLA_K), (0, 0)), _const_spec((D_MODEL, GLA_K), (0, 1)),
            _const_spec((D_MODEL, GLA_V), (0, 1)), _const_spec((D_MODEL, GLA_V), (0, 2)),
            _const_spec((D_MODEL, S5_W)), _const_spec((D_MODEL, LANES)),
            _const_spec((LANES, GLA_K)), _const_spec((1, GLA_K))]


def _l0_in_weights(w):
    return (w["norm_mix_0"], w["w_in_0"], w["w_in_0"], w["w_in_0"], w["w_in_0"], w["w_u"], w["w_lr"],
            w["w_alpha"], w["b_alpha"])


def _l0_in(x, w):
    n = x.shape[0]
    return pl.pallas_call(
        _l0_in_kernel,
        grid=(1,),
        in_specs=[_row_spec(n, D_MODEL)] + _l0_in_weight_specs(),
        out_specs=[_row_spec(n, c) for c in L0_IN_COLS],
        out_shape=[jax.ShapeDtypeStruct((n, c), F32) for c in L0_IN_COLS],
        compiler_params=_params(),
        name="l0_in",
    )(x, *_l0_in_weights(w))


def _gla_chunk_prep(ins, b_sc, nb, c):
    q_ref, k_ref, _, la_ref, _ = ins
    rows = nb * c
    seq_mask = nb - 1

    def cum_body(t, run):
        rws = pl.ds(pl.multiple_of(t * nb, nb), nb)
        run = run + la_ref[rws, :]
        b_sc[rws, :] = run
        return run

    bl = lax.fori_loop(0, c, cum_body, jnp.zeros((nb, GLA_K), F32), unroll=True)
    b = b_sc[...]
    q = q_ref[...]
    k = k_ref[...]
    bm = jnp.concatenate([b_sc[pl.ds((c // 2) * nb, nb), :]] * c, axis=0)
    ri = lax.broadcasted_iota(jnp.int32, (rows, rows), 0)
    ci = lax.broadcasted_iota(jnp.int32, (rows, rows), 1)
    xw = nb * GLA_DK
    return dict(
        qa=q * jnp.exp(b - bm), kt=k * jnp.exp(bm - b), qt=q * jnp.exp(b),
        kh=k * jnp.exp(jnp.concatenate([bl] * c, axis=0) - b),
        gam=jnp.exp(bl),
        pair_ok=(ri >= ci) & (((ri - ci) & seq_mask) == 0),
        own_blk=(lax.broadcasted_iota(jnp.int32, (rows, xw), 1) // GLA_DK
                 == (lax.broadcasted_iota(jnp.int32, (rows, xw), 0) & seq_mask)),
        own_blk_seq=(lax.broadcasted_iota(jnp.int32, (nb, xw), 1) // GLA_DK
                     == lax.broadcasted_iota(jnp.int32, (nb, xw), 0)),
        reps=xw // LANES)


def _gla_chunk_head(p, h, ins, gn_ref, o_ref, st_ref):
    v_ref, gs_ref = ins[2], ins[4]

    def head_dup(z):
        blk = z[:, (h // 2) * LANES:(h // 2 + 1) * LANES]
        rolled = pltpu.roll(blk, GLA_DK, axis=1)
        low = lax.broadcasted_iota(jnp.int32, blk.shape, 1) < GLA_DK
        return jnp.where(low, blk, rolled) if h % 2 == 0 else jnp.where(low, rolled, blk)

    def expand(zd, own):
        return jnp.where(own, jnp.concatenate([zd] * p["reps"], axis=1), 0.0)

    vs = slice(h * GLA_DV, (h + 1) * GLA_DV)
    qad, qd, ktd, khd = head_dup(p["qa"]), head_dup(p["qt"]), head_dup(p["kt"]), head_dup(p["kh"])
    vb = v_ref[:, vs].astype(BF16)
    st = st_ref[h]
    att = lax.dot_general(qad[:, :GLA_DK].astype(BF16), ktd[:, :GLA_DK].astype(BF16),
                          (((1,), (1,)), ((), ())), preferred_element_type=F32)
    att = jnp.where(p["pair_ok"], att, 0.0).astype(BF16)
    o = jnp.dot(att, vb, preferred_element_type=F32) + lax.dot_general(
        expand(qd, p["own_blk"]).astype(BF16), st.astype(BF16), (((1,), (1,)), ((), ())),
        preferred_element_type=F32)
    upd = lax.dot_general(vb, expand(khd, p["own_blk"]).astype(BF16), (((0,), (0,)), ((), ())),
                          preferred_element_type=F32)
    gam_row = jnp.sum(expand(head_dup(p["gam"]), p["own_blk_seq"]), axis=0, keepdims=True)
    st_ref[h] = st * gam_row + upd
    o_ref[:, vs] = _rms(o, gn_ref[:, vs]) * gs_ref[:, vs]


def _gla_chunk_body(ins, gn_ref, o_ref, st_ref, b_sc, nb, c):
    p = _gla_chunk_prep(ins, b_sc, nb, c)
    for h in range(GLA_HEADS):
        _gla_chunk_head(p, h, ins, gn_ref, o_ref, st_ref)


def _gla_chunk_kernel(*refs, nb, c):
    ins, gn_ref, st0_ref, o_ref, st_ref, b_sc = refs[0:5], refs[5], refs[6], refs[7], refs[8], refs[9]

    @pl.when(_is_first())
    def _():
        st_ref[...] = st0_ref[...]

    _gla_chunk_body(ins, gn_ref, o_ref, st_ref, b_sc, nb, c)


def _gla_chunk(ins, gn, st0, nb, c, row_blocks=None):
    assert nb & (nb - 1) == 0 and (nb * GLA_DK) % LANES == 0
    rows = nb * c
    n = ins[0].shape[0] // row_blocks[0] if row_blocks else ins[0].shape[0]
    first = (row_blocks[1] * n) // rows if row_blocks else 0
    sshape = (GLA_HEADS, GLA_DV, nb * GLA_DK)
    cols = (GLA_K, GLA_K, GLA_V, GLA_K, GLA_V)
    kern = functools.partial(_gla_chunk_kernel, nb=nb, c=c)
    return pl.pallas_call(
        kern,
        grid=(n // rows,),
        in_specs=[pl.BlockSpec((rows, w_), lambda i: (first + i, 0)) for w_ in cols]
                 + [_const_spec((1, GLA_V)), _const_spec(sshape)],
        out_specs=[_row_spec(rows, GLA_V), pl.BlockSpec(sshape, lambda i: (0, 0, 0))],
        out_shape=[jax.ShapeDtypeStruct((n, GLA_V), F32), jax.ShapeDtypeStruct(sshape, F32)],
        scratch_shapes=[pltpu.VMEM((rows, GLA_K), F32)],
        compiler_params=_params(),
        name="gla_chunk",
    )(*ins, gn, st0)


def _gla_state_from_stacked(st):
    nb = st.shape[2] // GLA_DK
    return jnp.transpose(st.reshape(GLA_HEADS, GLA_DV, nb, GLA_DK), (2, 0, 3, 1))


def _gla_step_kernel(q_ref, k_ref, la_ref, v_ref, gs_ref, gn_ref, s0_ref, o_ref, s_ref):
    qT = q_ref[...].T
    kT = k_ref[...].T
    aT = jnp.exp(la_ref[...]).T
    o_rows = []
    for b in range(q_ref.shape[0]):
        s_new = aT[:, b:b + 1] * s0_ref[b] + kT[:, b:b + 1] * v_ref[b:b + 1, :]
        s_ref[b] = s_new
        o_rows.append(jnp.sum(qT[:, b:b + 1] * s_new, axis=0, keepdims=True))
    o = jnp.concatenate(o_rows, axis=0)
    o_ref[...] = _rms(o, gn_ref[...]) * gs_ref[...]


def _gla_step(side, gn, s0, row0):
    nb = s0.shape[0]
    blk = row0 // nb
    q, k, v, la, gs = side
    heads = lambda z: jnp.transpose(z[row0:row0 + nb].reshape(nb, GLA_HEADS, GLA_DK), (1, 0, 2))
    hspec = pl.BlockSpec((None, nb, GLA_DK), lambda h: (h, 0, 0))
    vspec = pl.BlockSpec((nb, GLA_DV), lambda h: (blk, h))
    ospec = pl.BlockSpec((nb, GLA_DV), lambda h: (0, h))
    sspec = pl.BlockSpec((nb, None, GLA_DK, GLA_DV), lambda h: (0, h, 0, 0))
    return pl.pallas_call(
        _gla_step_kernel,
        grid=(GLA_HEADS,),
        in_specs=[hspec, hspec, hspec, vspec, vspec, pl.BlockSpec((1, GLA_DV), lambda h: (0, h)), sspec],
        out_specs=[ospec, sspec],
        out_shape=[jax.ShapeDtypeStruct((nb, GLA_V), F32),
                   jax.ShapeDtypeStruct((nb, GLA_HEADS, GLA_DK, GLA_DV), F32)],
        compiler_params=_params("parallel"),
        name="gla_step",
    )(heads(q), heads(k), heads(la), v, gs, gn, s0)


def _l0_out_body(u_ref, og_ref, x, wts, xo_ref, xre_ref, xim_ref, sr_sc, si_sc, nb, tt):
    npack = S5_GROUPS // S5_PACK
    for j in range(npack):
        _s5_input(u_ref, wts, sr_sc, si_sc, j)
    ys = []
    for j in range(npack):
        _s5_scan(wts, xre_ref, xim_ref, sr_sc, si_sc, j, nb, tt)
        ys.append(_s5_output(wts, sr_sc, si_sc, j))
    _l0_tail_body(_s5_glu(ys, u_ref, wts), og_ref, x, wts, xo_ref)


S5_KIN = S5_PACK * S5_H
S5_KST = S5_PACK * S5_P


def _s5_input(u_ref, wts, sr_sc, si_sc, j):
    bbre_ref, bbim_ref = wts[0:2]
    cs = slice(j * S5_KST, (j + 1) * S5_KST)
    uj = u_ref[:, j * S5_KIN:(j + 1) * S5_KIN].astype(BF16)
    sr_sc[:, cs] = jnp.dot(uj, bbre_ref[j], preferred_element_type=F32)
    si_sc[:, cs] = jnp.dot(uj, bbim_ref[j], preferred_element_type=F32)


def _s5_scan(wts, xre_ref, xim_ref, sr_sc, si_sc, j, nb, tt):
    are_ref, aim_ref = wts[2:4]
    cs = slice(j * S5_KST, (j + 1) * S5_KST)
    ar = jnp.broadcast_to(are_ref[:, cs], (nb, S5_KST))
    ai = jnp.broadcast_to(aim_ref[:, cs], (nb, S5_KST))
    xr = xre_ref[:, cs]
    xi = xim_ref[:, cs]
    for t in range(tt):
        rows = slice(t * nb, (t + 1) * nb)
        xr, xi = ar * xr - ai * xi + sr_sc[rows, cs], ar * xi + ai * xr + si_sc[rows, cs]
        sr_sc[rows, cs] = xr
        si_sc[rows, cs] = xi
    xre_ref[:, cs] = xr
    xim_ref[:, cs] = xi


def _s5_output(wts, sr_sc, si_sc, j):
    cre_ref, cim_ref = wts[4:6]
    cs = slice(j * S5_KST, (j + 1) * S5_KST)
    return (jnp.dot(sr_sc[:, cs].astype(BF16), cre_ref[j], preferred_element_type=F32)
            - jnp.dot(si_sc[:, cs].astype(BF16), cim_ref[j], preferred_element_type=F32))


def _s5_glu(ys, u_ref, wts):
    d_ref, wglu_ref, bglu_ref = wts[6:9]
    y = jax.nn.gelu(jnp.concatenate(ys, axis=-1) + d_ref[...] * u_ref[...])
    return y * _sigmoid(_mm(y, wglu_ref[...]) + bglu_ref[...])


def _l0_tail_body(y, og_ref, x, wts, xo_ref):
    woa_ref, wob_ref = wts[9:11]
    xo_ref[...] = x + _mm(og_ref[...], woa_ref[...]) + _mm(y, wob_ref[...])


L0_OUT_WEIGHTS = 11


def _l0_out_weight_specs():
    npack = S5_GROUPS // S5_PACK
    kin, kst = S5_PACK * S5_H, S5_PACK * S5_P
    return [_const_spec((npack, kin, kst)), _const_spec((npack, kin, kst)),
            _const_spec((1, S5_N)), _const_spec((1, S5_N)),
            _const_spec((npack, kst, kin)), _const_spec((npack, kst, kin)),
            _const_spec((1, S5_W)), _const_spec((S5_W, S5_W)), _const_spec((1, S5_W)),
            _const_spec((GLA_V, D_MODEL), (0, 0)), _const_spec((S5_W, D_MODEL), (1, 0))]


def _l0_out_weights(w):
    return (w["s5_bbre"], w["s5_bbim"], w["s5_are"], w["s5_aim"], w["s5_cre"], w["s5_cim"],
            w["s5_d"], w["s5_w_glu"], w["s5_b_glu"], w["w_out_0"], w["w_out_0"])


def _l0_out_side_kernel(*refs, nb, tt_meta, nb_s):
    um_ref, ogm_ref, xm_ref, us_ref, ogs_ref, xs_ref, xr0s_ref, xi0s_ref = refs[0:8]
    wts = refs[8:8 + L0_OUT_WEIGHTS]
    xom_ref, xre_ref, xim_ref, xos_ref, xres_ref, xims_ref, sr_sc, si_sc = refs[8 + L0_OUT_WEIGHTS:]
    xre_ref[...] = jnp.zeros(xre_ref.shape, F32)
    xim_ref[...] = jnp.zeros(xim_ref.shape, F32)
    _l0_out_body(um_ref, ogm_ref, xm_ref[...], wts, xom_ref, xre_ref, xim_ref, sr_sc, si_sc, nb, tt_meta)
    xres_ref[...] = xr0s_ref[...]
    xims_ref[...] = xi0s_ref[...]
    _l0_out_body(us_ref, ogs_ref, xs_ref[...], wts, xos_ref, xres_ref, xims_ref, sr_sc, si_sc, nb_s, 1)


def _l0_out_side(u_side, og_meta, og_samp, x_side, xr0_s, xi0_s, w, nb, tt_meta):
    mrows, nb_s = nb * tt_meta, xr0_s.shape[0]
    assert mrows == nb_s
    kern = functools.partial(_l0_out_side_kernel, nb=nb, tt_meta=tt_meta, nb_s=nb_s)
    f32 = lambda *shape: jax.ShapeDtypeStruct(shape, F32)
    full = lambda r, c: pl.BlockSpec((r, c), lambda i: (0, 0))
    return pl.pallas_call(
        kern,
        grid=(1,),
        in_specs=[_side_spec(mrows, S5_W, 0), full(mrows, GLA_V), _side_spec(mrows, D_MODEL, 0),
                  _side_spec(nb_s, S5_W, 1), full(nb_s, GLA_V), _side_spec(nb_s, D_MODEL, 1),
                  full(nb_s, S5_N), full(nb_s, S5_N)] + _l0_out_weight_specs(),
        out_specs=[full(mrows, D_MODEL), full(nb, S5_N), full(nb, S5_N), full(nb_s, D_MODEL),
                   full(nb_s, S5_N), full(nb_s, S5_N)],
        out_shape=[f32(mrows, D_MODEL), f32(nb, S5_N), f32(nb, S5_N), f32(nb_s, D_MODEL),
                   f32(nb_s, S5_N), f32(nb_s, S5_N)],
        scratch_shapes=[pltpu.VMEM((mrows, S5_N), F32)] * 2,
        compiler_params=_params(),
        name="l0_out_side",
    )(u_side, og_meta, x_side, u_side, og_samp, x_side, xr0_s, xi0_s, *_l0_out_weights(w))


def _l0_kernel(*refs, nb, tt, c):
    x_ref, st0_ref, xr0_ref, xi0_ref = refs[0:4]
    n_in = 4 + L0_IN_WEIGHTS
    in_wts, gn_ref, out_wts = refs[4:n_in], refs[n_in], refs[n_in + 1:n_in + 1 + L0_OUT_WEIGHTS]
    rest = refs[n_in + 1 + L0_OUT_WEIGHTS:]
    xo_ref, st_ref, xre_ref, xim_ref = rest[0:4]
    proj_sc, og_sc, b_sc, sr_sc, si_sc, tm_sc = rest[4:10], rest[10], rest[11], rest[12], rest[13], rest[14]

    @pl.when(_is_first())
    def _():
        st_ref[...] = st0_ref[...]
        xre_ref[...] = xr0_ref[...]
        xim_ref[...] = xi0_ref[...]

    x = _load_time_major(x_ref, tm_sc)
    _l0_in_body(x, in_wts, proj_sc)
    q_sc, k_sc, v_sc, gs_sc, la_sc, u_sc = proj_sc
    npack = S5_GROUPS // S5_PACK
    nchunk = tt // c
    units = [(j, hp) for j in range(nchunk) for hp in range(GLA_HEADS // 2)]
    assert len(units) == npack
    for j in range(npack):
        _s5_input(u_sc, out_wts, sr_sc, si_sc, j)
    ys, prep = [], {}
    for i, (j, hp) in enumerate(units):
        _s5_scan(out_wts, xre_ref, xim_ref, sr_sc, si_sc, i, nb, tt)
        rws = pl.ds(j * nb * c, nb * c)
        ins = [r.at[rws] for r in (q_sc, k_sc, v_sc, la_sc, gs_sc)]
        if hp == 0:
            prep = _gla_chunk_prep(ins, b_sc, nb, c)
        for h in (2 * hp, 2 * hp + 1):
            _gla_chunk_head(prep, h, ins, gn_ref, og_sc.at[rws], st_ref)
        ys.append(_s5_output(out_wts, sr_sc, si_sc, i))
    _l0_tail_body(_s5_glu(ys, u_sc, out_wts), og_sc, x, out_wts, xo_ref)


def _l0(x, st0, xr0, xi0, w, tt, c):
    nb, nt, _ = x.shape
    n, rows = nb * nt, nb * tt
    assert nb & (nb - 1) == 0 and (nb * GLA_DK) % LANES == 0 and tt % c == 0
    sshape = (GLA_HEADS, GLA_DV, nb * GLA_DK)
    kern = functools.partial(_l0_kernel, nb=nb, tt=tt, c=c)
    st_spec = pl.BlockSpec((nb, S5_N), lambda i: (0, 0))
    f32 = lambda *shape: jax.ShapeDtypeStruct(shape, F32)
    return pl.pallas_call(
        kern,
        grid=(n // rows,),
        in_specs=[_seq_spec(nb, tt, D_MODEL), _const_spec(sshape), _const_spec((nb, S5_N)),
                  _const_spec((nb, S5_N))] + _l0_in_weight_specs() + [_const_spec((1, GLA_V))]
                 + _l0_out_weight_specs(),
        out_specs=[_row_spec(rows, D_MODEL), pl.BlockSpec(sshape, lambda i: (0, 0, 0)), st_spec, st_spec],
        out_shape=[f32(n, D_MODEL), f32(*sshape), f32(nb, S5_N), f32(nb, S5_N)],
        scratch_shapes=[pltpu.VMEM((rows, cols), F32) for cols in L0_IN_COLS]
                       + [pltpu.VMEM((rows, GLA_V), F32), pltpu.VMEM((nb * c, GLA_K), F32),
                          pltpu.VMEM((rows, S5_N), F32), pltpu.VMEM((rows, S5_N), F32),
                          _tm_scratch(rows, D_MODEL)],
        compiler_params=_params(),
        name="l0_mixer",
    )(x, st0, xr0, xi0, *_l0_in_weights(w), w["gla_norm"], *_l0_out_weights(w))


def _ffn_body(x, wts, c0_ref, c_ref, hm_sc, nb, tt, final):
    g_ref, wg_ref, wv_ref, cw_ref, cb_ref, wd_ref, gf_ref = wts
    rows = nb * tt
    xn = _rms(x, g_ref[...]).astype(BF16)
    for ci in range(D_FF // FF_CHUNK):
        cs = slice(ci * FF_CHUNK, (ci + 1) * FF_CHUNK)
        gate = jnp.dot(xn, wg_ref[:, cs], preferred_element_type=F32)
        val = jnp.dot(xn, wv_ref[:, cs], preferred_element_type=F32)
        if tt == 1:
            taps = [c0_ref[:, j, cs] for j in range(FFN_CONV - 1)] + [gate]
            for j in range(FFN_CONV - 1):
                c_ref[:, j, cs] = taps[j + 1]
        else:
            ext = jnp.concatenate([c_ref[:, cs], gate], axis=0)
            taps = [ext[j * nb:j * nb + rows] for j in range(FFN_CONV)]
            c_ref[:, cs] = ext[tt * nb:(tt + FFN_CONV - 1) * nb]
        y = cb_ref[:, cs] + taps[0] * cw_ref[0:1, cs]
        for j in range(1, FFN_CONV):
            y = y + taps[j] * cw_ref[j:j + 1, cs]
        hm_sc[:, cs] = (jax.nn.gelu(y) * val).astype(BF16)
    out = x + jnp.dot(hm_sc[...], wd_ref[...], preferred_element_type=F32)
    return _rms(out, gf_ref[...]) if final else out


FFN_WEIGHTS = 7


def _ffn_weight_specs(layer):
    return [_const_spec((None, 1, D_MODEL), (layer, 0, 0)),
            _const_spec((None, D_MODEL, D_FF), (layer, 0, 0)),
            _const_spec((None, D_MODEL, D_FF), (layer, 0, 1)),
            _const_spec((None, FFN_CONV, D_FF), (layer, 0, 0)),
            _const_spec((None, 1, D_FF), (layer, 0, 0)),
            _const_spec((None, D_FF, D_MODEL), (layer, 0, 0)), _const_spec((1, D_MODEL))]


def _ffn_weights(w):
    return (w["norm_ffn"], w["ffn_w_up"], w["ffn_w_up"], w["ffn_conv_w"], w["ffn_conv_b"], w["ffn_w_down"],
            w["norm_final"])


def _ffn_kernel(*refs, nb, tt, final, batch_major_out):
    x_ref, c0_ref = refs[0:2]
    wts = refs[2:2 + FFN_WEIGHTS]
    xo_ref, c_ref, hm_sc = refs[2 + FFN_WEIGHTS:5 + FFN_WEIGHTS]
    tm_sc = refs[5 + FFN_WEIGHTS] if batch_major_out else None

    @pl.when(_is_first())
    def _():
        c_ref[...] = c0_ref[...]

    _store_time_major(xo_ref, _ffn_body(x_ref[...], wts, None, c_ref, hm_sc, nb, tt, final), tm_sc)


def _ffn(x, c0, w, layer, nb, tt, final, batch_major_out=False):
    n = x.shape[0]
    rows = nb * tt
    kern = functools.partial(_ffn_kernel, nb=nb, tt=tt, final=final, batch_major_out=batch_major_out)
    if batch_major_out:
        o_spec, o_shape = _seq_spec(nb, tt, D_MODEL), (nb, n // nb, D_MODEL)
    else:
        o_spec, o_shape = _row_spec(rows, D_MODEL), (n, D_MODEL)
    cshape = _cache_shape(nb, tt, FFN_CONV - 1, D_FF)
    f32 = lambda *shape: jax.ShapeDtypeStruct(shape, F32)
    return pl.pallas_call(
        kern,
        grid=(n // rows,),
        in_specs=[_row_spec(rows, D_MODEL), _const_spec(cshape)] + _ffn_weight_specs(layer),
        out_specs=[o_spec, pl.BlockSpec(cshape, lambda i: (0, 0))],
        out_shape=[f32(*o_shape), f32(*cshape)],
        scratch_shapes=[pltpu.VMEM((rows, D_FF), BF16)]
                       + ([_tm_scratch(rows, D_MODEL)] if batch_major_out else []),
        compiler_params=_params(),
        name="ffn%d" % layer,
    )(x, c0, *_ffn_weights(w))


def _ffn_side_kernel(*refs, nb, tt_meta, nb_s, final):
    xm_ref, xs_ref, c0s_ref = refs[0:3]
    wts = refs[3:3 + FFN_WEIGHTS]
    xom_ref, c_ref, xos_ref, cs_ref, hm_sc = refs[3 + FFN_WEIGHTS:]
    c_ref[...] = jnp.zeros(c_ref.shape, F32)
    xom_ref[...] = _ffn_body(xm_ref[...], wts, None, c_ref, hm_sc, nb, tt_meta, final)
    xos_ref[...] = _ffn_body(xs_ref[...], wts, c0s_ref, cs_ref, hm_sc, nb_s, 1, final)


def _ffn_side(x_meta, x_samp, c0_s, w, layer, nb, tt_meta, final):
    mrows, nb_s = nb * tt_meta, x_samp.shape[0]
    assert mrows == nb_s
    kern = functools.partial(_ffn_side_kernel, nb=nb, tt_meta=tt_meta, nb_s=nb_s, final=final)
    cshape = _cache_shape(nb, tt_meta, FFN_CONV - 1, D_FF)
    cs_shape = _cache_shape(nb_s, 1, FFN_CONV - 1, D_FF)
    f32 = lambda *shape: jax.ShapeDtypeStruct(shape, F32)
    full = lambda shape: pl.BlockSpec(shape, lambda i: (0,) * len(shape))
    return pl.pallas_call(
        kern,
        grid=(1,),
        in_specs=[full((mrows, D_MODEL)), full((nb_s, D_MODEL)),
                  pl.BlockSpec((None,) + cs_shape, lambda i: (layer, 0, 0, 0))] + _ffn_weight_specs(layer),
        out_specs=[full((mrows, D_MODEL)), full(cshape), full((nb_s, D_MODEL)), full(cs_shape)],
        out_shape=[f32(mrows, D_MODEL), f32(*cshape), f32(nb_s, D_MODEL), f32(*cs_shape)],
        scratch_shapes=[pltpu.VMEM((mrows, D_FF), BF16)],
        compiler_params=_params(),
        name="ffn%d_side" % layer,
    )(x_meta, x_samp, c0_s, *_ffn_weights(w))


def _l1_body(x_ref, wts, xo_ref, h_ref, c0_ref, c_ref, a_sc, b_sc, nb, tt):
    g_ref, wgt_ref, wxr_ref, cw_ref, cb_ref, wa_ref, ba_ref, wx_ref, bx_ref, lam_ref, wo_ref = wts
    sp = jax.nn.softplus(-lam_ref[...])
    nsplit = L1_SPLIT if tt % L1_SPLIT == 0 else 1
    th = tt // nsplit
    rows = nb * th
    carry = c_ref[...] if tt > 1 else None
    xs, xns, ggs = [], [], []

    for part in range(nsplit):
        prow = slice(part * rows, (part + 1) * rows)
        x = x_ref[prow, :]
        xn = _rms(x, g_ref[...]).astype(BF16)
        xs.append(x)
        xns.append(xn)
        xr = jnp.dot(xn, wxr_ref[...], preferred_element_type=F32)
        if tt == 1:
            taps = [c0_ref[:, j, :] for j in range(RNN_CONV - 1)] + [xr]
            for j in range(RNN_CONV - 1):
                c_ref[:, j, :] = taps[j + 1]
        else:
            ext = jnp.concatenate([carry, xr], axis=0)
            taps = [ext[j * nb:j * nb + rows] for j in range(RNN_CONV)]
            carry = ext[th * nb:(th + RNN_CONV - 1) * nb]
        xc = cb_ref[...] + taps[0] * cw_ref[0:1, :]
        for j in range(1, RNN_CONV):
            xc = xc + taps[j] * cw_ref[j:j + 1, :]

        xcb = xc.astype(BF16)
        rs, gs = [], []
        for wi in range(RNN_W // GATE_WIN):
            for ni in range(GATE_WIN // GATE_N):
                k0 = wi * GATE_WIN + ni * LANES
                lhs = xcb[:, k0:k0 + GATE_K]
                rs.append(jnp.dot(lhs, wa_ref[wi, ni], preferred_element_type=F32))
                gs.append(jnp.dot(lhs, wx_ref[wi, ni], preferred_element_type=F32))
        r = _sigmoid(jnp.concatenate(rs, axis=-1) + ba_ref[...])
        ig = _sigmoid(jnp.concatenate(gs, axis=-1) + bx_ref[...])
        log_a = (-RNN_C) * r * sp
        a = jnp.exp(log_a)
        a_sc[prow, :] = a
        b_sc[prow, :] = _sqrt_nonneg(jnp.tanh(-log_a) * (a * a + 1.0)) * (ig * xc)
    if tt > 1:
        c_ref[...] = carry

    for part in range(nsplit):
        ggs.append(jax.nn.gelu(jnp.dot(xns[part], wgt_ref[...], preferred_element_type=F32)))

    h = h_ref[...]
    for part in range(nsplit):
        prow = slice(part * rows, (part + 1) * rows)
        for t in range(part * th, (part + 1) * th):
            rws = slice(t * nb, (t + 1) * nb)
            h = a_sc[rws, :] * h + b_sc[rws, :]
            b_sc[rws, :] = h
        xo_ref[prow, :] = xs[part] + _mm(b_sc[prow, :] * ggs[part], wo_ref[...])
    h_ref[...] = h


L1_WEIGHTS = 11


def _l1_weight_specs():
    gshape = (RNN_W // GATE_WIN, GATE_WIN // GATE_N, GATE_K, GATE_N)
    return [_const_spec((1, D_MODEL)),
            _const_spec((D_MODEL, RNN_W), (0, 0)), _const_spec((D_MODEL, RNN_W), (0, 1)),
            _const_spec((RNN_CONV, RNN_W)), _const_spec((1, RNN_W)),
            _const_spec(gshape), _const_spec((1, RNN_W)),
            _const_spec(gshape), _const_spec((1, RNN_W)),
            _const_spec((1, RNN_W)), _const_spec((RNN_W, D_MODEL))]


def _l1_weights(w):
    return (w["norm_mix_1"], w["w_in_1"], w["w_in_1"], w["rnn_conv_w"], w["rnn_conv_b"], w["rnn_wa"],
            w["rnn_b_a"], w["rnn_wx"], w["rnn_b_x"], w["rnn_lam"], w["w_out_1"])


def _l1_kernel(*refs, nb, tt):
    x_ref, h0_ref, c0_ref = refs[0:3]
    wts = refs[3:3 + L1_WEIGHTS]
    xo_ref, h_ref, c_ref, a_sc, b_sc = refs[3 + L1_WEIGHTS:]

    @pl.when(_is_first())
    def _():
        h_ref[...] = h0_ref[...]
        c_ref[...] = c0_ref[...]

    _l1_body(x_ref, wts, xo_ref, h_ref, None, c_ref, a_sc, b_sc, nb, tt)


def _l1(x, h0, c0, w, nb, tt):
    n = x.shape[0]
    rows = nb * tt
    kern = functools.partial(_l1_kernel, nb=nb, tt=tt)
    cshape = _cache_shape(nb, tt, RNN_CONV - 1, RNN_W)
    f32 = lambda *shape: jax.ShapeDtypeStruct(shape, F32)
    return pl.pallas_call(
        kern,
        grid=(n // rows,),
        in_specs=[_row_spec(rows, D_MODEL), _const_spec((nb, RNN_W)), _const_spec(cshape)] + _l1_weight_specs(),
        out_specs=[_row_spec(rows, D_MODEL), pl.BlockSpec((nb, RNN_W), lambda i: (0, 0)),
                   pl.BlockSpec(cshape, lambda i: (0, 0))],
        out_shape=[f32(n, D_MODEL), f32(nb, RNN_W), f32(*cshape)],
        scratch_shapes=[pltpu.VMEM((rows, RNN_W), F32)] * 2,
        compiler_params=_params(),
        name="l1_mixer",
    )(x, h0, c0, *_l1_weights(w))


def _l1_side_kernel(*refs, nb, tt_meta, nb_s):
    xm_ref, xs_ref, h0s_ref, c0s_ref = refs[0:4]
    wts = refs[4:4 + L1_WEIGHTS]
    xom_ref, h_ref, c_ref, xos_ref, hs_ref, cs_ref, a_sc, b_sc = refs[4 + L1_WEIGHTS:]
    h_ref[...] = jnp.zeros(h_ref.shape, F32)
    c_ref[...] = jnp.zeros(c_ref.shape, F32)
    _l1_body(xm_ref, wts, xom_ref, h_ref, None, c_ref, a_sc, b_sc, nb, tt_meta)
    hs_ref[...] = h0s_ref[...]
    _l1_body(xs_ref, wts, xos_ref, hs_ref, c0s_ref, cs_ref, a_sc, b_sc, nb_s, 1)


def _l1_side(x_meta, x_samp, h0_s, c0_s, w, nb, tt_meta):
    mrows, nb_s = nb * tt_meta, x_samp.shape[0]
    assert mrows == nb_s
    kern = functools.partial(_l1_side_kernel, nb=nb, tt_meta=tt_meta, nb_s=nb_s)
    cshape = _cache_shape(nb, tt_meta, RNN_CONV - 1, RNN_W)
    cs_shape = _cache_shape(nb_s, 1, RNN_CONV - 1, RNN_W)
    f32 = lambda *shape: jax.ShapeDtypeStruct(shape, F32)
    full = lambda shape: pl.BlockSpec(shape, lambda i: (0,) * len(shape))
    return pl.pallas_call(
        kern,
        grid=(1,),
        in_specs=[full((mrows, D_MODEL)), full((nb_s, D_MODEL)), full((nb_s, RNN_W)), full(cs_shape)]
                 + _l1_weight_specs(),
        out_specs=[full((mrows, D_MODEL)), full((nb, RNN_W)), full(cshape), full((nb_s, D_MODEL)),
                   full((nb_s, RNN_W)), full(cs_shape)],
        out_shape=[f32(mrows, D_MODEL), f32(nb, RNN_W), f32(*cshape), f32(nb_s, D_MODEL), f32(nb_s, RNN_W),
                   f32(*cs_shape)],
        scratch_shapes=[pltpu.VMEM((mrows, RNN_W), F32)] * 2,
        compiler_params=_params(),
        name="l1_side",
    )(x_meta, x_samp, h0_s, c0_s, *_l1_weights(w))


def _pack_gate_kernel(wa_ref, wx_ref, oa_ref, ox_ref):
    tiles_per_win = GATE_WIN // GATE_N
    for w_ref, o_ref in ((wa_ref, oa_ref), (wx_ref, ox_ref)):
        o_ref[...] = jnp.zeros(o_ref.shape, o_ref.dtype)
        for n in range(RNN_BLOCKS):
            pos = n * RNN_BW
            wi = pos // GATE_WIN
            for ni in range(tiles_per_win):
                k0 = wi * GATE_WIN + ni * LANES
                n0 = wi * GATE_WIN + ni * GATE_N
                lo, hi = max(pos, n0), min(pos + RNN_BW, n0 + GATE_N)
                if lo < hi:
                    o_ref[wi, ni, pos - k0:pos - k0 + RNN_BW, lo - n0:hi - n0] = (
                        w_ref[n][:, lo - pos:hi - pos].astype(o_ref.dtype))


def _pack_gates(wa, wx):
    gshape = (RNN_W // GATE_WIN, GATE_WIN // GATE_N, GATE_K, GATE_N)
    return pl.pallas_call(
        _pack_gate_kernel,
        out_shape=(jax.ShapeDtypeStruct(gshape, BF16), jax.ShapeDtypeStruct(gshape, BF16)),
        name="pack_gates",
    )(wa, wx)


def _prep_weights(p):
    w = {}
    row = lambda v: v.reshape(1, -1).astype(F32)
    w_in = p["w_in_0"]
    c = 2 * GLA_K + 2 * GLA_V
    w["w_in_0"] = w_in.astype(BF16)
    w["w_lr"] = jnp.pad(w_in[:, c:c + GLA_RANK], ((0, 0), (0, LANES - GLA_RANK))).astype(BF16)
    c += GLA_RANK
    w["w_u"] = w_in[:, c:c + S5_W].astype(BF16)
    w["w_alpha"] = jnp.pad(p["w_alpha_0"], ((0, LANES - GLA_RANK), (0, 0))).astype(BF16)
    w["b_alpha"] = row(p["b_alpha_0"])
    w["norm_mix_0"] = row(p["norm_mix_0"])
    w["gla_norm"] = row(p["gla_norm_0"])

    are, aim, bbre, bbim = _s5_prep(p["s5_lam_re"], p["s5_lam_im"], p["s5_log_dt"], p["s5_b_re"],
                                    p["s5_b_im"])
    npack = S5_GROUPS // S5_PACK
    eye = jnp.eye(S5_PACK, dtype=F32)[None, :, None, :, None]
    grouped = lambda m: m.reshape(npack, S5_PACK, S5_H, S5_P)
    pack_b = lambda m: (grouped(m)[:, :, :, None, :] * eye).reshape(
        npack, S5_PACK * S5_H, S5_PACK * S5_P).astype(BF16)
    pack_c = lambda m: (jnp.swapaxes(grouped(m), 2, 3)[:, :, :, None, :] * eye).reshape(
        npack, S5_PACK * S5_P, S5_PACK * S5_H).astype(BF16)
    w["s5_are"] = are.reshape(1, S5_N)
    w["s5_aim"] = aim.reshape(1, S5_N)
    w["s5_bbre"] = pack_b(bbre)
    w["s5_bbim"] = pack_b(bbim)
    w["s5_cre"] = pack_c(p["s5_c_re"])
    w["s5_cim"] = pack_c(p["s5_c_im"])
    w["s5_d"] = row(p["s5_d"])
    w["s5_w_glu"] = p["s5_w_glu"].astype(BF16)
    w["s5_b_glu"] = row(p["s5_b_glu"])
    w["w_out_0"] = p["w_out_0"].astype(BF16)

    w["norm_mix_1"] = row(p["norm_mix_1"])
    w["w_in_1"] = p["w_in_1"].astype(BF16)
    w["rnn_conv_w"] = p["rnn_conv_w"].astype(F32)
    w["rnn_conv_b"] = row(p["rnn_conv_b"])
    w["rnn_wa"], w["rnn_wx"] = _pack_gates(p["rnn_w_a"], p["rnn_w_x"])
    w["rnn_b_a"] = row(p["rnn_b_a"])
    w["rnn_b_x"] = row(p["rnn_b_x"])
    w["rnn_lam"] = row(p["rnn_lam"])
    w["w_out_1"] = p["w_out_1"].astype(BF16)

    depth = p["norm_ffn"].shape[0]
    w["norm_ffn"] = p["norm_ffn"].reshape(depth, 1, D_MODEL)
    w["ffn_w_up"] = p["ffn_w_up"].astype(BF16)
    w["ffn_conv_w"] = p["ffn_conv_w"]
    w["ffn_conv_b"] = p["ffn_conv_b"].reshape(depth, 1, D_FF)
    w["ffn_w_down"] = p["ffn_w_down"].astype(BF16)
    w["norm_final"] = row(p["norm_final"])
    return w


def _tile_steps():
    return dict(l0=64, gla=32, ffn=128, l1=128)


def _batch_major(cache, nb):
    jb, c = cache.shape
    return jnp.transpose(cache.reshape(jb // nb, nb, c), (1, 0, 2))


def kernel(x_prompt, x_sample, state_gla, state_s5_re, state_s5_im, state_rglru, cache_rglru_conv,
           cache_ffn_conv, meta_tokens, norm_mix_0, w_in_0, w_alpha_0, b_alpha_0, gla_norm_0,
           s5_lam_re, s5_lam_im, s5_log_dt, s5_b_re, s5_b_im, s5_c_re, s5_c_im, s5_d, s5_w_glu,
           s5_b_glu, w_out_0, norm_mix_1, w_in_1, rnn_conv_w, rnn_conv_b, rnn_w_a, rnn_b_a, rnn_w_x,
           rnn_b_x, rnn_lam, w_out_1, norm_ffn, ffn_w_up, ffn_conv_w, ffn_conv_b, ffn_w_down, norm_final):
    w = _prep_weights(dict(
        norm_mix_0=norm_mix_0, w_in_0=w_in_0, w_alpha_0=w_alpha_0, b_alpha_0=b_alpha_0,
        gla_norm_0=gla_norm_0, s5_lam_re=s5_lam_re, s5_lam_im=s5_lam_im, s5_log_dt=s5_log_dt,
        s5_b_re=s5_b_re, s5_b_im=s5_b_im, s5_c_re=s5_c_re, s5_c_im=s5_c_im, s5_d=s5_d,
        s5_w_glu=s5_w_glu, s5_b_glu=s5_b_glu, w_out_0=w_out_0, norm_mix_1=norm_mix_1, w_in_1=w_in_1,
        rnn_conv_w=rnn_conv_w, rnn_conv_b=rnn_conv_b, rnn_w_a=rnn_w_a, rnn_b_a=rnn_b_a,
        rnn_w_x=rnn_w_x, rnn_b_x=rnn_b_x, rnn_lam=rnn_lam, w_out_1=w_out_1, norm_ffn=norm_ffn,
        ffn_w_up=ffn_w_up, ffn_conv_w=ffn_conv_w, ffn_conv_b=ffn_conv_b, ffn_w_down=ffn_w_down,
        norm_final=norm_final))

    bp = x_prompt.shape[0]
    bs = x_sample.shape[0]
    tt = _tile_steps()
    mrows = bp * N_META
    assert mrows == bs

    x_side = jnp.concatenate([jnp.repeat(meta_tokens.astype(F32), bp, axis=0),
                              x_sample.reshape(bs, D_MODEL)], axis=0)
    s5_re_s = state_s5_re.reshape(bs, S5_N)
    s5_im_s = state_s5_im.reshape(bs, S5_N)

    gla_in = lambda z: (z[0], z[1], z[2], z[4], z[3])
    gn = w["gla_norm"]

    side = _l0_in(x_side, w)
    og_m, gla_m = _gla_chunk(gla_in(side), gn, jnp.zeros((GLA_HEADS, GLA_DV, bp * GLA_DK), F32), bp, N_META,
                             row_blocks=(2, 0))
    og_s, gla_s = _gla_step(gla_in(side), gn, state_gla, mrows)
    x_m, re_m, im_m, x_s, re_s, im_s = _l0_out_side(side[5], og_m, og_s, x_side, s5_re_s, s5_im_s, w, bp, N_META)
    x_m, fc0_m, x_s, fc0_s = _ffn_side(x_m, x_s, cache_ffn_conv, w, 0, bp, N_META, False)
    x_m, h_m, rc_m, x_s, h_s, rc_s = _l1_side(x_m, x_s, state_rglru, cache_rglru_conv, w, bp, N_META)
    _, fc1_m, ys, fc1_s = _ffn_side(x_m, x_s, cache_ffn_conv, w, 1, bp, N_META, True)

    x, gla_p, re_p, im_p = _l0(x_prompt, gla_m, re_m, im_m, w, tt["l0"], tt["gla"])
    x, fc0_p = _ffn(x, fc0_m, w, 0, bp, tt["ffn"], False)
    x, h_p, rc_p = _l1(x, h_m, rc_m, w, bp, tt["l1"])
    yp, fc1_p = _ffn(x, fc1_m, w, 1, bp, tt["ffn"], True, batch_major_out=True)

    grp = lambda z, nb: z.reshape(nb, S5_GROUPS, S5_P)
    return (yp, ys.reshape(bs, 1, D_MODEL), _gla_state_from_stacked(gla_p), gla_s,
            grp(re_p, bp), grp(re_s, bs), grp(im_p, bp), grp(im_s, bs), h_p, h_s,
            _batch_major(rc_p, bp), rc_s,
            jnp.stack([_batch_major(fc0_p, bp), _batch_major(fc1_p, bp)]), jnp.stack([fc0_s, fc1_s]))
```

```python
import functools

import jax
import jax.numpy as jnp
from jax import lax
from jax.experimental import pallas as pl
from jax.experimental.pallas import tpu as pltpu

F32 = jnp.float32
BF16 = jnp.bfloat16

D_MODEL = 1024
N_META = 16
EPS = 1e-6
F32_TINY = 1.1754944e-38
GLA_HEADS = 4
GLA_DK = 64
GLA_DV = 128
GLA_RANK = 16
GLA_TAU = 16.0
GLA_K = GLA_HEADS * GLA_DK
GLA_V = GLA_HEADS * GLA_DV
S5_GROUPS = 32
S5_H = 16
S5_P = 64
S5_W = S5_GROUPS * S5_H
S5_N = S5_GROUPS * S5_P
RNN_W = 1536
RNN_BLOCKS = 16
RNN_BW = RNN_W // RNN_BLOCKS
RNN_C = 8.0
RNN_CONV = 4
D_FF = 2816
FFN_CONV = 3

LANES = 128
FF_CHUNK = 256
L1_SPLIT = 2
S5_PACK = 8
GATE_WIN = 768
GATE_K = 512
GATE_N = 256
VMEM_LIMIT = 56 * 1024 * 1024


def _rms(x, g):
    return x * lax.rsqrt(jnp.mean(x * x, axis=-1, keepdims=True) + EPS) * g


def _sigmoid(x):
    return 0.5 * jnp.tanh(0.5 * x) + 0.5


def _sqrt_nonneg(t):
    return t * lax.rsqrt(jnp.maximum(t, F32_TINY))


def _mm(a, w):
    return jnp.dot(a.astype(BF16), w, preferred_element_type=F32)


def _const_spec(shape, index=None):
    idx = tuple(index) if index is not None else (0,) * len(shape)
    return pl.BlockSpec(shape, lambda i: idx, pipeline_mode=pl.Buffered(1))


def _row_spec(rows, cols):
    return pl.BlockSpec((rows, cols), lambda i: (i, 0))


def _seq_spec(nb, tt, cols):
    return pl.BlockSpec((nb, tt, cols), lambda i: (0, i, 0))


def _tm_scratch(rows, cols):
    return pltpu.VMEM((cols // LANES, rows, LANES), F32)


def _load_time_major(x_ref, tm_sc):
    nb, tt, cols = x_ref.shape
    for b in range(nb):
        for j in range(cols // LANES):
            tm_sc[j, pl.ds(b, tt, stride=nb), :] = x_ref[b, :, j * LANES:(j + 1) * LANES]
    return jnp.concatenate([tm_sc[j] for j in range(cols // LANES)], axis=-1)


def _store_time_major(o_ref, val, tm_sc):
    if tm_sc is None:
        o_ref[...] = val
        return
    nb, tt, cols = o_ref.shape
    for j in range(cols // LANES):
        tm_sc[j] = val[:, j * LANES:(j + 1) * LANES]
    for b in range(nb):
        for j in range(cols // LANES):
            o_ref[b, :, j * LANES:(j + 1) * LANES] = tm_sc[j, pl.ds(b, tt, stride=nb), :]


def _cache_shape(nb, tt, taps, width):
    return (nb, taps, width) if tt == 1 else (taps * nb, width)


def _params(sem="arbitrary"):
    return pltpu.CompilerParams(dimension_semantics=(sem,), vmem_limit_bytes=VMEM_LIMIT)


def _s5_prep_kernel(lr_ref, li_ref, ldt_ref, brt_ref, bit_ref, are_ref, aim_ref, bbre_ref, bbim_ref):
    lr = lr_ref[...]
    li = li_ref[...]
    dt = jnp.exp(ldt_ref[...])
    mag = jnp.exp(lr * dt)
    ab_re = mag * jnp.cos(li * dt)
    ab_im = mag * jnp.sin(li * dt)
    den = lr * lr + li * li
    nr = ab_re - 1.0
    ni = ab_im
    f_re = (nr * lr + ni * li) / den
    f_im = (ni * lr - nr * li) / den
    are_ref[...] = ab_re
    aim_ref[...] = ab_im
    brt = brt_ref[...]
    bit = bit_ref[...]
    bbre_ref[...] = f_re[:, None, :] * brt - f_im[:, None, :] * bit
    bbim_ref[...] = f_re[:, None, :] * bit + f_im[:, None, :] * brt


def _s5_prep(lam_re, lam_im, log_dt, b_re, b_im):
    g, p, h = b_re.shape
    brt = jnp.transpose(b_re, (0, 2, 1))
    bit = jnp.transpose(b_im, (0, 2, 1))
    return pl.pallas_call(
        _s5_prep_kernel,
        out_shape=(jax.ShapeDtypeStruct((g, p), F32), jax.ShapeDtypeStruct((g, p), F32),
                   jax.ShapeDtypeStruct((g, h, p), F32), jax.ShapeDtypeStruct((g, h, p), F32)),
        name="s5_prep",
    )(lam_re, lam_im, log_dt.reshape(g, 1), brt, bit)


def _is_first():
    return pl.program_id(0) == 0


def _side_spec(rows, cols, block=0):
    return pl.BlockSpec((rows, cols), lambda i: (block, 0))


def _l0_in_body(x, wts, outs):
    g_ref, wq_ref, wk_ref, wv_ref, wg_ref, wu_ref, wlr_ref, wal_ref, bal_ref = wts
    q_ref, k_ref, v_ref, gs_ref, la_ref, u_ref = outs
    xn = _rms(x, g_ref[...]).astype(BF16)
    u_ref[...] = _mm(xn, wu_ref[...])
    lr = _mm(xn, wlr_ref[...])
    pre = _mm(lr, wal_ref[...]) + bal_ref[...]
    la_ref[...] = jax.nn.log_sigmoid(pre) * (1.0 / GLA_TAU)
    k_ref[...] = _mm(xn, wk_ref[...])
    q_ref[...] = _mm(xn, wq_ref[...]) * (GLA_DK ** -0.5)
    v_ref[...] = _mm(xn, wv_ref[...])
    g = _mm(xn, wg_ref[...])
    gs_ref[...] = g * _sigmoid(g)


def _l0_in_kernel(*refs):
    x_ref, wts, outs = refs[0], refs[1:1 + L0_IN_WEIGHTS], refs[1 + L0_IN_WEIGHTS:]
    _l0_in_body(x_ref[...], wts, outs)


L0_IN_COLS = (GLA_K, GLA_K, GLA_V, GLA_V, GLA_K, S5_W)
L0_IN_WEIGHTS = 9


def _l0_in_weight_specs():
    return [_const_spec((1, D_MODEL)),
            _const_spec((D_MODEL, GLA_K), (0, 0)), _const_spec((D_MODEL, GLA_K), (0, 1)),
            _const_spec((D_MODEL, GLA_V), (0, 1)), _const_spec((D_MODEL, GLA_V), (0, 2)),
            _const_spec((D_MODEL, S5_W)), _const_spec((D_MODEL, LANES)),
            _const_spec((LANES, GLA_K)), _const_spec((1, GLA_K))]


def _l0_in_weights(w):
    return (w["norm_mix_0"], w["w_in_0"], w["w_in_0"], w["w_in_0"], w["w_in_0"], w["w_u"], w["w_lr"],
            w["w_alpha"], w["b_alpha"])


def _l0_in(x, w):
    n = x.shape[0]
    return pl.pallas_call(
        _l0_in_kernel,
        grid=(1,),
        in_specs=[_row_spec(n, D_MODEL)] + _l0_in_weight_specs(),
        out_specs=[_row_spec(n, c) for c in L0_IN_COLS],
        out_shape=[jax.ShapeDtypeStruct((n, c), F32) for c in L0_IN_COLS],
        compiler_params=_params(),
        name="l0_in",
    )(x, *_l0_in_weights(w))


def _gla_chunk_prep(ins, b_sc, nb, c):
    q_ref, k_ref, _, la_ref, _ = ins
    rows = nb * c
    seq_mask = nb - 1

    def cum_body(t, run):
        rws = pl.ds(pl.multiple_of(t * nb, nb), nb)
        run = run + la_ref[rws, :]
        b_sc[rws, :] = run
        return run

    bl = lax.fori_loop(0, c, cum_body, jnp.zeros((nb, GLA_K), F32), unroll=True)
    b = b_sc[...]
    q = q_ref[...]
    k = k_ref[...]
    bm = jnp.concatenate([b_sc[pl.ds((c // 2) * nb, nb), :]] * c, axis=0)
    ri = lax.broadcasted_iota(jnp.int32, (rows, rows), 0)
    ci = lax.broadcasted_iota(jnp.int32, (rows, rows), 1)
    xw = nb * GLA_DK
    return dict(
        qa=q * jnp.exp(b - bm), kt=k * jnp.exp(bm - b), qt=q * jnp.exp(b),
        kh=k * jnp.exp(jnp.concatenate([bl] * c, axis=0) - b),
        gam=jnp.exp(bl),
        pair_ok=(ri >= ci) & (((ri - ci) & seq_mask) == 0),
        own_blk=(lax.broadcasted_iota(jnp.int32, (rows, xw), 1) // GLA_DK
                 == (lax.broadcasted_iota(jnp.int32, (rows, xw), 0) & seq_mask)),
        own_blk_seq=(lax.broadcasted_iota(jnp.int32, (nb, xw), 1) // GLA_DK
                     == lax.broadcasted_iota(jnp.int32, (nb, xw), 0)),
        reps=xw // LANES)


def _gla_chunk_head(p, h, ins, gn_ref, o_ref, st_ref):
    v_ref, gs_ref = ins[2], ins[4]

    def head_dup(z):
        blk = z[:, (h // 2) * LANES:(h // 2 + 1) * LANES]
        rolled = pltpu.roll(blk, GLA_DK, axis=1)
        low = lax.broadcasted_iota(jnp.int32, blk.shape, 1) < GLA_DK
        return jnp.where(low, blk, rolled) if h % 2 == 0 else jnp.where(low, rolled, blk)

    def expand(zd, own):
        return jnp.where(own, jnp.concatenate([zd] * p["reps"], axis=1), 0.0)

    vs = slice(h * GLA_DV, (h + 1) * GLA_DV)
    qad, qd, ktd, khd = head_dup(p["qa"]), head_dup(p["qt"]), head_dup(p["kt"]), head_dup(p["kh"])
    vb = v_ref[:, vs].astype(BF16)
    st = st_ref[h]
    att = lax.dot_general(qad[:, :GLA_DK].astype(BF16), ktd[:, :GLA_DK].astype(BF16),
                          (((1,), (1,)), ((), ())), preferred_element_type=F32)
    att = jnp.where(p["pair_ok"], att, 0.0).astype(BF16)
    o = jnp.dot(att, vb, preferred_element_type=F32) + lax.dot_general(
        expand(qd, p["own_blk"]).astype(BF16), st.astype(BF16), (((1,), (1,)), ((), ())),
        preferred_element_type=F32)
    upd = lax.dot_general(vb, expand(khd, p["own_blk"]).astype(BF16), (((0,), (0,)), ((), ())),
                          preferred_element_type=F32)
    gam_row = jnp.sum(expand(head_dup(p["gam"]), p["own_blk_seq"]), axis=0, keepdims=True)
    st_ref[h] = st * gam_row + upd
    o_ref[:, vs] = _rms(o, gn_ref[:, vs]) * gs_ref[:, vs]


def _gla_chunk_body(ins, gn_ref, o_ref, st_ref, b_sc, nb, c):
    p = _gla_chunk_prep(ins, b_sc, nb, c)
    for h in range(GLA_HEADS):
        _gla_chunk_head(p, h, ins, gn_ref, o_ref, st_ref)


def _gla_chunk_kernel(*refs, nb, c):
    ins, gn_ref, st0_ref, o_ref, st_ref, b_sc = refs[0:5], refs[5], refs[6], refs[7], refs[8], refs[9]

    @pl.when(_is_first())
    def _():
        st_ref[...] = st0_ref[...]

    _gla_chunk_body(ins, gn_ref, o_ref, st_ref, b_sc, nb, c)


def _gla_chunk(ins, gn, st0, nb, c, row_blocks=None):
    assert nb & (nb - 1) == 0 and (nb * GLA_DK) % LANES == 0
    rows = nb * c
    n = ins[0].shape[0] // row_blocks[0] if row_blocks else ins[0].shape[0]
    first = (row_blocks[1] * n) // rows if row_blocks else 0
    sshape = (GLA_HEADS, GLA_DV, nb * GLA_DK)
    cols = (GLA_K, GLA_K, GLA_V, GLA_K, GLA_V)
    kern = functools.partial(_gla_chunk_kernel, nb=nb, c=c)
    return pl.pallas_call(
        kern,
        grid=(n // rows,),
        in_specs=[pl.BlockSpec((rows, w_), lambda i: (first + i, 0)) for w_ in cols]
                 + [_const_spec((1, GLA_V)), _const_spec(sshape)],
        out_specs=[_row_spec(rows, GLA_V), pl.BlockSpec(sshape, lambda i: (0, 0, 0))],
        out_shape=[jax.ShapeDtypeStruct((n, GLA_V), F32), jax.ShapeDtypeStruct(sshape, F32)],
        scratch_shapes=[pltpu.VMEM((rows, GLA_K), F32)],
        compiler_params=_params(),
        name="gla_chunk",
    )(*ins, gn, st0)


def _gla_state_from_stacked(st):
    nb = st.shape[2] // GLA_DK
    return jnp.transpose(st.reshape(GLA_HEADS, GLA_DV, nb, GLA_DK), (2, 0, 3, 1))


def _gla_step_kernel(q_ref, k_ref, la_ref, v_ref, gs_ref, gn_ref, s0_ref, o_ref, s_ref):
    qT = q_ref[...].T
    kT = k_ref[...].T
    aT = jnp.exp(la_ref[...]).T
    o_rows = []
    for b in range(q_ref.shape[0]):
        s_new = aT[:, b:b + 1] * s0_ref[b] + kT[:, b:b + 1] * v_ref[b:b + 1, :]
        s_ref[b] = s_new
        o_rows.append(jnp.sum(qT[:, b:b + 1] * s_new, axis=0, keepdims=True))
    o = jnp.concatenate(o_rows, axis=0)
    o_ref[...] = _rms(o, gn_ref[...]) * gs_ref[...]


def _gla_step(side, gn, s0, row0):
    nb = s0.shape[0]
    blk = row0 // nb
    q, k, v, la, gs = side
    heads = lambda z: jnp.transpose(z[row0:row0 + nb].reshape(nb, GLA_HEADS, GLA_DK), (1, 0, 2))
    hspec = pl.BlockSpec((None, nb, GLA_DK), lambda h: (h, 0, 0))
    vspec = pl.BlockSpec((nb, GLA_DV), lambda h: (blk, h))
    ospec = pl.BlockSpec((nb, GLA_DV), lambda h: (0, h))
    sspec = pl.BlockSpec((nb, None, GLA_DK, GLA_DV), lambda h: (0, h, 0, 0))
    return pl.pallas_call(
        _gla_step_kernel,
        grid=(GLA_HEADS,),
        in_specs=[hspec, hspec, hspec, vspec, vspec, pl.BlockSpec((1, GLA_DV), lambda h: (0, h)), sspec],
        out_specs=[ospec, sspec],
        out_shape=[jax.ShapeDtypeStruct((nb, GLA_V), F32),
                   jax.ShapeDtypeStruct((nb, GLA_HEADS, GLA_DK, GLA_DV), F32)],
        compiler_params=_params("parallel"),
        name="gla_step",
    )(heads(q), heads(k), heads(la), v, gs, gn, s0)


def _l0_out_body(u_ref, og_ref, x, wts, xo_ref, xre_ref, xim_ref, sr_sc, si_sc, nb, tt):
    npack = S5_GROUPS // S5_PACK
    for j in range(npack):
        _s5_input(u_ref, wts, sr_sc, si_sc, j)
    ys = []
    for j in range(npack):
        _s5_scan(wts, xre_ref, xim_ref, sr_sc, si_sc, j, nb, tt)
        ys.append(_s5_output(wts, sr_sc, si_sc, j))
    _l0_tail_body(_s5_glu(ys, u_ref, wts), og_ref, x, wts, xo_ref)


S5_KIN = S5_PACK * S5_H
S5_KST = S5_PACK * S5_P


def _s5_input(u_ref, wts, sr_sc, si_sc, j):
    bbre_ref, bbim_ref = wts[0:2]
    cs = slice(j * S5_KST, (j + 1) * S5_KST)
    uj = u_ref[:, j * S5_KIN:(j + 1) * S5_KIN].astype(BF16)
    sr_sc[:, cs] = jnp.dot(uj, bbre_ref[j], preferred_element_type=F32)
    si_sc[:, cs] = jnp.dot(uj, bbim_ref[j], preferred_element_type=F32)


def _s5_scan(wts, xre_ref, xim_ref, sr_sc, si_sc, j, nb, tt):
    are_ref, aim_ref = wts[2:4]
    cs = slice(j * S5_KST, (j + 1) * S5_KST)
    ar = jnp.broadcast_to(are_ref[:, cs], (nb, S5_KST))
    ai = jnp.broadcast_to(aim_ref[:, cs], (nb, S5_KST))
    xr = xre_ref[:, cs]
    xi = xim_ref[:, cs]
    for t in range(tt):
        rows = slice(t * nb, (t + 1) * nb)
        xr, xi = ar * xr - ai * xi + sr_sc[rows, cs], ar * xi + ai * xr + si_sc[rows, cs]
        sr_sc[rows, cs] = xr
        si_sc[rows, cs] = xi
    xre_ref[:, cs] = xr
    xim_ref[:, cs] = xi


def _s5_output(wts, sr_sc, si_sc, j):
    cre_ref, cim_ref = wts[4:6]
    cs = slice(j * S5_KST, (j + 1) * S5_KST)
    return (jnp.dot(sr_sc[:, cs].astype(BF16), cre_ref[j], preferred_element_type=F32)
            - jnp.dot(si_sc[:, cs].astype(BF16), cim_ref[j], preferred_element_type=F32))


def _s5_glu(ys, u_ref, wts):
    d_ref, wglu_ref, bglu_ref = wts[6:9]
    y = jax.nn.gelu(jnp.concatenate(ys, axis=-1) + d_ref[...] * u_ref[...])
    return y * _sigmoid(_mm(y, wglu_ref[...]) + bglu_ref[...])


def _l0_tail_body(y, og_ref, x, wts, xo_ref):
    woa_ref, wob_ref = wts[9:11]
    xo_ref[...] = x + _mm(og_ref[...], woa_ref[...]) + _mm(y, wob_ref[...])


L0_OUT_WEIGHTS = 11


def _l0_out_weight_specs():
    npack = S5_GROUPS // S5_PACK
    kin, kst = S5_PACK * S5_H, S5_PACK * S5_P
    return [_const_spec((npack, kin, kst)), _const_spec((npack, kin, kst)),
            _const_spec((1, S5_N)), _const_spec((1, S5_N)),
            _const_spec((npack, kst, kin)), _const_spec((npack, kst, kin)),
            _const_spec((1, S5_W)), _const_spec((S5_W, S5_W)), _const_spec((1, S5_W)),
            _const_spec((GLA_V, D_MODEL), (0, 0)), _const_spec((S5_W, D_MODEL), (1, 0))]


def _l0_out_weights(w):
    return (w["s5_bbre"], w["s5_bbim"], w["s5_are"], w["s5_aim"], w["s5_cre"], w["s5_cim"],
            w["s5_d"], w["s5_w_glu"], w["s5_b_glu"], w["w_out_0"], w["w_out_0"])


def _l0_out_side_kernel(*refs, nb, tt_meta, nb_s):
    um_ref, ogm_ref, xm_ref, us_ref, ogs_ref, xs_ref, xr0s_ref, xi0s_ref = refs[0:8]
    wts = refs[8:8 + L0_OUT_WEIGHTS]
    xom_ref, xre_ref, xim_ref, xos_ref, xres_ref, xims_ref, sr_sc, si_sc = refs[8 + L0_OUT_WEIGHTS:]
    xre_ref[...] = jnp.zeros(xre_ref.shape, F32)
    xim_ref[...] = jnp.zeros(xim_ref.shape, F32)
    _l0_out_body(um_ref, ogm_ref, xm_ref[...], wts, xom_ref, xre_ref, xim_ref, sr_sc, si_sc, nb, tt_meta)
    xres_ref[...] = xr0s_ref[...]
    xims_ref[...] = xi0s_ref[...]
    _l0_out_body(us_ref, ogs_ref, xs_ref[...], wts, xos_ref, xres_ref, xims_ref, sr_sc, si_sc, nb_s, 1)


def _l0_out_side(u_side, og_meta, og_samp, x_side, xr0_s, xi0_s, w, nb, tt_meta):
    mrows, nb_s = nb * tt_meta, xr0_s.shape[0]
    assert mrows == nb_s
    kern = functools.partial(_l0_out_side_kernel, nb=nb, tt_meta=tt_meta, nb_s=nb_s)
    f32 = lambda *shape: jax.ShapeDtypeStruct(shape, F32)
    full = lambda r, c: pl.BlockSpec((r, c), lambda i: (0, 0))
    return pl.pallas_call(
        kern,
        grid=(1,),
        in_specs=[_side_spec(mrows, S5_W, 0), full(mrows, GLA_V), _side_spec(mrows, D_MODEL, 0),
                  _side_spec(nb_s, S5_W, 1), full(nb_s, GLA_V), _side_spec(nb_s, D_MODEL, 1),
                  full(nb_s, S5_N), full(nb_s, S5_N)] + _l0_out_weight_specs(),
        out_specs=[full(mrows, D_MODEL), full(nb, S5_N), full(nb, S5_N), full(nb_s, D_MODEL),
                   full(nb_s, S5_N), full(nb_s, S5_N)],
        out_shape=[f32(mrows, D_MODEL), f32(nb, S5_N), f32(nb, S5_N), f32(nb_s, D_MODEL),
                   f32(nb_s, S5_N), f32(nb_s, S5_N)],
        scratch_shapes=[pltpu.VMEM((mrows, S5_N), F32)] * 2,
        compiler_params=_params(),
        name="l0_out_side",
    )(u_side, og_meta, x_side, u_side, og_samp, x_side, xr0_s, xi0_s, *_l0_out_weights(w))


def _l0_kernel(*refs, nb, tt, c):
    x_ref, st0_ref, xr0_ref, xi0_ref = refs[0:4]
    n_in = 4 + L0_IN_WEIGHTS
    in_wts, gn_ref, out_wts = refs[4:n_in], refs[n_in], refs[n_in + 1:n_in + 1 + L0_OUT_WEIGHTS]
    rest = refs[n_in + 1 + L0_OUT_WEIGHTS:]
    xo_ref, st_ref, xre_ref, xim_ref = rest[0:4]
    proj_sc, og_sc, b_sc, sr_sc, si_sc, tm_sc = rest[4:10], rest[10], rest[11], rest[12], rest[13], rest[14]

    @pl.when(_is_first())
    def _():
        st_ref[...] = st0_ref[...]
        xre_ref[...] = xr0_ref[...]
        xim_ref[...] = xi0_ref[...]

    x = _load_time_major(x_ref, tm_sc)
    _l0_in_body(x, in_wts, proj_sc)
    q_sc, k_sc, v_sc, gs_sc, la_sc, u_sc = proj_sc
    npack = S5_GROUPS // S5_PACK
    nchunk = tt // c
    units = [(j, hp) for j in range(nchunk) for hp in range(GLA_HEADS // 2)]
    assert len(units) == npack
    for j in range(npack):
        _s5_input(u_sc, out_wts, sr_sc, si_sc, j)
    ys, prep = [], {}
    for i, (j, hp) in enumerate(units):
        _s5_scan(out_wts, xre_ref, xim_ref, sr_sc, si_sc, i, nb, tt)
        rws = pl.ds(j * nb * c, nb * c)
        ins = [r.at[rws] for r in (q_sc, k_sc, v_sc, la_sc, gs_sc)]
        if hp == 0:
            prep = _gla_chunk_prep(ins, b_sc, nb, c)
        for h in (2 * hp, 2 * hp + 1):
            _gla_chunk_head(prep, h, ins, gn_ref, og_sc.at[rws], st_ref)
        ys.append(_s5_output(out_wts, sr_sc, si_sc, i))
    _l0_tail_body(_s5_glu(ys, u_sc, out_wts), og_sc, x, out_wts, xo_ref)


def _l0(x, st0, xr0, xi0, w, tt, c):
    nb, nt, _ = x.shape
    n, rows = nb * nt, nb * tt
    assert nb & (nb - 1) == 0 and (nb * GLA_DK) % LANES == 0 and tt % c == 0
    sshape = (GLA_HEADS, GLA_DV, nb * GLA_DK)
    kern = functools.partial(_l0_kernel, nb=nb, tt=tt, c=c)
    st_spec = pl.BlockSpec((nb, S5_N), lambda i: (0, 0))
    f32 = lambda *shape: jax.ShapeDtypeStruct(shape, F32)
    return pl.pallas_call(
        kern,
        grid=(n // rows,),
        in_specs=[_seq_spec(nb, tt, D_MODEL), _const_spec(sshape), _const_spec((nb, S5_N)),
                  _const_spec((nb, S5_N))] + _l0_in_weight_specs() + [_const_spec((1, GLA_V))]
                 + _l0_out_weight_specs(),
        out_specs=[_row_spec(rows, D_MODEL), pl.BlockSpec(sshape, lambda i: (0, 0, 0)), st_spec, st_spec],
        out_shape=[f32(n, D_MODEL), f32(*sshape), f32(nb, S5_N), f32(nb, S5_N)],
        scratch_shapes=[pltpu.VMEM((rows, cols), F32) for cols in L0_IN_COLS]
                       + [pltpu.VMEM((rows, GLA_V), F32), pltpu.VMEM((nb * c, GLA_K), F32),
                          pltpu.VMEM((rows, S5_N), F32), pltpu.VMEM((rows, S5_N), F32),
                          _tm_scratch(rows, D_MODEL)],
        compiler_params=_params(),
        name="l0_mixer",
    )(x, st0, xr0, xi0, *_l0_in_weights(w), w["gla_norm"], *_l0_out_weights(w))


def _ffn_body(x, wts, c0_ref, c_ref, hm_sc, nb, tt, final):
    g_ref, wg_ref, wv_ref, cw_ref, cb_ref, wd_ref, gf_ref = wts
    rows = nb * tt
    xn = _rms(x, g_ref[...]).astype(BF16)
    for ci in range(D_FF // FF_CHUNK):
        cs = slice(ci * FF_CHUNK, (ci + 1) * FF_CHUNK)
        gate = jnp.dot(xn, wg_ref[:, cs], preferred_element_type=F32)
        val = jnp.dot(xn, wv_ref[:, cs], preferred_element_type=F32)
        if tt == 1:
            taps = [c0_ref[:, j, cs] for j in range(FFN_CONV - 1)] + [gate]
            for j in range(FFN_CONV - 1):
                c_ref[:, j, cs] = taps[j + 1]
        else:
            ext = jnp.concatenate([c_ref[:, cs], gate], axis=0)
            taps = [ext[j * nb:j * nb + rows] for j in range(FFN_CONV)]
            c_ref[:, cs] = ext[tt * nb:(tt + FFN_CONV - 1) * nb]
        y = cb_ref[:, cs] + taps[0] * cw_ref[0:1, cs]
        for j in range(1, FFN_CONV):
            y = y + taps[j] * cw_ref[j:j + 1, cs]
        hm_sc[:, cs] = (jax.nn.gelu(y) * val).astype(BF16)
    out = x + jnp.dot(hm_sc[...], wd_ref[...], preferred_element_type=F32)
    return _rms(out, gf_ref[...]) if final else out


FFN_WEIGHTS = 7


def _ffn_weight_specs(layer):
    return [_const_spec((None, 1, D_MODEL), (layer, 0, 0)),
            _const_spec((None, D_MODEL, D_FF), (layer, 0, 0)),
            _const_spec((None, D_MODEL, D_FF), (layer, 0, 1)),
            _const_spec((None, FFN_CONV, D_FF), (layer, 0, 0)),
            _const_spec((None, 1, D_FF), (layer, 0, 0)),
            _const_spec((None, D_FF, D_MODEL), (layer, 0, 0)), _const_spec((1, D_MODEL))]


def _ffn_weights(w):
    return (w["norm_ffn"], w["ffn_w_up"], w["ffn_w_up"], w["ffn_conv_w"], w["ffn_conv_b"], w["ffn_w_down"],
            w["norm_final"])


def _ffn_kernel(*refs, nb, tt, final, batch_major_out):
    x_ref, c0_ref = refs[0:2]
    wts = refs[2:2 + FFN_WEIGHTS]
    xo_ref, c_ref, hm_sc = refs[2 + FFN_WEIGHTS:5 + FFN_WEIGHTS]
    tm_sc = refs[5 + FFN_WEIGHTS] if batch_major_out else None

    @pl.when(_is_first())
    def _():
        c_ref[...] = c0_ref[...]

    _store_time_major(xo_ref, _ffn_body(x_ref[...], wts, None, c_ref, hm_sc, nb, tt, final), tm_sc)


def _ffn(x, c0, w, layer, nb, tt, final, batch_major_out=False):
    n = x.shape[0]
    rows = nb * tt
    kern = functools.partial(_ffn_kernel, nb=nb, tt=tt, final=final, batch_major_out=batch_major_out)
    if batch_major_out:
        o_spec, o_shape = _seq_spec(nb, tt, D_MODEL), (nb, n // nb, D_MODEL)
    else:
        o_spec, o_shape = _row_spec(rows, D_MODEL), (n, D_MODEL)
    cshape = _cache_shape(nb, tt, FFN_CONV - 1, D_FF)
    f32 = lambda *shape: jax.ShapeDtypeStruct(shape, F32)
    return pl.pallas_call(
        kern,
        grid=(n // rows,),
        in_specs=[_row_spec(rows, D_MODEL), _const_spec(cshape)] + _ffn_weight_specs(layer),
        out_specs=[o_spec, pl.BlockSpec(cshape, lambda i: (0, 0))],
        out_shape=[f32(*o_shape), f32(*cshape)],
        scratch_shapes=[pltpu.VMEM((rows, D_FF), BF16)]
                       + ([_tm_scratch(rows, D_MODEL)] if batch_major_out else []),
        compiler_params=_params(),
        name="ffn%d" % layer,
    )(x, c0, *_ffn_weights(w))


def _ffn_side_kernel(*refs, nb, tt_meta, nb_s, final):
    xm_ref, xs_ref, c0s_ref = refs[0:3]
    wts = refs[3:3 + FFN_WEIGHTS]
    xom_ref, c_ref, xos_ref, cs_ref, hm_sc = refs[-5:]
    c_ref[...] = jnp.zeros(c_ref.shape, F32)
    xom_ref[...] = _ffn_body(xm_ref[...], wts, None, c_ref, hm_sc, nb, tt_meta, final)
    xos_ref[...] = _ffn_body(xs_ref[...], wts, c0s_ref, cs_ref, hm_sc, nb_s, 1, final)


def _ffn_side(x_meta, x_samp, c0_s, w, layer, nb, tt_meta, final, cache_out=None):
    mrows, nb_s = nb * tt_meta, x_samp.shape[0]
    assert mrows == nb_s
    kern = functools.partial(_ffn_side_kernel, nb=nb, tt_meta=tt_meta, nb_s=nb_s, final=final)
    cshape = _cache_shape(nb, tt_meta, FFN_CONV - 1, D_FF)
    cs_shape = _cache_shape(nb_s, 1, FFN_CONV - 1, D_FF)
    f32 = lambda *shape: jax.ShapeDtypeStruct(shape, F32)
    full = lambda shape: pl.BlockSpec(shape, lambda i: (0,) * len(shape))
    layer_blk = pl.BlockSpec((None,) + cs_shape, lambda i: (layer, 0, 0, 0))
    extra_in, extra_spec, alias = (), [], {}
    if cache_out is not None:
        extra_in, extra_spec = (cache_out,), [pl.BlockSpec(memory_space=pl.ANY)]
        alias = {3 + FFN_WEIGHTS: 3}
    return pl.pallas_call(
        kern,
        grid=(1,),
        in_specs=[full((mrows, D_MODEL)), full((nb_s, D_MODEL)), layer_blk] + _ffn_weight_specs(layer)
                 + extra_spec,
        out_specs=[full((mrows, D_MODEL)), full(cshape), full((nb_s, D_MODEL)), layer_blk],
        out_shape=[f32(mrows, D_MODEL), f32(*cshape), f32(nb_s, D_MODEL), f32(*c0_s.shape)],
        scratch_shapes=[pltpu.VMEM((mrows, D_FF), BF16)],
        input_output_aliases=alias,
        compiler_params=_params(),
        name="ffn%d_side" % layer,
    )(x_meta, x_samp, c0_s, *_ffn_weights(w), *extra_in)


def _l1_body(x_ref, wts, xo_ref, h_ref, c0_ref, c_ref, a_sc, b_sc, nb, tt):
    g_ref, wgt_ref, wxr_ref, cw_ref, cb_ref, wa_ref, ba_ref, wx_ref, bx_ref, lam_ref, wo_ref = wts
    sp = jax.nn.softplus(-lam_ref[...])
    nsplit = L1_SPLIT if tt % L1_SPLIT == 0 else 1
    th = tt // nsplit
    rows = nb * th
    carry = c_ref[...] if tt > 1 else None
    xs, xns, ggs = [], [], []

    for part in range(nsplit):
        prow = slice(part * rows, (part + 1) * rows)
        x = x_ref[prow, :]
        xn = _rms(x, g_ref[...]).astype(BF16)
        xs.append(x)
        xns.append(xn)
        xr = jnp.dot(xn, wxr_ref[...], preferred_element_type=F32)
        if tt == 1:
            taps = [c0_ref[:, j, :] for j in range(RNN_CONV - 1)] + [xr]
            for j in range(RNN_CONV - 1):
                c_ref[:, j, :] = taps[j + 1]
        else:
            ext = jnp.concatenate([carry, xr], axis=0)
            taps = [ext[j * nb:j * nb + rows] for j in range(RNN_CONV)]
            carry = ext[th * nb:(th + RNN_CONV - 1) * nb]
        xc = cb_ref[...] + taps[0] * cw_ref[0:1, :]
        for j in range(1, RNN_CONV):
            xc = xc + taps[j] * cw_ref[j:j + 1, :]

        xcb = xc.astype(BF16)
        rs, gs = [], []
        for wi in range(RNN_W // GATE_WIN):
            for ni in range(GATE_WIN // GATE_N):
                k0 = wi * GATE_WIN + ni * LANES
                lhs = xcb[:, k0:k0 + GATE_K]
                rs.append(jnp.dot(lhs, wa_ref[wi, ni], preferred_element_type=F32))
                gs.append(jnp.dot(lhs, wx_ref[wi, ni], preferred_element_type=F32))
        r = _sigmoid(jnp.concatenate(rs, axis=-1) + ba_ref[...])
        ig = _sigmoid(jnp.concatenate(gs, axis=-1) + bx_ref[...])
        log_a = (-RNN_C) * r * sp
        a = jnp.exp(log_a)
        a_sc[prow, :] = a
        b_sc[prow, :] = _sqrt_nonneg(jnp.tanh(-log_a) * (a * a + 1.0)) * (ig * xc)
    if tt > 1:
        c_ref[...] = carry

    for part in range(nsplit):
        ggs.append(jax.nn.gelu(jnp.dot(xns[part], wgt_ref[...], preferred_element_type=F32)))

    h = h_ref[...]
    for part in range(nsplit):
        prow = slice(part * rows, (part + 1) * rows)
        for t in range(part * th, (part + 1) * th):
            rws = slice(t * nb, (t + 1) * nb)
            h = a_sc[rws, :] * h + b_sc[rws, :]
            b_sc[rws, :] = h
        xo_ref[prow, :] = xs[part] + _mm(b_sc[prow, :] * ggs[part], wo_ref[...])
    h_ref[...] = h


L1_WEIGHTS = 11


def _l1_weight_specs():
    gshape = (RNN_W // GATE_WIN, GATE_WIN // GATE_N, GATE_K, GATE_N)
    return [_const_spec((1, D_MODEL)),
            _const_spec((D_MODEL, RNN_W), (0, 0)), _const_spec((D_MODEL, RNN_W), (0, 1)),
            _const_spec((RNN_CONV, RNN_W)), _const_spec((1, RNN_W)),
            _const_spec(gshape), _const_spec((1, RNN_W)),
            _const_spec(gshape), _const_spec((1, RNN_W)),
            _const_spec((1, RNN_W)), _const_spec((RNN_W, D_MODEL))]


def _l1_weights(w):
    return (w["norm_mix_1"], w["w_in_1"], w["w_in_1"], w["rnn_conv_w"], w["rnn_conv_b"], w["rnn_wa"],
            w["rnn_b_a"], w["rnn_wx"], w["rnn_b_x"], w["rnn_lam"], w["w_out_1"])


def _l1_kernel(*refs, nb, tt):
    x_ref, h0_ref, c0_ref = refs[0:3]
    wts = refs[3:3 + L1_WEIGHTS]
    xo_ref, h_ref, c_ref, a_sc, b_sc = refs[3 + L1_WEIGHTS:]

    @pl.when(_is_first())
    def _():
        h_ref[...] = h0_ref[...]
        c_ref[...] = c0_ref[...]

    _l1_body(x_ref, wts, xo_ref, h_ref, None, c_ref, a_sc, b_sc, nb, tt)


def _l1(x, h0, c0, w, nb, tt):
    n = x.shape[0]
    rows = nb * tt
    kern = functools.partial(_l1_kernel, nb=nb, tt=tt)
    cshape = _cache_shape(nb, tt, RNN_CONV - 1, RNN_W)
    f32 = lambda *shape: jax.ShapeDtypeStruct(shape, F32)
    return pl.pallas_call(
        kern,
        grid=(n // rows,),
        in_specs=[_row_spec(rows, D_MODEL), _const_spec((nb, RNN_W)), _const_spec(cshape)] + _l1_weight_specs(),
        out_specs=[_row_spec(rows, D_MODEL), pl.BlockSpec((nb, RNN_W), lambda i: (0, 0)),
                   pl.BlockSpec(cshape, lambda i: (0, 0))],
        out_shape=[f32(n, D_MODEL), f32(nb, RNN_W), f32(*cshape)],
        scratch_shapes=[pltpu.VMEM((rows, RNN_W), F32)] * 2,
        compiler_params=_params(),
        name="l1_mixer",
    )(x, h0, c0, *_l1_weights(w))


def _l1_side_kernel(*refs, nb, tt_meta, nb_s):
    xm_ref, xs_ref, h0s_ref, c0s_ref = refs[0:4]
    wts = refs[4:4 + L1_WEIGHTS]
    xom_ref, h_ref, c_ref, xos_ref, hs_ref, cs_ref, a_sc, b_sc = refs[4 + L1_WEIGHTS:]
    h_ref[...] = jnp.zeros(h_ref.shape, F32)
    c_ref[...] = jnp.zeros(c_ref.shape, F32)
    _l1_body(xm_ref, wts, xom_ref, h_ref, None, c_ref, a_sc, b_sc, nb, tt_meta)
    hs_ref[...] = h0s_ref[...]
    _l1_body(xs_ref, wts, xos_ref, hs_ref, c0s_ref, cs_ref, a_sc, b_sc, nb_s, 1)


def _l1_side(x_meta, x_samp, h0_s, c0_s, w, nb, tt_meta):
    mrows, nb_s = nb * tt_meta, x_samp.shape[0]
    assert mrows == nb_s
    kern = functools.partial(_l1_side_kernel, nb=nb, tt_meta=tt_meta, nb_s=nb_s)
    cshape = _cache_shape(nb, tt_meta, RNN_CONV - 1, RNN_W)
    cs_shape = _cache_shape(nb_s, 1, RNN_CONV - 1, RNN_W)
    f32 = lambda *shape: jax.ShapeDtypeStruct(shape, F32)
    full = lambda shape: pl.BlockSpec(shape, lambda i: (0,) * len(shape))
    return pl.pallas_call(
        kern,
        grid=(1,),
        in_specs=[full((mrows, D_MODEL)), full((nb_s, D_MODEL)), full((nb_s, RNN_W)), full(cs_shape)]
                 + _l1_weight_specs(),
        out_specs=[full((mrows, D_MODEL)), full((nb, RNN_W)), full(cshape), full((nb_s, D_MODEL)),
                   full((nb_s, RNN_W)), full(cs_shape)],
        out_shape=[f32(mrows, D_MODEL), f32(nb, RNN_W), f32(*cshape), f32(nb_s, D_MODEL), f32(nb_s, RNN_W),
                   f32(*cs_shape)],
        scratch_shapes=[pltpu.VMEM((mrows, RNN_W), F32)] * 2,
        compiler_params=_params(),
        name="l1_side",
    )(x_meta, x_samp, h0_s, c0_s, *_l1_weights(w))


def _pack_gate_kernel(wa_ref, wx_ref, oa_ref, ox_ref):
    tiles_per_win = GATE_WIN // GATE_N
    for w_ref, o_ref in ((wa_ref, oa_ref), (wx_ref, ox_ref)):
        o_ref[...] = jnp.zeros(o_ref.shape, o_ref.dtype)
        for n in range(RNN_BLOCKS):
            pos = n * RNN_BW
            wi = pos // GATE_WIN
            for ni in range(tiles_per_win):
                k0 = wi * GATE_WIN + ni * LANES
                n0 = wi * GATE_WIN + ni * GATE_N
                lo, hi = max(pos, n0), min(pos + RNN_BW, n0 + GATE_N)
                if lo < hi:
                    o_ref[wi, ni, pos - k0:pos - k0 + RNN_BW, lo - n0:hi - n0] = (
                        w_ref[n][:, lo - pos:hi - pos].astype(o_ref.dtype))


def _pack_gates(wa, wx):
    gshape = (RNN_W // GATE_WIN, GATE_WIN // GATE_N, GATE_K, GATE_N)
    return pl.pallas_call(
        _pack_gate_kernel,
        out_shape=(jax.ShapeDtypeStruct(gshape, BF16), jax.ShapeDtypeStruct(gshape, BF16)),
        name="pack_gates",
    )(wa, wx)


def _prep_weights(p):
    w = {}
    row = lambda v: v.reshape(1, -1).astype(F32)
    w_in = p["w_in_0"]
    c = 2 * GLA_K + 2 * GLA_V
    w["w_in_0"] = w_in.astype(BF16)
    w["w_lr"] = jnp.pad(w_in[:, c:c + GLA_RANK], ((0, 0), (0, LANES - GLA_RANK))).astype(BF16)
    c += GLA_RANK
    w["w_u"] = w_in[:, c:c + S5_W].astype(BF16)
    w["w_alpha"] = jnp.pad(p["w_alpha_0"], ((0, LANES - GLA_RANK), (0, 0))).astype(BF16)
    w["b_alpha"] = row(p["b_alpha_0"])
    w["norm_mix_0"] = row(p["norm_mix_0"])
    w["gla_norm"] = row(p["gla_norm_0"])

    are, aim, bbre, bbim = _s5_prep(p["s5_lam_re"], p["s5_lam_im"], p["s5_log_dt"], p["s5_b_re"],
                                    p["s5_b_im"])
    npack = S5_GROUPS // S5_PACK
    eye = jnp.eye(S5_PACK, dtype=F32)[None, :, None, :, None]
    grouped = lambda m: m.reshape(npack, S5_PACK, S5_H, S5_P)
    pack_b = lambda m: (grouped(m)[:, :, :, None, :] * eye).reshape(
        npack, S5_PACK * S5_H, S5_PACK * S5_P).astype(BF16)
    pack_c = lambda m: (jnp.swapaxes(grouped(m), 2, 3)[:, :, :, None, :] * eye).reshape(
        npack, S5_PACK * S5_P, S5_PACK * S5_H).astype(BF16)
    w["s5_are"] = are.reshape(1, S5_N)
    w["s5_aim"] = aim.reshape(1, S5_N)
    w["s5_bbre"] = pack_b(bbre)
    w["s5_bbim"] = pack_b(bbim)
    w["s5_cre"] = pack_c(p["s5_c_re"])
    w["s5_cim"] = pack_c(p["s5_c_im"])
    w["s5_d"] = row(p["s5_d"])
    w["s5_w_glu"] = p["s5_w_glu"].astype(BF16)
    w["s5_b_glu"] = row(p["s5_b_glu"])
    w["w_out_0"] = p["w_out_0"].astype(BF16)

    w["norm_mix_1"] = row(p["norm_mix_1"])
    w["w_in_1"] = p["w_in_1"].astype(BF16)
    w["rnn_conv_w"] = p["rnn_conv_w"].astype(F32)
    w["rnn_conv_b"] = row(p["rnn_conv_b"])
    w["rnn_wa"], w["rnn_wx"] = _pack_gates(p["rnn_w_a"], p["rnn_w_x"])
    w["rnn_b_a"] = row(p["rnn_b_a"])
    w["rnn_b_x"] = row(p["rnn_b_x"])
    w["rnn_lam"] = row(p["rnn_lam"])
    w["w_out_1"] = p["w_out_1"].astype(BF16)

    depth = p["norm_ffn"].shape[0]
    w["norm_ffn"] = p["norm_ffn"].reshape(depth, 1, D_MODEL)
    w["ffn_w_up"] = p["ffn_w_up"].astype(BF16)
    w["ffn_conv_w"] = p["ffn_conv_w"]
    w["ffn_conv_b"] = p["ffn_conv_b"].reshape(depth, 1, D_FF)
    w["ffn_w_down"] = p["ffn_w_down"].astype(BF16)
    w["norm_final"] = row(p["norm_final"])
    return w


def _tile_steps():
    return dict(l0=64, gla=32, ffn=128, l1=128)


def _batch_major(cache, nb):
    jb, c = cache.shape
    return jnp.transpose(cache.reshape(jb // nb, nb, c), (1, 0, 2))


def kernel(x_prompt, x_sample, state_gla, state_s5_re, state_s5_im, state_rglru, cache_rglru_conv,
           cache_ffn_conv, meta_tokens, norm_mix_0, w_in_0, w_alpha_0, b_alpha_0, gla_norm_0,
           s5_lam_re, s5_lam_im, s5_log_dt, s5_b_re, s5_b_im, s5_c_re, s5_c_im, s5_d, s5_w_glu,
           s5_b_glu, w_out_0, norm_mix_1, w_in_1, rnn_conv_w, rnn_conv_b, rnn_w_a, rnn_b_a, rnn_w_x,
           rnn_b_x, rnn_lam, w_out_1, norm_ffn, ffn_w_up, ffn_conv_w, ffn_conv_b, ffn_w_down, norm_final):
    w = _prep_weights(dict(
        norm_mix_0=norm_mix_0, w_in_0=w_in_0, w_alpha_0=w_alpha_0, b_alpha_0=b_alpha_0,
        gla_norm_0=gla_norm_0, s5_lam_re=s5_lam_re, s5_lam_im=s5_lam_im, s5_log_dt=s5_log_dt,
        s5_b_re=s5_b_re, s5_b_im=s5_b_im, s5_c_re=s5_c_re, s5_c_im=s5_c_im, s5_d=s5_d,
        s5_w_glu=s5_w_glu, s5_b_glu=s5_b_glu, w_out_0=w_out_0, norm_mix_1=norm_mix_1, w_in_1=w_in_1,
        rnn_conv_w=rnn_conv_w, rnn_conv_b=rnn_conv_b, rnn_w_a=rnn_w_a, rnn_b_a=rnn_b_a,
        rnn_w_x=rnn_w_x, rnn_b_x=rnn_b_x, rnn_lam=rnn_lam, w_out_1=w_out_1, norm_ffn=norm_ffn,
        ffn_w_up=ffn_w_up, ffn_conv_w=ffn_conv_w, ffn_conv_b=ffn_conv_b, ffn_w_down=ffn_w_down,
        norm_final=norm_final))

    bp = x_prompt.shape[0]
    bs = x_sample.shape[0]
    tt = _tile_steps()
    mrows = bp * N_META
    assert mrows == bs

    x_side = jnp.concatenate([jnp.repeat(meta_tokens.astype(F32), bp, axis=0),
                              x_sample.reshape(bs, D_MODEL)], axis=0)
    s5_re_s = state_s5_re.reshape(bs, S5_N)
    s5_im_s = state_s5_im.reshape(bs, S5_N)

    gla_in = lambda z: (z[0], z[1], z[2], z[4], z[3])
    gn = w["gla_norm"]

    side = _l0_in(x_side, w)
    og_m, gla_m = _gla_chunk(gla_in(side), gn, jnp.zeros((GLA_HEADS, GLA_DV, bp * GLA_DK), F32), bp, N_META,
                             row_blocks=(2, 0))
    og_s, gla_s = _gla_step(gla_in(side), gn, state_gla, mrows)
    x_m, re_m, im_m, x_s, re_s, im_s = _l0_out_side(side[5], og_m, og_s, x_side, s5_re_s, s5_im_s, w, bp, N_META)
    x_m, fc0_m, x_s, fc0_s = _ffn_side(x_m, x_s, cache_ffn_conv, w, 0, bp, N_META, False)
    x_m, h_m, rc_m, x_s, h_s, rc_s = _l1_side(x_m, x_s, state_rglru, cache_rglru_conv, w, bp, N_META)
    _, fc1_m, ys, fc_s = _ffn_side(x_m, x_s, cache_ffn_conv, w, 1, bp, N_META, True, cache_out=fc0_s)

    x, gla_p, re_p, im_p = _l0(x_prompt, gla_m, re_m, im_m, w, tt["l0"], tt["gla"])
    x, fc0_p = _ffn(x, fc0_m, w, 0, bp, tt["ffn"], False)
    x, h_p, rc_p = _l1(x, h_m, rc_m, w, bp, tt["l1"])
    yp, fc1_p = _ffn(x, fc1_m, w, 1, bp, tt["ffn"], True, batch_major_out=True)

    grp = lambda z, nb: z.reshape(nb, S5_GROUPS, S5_P)
    return (yp, ys.reshape(bs, 1, D_MODEL), _gla_state_from_stacked(gla_p), gla_s,
            grp(re_p, bp), grp(re_s, bs), grp(im_p, bp), grp(im_s, bs), h_p, h_s,
            _batch_major(rc_p, bp), rc_s,
            jnp.stack([_batch_major(fc0_p, bp), _batch_major(fc1_p, bp)]), fc_s)
```
